```python
import math
import jax
import jax.numpy as jnp
from jax import lax
import numpy as np

D_MODEL = 1024
BATCH = 32
SEQ = 2048
DEPTH = 1

CTX_LEN = 256
GRID_W = 64
SSD_D_INNER = 2 * D_MODEL
SSD_HEADDIM = 64
SSD_HEADS = SSD_D_INNER // SSD_HEADDIM
SSD_GROUPS = 8
SSD_HPG = SSD_HEADS // SSD_GROUPS
SSD_STATE = 128
SSD_CHUNK = 128
CONV_K = 4
CONV_LEFT = 2
LRU_WIDTH = D_MODEL
LRU_BLOCKS = 8
LRU_BLOCK_W = LRU_WIDTH // LRU_BLOCKS
LRU_C = 8.0
MLP_HIDDEN = 4 * D_MODEL
N_BRANCH = 2
N_MOD = 6
DEEPNORM_ALPHA = (2 * DEPTH) ** 0.25
DEEPNORM_BETA = (8 * DEPTH) ** -0.25
LN_EPS = 1e-6
RMS_EPS = 1e-5

SSD_BC_W = SSD_GROUPS * SSD_STATE
SSD_XB = SSD_D_INNER + SSD_BC_W
SSD_XBC = SSD_D_INNER + 2 * SSD_BC_W
SSD_DT = 2 * SSD_HEADS
O_DT = SSD_XB
O_LRU = O_DT + SSD_DT
STATE_COLS = O_LRU + LRU_WIDTH
O_C = STATE_COLS
O_Z = O_C + SSD_BC_W
O_LRU_GATE = O_Z + SSD_D_INNER
O_MERGE = O_LRU_GATE + LRU_WIDTH
IN_COLS = O_MERGE + N_BRANCH * D_MODEL

kernel_name = 'hybrid_ssd_rglru_dit_block'


def layer_norm(x, g=None, b=None):
    xf = x.astype(jnp.float32)
    mu = jnp.mean(xf, axis=-1, keepdims=True)
    var = jnp.mean(jnp.square(xf - mu), axis=-1, keepdims=True)
    y = (xf - mu) * lax.rsqrt(var + LN_EPS)
    if g is not None:
        y = y * g.astype(jnp.float32) + b.astype(jnp.float32)
    return y.astype(x.dtype)


def modulation(cvec, w_mod, b_mod, n_chunks):
    m = jax.nn.silu(cvec) @ w_mod[:, :n_chunks * D_MODEL] + b_mod[:n_chunks * D_MODEL]
    return jnp.split(m, n_chunks, axis=-1)


def modulate(x, shift, scale):
    return layer_norm(x) * (1.0 + scale) + shift


def short_conv(u, w, b, rows):
    bsz, t, ch = u.shape
    v = u if rows is None else u.reshape(bsz, rows, GRID_W, ch)
    n = v.shape[-2]
    pad = [(0, 0)] * (v.ndim - 2) + [(CONV_LEFT, CONV_K - 1 - CONV_LEFT), (0, 0)]
    vp = jnp.pad(v, pad)
    out = b
    for k in range(CONV_K):
        out = out + vp[..., k:k + n, :] * w[k]
    return out.reshape(bsz, t, ch)


def ssd_chunked(xh, dt, a_neg, bm, cm, h0):
    bsz, t = xh.shape[:2]
    nc = t // SSD_CHUNK
    xc = (xh.astype(jnp.float32) * dt[..., None]).reshape(bsz, nc, SSD_CHUNK, SSD_GROUPS, SSD_HPG, SSD_HEADDIM)
    cum = jnp.cumsum((dt * a_neg).reshape(bsz, nc, SSD_CHUNK, SSD_GROUPS, SSD_HPG), axis=2)
    bc = bm.astype(jnp.float32).reshape(bsz, nc, SSD_CHUNK, SSD_GROUPS, SSD_STATE)
    to_end = jnp.exp(cum[:, :, -1:] - cum)
    states = jnp.einsum('bcjgn,bcjgh,bcjghp->bcghpn', bc, to_end, xc)
    chunk_decay = jnp.exp(cum[:, :, -1])

    def step(h, inp):
        dec, st = inp
        return dec[..., None, None] * h + st, h

    h_fin, h_start = lax.scan(step, h0, (jnp.moveaxis(chunk_decay, 1, 0), jnp.moveaxis(states, 1, 0)))
    if cm is None:
        return None, h_fin
    h_start = jnp.moveaxis(h_start, 0, 1)
    cc = cm.astype(jnp.float32).reshape(bsz, nc, SSD_CHUNK, SSD_GROUPS, SSD_STATE)
    seg = cum[:, :, :, None] - cum[:, :, None, :]
    lower = jnp.tril(jnp.ones((SSD_CHUNK, SSD_CHUNK), dtype=bool))[:, :, None, None]
    decay = jnp.exp(jnp.where(lower, seg, -jnp.inf))
    cb = jnp.einsum('bcign,bcjgn->bcijg', cc, bc)
    y = (jnp.einsum('bcijg,bcijgh,bcjghp->bcighp', cb, decay, xc)
         + jnp.einsum('bcign,bcigh,bcghpn->bcighp', cc, jnp.exp(cum), h_start))
    return y.reshape(bsz, t, SSD_HEADS, SSD_HEADDIM), h_fin


def gated_rmsnorm(y, z, w):
    u = (y * jax.nn.silu(z)).astype(jnp.float32)
    ug = u.reshape(*u.shape[:-1], SSD_GROUPS, -1)
    ug = ug * lax.rsqrt(jnp.mean(jnp.square(ug), axis=-1, keepdims=True) + RMS_EPS)
    return (ug.reshape(u.shape) * w.astype(jnp.float32)).astype(y.dtype)


def ssd_branch(xb_raw, c_raw, dt_raw, p, h0_f, h0_b, rows):
    bsz, t, _ = xb_raw.shape
    xb = jax.nn.silu(short_conv(xb_raw, p['ssd_conv_w'][:, :SSD_XB], p['ssd_conv_b'][:SSD_XB], rows))
    xh = xb[..., :SSD_D_INNER].reshape(bsz, t, SSD_HEADS, SSD_HEADDIM)
    bm = xb[..., SSD_D_INNER:].reshape(bsz, t, SSD_GROUPS, SSD_STATE)
    cm = None
    if c_raw is not None:
        cm = jax.nn.silu(short_conv(c_raw, p['ssd_conv_w'][:, SSD_XB:], p['ssd_conv_b'][SSD_XB:], rows))
        cm = cm.reshape(bsz, t, SSD_GROUPS, SSD_STATE)
    dt = jax.nn.softplus(dt_raw.astype(jnp.float32).reshape(bsz, t, 2, SSD_HEADS) + p['ssd_dt_bias'])
    a_neg = -jnp.exp(p['ssd_a_log'].astype(jnp.float32))
    flip = lambda u: None if u is None else jnp.flip(u, axis=1)
    y_f, s_f = ssd_chunked(xh, dt[:, :, 0], a_neg[0], bm, cm, h0_f)
    y_b, s_b = ssd_chunked(flip(xh), flip(dt[:, :, 1]), a_neg[1], flip(bm), flip(cm), h0_b)
    if c_raw is None:
        return None, s_f, s_b
    y = y_f + flip(y_b) + p['ssd_d'][:, None] * xh
    return y.reshape(bsz, t, SSD_D_INNER).astype(xb_raw.dtype), s_f, s_b


def lru_combine(e1, e2):
    a1, b1 = e1
    a2, b2 = e2
    return a1 * a2, a2 * b1 + b2


def rglru(u, wa, ba, wi, bi, lam, h0, reverse):
    bsz, t, w = u.shape
    uf = u.astype(jnp.float32)
    ub = uf.reshape(bsz, t, LRU_BLOCKS, LRU_BLOCK_W)
    r = jax.nn.sigmoid(jnp.einsum('btkc,kcd->btkd', ub, wa).reshape(bsz, t, w) + ba)
    i = jax.nn.sigmoid(jnp.einsum('btkc,kcd->btkd', ub, wi).reshape(bsz, t, w) + bi)
    log_a = -LRU_C * r * jax.nn.softplus(-lam)
    a = jnp.exp(log_a)
    b_in = jnp.sqrt(-jnp.expm1(2.0 * log_a)) * (i * uf)
    edge = t - 1 if reverse else 0
    b_in = b_in.at[:, edge].add(a[:, edge] * h0)
    _, h = lax.associative_scan(lru_combine, (a, b_in), reverse=reverse, axis=1)
    return h, h[:, 0 if reverse else t - 1]


def lru_branch(u_raw, p, h0_f, h0_b, rows, need_y):
    u = short_conv(u_raw, p['lru_conv_w'], p['lru_conv_b'], rows)
    h_f, s_f = rglru(u, p['lru_wa'][0], p['lru_ba'][0], p['lru_wi'][0], p['lru_bi'][0], p['lru_lambda'][0], h0_f, False)
    h_b, s_b = rglru(u, p['lru_wa'][1], p['lru_ba'][1], p['lru_wi'][1], p['lru_bi'][1], p['lru_lambda'][1], h0_b, True)
    if not need_y:
        return None, s_f, s_b
    return (h_f + h_b).astype(u_raw.dtype), s_f, s_b


def token_mixer(h, p, init, rows, need_out):
    cols = IN_COLS if need_out else STATE_COLS
    proj = h @ p['w_in'][:, :cols]
    xb_raw = proj[..., :O_DT]
    dt_raw = proj[..., O_DT:O_LRU]
    lru_raw = proj[..., O_LRU:STATE_COLS]
    c_raw = proj[..., O_C:O_Z] if need_out else None
    y_ssd, s_f, s_b = ssd_branch(xb_raw, c_raw, dt_raw, p, init[0], init[1], rows)
    y_lru, l_f, l_b = lru_branch(lru_raw, p, init[2], init[3], rows, need_out)
    states = (s_f, s_b, l_f, l_b)
    if not need_out:
        return None, states
    z = proj[..., O_Z:O_LRU_GATE]
    lru_gate = proj[..., O_LRU_GATE:O_MERGE]
    gates = jax.nn.sigmoid(proj[..., O_MERGE:] + p['b_gate'])
    g_ssd, g_lru = jnp.split(gates, N_BRANCH, axis=-1)
    br_ssd = gated_rmsnorm(y_ssd, z, p['ssd_norm_w']) @ p['w_br_ssd']
    br_lru = (y_lru * jax.nn.gelu(lru_gate)) @ p['w_br_lru']
    return (g_ssd * br_ssd + g_lru * br_lru) @ p['w_out'], states


def sq_relu_mlp(h, p):
    return jnp.square(jax.nn.relu(h @ p['w_mlp1'] + p['b_mlp1'])) @ p['w_mlp2'] + p['b_mlp2']


def _fwd_setup_inputs(seed: int = 0) -> dict:
    key = jax.random.key(seed)
    ks = jax.random.split(key, 40)
    f32 = jnp.float32

    def nrm(k, shape, fan_in, gain=1.0):
        return jax.random.normal(k, shape, f32) * (gain * fan_in ** -0.5)

    def small(k, shape):
        return 0.01 * jax.random.normal(k, shape, f32)

    dt0 = jnp.exp(jax.random.uniform(ks[8], (DEPTH, 2, SSD_HEADS), f32, minval=math.log(1e-3), maxval=math.log(1e-1)))
    a_pow = jax.random.uniform(ks[17], (DEPTH, 2, LRU_WIDTH), f32, minval=0.9, maxval=0.999)
    a_base = a_pow ** (1.0 / LRU_C)
    return {
        'x': jax.random.normal(ks[0], (BATCH, SEQ, D_MODEL), f32),
        'c': jax.random.normal(ks[1], (BATCH, D_MODEL), f32),
        'ctx': jax.random.normal(ks[2], (BATCH, CTX_LEN, D_MODEL), f32),
        'c_ctx': jax.random.normal(ks[3], (D_MODEL,), f32),
        'w_mod': nrm(ks[4], (DEPTH, D_MODEL, N_MOD * D_MODEL), D_MODEL),
        'b_mod': small(ks[5], (DEPTH, N_MOD * D_MODEL)),
        'w_in': nrm(ks[6], (DEPTH, D_MODEL, IN_COLS), D_MODEL),
        'b_gate': small(ks[7], (DEPTH, N_BRANCH * D_MODEL)),
        'ssd_conv_w': nrm(ks[9], (DEPTH, CONV_K, SSD_XBC), CONV_K),
        'ssd_conv_b': small(ks[10], (DEPTH, SSD_XBC)),
        'ssd_dt_bias': dt0 + jnp.log(-jnp.expm1(-dt0)),
        'ssd_a_log': jnp.log(jax.random.uniform(ks[11], (DEPTH, 2, SSD_HEADS), f32, minval=1.0, maxval=16.0)),
        'ssd_d': 1.0 + small(ks[12], (DEPTH, SSD_HEADS)),
        'ssd_norm_w': 1.0 + small(ks[13], (DEPTH, SSD_D_INNER)),
        'lru_conv_w': nrm(ks[14], (DEPTH, CONV_K, LRU_WIDTH), CONV_K),
        'lru_conv_b': small(ks[15], (DEPTH, LRU_WIDTH)),
        'lru_wa': nrm(ks[16], (DEPTH, 2, LRU_BLOCKS, LRU_BLOCK_W, LRU_BLOCK_W), LRU_BLOCK_W),
        'lru_ba': small(ks[18], (DEPTH, 2, LRU_WIDTH)),
        'lru_wi': nrm(ks[19], (DEPTH, 2, LRU_BLOCKS, LRU_BLOCK_W, LRU_BLOCK_W), LRU_BLOCK_W),
        'lru_bi': small(ks[20], (DEPTH, 2, LRU_WIDTH)),
        'lru_lambda': jnp.log(a_base) - jnp.log1p(-a_base),
        'w_br_ssd': nrm(ks[21], (DEPTH, SSD_D_INNER, D_MODEL), SSD_D_INNER, DEEPNORM_BETA),
        'w_br_lru': nrm(ks[22], (DEPTH, LRU_WIDTH, D_MODEL), LRU_WIDTH, DEEPNORM_BETA),
        'w_out': nrm(ks[23], (DEPTH, D_MODEL, D_MODEL), D_MODEL, DEEPNORM_BETA),
        'ln1_g': 1.0 + small(ks[24], (DEPTH, D_MODEL)),
        'ln1_b': small(ks[25], (DEPTH, D_MODEL)),
        'w_mlp1': nrm(ks[26], (DEPTH, D_MODEL, MLP_HIDDEN), D_MODEL),
        'b_mlp1': small(ks[27], (DEPTH, MLP_HIDDEN)),
        'w_mlp2': nrm(ks[28], (DEPTH, MLP_HIDDEN, D_MODEL), MLP_HIDDEN, DEEPNORM_BETA),
        'b_mlp2': small(ks[29], (DEPTH, D_MODEL)),
        'ln2_g': 1.0 + small(ks[30], (DEPTH, D_MODEL)),
        'ln2_b': small(ks[31], (DEPTH, D_MODEL)),
    }


def _fwd_reference(x, c, ctx, c_ctx, w_mod, b_mod, w_in, b_gate, ssd_conv_w, ssd_conv_b, ssd_dt_bias,
              ssd_a_log, ssd_d, ssd_norm_w, lru_conv_w, lru_conv_b, lru_wa, lru_ba, lru_wi, lru_bi,
              lru_lambda, w_br_ssd, w_br_lru, w_out, ln1_g, ln1_b, w_mlp1, b_mlp1, w_mlp2, b_mlp2,
              ln2_g, ln2_b):
    bsz = x.shape[0]
    rows = x.shape[1] // GRID_W
    for l in range(DEPTH):
        p = dict(w_in=w_in[l], b_gate=b_gate[l], ssd_conv_w=ssd_conv_w[l], ssd_conv_b=ssd_conv_b[l],
                 ssd_dt_bias=ssd_dt_bias[l], ssd_a_log=ssd_a_log[l], ssd_d=ssd_d[l], ssd_norm_w=ssd_norm_w[l],
                 lru_conv_w=lru_conv_w[l], lru_conv_b=lru_conv_b[l], lru_wa=lru_wa[l], lru_ba=lru_ba[l],
                 lru_wi=lru_wi[l], lru_bi=lru_bi[l], lru_lambda=lru_lambda[l], w_br_ssd=w_br_ssd[l],
                 w_br_lru=w_br_lru[l], w_out=w_out[l], w_mlp1=w_mlp1[l], b_mlp1=b_mlp1[l],
                 w_mlp2=w_mlp2[l], b_mlp2=b_mlp2[l])
        last = l == DEPTH - 1
        zero_ssd = jnp.zeros((bsz, SSD_GROUPS, SSD_HPG, SSD_HEADDIM, SSD_STATE), jnp.float32)
        zero_lru = jnp.zeros((bsz, LRU_WIDTH), jnp.float32)
        mc = modulation(c_ctx, w_mod[l], b_mod[l], 2 if last else N_MOD)
        ctx_mix, ctx_states = token_mixer(modulate(ctx, mc[0], mc[1]), p,
                                          (zero_ssd, zero_ssd, zero_lru, zero_lru), None, not last)
        mx = [m[:, None, :] for m in modulation(c, w_mod[l], b_mod[l], N_MOD)]
        x_mix, _ = token_mixer(modulate(x, mx[0], mx[1]), p, ctx_states, rows, True)
        x = layer_norm(DEEPNORM_ALPHA * x + mx[2] * x_mix, ln1_g[l], ln1_b[l])
        x = layer_norm(DEEPNORM_ALPHA * x + mx[5] * sq_relu_mlp(modulate(x, mx[3], mx[4]), p), ln2_g[l], ln2_b[l])
        if not last:
            ctx = layer_norm(DEEPNORM_ALPHA * ctx + mc[2] * ctx_mix, ln1_g[l], ln1_b[l])
            ctx = layer_norm(DEEPNORM_ALPHA * ctx + mc[5] * sq_relu_mlp(modulate(ctx, mc[3], mc[4]), p),
                             ln2_g[l], ln2_b[l])
    return x


import jax as _jax
import jax.numpy as _jnp

TWIN_FORMAT = 'train_step'
FWD_PARAMS = ['x', 'c', 'ctx', 'c_ctx', 'w_mod', 'b_mod', 'w_in', 'b_gate', 'ssd_conv_w', 'ssd_conv_b', 'ssd_dt_bias', 'ssd_a_log', 'ssd_d', 'ssd_norm_w', 'lru_conv_w', 'lru_conv_b', 'lru_wa', 'lru_ba', 'lru_wi', 'lru_bi', 'lru_lambda', 'w_br_ssd', 'w_br_lru', 'w_out', 'ln1_g', 'ln1_b', 'w_mlp1', 'b_mlp1', 'w_mlp2', 'b_mlp2', 'ln2_g', 'ln2_b']
TWIN_WEIGHTS = ['c_ctx', 'w_mod', 'b_mod', 'w_in', 'b_gate', 'ssd_conv_w', 'ssd_conv_b', 'ssd_dt_bias', 'ssd_a_log', 'ssd_d', 'ssd_norm_w', 'lru_conv_w', 'lru_conv_b', 'lru_wa', 'lru_ba', 'lru_wi', 'lru_bi', 'lru_lambda', 'w_br_ssd', 'w_br_lru', 'w_out', 'ln1_g', 'ln1_b', 'w_mlp1', 'b_mlp1', 'w_mlp2', 'b_mlp2', 'ln2_g', 'ln2_b']
TWIN_DIFF_INPUT = 'x'
TWIN_INPUTS = ['x', 'c', 'ctx', 'c_ctx', 'w_mod', 'b_mod', 'w_in', 'b_gate', 'ssd_conv_w', 'ssd_conv_b', 'ssd_dt_bias', 'ssd_a_log', 'ssd_d', 'ssd_norm_w', 'lru_conv_w', 'lru_conv_b', 'lru_wa', 'lru_ba', 'lru_wi', 'lru_bi', 'lru_lambda', 'w_br_ssd', 'w_br_lru', 'w_out', 'ln1_g', 'ln1_b', 'w_mlp1', 'b_mlp1', 'w_mlp2', 'b_mlp2', 'ln2_g', 'ln2_b', 'loss_target', 'm_c_ctx', 'm_w_mod', 'm_b_mod', 'm_w_in', 'm_b_gate', 'm_ssd_conv_w', 'm_ssd_conv_b', 'm_ssd_dt_bias', 'm_ssd_a_log', 'm_ssd_d', 'm_ssd_norm_w', 'm_lru_conv_w', 'm_lru_conv_b', 'm_lru_wa', 'm_lru_ba', 'm_lru_wi', 'm_lru_bi', 'm_lru_lambda', 'm_w_br_ssd', 'm_w_br_lru', 'm_w_out', 'm_ln1_g', 'm_ln1_b', 'm_w_mlp1', 'm_b_mlp1', 'm_w_mlp2', 'm_b_mlp2', 'm_ln2_g', 'm_ln2_b', 'v_c_ctx', 'v_w_mod', 'v_b_mod', 'v_w_in', 'v_b_gate', 'v_ssd_conv_w', 'v_ssd_conv_b', 'v_ssd_dt_bias', 'v_ssd_a_log', 'v_ssd_d', 'v_ssd_norm_w', 'v_lru_conv_w', 'v_lru_conv_b', 'v_lru_wa', 'v_lru_ba', 'v_lru_wi', 'v_lru_bi', 'v_lru_lambda', 'v_w_br_ssd', 'v_w_br_lru', 'v_w_out', 'v_ln1_g', 'v_ln1_b', 'v_w_mlp1', 'v_b_mlp1', 'v_w_mlp2', 'v_b_mlp2', 'v_ln2_g', 'v_ln2_b']
TWIN_OUTPUTS = ['loss', 'grad_x', 'grad_c_ctx', 'grad_w_mod', 'grad_b_mod', 'grad_w_in', 'grad_b_gate', 'grad_ssd_conv_w', 'grad_ssd_conv_b', 'grad_ssd_dt_bias', 'grad_ssd_a_log', 'grad_ssd_d', 'grad_ssd_norm_w', 'grad_lru_conv_w', 'grad_lru_conv_b', 'grad_lru_wa', 'grad_lru_ba', 'grad_lru_wi', 'grad_lru_bi', 'grad_lru_lambda', 'grad_w_br_ssd', 'grad_w_br_lru', 'grad_w_out', 'grad_ln1_g', 'grad_ln1_b', 'grad_w_mlp1', 'grad_b_mlp1', 'grad_w_mlp2', 'grad_b_mlp2', 'grad_ln2_g', 'grad_ln2_b', 'delta_c_ctx', 'delta_w_mod', 'delta_b_mod', 'delta_w_in', 'delta_b_gate', 'delta_ssd_conv_w', 'delta_ssd_conv_b', 'delta_ssd_dt_bias', 'delta_ssd_a_log', 'delta_ssd_d', 'delta_ssd_norm_w', 'delta_lru_conv_w', 'delta_lru_conv_b', 'delta_lru_wa', 'delta_lru_ba', 'delta_lru_wi', 'delta_lru_bi', 'delta_lru_lambda', 'delta_w_br_ssd', 'delta_w_br_lru', 'delta_w_out', 'delta_ln1_g', 'delta_ln1_b', 'delta_w_mlp1', 'delta_b_mlp1', 'delta_w_mlp2', 'delta_b_mlp2', 'delta_ln2_g', 'delta_ln2_b', 'new_m_c_ctx', 'new_m_w_mod', 'new_m_b_mod', 'new_m_w_in', 'new_m_b_gate', 'new_m_ssd_conv_w', 'new_m_ssd_conv_b', 'new_m_ssd_dt_bias', 'new_m_ssd_a_log', 'new_m_ssd_d', 'new_m_ssd_norm_w', 'new_m_lru_conv_w', 'new_m_lru_conv_b', 'new_m_lru_wa', 'new_m_lru_ba', 'new_m_lru_wi', 'new_m_lru_bi', 'new_m_lru_lambda', 'new_m_w_br_ssd', 'new_m_w_br_lru', 'new_m_w_out', 'new_m_ln1_g', 'new_m_ln1_b', 'new_m_w_mlp1', 'new_m_b_mlp1', 'new_m_w_mlp2', 'new_m_b_mlp2', 'new_m_ln2_g', 'new_m_ln2_b', 'new_v_c_ctx', 'new_v_w_mod', 'new_v_b_mod', 'new_v_w_in', 'new_v_b_gate', 'new_v_ssd_conv_w', 'new_v_ssd_conv_b', 'new_v_ssd_dt_bias', 'new_v_ssd_a_log', 'new_v_ssd_d', 'new_v_ssd_norm_w', 'new_v_lru_conv_w', 'new_v_lru_conv_b', 'new_v_lru_wa', 'new_v_lru_ba', 'new_v_lru_wi', 'new_v_lru_bi', 'new_v_lru_lambda', 'new_v_w_br_ssd', 'new_v_w_br_lru', 'new_v_w_out', 'new_v_ln1_g', 'new_v_ln1_b', 'new_v_w_mlp1', 'new_v_b_mlp1', 'new_v_w_mlp2', 'new_v_b_mlp2', 'new_v_ln2_g', 'new_v_ln2_b']
TWIN_LEAF_KINDS = {'loss': 'loss', 'grad_x': 'grad_x', 'grad_c_ctx': 'grad_w', 'grad_w_mod': 'grad_w', 'grad_b_mod': 'grad_w', 'grad_w_in': 'grad_w', 'grad_b_gate': 'grad_w', 'grad_ssd_conv_w': 'grad_w', 'grad_ssd_conv_b': 'grad_w', 'grad_ssd_dt_bias': 'grad_w', 'grad_ssd_a_log': 'grad_w', 'grad_ssd_d': 'grad_w', 'grad_ssd_norm_w': 'grad_w', 'grad_lru_conv_w': 'grad_w', 'grad_lru_conv_b': 'grad_w', 'grad_lru_wa': 'grad_w', 'grad_lru_ba': 'grad_w', 'grad_lru_wi': 'grad_w', 'grad_lru_bi': 'grad_w', 'grad_lru_lambda': 'grad_w', 'grad_w_br_ssd': 'grad_w', 'grad_w_br_lru': 'grad_w', 'grad_w_out': 'grad_w', 'grad_ln1_g': 'grad_w', 'grad_ln1_b': 'grad_w', 'grad_w_mlp1': 'grad_w', 'grad_b_mlp1': 'grad_w', 'grad_w_mlp2': 'grad_w', 'grad_b_mlp2': 'grad_w', 'grad_ln2_g': 'grad_w', 'grad_ln2_b': 'grad_w', 'delta_c_ctx': 'delta_w', 'delta_w_mod': 'delta_w', 'delta_b_mod': 'delta_w', 'delta_w_in': 'delta_w', 'delta_b_gate': 'delta_w', 'delta_ssd_conv_w': 'delta_w', 'delta_ssd_conv_b': 'delta_w', 'delta_ssd_dt_bias': 'delta_w', 'delta_ssd_a_log': 'delta_w', 'delta_ssd_d': 'delta_w', 'delta_ssd_norm_w': 'delta_w', 'delta_lru_conv_w': 'delta_w', 'delta_lru_conv_b': 'delta_w', 'delta_lru_wa': 'delta_w', 'delta_lru_ba': 'delta_w', 'delta_lru_wi': 'delta_w', 'delta_lru_bi': 'delta_w', 'delta_lru_lambda': 'delta_w', 'delta_w_br_ssd': 'delta_w', 'delta_w_br_lru': 'delta_w', 'delta_w_out': 'delta_w', 'delta_ln1_g': 'delta_w', 'delta_ln1_b': 'delta_w', 'delta_w_mlp1': 'delta_w', 'delta_b_mlp1': 'delta_w', 'delta_w_mlp2': 'delta_w', 'delta_b_mlp2': 'delta_w', 'delta_ln2_g': 'delta_w', 'delta_ln2_b': 'delta_w', 'new_m_c_ctx': 'new_m', 'new_m_w_mod': 'new_m', 'new_m_b_mod': 'new_m', 'new_m_w_in': 'new_m', 'new_m_b_gate': 'new_m', 'new_m_ssd_conv_w': 'new_m', 'new_m_ssd_conv_b': 'new_m', 'new_m_ssd_dt_bias': 'new_m', 'new_m_ssd_a_log': 'new_m', 'new_m_ssd_d': 'new_m', 'new_m_ssd_norm_w': 'new_m', 'new_m_lru_conv_w': 'new_m', 'new_m_lru_conv_b': 'new_m', 'new_m_lru_wa': 'new_m', 'new_m_lru_ba': 'new_m', 'new_m_lru_wi': 'new_m', 'new_m_lru_bi': 'new_m', 'new_m_lru_lambda': 'new_m', 'new_m_w_br_ssd': 'new_m', 'new_m_w_br_lru': 'new_m', 'new_m_w_out': 'new_m', 'new_m_ln1_g': 'new_m', 'new_m_ln1_b': 'new_m', 'new_m_w_mlp1': 'new_m', 'new_m_b_mlp1': 'new_m', 'new_m_w_mlp2': 'new_m', 'new_m_b_mlp2': 'new_m', 'new_m_ln2_g': 'new_m', 'new_m_ln2_b': 'new_m', 'new_v_c_ctx': 'new_v', 'new_v_w_mod': 'new_v', 'new_v_b_mod': 'new_v', 'new_v_w_in': 'new_v', 'new_v_b_gate': 'new_v', 'new_v_ssd_conv_w': 'new_v', 'new_v_ssd_conv_b': 'new_v', 'new_v_ssd_dt_bias': 'new_v', 'new_v_ssd_a_log': 'new_v', 'new_v_ssd_d': 'new_v', 'new_v_ssd_norm_w': 'new_v', 'new_v_lru_conv_w': 'new_v', 'new_v_lru_conv_b': 'new_v', 'new_v_lru_wa': 'new_v', 'new_v_lru_ba': 'new_v', 'new_v_lru_wi': 'new_v', 'new_v_lru_bi': 'new_v', 'new_v_lru_lambda': 'new_v', 'new_v_w_br_ssd': 'new_v', 'new_v_w_br_lru': 'new_v', 'new_v_w_out': 'new_v', 'new_v_ln1_g': 'new_v', 'new_v_ln1_b': 'new_v', 'new_v_w_mlp1': 'new_v', 'new_v_b_mlp1': 'new_v', 'new_v_w_mlp2': 'new_v', 'new_v_b_mlp2': 'new_v', 'new_v_ln2_g': 'new_v', 'new_v_ln2_b': 'new_v'}


def _forward(args):
    return _fwd_reference(*[args[k] for k in FWD_PARAMS])


def _output_shape():
    out = _jax.eval_shape(lambda: _forward(_fwd_setup_inputs(0)))
    return out.shape, out.dtype

N_MICROBATCH = 1
ADAM_LR = 0.001
ADAM_B1 = 0.9
ADAM_B2 = 0.999
ADAM_EPS = 1e-08
ADAM_WD = 0.01
ADAM_STEP = 10
PER_EXAMPLE_BATCH_AXIS = {'x': 0, 'c': 0, 'ctx': 0, 'loss_target': 0}
SHARED_INPUTS = []
_WEIGHT_DTYPES = {'c_ctx': _jnp.float32, 'w_mod': _jnp.float32, 'b_mod': _jnp.float32, 'w_in': _jnp.float32, 'b_gate': _jnp.float32, 'ssd_conv_w': _jnp.float32, 'ssd_conv_b': _jnp.float32, 'ssd_dt_bias': _jnp.float32, 'ssd_a_log': _jnp.float32, 'ssd_d': _jnp.float32, 'ssd_norm_w': _jnp.float32, 'lru_conv_w': _jnp.float32, 'lru_conv_b': _jnp.float32, 'lru_wa': _jnp.float32, 'lru_ba': _jnp.float32, 'lru_wi': _jnp.float32, 'lru_bi': _jnp.float32, 'lru_lambda': _jnp.float32, 'w_br_ssd': _jnp.float32, 'w_br_lru': _jnp.float32, 'w_out': _jnp.float32, 'ln1_g': _jnp.float32, 'ln1_b': _jnp.float32, 'w_mlp1': _jnp.float32, 'b_mlp1': _jnp.float32, 'w_mlp2': _jnp.float32, 'b_mlp2': _jnp.float32, 'ln2_g': _jnp.float32, 'ln2_b': _jnp.float32}
MOMENT_SCALE = {'c_ctx': 4.969673e-02, 'w_mod': 1.469848e-01, 'b_mod': 2.402858e-01, 'w_in': 8.419595e-02, 'b_gate': 4.833740e-02, 'ssd_conv_w': 1.107104e-02, 'ssd_conv_b': 1.225428e-02, 'ssd_dt_bias': 2.629829e-02, 'ssd_a_log': 4.782520e-02, 'ssd_d': 3.277801e-02, 'ssd_norm_w': 1.574147e-02, 'lru_conv_w': 1.461362e-01, 'lru_conv_b': 2.573589e-01, 'lru_wa': 1.137303e-02, 'lru_ba': 1.371352e-02, 'lru_wi': 2.261069e-02, 'lru_bi': 2.333275e-02, 'lru_lambda': 3.443755e-02, 'w_br_ssd': 3.813216e-02, 'w_br_lru': 3.119809e-01, 'w_out': 3.115406e-01, 'ln1_g': 5.363798e-01, 'ln1_b': 3.222863e-01, 'w_mlp1': 8.390178e-02, 'b_mlp1': 6.795687e-02, 'w_mlp2': 2.517935e-01, 'b_mlp2': 1.314195e-01, 'ln2_g': 6.693872e+01, 'ln2_b': 1.378170e+01}


def _to_microbatches(a, axis):
    t = _jnp.moveaxis(a, axis, 0)
    t = t.reshape((N_MICROBATCH, t.shape[0] // N_MICROBATCH) + t.shape[1:])
    return _jnp.moveaxis(t, 1, axis + 1)


def setup_inputs(seed: int = 0) -> dict:
    inp = _fwd_setup_inputs(seed)
    key = _jax.random.fold_in(_jax.random.key(seed), 7919)
    shape, _ = _output_shape()
    out = dict(inp)
    out["loss_target"] = _jax.random.normal(_jax.random.fold_in(key, 0), shape, _jnp.float32)
    for i, name in enumerate(TWIN_WEIGHTS):
        w = inp[name].astype(_jnp.float32)
        if MOMENT_SCALE is None:
            s = _jnp.sqrt(_jnp.mean(_jnp.square(w)) + 1e-30)
        else:
            s = MOMENT_SCALE[name]
        km, kv = _jax.random.split(_jax.random.fold_in(key, i + 1))
        out[name] = w
        out["m_" + name] = s * _jax.random.normal(km, w.shape, _jnp.float32)
        out["v_" + name] = (s * s) * _jax.random.uniform(kv, w.shape, _jnp.float32, 0.5, 1.5)
    if N_MICROBATCH > 1:
        for name, axis in PER_EXAMPLE_BATCH_AXIS.items():
            out[name] = _to_microbatches(out[name], axis)
    return {'x': out['x'], 'c': out['c'], 'ctx': out['ctx'], 'c_ctx': out['c_ctx'], 'w_mod': out['w_mod'], 'b_mod': out['b_mod'], 'w_in': out['w_in'], 'b_gate': out['b_gate'], 'ssd_conv_w': out['ssd_conv_w'], 'ssd_conv_b': out['ssd_conv_b'], 'ssd_dt_bias': out['ssd_dt_bias'], 'ssd_a_log': out['ssd_a_log'], 'ssd_d': out['ssd_d'], 'ssd_norm_w': out['ssd_norm_w'], 'lru_conv_w': out['lru_conv_w'], 'lru_conv_b': out['lru_conv_b'], 'lru_wa': out['lru_wa'], 'lru_ba': out['lru_ba'], 'lru_wi': out['lru_wi'], 'lru_bi': out['lru_bi'], 'lru_lambda': out['lru_lambda'], 'w_br_ssd': out['w_br_ssd'], 'w_br_lru': out['w_br_lru'], 'w_out': out['w_out'], 'ln1_g': out['ln1_g'], 'ln1_b': out['ln1_b'], 'w_mlp1': out['w_mlp1'], 'b_mlp1': out['b_mlp1'], 'w_mlp2': out['w_mlp2'], 'b_mlp2': out['b_mlp2'], 'ln2_g': out['ln2_g'], 'ln2_b': out['ln2_b'], 'loss_target': out['loss_target'], 'm_c_ctx': out['m_c_ctx'], 'm_w_mod': out['m_w_mod'], 'm_b_mod': out['m_b_mod'], 'm_w_in': out['m_w_in'], 'm_b_gate': out['m_b_gate'], 'm_ssd_conv_w': out['m_ssd_conv_w'], 'm_ssd_conv_b': out['m_ssd_conv_b'], 'm_ssd_dt_bias': out['m_ssd_dt_bias'], 'm_ssd_a_log': out['m_ssd_a_log'], 'm_ssd_d': out['m_ssd_d'], 'm_ssd_norm_w': out['m_ssd_norm_w'], 'm_lru_conv_w': out['m_lru_conv_w'], 'm_lru_conv_b': out['m_lru_conv_b'], 'm_lru_wa': out['m_lru_wa'], 'm_lru_ba': out['m_lru_ba'], 'm_lru_wi': out['m_lru_wi'], 'm_lru_bi': out['m_lru_bi'], 'm_lru_lambda': out['m_lru_lambda'], 'm_w_br_ssd': out['m_w_br_ssd'], 'm_w_br_lru': out['m_w_br_lru'], 'm_w_out': out['m_w_out'], 'm_ln1_g': out['m_ln1_g'], 'm_ln1_b': out['m_ln1_b'], 'm_w_mlp1': out['m_w_mlp1'], 'm_b_mlp1': out['m_b_mlp1'], 'm_w_mlp2': out['m_w_mlp2'], 'm_b_mlp2': out['m_b_mlp2'], 'm_ln2_g': out['m_ln2_g'], 'm_ln2_b': out['m_ln2_b'], 'v_c_ctx': out['v_c_ctx'], 'v_w_mod': out['v_w_mod'], 'v_b_mod': out['v_b_mod'], 'v_w_in': out['v_w_in'], 'v_b_gate': out['v_b_gate'], 'v_ssd_conv_w': out['v_ssd_conv_w'], 'v_ssd_conv_b': out['v_ssd_conv_b'], 'v_ssd_dt_bias': out['v_ssd_dt_bias'], 'v_ssd_a_log': out['v_ssd_a_log'], 'v_ssd_d': out['v_ssd_d'], 'v_ssd_norm_w': out['v_ssd_norm_w'], 'v_lru_conv_w': out['v_lru_conv_w'], 'v_lru_conv_b': out['v_lru_conv_b'], 'v_lru_wa': out['v_lru_wa'], 'v_lru_ba': out['v_lru_ba'], 'v_lru_wi': out['v_lru_wi'], 'v_lru_bi': out['v_lru_bi'], 'v_lru_lambda': out['v_lru_lambda'], 'v_w_br_ssd': out['v_w_br_ssd'], 'v_w_br_lru': out['v_w_br_lru'], 'v_w_out': out['v_w_out'], 'v_ln1_g': out['v_ln1_g'], 'v_ln1_b': out['v_ln1_b'], 'v_w_mlp1': out['v_w_mlp1'], 'v_b_mlp1': out['v_b_mlp1'], 'v_w_mlp2': out['v_w_mlp2'], 'v_b_mlp2': out['v_b_mlp2'], 'v_ln2_g': out['v_ln2_g'], 'v_ln2_b': out['v_ln2_b']}


def _loss(weights, diff, rest, loss_target):
    with _jax.named_scope("forward"):
        args = {**rest, TWIN_DIFF_INPUT: diff, **{k: w.astype(_WEIGHT_DTYPES[k]) for k, w in weights.items()}}
        y = _forward(args)
    with _jax.named_scope("loss_head"):
        err = _jnp.square(y.astype(_jnp.float32) - loss_target)
        return 0.5 * _jnp.sum(_jnp.mean(err, axis=-1)) if err.ndim else 0.5 * err


def _adamw(w, g, m, v):
    m = ADAM_B1 * m + (1.0 - ADAM_B1) * g
    v = ADAM_B2 * v + (1.0 - ADAM_B2) * _jnp.square(g)
    m_hat = m / (1.0 - ADAM_B1 ** ADAM_STEP)
    v_hat = v / (1.0 - ADAM_B2 ** ADAM_STEP)
    delta = -ADAM_LR * (m_hat / (_jnp.sqrt(v_hat) + ADAM_EPS) + ADAM_WD * w)
    return delta, m, v


def reference(x, c, ctx, c_ctx, w_mod, b_mod, w_in, b_gate, ssd_conv_w, ssd_conv_b, ssd_dt_bias, ssd_a_log, ssd_d, ssd_norm_w, lru_conv_w, lru_conv_b, lru_wa, lru_ba, lru_wi, lru_bi, lru_lambda, w_br_ssd, w_br_lru, w_out, ln1_g, ln1_b, w_mlp1, b_mlp1, w_mlp2, b_mlp2, ln2_g, ln2_b, loss_target, m_c_ctx, m_w_mod, m_b_mod, m_w_in, m_b_gate, m_ssd_conv_w, m_ssd_conv_b, m_ssd_dt_bias, m_ssd_a_log, m_ssd_d, m_ssd_norm_w, m_lru_conv_w, m_lru_conv_b, m_lru_wa, m_lru_ba, m_lru_wi, m_lru_bi, m_lru_lambda, m_w_br_ssd, m_w_br_lru, m_w_out, m_ln1_g, m_ln1_b, m_w_mlp1, m_b_mlp1, m_w_mlp2, m_b_mlp2, m_ln2_g, m_ln2_b, v_c_ctx, v_w_mod, v_b_mod, v_w_in, v_b_gate, v_ssd_conv_w, v_ssd_conv_b, v_ssd_dt_bias, v_ssd_a_log, v_ssd_d, v_ssd_norm_w, v_lru_conv_w, v_lru_conv_b, v_lru_wa, v_lru_ba, v_lru_wi, v_lru_bi, v_lru_lambda, v_w_br_ssd, v_w_br_lru, v_w_out, v_ln1_g, v_ln1_b, v_w_mlp1, v_b_mlp1, v_w_mlp2, v_b_mlp2, v_ln2_g, v_ln2_b):
    given = dict(x=x, c=c, ctx=ctx, c_ctx=c_ctx, w_mod=w_mod, b_mod=b_mod, w_in=w_in, b_gate=b_gate, ssd_conv_w=ssd_conv_w, ssd_conv_b=ssd_conv_b, ssd_dt_bias=ssd_dt_bias, ssd_a_log=ssd_a_log, ssd_d=ssd_d, ssd_norm_w=ssd_norm_w, lru_conv_w=lru_conv_w, lru_conv_b=lru_conv_b, lru_wa=lru_wa, lru_ba=lru_ba, lru_wi=lru_wi, lru_bi=lru_bi, lru_lambda=lru_lambda, w_br_ssd=w_br_ssd, w_br_lru=w_br_lru, w_out=w_out, ln1_g=ln1_g, ln1_b=ln1_b, w_mlp1=w_mlp1, b_mlp1=b_mlp1, w_mlp2=w_mlp2, b_mlp2=b_mlp2, ln2_g=ln2_g, ln2_b=ln2_b, loss_target=loss_target, m_c_ctx=m_c_ctx, m_w_mod=m_w_mod, m_b_mod=m_b_mod, m_w_in=m_w_in, m_b_gate=m_b_gate, m_ssd_conv_w=m_ssd_conv_w, m_ssd_conv_b=m_ssd_conv_b, m_ssd_dt_bias=m_ssd_dt_bias, m_ssd_a_log=m_ssd_a_log, m_ssd_d=m_ssd_d, m_ssd_norm_w=m_ssd_norm_w, m_lru_conv_w=m_lru_conv_w, m_lru_conv_b=m_lru_conv_b, m_lru_wa=m_lru_wa, m_lru_ba=m_lru_ba, m_lru_wi=m_lru_wi, m_lru_bi=m_lru_bi, m_lru_lambda=m_lru_lambda, m_w_br_ssd=m_w_br_ssd, m_w_br_lru=m_w_br_lru, m_w_out=m_w_out, m_ln1_g=m_ln1_g, m_ln1_b=m_ln1_b, m_w_mlp1=m_w_mlp1, m_b_mlp1=m_b_mlp1, m_w_mlp2=m_w_mlp2, m_b_mlp2=m_b_mlp2, m_ln2_g=m_ln2_g, m_ln2_b=m_ln2_b, v_c_ctx=v_c_ctx, v_w_mod=v_w_mod, v_b_mod=v_b_mod, v_w_in=v_w_in, v_b_gate=v_b_gate, v_ssd_conv_w=v_ssd_conv_w, v_ssd_conv_b=v_ssd_conv_b, v_ssd_dt_bias=v_ssd_dt_bias, v_ssd_a_log=v_ssd_a_log, v_ssd_d=v_ssd_d, v_ssd_norm_w=v_ssd_norm_w, v_lru_conv_w=v_lru_conv_w, v_lru_conv_b=v_lru_conv_b, v_lru_wa=v_lru_wa, v_lru_ba=v_lru_ba, v_lru_wi=v_lru_wi, v_lru_bi=v_lru_bi, v_lru_lambda=v_lru_lambda, v_w_br_ssd=v_w_br_ssd, v_w_br_lru=v_w_br_lru, v_w_out=v_w_out, v_ln1_g=v_ln1_g, v_ln1_b=v_ln1_b, v_w_mlp1=v_w_mlp1, v_b_mlp1=v_b_mlp1, v_w_mlp2=v_w_mlp2, v_b_mlp2=v_b_mlp2, v_ln2_g=v_ln2_g, v_ln2_b=v_ln2_b)
    weights = {n: given[n] for n in TWIN_WEIGHTS}
    shared = {n: given[n] for n in SHARED_INPUTS}
    per_example = {n: given[n] for n in ['x', 'c', 'ctx']}
    grad_fn = _jax.value_and_grad(_loss, argnums=(0, 1))

    def one_microbatch(ex, loss_target):
        ex = dict(ex)
        diff = ex.pop(TWIN_DIFF_INPUT)
        return grad_fn(weights, diff, {**shared, **ex}, loss_target)

    if N_MICROBATCH == 1:
        loss, (grad_w, grad_x) = one_microbatch(per_example, given["loss_target"])
    else:
        def body(carry, xs):
            loss_sum, grad_sum = carry
            l_k, (gw_k, gx_k) = one_microbatch(xs[0], xs[1])
            with _jax.named_scope("update"):
                return (loss_sum + l_k, _jax.tree.map(_jnp.add, grad_sum, gw_k)), gx_k

        init = (_jnp.zeros((), _jnp.float32), _jax.tree.map(_jnp.zeros_like, weights))
        (loss, grad_w), grad_x = _jax.lax.scan(body, init, (per_example, given["loss_target"]))
    with _jax.named_scope("update"):
        delta_w, new_m, new_v = {}, {}, {}
        for n in TWIN_WEIGHTS:
            delta_w[n], new_m[n], new_v[n] = _adamw(weights[n], grad_w[n], given["m_" + n], given["v_" + n])
    return (loss, grad_x, *[grad_w[n] for n in TWIN_WEIGHTS], *[delta_w[n] for n in TWIN_WEIGHTS],
            *[new_m[n] for n in TWIN_WEIGHTS], *[new_v[n] for n in TWIN_WEIGHTS])
```

```python
import functools

import numpy as np
import jax
import jax.numpy as jnp
from jax import lax
from jax.experimental import pallas as pl
from jax.experimental.pallas import tpu as pltpu

F32, BF16 = jnp.float32, jnp.bfloat16
S = jax.ShapeDtypeStruct
BS = pl.BlockSpec
MESH = pl.DeviceIdType.MESH

D = 1024
GRID_W = 64
SSD_INNER, SSD_G, SSD_N, SSD_L = 2048, 8, 128, 128
SSD_GW = 512
MLP_H = 4096
N_MOD = 6
ALPHA = 2.0 ** 0.25
LN_EPS, RMS_EPS = 1e-6, 1e-5
LRU_C = 8.0
P_XBC, P_LRU, P_Z, P_LG, P_MG, P_DT, P_W = 0, 4096, 5120, 7168, 8192, 10240, 10368
P_CB = 3456
IN_COLS = 10304
LRU_CB = 256
ADAM_LR, ADAM_B1, ADAM_B2, ADAM_EPS, ADAM_WD, ADAM_STEP = 0.001, 0.9, 0.999, 1e-08, 0.01, 10
VMEM_LIMIT = 56 * 2 ** 20


def _params(**kw):
    return pltpu.CompilerParams(vmem_limit_bytes=VMEM_LIMIT, **kw)


def _dot(a, b):
    return jnp.dot(a.astype(BF16), b.astype(BF16), preferred_element_type=F32)


def _dot_nt(a, b):
    return lax.dot_general(a.astype(BF16), b.astype(BF16), (((1,), (1,)), ((), ())), preferred_element_type=F32)


def _dot_tn(a, b):
    return lax.dot_general(a.astype(BF16), b.astype(BF16), (((0,), (0,)), ((), ())), preferred_element_type=F32)


@jax.custom_vjp
def _mm(a, b):
    return _dot(a, b)


_mm.defvjp(lambda a, b: (_dot(a, b), (a, b)), lambda r, g: (_dot_nt(g, r[1]), _dot_tn(r[0], g)))


@jax.custom_vjp
def _mm_nt(a, b):
    return _dot_nt(a, b)


_mm_nt.defvjp(lambda a, b: (_dot_nt(a, b), (a, b)), lambda r, g: (_dot(g, r[1]), _dot_tn(g, r[0])))


@jax.custom_vjp
def _mm_tn(a, b):
    return _dot_tn(a, b)


_mm_tn.defvjp(lambda a, b: (_dot_tn(a, b), (a, b)), lambda r, g: (_dot_nt(r[1], g), _dot(r[0], g)))

_HI = lax.Precision.HIGHEST


@jax.custom_vjp
def _cum_mm(tri, v):
    return jnp.dot(tri, v, precision=_HI, preferred_element_type=F32)


_cum_mm.defvjp(
    lambda tri, v: (jnp.dot(tri, v, precision=_HI, preferred_element_type=F32), tri),
    lambda tri, g: (jnp.zeros_like(tri),
                    lax.dot_general(tri, g, (((0,), (0,)), ((), ())), precision=_HI, preferred_element_type=F32)))


@jax.custom_vjp
def _cum_mm_t(v, tri):
    return lax.dot_general(v, tri, (((0,), (1,)), ((), ())), precision=_HI, preferred_element_type=F32)


_cum_mm_t.defvjp(
    lambda v, tri: (lax.dot_general(v, tri, (((0,), (1,)), ((), ())), precision=_HI, preferred_element_type=F32), tri),
    lambda tri, g: (lax.dot_general(tri, g, (((0,), (1,)), ((), ())), precision=_HI, preferred_element_type=F32),
                    jnp.zeros_like(tri)))


def _sigmoid(x):
    return 1.0 / (1.0 + jnp.exp(-x))


def _silu(x):
    return x * _sigmoid(x)


def _dsilu(x):
    s = _sigmoid(x)
    return s * (1.0 + x * (1.0 - s))


def _softplus(x):
    return jnp.maximum(x, 0.0) + jnp.log1p(jnp.exp(-jnp.abs(x)))


def _gelu(x):
    return 0.5 * x * (1.0 + jnp.tanh(0.7978845608028654 * (x + 0.044715 * x * x * x)))


def _ln(x):
    mu = jnp.mean(x, axis=-1, keepdims=True)
    xc = x - mu
    var = jnp.mean(xc * xc, axis=-1, keepdims=True)
    return xc * lax.rsqrt(var + LN_EPS)


def _modln(x, shift, scale):
    return _ln(x) * (1.0 + scale) + shift


def _resln(x, sub, gate, g, b):
    return _ln(ALPHA * x + gate * sub) * g + b


def _grms(y, z, w):
    u = y * _silu(z)
    return u * lax.rsqrt(jnp.mean(u * u, axis=-1, keepdims=True) + RMS_EPS) * w


def _colsum(v):
    return jnp.sum(v, axis=0, keepdims=True)


def _conv_taps(width, period):
    def masks(rows):
        pos = lax.broadcasted_iota(jnp.int32, (rows, width), 0) & (period - 1)
        return [jnp.logical_and(pos + (k - 2) >= 0, pos + (k - 2) < period) for k in range(4)]
    return masks


def _conv_fwd(raw, w, b, masks):
    rows = raw.shape[0]
    pre = b + raw * w[2:3, :]
    for k in (0, 1, 3):
        sh = pltpu.roll(raw, (2 - k) % rows, 0)
        pre = pre + jnp.where(masks[k], sh, 0.0) * w[k:k + 1, :]
    return pre


def _conv_bwd(dpre, raw, w, masks):
    rows = raw.shape[0]
    draw = dpre * w[2:3, :]
    dws = []
    for k in range(4):
        if k == 2:
            dws.append(_colsum(dpre * raw))
            continue
        sh = pltpu.roll(raw, (2 - k) % rows, 0)
        dws.append(_colsum(dpre * jnp.where(masks[k], sh, 0.0)))
        back = pltpu.roll(jnp.where(masks[k], dpre, 0.0) * w[k:k + 1, :], (k - 2) % rows, 0)
        draw = draw + back
    return draw, jnp.concatenate(dws, axis=0), _colsum(dpre)


def _mod_fwd(cc, wmod, bmod):
    def body(cc_ref, w_ref, b_ref, o_ref):
        o_ref[...] = _dot(_silu(cc_ref[...]), w_ref[...]) + b_ref[...]

    return pl.pallas_call(
        body, grid=(N_MOD,), name="mod_fwd",
        in_specs=[BS((8, D), lambda j: (0, 0)), BS((D, D), lambda j: (0, j)), BS((1, D), lambda j: (0, j))],
        out_specs=BS((8, D), lambda j: (0, j)), out_shape=S((8, N_MOD * D), F32), compiler_params=_params(),
    )(cc, wmod, bmod)


def _mod_bwd(cc, wmod, dm):
    def body(cc_ref, w_ref, dm_ref, dw_ref, db_ref, dcc_ref):
        j = pl.program_id(0)
        c = cc_ref[...]
        dmv = dm_ref[...]
        dw_ref[...] = _dot_tn(_silu(c), dmv)
        db_ref[...] = _colsum(dmv)

        @pl.when(j == 0)
        def _():
            dcc_ref[...] = jnp.zeros_like(dcc_ref)

        dcc_ref[...] += _dot_nt(dmv, w_ref[...]) * _dsilu(c)

    return pl.pallas_call(
        body, grid=(N_MOD,), name="mod_bwd",
        in_specs=[BS((8, D), lambda j: (0, 0)), BS((D, D), lambda j: (0, j)), BS((8, D), lambda j: (0, j))],
        out_specs=[BS((D, D), lambda j: (0, j)), BS((1, D), lambda j: (0, j)), BS((8, D), lambda j: (0, 0))],
        out_shape=[S((D, N_MOD * D), F32), S((1, N_MOD * D), F32), S((8, D), F32)], compiler_params=_params(),
    )(cc, wmod, dm)


def _inproj_fwd(xa, m4, win, rt, n_lat_tiles, tiles_per_b, ctx_row):
    n_tiles = xa.shape[0] // rt

    def mrow(i):
        return jnp.where(i < n_lat_tiles, i // tiles_per_b, ctx_row)

    def body(x_ref, sh_ref, sc_ref, w_hbm, p_ref, h_ref, w_vm, sem):
        @pl.when(pl.program_id(0) == 0)
        def _():
            cp = pltpu.make_async_copy(w_hbm, w_vm, sem)
            cp.start()
            cp.wait()

        hb = _modln(x_ref[...], sh_ref[...], sc_ref[...]).astype(BF16)
        h_ref[...] = hb
        for j in range(P_W // P_CB):
            sl = slice(j * P_CB, (j + 1) * P_CB)
            p_ref[:, sl] = jnp.dot(hb, w_vm[:, sl], preferred_element_type=F32)

    return pl.pallas_call(
        body, grid=(n_tiles,), name="inproj_fwd",
        in_specs=[BS((rt, D), lambda i: (i, 0)),
                  BS((None, None, 1, D), lambda i: (mrow(i), 0, 0, 0)),
                  BS((None, None, 1, D), lambda i: (mrow(i), 1, 0, 0)),
                  BS(memory_space=pl.ANY)],
        out_specs=[BS((rt, P_W), lambda i: (i, 0)), BS((rt, D), lambda i: (i, 0))],
        out_shape=[S((xa.shape[0], P_W), F32), S((xa.shape[0], D), BF16)],
        scratch_shapes=[pltpu.VMEM((D, P_W), BF16), pltpu.SemaphoreType.DMA(())], compiler_params=_params(),
    )(xa, m4, m4, win)


def _inproj_bwd(xa, m4, win, dproj, rt, tile0, n_tiles, tiles_per_b, ctx_row, latent, dxres):
    def mrow(i):
        return (i // tiles_per_b) if latent else ctx_row

    def body(x_ref, sh_ref, sc_ref, dp_ref, w_hbm, *rest):
        if latent:
            dxr_ref, gx_ref, dm_ref, w_vm, sem = rest
        else:
            dm_ref, w_vm, sem = rest
        i = pl.program_id(0)

        @pl.when(i == 0)
        def _():
            cp = pltpu.make_async_copy(w_hbm, w_vm, sem)
            cp.start()
            cp.wait()

        dh = lax.dot_general(dp_ref[...], w_vm[...], (((1,), (1,)), ((), ())), preferred_element_type=F32)
        _, vjp = jax.vjp(_modln, x_ref[...], sh_ref[...], sc_ref[...])
        dx, dsh, dsc = vjp(dh)
        if latent:
            gx_ref[...] = dx + dxr_ref[...]

        @pl.when(i % tiles_per_b == 0)
        def _():
            dm_ref[...] = jnp.zeros_like(dm_ref)

        dm_ref[0:1, :] += dsh
        dm_ref[1:2, :] += dsc

    nb = n_tiles // tiles_per_b
    in_specs = [BS((rt, D), lambda i: (tile0 + i, 0)),
                BS((None, None, 1, D), lambda i: (mrow(i), 0, 0, 0)),
                BS((None, None, 1, D), lambda i: (mrow(i), 1, 0, 0)),
                BS((rt, P_W), lambda i: (tile0 + i, 0)),
                BS(memory_space=pl.ANY)]
    args = [xa, m4, m4, dproj, win]
    dm_spec = BS((None, 2, D), lambda i: (i // tiles_per_b, 0, 0))
    if latent:
        in_specs.append(BS((rt, D), lambda i: (i, 0)))
        args.append(dxres)
        out_specs = [BS((rt, D), lambda i: (i, 0)), dm_spec]
        out_shape = [S((n_tiles * rt, D), F32), S((nb, 2, D), F32)]
    else:
        out_specs = [dm_spec]
        out_shape = [S((nb, 2, D), F32)]
    return pl.pallas_call(
        body, grid=(n_tiles,), name="inproj_bwd_lat" if latent else "inproj_bwd_ctx",
        in_specs=in_specs, out_specs=out_specs, out_shape=out_shape,
        scratch_shapes=[pltpu.VMEM((D, P_W), BF16), pltpu.SemaphoreType.DMA(())],
        compiler_params=_params(),
    )(*args)


def _matmul_tn(a, b, tm, tn, tk, name):
    k, m = a.shape
    n = b.shape[1]

    def body(a_ref, b_ref, o_ref):
        @pl.when(pl.program_id(2) == 0)
        def _():
            o_ref[...] = jnp.zeros_like(o_ref)

        o_ref[...] += lax.dot_general(a_ref[...], b_ref[...], (((0,), (0,)), ((), ())), preferred_element_type=F32)

    return pl.pallas_call(
        body, grid=(m // tm, n // tn, k // tk), name=name,
        in_specs=[BS((tk, tm), lambda i, j, kk: (kk, i)), BS((tk, tn), lambda i, j, kk: (kk, j))],
        out_specs=BS((tm, tn), lambda i, j, kk: (i, j)), out_shape=S((m, n), F32), compiler_params=_params(),
    )(a, b)


def _ssd_chunk(xs, bm, cm, dtc, alog, hs, rev):
    n = SSD_L
    ii = lax.broadcasted_iota(jnp.int32, (n, n), 0)
    jj = lax.broadcasted_iota(jnp.int32, (n, n), 1)
    mask = (jj >= ii) if rev else (jj <= ii)
    tri = mask.astype(F32)
    dta = dtc * (-jnp.exp(alog))
    cum = _cum_mm(tri, dta)
    cum_t = _cum_mm_t(dta, tri)
    tot = cum[0:1, :] if rev else cum[n - 1:n, :]
    to_end = jnp.exp(tot - cum)
    from_start = jnp.exp(cum)
    cb = _mm_nt(cm, bm) if cm is not None else None
    ys, hn = [], []
    for h in range(4):
        k = 4 * rev + h
        xd = xs[h] * dtc[:, k:k + 1]
        if cm is not None:
            seg = cum[:, k:k + 1] - cum_t[k:k + 1, :]
            decay = jnp.exp(jnp.where(mask, seg, -1e30))
            ys.append(_mm(cb * decay, xd) + from_start[:, k:k + 1] * _mm_nt(cm, hs[h]))
        hn.append(jnp.exp(tot[:, k:k + 1]) * hs[h] + _mm_tn(xd * to_end[:, k:k + 1], bm))
    return tuple(ys), tuple(hn)


def _ssd_fwd(proj, dtg, cw, cb, dtb, alog, drow, h0f, h0b, nb, t, period, blk0, need_y):
    nc = t // SSD_L
    masks_of = _conv_taps(SSD_GW, period)

    def body(p_ref, dt_ref, cw_ref, cb_ref, dtb_ref, al_ref, d_ref, h0f_ref, h0b_ref, *rest):
        if need_y:
            y_ref, hsf_ref, hsb_ref, sf_ref, sb_ref, act, dts = rest
        else:
            hsf_ref, hsb_ref, sf_ref, sb_ref, act, dts = rest
        act[...] = _silu(_conv_fwd(p_ref[...], cw_ref[...], cb_ref[...], masks_of(t)))
        dts[...] = _softplus(dt_ref[...] + dtb_ref[...])
        al = al_ref[...]
        for d in (0, 1):
            hs_ref = hsb_ref if d else hsf_ref
            h0_ref = h0b_ref if d else h0f_ref
            s_ref = sb_ref if d else sf_ref

            def chunk(ci, hc, d=d, hs_ref=hs_ref):
                c = (nc - 1 - ci) if d else ci
                r0 = pl.multiple_of(c * SSD_L, SSD_L)
                a = act[pl.ds(r0, SSD_L), :]
                xs = tuple(a[:, 64 * h:64 * h + 64] for h in range(4))
                for h in range(4):
                    hs_ref[c, 64 * h:64 * h + 64, :] = hc[h]
                ys, hn = _ssd_chunk(xs, a[:, 256:384], a[:, 384:512] if need_y else None,
                                    dts[pl.ds(r0, SSD_L), :], al, hc, d)
                if need_y:
                    for h in range(4):
                        sl = slice(64 * h, 64 * h + 64)
                        if d == 0:
                            y_ref[pl.ds(r0, SSD_L), sl] = ys[h] + d_ref[:, sl] * xs[h]
                        else:
                            y_ref[pl.ds(r0, SSD_L), sl] += ys[h]
                return hn

            hfin = lax.fori_loop(0, nc, chunk, tuple(h0_ref[64 * h:64 * h + 64, :] for h in range(4)))
            for h in range(4):
                s_ref[64 * h:64 * h + 64, :] = hfin[h]

    gspec = lambda shp: BS((None,) + shp, lambda b, g: (g,) + (0,) * len(shp))
    st_spec = BS((None, None, 256, SSD_N), lambda b, g: (b, g, 0, 0))
    hs_spec = BS((None, None, nc, 256, SSD_N), lambda b, g: (b, g, 0, 0, 0))
    in_specs = [BS((t, SSD_GW), lambda b, g: (blk0 + b, g)), BS((None, t, 128), lambda b, g: (g, blk0 + b, 0)),
                gspec((4, SSD_GW)), gspec((1, SSD_GW)), gspec((1, 128)), gspec((1, 128)), gspec((1, 256)),
                st_spec, st_spec]
    out_specs = [hs_spec, hs_spec, st_spec, st_spec]
    out_shape = [S((nb, SSD_G, nc, 256, SSD_N), F32)] * 2 + [S((nb, SSD_G, 256, SSD_N), F32)] * 2
    if need_y:
        out_specs = [BS((t, 256), lambda b, g: (b, g))] + out_specs
        out_shape = [S((nb * t, SSD_INNER), F32)] + out_shape
    return pl.pallas_call(
        body, grid=(nb, SSD_G), name="ssd_fwd_lat" if need_y else "ssd_fwd_ctx",
        in_specs=in_specs, out_specs=out_specs, out_shape=out_shape,
        scratch_shapes=[pltpu.VMEM((t, SSD_GW), F32), pltpu.VMEM((t, 128), F32)],
        compiler_params=_params(),
    )(proj, dtg, cw, cb, dtb, alog, drow, h0f, h0b)


def _ssd_bwd(proj, dtg, cw, cb, dtb, alog, drow, hsf, hsb, dy, dsf, dsb, dproj, nb, t, period, blk0, need_y):
    nc = t // SSD_L
    masks_of = _conv_taps(SSD_GW, period)

    def body(*refs):
        if need_y:
            (p_ref, dt_ref, cw_ref, cb_ref, dtb_ref, al_ref, d_ref, hsf_ref, hsb_ref, dy_ref, dsf_ref, dsb_ref, _,
             dp_ref, ddt_ref, dhf_ref, dhb_ref, dcw_ref, dcb_ref, ddtb_ref, dal_ref, dd_ref, pre, dact, dts, ddts) = refs
        else:
            (p_ref, dt_ref, cw_ref, cb_ref, dtb_ref, al_ref, d_ref, hsf_ref, hsb_ref, dsf_ref, dsb_ref, _,
             dp_ref, ddt_ref, dhf_ref, dhb_ref, dcw_ref, dcb_ref, ddtb_ref, dal_ref, dd_ref, pre, dact, dts, ddts) = refs
            dy_ref = None
        b, g = pl.program_id(0), pl.program_id(1)

        @pl.when(jnp.logical_and(b == 0, g == 0))
        def _():
            for r in (dcw_ref, dcb_ref, ddtb_ref, dal_ref, dd_ref):
                r[...] = jnp.zeros_like(r)

        masks = masks_of(t)
        pre[...] = _conv_fwd(p_ref[...], cw_ref[...], cb_ref[...], masks)
        dtpre = dt_ref[...] + dtb_ref[...]
        dts[...] = _softplus(dtpre)
        dact[...] = jnp.zeros_like(dact)
        ddts[...] = jnp.zeros_like(ddts)
        al = al_ref[...]
        dal_acc = jnp.zeros((1, 128), F32)
        for d in (0, 1):
            hs_ref = hsb_ref if d else hsf_ref
            ds_ref = dsb_ref if d else dsf_ref
            dh_ref = dhb_ref if d else dhf_ref

            def chunk(ci, carry, d=d, hs_ref=hs_ref):
                dh, dal_c = carry
                c = ci if d else (nc - 1 - ci)
                r0 = pl.multiple_of(c * SSD_L, SSD_L)
                a = _silu(pre[pl.ds(r0, SSD_L), :])
                xs = tuple(a[:, 64 * h:64 * h + 64] for h in range(4))
                hs = tuple(hs_ref[c, 64 * h:64 * h + 64, :] for h in range(4))
                dtc = dts[pl.ds(r0, SSD_L), :]
                if need_y:
                    fn = lambda xs_, bm_, cm_, dt_, al_, hs_: _ssd_chunk(xs_, bm_, cm_, dt_, al_, hs_, d)
                    _, vjp = jax.vjp(fn, xs, a[:, 256:384], a[:, 384:512], dtc, al, hs)
                    dys = tuple(dy_ref[pl.ds(r0, SSD_L), 64 * h:64 * h + 64] for h in range(4))
                    dxs, dbm, dcm, ddtc, dal_k, dhs = vjp((dys, dh))
                    dact[pl.ds(r0, SSD_L), 384:512] += dcm
                else:
                    fn = lambda xs_, bm_, dt_, al_, hs_: _ssd_chunk(xs_, bm_, None, dt_, al_, hs_, d)[1]
                    _, vjp = jax.vjp(fn, xs, a[:, 256:384], dtc, al, hs)
                    dxs, dbm, ddtc, dal_k, dhs = vjp(dh)
                for h in range(4):
                    dact[pl.ds(r0, SSD_L), 64 * h:64 * h + 64] += dxs[h]
                dact[pl.ds(r0, SSD_L), 256:384] += dbm
                ddts[pl.ds(r0, SSD_L), :] += ddtc
                return dhs, dal_c + dal_k

            dh0 = tuple(ds_ref[64 * h:64 * h + 64, :] for h in range(4))
            dhfin, dal_acc = lax.fori_loop(0, nc, chunk, (dh0, dal_acc))
            for h in range(4):
                dh_ref[64 * h:64 * h + 64, :] = dhfin[h]

        prev = pre[...]
        if need_y:
            dyv = dy_ref[...]
            dact[:, 0:256] += d_ref[...] * dyv
            dd_ref[g] += _colsum(dyv * _silu(prev[:, 0:256]))
        dpre = dact[...] * _dsilu(prev)
        draw, dcw, dcb = _conv_bwd(dpre, p_ref[...], cw_ref[...], masks)
        dp_ref[...] = draw.astype(BF16)
        dcw_ref[g] += dcw
        dcb_ref[g] += dcb
        ddraw = ddts[...] * _sigmoid(dtpre)
        ddt_ref[...] = ddraw
        ddtb_ref[g] += _colsum(ddraw)
        dal_ref[g] += dal_acc

    gspec = lambda shp: BS((None,) + shp, lambda b, g: (g,) + (0,) * len(shp))
    full = lambda shp: BS(shp, lambda b, g: (0,) * len(shp))
    st_spec = BS((None, None, 256, SSD_N), lambda b, g: (b, g, 0, 0))
    hs_spec = BS((None, None, nc, 256, SSD_N), lambda b, g: (b, g, 0, 0, 0))
    p_spec = BS((t, SSD_GW), lambda b, g: (blk0 + b, g))
    in_specs = [p_spec, BS((None, t, 128), lambda b, g: (g, blk0 + b, 0)),
                gspec((4, SSD_GW)), gspec((1, SSD_GW)), gspec((1, 128)), gspec((1, 128)), gspec((1, 256)),
                hs_spec, hs_spec]
    args = [proj, dtg, cw, cb, dtb, alog, drow, hsf, hsb]
    if need_y:
        in_specs.append(BS((t, 256), lambda b, g: (b, g)))
        args.append(dy)
    in_specs += [st_spec, st_spec, BS(memory_space=pl.ANY)]
    args += [dsf, dsb, dproj]
    out_specs = [p_spec, BS((None, t, 128), lambda b, g: (g, b, 0)), st_spec, st_spec,
                 full((SSD_G, 4, SSD_GW)), full((SSD_G, 1, SSD_GW)), full((SSD_G, 1, 128)), full((SSD_G, 1, 128)),
                 full((SSD_G, 1, 256))]
    out_shape = [S(dproj.shape, BF16), S((SSD_G, nb * t, 128), F32),
                 S((nb, SSD_G, 256, SSD_N), F32), S((nb, SSD_G, 256, SSD_N), F32),
                 S((SSD_G, 4, SSD_GW), F32), S((SSD_G, 1, SSD_GW), F32), S((SSD_G, 1, 128), F32),
                 S((SSD_G, 1, 128), F32), S((SSD_G, 1, 256), F32)]
    return pl.pallas_call(
        body, grid=(nb, SSD_G), name="ssd_bwd_lat" if need_y else "ssd_bwd_ctx",
        in_specs=in_specs, out_specs=out_specs, out_shape=out_shape,
        input_output_aliases={len(args) - 1: 0},
        scratch_shapes=[pltpu.VMEM((t, SSD_GW), F32), pltpu.VMEM((t, SSD_GW), F32), pltpu.VMEM((t, 128), F32),
                        pltpu.VMEM((t, 128), F32)],
        compiler_params=_params(),
    )(*args)


def _lru_gate(u, wa, ba, wi, bi, lam):
    r = _sigmoid(_mm(u, wa) + ba)
    i = _sigmoid(_mm(u, wi) + bi)
    log_a = -LRU_C * r * _softplus(-lam)
    a = jnp.exp(log_a)
    x2 = 2.0 * log_a
    em1 = jnp.where(x2 > -0.01, x2 * (1.0 + x2 * (0.5 + x2 * (1.0 / 6.0 + x2 * (1.0 / 24.0)))), jnp.exp(x2) - 1.0)
    return a, jnp.sqrt(-em1) * (i * u)


def _scan_pair(fwd, rev, nblk, width):
    row = lax.broadcasted_iota(jnp.int32, (8, width), 0)

    def block(a_ref, b_ref, h_ref, st, carry, reverse):
        av, bv = a_ref[pl.ds(st, 8), :], b_ref[pl.ds(st, 8), :]
        for s in (1, 2, 4):
            ok = (row < 8 - s) if reverse else (row >= s)
            sh = (8 - s) if reverse else s
            a_sh = jnp.where(ok, pltpu.roll(av, sh, 0), 1.0)
            b_sh = jnp.where(ok, pltpu.roll(bv, sh, 0), 0.0)
            bv = av * b_sh + bv
            av = av * a_sh
        h = bv + av * carry
        h_ref[pl.ds(st, 8), :] = h
        return h[0:1, :] if reverse else h[7:8, :]

    def step(i, carry):
        cf, cr = carry
        cf = block(fwd[0], fwd[1], fwd[2], pl.multiple_of(i * 8, 8), cf, False)
        cr = block(rev[0], rev[1], rev[2], pl.multiple_of((nblk - 1 - i) * 8, 8), cr, True)
        return cf, cr

    return lax.fori_loop(0, nblk, step, (fwd[3], rev[3]))


def _lru_specs(t, blk0):
    p_spec = BS((t, LRU_CB), lambda b, q: (blk0 + b, P_LRU // LRU_CB + q))
    w_spec = BS((2, 2, 128, 128), lambda b, q: (0, q, 0, 0))
    v_spec = BS((2, LRU_CB), lambda b, q: (0, q))
    c_spec = lambda r: BS((r, LRU_CB), lambda b, q: (0, q))
    s_spec = BS((None, 2, LRU_CB), lambda b, q: (b, 0, q))
    return p_spec, w_spec, v_spec, c_spec, s_spec


def _lru_fwd(proj, cw, cb, wa, ba, wi, bi, lam, h0, nb, t, period, blk0, need_y):
    nq = D // LRU_CB
    masks_of = _conv_taps(LRU_CB, period)

    def body(p_ref, cw_ref, cb_ref, wa_ref, ba_ref, wi_ref, bi_ref, lam_ref, h0_ref, *rest):
        if need_y:
            y_ref, hf_ref, hb_ref, fin_ref, sa0, sb0, sa1, sb1 = rest
        else:
            hf_ref, hb_ref, fin_ref, sa0, sb0, sa1, sb1 = rest
        u = _conv_fwd(p_ref[...], cw_ref[...], cb_ref[...], masks_of(t))
        for d, (sa, sb) in enumerate(((sa0, sb0), (sa1, sb1))):
            for j in range(2):
                sl = slice(128 * j, 128 * j + 128)
                a, bb = _lru_gate(u[:, sl], wa_ref[d, j], ba_ref[d:d + 1, sl], wi_ref[d, j], bi_ref[d:d + 1, sl],
                                  lam_ref[d:d + 1, sl])
                sa[:, sl] = a
                sb[:, sl] = bb
        lf, lb = _scan_pair((sa0, sb0, hf_ref, h0_ref[0:1, :]), (sa1, sb1, hb_ref, h0_ref[1:2, :]), t // 8, LRU_CB)
        fin_ref[0:1, :] = lf
        fin_ref[1:2, :] = lb
        if need_y:
            y_ref[...] = hf_ref[...] + hb_ref[...]

    p_spec, w_spec, v_spec, c_spec, s_spec = _lru_specs(t, blk0)
    o_spec = BS((t, LRU_CB), lambda b, q: (b, q))
    out_specs = [o_spec, o_spec, s_spec]
    out_shape = [S((nb * t, D), F32), S((nb * t, D), F32), S((nb, 2, D), F32)]
    if need_y:
        out_specs = [o_spec] + out_specs
        out_shape = [S((nb * t, D), F32)] + out_shape
    return pl.pallas_call(
        body, grid=(nb, nq), name="lru_fwd_lat" if need_y else "lru_fwd_ctx",
        in_specs=[p_spec, c_spec(4), c_spec(1), w_spec, v_spec, w_spec, v_spec, v_spec, s_spec],
        out_specs=out_specs, out_shape=out_shape,
        scratch_shapes=[pltpu.VMEM((t, LRU_CB), F32)] * 4, compiler_params=_params(),
    )(proj, cw, cb, wa, ba, wi, bi, lam, h0)


def _lru_bwd(proj, cw, cb, wa, ba, wi, bi, lam, h0, hf, hb, dy, dfin, dproj, nb, t, period, blk0, need_y):
    nq = D // LRU_CB
    rc = min(256, t)
    masks_of = _conv_taps(LRU_CB, period)

    def body(*refs):
        if need_y:
            (p_ref, cw_ref, cb_ref, wa_ref, ba_ref, wi_ref, bi_ref, lam_ref, h0_ref, hf_ref, hb_ref, dy_ref, dfin_ref, _,
             dp_ref, dh0_ref, dcw_ref, dcb_ref, dwa_ref, dwi_ref, dba_ref, dbi_ref, dlam_ref,
             su, sa0, sa1, sc0, sc1, sg0, sg1) = refs
        else:
            (p_ref, cw_ref, cb_ref, wa_ref, ba_ref, wi_ref, bi_ref, lam_ref, h0_ref, hf_ref, hb_ref, dfin_ref, _,
             dp_ref, dh0_ref, dcw_ref, dcb_ref, dwa_ref, dwi_ref, dba_ref, dbi_ref, dlam_ref,
             su, sa0, sa1, sc0, sc1, sg0, sg1) = refs
            dy_ref = None
        b, q = pl.program_id(0), pl.program_id(1)

        @pl.when(jnp.logical_and(b == 0, q == 0))
        def _():
            for r in (dcw_ref, dcb_ref, dwa_ref, dwi_ref, dba_ref, dbi_ref, dlam_ref):
                r[...] = jnp.zeros_like(r)

        masks = masks_of(t)
        u = _conv_fwd(p_ref[...], cw_ref[...], cb_ref[...], masks)
        su[...] = u
        for d, sa in enumerate((sa0, sa1)):
            for j in range(2):
                sl = slice(128 * j, 128 * j + 128)
                a, _unused = _lru_gate(u[:, sl], wa_ref[d, j], ba_ref[d:d + 1, sl], wi_ref[d, j], bi_ref[d:d + 1, sl],
                                       lam_ref[d:d + 1, sl])
                sa[:, sl] = a
        rowi = lax.broadcasted_iota(jnp.int32, (t, LRU_CB), 0)
        last, first = rowi == t - 1, rowi == 0
        sc0[...] = jnp.where(last, 0.0, pltpu.roll(sa0[...], t - 1, 0))
        sc1[...] = jnp.where(first, 0.0, pltpu.roll(sa1[...], 1, 0))
        g0 = jnp.where(last, dfin_ref[0:1, :], 0.0)
        g1 = jnp.where(first, dfin_ref[1:2, :], 0.0)
        if need_y:
            g0 = g0 + dy_ref[...]
            g1 = g1 + dy_ref[...]
        sg0[...] = g0
        sg1[...] = g1
        zero = jnp.zeros((1, LRU_CB), F32)
        _scan_pair((sc1, sg1, sg1, zero), (sc0, sg0, sg0, zero), t // 8, LRU_CB)
        dh0_ref[0:1, :] = sa0[0:1, :] * sg0[0:1, :]
        dh0_ref[1:2, :] = sa1[t - 1:t, :] * sg1[t - 1:t, :]
        sc0[...] = sg0[...] * jnp.where(first, h0_ref[0:1, :], pltpu.roll(hf_ref[...], 1, 0))
        sc1[...] = sg1[...] * jnp.where(last, h0_ref[1:2, :], pltpu.roll(hb_ref[...], t - 1, 0))

        def rows(ci, carry):
            r0 = pl.multiple_of(ci * rc, rc)
            for j in range(2):
                sl = slice(128 * j, 128 * j + 128)
                du = jnp.zeros((rc, 128), F32)
                for d, (sc, sg) in enumerate(((sc0, sg0), (sc1, sg1))):
                    _, vjp = jax.vjp(_lru_gate, su[pl.ds(r0, rc), sl], wa_ref[d, j], ba_ref[d:d + 1, sl], wi_ref[d, j],
                                     bi_ref[d:d + 1, sl], lam_ref[d:d + 1, sl])
                    du_d, dwa, dba, dwi, dbi, dlam = vjp((sc[pl.ds(r0, rc), sl], sg[pl.ds(r0, rc), sl]))
                    du = du + du_d
                    dwa_ref[d, 2 * q + j] += dwa
                    dwi_ref[d, 2 * q + j] += dwi
                    dba_ref[q, d:d + 1, sl] += dba
                    dbi_ref[q, d:d + 1, sl] += dbi
                    dlam_ref[q, d:d + 1, sl] += dlam
                sa0[pl.ds(r0, rc), sl] = du
            return carry

        lax.fori_loop(0, t // rc, rows, 0)
        draw, dcw, dcb = _conv_bwd(sa0[...], p_ref[...], cw_ref[...], masks)
        dp_ref[...] = draw.astype(BF16)
        dcw_ref[q] += dcw
        dcb_ref[q] += dcb

    p_spec, w_spec, v_spec, c_spec, s_spec = _lru_specs(t, blk0)
    o_spec = BS((t, LRU_CB), lambda b, q: (b, q))
    full = lambda shp: BS(shp, lambda b, q: (0,) * len(shp))
    in_specs = [p_spec, c_spec(4), c_spec(1), w_spec, v_spec, w_spec, v_spec, v_spec, s_spec, o_spec, o_spec]
    args = [proj, cw, cb, wa, ba, wi, bi, lam, h0, hf, hb]
    if need_y:
        in_specs.append(o_spec)
        args.append(dy)
    in_specs += [s_spec, BS(memory_space=pl.ANY)]
    args += [dfin, dproj]
    out_specs = [p_spec, s_spec, full((nq, 4, LRU_CB)), full((nq, 1, LRU_CB)), full((2, 8, 128, 128)),
                 full((2, 8, 128, 128)), full((nq, 2, LRU_CB)), full((nq, 2, LRU_CB)), full((nq, 2, LRU_CB))]
    out_shape = [S(dproj.shape, BF16), S((nb, 2, D), F32), S((nq, 4, LRU_CB), F32), S((nq, 1, LRU_CB), F32),
                 S((2, 8, 128, 128), F32), S((2, 8, 128, 128), F32), S((nq, 2, LRU_CB), F32), S((nq, 2, LRU_CB), F32),
                 S((nq, 2, LRU_CB), F32)]
    return pl.pallas_call(
        body, grid=(nb, nq), name="lru_bwd_lat" if need_y else "lru_bwd_ctx",
        in_specs=in_specs, out_specs=out_specs, out_shape=out_shape, input_output_aliases={len(args) - 1: 0},
        scratch_shapes=[pltpu.VMEM((t, LRU_CB), F32)] * 7, compiler_params=_params(),
    )(*args)


def _mix_core(y_ref, yl_ref, p_ref, nw_ref, bg_ref, wbs_ref, wbl_ref, wo_ref, nrm_s):
    for g in range(SSD_G):
        sl = slice(256 * g, 256 * g + 256)
        nrm_s[:, sl] = _grms(y_ref[:, sl], p_ref[:, sl], nw_ref[:, sl]).astype(BF16)
    br_s = jnp.dot(nrm_s[...], wbs_ref[...], preferred_element_type=F32)
    gl = (yl_ref[...] * _gelu(p_ref[:, 2048:3072])).astype(BF16)
    br_l = jnp.dot(gl, wbl_ref[...], preferred_element_type=F32)
    gs = _sigmoid(p_ref[:, 3072:4096] + bg_ref[:, 0:D])
    gr = _sigmoid(p_ref[:, 4096:5120] + bg_ref[:, D:2 * D])
    mix = (gs * br_s + gr * br_l).astype(BF16)
    xmix = jnp.dot(mix, wo_ref[...], preferred_element_type=F32)
    return br_s, gl, br_l, gs, gr, mix, xmix


def _mix_specs(rt, tiles_per_b):
    row = lambda w: BS((rt, w), lambda i: (i, 0))
    const = lambda shp: BS(shp, lambda i: (0,) * len(shp))
    gate = BS((None, None, 1, D), lambda i: (i // tiles_per_b, 2, 0, 0))
    return row, const, gate


def _mix_fwd(y, ylru, proj, x, m4, wbs, wbl, wo, nw, bg, l1g, l1b, rt, tiles_per_b):
    n = x.shape[0]

    def body(y_ref, yl_ref, p_ref, x_ref, g1_ref, wbs_ref, wbl_ref, wo_ref, nw_ref, bg_ref, lg_ref, lb_ref, x1_ref, nrm_s):
        xmix = _mix_core(y_ref, yl_ref, p_ref, nw_ref, bg_ref, wbs_ref, wbl_ref, wo_ref, nrm_s)[6]
        x1_ref[...] = _resln(x_ref[...], xmix, g1_ref[...], lg_ref[...], lb_ref[...])

    row, const, gate = _mix_specs(rt, tiles_per_b)
    return pl.pallas_call(
        body, grid=(n // rt,), name="mix_fwd",
        in_specs=[row(SSD_INNER), row(D), BS((rt, 5120), lambda i: (i, 1)), row(D), gate,
                  const((SSD_INNER, D)), const((D, D)), const((D, D)), const((1, SSD_INNER)), const((1, 2 * D)),
                  const((1, D)), const((1, D))],
        out_specs=row(D), out_shape=S((n, D), F32),
        scratch_shapes=[pltpu.VMEM((rt, SSD_INNER), BF16)], compiler_params=_params(),
    )(y, ylru, proj, x, m4, wbs, wbl, wo, nw, bg, l1g, l1b)


def _mix_bwd(y, ylru, proj, x, m4, wbs, wbl, wo, nw, bg, l1g, l1b, dx1, dproj, rt, tiles_per_b):
    n = x.shape[0]

    def body(y_ref, yl_ref, p_ref, x_ref, g1_ref, wbs_ref, wbl_ref, wo_ref, nw_ref, bg_ref, lg_ref, lb_ref, dx1_ref, _,
             dp_ref, dy_ref, dyl_ref, dxr_ref, nrm_ref, dbrs_ref, gl_ref, dbrl_ref, mix_ref, dxm_ref,
             dg1_ref, dnw_ref, dbg_ref, dlg_ref, dlb_ref, nrm_s):
        i = pl.program_id(0)

        @pl.when(i == 0)
        def _():
            for r in (dnw_ref, dbg_ref, dlg_ref, dlb_ref):
                r[...] = jnp.zeros_like(r)

        @pl.when(i % tiles_per_b == 0)
        def _():
            dg1_ref[...] = jnp.zeros_like(dg1_ref)

        br_s, gl, br_l, gs, gr, mix, xmix = _mix_core(y_ref, yl_ref, p_ref, nw_ref, bg_ref, wbs_ref, wbl_ref, wo_ref,
                                                      nrm_s)
        _, vjp = jax.vjp(_resln, x_ref[...], xmix, g1_ref[...], lg_ref[...], lb_ref[...])
        dxr, dxmix, dg1, dlg, dlb = vjp(dx1_ref[...])
        dxr_ref[...] = dxr
        dg1_ref[...] += dg1
        dlg_ref[...] += dlg
        dlb_ref[...] += dlb
        dxmb = dxmix.astype(BF16)
        dxm_ref[...] = dxmb
        mix_ref[...] = mix
        nrm_ref[...] = nrm_s[...]
        gl_ref[...] = gl
        dmix = lax.dot_general(dxmb, wo_ref[...], (((1,), (1,)), ((), ())), preferred_element_type=F32)
        dbrs = (dmix * gs).astype(BF16)
        dbrl = (dmix * gr).astype(BF16)
        dbrs_ref[...] = dbrs
        dbrl_ref[...] = dbrl
        dmg_s = dmix * br_s * gs * (1.0 - gs)
        dmg_r = dmix * br_l * gr * (1.0 - gr)
        dp_ref[:, 3072:4096] = dmg_s.astype(BF16)
        dp_ref[:, 4096:5120] = dmg_r.astype(BF16)
        dbg_ref[:, 0:D] += _colsum(dmg_s)
        dbg_ref[:, D:2 * D] += _colsum(dmg_r)
        dnrm = lax.dot_general(dbrs, wbs_ref[...], (((1,), (1,)), ((), ())), preferred_element_type=F32)
        for g in range(SSD_G):
            sl = slice(256 * g, 256 * g + 256)
            _, vjp = jax.vjp(_grms, y_ref[:, sl], p_ref[:, sl], nw_ref[:, sl])
            dyg, dzg, dnwg = vjp(dnrm[:, sl])
            dy_ref[:, sl] = dyg
            dp_ref[:, sl] = dzg.astype(BF16)
            dnw_ref[:, sl] += dnwg
        dgl = lax.dot_general(dbrl, wbl_ref[...], (((1,), (1,)), ((), ())), preferred_element_type=F32)
        _, vjp = jax.vjp(lambda a, c: a * _gelu(c), yl_ref[...], p_ref[:, 2048:3072])
        dyl, dlgate = vjp(dgl)
        dyl_ref[...] = dyl
        dp_ref[:, 2048:3072] = dlgate.astype(BF16)

    row, const, gate = _mix_specs(rt, tiles_per_b)
    pblk = BS((rt, 5120), lambda i: (i, 1))
    nb = n // (rt * tiles_per_b)
    out_specs = [pblk, row(SSD_INNER), row(D), row(D), row(SSD_INNER), row(D), row(D), row(D), row(D), row(D),
                 BS((None, 1, D), lambda i: (i // tiles_per_b, 0, 0)), const((1, SSD_INNER)), const((1, 2 * D)),
                 const((1, D)), const((1, D))]
    out_shape = [S(dproj.shape, BF16), S((n, SSD_INNER), F32), S((n, D), F32), S((n, D), F32),
                 S((n, SSD_INNER), BF16), S((n, D), BF16), S((n, D), BF16), S((n, D), BF16), S((n, D), BF16),
                 S((n, D), BF16), S((nb, 1, D), F32), S((1, SSD_INNER), F32), S((1, 2 * D), F32), S((1, D), F32),
                 S((1, D), F32)]
    return pl.pallas_call(
        body, grid=(n // rt,), name="mix_bwd",
        in_specs=[row(SSD_INNER), row(D), pblk, row(D), gate,
                  const((SSD_INNER, D)), const((D, D)), const((D, D)), const((1, SSD_INNER)), const((1, 2 * D)),
                  const((1, D)), const((1, D)), row(D), BS(memory_space=pl.ANY)],
        out_specs=out_specs, out_shape=out_shape, input_output_aliases={13: 0},
        scratch_shapes=[pltpu.VMEM((rt, SSD_INNER), BF16)], compiler_params=_params(),
    )(y, ylru, proj, x, m4, wbs, wbl, wo, nw, bg, l1g, l1b, dx1, dproj)


def _mlp_step(x1, tgt, m4, w1, b1, w2, b2, l2g, l2b, rt, tiles_per_b):
    n = x1.shape[0]

    def body(x_ref, t_ref, sh_ref, sc_ref, gt_ref, w1_hbm, b1_ref, w2_hbm, b2_ref, lg_ref, lb_ref,
             loss_ref, dx_ref, h2_ref, da1_ref, r2_ref, dmlp_ref, dm_ref, db1_ref, db2_ref, dlg_ref, dlb_ref,
             w1_vm, w2_vm, sem):
        i = pl.program_id(0)

        @pl.when(i == 0)
        def _():
            c1 = pltpu.make_async_copy(w1_hbm, w1_vm, sem.at[0])
            c2 = pltpu.make_async_copy(w2_hbm, w2_vm, sem.at[1])
            c1.start()
            c2.start()
            for r in (loss_ref, db1_ref, db2_ref, dlg_ref, dlb_ref):
                r[...] = jnp.zeros_like(r)
            c1.wait()
            c2.wait()

        @pl.when(i % tiles_per_b == 0)
        def _():
            dm_ref[...] = jnp.zeros_like(dm_ref)

        x1v = x_ref[...]
        h2, vjp_h = jax.vjp(_modln, x1v, sh_ref[...], sc_ref[...])
        h2b = h2.astype(BF16)
        h2_ref[...] = h2b
        r = jnp.maximum(jnp.dot(h2b, w1_vm[...], preferred_element_type=F32) + b1_ref[...], 0.0)
        r2b = (r * r).astype(BF16)
        r2_ref[...] = r2b
        mlp = jnp.dot(r2b, w2_vm[...], preferred_element_type=F32) + b2_ref[...]
        x2, vjp_r = jax.vjp(_resln, x1v, mlp, gt_ref[...], lg_ref[...], lb_ref[...])
        diff = x2 - t_ref[...]
        loss_ref[...] += (0.5 / D) * jnp.sum(diff * diff)
        dxa, dmlp, dgt, dlg, dlb = vjp_r(diff * (1.0 / D))
        dlg_ref[...] += dlg
        dlb_ref[...] += dlb
        dm_ref[2:3, :] += dgt
        db2_ref[...] += _colsum(dmlp)
        dmlpb = dmlp.astype(BF16)
        dmlp_ref[...] = dmlpb
        da1 = lax.dot_general(dmlpb, w2_vm[...], (((1,), (1,)), ((), ())), preferred_element_type=F32) * (2.0 * r)
        db1_ref[...] += _colsum(da1)
        da1b = da1.astype(BF16)
        da1_ref[...] = da1b
        dh2 = lax.dot_general(da1b, w1_vm[...], (((1,), (1,)), ((), ())), preferred_element_type=F32)
        dxb, dsh, dsc = vjp_h(dh2)
        dx_ref[...] = dxa + dxb
        dm_ref[0:1, :] += dsh
        dm_ref[1:2, :] += dsc

    row = lambda w: BS((rt, w), lambda i: (i, 0))
    const = lambda shp: BS(shp, lambda i: (0,) * len(shp))
    mod = lambda k: BS((None, None, 1, D), lambda i: (i // tiles_per_b, k, 0, 0))
    nb = n // (rt * tiles_per_b)
    anyspec = BS(memory_space=pl.ANY)
    return pl.pallas_call(
        body, grid=(n // rt,), name="mlp_step",
        in_specs=[row(D), row(D), mod(3), mod(4), mod(5), anyspec, const((1, MLP_H)), anyspec, const((1, D)),
                  const((1, D)), const((1, D))],
        out_specs=[const((8, 128)), row(D), row(D), row(MLP_H), row(MLP_H), row(D),
                   BS((None, 3, D), lambda i: (i // tiles_per_b, 0, 0)), const((1, MLP_H)), const((1, D)),
                   const((1, D)), const((1, D))],
        out_shape=[S((8, 128), F32), S((n, D), F32), S((n, D), BF16), S((n, MLP_H), BF16), S((n, MLP_H), BF16),
                   S((n, D), BF16), S((nb, 3, D), F32), S((1, MLP_H), F32), S((1, D), F32), S((1, D), F32),
                   S((1, D), F32)],
        scratch_shapes=[pltpu.VMEM((D, MLP_H), BF16), pltpu.VMEM((MLP_H, D), BF16), pltpu.SemaphoreType.DMA((2,))],
        compiler_params=_params(),
    )(x1, tgt, m4, m4, m4, w1, b1, w2, b2, l2g, l2b)


def _pack_win(w):
    parts = []
    for g in range(SSD_G):
        parts += [w[:, 256 * g:256 * g + 256], w[:, 2048 + 128 * g:2176 + 128 * g], w[:, 4160 + 128 * g:4288 + 128 * g]]
    parts += [w[:, 3136:4160], w[:, 5184:7232], w[:, 7232:8256], w[:, 8256:10304], w[:, 3072:3136],
              jnp.zeros((w.shape[0], P_W - P_DT - 64), w.dtype)]
    return jnp.concatenate(parts, axis=1)


def _unpack_win(p):
    xs = [p[:, 512 * g:512 * g + 256] for g in range(SSD_G)]
    bs = [p[:, 512 * g + 256:512 * g + 384] for g in range(SSD_G)]
    cs = [p[:, 512 * g + 384:512 * g + 512] for g in range(SSD_G)]
    return jnp.concatenate(xs + bs + [p[:, P_DT:P_DT + 64], p[:, P_LRU:P_Z]] + cs + [p[:, P_Z:P_DT]], axis=1)


def _pack_conv(w):
    return jnp.stack([jnp.concatenate([w[:, 256 * g:256 * g + 256], w[:, 2048 + 128 * g:2176 + 128 * g],
                                       w[:, 3072 + 128 * g:3200 + 128 * g]], axis=1) for g in range(SSD_G)])


def _unpack_conv(p):
    r = p.shape[1]
    x = jnp.transpose(p[:, :, 0:256], (1, 0, 2)).reshape(r, 2048)
    b = jnp.transpose(p[:, :, 256:384], (1, 0, 2)).reshape(r, 1024)
    c = jnp.transpose(p[:, :, 384:512], (1, 0, 2)).reshape(r, 1024)
    return jnp.concatenate([x, b, c], axis=1)


def _pack_heads(v):
    p = jnp.transpose(v.reshape(2, SSD_G, 4), (1, 0, 2)).reshape(SSD_G, 1, 8)
    return jnp.pad(p, ((0, 0), (0, 0), (0, 120)))


def _unpack_heads(p):
    return jnp.transpose(p[:, 0, 0:8].reshape(SSD_G, 2, 4), (1, 0, 2)).reshape(2, 32)


def _pack_dt(dt):
    n = dt.shape[0]
    p = jnp.transpose(dt.reshape(n, 2, SSD_G, 4), (2, 0, 1, 3)).reshape(SSD_G, n, 8)
    return jnp.pad(p, ((0, 0), (0, 0), (0, 120)))


def _unpack_dt(p):
    n = p.shape[1]
    return jnp.transpose(p[:, :, 0:8].reshape(SSD_G, n, 2, 4), (1, 2, 0, 3)).reshape(n, 64)


def _tk(rows):
    return 512 if rows % 512 == 0 else (256 if rows % 256 == 0 else 128)


def _local_step(x, c, ctx, tgt, sm, wmod, win, wbs, wbl, wo, w1, w2):
    nb, t, _ = x.shape
    tc = ctx.shape[1]
    nl, ncx = nb * t, nb * tc
    rt = 256 if tc % 256 == 0 else 128
    rtm = 128
    xa = jnp.concatenate([x.reshape(nl, D), ctx.reshape(ncx, D)], axis=0)
    tgt2 = tgt.reshape(nl, D)
    cc = jnp.zeros((8, D), F32).at[0:nb].set(c).at[nb].set(sm["c_ctx"])
    m = _mod_fwd(cc, wmod, sm["b_mod"])
    m4 = m.reshape(8, N_MOD, 1, D)
    proj, h1 = _inproj_fwd(xa, m4, win, rtm, nl // rtm, t // rtm, nb)

    cw_s, cb_s = _pack_conv(sm["ssd_conv_w"]), _pack_conv(sm["ssd_conv_b"])
    dtb, alog = _pack_heads(sm["ssd_dt_bias"]), _pack_heads(sm["ssd_a_log"])
    drow = jnp.repeat(sm["ssd_d"].reshape(32), 64).reshape(SSD_G, 1, 256)
    dtg = _pack_dt(proj[:, P_DT:P_DT + 64])
    zst = jnp.zeros((nb, SSD_G, 256, SSD_N), F32)
    zl = jnp.zeros((nb, 2, D), F32)
    ssd_p = (cw_s, cb_s, dtb, alog, drow)
    lru_p = (sm["lru_conv_w"], sm["lru_conv_b"], sm["lru_wa"], sm["lru_ba"], sm["lru_wi"], sm["lru_bi"], sm["lru_lambda"])

    chsf, chsb, csf, csb = _ssd_fwd(proj, dtg, *ssd_p, zst, zst, nb, tc, tc, nl // tc, False)
    y, lhsf, lhsb, _, _ = _ssd_fwd(proj, dtg, *ssd_p, csf, csb, nb, t, GRID_W, 0, True)
    chf, chb, cfin = _lru_fwd(proj, *lru_p, zl, nb, tc, tc, nl // tc, False)
    ylru, lhf, lhb, _ = _lru_fwd(proj, *lru_p, cfin, nb, t, GRID_W, 0, True)
    xl = xa[0:nl]
    mix_w = (wbs, wbl, wo, sm["ssd_norm_w"], sm["b_gate"], sm["ln1_g"], sm["ln1_b"])
    x1 = _mix_fwd(y, ylru, proj, xl, m4, *mix_w, rtm, t // rtm)
    (loss, dx1, h2, da1, r2, dmlp, dm2, db1, db2, dl2g, dl2b) = _mlp_step(
        x1, tgt2, m4, w1, sm["b_mlp1"], w2, sm["b_mlp2"], sm["ln2_g"], sm["ln2_b"], rt, t // rt)

    dproj = jnp.zeros((nl + ncx, P_W), BF16)
    (dproj, dy, dylru, dxres, nrm, dbrs, gl, dbrl, mixb, dxm, dg1, dnw, dbg, dl1g, dl1b) = _mix_bwd(
        y, ylru, proj, xl, m4, *mix_w, dx1, dproj, rtm, t // rtm)
    (dproj, ddt_l, dh0f, dh0b, dcw_l, dcb_l, ddtb_l, dal_l, dd) = _ssd_bwd(
        proj, dtg, *ssd_p, lhsf, lhsb, dy, zst, zst, dproj, nb, t, GRID_W, 0, True)
    (dproj, ddt_c, _, _, dcw_c, dcb_c, ddtb_c, dal_c, _) = _ssd_bwd(
        proj, dtg, *ssd_p, chsf, chsb, None, dh0f, dh0b, dproj, nb, tc, tc, nl // tc, False)
    (dproj, dlh0, gcw_l, gcb_l, gwa_l, gwi_l, gba_l, gbi_l, glam_l) = _lru_bwd(
        proj, *lru_p, cfin, lhf, lhb, dylru, zl, dproj, nb, t, GRID_W, 0, True)
    (dproj, _, gcw_c, gcb_c, gwa_c, gwi_c, gba_c, gbi_c, glam_c) = _lru_bwd(
        proj, *lru_p, zl, chf, chb, None, dlh0, dproj, nb, tc, tc, nl // tc, False)
    ddt = jnp.concatenate([_unpack_dt(ddt_l), _unpack_dt(ddt_c)], axis=0).astype(BF16)
    dproj = lax.dynamic_update_slice(dproj, ddt, (0, P_DT))

    gx, dm1 = _inproj_bwd(xa, m4, win, dproj, rt, 0, nl // rt, t // rt, nb, True, dxres)
    (dmc,) = _inproj_bwd(xa, m4, win, dproj, rt, nl // rt, ncx // rt, ncx // rt, nb, False, None)
    dm = jnp.zeros((8, N_MOD, D), F32)
    dm = dm.at[0:nb].set(jnp.concatenate([dm1, dg1, dm2], axis=1)).at[nb, 0:2].set(dmc[0])
    dwmod, dbmod, dcc = _mod_bwd(cc, wmod, dm.reshape(8, N_MOD * D))

    big = {
        "w_mod": dwmod,
        "w_in": _matmul_tn(h1, dproj, D, 1152, _tk(nl + ncx), "dw_in"),
        "w_br_ssd": _matmul_tn(nrm, dbrs, D, D, _tk(nl), "dw_br_ssd"),
        "w_br_lru": _matmul_tn(gl, dbrl, D, D, _tk(nl), "dw_br_lru"),
        "w_out": _matmul_tn(mixb, dxm, D, D, _tk(nl), "dw_out"),
        "w_mlp1": _matmul_tn(h2, da1, D, D, _tk(nl), "dw_mlp1"),
        "w_mlp2": _matmul_tn(r2, dmlp, D, D, _tk(nl), "dw_mlp2"),
    }
    nq = D // LRU_CB
    small = {
        "c_ctx": dcc[nb],
        "b_mod": dbmod,
        "b_gate": dbg,
        "ssd_conv_w": _unpack_conv(dcw_l + dcw_c),
        "ssd_conv_b": _unpack_conv(dcb_l + dcb_c),
        "ssd_dt_bias": _unpack_heads(ddtb_l + ddtb_c),
        "ssd_a_log": _unpack_heads(dal_l + dal_c),
        "ssd_d": jnp.sum(dd.reshape(32, 64), axis=1),
        "ssd_norm_w": dnw,
        "lru_conv_w": jnp.transpose(gcw_l + gcw_c, (1, 0, 2)).reshape(4, D),
        "lru_conv_b": (gcb_l + gcb_c).reshape(1, D),
        "lru_wa": gwa_l + gwa_c,
        "lru_ba": jnp.transpose(gba_l + gba_c, (1, 0, 2)).reshape(2, D),
        "lru_wi": gwi_l + gwi_c,
        "lru_bi": jnp.transpose(gbi_l + gbi_c, (1, 0, 2)).reshape(2, D),
        "lru_lambda": jnp.transpose(glam_l + glam_c, (1, 0, 2)).reshape(2, D),
        "ln1_g": dl1g, "ln1_b": dl1b, "b_mlp1": db1, "b_mlp2": db2, "ln2_g": dl2g, "ln2_b": dl2b,
    }
    return loss[0, 0], gx.reshape(nb, t, D), big, small


_HBM = BS(memory_space=pl.ANY)


def _place():
    return lax.axis_index("x"), lax.axis_index("y"), lax.axis_index("c")


def _other_chips(x, y):
    return [(1 - x, y), (x, 1 - y), (1 - x, 1 - y)]


def _gather_chips(arrs):
    n = len(arrs)

    def body(*refs):
        ins, outs = refs[:n], refs[n:2 * n]
        send_sems, recv_sems, loc_sems = refs[2 * n:]
        x, y, c = _place()
        me = 2 * x + y
        chips = _other_chips(x, y)
        sends = []
        for a in range(n):
            loc = pltpu.make_async_copy(ins[a], outs[a].at[me], loc_sems.at[a])
            loc.start()
            sends.append(loc)
            for k, (px, py) in enumerate(chips):
                cp = pltpu.make_async_remote_copy(src_ref=ins[a], dst_ref=outs[a].at[me], send_sem=send_sems.at[a, k],
                                                  recv_sem=recv_sems.at[a, k], device_id=(px, py, c), device_id_type=MESH)
                cp.start()
                sends.append(cp)
        for a in range(n):
            for k, (px, py) in enumerate(chips):
                pltpu.make_async_remote_copy(src_ref=ins[a], dst_ref=outs[a].at[2 * px + py], send_sem=send_sems.at[a, k],
                                             recv_sem=recv_sems.at[a, k], device_id=(px, py, c),
                                             device_id_type=MESH).wait_recv()
        for a in range(n):
            sends[4 * a].wait()
            for k in range(3):
                sends[4 * a + 1 + k].wait_send()

    return pl.pallas_call(
        body, name="gather_weights", in_specs=[_HBM] * n, out_specs=[_HBM] * n,
        out_shape=[S((4,) + a.shape, a.dtype) for a in arrs],
        scratch_shapes=[pltpu.SemaphoreType.DMA((n, 3)), pltpu.SemaphoreType.DMA((n, 3)), pltpu.SemaphoreType.DMA((n,))],
    )(*arrs)


def _scatter_chips(arrs):
    n = len(arrs)

    def body(*refs):
        ins, outs = refs[:n], refs[n:2 * n]
        send_sems, recv_sems = refs[2 * n:]
        x, y, c = _place()
        chips = _other_chips(x, y)
        sends = []
        for a in range(n):
            for k, (px, py) in enumerate(chips):
                cp = pltpu.make_async_remote_copy(src_ref=ins[a].at[2 * px + py], dst_ref=outs[a].at[k],
                                                  send_sem=send_sems.at[a, k], recv_sem=recv_sems.at[a, k],
                                                  device_id=(px, py, c), device_id_type=MESH)
                cp.start()
                sends.append(cp)
        for cp in sends:
            cp.wait_recv()
        for cp in sends:
            cp.wait_send()

    return pl.pallas_call(
        body, name="scatter_grads", in_specs=[_HBM] * n, out_specs=[_HBM] * n,
        out_shape=[S((3,) + a.shape[1:], a.dtype) for a in arrs],
        scratch_shapes=[pltpu.SemaphoreType.DMA((n, 3)), pltpu.SemaphoreType.DMA((n, 3))],
    )(*arrs)


def _swap_cores(arrs):
    n = len(arrs)

    def body(*refs):
        ins, outs = refs[:n], refs[n:2 * n]
        send_sems, recv_sems = refs[2 * n:]
        x, y, c = _place()
        sends = []
        for a in range(n):
            cp = pltpu.make_async_remote_copy(src_ref=ins[a], dst_ref=outs[a], send_sem=send_sems.at[a],
                                              recv_sem=recv_sems.at[a], device_id=(x, y, 1 - c), device_id_type=MESH)
            cp.start()
            sends.append(cp)
        for cp in sends:
            cp.wait_recv()
        for cp in sends:
            cp.wait_send()

    return pl.pallas_call(
        body, name="swap_cores", in_specs=[_HBM] * n, out_specs=[_HBM] * n,
        out_shape=[S(a.shape, a.dtype) for a in arrs],
        scratch_shapes=[pltpu.SemaphoreType.DMA((n,)), pltpu.SemaphoreType.DMA((n,))],
    )(*arrs)


def _gather_all(v):
    def body(in_ref, out_ref, send_sems, recv_sems, loc_sem):
        x, y, c = _place()
        me = 4 * x + 2 * y + c
        loc = pltpu.make_async_copy(in_ref, out_ref.at[me], loc_sem)
        loc.start()
        sends = []
        for k in range(1, 8):
            px, py, pc = x ^ (k >> 2), y ^ ((k >> 1) & 1), c ^ (k & 1)
            cp = pltpu.make_async_remote_copy(src_ref=in_ref, dst_ref=out_ref.at[me], send_sem=send_sems.at[k - 1],
                                              recv_sem=recv_sems.at[k - 1], device_id=(px, py, pc), device_id_type=MESH)
            cp.start()
            sends.append(cp)
        for k in range(1, 8):
            px, py, pc = x ^ (k >> 2), y ^ ((k >> 1) & 1), c ^ (k & 1)
            pltpu.make_async_remote_copy(src_ref=in_ref, dst_ref=out_ref.at[4 * px + 2 * py + pc],
                                         send_sem=send_sems.at[k - 1], recv_sem=recv_sems.at[k - 1],
                                         device_id=(px, py, pc), device_id_type=MESH).wait_recv()
        for cp in sends:
            cp.wait_send()
        loc.wait()

    return pl.pallas_call(
        body, name="gather_small", in_specs=[_HBM], out_specs=_HBM, out_shape=S((8,) + v.shape, v.dtype),
        scratch_shapes=[pltpu.SemaphoreType.DMA((7,)), pltpu.SemaphoreType.DMA((7,)), pltpu.SemaphoreType.DMA(())],
    )(v)


def _row_tile(r, c=128):
    tr = 256 if c <= 1024 else (128 if c <= 2048 else 64)
    return tr if r % tr == 0 else r


def _sum4(own, recv, name):
    r, c = own.shape
    tr = _row_tile(r, c)

    def body(o_ref, r_ref, out_ref):
        acc = o_ref[...]
        for k in range(3):
            acc = acc + r_ref[k].astype(F32)
        out_ref[...] = acc

    return pl.pallas_call(
        body, grid=(r // tr,), name=name,
        in_specs=[BS((tr, c), lambda i: (i, 0)), BS((3, tr, c), lambda i: (0, i, 0))],
        out_specs=BS((tr, c), lambda i: (i, 0)), out_shape=S((r, c), F32), compiler_params=_params(),
    )(own, recv)


def _adam_math(w, g, m, v):
    m = ADAM_B1 * m + (1.0 - ADAM_B1) * g
    v = ADAM_B2 * v + (1.0 - ADAM_B2) * (g * g)
    m_hat = m / (1.0 - ADAM_B1 ** ADAM_STEP)
    v_hat = v / (1.0 - ADAM_B2 ** ADAM_STEP)
    return -ADAM_LR * (m_hat / (jnp.sqrt(v_hat) + ADAM_EPS) + ADAM_WD * w), m, v


def _adam_pair(pa, pb, w, m, v, name):
    r, c = w.shape
    tr = _row_tile(r, c)

    def body(pa_ref, pb_ref, w_ref, m_ref, v_ref, g_ref, d_ref, nm_ref, nv_ref):
        g = pa_ref[...] + pb_ref[...]
        g_ref[...] = g
        d_ref[...], nm_ref[...], nv_ref[...] = _adam_math(w_ref[...], g, m_ref[...], v_ref[...])

    spec = BS((tr, c), lambda i: (i, 0))
    return pl.pallas_call(
        body, grid=(r // tr,), name=name, in_specs=[spec] * 5, out_specs=[spec] * 4, out_shape=[S((r, c), F32)] * 4,
        compiler_params=_params(),
    )(pa, pb, w, m, v)


def _sum8(parts):
    r = parts.shape[1]
    tr = _row_tile(r)

    def body(p_ref, out_ref):
        acc = p_ref[0]
        for k in range(1, 8):
            acc = acc + p_ref[k]
        out_ref[...] = acc

    return pl.pallas_call(
        body, grid=(r // tr,), name="sum_small", in_specs=[BS((8, tr, 128), lambda i: (0, i, 0))],
        out_specs=BS((tr, 128), lambda i: (i, 0)), out_shape=S((r, 128), F32), compiler_params=_params(),
    )(parts)


def _adam_flat(g, w, m, v):
    r = w.shape[0]
    tr = _row_tile(r)

    def body(g_ref, w_ref, m_ref, v_ref, d_ref, nm_ref, nv_ref):
        d_ref[...], nm_ref[...], nv_ref[...] = _adam_math(w_ref[...], g_ref[...], m_ref[...], v_ref[...])

    spec = BS((tr, 128), lambda i: (i, 0))
    return pl.pallas_call(
        body, grid=(r // tr,), name="adam_small", in_specs=[spec] * 4, out_specs=[spec] * 3,
        out_shape=[S((r, 128), F32)] * 3, compiler_params=_params(),
    )(g, w, m, v)


def _flatten(arrs, rows_mult=256):
    flat = jnp.concatenate([a.reshape(-1) for a in arrs])
    n = flat.shape[0]
    rows = -(-n // 128)
    rows = -(-rows // rows_mult) * rows_mult
    return jnp.pad(flat, (0, rows * 128 - n)).reshape(rows, 128)


def _unflatten(flat, shapes):
    flat = flat.reshape(-1)
    out, o = [], 0
    for shp in shapes:
        n = int(np.prod(shp))
        out.append(flat[o:o + n].reshape(shp))
        o += n
    return out


BIG = ["w_mod", "w_in", "w_br_ssd", "w_br_lru", "w_out", "w_mlp1", "w_mlp2"]
COL_SHARDED = {"w_mod": N_MOD * D, "w_in": IN_COLS, "w_mlp1": MLP_H}
SMALL_SHARDED = ["ssd_conv_w", "lru_conv_w", "lru_ba", "lru_bi", "lru_lambda"]
WEIGHTS = ['c_ctx', 'w_mod', 'b_mod', 'w_in', 'b_gate', 'ssd_conv_w', 'ssd_conv_b', 'ssd_dt_bias', 'ssd_a_log', 'ssd_d',
           'ssd_norm_w', 'lru_conv_w', 'lru_conv_b', 'lru_wa', 'lru_ba', 'lru_wi', 'lru_bi', 'lru_lambda', 'w_br_ssd',
           'w_br_lru', 'w_out', 'ln1_g', 'ln1_b', 'w_mlp1', 'b_mlp1', 'w_mlp2', 'b_mlp2', 'ln2_g', 'ln2_b']
SMALL = [n for n in WEIGHTS if n not in BIG]


def _full_from_chips(g4, name):
    if name in COL_SHARDED:
        return jnp.transpose(g4, (1, 0, 2)).reshape(g4.shape[1], 4 * g4.shape[2])
    return g4.reshape(4 * g4.shape[1], g4.shape[2])


def _chips_from_full(full, name):
    if name in COL_SHARDED:
        r, c = full.shape
        return jnp.transpose(full.reshape(r, 4, c // 4), (1, 0, 2))
    return full.reshape(4, full.shape[0] // 4, full.shape[1])


def kernel(x, c, ctx, c_ctx, w_mod, b_mod, w_in, b_gate, ssd_conv_w, ssd_conv_b, ssd_dt_bias, ssd_a_log, ssd_d, ssd_norm_w, lru_conv_w, lru_conv_b, lru_wa, lru_ba, lru_wi, lru_bi, lru_lambda, w_br_ssd, w_br_lru, w_out, ln1_g, ln1_b, w_mlp1, b_mlp1, w_mlp2, b_mlp2, ln2_g, ln2_b, loss_target, m_c_ctx, m_w_mod, m_b_mod, m_w_in, m_b_gate, m_ssd_conv_w, m_ssd_conv_b, m_ssd_dt_bias, m_ssd_a_log, m_ssd_d, m_ssd_norm_w, m_lru_conv_w, m_lru_conv_b, m_lru_wa, m_lru_ba, m_lru_wi, m_lru_bi, m_lru_lambda, m_w_br_ssd, m_w_br_lru, m_w_out, m_ln1_g, m_ln1_b, m_w_mlp1, m_b_mlp1, m_w_mlp2, m_b_mlp2, m_ln2_g, m_ln2_b, v_c_ctx, v_w_mod, v_b_mod, v_w_in, v_b_gate, v_ssd_conv_w, v_ssd_conv_b, v_ssd_dt_bias, v_ssd_a_log, v_ssd_d, v_ssd_norm_w, v_lru_conv_w, v_lru_conv_b, v_lru_wa, v_lru_ba, v_lru_wi, v_lru_bi, v_lru_lambda, v_w_br_ssd, v_w_br_lru, v_w_out, v_ln1_g, v_ln1_b, v_w_mlp1, v_b_mlp1, v_w_mlp2, v_b_mlp2, v_ln2_g, v_ln2_b):
    given = dict(locals())
    w = {n: given[n] for n in WEIGHTS}
    mom = {n: given["m_" + n] for n in WEIGHTS}
    var = {n: given["v_" + n] for n in WEIGHTS}
    chip = 2 * lax.axis_index("x") + lax.axis_index("y")

    shard2d = {n: w[n].reshape(w[n].shape[-2:]) for n in BIG}
    small_pack = _flatten([w[n] for n in SMALL_SHARDED], rows_mult=8)
    gathered = _gather_chips([shard2d[n].astype(BF16) for n in BIG] + [small_pack])
    full = {n: _full_from_chips(g, n) for n, g in zip(BIG, gathered[:-1])}
    full["w_in"] = _pack_win(full["w_in"])
    per_chip = [_unflatten(gathered[-1][q], [w[n].shape for n in SMALL_SHARDED]) for q in range(4)]
    sm = {n: jnp.concatenate([per_chip[q][i] for q in range(4)], axis=-1) for i, n in enumerate(SMALL_SHARDED)}
    for n in SMALL:
        if n not in sm:
            sm[n] = w[n]
    sm = {n: (a.reshape(a.shape[1:]) if a.ndim >= 3 else a) for n, a in sm.items()}

    loss, gx, gbig, gsmall = _local_step(x, c, ctx, loss_target, sm, *[full[n] for n in BIG])
    gbig["w_in"] = _unpack_win(gbig["w_in"])

    slabs = {n: _chips_from_full(gbig[n], n) for n in BIG}
    recv = _scatter_chips([slabs[n].astype(BF16) for n in BIG])
    part = [_sum4(lax.dynamic_index_in_dim(slabs[n], chip, 0, keepdims=False), r, "sum_" + n) for n, r in zip(BIG, recv)]
    other = _swap_cores(part)
    out = {}
    for n, pa, pb in zip(BIG, part, other):
        res = _adam_pair(pa, pb, shard2d[n], mom[n].reshape(pa.shape), var[n].reshape(pa.shape), "adam_" + n)
        out[n] = [r.reshape(w[n].shape) for r in res]

    full_shapes = [gsmall[n].shape for n in SMALL]
    gs_all = _unflatten(_sum8(_gather_all(_flatten([gsmall[n] for n in SMALL]))), full_shapes)
    gs = {}
    for n, g in zip(SMALL, gs_all):
        if n in SMALL_SHARDED:
            width = w[n].shape[-1]
            g = lax.dynamic_slice_in_dim(g, chip * width, width, axis=g.ndim - 1)
        gs[n] = g.reshape(w[n].shape)
    shapes = [w[n].shape for n in SMALL]
    d_s, m_s, v_s = _adam_flat(_flatten([gs[n] for n in SMALL]), _flatten([w[n] for n in SMALL]),
                               _flatten([mom[n] for n in SMALL]), _flatten([var[n] for n in SMALL]))
    for n, d_, m_, v_ in zip(SMALL, _unflatten(d_s, shapes), _unflatten(m_s, shapes), _unflatten(v_s, shapes)):
        out[n] = [gs[n], d_, m_, v_]

    loss = lax.psum(loss, ("x", "y", "c"))
    return (loss, gx, *[out[n][0] for n in WEIGHTS], *[out[n][1] for n in WEIGHTS], *[out[n][2] for n in WEIGHTS],
            *[out[n][3] for n in WEIGHTS])
```

```python
import functools

import numpy as np
import jax
import jax.numpy as jnp
from jax import lax
from jax.experimental import pallas as pl
from jax.experimental.pallas import tpu as pltpu

F32, BF16 = jnp.float32, jnp.bfloat16
S = jax.ShapeDtypeStruct
BS = pl.BlockSpec
MESH = pl.DeviceIdType.MESH

D = 1024
GRID_W = 64
SSD_INNER, SSD_G, SSD_N, SSD_L = 2048, 8, 128, 128
SSD_GW = 512
MLP_H = 4096
N_MOD = 6
ALPHA = 2.0 ** 0.25
LN_EPS, RMS_EPS = 1e-6, 1e-5
LRU_C = 8.0
P_XBC, P_LRU, P_Z, P_LG, P_MG, P_DT, P_W = 0, 4096, 5120, 7168, 8192, 10240, 10368
P_CB = 3456
IN_COLS = 10304
LRU_CB = 256
ADAM_LR, ADAM_B1, ADAM_B2, ADAM_EPS, ADAM_WD, ADAM_STEP = 0.001, 0.9, 0.999, 1e-08, 0.01, 10
VMEM_LIMIT = 56 * 2 ** 20


def _params(**kw):
    return pltpu.CompilerParams(vmem_limit_bytes=VMEM_LIMIT, **kw)


def _dot(a, b):
    return jnp.dot(a.astype(BF16), b.astype(BF16), preferred_element_type=F32)


def _dot_nt(a, b):
    return lax.dot_general(a.astype(BF16), b.astype(BF16), (((1,), (1,)), ((), ())), preferred_element_type=F32)


def _dot_tn(a, b):
    return lax.dot_general(a.astype(BF16), b.astype(BF16), (((0,), (0,)), ((), ())), preferred_element_type=F32)


@jax.custom_vjp
def _mm(a, b):
    return _dot(a, b)


_mm.defvjp(lambda a, b: (_dot(a, b), (a, b)), lambda r, g: (_dot_nt(g, r[1]), _dot_tn(r[0], g)))


@jax.custom_vjp
def _mm_nt(a, b):
    return _dot_nt(a, b)


_mm_nt.defvjp(lambda a, b: (_dot_nt(a, b), (a, b)), lambda r, g: (_dot(g, r[1]), _dot_tn(g, r[0])))


@jax.custom_vjp
def _mm_tn(a, b):
    return _dot_tn(a, b)


_mm_tn.defvjp(lambda a, b: (_dot_tn(a, b), (a, b)), lambda r, g: (_dot_nt(r[1], g), _dot(r[0], g)))

def _split3(v):
    h = v.astype(BF16)
    r = v - h.astype(F32)
    m = r.astype(BF16)
    return h, m, (r - m.astype(F32)).astype(BF16)


def _sel_dot(sel, v, dims):
    sel_first = dims[0] == "s"
    dn = {"sv": (((1,), (0,)), ((), ())), "sTv": (((0,), (0,)), ((), ())), "vs": (((1,), (0,)), ((), ())),
          "vsT": (((1,), (1,)), ((), ()))}[dims]
    out = None
    for part in _split3(v):
        a, b = (sel, part) if sel_first else (part, sel)
        term = lax.dot_general(a, b, dn, preferred_element_type=F32)
        out = term if out is None else out + term
    return out


@jax.custom_vjp
def _cum_mm(tri, v):
    return _sel_dot(tri, v, "sv")


_cum_mm.defvjp(lambda tri, v: (_sel_dot(tri, v, "sv"), tri),
               lambda tri, g: (jnp.zeros_like(tri), _sel_dot(tri, g, "sTv")))


@jax.custom_vjp
def _xp_mm(v, e):
    return _sel_dot(e, v, "vs")


_xp_mm.defvjp(lambda v, e: (_sel_dot(e, v, "vs"), e),
              lambda e, g: (_sel_dot(e, g, "vsT"), jnp.zeros_like(e)))


def _sigmoid(x):
    return 1.0 / (1.0 + jnp.exp(-x))


def _silu(x):
    return x * _sigmoid(x)


def _dsilu(x):
    s = _sigmoid(x)
    return s * (1.0 + x * (1.0 - s))


def _softplus(x):
    return jnp.maximum(x, 0.0) + jnp.log1p(jnp.exp(-jnp.abs(x)))


def _gelu(x):
    return 0.5 * x * (1.0 + jnp.tanh(0.7978845608028654 * (x + 0.044715 * x * x * x)))


def _ln(x):
    mu = jnp.mean(x, axis=-1, keepdims=True)
    xc = x - mu
    var = jnp.mean(xc * xc, axis=-1, keepdims=True)
    return xc * lax.rsqrt(var + LN_EPS)


def _modln(x, shift, scale):
    return _ln(x) * (1.0 + scale) + shift


def _resln(x, sub, gate, g, b):
    return _ln(ALPHA * x + gate * sub) * g + b


def _grms(y, z, w):
    u = y * _silu(z)
    return u * lax.rsqrt(jnp.mean(u * u, axis=-1, keepdims=True) + RMS_EPS) * w


def _colsum(v):
    return jnp.sum(v, axis=0, keepdims=True)


def _conv_taps(width, period):
    def masks(rows):
        pos = lax.broadcasted_iota(jnp.int32, (rows, width), 0) & (period - 1)
        return [jnp.logical_and(pos + (k - 2) >= 0, pos + (k - 2) < period) for k in range(4)]
    return masks


def _conv_fwd(raw, w, b, masks):
    rows = raw.shape[0]
    pre = b + raw * w[2:3, :]
    for k in (0, 1, 3):
        sh = pltpu.roll(raw, (2 - k) % rows, 0)
        pre = pre + jnp.where(masks[k], sh, 0.0) * w[k:k + 1, :]
    return pre


def _for_rows(t, rb, fn):
    n = t // rb
    unroll = 4 if n % 4 == 0 else 1

    def step(i, carry):
        for u in range(unroll):
            fn(pl.multiple_of((i * unroll + u) * rb, rb))
        return carry

    lax.fori_loop(0, n // unroll, step, 0)


def _loop_unrolled(n, unroll, body, init):
    def step(i, carry):
        for u in range(unroll):
            carry = body(i * unroll + u, carry)
        return carry

    return lax.fori_loop(0, n // unroll, step, init)


def _conv_bwd(dpre, raw, w, masks):
    rows = raw.shape[0]
    draw = dpre * w[2:3, :]
    dws = []
    for k in range(4):
        if k == 2:
            dws.append(_colsum(dpre * raw))
            continue
        sh = pltpu.roll(raw, (2 - k) % rows, 0)
        dws.append(_colsum(dpre * jnp.where(masks[k], sh, 0.0)))
        back = pltpu.roll(jnp.where(masks[k], dpre, 0.0) * w[k:k + 1, :], (k - 2) % rows, 0)
        draw = draw + back
    return draw, jnp.concatenate(dws, axis=0), _colsum(dpre)


def _mod_fwd(cc, wmod, bmod):
    def body(cc_ref, w_ref, b_ref, o_ref):
        o_ref[...] = _dot(_silu(cc_ref[...]), w_ref[...]) + b_ref[...]

    return pl.pallas_call(
        body, grid=(N_MOD,), name="mod_fwd",
        in_specs=[BS((8, D), lambda j: (0, 0)), BS((D, D), lambda j: (0, j)), BS((1, D), lambda j: (0, j))],
        out_specs=BS((8, D), lambda j: (0, j)), out_shape=S((8, N_MOD * D), F32), compiler_params=_params(),
    )(cc, wmod, bmod)


def _mod_bwd(cc, wmod, dm):
    def body(cc_ref, w_ref, dm_ref, dw_ref, db_ref, dcc_ref):
        j = pl.program_id(0)
        c = cc_ref[...]
        dmv = dm_ref[...]
        dw_ref[...] = _dot_tn(_silu(c), dmv)
        db_ref[...] = _colsum(dmv)

        @pl.when(j == 0)
        def _():
            dcc_ref[...] = jnp.zeros_like(dcc_ref)

        dcc_ref[...] += _dot_nt(dmv, w_ref[...]) * _dsilu(c)

    return pl.pallas_call(
        body, grid=(N_MOD,), name="mod_bwd",
        in_specs=[BS((8, D), lambda j: (0, 0)), BS((D, D), lambda j: (0, j)), BS((8, D), lambda j: (0, j))],
        out_specs=[BS((D, D), lambda j: (0, j)), BS((1, D), lambda j: (0, j)), BS((8, D), lambda j: (0, 0))],
        out_shape=[S((D, N_MOD * D), F32), S((1, N_MOD * D), F32), S((8, D), F32)], compiler_params=_params(),
    )(cc, wmod, dm)


def _inproj_fwd(xa, m4, win, rt, n_lat_tiles, tiles_per_b, ctx_row):
    n_tiles = xa.shape[0] // rt

    def mrow(i):
        return jnp.where(i < n_lat_tiles, i // tiles_per_b, ctx_row)

    def body(x_ref, sh_ref, sc_ref, w_hbm, p_ref, h_ref, w_vm, sem):
        @pl.when(pl.program_id(0) == 0)
        def _():
            cp = pltpu.make_async_copy(w_hbm, w_vm, sem)
            cp.start()
            cp.wait()

        hb = _modln(x_ref[...], sh_ref[...], sc_ref[...]).astype(BF16)
        h_ref[...] = hb
        for j in range(P_W // P_CB):
            sl = slice(j * P_CB, (j + 1) * P_CB)
            p_ref[:, sl] = jnp.dot(hb, w_vm[:, sl], preferred_element_type=F32)

    return pl.pallas_call(
        body, grid=(n_tiles,), name="inproj_fwd",
        in_specs=[BS((rt, D), lambda i: (i, 0)),
                  BS((None, None, 1, D), lambda i: (mrow(i), 0, 0, 0)),
                  BS((None, None, 1, D), lambda i: (mrow(i), 1, 0, 0)),
                  BS(memory_space=pl.ANY)],
        out_specs=[BS((rt, P_W), lambda i: (i, 0)), BS((rt, D), lambda i: (i, 0))],
        out_shape=[S((xa.shape[0], P_W), F32), S((xa.shape[0], D), BF16)],
        scratch_shapes=[pltpu.VMEM((D, P_W), BF16), pltpu.SemaphoreType.DMA(())], compiler_params=_params(),
    )(xa, m4, m4, win)


def _inproj_bwd(xa, m4, win, dproj, rt, tile0, n_tiles, tiles_per_b, ctx_row, latent, dxres):
    def mrow(i):
        return (i // tiles_per_b) if latent else ctx_row

    def body(x_ref, sh_ref, sc_ref, dp_ref, w_hbm, *rest):
        if latent:
            dxr_ref, gx_ref, dm_ref, w_vm, sem = rest
        else:
            dm_ref, w_vm, sem = rest
        i = pl.program_id(0)

        @pl.when(i == 0)
        def _():
            cp = pltpu.make_async_copy(w_hbm, w_vm, sem)
            cp.start()
            cp.wait()

        dh = lax.dot_general(dp_ref[...], w_vm[...], (((1,), (1,)), ((), ())), preferred_element_type=F32)
        _, vjp = jax.vjp(_modln, x_ref[...], sh_ref[...], sc_ref[...])
        dx, dsh, dsc = vjp(dh)
        if latent:
            gx_ref[...] = dx + dxr_ref[...]

        @pl.when(i % tiles_per_b == 0)
        def _():
            dm_ref[...] = jnp.zeros_like(dm_ref)

        dm_ref[0:1, :] += dsh
        dm_ref[1:2, :] += dsc

    nb = n_tiles // tiles_per_b
    in_specs = [BS((rt, D), lambda i: (tile0 + i, 0)),
                BS((None, None, 1, D), lambda i: (mrow(i), 0, 0, 0)),
                BS((None, None, 1, D), lambda i: (mrow(i), 1, 0, 0)),
                BS((rt, P_W), lambda i: (tile0 + i, 0)),
                BS(memory_space=pl.ANY)]
    args = [xa, m4, m4, dproj, win]
    dm_spec = BS((None, 2, D), lambda i: (i // tiles_per_b, 0, 0))
    if latent:
        in_specs.append(BS((rt, D), lambda i: (i, 0)))
        args.append(dxres)
        out_specs = [BS((rt, D), lambda i: (i, 0)), dm_spec]
        out_shape = [S((n_tiles * rt, D), F32), S((nb, 2, D), F32)]
    else:
        out_specs = [dm_spec]
        out_shape = [S((nb, 2, D), F32)]
    return pl.pallas_call(
        body, grid=(n_tiles,), name="inproj_bwd_lat" if latent else "inproj_bwd_ctx",
        in_specs=in_specs, out_specs=out_specs, out_shape=out_shape,
        scratch_shapes=[pltpu.VMEM((D, P_W), BF16), pltpu.SemaphoreType.DMA(())],
        compiler_params=_params(),
    )(*args)


def _matmul_tn(a, b, tm, tn, tk, name):
    k, m = a.shape
    n = b.shape[1]

    def body(a_ref, b_ref, o_ref):
        @pl.when(pl.program_id(2) == 0)
        def _():
            o_ref[...] = jnp.zeros_like(o_ref)

        o_ref[...] += lax.dot_general(a_ref[...], b_ref[...], (((0,), (0,)), ((), ())), preferred_element_type=F32)

    return pl.pallas_call(
        body, grid=(m // tm, n // tn, k // tk), name=name,
        in_specs=[BS((tk, tm), lambda i, j, kk: (kk, i)), BS((tk, tn), lambda i, j, kk: (kk, j))],
        out_specs=BS((tm, tn), lambda i, j, kk: (i, j)), out_shape=S((m, n), F32), compiler_params=_params(),
    )(a, b)


def _ssd_consts():
    n = SSD_L
    ii = lax.broadcasted_iota(jnp.int32, (n, n), 0)
    jj = lax.broadcasted_iota(jnp.int32, (n, n), 1)
    er = lax.broadcasted_iota(jnp.int32, (128, 256), 0)
    ec = lax.broadcasted_iota(jnp.int32, (128, 256), 1) >> 6
    lane = lax.broadcasted_iota(jnp.int32, (1, 256), 1) >> 6
    per_dir = []
    for d in (0, 1):
        mask = (jj >= ii) if d else (jj <= ii)
        per_dir.append((mask, mask.astype(BF16), (er == ec + 4 * d).astype(BF16)))
    return per_dir, [(lane == h).astype(F32) for h in range(4)]


def _ssd_chunk(x, bm, cm, dtc, dtx, alog, hst, consts, hmasks, rev):
    n = SSD_L
    mask, tri, e = consts
    cum = _cum_mm(tri, dtc * (-jnp.exp(alog)))
    cum_x = _xp_mm(cum, e)
    tot_x = cum_x[0:1, :] if rev else cum_x[n - 1:n, :]
    xd = x * dtx
    hn = jnp.exp(tot_x) * hst + _mm_tn(bm, xd * jnp.exp(tot_x - cum_x))
    if cm is None:
        return hn
    cum_t = cum.T
    cb = _mm_nt(cm, bm)
    y = jnp.exp(cum_x) * _mm(cm, hst)
    for h in range(4):
        k = 4 * rev + h
        decay = jnp.exp(jnp.where(mask, cum[:, k:k + 1] - cum_t[k:k + 1, :], -1e30))
        y = y + _mm(cb * decay, xd * hmasks[h])
    return y, hn


def _ssd_fwd(proj, dtg, cw, cb, dtb, alog, drow, h0f, h0b, nb, t, period, blk0, need_y):
    nc = t // SSD_L
    rb = period
    assert t % rb == 0
    unroll = 2 if nc % 2 == 0 else 1
    masks_of = _conv_taps(SSD_GW, period)

    def body(p_ref, dt_ref, cw_ref, cb_ref, dtb_ref, al_ref, d_ref, h0f_ref, h0b_ref, *rest):
        if need_y:
            y_ref, hsf_ref, hsb_ref, sf_ref, sb_ref, act, dts, dtxs = rest
        else:
            hsf_ref, hsb_ref, sf_ref, sb_ref, act, dts, dtxs = rest
        masks = masks_of(rb)
        per_dir, hmasks = _ssd_consts()

        def prologue(r0):
            rows = pl.ds(r0, rb)
            a = _silu(_conv_fwd(p_ref[rows, :], cw_ref[...], cb_ref[...], masks))
            act[rows, :] = a
            if need_y:
                y_ref[rows, :] = d_ref[...] * a[:, 0:256]
            dtv = _softplus(dt_ref[rows, :] + dtb_ref[...])
            dts[rows, :] = dtv
            for d in (0, 1):
                dtxs[rows, 256 * d:256 * d + 256] = _xp_mm(dtv, per_dir[d][2])

        _for_rows(t, rb, prologue)
        al = al_ref[...]

        def chunk(ci, carry):
            out = []
            for d, hst, hs_ref in ((0, carry[0], hsf_ref), (1, carry[1], hsb_ref)):
                c = (nc - 1 - ci) if d else ci
                r0 = pl.multiple_of(c * SSD_L, SSD_L)
                a = act[pl.ds(r0, SSD_L), :]
                hs_ref[c] = hst
                res = _ssd_chunk(a[:, 0:256], a[:, 256:384], a[:, 384:512] if need_y else None,
                                 dts[pl.ds(r0, SSD_L), :], dtxs[pl.ds(r0, SSD_L), 256 * d:256 * d + 256], al, hst,
                                 per_dir[d], hmasks, d)
                if need_y:
                    y_ref[pl.ds(r0, SSD_L), :] += res[0]
                    res = res[1]
                out.append(res)
            return tuple(out)

        sf_ref[...], sb_ref[...] = _loop_unrolled(nc, unroll, chunk, (h0f_ref[...], h0b_ref[...]))

    gspec = lambda shp: BS((None,) + shp, lambda b, g: (g,) + (0,) * len(shp))
    st_spec = BS((None, None, SSD_N, 256), lambda b, g: (b, g, 0, 0))
    hs_spec = BS((None, None, nc, SSD_N, 256), lambda b, g: (b, g, 0, 0, 0))
    in_specs = [BS((t, SSD_GW), lambda b, g: (blk0 + b, g)), BS((None, t, 128), lambda b, g: (g, blk0 + b, 0)),
                gspec((4, SSD_GW)), gspec((1, SSD_GW)), gspec((1, 128)), gspec((1, 128)), gspec((1, 256)),
                st_spec, st_spec]
    out_specs = [hs_spec, hs_spec, st_spec, st_spec]
    out_shape = [S((nb, SSD_G, nc, SSD_N, 256), F32)] * 2 + [S((nb, SSD_G, SSD_N, 256), F32)] * 2
    if need_y:
        out_specs = [BS((t, 256), lambda b, g: (b, g))] + out_specs
        out_shape = [S((nb * t, SSD_INNER), F32)] + out_shape
    return pl.pallas_call(
        body, grid=(nb, SSD_G), name="ssd_fwd_lat" if need_y else "ssd_fwd_ctx",
        in_specs=in_specs, out_specs=out_specs, out_shape=out_shape,
        scratch_shapes=[pltpu.VMEM((t, SSD_GW), F32), pltpu.VMEM((t, 128), F32), pltpu.VMEM((t, SSD_GW), F32)],
        compiler_params=_params(),
    )(proj, dtg, cw, cb, dtb, alog, drow, h0f, h0b)


def _ssd_bwd(proj, dtg, cw, cb, dtb, alog, drow, hsf, hsb, dy, dsf, dsb, dproj, nb, t, period, blk0, need_y):
    nc = t // SSD_L
    rb = period
    assert t % rb == 0
    unroll = 2 if nc % 2 == 0 else 1
    masks_of = _conv_taps(SSD_GW, period)

    def body(*refs):
        if need_y:
            (p_ref, dt_ref, cw_ref, cb_ref, dtb_ref, al_ref, d_ref, hsf_ref, hsb_ref, dy_ref, dsf_ref, dsb_ref, _,
             dp_ref, ddt_ref, dhf_ref, dhb_ref, dcw_ref, dcb_ref, ddtb_ref, dal_ref, dd_ref,
             pre, dact, dts, ddts, dtxs) = refs
        else:
            (p_ref, dt_ref, cw_ref, cb_ref, dtb_ref, al_ref, d_ref, hsf_ref, hsb_ref, dsf_ref, dsb_ref, _,
             dp_ref, ddt_ref, dhf_ref, dhb_ref, dcw_ref, dcb_ref, ddtb_ref, dal_ref, dd_ref,
             pre, dact, dts, ddts, dtxs) = refs
            dy_ref = None
        b, g = pl.program_id(0), pl.program_id(1)

        @pl.when(jnp.logical_and(b == 0, g == 0))
        def _():
            for r in (dcw_ref, dcb_ref, ddtb_ref, dal_ref, dd_ref):
                r[...] = jnp.zeros_like(r)

        masks = masks_of(rb)
        per_dir, hmasks = _ssd_consts()

        def prologue(r0):
            rows = pl.ds(r0, rb)
            pre[rows, :] = _conv_fwd(p_ref[rows, :], cw_ref[...], cb_ref[...], masks)
            dtv = _softplus(dt_ref[rows, :] + dtb_ref[...])
            dts[rows, :] = dtv
            for d in (0, 1):
                dtxs[rows, 256 * d:256 * d + 256] = _xp_mm(dtv, per_dir[d][2])
            dact[rows, :] = jnp.zeros((rb, SSD_GW), F32)
            ddts[rows, :] = jnp.zeros((rb, 128), F32)

        _for_rows(t, rb, prologue)
        al = al_ref[...]
        def chunk(ci, carry):
            dal_c = carry[2]
            dhs_out = []
            for d, dh, hs_ref in ((0, carry[0], hsf_ref), (1, carry[1], hsb_ref)):
                c = ci if d else (nc - 1 - ci)
                r0 = pl.multiple_of(c * SSD_L, SSD_L)
                a = _silu(pre[pl.ds(r0, SSD_L), :])
                dtc = dts[pl.ds(r0, SSD_L), :]
                dtx = dtxs[pl.ds(r0, SSD_L), 256 * d:256 * d + 256]
                if need_y:
                    fn = lambda x_, bm_, cm_, dt_, dx_, al_, hs_: _ssd_chunk(x_, bm_, cm_, dt_, dx_, al_, hs_, per_dir[d],
                                                                             hmasks, d)
                    _, vjp = jax.vjp(fn, a[:, 0:256], a[:, 256:384], a[:, 384:512], dtc, dtx, al, hs_ref[c])
                    dx, dbm, dcm, ddtc, ddtx, dal_k, dhs = vjp((dy_ref[pl.ds(r0, SSD_L), :], dh))
                    dact[pl.ds(r0, SSD_L), 384:512] += dcm
                else:
                    fn = lambda x_, bm_, dt_, dx_, al_, hs_: _ssd_chunk(x_, bm_, None, dt_, dx_, al_, hs_, per_dir[d],
                                                                        hmasks, d)
                    _, vjp = jax.vjp(fn, a[:, 0:256], a[:, 256:384], dtc, dtx, al, hs_ref[c])
                    dx, dbm, ddtc, ddtx, dal_k, dhs = vjp(dh)
                dact[pl.ds(r0, SSD_L), 0:256] += dx
                dact[pl.ds(r0, SSD_L), 256:384] += dbm
                ddts[pl.ds(r0, SSD_L), :] += ddtc + _dot_nt(ddtx, per_dir[d][2])
                dhs_out.append(dhs)
                dal_c = dal_c + dal_k
            return dhs_out[0], dhs_out[1], dal_c

        dhf_ref[...], dhb_ref[...], dal_acc = _loop_unrolled(
            nc, unroll, chunk, (dsf_ref[...], dsb_ref[...], jnp.zeros((1, 128), F32)))

        def epilogue(r0):
            rows = pl.ds(r0, rb)
            prev = pre[rows, :]
            if need_y:
                dyv = dy_ref[rows, :]
                dact[rows, 0:256] += d_ref[...] * dyv
                dd_ref[g] += _colsum(dyv * _silu(prev[:, 0:256]))
            dpre = dact[rows, :] * _dsilu(prev)
            draw, dcw, dcb = _conv_bwd(dpre, p_ref[rows, :], cw_ref[...], masks)
            dp_ref[rows, :] = draw.astype(BF16)
            dcw_ref[g] += dcw
            dcb_ref[g] += dcb
            ddraw = ddts[rows, :] * _sigmoid(dt_ref[rows, :] + dtb_ref[...])
            ddt_ref[rows, :] = ddraw
            ddtb_ref[g] += _colsum(ddraw)

        _for_rows(t, rb, epilogue)
        dal_ref[g] += dal_acc

    gspec = lambda shp: BS((None,) + shp, lambda b, g: (g,) + (0,) * len(shp))
    full = lambda shp: BS(shp, lambda b, g: (0,) * len(shp))
    st_spec = BS((None, None, SSD_N, 256), lambda b, g: (b, g, 0, 0))
    hs_spec = BS((None, None, nc, SSD_N, 256), lambda b, g: (b, g, 0, 0, 0))
    p_spec = BS((t, SSD_GW), lambda b, g: (blk0 + b, g))
    in_specs = [p_spec, BS((None, t, 128), lambda b, g: (g, blk0 + b, 0)),
                gspec((4, SSD_GW)), gspec((1, SSD_GW)), gspec((1, 128)), gspec((1, 128)), gspec((1, 256)),
                hs_spec, hs_spec]
    args = [proj, dtg, cw, cb, dtb, alog, drow, hsf, hsb]
    if need_y:
        in_specs.append(BS((t, 256), lambda b, g: (b, g)))
        args.append(dy)
    in_specs += [st_spec, st_spec, BS(memory_space=pl.ANY)]
    args += [dsf, dsb, dproj]
    out_specs = [p_spec, BS((None, t, 128), lambda b, g: (g, b, 0)), st_spec, st_spec,
                 full((SSD_G, 4, SSD_GW)), full((SSD_G, 1, SSD_GW)), full((SSD_G, 1, 128)), full((SSD_G, 1, 128)),
                 full((SSD_G, 1, 256))]
    out_shape = [S(dproj.shape, BF16), S((SSD_G, nb * t, 128), F32),
                 S((nb, SSD_G, SSD_N, 256), F32), S((nb, SSD_G, SSD_N, 256), F32),
                 S((SSD_G, 4, SSD_GW), F32), S((SSD_G, 1, SSD_GW), F32), S((SSD_G, 1, 128), F32),
                 S((SSD_G, 1, 128), F32), S((SSD_G, 1, 256), F32)]
    return pl.pallas_call(
        body, grid=(nb, SSD_G), name="ssd_bwd_lat" if need_y else "ssd_bwd_ctx",
        in_specs=in_specs, out_specs=out_specs, out_shape=out_shape,
        input_output_aliases={len(args) - 1: 0},
        scratch_shapes=[pltpu.VMEM((t, SSD_GW), F32), pltpu.VMEM((t, SSD_GW), F32), pltpu.VMEM((t, 128), F32),
                        pltpu.VMEM((t, 128), F32), pltpu.VMEM((t, SSD_GW), F32)],
        compiler_params=_params(),
    )(*args)


def _lru_gate(u, wa, ba, wi, bi, lam):
    r = _sigmoid(_mm(u, wa) + ba)
    i = _sigmoid(_mm(u, wi) + bi)
    log_a = -LRU_C * r * _softplus(-lam)
    a = jnp.exp(log_a)
    x2 = 2.0 * log_a
    em1 = jnp.where(x2 > -0.01, x2 * (1.0 + x2 * (0.5 + x2 * (1.0 / 6.0 + x2 * (1.0 / 24.0)))), jnp.exp(x2) - 1.0)
    return a, jnp.sqrt(-em1) * (i * u)


def _scan_pair(fwd, rev, nblk, width):
    row = lax.broadcasted_iota(jnp.int32, (8, width), 0)

    def block(a_ref, b_ref, h_ref, st, carry, reverse):
        av, bv = a_ref[pl.ds(st, 8), :], b_ref[pl.ds(st, 8), :]
        for s in (1, 2, 4):
            ok = (row < 8 - s) if reverse else (row >= s)
            sh = (8 - s) if reverse else s
            a_sh = jnp.where(ok, pltpu.roll(av, sh, 0), 1.0)
            b_sh = jnp.where(ok, pltpu.roll(bv, sh, 0), 0.0)
            bv = av * b_sh + bv
            av = av * a_sh
        h = bv + av * carry
        h_ref[pl.ds(st, 8), :] = h
        return h[0:1, :] if reverse else h[7:8, :]

    def step(i, carry):
        cf, cr = carry
        cf = block(fwd[0], fwd[1], fwd[2], pl.multiple_of(i * 8, 8), cf, False)
        cr = block(rev[0], rev[1], rev[2], pl.multiple_of((nblk - 1 - i) * 8, 8), cr, True)
        return cf, cr

    return lax.fori_loop(0, nblk, step, (fwd[3], rev[3]))


def _lru_specs(t, blk0):
    p_spec = BS((t, LRU_CB), lambda b, q: (blk0 + b, P_LRU // LRU_CB + q))
    w_spec = BS((2, 2, 128, 128), lambda b, q: (0, q, 0, 0))
    v_spec = BS((2, LRU_CB), lambda b, q: (0, q))
    c_spec = lambda r: BS((r, LRU_CB), lambda b, q: (0, q))
    s_spec = BS((None, 2, LRU_CB), lambda b, q: (b, 0, q))
    return p_spec, w_spec, v_spec, c_spec, s_spec


def _lru_fwd(proj, cw, cb, wa, ba, wi, bi, lam, h0, nb, t, period, blk0, need_y):
    nq = D // LRU_CB
    masks_of = _conv_taps(LRU_CB, period)

    def body(p_ref, cw_ref, cb_ref, wa_ref, ba_ref, wi_ref, bi_ref, lam_ref, h0_ref, *rest):
        if need_y:
            y_ref, hf_ref, hb_ref, fin_ref, sa0, sb0, sa1, sb1 = rest
        else:
            hf_ref, hb_ref, fin_ref, sa0, sb0, sa1, sb1 = rest
        u = _conv_fwd(p_ref[...], cw_ref[...], cb_ref[...], masks_of(t))
        for d, (sa, sb) in enumerate(((sa0, sb0), (sa1, sb1))):
            for j in range(2):
                sl = slice(128 * j, 128 * j + 128)
                a, bb = _lru_gate(u[:, sl], wa_ref[d, j], ba_ref[d:d + 1, sl], wi_ref[d, j], bi_ref[d:d + 1, sl],
                                  lam_ref[d:d + 1, sl])
                sa[:, sl] = a
                sb[:, sl] = bb
        lf, lb = _scan_pair((sa0, sb0, hf_ref, h0_ref[0:1, :]), (sa1, sb1, hb_ref, h0_ref[1:2, :]), t // 8, LRU_CB)
        fin_ref[0:1, :] = lf
        fin_ref[1:2, :] = lb
        if need_y:
            y_ref[...] = hf_ref[...] + hb_ref[...]

    p_spec, w_spec, v_spec, c_spec, s_spec = _lru_specs(t, blk0)
    o_spec = BS((t, LRU_CB), lambda b, q: (b, q))
    out_specs = [o_spec, o_spec, s_spec]
    out_shape = [S((nb * t, D), F32), S((nb * t, D), F32), S((nb, 2, D), F32)]
    if need_y:
        out_specs = [o_spec] + out_specs
        out_shape = [S((nb * t, D), F32)] + out_shape
    return pl.pallas_call(
        body, grid=(nb, nq), name="lru_fwd_lat" if need_y else "lru_fwd_ctx",
        in_specs=[p_spec, c_spec(4), c_spec(1), w_spec, v_spec, w_spec, v_spec, v_spec, s_spec],
        out_specs=out_specs, out_shape=out_shape,
        scratch_shapes=[pltpu.VMEM((t, LRU_CB), F32)] * 4, compiler_params=_params(),
    )(proj, cw, cb, wa, ba, wi, bi, lam, h0)


def _lru_bwd(proj, cw, cb, wa, ba, wi, bi, lam, h0, hf, hb, dy, dfin, dproj, nb, t, period, blk0, need_y):
    nq = D // LRU_CB
    rc = min(256, t)
    masks_of = _conv_taps(LRU_CB, period)

    def body(*refs):
        if need_y:
            (p_ref, cw_ref, cb_ref, wa_ref, ba_ref, wi_ref, bi_ref, lam_ref, h0_ref, hf_ref, hb_ref, dy_ref, dfin_ref, _,
             dp_ref, dh0_ref, dcw_ref, dcb_ref, dwa_ref, dwi_ref, dba_ref, dbi_ref, dlam_ref,
             su, sa0, sa1, sc0, sc1, sg0, sg1) = refs
        else:
            (p_ref, cw_ref, cb_ref, wa_ref, ba_ref, wi_ref, bi_ref, lam_ref, h0_ref, hf_ref, hb_ref, dfin_ref, _,
             dp_ref, dh0_ref, dcw_ref, dcb_ref, dwa_ref, dwi_ref, dba_ref, dbi_ref, dlam_ref,
             su, sa0, sa1, sc0, sc1, sg0, sg1) = refs
            dy_ref = None
        b, q = pl.program_id(0), pl.program_id(1)

        @pl.when(jnp.logical_and(b == 0, q == 0))
        def _():
            for r in (dcw_ref, dcb_ref, dwa_ref, dwi_ref, dba_ref, dbi_ref, dlam_ref):
                r[...] = jnp.zeros_like(r)

        masks = masks_of(t)
        u = _conv_fwd(p_ref[...], cw_ref[...], cb_ref[...], masks)
        su[...] = u
        for d, sa in enumerate((sa0, sa1)):
            for j in range(2):
                sl = slice(128 * j, 128 * j + 128)
                a, _unused = _lru_gate(u[:, sl], wa_ref[d, j], ba_ref[d:d + 1, sl], wi_ref[d, j], bi_ref[d:d + 1, sl],
                                       lam_ref[d:d + 1, sl])
                sa[:, sl] = a
        rowi = lax.broadcasted_iota(jnp.int32, (t, LRU_CB), 0)
        last, first = rowi == t - 1, rowi == 0
        sc0[...] = jnp.where(last, 0.0, pltpu.roll(sa0[...], t - 1, 0))
        sc1[...] = jnp.where(first, 0.0, pltpu.roll(sa1[...], 1, 0))
        g0 = jnp.where(last, dfin_ref[0:1, :], 0.0)
        g1 = jnp.where(first, dfin_ref[1:2, :], 0.0)
        if need_y:
            g0 = g0 + dy_ref[...]
            g1 = g1 + dy_ref[...]
        sg0[...] = g0
        sg1[...] = g1
        zero = jnp.zeros((1, LRU_CB), F32)
        _scan_pair((sc1, sg1, sg1, zero), (sc0, sg0, sg0, zero), t // 8, LRU_CB)
        dh0_ref[0:1, :] = sa0[0:1, :] * sg0[0:1, :]
        dh0_ref[1:2, :] = sa1[t - 1:t, :] * sg1[t - 1:t, :]
        sc0[...] = sg0[...] * jnp.where(first, h0_ref[0:1, :], pltpu.roll(hf_ref[...], 1, 0))
        sc1[...] = sg1[...] * jnp.where(last, h0_ref[1:2, :], pltpu.roll(hb_ref[...], t - 1, 0))

        def rows(ci, carry):
            r0 = pl.multiple_of(ci * rc, rc)
            for j in range(2):
                sl = slice(128 * j, 128 * j + 128)
                du = jnp.zeros((rc, 128), F32)
                for d, (sc, sg) in enumerate(((sc0, sg0), (sc1, sg1))):
                    _, vjp = jax.vjp(_lru_gate, su[pl.ds(r0, rc), sl], wa_ref[d, j], ba_ref[d:d + 1, sl], wi_ref[d, j],
                                     bi_ref[d:d + 1, sl], lam_ref[d:d + 1, sl])
                    du_d, dwa, dba, dwi, dbi, dlam = vjp((sc[pl.ds(r0, rc), sl], sg[pl.ds(r0, rc), sl]))
                    du = du + du_d
                    dwa_ref[d, 2 * q + j] += dwa
                    dwi_ref[d, 2 * q + j] += dwi
                    dba_ref[q, d:d + 1, sl] += dba
                    dbi_ref[q, d:d + 1, sl] += dbi
                    dlam_ref[q, d:d + 1, sl] += dlam
                sa0[pl.ds(r0, rc), sl] = du
            return carry

        lax.fori_loop(0, t // rc, rows, 0)
        draw, dcw, dcb = _conv_bwd(sa0[...], p_ref[...], cw_ref[...], masks)
        dp_ref[...] = draw.astype(BF16)
        dcw_ref[q] += dcw
        dcb_ref[q] += dcb

    p_spec, w_spec, v_spec, c_spec, s_spec = _lru_specs(t, blk0)
    o_spec = BS((t, LRU_CB), lambda b, q: (b, q))
    full = lambda shp: BS(shp, lambda b, q: (0,) * len(shp))
    in_specs = [p_spec, c_spec(4), c_spec(1), w_spec, v_spec, w_spec, v_spec, v_spec, s_spec, o_spec, o_spec]
    args = [proj, cw, cb, wa, ba, wi, bi, lam, h0, hf, hb]
    if need_y:
        in_specs.append(o_spec)
        args.append(dy)
    in_specs += [s_spec, BS(memory_space=pl.ANY)]
    args += [dfin, dproj]
    out_specs = [p_spec, s_spec, full((nq, 4, LRU_CB)), full((nq, 1, LRU_CB)), full((2, 8, 128, 128)),
                 full((2, 8, 128, 128)), full((nq, 2, LRU_CB)), full((nq, 2, LRU_CB)), full((nq, 2, LRU_CB))]
    out_shape = [S(dproj.shape, BF16), S((nb, 2, D), F32), S((nq, 4, LRU_CB), F32), S((nq, 1, LRU_CB), F32),
                 S((2, 8, 128, 128), F32), S((2, 8, 128, 128), F32), S((nq, 2, LRU_CB), F32), S((nq, 2, LRU_CB), F32),
                 S((nq, 2, LRU_CB), F32)]
    return pl.pallas_call(
        body, grid=(nb, nq), name="lru_bwd_lat" if need_y else "lru_bwd_ctx",
        in_specs=in_specs, out_specs=out_specs, out_shape=out_shape, input_output_aliases={len(args) - 1: 0},
        scratch_shapes=[pltpu.VMEM((t, LRU_CB), F32)] * 7, compiler_params=_params(),
    )(*args)


def _mix_core(y_ref, yl_ref, p_ref, nw_ref, bg_ref, wbs_ref, wbl_ref, wo_ref, nrm_s):
    for g in range(SSD_G):
        sl = slice(256 * g, 256 * g + 256)
        nrm_s[:, sl] = _grms(y_ref[:, sl], p_ref[:, sl], nw_ref[:, sl]).astype(BF16)
    br_s = jnp.dot(nrm_s[...], wbs_ref[...], preferred_element_type=F32)
    gl = (yl_ref[...] * _gelu(p_ref[:, 2048:3072])).astype(BF16)
    br_l = jnp.dot(gl, wbl_ref[...], preferred_element_type=F32)
    gs = _sigmoid(p_ref[:, 3072:4096] + bg_ref[:, 0:D])
    gr = _sigmoid(p_ref[:, 4096:5120] + bg_ref[:, D:2 * D])
    mix = (gs * br_s + gr * br_l).astype(BF16)
    xmix = jnp.dot(mix, wo_ref[...], preferred_element_type=F32)
    return br_s, gl, br_l, gs, gr, mix, xmix


def _mix_specs(rt, tiles_per_b):
    row = lambda w: BS((rt, w), lambda i: (i, 0))
    const = lambda shp: BS(shp, lambda i: (0,) * len(shp))
    gate = BS((None, None, 1, D), lambda i: (i // tiles_per_b, 2, 0, 0))
    return row, const, gate


def _mix_fwd(y, ylru, proj, x, m4, wbs, wbl, wo, nw, bg, l1g, l1b, rt, tiles_per_b):
    n = x.shape[0]

    def body(y_ref, yl_ref, p_ref, x_ref, g1_ref, wbs_ref, wbl_ref, wo_ref, nw_ref, bg_ref, lg_ref, lb_ref, x1_ref, nrm_s):
        xmix = _mix_core(y_ref, yl_ref, p_ref, nw_ref, bg_ref, wbs_ref, wbl_ref, wo_ref, nrm_s)[6]
        x1_ref[...] = _resln(x_ref[...], xmix, g1_ref[...], lg_ref[...], lb_ref[...])

    row, const, gate = _mix_specs(rt, tiles_per_b)
    return pl.pallas_call(
        body, grid=(n // rt,), name="mix_fwd",
        in_specs=[row(SSD_INNER), row(D), BS((rt, 5120), lambda i: (i, 1)), row(D), gate,
                  const((SSD_INNER, D)), const((D, D)), const((D, D)), const((1, SSD_INNER)), const((1, 2 * D)),
                  const((1, D)), const((1, D))],
        out_specs=row(D), out_shape=S((n, D), F32),
        scratch_shapes=[pltpu.VMEM((rt, SSD_INNER), BF16)], compiler_params=_params(),
    )(y, ylru, proj, x, m4, wbs, wbl, wo, nw, bg, l1g, l1b)


def _mix_bwd(y, ylru, proj, x, m4, wbs, wbl, wo, nw, bg, l1g, l1b, dx1, dproj, rt, tiles_per_b):
    n = x.shape[0]

    def body(y_ref, yl_ref, p_ref, x_ref, g1_ref, wbs_ref, wbl_ref, wo_ref, nw_ref, bg_ref, lg_ref, lb_ref, dx1_ref, _,
             dp_ref, dy_ref, dyl_ref, dxr_ref, nrm_ref, dbrs_ref, gl_ref, dbrl_ref, mix_ref, dxm_ref,
             dg1_ref, dnw_ref, dbg_ref, dlg_ref, dlb_ref, nrm_s):
        i = pl.program_id(0)

        @pl.when(i == 0)
        def _():
            for r in (dnw_ref, dbg_ref, dlg_ref, dlb_ref):
                r[...] = jnp.zeros_like(r)

        @pl.when(i % tiles_per_b == 0)
        def _():
            dg1_ref[...] = jnp.zeros_like(dg1_ref)

        br_s, gl, br_l, gs, gr, mix, xmix = _mix_core(y_ref, yl_ref, p_ref, nw_ref, bg_ref, wbs_ref, wbl_ref, wo_ref,
                                                      nrm_s)
        _, vjp = jax.vjp(_resln, x_ref[...], xmix, g1_ref[...], lg_ref[...], lb_ref[...])
        dxr, dxmix, dg1, dlg, dlb = vjp(dx1_ref[...])
        dxr_ref[...] = dxr
        dg1_ref[...] += dg1
        dlg_ref[...] += dlg
        dlb_ref[...] += dlb
        dxmb = dxmix.astype(BF16)
        dxm_ref[...] = dxmb
        mix_ref[...] = mix
        nrm_ref[...] = nrm_s[...]
        gl_ref[...] = gl
        dmix = lax.dot_general(dxmb, wo_ref[...], (((1,), (1,)), ((), ())), preferred_element_type=F32)
        dbrs = (dmix * gs).astype(BF16)
        dbrl = (dmix * gr).astype(BF16)
        dbrs_ref[...] = dbrs
        dbrl_ref[...] = dbrl
        dmg_s = dmix * br_s * gs * (1.0 - gs)
        dmg_r = dmix * br_l * gr * (1.0 - gr)
        dp_ref[:, 3072:4096] = dmg_s.astype(BF16)
        dp_ref[:, 4096:5120] = dmg_r.astype(BF16)
        dbg_ref[:, 0:D] += _colsum(dmg_s)
        dbg_ref[:, D:2 * D] += _colsum(dmg_r)
        dnrm = lax.dot_general(dbrs, wbs_ref[...], (((1,), (1,)), ((), ())), preferred_element_type=F32)
        for g in range(SSD_G):
            sl = slice(256 * g, 256 * g + 256)
            _, vjp = jax.vjp(_grms, y_ref[:, sl], p_ref[:, sl], nw_ref[:, sl])
            dyg, dzg, dnwg = vjp(dnrm[:, sl])
            dy_ref[:, sl] = dyg
            dp_ref[:, sl] = dzg.astype(BF16)
            dnw_ref[:, sl] += dnwg
        dgl = lax.dot_general(dbrl, wbl_ref[...], (((1,), (1,)), ((), ())), preferred_element_type=F32)
        _, vjp = jax.vjp(lambda a, c: a * _gelu(c), yl_ref[...], p_ref[:, 2048:3072])
        dyl, dlgate = vjp(dgl)
        dyl_ref[...] = dyl
        dp_ref[:, 2048:3072] = dlgate.astype(BF16)

    row, const, gate = _mix_specs(rt, tiles_per_b)
    pblk = BS((rt, 5120), lambda i: (i, 1))
    nb = n // (rt * tiles_per_b)
    out_specs = [pblk, row(SSD_INNER), row(D), row(D), row(SSD_INNER), row(D), row(D), row(D), row(D), row(D),
                 BS((None, 1, D), lambda i: (i // tiles_per_b, 0, 0)), const((1, SSD_INNER)), const((1, 2 * D)),
                 const((1, D)), const((1, D))]
    out_shape = [S(dproj.shape, BF16), S((n, SSD_INNER), F32), S((n, D), F32), S((n, D), F32),
                 S((n, SSD_INNER), BF16), S((n, D), BF16), S((n, D), BF16), S((n, D), BF16), S((n, D), BF16),
                 S((n, D), BF16), S((nb, 1, D), F32), S((1, SSD_INNER), F32), S((1, 2 * D), F32), S((1, D), F32),
                 S((1, D), F32)]
    return pl.pallas_call(
        body, grid=(n // rt,), name="mix_bwd",
        in_specs=[row(SSD_INNER), row(D), pblk, row(D), gate,
                  const((SSD_INNER, D)), const((D, D)), const((D, D)), const((1, SSD_INNER)), const((1, 2 * D)),
                  const((1, D)), const((1, D)), row(D), BS(memory_space=pl.ANY)],
        out_specs=out_specs, out_shape=out_shape, input_output_aliases={13: 0},
        scratch_shapes=[pltpu.VMEM((rt, SSD_INNER), BF16)], compiler_params=_params(),
    )(y, ylru, proj, x, m4, wbs, wbl, wo, nw, bg, l1g, l1b, dx1, dproj)


def _mlp_step(x1, tgt, m4, w1, b1, w2, b2, l2g, l2b, rt, tiles_per_b):
    n = x1.shape[0]

    def body(x_ref, t_ref, sh_ref, sc_ref, gt_ref, w1_hbm, b1_ref, w2_hbm, b2_ref, lg_ref, lb_ref,
             loss_ref, dx_ref, h2_ref, da1_ref, r2_ref, dmlp_ref, dm_ref, db1_ref, db2_ref, dlg_ref, dlb_ref,
             w1_vm, w2_vm, sem):
        i = pl.program_id(0)

        @pl.when(i == 0)
        def _():
            c1 = pltpu.make_async_copy(w1_hbm, w1_vm, sem.at[0])
            c2 = pltpu.make_async_copy(w2_hbm, w2_vm, sem.at[1])
            c1.start()
            c2.start()
            for r in (loss_ref, db1_ref, db2_ref, dlg_ref, dlb_ref):
                r[...] = jnp.zeros_like(r)
            c1.wait()
            c2.wait()

        @pl.when(i % tiles_per_b == 0)
        def _():
            dm_ref[...] = jnp.zeros_like(dm_ref)

        x1v = x_ref[...]
        h2, vjp_h = jax.vjp(_modln, x1v, sh_ref[...], sc_ref[...])
        h2b = h2.astype(BF16)
        h2_ref[...] = h2b
        r = jnp.maximum(jnp.dot(h2b, w1_vm[...], preferred_element_type=F32) + b1_ref[...], 0.0)
        r2b = (r * r).astype(BF16)
        r2_ref[...] = r2b
        mlp = jnp.dot(r2b, w2_vm[...], preferred_element_type=F32) + b2_ref[...]
        x2, vjp_r = jax.vjp(_resln, x1v, mlp, gt_ref[...], lg_ref[...], lb_ref[...])
        diff = x2 - t_ref[...]
        loss_ref[...] += (0.5 / D) * jnp.sum(diff * diff)
        dxa, dmlp, dgt, dlg, dlb = vjp_r(diff * (1.0 / D))
        dlg_ref[...] += dlg
        dlb_ref[...] += dlb
        dm_ref[2:3, :] += dgt
        db2_ref[...] += _colsum(dmlp)
        dmlpb = dmlp.astype(BF16)
        dmlp_ref[...] = dmlpb
        da1 = lax.dot_general(dmlpb, w2_vm[...], (((1,), (1,)), ((), ())), preferred_element_type=F32) * (2.0 * r)
        db1_ref[...] += _colsum(da1)
        da1b = da1.astype(BF16)
        da1_ref[...] = da1b
        dh2 = lax.dot_general(da1b, w1_vm[...], (((1,), (1,)), ((), ())), preferred_element_type=F32)
        dxb, dsh, dsc = vjp_h(dh2)
        dx_ref[...] = dxa + dxb
        dm_ref[0:1, :] += dsh
        dm_ref[1:2, :] += dsc

    row = lambda w: BS((rt, w), lambda i: (i, 0))
    const = lambda shp: BS(shp, lambda i: (0,) * len(shp))
    mod = lambda k: BS((None, None, 1, D), lambda i: (i // tiles_per_b, k, 0, 0))
    nb = n // (rt * tiles_per_b)
    anyspec = BS(memory_space=pl.ANY)
    return pl.pallas_call(
        body, grid=(n // rt,), name="mlp_step",
        in_specs=[row(D), row(D), mod(3), mod(4), mod(5), anyspec, const((1, MLP_H)), anyspec, const((1, D)),
                  const((1, D)), const((1, D))],
        out_specs=[const((8, 128)), row(D), row(D), row(MLP_H), row(MLP_H), row(D),
                   BS((None, 3, D), lambda i: (i // tiles_per_b, 0, 0)), const((1, MLP_H)), const((1, D)),
                   const((1, D)), const((1, D))],
        out_shape=[S((8, 128), F32), S((n, D), F32), S((n, D), BF16), S((n, MLP_H), BF16), S((n, MLP_H), BF16),
                   S((n, D), BF16), S((nb, 3, D), F32), S((1, MLP_H), F32), S((1, D), F32), S((1, D), F32),
                   S((1, D), F32)],
        scratch_shapes=[pltpu.VMEM((D, MLP_H), BF16), pltpu.VMEM((MLP_H, D), BF16), pltpu.SemaphoreType.DMA((2,))],
        compiler_params=_params(),
    )(x1, tgt, m4, m4, m4, w1, b1, w2, b2, l2g, l2b)


def _pack_win(w):
    parts = []
    for g in range(SSD_G):
        parts += [w[:, 256 * g:256 * g + 256], w[:, 2048 + 128 * g:2176 + 128 * g], w[:, 4160 + 128 * g:4288 + 128 * g]]
    parts += [w[:, 3136:4160], w[:, 5184:7232], w[:, 7232:8256], w[:, 8256:10304], w[:, 3072:3136],
              jnp.zeros((w.shape[0], P_W - P_DT - 64), w.dtype)]
    return jnp.concatenate(parts, axis=1)


def _unpack_win(p):
    xs = [p[:, 512 * g:512 * g + 256] for g in range(SSD_G)]
    bs = [p[:, 512 * g + 256:512 * g + 384] for g in range(SSD_G)]
    cs = [p[:, 512 * g + 384:512 * g + 512] for g in range(SSD_G)]
    return jnp.concatenate(xs + bs + [p[:, P_DT:P_DT + 64], p[:, P_LRU:P_Z]] + cs + [p[:, P_Z:P_DT]], axis=1)


def _pack_conv(w):
    return jnp.stack([jnp.concatenate([w[:, 256 * g:256 * g + 256], w[:, 2048 + 128 * g:2176 + 128 * g],
                                       w[:, 3072 + 128 * g:3200 + 128 * g]], axis=1) for g in range(SSD_G)])


def _unpack_conv(p):
    r = p.shape[1]
    x = jnp.transpose(p[:, :, 0:256], (1, 0, 2)).reshape(r, 2048)
    b = jnp.transpose(p[:, :, 256:384], (1, 0, 2)).reshape(r, 1024)
    c = jnp.transpose(p[:, :, 384:512], (1, 0, 2)).reshape(r, 1024)
    return jnp.concatenate([x, b, c], axis=1)


def _pack_heads(v):
    p = jnp.transpose(v.reshape(2, SSD_G, 4), (1, 0, 2)).reshape(SSD_G, 1, 8)
    return jnp.pad(p, ((0, 0), (0, 0), (0, 120)))


def _unpack_heads(p):
    return jnp.transpose(p[:, 0, 0:8].reshape(SSD_G, 2, 4), (1, 0, 2)).reshape(2, 32)


def _pack_dt(dt):
    n = dt.shape[0]
    p = jnp.transpose(dt.reshape(n, 2, SSD_G, 4), (2, 0, 1, 3)).reshape(SSD_G, n, 8)
    return jnp.pad(p, ((0, 0), (0, 0), (0, 120)))


def _unpack_dt(p):
    n = p.shape[1]
    return jnp.transpose(p[:, :, 0:8].reshape(SSD_G, n, 2, 4), (1, 2, 0, 3)).reshape(n, 64)


def _tk(rows):
    return 512 if rows % 512 == 0 else (256 if rows % 256 == 0 else 128)


def _local_step(x, c, ctx, tgt, sm, wmod, win, wbs, wbl, wo, w1, w2):
    nb, t, _ = x.shape
    tc = ctx.shape[1]
    nl, ncx = nb * t, nb * tc
    rt = 256 if tc % 256 == 0 else 128
    rtm = 128
    xa = jnp.concatenate([x.reshape(nl, D), ctx.reshape(ncx, D)], axis=0)
    tgt2 = tgt.reshape(nl, D)
    cc = jnp.zeros((8, D), F32).at[0:nb].set(c).at[nb].set(sm["c_ctx"])
    m = _mod_fwd(cc, wmod, sm["b_mod"])
    m4 = m.reshape(8, N_MOD, 1, D)
    proj, h1 = _inproj_fwd(xa, m4, win, rtm, nl // rtm, t // rtm, nb)

    cw_s, cb_s = _pack_conv(sm["ssd_conv_w"]), _pack_conv(sm["ssd_conv_b"])
    dtb, alog = _pack_heads(sm["ssd_dt_bias"]), _pack_heads(sm["ssd_a_log"])
    drow = jnp.repeat(sm["ssd_d"].reshape(32), 64).reshape(SSD_G, 1, 256)
    dtg = _pack_dt(proj[:, P_DT:P_DT + 64])
    zst = jnp.zeros((nb, SSD_G, SSD_N, 256), F32)
    zl = jnp.zeros((nb, 2, D), F32)
    ssd_p = (cw_s, cb_s, dtb, alog, drow)
    lru_p = (sm["lru_conv_w"], sm["lru_conv_b"], sm["lru_wa"], sm["lru_ba"], sm["lru_wi"], sm["lru_bi"], sm["lru_lambda"])

    chsf, chsb, csf, csb = _ssd_fwd(proj, dtg, *ssd_p, zst, zst, nb, tc, tc, nl // tc, False)
    y, lhsf, lhsb, _, _ = _ssd_fwd(proj, dtg, *ssd_p, csf, csb, nb, t, GRID_W, 0, True)
    chf, chb, cfin = _lru_fwd(proj, *lru_p, zl, nb, tc, tc, nl // tc, False)
    ylru, lhf, lhb, _ = _lru_fwd(proj, *lru_p, cfin, nb, t, GRID_W, 0, True)
    xl = xa[0:nl]
    mix_w = (wbs, wbl, wo, sm["ssd_norm_w"], sm["b_gate"], sm["ln1_g"], sm["ln1_b"])
    x1 = _mix_fwd(y, ylru, proj, xl, m4, *mix_w, rtm, t // rtm)
    (loss, dx1, h2, da1, r2, dmlp, dm2, db1, db2, dl2g, dl2b) = _mlp_step(
        x1, tgt2, m4, w1, sm["b_mlp1"], w2, sm["b_mlp2"], sm["ln2_g"], sm["ln2_b"], rt, t // rt)

    dproj = jnp.zeros((nl + ncx, P_W), BF16)
    (dproj, dy, dylru, dxres, nrm, dbrs, gl, dbrl, mixb, dxm, dg1, dnw, dbg, dl1g, dl1b) = _mix_bwd(
        y, ylru, proj, xl, m4, *mix_w, dx1, dproj, rtm, t // rtm)
    (dproj, ddt_l, dh0f, dh0b, dcw_l, dcb_l, ddtb_l, dal_l, dd) = _ssd_bwd(
        proj, dtg, *ssd_p, lhsf, lhsb, dy, zst, zst, dproj, nb, t, GRID_W, 0, True)
    (dproj, ddt_c, _, _, dcw_c, dcb_c, ddtb_c, dal_c, _) = _ssd_bwd(
        proj, dtg, *ssd_p, chsf, chsb, None, dh0f, dh0b, dproj, nb, tc, tc, nl // tc, False)
    (dproj, dlh0, gcw_l, gcb_l, gwa_l, gwi_l, gba_l, gbi_l, glam_l) = _lru_bwd(
        proj, *lru_p, cfin, lhf, lhb, dylru, zl, dproj, nb, t, GRID_W, 0, True)
    (dproj, _, gcw_c, gcb_c, gwa_c, gwi_c, gba_c, gbi_c, glam_c) = _lru_bwd(
        proj, *lru_p, zl, chf, chb, None, dlh0, dproj, nb, tc, tc, nl // tc, False)
    ddt = jnp.concatenate([_unpack_dt(ddt_l), _unpack_dt(ddt_c)], axis=0).astype(BF16)
    dproj = lax.dynamic_update_slice(dproj, ddt, (0, P_DT))

    gx, dm1 = _inproj_bwd(xa, m4, win, dproj, rt, 0, nl // rt, t // rt, nb, True, dxres)
    (dmc,) = _inproj_bwd(xa, m4, win, dproj, rt, nl // rt, ncx // rt, ncx // rt, nb, False, None)
    dm = jnp.zeros((8, N_MOD, D), F32)
    dm = dm.at[0:nb].set(jnp.concatenate([dm1, dg1, dm2], axis=1)).at[nb, 0:2].set(dmc[0])
    dwmod, dbmod, dcc = _mod_bwd(cc, wmod, dm.reshape(8, N_MOD * D))

    big = {
        "w_mod": dwmod,
        "w_in": _matmul_tn(h1, dproj, D, 1152, _tk(nl + ncx), "dw_in"),
        "w_br_ssd": _matmul_tn(nrm, dbrs, D, D, _tk(nl), "dw_br_ssd"),
        "w_br_lru": _matmul_tn(gl, dbrl, D, D, _tk(nl), "dw_br_lru"),
        "w_out": _matmul_tn(mixb, dxm, D, D, _tk(nl), "dw_out"),
        "w_mlp1": _matmul_tn(h2, da1, D, D, _tk(nl), "dw_mlp1"),
        "w_mlp2": _matmul_tn(r2, dmlp, D, D, _tk(nl), "dw_mlp2"),
    }
    nq = D // LRU_CB
    small = {
        "c_ctx": dcc[nb],
        "b_mod": dbmod,
        "b_gate": dbg,
        "ssd_conv_w": _unpack_conv(dcw_l + dcw_c),
        "ssd_conv_b": _unpack_conv(dcb_l + dcb_c),
        "ssd_dt_bias": _unpack_heads(ddtb_l + ddtb_c),
        "ssd_a_log": _unpack_heads(dal_l + dal_c),
        "ssd_d": jnp.sum(dd.reshape(32, 64), axis=1),
        "ssd_norm_w": dnw,
        "lru_conv_w": jnp.transpose(gcw_l + gcw_c, (1, 0, 2)).reshape(4, D),
        "lru_conv_b": (gcb_l + gcb_c).reshape(1, D),
        "lru_wa": gwa_l + gwa_c,
        "lru_ba": jnp.transpose(gba_l + gba_c, (1, 0, 2)).reshape(2, D),
        "lru_wi": gwi_l + gwi_c,
        "lru_bi": jnp.transpose(gbi_l + gbi_c, (1, 0, 2)).reshape(2, D),
        "lru_lambda": jnp.transpose(glam_l + glam_c, (1, 0, 2)).reshape(2, D),
        "ln1_g": dl1g, "ln1_b": dl1b, "b_mlp1": db1, "b_mlp2": db2, "ln2_g": dl2g, "ln2_b": dl2b,
    }
    return loss[0, 0], gx.reshape(nb, t, D), big, small


_HBM = BS(memory_space=pl.ANY)


def _place():
    return lax.axis_index("x"), lax.axis_index("y"), lax.axis_index("c")


def _other_chips(x, y):
    return [(1 - x, y), (x, 1 - y), (1 - x, 1 - y)]


def _gather_chips(arrs):
    n = len(arrs)

    def body(*refs):
        ins, outs = refs[:n], refs[n:2 * n]
        send_sems, recv_sems, loc_sems = refs[2 * n:]
        x, y, c = _place()
        me = 2 * x + y
        chips = _other_chips(x, y)
        sends = []
        for a in range(n):
            loc = pltpu.make_async_copy(ins[a], outs[a].at[me], loc_sems.at[a])
            loc.start()
            sends.append(loc)
            for k, (px, py) in enumerate(chips):
                cp = pltpu.make_async_remote_copy(src_ref=ins[a], dst_ref=outs[a].at[me], send_sem=send_sems.at[a, k],
                                                  recv_sem=recv_sems.at[a, k], device_id=(px, py, c), device_id_type=MESH)
                cp.start()
                sends.append(cp)
        for a in range(n):
            for k, (px, py) in enumerate(chips):
                pltpu.make_async_remote_copy(src_ref=ins[a], dst_ref=outs[a].at[2 * px + py], send_sem=send_sems.at[a, k],
                                             recv_sem=recv_sems.at[a, k], device_id=(px, py, c),
                                             device_id_type=MESH).wait_recv()
        for a in range(n):
            sends[4 * a].wait()
            for k in range(3):
                sends[4 * a + 1 + k].wait_send()

    return pl.pallas_call(
        body, name="gather_weights", in_specs=[_HBM] * n, out_specs=[_HBM] * n,
        out_shape=[S((4,) + a.shape, a.dtype) for a in arrs],
        scratch_shapes=[pltpu.SemaphoreType.DMA((n, 3)), pltpu.SemaphoreType.DMA((n, 3)), pltpu.SemaphoreType.DMA((n,))],
    )(*arrs)


def _scatter_chips(arrs):
    n = len(arrs)

    def body(*refs):
        ins, outs = refs[:n], refs[n:2 * n]
        send_sems, recv_sems = refs[2 * n:]
        x, y, c = _place()
        chips = _other_chips(x, y)
        sends = []
        for a in range(n):
            for k, (px, py) in enumerate(chips):
                cp = pltpu.make_async_remote_copy(src_ref=ins[a].at[2 * px + py], dst_ref=outs[a].at[k],
                                                  send_sem=send_sems.at[a, k], recv_sem=recv_sems.at[a, k],
                                                  device_id=(px, py, c), device_id_type=MESH)
                cp.start()
                sends.append(cp)
        for cp in sends:
            cp.wait_recv()
        for cp in sends:
            cp.wait_send()

    return pl.pallas_call(
        body, name="scatter_grads", in_specs=[_HBM] * n, out_specs=[_HBM] * n,
        out_shape=[S((3,) + a.shape[1:], a.dtype) for a in arrs],
        scratch_shapes=[pltpu.SemaphoreType.DMA((n, 3)), pltpu.SemaphoreType.DMA((n, 3))],
    )(*arrs)


def _swap_cores(arrs):
    n = len(arrs)

    def body(*refs):
        ins, outs = refs[:n], refs[n:2 * n]
        send_sems, recv_sems = refs[2 * n:]
        x, y, c = _place()
        sends = []
        for a in range(n):
            cp = pltpu.make_async_remote_copy(src_ref=ins[a], dst_ref=outs[a], send_sem=send_sems.at[a],
                                              recv_sem=recv_sems.at[a], device_id=(x, y, 1 - c), device_id_type=MESH)
            cp.start()
            sends.append(cp)
        for cp in sends:
            cp.wait_recv()
        for cp in sends:
            cp.wait_send()

    return pl.pallas_call(
        body, name="swap_cores", in_specs=[_HBM] * n, out_specs=[_HBM] * n,
        out_shape=[S(a.shape, a.dtype) for a in arrs],
        scratch_shapes=[pltpu.SemaphoreType.DMA((n,)), pltpu.SemaphoreType.DMA((n,))],
    )(*arrs)


def _gather_all(v):
    def body(in_ref, out_ref, send_sems, recv_sems, loc_sem):
        x, y, c = _place()
        me = 4 * x + 2 * y + c
        loc = pltpu.make_async_copy(in_ref, out_ref.at[me], loc_sem)
        loc.start()
        sends = []
        for k in range(1, 8):
            px, py, pc = x ^ (k >> 2), y ^ ((k >> 1) & 1), c ^ (k & 1)
            cp = pltpu.make_async_remote_copy(src_ref=in_ref, dst_ref=out_ref.at[me], send_sem=send_sems.at[k - 1],
                                              recv_sem=recv_sems.at[k - 1], device_id=(px, py, pc), device_id_type=MESH)
            cp.start()
            sends.append(cp)
        for k in range(1, 8):
            px, py, pc = x ^ (k >> 2), y ^ ((k >> 1) & 1), c ^ (k & 1)
            pltpu.make_async_remote_copy(src_ref=in_ref, dst_ref=out_ref.at[4 * px + 2 * py + pc],
                                         send_sem=send_sems.at[k - 1], recv_sem=recv_sems.at[k - 1],
                                         device_id=(px, py, pc), device_id_type=MESH).wait_recv()
        for cp in sends:
            cp.wait_send()
        loc.wait()

    return pl.pallas_call(
        body, name="gather_small", in_specs=[_HBM], out_specs=_HBM, out_shape=S((8,) + v.shape, v.dtype),
        scratch_shapes=[pltpu.SemaphoreType.DMA((7,)), pltpu.SemaphoreType.DMA((7,)), pltpu.SemaphoreType.DMA(())],
    )(v)


def _row_tile(r, c=128):
    tr = 256 if c <= 1024 else (128 if c <= 2048 else 64)
    return tr if r % tr == 0 else r


def _sum4(own, recv, name):
    r, c = own.shape
    tr = _row_tile(r, c)

    def body(o_ref, r_ref, out_ref):
        acc = o_ref[...]
        for k in range(3):
            acc = acc + r_ref[k].astype(F32)
        out_ref[...] = acc

    return pl.pallas_call(
        body, grid=(r // tr,), name=name,
        in_specs=[BS((tr, c), lambda i: (i, 0)), BS((3, tr, c), lambda i: (0, i, 0))],
        out_specs=BS((tr, c), lambda i: (i, 0)), out_shape=S((r, c), F32), compiler_params=_params(),
    )(own, recv)


def _adam_math(w, g, m, v):
    m = ADAM_B1 * m + (1.0 - ADAM_B1) * g
    v = ADAM_B2 * v + (1.0 - ADAM_B2) * (g * g)
    m_hat = m / (1.0 - ADAM_B1 ** ADAM_STEP)
    v_hat = v / (1.0 - ADAM_B2 ** ADAM_STEP)
    return -ADAM_LR * (m_hat / (jnp.sqrt(v_hat) + ADAM_EPS) + ADAM_WD * w), m, v


def _adam_pair(pa, pb, w, m, v, name):
    r, c = w.shape
    tr = _row_tile(r, c)

    def body(pa_ref, pb_ref, w_ref, m_ref, v_ref, g_ref, d_ref, nm_ref, nv_ref):
        g = pa_ref[...] + pb_ref[...]
        g_ref[...] = g
        d_ref[...], nm_ref[...], nv_ref[...] = _adam_math(w_ref[...], g, m_ref[...], v_ref[...])

    spec = BS((tr, c), lambda i: (i, 0))
    return pl.pallas_call(
        body, grid=(r // tr,), name=name, in_specs=[spec] * 5, out_specs=[spec] * 4, out_shape=[S((r, c), F32)] * 4,
        compiler_params=_params(),
    )(pa, pb, w, m, v)


def _sum8(parts):
    r = parts.shape[1]
    tr = _row_tile(r)

    def body(p_ref, out_ref):
        acc = p_ref[0]
        for k in range(1, 8):
            acc = acc + p_ref[k]
        out_ref[...] = acc

    return pl.pallas_call(
        body, grid=(r // tr,), name="sum_small", in_specs=[BS((8, tr, 128), lambda i: (0, i, 0))],
        out_specs=BS((tr, 128), lambda i: (i, 0)), out_shape=S((r, 128), F32), compiler_params=_params(),
    )(parts)


def _adam_flat(g, w, m, v):
    r = w.shape[0]
    tr = _row_tile(r)

    def body(g_ref, w_ref, m_ref, v_ref, d_ref, nm_ref, nv_ref):
        d_ref[...], nm_ref[...], nv_ref[...] = _adam_math(w_ref[...], g_ref[...], m_ref[...], v_ref[...])

    spec = BS((tr, 128), lambda i: (i, 0))
    return pl.pallas_call(
        body, grid=(r // tr,), name="adam_small", in_specs=[spec] * 4, out_specs=[spec] * 3,
        out_shape=[S((r, 128), F32)] * 3, compiler_params=_params(),
    )(g, w, m, v)


def _flatten(arrs, rows_mult=256):
    flat = jnp.concatenate([a.reshape(-1) for a in arrs])
    n = flat.shape[0]
    rows = -(-n // 128)
    rows = -(-rows // rows_mult) * rows_mult
    return jnp.pad(flat, (0, rows * 128 - n)).reshape(rows, 128)


def _unflatten(flat, shapes):
    flat = flat.reshape(-1)
    out, o = [], 0
    for shp in shapes:
        n = int(np.prod(shp))
        out.append(flat[o:o + n].reshape(shp))
        o += n
    return out


BIG = ["w_mod", "w_in", "w_br_ssd", "w_br_lru", "w_out", "w_mlp1", "w_mlp2"]
COL_SHARDED = {"w_mod": N_MOD * D, "w_in": IN_COLS, "w_mlp1": MLP_H}
SMALL_SHARDED = ["ssd_conv_w", "lru_conv_w", "lru_ba", "lru_bi", "lru_lambda"]
WEIGHTS = ['c_ctx', 'w_mod', 'b_mod', 'w_in', 'b_gate', 'ssd_conv_w', 'ssd_conv_b', 'ssd_dt_bias', 'ssd_a_log', 'ssd_d',
           'ssd_norm_w', 'lru_conv_w', 'lru_conv_b', 'lru_wa', 'lru_ba', 'lru_wi', 'lru_bi', 'lru_lambda', 'w_br_ssd',
           'w_br_lru', 'w_out', 'ln1_g', 'ln1_b', 'w_mlp1', 'b_mlp1', 'w_mlp2', 'b_mlp2', 'ln2_g', 'ln2_b']
SMALL = [n for n in WEIGHTS if n not in BIG]


def _full_from_chips(g4, name):
    if name in COL_SHARDED:
        return jnp.transpose(g4, (1, 0, 2)).reshape(g4.shape[1], 4 * g4.shape[2])
    return g4.reshape(4 * g4.shape[1], g4.shape[2])


def _chips_from_full(full, name):
    if name in COL_SHARDED:
        r, c = full.shape
        return jnp.transpose(full.reshape(r, 4, c // 4), (1, 0, 2))
    return full.reshape(4, full.shape[0] // 4, full.shape[1])


def kernel(x, c, ctx, c_ctx, w_mod, b_mod, w_in, b_gate, ssd_conv_w, ssd_conv_b, ssd_dt_bias, ssd_a_log, ssd_d, ssd_norm_w, lru_conv_w, lru_conv_b, lru_wa, lru_ba, lru_wi, lru_bi, lru_lambda, w_br_ssd, w_br_lru, w_out, ln1_g, ln1_b, w_mlp1, b_mlp1, w_mlp2, b_mlp2, ln2_g, ln2_b, loss_target, m_c_ctx, m_w_mod, m_b_mod, m_w_in, m_b_gate, m_ssd_conv_w, m_ssd_conv_b, m_ssd_dt_bias, m_ssd_a_log, m_ssd_d, m_ssd_norm_w, m_lru_conv_w, m_lru_conv_b, m_lru_wa, m_lru_ba, m_lru_wi, m_lru_bi, m_lru_lambda, m_w_br_ssd, m_w_br_lru, m_w_out, m_ln1_g, m_ln1_b, m_w_mlp1, m_b_mlp1, m_w_mlp2, m_b_mlp2, m_ln2_g, m_ln2_b, v_c_ctx, v_w_mod, v_b_mod, v_w_in, v_b_gate, v_ssd_conv_w, v_ssd_conv_b, v_ssd_dt_bias, v_ssd_a_log, v_ssd_d, v_ssd_norm_w, v_lru_conv_w, v_lru_conv_b, v_lru_wa, v_lru_ba, v_lru_wi, v_lru_bi, v_lru_lambda, v_w_br_ssd, v_w_br_lru, v_w_out, v_ln1_g, v_ln1_b, v_w_mlp1, v_b_mlp1, v_w_mlp2, v_b_mlp2, v_ln2_g, v_ln2_b):
    given = dict(locals())
    w = {n: given[n] for n in WEIGHTS}
    mom = {n: given["m_" + n] for n in WEIGHTS}
    var = {n: given["v_" + n] for n in WEIGHTS}
    chip = 2 * lax.axis_index("x") + lax.axis_index("y")

    shard2d = {n: w[n].reshape(w[n].shape[-2:]) for n in BIG}
    small_pack = _flatten([w[n] for n in SMALL_SHARDED], rows_mult=8)
    gathered = _gather_chips([shard2d[n].astype(BF16) for n in BIG] + [small_pack])
    full = {n: _full_from_chips(g, n) for n, g in zip(BIG, gathered[:-1])}
    full["w_in"] = _pack_win(full["w_in"])
    per_chip = [_unflatten(gathered[-1][q], [w[n].shape for n in SMALL_SHARDED]) for q in range(4)]
    sm = {n: jnp.concatenate([per_chip[q][i] for q in range(4)], axis=-1) for i, n in enumerate(SMALL_SHARDED)}
    for n in SMALL:
        if n not in sm:
            sm[n] = w[n]
    sm = {n: (a.reshape(a.shape[1:]) if a.ndim >= 3 else a) for n, a in sm.items()}

    loss, gx, gbig, gsmall = _local_step(x, c, ctx, loss_target, sm, *[full[n] for n in BIG])
    gbig["w_in"] = _unpack_win(gbig["w_in"])

    slabs = {n: _chips_from_full(gbig[n], n) for n in BIG}
    recv = _scatter_chips([slabs[n].astype(BF16) for n in BIG])
    part = [_sum4(lax.dynamic_index_in_dim(slabs[n], chip, 0, keepdims=False), r, "sum_" + n) for n, r in zip(BIG, recv)]
    other = _swap_cores(part)
    out = {}
    for n, pa, pb in zip(BIG, part, other):
        res = _adam_pair(pa, pb, shard2d[n], mom[n].reshape(pa.shape), var[n].reshape(pa.shape), "adam_" + n)
        out[n] = [r.reshape(w[n].shape) for r in res]

    full_shapes = [gsmall[n].shape for n in SMALL]
    gs_all = _unflatten(_sum8(_gather_all(_flatten([gsmall[n] for n in SMALL]))), full_shapes)
    gs = {}
    for n, g in zip(SMALL, gs_all):
        if n in SMALL_SHARDED:
            width = w[n].shape[-1]
            g = lax.dynamic_slice_in_dim(g, chip * width, width, axis=g.ndim - 1)
        gs[n] = g.reshape(w[n].shape)
    shapes = [w[n].shape for n in SMALL]
    d_s, m_s, v_s = _adam_flat(_flatten([gs[n] for n in SMALL]), _flatten([w[n] for n in SMALL]),
                               _flatten([mom[n] for n in SMALL]), _flatten([var[n] for n in SMALL]))
    for n, d_, m_, v_ in zip(SMALL, _unflatten(d_s, shapes), _unflatten(m_s, shapes), _unflatten(v_s, shapes)):
        out[n] = [gs[n], d_, m_, v_]

    loss = lax.psum(loss, ("x", "y", "c"))
    return (loss, gx, *[out[n][0] for n in WEIGHTS], *[out[n][1] for n in WEIGHTS], *[out[n][2] for n in WEIGHTS],
            *[out[n][3] for n in WEIGHTS])
```

```python
import functools

import numpy as np
import jax
import jax.numpy as jnp
from jax import lax
from jax.experimental import pallas as pl
from jax.experimental.pallas import tpu as pltpu

F32, BF16 = jnp.float32, jnp.bfloat16
S = jax.ShapeDtypeStruct
BS = pl.BlockSpec
MESH = pl.DeviceIdType.MESH

D = 1024
GRID_W = 64
SSD_INNER, SSD_G, SSD_N, SSD_L = 2048, 8, 128, 128
SSD_GW = 512
MLP_H = 4096
N_MOD = 6
ALPHA = 2.0 ** 0.25
LN_EPS, RMS_EPS = 1e-6, 1e-5
LRU_C = 8.0
P_XBC, P_LRU, P_Z, P_LG, P_MG, P_DT, P_W = 0, 4096, 5120, 7168, 8192, 10240, 10368
P_CB = 3456
IN_COLS = 10304
LRU_CB = 256
ADAM_LR, ADAM_B1, ADAM_B2, ADAM_EPS, ADAM_WD, ADAM_STEP = 0.001, 0.9, 0.999, 1e-08, 0.01, 10
VMEM_LIMIT = 56 * 2 ** 20


def _params(**kw):
    return pltpu.CompilerParams(vmem_limit_bytes=VMEM_LIMIT, **kw)


def _dot(a, b):
    return jnp.dot(a.astype(BF16), b.astype(BF16), preferred_element_type=F32)


def _dot_nt(a, b):
    return lax.dot_general(a.astype(BF16), b.astype(BF16), (((1,), (1,)), ((), ())), preferred_element_type=F32)


def _dot_tn(a, b):
    return lax.dot_general(a.astype(BF16), b.astype(BF16), (((0,), (0,)), ((), ())), preferred_element_type=F32)


@jax.custom_vjp
def _mm(a, b):
    return _dot(a, b)


def _cast_pair(a, b):
    return a.astype(BF16), b.astype(BF16)


def _mm_f(a, b):
    r = _cast_pair(a, b)
    return _dot(*r), r


def _mm_b(r, g):
    g = g.astype(BF16)
    return _dot_nt(g, r[1]), _dot_tn(r[0], g)


_mm.defvjp(_mm_f, _mm_b)


@jax.custom_vjp
def _mm_nt(a, b):
    return _dot_nt(a, b)


def _mm_nt_f(a, b):
    r = _cast_pair(a, b)
    return _dot_nt(*r), r


def _mm_nt_b(r, g):
    g = g.astype(BF16)
    return _dot(g, r[1]), _dot_tn(g, r[0])


_mm_nt.defvjp(_mm_nt_f, _mm_nt_b)


@jax.custom_vjp
def _mm_tn(a, b):
    return _dot_tn(a, b)


def _mm_tn_f(a, b):
    r = _cast_pair(a, b)
    return _dot_tn(*r), r


def _mm_tn_b(r, g):
    g = g.astype(BF16)
    return _dot_nt(r[1], g), _dot(r[0], g)


_mm_tn.defvjp(_mm_tn_f, _mm_tn_b)

def _split3(v):
    h = v.astype(BF16)
    r = v - h.astype(F32)
    m = r.astype(BF16)
    return h, m, (r - m.astype(F32)).astype(BF16)


def _sel_dot(sel, v, dims):
    sel_first = dims[0] == "s"
    dn = {"sv": (((1,), (0,)), ((), ())), "sTv": (((0,), (0,)), ((), ())), "vs": (((1,), (0,)), ((), ())),
          "vsT": (((1,), (1,)), ((), ()))}[dims]
    out = None
    for part in _split3(v):
        a, b = (sel, part) if sel_first else (part, sel)
        term = lax.dot_general(a, b, dn, preferred_element_type=F32)
        out = term if out is None else out + term
    return out


@jax.custom_vjp
def _cum_mm(tri, v):
    return _sel_dot(tri, v, "sv")


_cum_mm.defvjp(lambda tri, v: (_sel_dot(tri, v, "sv"), tri),
               lambda tri, g: (jnp.zeros_like(tri), _sel_dot(tri, g, "sTv")))


@jax.custom_vjp
def _xp_mm(v, e):
    return _sel_dot(e, v, "vs")


_xp_mm.defvjp(lambda v, e: (_sel_dot(e, v, "vs"), e),
              lambda e, g: (_sel_dot(e, g, "vsT"), jnp.zeros_like(e)))


def _sigmoid(x):
    return 1.0 / (1.0 + jnp.exp(-x))


def _silu(x):
    return x * _sigmoid(x)


def _dsilu(x):
    s = _sigmoid(x)
    return s * (1.0 + x * (1.0 - s))


def _softplus(x):
    return jnp.maximum(x, 0.0) + jnp.log1p(jnp.exp(-jnp.abs(x)))


def _gelu(x):
    return 0.5 * x * (1.0 + jnp.tanh(0.7978845608028654 * (x + 0.044715 * x * x * x)))


def _ln(x):
    mu = jnp.mean(x, axis=-1, keepdims=True)
    xc = x - mu
    var = jnp.mean(xc * xc, axis=-1, keepdims=True)
    return xc * lax.rsqrt(var + LN_EPS)


def _modln(x, shift, scale):
    return _ln(x) * (1.0 + scale) + shift


def _resln(x, sub, gate, g, b):
    return _ln(ALPHA * x + gate * sub) * g + b


def _grms(y, z, w):
    u = y * _silu(z)
    return u * lax.rsqrt(jnp.mean(u * u, axis=-1, keepdims=True) + RMS_EPS) * w


def _colsum(v):
    return jnp.sum(v, axis=0, keepdims=True)


def _conv_taps(width, period):
    def masks(rows):
        pos = lax.broadcasted_iota(jnp.int32, (rows, width), 0) & (period - 1)
        return [jnp.logical_and(pos + (k - 2) >= 0, pos + (k - 2) < period) for k in range(4)]
    return masks


def _conv_fwd(raw, w, b, masks):
    rows = raw.shape[0]
    pre = b + raw * w[2:3, :]
    for k in (0, 1, 3):
        sh = pltpu.roll(raw, (2 - k) % rows, 0)
        pre = pre + jnp.where(masks[k], sh, 0.0) * w[k:k + 1, :]
    return pre


def _for_rows(t, rb, fn):
    n = t // rb
    unroll = 4 if n % 4 == 0 else 1

    def step(i, carry):
        for u in range(unroll):
            fn(pl.multiple_of((i * unroll + u) * rb, rb))
        return carry

    lax.fori_loop(0, n // unroll, step, 0)


def _loop_unrolled(n, unroll, body, init):
    def step(i, carry):
        for u in range(unroll):
            carry = body(i * unroll + u, carry)
        return carry

    return lax.fori_loop(0, n // unroll, step, init)


def _conv_bwd(dpre, raw, w, masks):
    rows = raw.shape[0]
    draw = dpre * w[2:3, :]
    dws = []
    for k in range(4):
        if k == 2:
            dws.append(_colsum(dpre * raw))
            continue
        sh = pltpu.roll(raw, (2 - k) % rows, 0)
        dws.append(_colsum(dpre * jnp.where(masks[k], sh, 0.0)))
        back = pltpu.roll(jnp.where(masks[k], dpre, 0.0) * w[k:k + 1, :], (k - 2) % rows, 0)
        draw = draw + back
    return draw, jnp.concatenate(dws, axis=0), _colsum(dpre)


def _mod_fwd(cc, wmod, bmod):
    def body(cc_ref, w_ref, b_ref, o_ref):
        o_ref[...] = _dot(_silu(cc_ref[...]), w_ref[...]) + b_ref[...]

    return pl.pallas_call(
        body, grid=(N_MOD,), name="mod_fwd",
        in_specs=[BS((8, D), lambda j: (0, 0)), BS((D, D), lambda j: (0, j)), BS((1, D), lambda j: (0, j))],
        out_specs=BS((8, D), lambda j: (0, j)), out_shape=S((8, N_MOD * D), F32), compiler_params=_params(),
    )(cc, wmod, bmod)


def _mod_bwd(cc, wmod, dm):
    def body(cc_ref, w_ref, dm_ref, dw_ref, db_ref, dcc_ref):
        j = pl.program_id(0)
        c = cc_ref[...]
        dmv = dm_ref[...]
        dw_ref[...] = _dot_tn(_silu(c), dmv)
        db_ref[...] = _colsum(dmv)

        @pl.when(j == 0)
        def _():
            dcc_ref[...] = jnp.zeros_like(dcc_ref)

        dcc_ref[...] += _dot_nt(dmv, w_ref[...]) * _dsilu(c)

    return pl.pallas_call(
        body, grid=(N_MOD,), name="mod_bwd",
        in_specs=[BS((8, D), lambda j: (0, 0)), BS((D, D), lambda j: (0, j)), BS((8, D), lambda j: (0, j))],
        out_specs=[BS((D, D), lambda j: (0, j)), BS((1, D), lambda j: (0, j)), BS((8, D), lambda j: (0, 0))],
        out_shape=[S((D, N_MOD * D), F32), S((1, N_MOD * D), F32), S((8, D), F32)], compiler_params=_params(),
    )(cc, wmod, dm)


def _inproj_fwd(xa, m4, win, rt, n_lat_tiles, tiles_per_b, ctx_row):
    n_tiles = xa.shape[0] // rt

    def mrow(i):
        return jnp.where(i < n_lat_tiles, i // tiles_per_b, ctx_row)

    def body(x_ref, sh_ref, sc_ref, w_hbm, p_ref, h_ref, w_vm, sem):
        @pl.when(pl.program_id(0) == 0)
        def _():
            cp = pltpu.make_async_copy(w_hbm, w_vm, sem)
            cp.start()
            cp.wait()

        hb = _modln(x_ref[...], sh_ref[...], sc_ref[...]).astype(BF16)
        h_ref[...] = hb
        for j in range(P_W // P_CB):
            sl = slice(j * P_CB, (j + 1) * P_CB)
            p_ref[:, sl] = jnp.dot(hb, w_vm[:, sl], preferred_element_type=F32)

    return pl.pallas_call(
        body, grid=(n_tiles,), name="inproj_fwd",
        in_specs=[BS((rt, D), lambda i: (i, 0)),
                  BS((None, None, 1, D), lambda i: (mrow(i), 0, 0, 0)),
                  BS((None, None, 1, D), lambda i: (mrow(i), 1, 0, 0)),
                  BS(memory_space=pl.ANY)],
        out_specs=[BS((rt, P_W), lambda i: (i, 0)), BS((rt, D), lambda i: (i, 0))],
        out_shape=[S((xa.shape[0], P_W), F32), S((xa.shape[0], D), BF16)],
        scratch_shapes=[pltpu.VMEM((D, P_W), BF16), pltpu.SemaphoreType.DMA(())], compiler_params=_params(),
    )(xa, m4, m4, win)


def _inproj_bwd(xa, m4, win, dproj, rt, tile0, n_tiles, tiles_per_b, ctx_row, latent, dxres):
    def mrow(i):
        return (i // tiles_per_b) if latent else ctx_row

    def body(x_ref, sh_ref, sc_ref, dp_ref, w_hbm, *rest):
        if latent:
            dxr_ref, gx_ref, dm_ref, w_vm, sem = rest
        else:
            dm_ref, w_vm, sem = rest
        i = pl.program_id(0)

        @pl.when(i == 0)
        def _():
            cp = pltpu.make_async_copy(w_hbm, w_vm, sem)
            cp.start()
            cp.wait()

        dh = lax.dot_general(dp_ref[...], w_vm[...], (((1,), (1,)), ((), ())), preferred_element_type=F32)
        _, vjp = jax.vjp(_modln, x_ref[...], sh_ref[...], sc_ref[...])
        dx, dsh, dsc = vjp(dh)
        if latent:
            gx_ref[...] = dx + dxr_ref[...]

        @pl.when(i % tiles_per_b == 0)
        def _():
            dm_ref[...] = jnp.zeros_like(dm_ref)

        dm_ref[0:1, :] += dsh
        dm_ref[1:2, :] += dsc

    nb = n_tiles // tiles_per_b
    in_specs = [BS((rt, D), lambda i: (tile0 + i, 0)),
                BS((None, None, 1, D), lambda i: (mrow(i), 0, 0, 0)),
                BS((None, None, 1, D), lambda i: (mrow(i), 1, 0, 0)),
                BS((rt, P_W), lambda i: (tile0 + i, 0)),
                BS(memory_space=pl.ANY)]
    args = [xa, m4, m4, dproj, win]
    dm_spec = BS((None, 2, D), lambda i: (i // tiles_per_b, 0, 0))
    if latent:
        in_specs.append(BS((rt, D), lambda i: (i, 0)))
        args.append(dxres)
        out_specs = [BS((rt, D), lambda i: (i, 0)), dm_spec]
        out_shape = [S((n_tiles * rt, D), F32), S((nb, 2, D), F32)]
    else:
        out_specs = [dm_spec]
        out_shape = [S((nb, 2, D), F32)]
    return pl.pallas_call(
        body, grid=(n_tiles,), name="inproj_bwd_lat" if latent else "inproj_bwd_ctx",
        in_specs=in_specs, out_specs=out_specs, out_shape=out_shape,
        scratch_shapes=[pltpu.VMEM((D, P_W), BF16), pltpu.SemaphoreType.DMA(())],
        compiler_params=_params(),
    )(*args)


def _matmul_tn(a, b, tm, tn, tk, name):
    k, m = a.shape
    n = b.shape[1]

    def body(a_ref, b_ref, o_ref):
        @pl.when(pl.program_id(2) == 0)
        def _():
            o_ref[...] = jnp.zeros_like(o_ref)

        o_ref[...] += lax.dot_general(a_ref[...], b_ref[...], (((0,), (0,)), ((), ())), preferred_element_type=F32)

    return pl.pallas_call(
        body, grid=(m // tm, n // tn, k // tk), name=name,
        in_specs=[BS((tk, tm), lambda i, j, kk: (kk, i)), BS((tk, tn), lambda i, j, kk: (kk, j))],
        out_specs=BS((tm, tn), lambda i, j, kk: (i, j)), out_shape=S((m, n), F32), compiler_params=_params(),
    )(a, b)


def _ssd_consts():
    n = SSD_L
    ii = lax.broadcasted_iota(jnp.int32, (n, n), 0)
    jj = lax.broadcasted_iota(jnp.int32, (n, n), 1)
    er = lax.broadcasted_iota(jnp.int32, (128, 256), 0)
    ec = lax.broadcasted_iota(jnp.int32, (128, 256), 1) >> 6
    lane = lax.broadcasted_iota(jnp.int32, (1, 256), 1) >> 6
    per_dir = []
    for d in (0, 1):
        mask = (jj >= ii) if d else (jj <= ii)
        per_dir.append((mask, mask.astype(BF16), (er == ec + 4 * d).astype(BF16)))
    return per_dir, [(lane == h).astype(F32) for h in range(4)]


def _ssd_chunk(x, bm, cm, dtc, dtx, alog, hst, consts, hmasks, rev):
    n = SSD_L
    mask, tri, e = consts
    cum = _cum_mm(tri, dtc * (-jnp.exp(alog)))
    cum_x = _xp_mm(cum, e)
    tot_x = cum_x[0:1, :] if rev else cum_x[n - 1:n, :]
    xd = x * dtx
    hn = jnp.exp(tot_x) * hst + _mm_tn(bm, xd * jnp.exp(tot_x - cum_x))
    if cm is None:
        return hn
    cum_t = cum.T
    cb = _mm_nt(cm, bm)
    y = jnp.exp(cum_x) * _mm(cm, hst)
    for h in range(4):
        k = 4 * rev + h
        decay = jnp.exp(jnp.where(mask, cum[:, k:k + 1] - cum_t[k:k + 1, :], -1e30))
        y = y + _mm(cb * decay, xd * hmasks[h])
    return y, hn


def _ssd_fwd(proj, dtg, cw, cb, dtb, alog, drow, h0f, h0b, nb, t, period, blk0, need_y):
    nc = t // SSD_L
    rb = period
    assert t % rb == 0
    unroll = 2 if nc % 2 == 0 else 1
    masks_of = _conv_taps(SSD_GW, period)

    def body(p_ref, dt_ref, cw_ref, cb_ref, dtb_ref, al_ref, d_ref, h0f_ref, h0b_ref, *rest):
        if need_y:
            y_ref, hsf_ref, hsb_ref, sf_ref, sb_ref, act, dts, dtxs = rest
        else:
            hsf_ref, hsb_ref, sf_ref, sb_ref, act, dts, dtxs = rest
        masks = masks_of(rb)
        per_dir, hmasks = _ssd_consts()

        def prologue(r0):
            rows = pl.ds(r0, rb)
            a = _silu(_conv_fwd(p_ref[rows, :], cw_ref[...], cb_ref[...], masks))
            act[rows, :] = a
            if need_y:
                y_ref[rows, :] = d_ref[...] * a[:, 0:256]
            dtv = _softplus(dt_ref[rows, :] + dtb_ref[...])
            dts[rows, :] = dtv
            for d in (0, 1):
                dtxs[rows, 256 * d:256 * d + 256] = _xp_mm(dtv, per_dir[d][2])

        _for_rows(t, rb, prologue)
        al = al_ref[...]

        def chunk(ci, carry):
            out = []
            for d, hst, hs_ref in ((0, carry[0], hsf_ref), (1, carry[1], hsb_ref)):
                c = (nc - 1 - ci) if d else ci
                r0 = pl.multiple_of(c * SSD_L, SSD_L)
                a = act[pl.ds(r0, SSD_L), :]
                hs_ref[c] = hst
                res = _ssd_chunk(a[:, 0:256], a[:, 256:384], a[:, 384:512] if need_y else None,
                                 dts[pl.ds(r0, SSD_L), :], dtxs[pl.ds(r0, SSD_L), 256 * d:256 * d + 256], al, hst,
                                 per_dir[d], hmasks, d)
                if need_y:
                    y_ref[pl.ds(r0, SSD_L), :] += res[0]
                    res = res[1]
                out.append(res)
            return tuple(out)

        sf_ref[...], sb_ref[...] = _loop_unrolled(nc, unroll, chunk, (h0f_ref[...], h0b_ref[...]))

    gspec = lambda shp: BS((None,) + shp, lambda b, g: (g,) + (0,) * len(shp))
    st_spec = BS((None, None, SSD_N, 256), lambda b, g: (b, g, 0, 0))
    hs_spec = BS((None, None, nc, SSD_N, 256), lambda b, g: (b, g, 0, 0, 0))
    in_specs = [BS((t, SSD_GW), lambda b, g: (blk0 + b, g)), BS((None, t, 128), lambda b, g: (g, blk0 + b, 0)),
                gspec((4, SSD_GW)), gspec((1, SSD_GW)), gspec((1, 128)), gspec((1, 128)), gspec((1, 256)),
                st_spec, st_spec]
    out_specs = [hs_spec, hs_spec, st_spec, st_spec]
    out_shape = [S((nb, SSD_G, nc, SSD_N, 256), F32)] * 2 + [S((nb, SSD_G, SSD_N, 256), F32)] * 2
    if need_y:
        out_specs = [BS((t, 256), lambda b, g: (b, g))] + out_specs
        out_shape = [S((nb * t, SSD_INNER), F32)] + out_shape
    return pl.pallas_call(
        body, grid=(nb, SSD_G), name="ssd_fwd_lat" if need_y else "ssd_fwd_ctx",
        in_specs=in_specs, out_specs=out_specs, out_shape=out_shape,
        scratch_shapes=[pltpu.VMEM((t, SSD_GW), F32), pltpu.VMEM((t, 128), F32), pltpu.VMEM((t, SSD_GW), F32)],
        compiler_params=_params(),
    )(proj, dtg, cw, cb, dtb, alog, drow, h0f, h0b)


def _ssd_bwd(proj, dtg, cw, cb, dtb, alog, drow, hsf, hsb, dy, dsf, dsb, dproj, nb, t, period, blk0, need_y):
    nc = t // SSD_L
    rb = period
    assert t % rb == 0
    unroll = 2 if nc % 2 == 0 else 1
    masks_of = _conv_taps(SSD_GW, period)

    def body(*refs):
        if need_y:
            (p_ref, dt_ref, cw_ref, cb_ref, dtb_ref, al_ref, d_ref, hsf_ref, hsb_ref, dy_ref, dsf_ref, dsb_ref, _,
             dp_ref, ddt_ref, dhf_ref, dhb_ref, dcw_ref, dcb_ref, ddtb_ref, dal_ref, dd_ref,
             pre, dact, dts, ddts, dtxs) = refs
        else:
            (p_ref, dt_ref, cw_ref, cb_ref, dtb_ref, al_ref, d_ref, hsf_ref, hsb_ref, dsf_ref, dsb_ref, _,
             dp_ref, ddt_ref, dhf_ref, dhb_ref, dcw_ref, dcb_ref, ddtb_ref, dal_ref, dd_ref,
             pre, dact, dts, ddts, dtxs) = refs
            dy_ref = None
        b, g = pl.program_id(0), pl.program_id(1)

        @pl.when(jnp.logical_and(b == 0, g == 0))
        def _():
            for r in (dcw_ref, dcb_ref, ddtb_ref, dal_ref, dd_ref):
                r[...] = jnp.zeros_like(r)

        masks = masks_of(rb)
        per_dir, hmasks = _ssd_consts()

        def prologue(r0):
            rows = pl.ds(r0, rb)
            pre[rows, :] = _conv_fwd(p_ref[rows, :], cw_ref[...], cb_ref[...], masks)
            dtv = _softplus(dt_ref[rows, :] + dtb_ref[...])
            dts[rows, :] = dtv
            for d in (0, 1):
                dtxs[rows, 256 * d:256 * d + 256] = _xp_mm(dtv, per_dir[d][2])
            dact[rows, :] = jnp.zeros((rb, SSD_GW), F32)
            ddts[rows, :] = jnp.zeros((rb, 128), F32)

        _for_rows(t, rb, prologue)
        al = al_ref[...]
        def chunk(ci, carry):
            dal_c = carry[2]
            dhs_out = []
            for d, dh, hs_ref in ((0, carry[0], hsf_ref), (1, carry[1], hsb_ref)):
                c = ci if d else (nc - 1 - ci)
                r0 = pl.multiple_of(c * SSD_L, SSD_L)
                a = _silu(pre[pl.ds(r0, SSD_L), :])
                dtc = dts[pl.ds(r0, SSD_L), :]
                dtx = dtxs[pl.ds(r0, SSD_L), 256 * d:256 * d + 256]
                if need_y:
                    fn = lambda x_, bm_, cm_, dt_, dx_, al_, hs_: _ssd_chunk(x_, bm_, cm_, dt_, dx_, al_, hs_, per_dir[d],
                                                                             hmasks, d)
                    _, vjp = jax.vjp(fn, a[:, 0:256], a[:, 256:384], a[:, 384:512], dtc, dtx, al, hs_ref[c])
                    dx, dbm, dcm, ddtc, ddtx, dal_k, dhs = vjp((dy_ref[pl.ds(r0, SSD_L), :], dh))
                    dact[pl.ds(r0, SSD_L), 384:512] += dcm
                else:
                    fn = lambda x_, bm_, dt_, dx_, al_, hs_: _ssd_chunk(x_, bm_, None, dt_, dx_, al_, hs_, per_dir[d],
                                                                        hmasks, d)
                    _, vjp = jax.vjp(fn, a[:, 0:256], a[:, 256:384], dtc, dtx, al, hs_ref[c])
                    dx, dbm, ddtc, ddtx, dal_k, dhs = vjp(dh)
                dact[pl.ds(r0, SSD_L), 0:256] += dx
                dact[pl.ds(r0, SSD_L), 256:384] += dbm
                ddts[pl.ds(r0, SSD_L), :] += ddtc + _dot_nt(ddtx, per_dir[d][2])
                dhs_out.append(dhs)
                dal_c = dal_c + dal_k
            return dhs_out[0], dhs_out[1], dal_c

        dhf_ref[...], dhb_ref[...], dal_acc = _loop_unrolled(
            nc, unroll, chunk, (dsf_ref[...], dsb_ref[...], jnp.zeros((1, 128), F32)))

        def epilogue(r0):
            rows = pl.ds(r0, rb)
            prev = pre[rows, :]
            if need_y:
                dyv = dy_ref[rows, :]
                dact[rows, 0:256] += d_ref[...] * dyv
                dd_ref[g] += _colsum(dyv * _silu(prev[:, 0:256]))
            dpre = dact[rows, :] * _dsilu(prev)
            draw, dcw, dcb = _conv_bwd(dpre, p_ref[rows, :], cw_ref[...], masks)
            dp_ref[rows, :] = draw.astype(BF16)
            dcw_ref[g] += dcw
            dcb_ref[g] += dcb
            ddraw = ddts[rows, :] * _sigmoid(dt_ref[rows, :] + dtb_ref[...])
            ddt_ref[rows, :] = ddraw
            ddtb_ref[g] += _colsum(ddraw)

        _for_rows(t, rb, epilogue)
        dal_ref[g] += dal_acc

    gspec = lambda shp: BS((None,) + shp, lambda b, g: (g,) + (0,) * len(shp))
    full = lambda shp: BS(shp, lambda b, g: (0,) * len(shp))
    st_spec = BS((None, None, SSD_N, 256), lambda b, g: (b, g, 0, 0))
    hs_spec = BS((None, None, nc, SSD_N, 256), lambda b, g: (b, g, 0, 0, 0))
    p_spec = BS((t, SSD_GW), lambda b, g: (blk0 + b, g))
    in_specs = [p_spec, BS((None, t, 128), lambda b, g: (g, blk0 + b, 0)),
                gspec((4, SSD_GW)), gspec((1, SSD_GW)), gspec((1, 128)), gspec((1, 128)), gspec((1, 256)),
                hs_spec, hs_spec]
    args = [proj, dtg, cw, cb, dtb, alog, drow, hsf, hsb]
    if need_y:
        in_specs.append(BS((t, 256), lambda b, g: (b, g)))
        args.append(dy)
    in_specs += [st_spec, st_spec, BS(memory_space=pl.ANY)]
    args += [dsf, dsb, dproj]
    out_specs = [p_spec, BS((None, t, 128), lambda b, g: (g, b, 0)), st_spec, st_spec,
                 full((SSD_G, 4, SSD_GW)), full((SSD_G, 1, SSD_GW)), full((SSD_G, 1, 128)), full((SSD_G, 1, 128)),
                 full((SSD_G, 1, 256))]
    out_shape = [S(dproj.shape, BF16), S((SSD_G, nb * t, 128), F32),
                 S((nb, SSD_G, SSD_N, 256), F32), S((nb, SSD_G, SSD_N, 256), F32),
                 S((SSD_G, 4, SSD_GW), F32), S((SSD_G, 1, SSD_GW), F32), S((SSD_G, 1, 128), F32),
                 S((SSD_G, 1, 128), F32), S((SSD_G, 1, 256), F32)]
    return pl.pallas_call(
        body, grid=(nb, SSD_G), name="ssd_bwd_lat" if need_y else "ssd_bwd_ctx",
        in_specs=in_specs, out_specs=out_specs, out_shape=out_shape,
        input_output_aliases={len(args) - 1: 0},
        scratch_shapes=[pltpu.VMEM((t, SSD_GW), F32), pltpu.VMEM((t, SSD_GW), F32), pltpu.VMEM((t, 128), F32),
                        pltpu.VMEM((t, 128), F32), pltpu.VMEM((t, SSD_GW), F32)],
        compiler_params=_params(),
    )(*args)


def _lru_gate(u, wa, ba, wi, bi, lam):
    r = _sigmoid(_mm(u, wa) + ba)
    i = _sigmoid(_mm(u, wi) + bi)
    log_a = -LRU_C * r * _softplus(-lam)
    a = jnp.exp(log_a)
    x2 = 2.0 * log_a
    em1 = jnp.where(x2 > -0.01, x2 * (1.0 + x2 * (0.5 + x2 * (1.0 / 6.0 + x2 * (1.0 / 24.0)))), jnp.exp(x2) - 1.0)
    return a, jnp.sqrt(-em1) * (i * u)


def _scan_pair(fwd, rev, nblk, width):
    row = lax.broadcasted_iota(jnp.int32, (8, width), 0)

    def block(a_ref, b_ref, h_ref, st, carry, reverse):
        av, bv = a_ref[pl.ds(st, 8), :], b_ref[pl.ds(st, 8), :]
        for s in (1, 2, 4):
            ok = (row < 8 - s) if reverse else (row >= s)
            sh = (8 - s) if reverse else s
            a_sh = jnp.where(ok, pltpu.roll(av, sh, 0), 1.0)
            b_sh = jnp.where(ok, pltpu.roll(bv, sh, 0), 0.0)
            bv = av * b_sh + bv
            av = av * a_sh
        h = bv + av * carry
        h_ref[pl.ds(st, 8), :] = h
        return h[0:1, :] if reverse else h[7:8, :]

    def step(i, carry):
        cf, cr = carry
        cf = block(fwd[0], fwd[1], fwd[2], pl.multiple_of(i * 8, 8), cf, False)
        cr = block(rev[0], rev[1], rev[2], pl.multiple_of((nblk - 1 - i) * 8, 8), cr, True)
        return cf, cr

    return lax.fori_loop(0, nblk, step, (fwd[3], rev[3]))


def _lru_specs(t, blk0):
    p_spec = BS((t, LRU_CB), lambda b, q: (blk0 + b, P_LRU // LRU_CB + q))
    w_spec = BS((2, 2, 128, 128), lambda b, q: (0, q, 0, 0))
    v_spec = BS((2, LRU_CB), lambda b, q: (0, q))
    c_spec = lambda r: BS((r, LRU_CB), lambda b, q: (0, q))
    s_spec = BS((None, 2, LRU_CB), lambda b, q: (b, 0, q))
    return p_spec, w_spec, v_spec, c_spec, s_spec


def _lru_fwd(proj, cw, cb, wa, ba, wi, bi, lam, h0, nb, t, period, blk0, need_y):
    nq = D // LRU_CB
    masks_of = _conv_taps(LRU_CB, period)

    def body(p_ref, cw_ref, cb_ref, wa_ref, ba_ref, wi_ref, bi_ref, lam_ref, h0_ref, *rest):
        if need_y:
            y_ref, hf_ref, hb_ref, fin_ref, sa0, sb0, sa1, sb1 = rest
        else:
            hf_ref, hb_ref, fin_ref, sa0, sb0, sa1, sb1 = rest
        u = _conv_fwd(p_ref[...], cw_ref[...], cb_ref[...], masks_of(t))
        for d, (sa, sb) in enumerate(((sa0, sb0), (sa1, sb1))):
            for j in range(2):
                sl = slice(128 * j, 128 * j + 128)
                a, bb = _lru_gate(u[:, sl], wa_ref[d, j], ba_ref[d:d + 1, sl], wi_ref[d, j], bi_ref[d:d + 1, sl],
                                  lam_ref[d:d + 1, sl])
                sa[:, sl] = a
                sb[:, sl] = bb
        lf, lb = _scan_pair((sa0, sb0, hf_ref, h0_ref[0:1, :]), (sa1, sb1, hb_ref, h0_ref[1:2, :]), t // 8, LRU_CB)
        fin_ref[0:1, :] = lf
        fin_ref[1:2, :] = lb
        if need_y:
            y_ref[...] = hf_ref[...] + hb_ref[...]

    p_spec, w_spec, v_spec, c_spec, s_spec = _lru_specs(t, blk0)
    o_spec = BS((t, LRU_CB), lambda b, q: (b, q))
    out_specs = [o_spec, o_spec, s_spec]
    out_shape = [S((nb * t, D), F32), S((nb * t, D), F32), S((nb, 2, D), F32)]
    if need_y:
        out_specs = [o_spec] + out_specs
        out_shape = [S((nb * t, D), F32)] + out_shape
    return pl.pallas_call(
        body, grid=(nb, nq), name="lru_fwd_lat" if need_y else "lru_fwd_ctx",
        in_specs=[p_spec, c_spec(4), c_spec(1), w_spec, v_spec, w_spec, v_spec, v_spec, s_spec],
        out_specs=out_specs, out_shape=out_shape,
        scratch_shapes=[pltpu.VMEM((t, LRU_CB), F32)] * 4, compiler_params=_params(),
    )(proj, cw, cb, wa, ba, wi, bi, lam, h0)


def _lru_bwd(proj, cw, cb, wa, ba, wi, bi, lam, h0, hf, hb, dy, dfin, dproj, nb, t, period, blk0, need_y):
    nq = D // LRU_CB
    rc = min(256, t)
    masks_of = _conv_taps(LRU_CB, period)

    def body(*refs):
        if need_y:
            (p_ref, cw_ref, cb_ref, wa_ref, ba_ref, wi_ref, bi_ref, lam_ref, h0_ref, hf_ref, hb_ref, dy_ref, dfin_ref, _,
             dp_ref, dh0_ref, dcw_ref, dcb_ref, dwa_ref, dwi_ref, dba_ref, dbi_ref, dlam_ref,
             su, sa0, sa1, sc0, sc1, sg0, sg1) = refs
        else:
            (p_ref, cw_ref, cb_ref, wa_ref, ba_ref, wi_ref, bi_ref, lam_ref, h0_ref, hf_ref, hb_ref, dfin_ref, _,
             dp_ref, dh0_ref, dcw_ref, dcb_ref, dwa_ref, dwi_ref, dba_ref, dbi_ref, dlam_ref,
             su, sa0, sa1, sc0, sc1, sg0, sg1) = refs
            dy_ref = None
        b, q = pl.program_id(0), pl.program_id(1)

        @pl.when(jnp.logical_and(b == 0, q == 0))
        def _():
            for r in (dcw_ref, dcb_ref, dwa_ref, dwi_ref, dba_ref, dbi_ref, dlam_ref):
                r[...] = jnp.zeros_like(r)

        masks = masks_of(t)
        u = _conv_fwd(p_ref[...], cw_ref[...], cb_ref[...], masks)
        su[...] = u
        for d, sa in enumerate((sa0, sa1)):
            for j in range(2):
                sl = slice(128 * j, 128 * j + 128)
                a, _unused = _lru_gate(u[:, sl], wa_ref[d, j], ba_ref[d:d + 1, sl], wi_ref[d, j], bi_ref[d:d + 1, sl],
                                       lam_ref[d:d + 1, sl])
                sa[:, sl] = a
        rowi = lax.broadcasted_iota(jnp.int32, (t, LRU_CB), 0)
        last, first = rowi == t - 1, rowi == 0
        sc0[...] = jnp.where(last, 0.0, pltpu.roll(sa0[...], t - 1, 0))
        sc1[...] = jnp.where(first, 0.0, pltpu.roll(sa1[...], 1, 0))
        g0 = jnp.where(last, dfin_ref[0:1, :], 0.0)
        g1 = jnp.where(first, dfin_ref[1:2, :], 0.0)
        if need_y:
            g0 = g0 + dy_ref[...]
            g1 = g1 + dy_ref[...]
        sg0[...] = g0
        sg1[...] = g1
        zero = jnp.zeros((1, LRU_CB), F32)
        _scan_pair((sc1, sg1, sg1, zero), (sc0, sg0, sg0, zero), t // 8, LRU_CB)
        dh0_ref[0:1, :] = sa0[0:1, :] * sg0[0:1, :]
        dh0_ref[1:2, :] = sa1[t - 1:t, :] * sg1[t - 1:t, :]
        sc0[...] = sg0[...] * jnp.where(first, h0_ref[0:1, :], pltpu.roll(hf_ref[...], 1, 0))
        sc1[...] = sg1[...] * jnp.where(last, h0_ref[1:2, :], pltpu.roll(hb_ref[...], t - 1, 0))

        def rows(ci, carry):
            r0 = pl.multiple_of(ci * rc, rc)
            for j in range(2):
                sl = slice(128 * j, 128 * j + 128)
                du = jnp.zeros((rc, 128), F32)
                for d, (sc, sg) in enumerate(((sc0, sg0), (sc1, sg1))):
                    _, vjp = jax.vjp(_lru_gate, su[pl.ds(r0, rc), sl], wa_ref[d, j], ba_ref[d:d + 1, sl], wi_ref[d, j],
                                     bi_ref[d:d + 1, sl], lam_ref[d:d + 1, sl])
                    du_d, dwa, dba, dwi, dbi, dlam = vjp((sc[pl.ds(r0, rc), sl], sg[pl.ds(r0, rc), sl]))
                    du = du + du_d
                    dwa_ref[d, 2 * q + j] += dwa
                    dwi_ref[d, 2 * q + j] += dwi
                    dba_ref[q, d:d + 1, sl] += dba
                    dbi_ref[q, d:d + 1, sl] += dbi
                    dlam_ref[q, d:d + 1, sl] += dlam
                sa0[pl.ds(r0, rc), sl] = du
            return carry

        lax.fori_loop(0, t // rc, rows, 0)
        draw, dcw, dcb = _conv_bwd(sa0[...], p_ref[...], cw_ref[...], masks)
        dp_ref[...] = draw.astype(BF16)
        dcw_ref[q] += dcw
        dcb_ref[q] += dcb

    p_spec, w_spec, v_spec, c_spec, s_spec = _lru_specs(t, blk0)
    o_spec = BS((t, LRU_CB), lambda b, q: (b, q))
    full = lambda shp: BS(shp, lambda b, q: (0,) * len(shp))
    in_specs = [p_spec, c_spec(4), c_spec(1), w_spec, v_spec, w_spec, v_spec, v_spec, s_spec, o_spec, o_spec]
    args = [proj, cw, cb, wa, ba, wi, bi, lam, h0, hf, hb]
    if need_y:
        in_specs.append(o_spec)
        args.append(dy)
    in_specs += [s_spec, BS(memory_space=pl.ANY)]
    args += [dfin, dproj]
    out_specs = [p_spec, s_spec, full((nq, 4, LRU_CB)), full((nq, 1, LRU_CB)), full((2, 8, 128, 128)),
                 full((2, 8, 128, 128)), full((nq, 2, LRU_CB)), full((nq, 2, LRU_CB)), full((nq, 2, LRU_CB))]
    out_shape = [S(dproj.shape, BF16), S((nb, 2, D), F32), S((nq, 4, LRU_CB), F32), S((nq, 1, LRU_CB), F32),
                 S((2, 8, 128, 128), F32), S((2, 8, 128, 128), F32), S((nq, 2, LRU_CB), F32), S((nq, 2, LRU_CB), F32),
                 S((nq, 2, LRU_CB), F32)]
    return pl.pallas_call(
        body, grid=(nb, nq), name="lru_bwd_lat" if need_y else "lru_bwd_ctx",
        in_specs=in_specs, out_specs=out_specs, out_shape=out_shape, input_output_aliases={len(args) - 1: 0},
        scratch_shapes=[pltpu.VMEM((t, LRU_CB), F32)] * 7, compiler_params=_params(),
    )(*args)


def _mix_core(y_ref, yl_ref, p_ref, nw_ref, bg_ref, wbs_ref, wbl_ref, wo_ref, nrm_s):
    for g in range(SSD_G):
        sl = slice(256 * g, 256 * g + 256)
        nrm_s[:, sl] = _grms(y_ref[:, sl], p_ref[:, sl], nw_ref[:, sl]).astype(BF16)
    br_s = jnp.dot(nrm_s[...], wbs_ref[...], preferred_element_type=F32)
    gl = (yl_ref[...] * _gelu(p_ref[:, 2048:3072])).astype(BF16)
    br_l = jnp.dot(gl, wbl_ref[...], preferred_element_type=F32)
    gs = _sigmoid(p_ref[:, 3072:4096] + bg_ref[:, 0:D])
    gr = _sigmoid(p_ref[:, 4096:5120] + bg_ref[:, D:2 * D])
    mix = (gs * br_s + gr * br_l).astype(BF16)
    xmix = jnp.dot(mix, wo_ref[...], preferred_element_type=F32)
    return br_s, gl, br_l, gs, gr, mix, xmix


def _mix_specs(rt, tiles_per_b):
    row = lambda w: BS((rt, w), lambda i: (i, 0))
    const = lambda shp: BS(shp, lambda i: (0,) * len(shp))
    gate = BS((None, None, 1, D), lambda i: (i // tiles_per_b, 2, 0, 0))
    return row, const, gate


def _mix_fwd(y, ylru, proj, x, m4, wbs, wbl, wo, nw, bg, l1g, l1b, rt, tiles_per_b):
    n = x.shape[0]

    def body(y_ref, yl_ref, p_ref, x_ref, g1_ref, wbs_ref, wbl_ref, wo_ref, nw_ref, bg_ref, lg_ref, lb_ref, x1_ref, nrm_s):
        xmix = _mix_core(y_ref, yl_ref, p_ref, nw_ref, bg_ref, wbs_ref, wbl_ref, wo_ref, nrm_s)[6]
        x1_ref[...] = _resln(x_ref[...], xmix, g1_ref[...], lg_ref[...], lb_ref[...])

    row, const, gate = _mix_specs(rt, tiles_per_b)
    return pl.pallas_call(
        body, grid=(n // rt,), name="mix_fwd",
        in_specs=[row(SSD_INNER), row(D), BS((rt, 5120), lambda i: (i, 1)), row(D), gate,
                  const((SSD_INNER, D)), const((D, D)), const((D, D)), const((1, SSD_INNER)), const((1, 2 * D)),
                  const((1, D)), const((1, D))],
        out_specs=row(D), out_shape=S((n, D), F32),
        scratch_shapes=[pltpu.VMEM((rt, SSD_INNER), BF16)], compiler_params=_params(),
    )(y, ylru, proj, x, m4, wbs, wbl, wo, nw, bg, l1g, l1b)


def _mix_bwd(y, ylru, proj, x, m4, wbs, wbl, wo, nw, bg, l1g, l1b, dx1, dproj, rt, tiles_per_b):
    n = x.shape[0]

    def body(y_ref, yl_ref, p_ref, x_ref, g1_ref, wbs_ref, wbl_ref, wo_ref, nw_ref, bg_ref, lg_ref, lb_ref, dx1_ref, _,
             dp_ref, dy_ref, dyl_ref, dxr_ref, nrm_ref, dbrs_ref, gl_ref, dbrl_ref, mix_ref, dxm_ref,
             dg1_ref, dnw_ref, dbg_ref, dlg_ref, dlb_ref, nrm_s):
        i = pl.program_id(0)

        @pl.when(i == 0)
        def _():
            for r in (dnw_ref, dbg_ref, dlg_ref, dlb_ref):
                r[...] = jnp.zeros_like(r)

        @pl.when(i % tiles_per_b == 0)
        def _():
            dg1_ref[...] = jnp.zeros_like(dg1_ref)

        br_s, gl, br_l, gs, gr, mix, xmix = _mix_core(y_ref, yl_ref, p_ref, nw_ref, bg_ref, wbs_ref, wbl_ref, wo_ref,
                                                      nrm_s)
        _, vjp = jax.vjp(_resln, x_ref[...], xmix, g1_ref[...], lg_ref[...], lb_ref[...])
        dxr, dxmix, dg1, dlg, dlb = vjp(dx1_ref[...])
        dxr_ref[...] = dxr
        dg1_ref[...] += dg1
        dlg_ref[...] += dlg
        dlb_ref[...] += dlb
        dxmb = dxmix.astype(BF16)
        dxm_ref[...] = dxmb
        mix_ref[...] = mix
        nrm_ref[...] = nrm_s[...]
        gl_ref[...] = gl
        dmix = lax.dot_general(dxmb, wo_ref[...], (((1,), (1,)), ((), ())), preferred_element_type=F32)
        dbrs = (dmix * gs).astype(BF16)
        dbrl = (dmix * gr).astype(BF16)
        dbrs_ref[...] = dbrs
        dbrl_ref[...] = dbrl
        dmg_s = dmix * br_s * gs * (1.0 - gs)
        dmg_r = dmix * br_l * gr * (1.0 - gr)
        dp_ref[:, 3072:4096] = dmg_s.astype(BF16)
        dp_ref[:, 4096:5120] = dmg_r.astype(BF16)
        dbg_ref[:, 0:D] += _colsum(dmg_s)
        dbg_ref[:, D:2 * D] += _colsum(dmg_r)
        dnrm = lax.dot_general(dbrs, wbs_ref[...], (((1,), (1,)), ((), ())), preferred_element_type=F32)
        for g in range(SSD_G):
            sl = slice(256 * g, 256 * g + 256)
            _, vjp = jax.vjp(_grms, y_ref[:, sl], p_ref[:, sl], nw_ref[:, sl])
            dyg, dzg, dnwg = vjp(dnrm[:, sl])
            dy_ref[:, sl] = dyg
            dp_ref[:, sl] = dzg.astype(BF16)
            dnw_ref[:, sl] += dnwg
        dgl = lax.dot_general(dbrl, wbl_ref[...], (((1,), (1,)), ((), ())), preferred_element_type=F32)
        _, vjp = jax.vjp(lambda a, c: a * _gelu(c), yl_ref[...], p_ref[:, 2048:3072])
        dyl, dlgate = vjp(dgl)
        dyl_ref[...] = dyl
        dp_ref[:, 2048:3072] = dlgate.astype(BF16)

    row, const, gate = _mix_specs(rt, tiles_per_b)
    pblk = BS((rt, 5120), lambda i: (i, 1))
    nb = n // (rt * tiles_per_b)
    out_specs = [pblk, row(SSD_INNER), row(D), row(D), row(SSD_INNER), row(D), row(D), row(D), row(D), row(D),
                 BS((None, 1, D), lambda i: (i // tiles_per_b, 0, 0)), const((1, SSD_INNER)), const((1, 2 * D)),
                 const((1, D)), const((1, D))]
    out_shape = [S(dproj.shape, BF16), S((n, SSD_INNER), F32), S((n, D), F32), S((n, D), F32),
                 S((n, SSD_INNER), BF16), S((n, D), BF16), S((n, D), BF16), S((n, D), BF16), S((n, D), BF16),
                 S((n, D), BF16), S((nb, 1, D), F32), S((1, SSD_INNER), F32), S((1, 2 * D), F32), S((1, D), F32),
                 S((1, D), F32)]
    return pl.pallas_call(
        body, grid=(n // rt,), name="mix_bwd",
        in_specs=[row(SSD_INNER), row(D), pblk, row(D), gate,
                  const((SSD_INNER, D)), const((D, D)), const((D, D)), const((1, SSD_INNER)), const((1, 2 * D)),
                  const((1, D)), const((1, D)), row(D), BS(memory_space=pl.ANY)],
        out_specs=out_specs, out_shape=out_shape, input_output_aliases={13: 0},
        scratch_shapes=[pltpu.VMEM((rt, SSD_INNER), BF16)], compiler_params=_params(),
    )(y, ylru, proj, x, m4, wbs, wbl, wo, nw, bg, l1g, l1b, dx1, dproj)


def _mlp_step(x1, tgt, m4, w1, b1, w2, b2, l2g, l2b, rt, tiles_per_b):
    n = x1.shape[0]

    def body(x_ref, t_ref, sh_ref, sc_ref, gt_ref, w1_hbm, b1_ref, w2_hbm, b2_ref, lg_ref, lb_ref,
             loss_ref, dx_ref, h2_ref, da1_ref, r2_ref, dmlp_ref, dm_ref, db1_ref, db2_ref, dlg_ref, dlb_ref,
             w1_vm, w2_vm, sem):
        i = pl.program_id(0)

        @pl.when(i == 0)
        def _():
            c1 = pltpu.make_async_copy(w1_hbm, w1_vm, sem.at[0])
            c2 = pltpu.make_async_copy(w2_hbm, w2_vm, sem.at[1])
            c1.start()
            c2.start()
            for r in (loss_ref, db1_ref, db2_ref, dlg_ref, dlb_ref):
                r[...] = jnp.zeros_like(r)
            c1.wait()
            c2.wait()

        @pl.when(i % tiles_per_b == 0)
        def _():
            dm_ref[...] = jnp.zeros_like(dm_ref)

        x1v = x_ref[...]
        h2, vjp_h = jax.vjp(_modln, x1v, sh_ref[...], sc_ref[...])
        h2b = h2.astype(BF16)
        h2_ref[...] = h2b
        r = jnp.maximum(jnp.dot(h2b, w1_vm[...], preferred_element_type=F32) + b1_ref[...], 0.0)
        r2b = (r * r).astype(BF16)
        r2_ref[...] = r2b
        mlp = jnp.dot(r2b, w2_vm[...], preferred_element_type=F32) + b2_ref[...]
        x2, vjp_r = jax.vjp(_resln, x1v, mlp, gt_ref[...], lg_ref[...], lb_ref[...])
        diff = x2 - t_ref[...]
        loss_ref[...] += (0.5 / D) * jnp.sum(diff * diff)
        dxa, dmlp, dgt, dlg, dlb = vjp_r(diff * (1.0 / D))
        dlg_ref[...] += dlg
        dlb_ref[...] += dlb
        dm_ref[2:3, :] += dgt
        db2_ref[...] += _colsum(dmlp)
        dmlpb = dmlp.astype(BF16)
        dmlp_ref[...] = dmlpb
        da1 = lax.dot_general(dmlpb, w2_vm[...], (((1,), (1,)), ((), ())), preferred_element_type=F32) * (2.0 * r)
        db1_ref[...] += _colsum(da1)
        da1b = da1.astype(BF16)
        da1_ref[...] = da1b
        dh2 = lax.dot_general(da1b, w1_vm[...], (((1,), (1,)), ((), ())), preferred_element_type=F32)
        dxb, dsh, dsc = vjp_h(dh2)
        dx_ref[...] = dxa + dxb
        dm_ref[0:1, :] += dsh
        dm_ref[1:2, :] += dsc

    row = lambda w: BS((rt, w), lambda i: (i, 0))
    const = lambda shp: BS(shp, lambda i: (0,) * len(shp))
    mod = lambda k: BS((None, None, 1, D), lambda i: (i // tiles_per_b, k, 0, 0))
    nb = n // (rt * tiles_per_b)
    anyspec = BS(memory_space=pl.ANY)
    return pl.pallas_call(
        body, grid=(n // rt,), name="mlp_step",
        in_specs=[row(D), row(D), mod(3), mod(4), mod(5), anyspec, const((1, MLP_H)), anyspec, const((1, D)),
                  const((1, D)), const((1, D))],
        out_specs=[const((8, 128)), row(D), row(D), row(MLP_H), row(MLP_H), row(D),
                   BS((None, 3, D), lambda i: (i // tiles_per_b, 0, 0)), const((1, MLP_H)), const((1, D)),
                   const((1, D)), const((1, D))],
        out_shape=[S((8, 128), F32), S((n, D), F32), S((n, D), BF16), S((n, MLP_H), BF16), S((n, MLP_H), BF16),
                   S((n, D), BF16), S((nb, 3, D), F32), S((1, MLP_H), F32), S((1, D), F32), S((1, D), F32),
                   S((1, D), F32)],
        scratch_shapes=[pltpu.VMEM((D, MLP_H), BF16), pltpu.VMEM((MLP_H, D), BF16), pltpu.SemaphoreType.DMA((2,))],
        compiler_params=_params(),
    )(x1, tgt, m4, m4, m4, w1, b1, w2, b2, l2g, l2b)


def _pack_win(w):
    parts = []
    for g in range(SSD_G):
        parts += [w[:, 256 * g:256 * g + 256], w[:, 2048 + 128 * g:2176 + 128 * g], w[:, 4160 + 128 * g:4288 + 128 * g]]
    parts += [w[:, 3136:4160], w[:, 5184:7232], w[:, 7232:8256], w[:, 8256:10304], w[:, 3072:3136],
              jnp.zeros((w.shape[0], P_W - P_DT - 64), w.dtype)]
    return jnp.concatenate(parts, axis=1)


def _unpack_win(p):
    xs = [p[:, 512 * g:512 * g + 256] for g in range(SSD_G)]
    bs = [p[:, 512 * g + 256:512 * g + 384] for g in range(SSD_G)]
    cs = [p[:, 512 * g + 384:512 * g + 512] for g in range(SSD_G)]
    return jnp.concatenate(xs + bs + [p[:, P_DT:P_DT + 64], p[:, P_LRU:P_Z]] + cs + [p[:, P_Z:P_DT]], axis=1)


def _pack_conv(w):
    return jnp.stack([jnp.concatenate([w[:, 256 * g:256 * g + 256], w[:, 2048 + 128 * g:2176 + 128 * g],
                                       w[:, 3072 + 128 * g:3200 + 128 * g]], axis=1) for g in range(SSD_G)])


def _unpack_conv(p):
    r = p.shape[1]
    x = jnp.transpose(p[:, :, 0:256], (1, 0, 2)).reshape(r, 2048)
    b = jnp.transpose(p[:, :, 256:384], (1, 0, 2)).reshape(r, 1024)
    c = jnp.transpose(p[:, :, 384:512], (1, 0, 2)).reshape(r, 1024)
    return jnp.concatenate([x, b, c], axis=1)


def _pack_heads(v):
    p = jnp.transpose(v.reshape(2, SSD_G, 4), (1, 0, 2)).reshape(SSD_G, 1, 8)
    return jnp.pad(p, ((0, 0), (0, 0), (0, 120)))


def _unpack_heads(p):
    return jnp.transpose(p[:, 0, 0:8].reshape(SSD_G, 2, 4), (1, 0, 2)).reshape(2, 32)


def _pack_dt(dt):
    n = dt.shape[0]
    p = jnp.transpose(dt.reshape(n, 2, SSD_G, 4), (2, 0, 1, 3)).reshape(SSD_G, n, 8)
    return jnp.pad(p, ((0, 0), (0, 0), (0, 120)))


def _unpack_dt(p):
    n = p.shape[1]
    return jnp.transpose(p[:, :, 0:8].reshape(SSD_G, n, 2, 4), (1, 2, 0, 3)).reshape(n, 64)


def _tk(rows):
    return 512 if rows % 512 == 0 else (256 if rows % 256 == 0 else 128)


def _local_step(x, c, ctx, tgt, sm, wmod, win, wbs, wbl, wo, w1, w2):
    nb, t, _ = x.shape
    tc = ctx.shape[1]
    nl, ncx = nb * t, nb * tc
    rt = 256 if tc % 256 == 0 else 128
    rtm = 128
    xa = jnp.concatenate([x.reshape(nl, D), ctx.reshape(ncx, D)], axis=0)
    tgt2 = tgt.reshape(nl, D)
    cc = jnp.zeros((8, D), F32).at[0:nb].set(c).at[nb].set(sm["c_ctx"])
    m = _mod_fwd(cc, wmod, sm["b_mod"])
    m4 = m.reshape(8, N_MOD, 1, D)
    proj, h1 = _inproj_fwd(xa, m4, win, rtm, nl // rtm, t // rtm, nb)

    cw_s, cb_s = _pack_conv(sm["ssd_conv_w"]), _pack_conv(sm["ssd_conv_b"])
    dtb, alog = _pack_heads(sm["ssd_dt_bias"]), _pack_heads(sm["ssd_a_log"])
    drow = jnp.repeat(sm["ssd_d"].reshape(32), 64).reshape(SSD_G, 1, 256)
    dtg = _pack_dt(proj[:, P_DT:P_DT + 64])
    zst = jnp.zeros((nb, SSD_G, SSD_N, 256), F32)
    zl = jnp.zeros((nb, 2, D), F32)
    ssd_p = (cw_s, cb_s, dtb, alog, drow)
    lru_p = (sm["lru_conv_w"], sm["lru_conv_b"], sm["lru_wa"], sm["lru_ba"], sm["lru_wi"], sm["lru_bi"], sm["lru_lambda"])

    chsf, chsb, csf, csb = _ssd_fwd(proj, dtg, *ssd_p, zst, zst, nb, tc, tc, nl // tc, False)
    y, lhsf, lhsb, _, _ = _ssd_fwd(proj, dtg, *ssd_p, csf, csb, nb, t, GRID_W, 0, True)
    chf, chb, cfin = _lru_fwd(proj, *lru_p, zl, nb, tc, tc, nl // tc, False)
    ylru, lhf, lhb, _ = _lru_fwd(proj, *lru_p, cfin, nb, t, GRID_W, 0, True)
    xl = xa[0:nl]
    mix_w = (wbs, wbl, wo, sm["ssd_norm_w"], sm["b_gate"], sm["ln1_g"], sm["ln1_b"])
    x1 = _mix_fwd(y, ylru, proj, xl, m4, *mix_w, rtm, t // rtm)
    (loss, dx1, h2, da1, r2, dmlp, dm2, db1, db2, dl2g, dl2b) = _mlp_step(
        x1, tgt2, m4, w1, sm["b_mlp1"], w2, sm["b_mlp2"], sm["ln2_g"], sm["ln2_b"], rt, t // rt)

    dproj = jnp.zeros((nl + ncx, P_W), BF16)
    (dproj, dy, dylru, dxres, nrm, dbrs, gl, dbrl, mixb, dxm, dg1, dnw, dbg, dl1g, dl1b) = _mix_bwd(
        y, ylru, proj, xl, m4, *mix_w, dx1, dproj, rtm, t // rtm)
    big = {
        "w_br_ssd": _matmul_tn(nrm, dbrs, D, D, _tk(nl), "dw_br_ssd"),
        "w_br_lru": _matmul_tn(gl, dbrl, D, D, _tk(nl), "dw_br_lru"),
        "w_out": _matmul_tn(mixb, dxm, D, D, _tk(nl), "dw_out"),
        "w_mlp1": _matmul_tn(h2, da1, D, D, _tk(nl), "dw_mlp1"),
        "w_mlp2": _matmul_tn(r2, dmlp, D, D, _tk(nl), "dw_mlp2"),
    }
    (dproj, ddt_l, dh0f, dh0b, dcw_l, dcb_l, ddtb_l, dal_l, dd) = _ssd_bwd(
        proj, dtg, *ssd_p, lhsf, lhsb, dy, zst, zst, dproj, nb, t, GRID_W, 0, True)
    (dproj, ddt_c, _, _, dcw_c, dcb_c, ddtb_c, dal_c, _) = _ssd_bwd(
        proj, dtg, *ssd_p, chsf, chsb, None, dh0f, dh0b, dproj, nb, tc, tc, nl // tc, False)
    (dproj, dlh0, gcw_l, gcb_l, gwa_l, gwi_l, gba_l, gbi_l, glam_l) = _lru_bwd(
        proj, *lru_p, cfin, lhf, lhb, dylru, zl, dproj, nb, t, GRID_W, 0, True)
    (dproj, _, gcw_c, gcb_c, gwa_c, gwi_c, gba_c, gbi_c, glam_c) = _lru_bwd(
        proj, *lru_p, zl, chf, chb, None, dlh0, dproj, nb, tc, tc, nl // tc, False)
    ddt = jnp.concatenate([_unpack_dt(ddt_l), _unpack_dt(ddt_c)], axis=0).astype(BF16)
    dproj = lax.dynamic_update_slice(dproj, ddt, (0, P_DT))

    gx, dm1 = _inproj_bwd(xa, m4, win, dproj, rt, 0, nl // rt, t // rt, nb, True, dxres)
    (dmc,) = _inproj_bwd(xa, m4, win, dproj, rt, nl // rt, ncx // rt, ncx // rt, nb, False, None)
    dm = jnp.zeros((8, N_MOD, D), F32)
    dm = dm.at[0:nb].set(jnp.concatenate([dm1, dg1, dm2], axis=1)).at[nb, 0:2].set(dmc[0])
    dwmod, dbmod, dcc = _mod_bwd(cc, wmod, dm.reshape(8, N_MOD * D))

    big["w_mod"] = dwmod
    big["w_in"] = _matmul_tn(h1, dproj, D, 1152, _tk(nl + ncx), "dw_in")
    nq = D // LRU_CB
    small = {
        "c_ctx": dcc[nb],
        "b_mod": dbmod,
        "b_gate": dbg,
        "ssd_conv_w": _unpack_conv(dcw_l + dcw_c),
        "ssd_conv_b": _unpack_conv(dcb_l + dcb_c),
        "ssd_dt_bias": _unpack_heads(ddtb_l + ddtb_c),
        "ssd_a_log": _unpack_heads(dal_l + dal_c),
        "ssd_d": jnp.sum(dd.reshape(32, 64), axis=1),
        "ssd_norm_w": dnw,
        "lru_conv_w": jnp.transpose(gcw_l + gcw_c, (1, 0, 2)).reshape(4, D),
        "lru_conv_b": (gcb_l + gcb_c).reshape(1, D),
        "lru_wa": gwa_l + gwa_c,
        "lru_ba": jnp.transpose(gba_l + gba_c, (1, 0, 2)).reshape(2, D),
        "lru_wi": gwi_l + gwi_c,
        "lru_bi": jnp.transpose(gbi_l + gbi_c, (1, 0, 2)).reshape(2, D),
        "lru_lambda": jnp.transpose(glam_l + glam_c, (1, 0, 2)).reshape(2, D),
        "ln1_g": dl1g, "ln1_b": dl1b, "b_mlp1": db1, "b_mlp2": db2, "ln2_g": dl2g, "ln2_b": dl2b,
    }
    return loss[0, 0], gx.reshape(nb, t, D), big, small


_HBM = BS(memory_space=pl.ANY)


def _place():
    return lax.axis_index("x"), lax.axis_index("y"), lax.axis_index("c")


def _other_chips(x, y):
    return [(1 - x, y), (x, 1 - y), (1 - x, 1 - y)]


def _gather_chips(arrs):
    n = len(arrs)

    def body(*refs):
        ins, outs = refs[:n], refs[n:2 * n]
        ici_send, ici_recv, d2d_send, d2d_recv, loc_sems = refs[2 * n:]
        x, y, c = _place()
        me = 2 * x + y
        chips = _other_chips(x, y)
        halves = [(pl.ds(pl.multiple_of(c * (a.shape[0] // 2), 8), a.shape[0] // 2),
                   pl.ds(pl.multiple_of((1 - c) * (a.shape[0] // 2), 8), a.shape[0] // 2)) for a in arrs]
        drains = []
        for a in range(n):
            mine = halves[a][0]
            loc = pltpu.make_async_copy(ins[a], outs[a].at[me], loc_sems.at[a])
            loc.start()
            drains.append(loc.wait)
            for k, (px, py) in enumerate(chips):
                cp = pltpu.make_async_remote_copy(src_ref=ins[a].at[mine], dst_ref=outs[a].at[me, mine],
                                                  send_sem=ici_send.at[3 * a + k], recv_sem=ici_recv.at[3 * a + k],
                                                  device_id=(px, py, c), device_id_type=MESH)
                cp.start()
                drains.append(cp.wait_send)
        for a in range(n):
            mine = halves[a][0]
            for k, (px, py) in enumerate(chips):
                landed = outs[a].at[2 * px + py, mine]
                pltpu.make_async_remote_copy(src_ref=ins[a].at[mine], dst_ref=landed, send_sem=ici_send.at[3 * a + k],
                                             recv_sem=ici_recv.at[3 * a + k], device_id=(px, py, c),
                                             device_id_type=MESH).wait_recv()
                fw = pltpu.make_async_remote_copy(src_ref=landed, dst_ref=landed, send_sem=d2d_send.at[3 * a + k],
                                                  recv_sem=d2d_recv.at[3 * a + k], device_id=(x, y, 1 - c),
                                                  device_id_type=MESH)
                fw.start()
                drains.append(fw.wait_send)
        for a in range(n):
            theirs = halves[a][1]
            for k, (px, py) in enumerate(chips):
                got = outs[a].at[2 * px + py, theirs]
                pltpu.make_async_remote_copy(src_ref=got, dst_ref=got, send_sem=d2d_send.at[3 * a + k],
                                             recv_sem=d2d_recv.at[3 * a + k], device_id=(x, y, 1 - c),
                                             device_id_type=MESH).wait_recv()
        for wait in drains:
            wait()

    return pl.pallas_call(
        body, name="gather_weights", in_specs=[_HBM] * n, out_specs=[_HBM] * n,
        out_shape=[S((4,) + a.shape, a.dtype) for a in arrs],
        scratch_shapes=[pltpu.SemaphoreType.DMA((3 * n,))] * 4 + [pltpu.SemaphoreType.DMA((n,))],
    )(*arrs)


def _scatter_chips(arrs):
    n = len(arrs)

    def body(*refs):
        ins, outs = refs[:n], refs[n:2 * n]
        send_sems, recv_sems = refs[2 * n:]
        x, y, c = _place()
        chips = _other_chips(x, y)
        sends = []
        for a in range(n):
            for k, (px, py) in enumerate(chips):
                cp = pltpu.make_async_remote_copy(src_ref=ins[a].at[2 * px + py], dst_ref=outs[a].at[k],
                                                  send_sem=send_sems.at[a, k], recv_sem=recv_sems.at[a, k],
                                                  device_id=(px, py, c), device_id_type=MESH)
                cp.start()
                sends.append(cp)
        for cp in sends:
            cp.wait_recv()
        for cp in sends:
            cp.wait_send()

    return pl.pallas_call(
        body, name="scatter_grads", in_specs=[_HBM] * n, out_specs=[_HBM] * n,
        out_shape=[S((3,) + a.shape[1:], a.dtype) for a in arrs],
        scratch_shapes=[pltpu.SemaphoreType.DMA((n, 3)), pltpu.SemaphoreType.DMA((n, 3))],
    )(*arrs)


def _swap_halves(arrs):
    n = len(arrs)

    def body(*refs):
        ins, outs = refs[:n], refs[n:2 * n]
        send_sems, recv_sems = refs[2 * n:]
        x, y, c = _place()
        sends = []
        for a in range(n):
            hr = arrs[a].shape[1] // 2
            theirs = pl.ds(pl.multiple_of((1 - c) * hr, 8), hr)
            for q in range(4):
                cp = pltpu.make_async_remote_copy(src_ref=ins[a].at[q, theirs], dst_ref=outs[a].at[q],
                                                  send_sem=send_sems.at[4 * a + q], recv_sem=recv_sems.at[4 * a + q],
                                                  device_id=(x, y, 1 - c), device_id_type=MESH)
                cp.start()
                sends.append(cp)
        for cp in sends:
            cp.wait_recv()
        for cp in sends:
            cp.wait_send()

    return pl.pallas_call(
        body, name="swap_halves", in_specs=[_HBM] * n, out_specs=[_HBM] * n,
        out_shape=[S((4, a.shape[1] // 2, a.shape[2]), a.dtype) for a in arrs],
        scratch_shapes=[pltpu.SemaphoreType.DMA((4 * n,)), pltpu.SemaphoreType.DMA((4 * n,))],
    )(*arrs)


def _allreduce_small(v):
    def body(in_ref, out_ref, r0, r1, r2, send_sems, recv_sems):
        x, y, c = _place()
        src = in_ref
        for s, (buf, peer) in enumerate(((r0, (x, y, 1 - c)), (r1, (x, 1 - y, c)), (r2, (1 - x, y, c)))):
            cp = pltpu.make_async_remote_copy(src_ref=src, dst_ref=buf, send_sem=send_sems.at[s], recv_sem=recv_sems.at[s],
                                              device_id=peer, device_id_type=MESH)
            cp.start()
            cp.wait()
            out_ref[...] = src[...] + buf[...]
            src = out_ref

    vm = BS(memory_space=pltpu.VMEM)
    return pl.pallas_call(
        body, name="allreduce_small", in_specs=[vm], out_specs=vm, out_shape=S(v.shape, F32),
        scratch_shapes=[pltpu.VMEM(v.shape, F32)] * 3 + [pltpu.SemaphoreType.DMA((3,)), pltpu.SemaphoreType.DMA((3,))],
        compiler_params=_params(),
    )(v)


def _swap_cores(arrs):
    n = len(arrs)

    def body(*refs):
        ins, outs = refs[:n], refs[n:2 * n]
        send_sems, recv_sems = refs[2 * n:]
        x, y, c = _place()
        sends = []
        for a in range(n):
            cp = pltpu.make_async_remote_copy(src_ref=ins[a], dst_ref=outs[a], send_sem=send_sems.at[a],
                                              recv_sem=recv_sems.at[a], device_id=(x, y, 1 - c), device_id_type=MESH)
            cp.start()
            sends.append(cp)
        for cp in sends:
            cp.wait_recv()
        for cp in sends:
            cp.wait_send()

    return pl.pallas_call(
        body, name="swap_cores", in_specs=[_HBM] * n, out_specs=[_HBM] * n,
        out_shape=[S(a.shape, a.dtype) for a in arrs],
        scratch_shapes=[pltpu.SemaphoreType.DMA((n,)), pltpu.SemaphoreType.DMA((n,))],
    )(*arrs)


def _row_tile(r, c=128):
    tr = 256 if c <= 1024 else (128 if c <= 2048 else 64)
    return tr if r % tr == 0 else r


def _sum_half(own, sib, core, name):
    _, r, c = own.shape
    hr = r // 2
    tr = _row_tile(hr, c)
    nbk = hr // tr

    def body(core_ref, o_ref, s_ref, p_ref, pb_ref):
        p = o_ref[...] + s_ref[...]
        p_ref[...] = p
        pb_ref[...] = p.astype(BF16)

    blk = BS((None, tr, c), lambda q, i, cr: (q, i, 0))
    return pl.pallas_call(
        body, name=name, out_shape=[S((4, hr, c), F32), S((4, hr, c), BF16)],
        grid_spec=pltpu.PrefetchScalarGridSpec(
            num_scalar_prefetch=1, grid=(4, nbk),
            in_specs=[BS((None, tr, c), lambda q, i, cr: (q, cr[0] * nbk + i, 0)), blk], out_specs=[blk, blk]),
        compiler_params=_params(),
    )(core, own, sib)


def _sum4(part, recv, chip, name):
    _, r, c = part.shape
    tr = _row_tile(r, c)

    def body(chip_ref, o_ref, r_ref, out_ref):
        acc = o_ref[...]
        for k in range(3):
            acc = acc + r_ref[k].astype(F32)
        out_ref[...] = acc

    return pl.pallas_call(
        body, name=name, out_shape=S((r, c), F32),
        grid_spec=pltpu.PrefetchScalarGridSpec(
            num_scalar_prefetch=1, grid=(r // tr,),
            in_specs=[BS((None, tr, c), lambda i, ch: (ch[0], i, 0)), BS((3, tr, c), lambda i, ch: (0, i, 0))],
            out_specs=BS((tr, c), lambda i, ch: (i, 0))),
        compiler_params=_params(),
    )(chip, part, recv)


def _adam_math(w, g, m, v):
    m = ADAM_B1 * m + (1.0 - ADAM_B1) * g
    v = ADAM_B2 * v + (1.0 - ADAM_B2) * (g * g)
    m_hat = m / (1.0 - ADAM_B1 ** ADAM_STEP)
    v_hat = v / (1.0 - ADAM_B2 ** ADAM_STEP)
    return -ADAM_LR * (m_hat / (jnp.sqrt(v_hat) + ADAM_EPS) + ADAM_WD * w), m, v


def _adam_halves(mine, other, w, m, v, core, name):
    r, c = w.shape
    tr = _row_tile(r // 2, c)
    nbk = (r // 2) // tr

    def body(core_ref, a_ref, b_ref, w_ref, m_ref, v_ref, g_ref, d_ref, nm_ref, nv_ref):
        g = jnp.where(pl.program_id(0) // nbk == core_ref[0], a_ref[...], b_ref[...])
        g_ref[...] = g
        d_ref[...], nm_ref[...], nv_ref[...] = _adam_math(w_ref[...], g, m_ref[...], v_ref[...])

    spec = BS((tr, c), lambda i, cr: (i, 0))
    half = BS((tr, c), lambda i, cr: (i % nbk, 0))
    return pl.pallas_call(
        body, name=name, out_shape=[S((r, c), F32)] * 4,
        grid_spec=pltpu.PrefetchScalarGridSpec(num_scalar_prefetch=1, grid=(r // tr,), in_specs=[half, half] + [spec] * 3,
                                               out_specs=[spec] * 4),
        compiler_params=_params(),
    )(core, mine, other, w, m, v)


def _adam_flat(g, w, m, v):
    r = w.shape[0]
    tr = _row_tile(r)

    def body(g_ref, w_ref, m_ref, v_ref, d_ref, nm_ref, nv_ref):
        d_ref[...], nm_ref[...], nv_ref[...] = _adam_math(w_ref[...], g_ref[...], m_ref[...], v_ref[...])

    spec = BS((tr, 128), lambda i: (i, 0))
    return pl.pallas_call(
        body, grid=(r // tr,), name="adam_small", in_specs=[spec] * 4, out_specs=[spec] * 3,
        out_shape=[S((r, 128), F32)] * 3, compiler_params=_params(),
    )(g, w, m, v)


def _flatten(arrs, rows_mult=256):
    flat = jnp.concatenate([a.reshape(-1) for a in arrs])
    n = flat.shape[0]
    rows = -(-n // 128)
    rows = -(-rows // rows_mult) * rows_mult
    return jnp.pad(flat, (0, rows * 128 - n)).reshape(rows, 128)


def _unflatten(flat, shapes):
    flat = flat.reshape(-1)
    out, o = [], 0
    for shp in shapes:
        n = int(np.prod(shp))
        out.append(flat[o:o + n].reshape(shp))
        o += n
    return out


BIG = ["w_mod", "w_in", "w_br_ssd", "w_br_lru", "w_out", "w_mlp1", "w_mlp2"]
COL_SHARDED = {"w_mod": N_MOD * D, "w_in": IN_COLS, "w_mlp1": MLP_H}
SMALL_SHARDED = ["ssd_conv_w", "lru_conv_w", "lru_ba", "lru_bi", "lru_lambda"]
WEIGHTS = ['c_ctx', 'w_mod', 'b_mod', 'w_in', 'b_gate', 'ssd_conv_w', 'ssd_conv_b', 'ssd_dt_bias', 'ssd_a_log', 'ssd_d',
           'ssd_norm_w', 'lru_conv_w', 'lru_conv_b', 'lru_wa', 'lru_ba', 'lru_wi', 'lru_bi', 'lru_lambda', 'w_br_ssd',
           'w_br_lru', 'w_out', 'ln1_g', 'ln1_b', 'w_mlp1', 'b_mlp1', 'w_mlp2', 'b_mlp2', 'ln2_g', 'ln2_b']
SMALL = [n for n in WEIGHTS if n not in BIG]


def _full_from_chips(g4, name):
    if name in COL_SHARDED:
        return jnp.transpose(g4, (1, 0, 2)).reshape(g4.shape[1], 4 * g4.shape[2])
    return g4.reshape(4 * g4.shape[1], g4.shape[2])


def _chips_from_full(full, name):
    if name in COL_SHARDED:
        r, c = full.shape
        return jnp.transpose(full.reshape(r, 4, c // 4), (1, 0, 2))
    return full.reshape(4, full.shape[0] // 4, full.shape[1])


def kernel(x, c, ctx, c_ctx, w_mod, b_mod, w_in, b_gate, ssd_conv_w, ssd_conv_b, ssd_dt_bias, ssd_a_log, ssd_d, ssd_norm_w, lru_conv_w, lru_conv_b, lru_wa, lru_ba, lru_wi, lru_bi, lru_lambda, w_br_ssd, w_br_lru, w_out, ln1_g, ln1_b, w_mlp1, b_mlp1, w_mlp2, b_mlp2, ln2_g, ln2_b, loss_target, m_c_ctx, m_w_mod, m_b_mod, m_w_in, m_b_gate, m_ssd_conv_w, m_ssd_conv_b, m_ssd_dt_bias, m_ssd_a_log, m_ssd_d, m_ssd_norm_w, m_lru_conv_w, m_lru_conv_b, m_lru_wa, m_lru_ba, m_lru_wi, m_lru_bi, m_lru_lambda, m_w_br_ssd, m_w_br_lru, m_w_out, m_ln1_g, m_ln1_b, m_w_mlp1, m_b_mlp1, m_w_mlp2, m_b_mlp2, m_ln2_g, m_ln2_b, v_c_ctx, v_w_mod, v_b_mod, v_w_in, v_b_gate, v_ssd_conv_w, v_ssd_conv_b, v_ssd_dt_bias, v_ssd_a_log, v_ssd_d, v_ssd_norm_w, v_lru_conv_w, v_lru_conv_b, v_lru_wa, v_lru_ba, v_lru_wi, v_lru_bi, v_lru_lambda, v_w_br_ssd, v_w_br_lru, v_w_out, v_ln1_g, v_ln1_b, v_w_mlp1, v_b_mlp1, v_w_mlp2, v_b_mlp2, v_ln2_g, v_ln2_b):
    given = dict(locals())
    w = {n: given[n] for n in WEIGHTS}
    mom = {n: given["m_" + n] for n in WEIGHTS}
    var = {n: given["v_" + n] for n in WEIGHTS}
    chip = 2 * lax.axis_index("x") + lax.axis_index("y")

    shard2d = {n: w[n].reshape(w[n].shape[-2:]) for n in BIG}
    small_pack = _flatten([w[n] for n in SMALL_SHARDED], rows_mult=16)
    gathered = _gather_chips([shard2d[n].astype(BF16) for n in BIG] + [small_pack])
    full = {n: _full_from_chips(g, n) for n, g in zip(BIG, gathered[:-1])}
    full["w_in"] = _pack_win(full["w_in"])
    per_chip = [_unflatten(gathered[-1][q], [w[n].shape for n in SMALL_SHARDED]) for q in range(4)]
    sm = {n: jnp.concatenate([per_chip[q][i] for q in range(4)], axis=-1) for i, n in enumerate(SMALL_SHARDED)}
    for n in SMALL:
        if n not in sm:
            sm[n] = w[n]
    sm = {n: (a.reshape(a.shape[1:]) if a.ndim >= 3 else a) for n, a in sm.items()}

    loss, gx, gbig, gsmall = _local_step(x, c, ctx, loss_target, sm, *[full[n] for n in BIG])
    gbig["w_in"] = _unpack_win(gbig["w_in"])

    core_id = lax.axis_index("c").astype(jnp.int32).reshape(1)
    chip_id = chip.astype(jnp.int32).reshape(1)
    slabs = [_chips_from_full(gbig[n], n) for n in BIG]
    sib = _swap_halves(slabs)
    halves = [_sum_half(s, o, core_id, "half_" + n) for n, s, o in zip(BIG, slabs, sib)]
    recv = _scatter_chips([h[1] for h in halves])
    mine = [_sum4(h[0], r, chip_id, "sum_" + n) for n, h, r in zip(BIG, halves, recv)]
    other = _swap_cores(mine)
    out = {}
    for n, pa, pb in zip(BIG, mine, other):
        shp = shard2d[n].shape
        res = _adam_halves(pa, pb, shard2d[n], mom[n].reshape(shp), var[n].reshape(shp), core_id, "adam_" + n)
        out[n] = [r.reshape(w[n].shape) for r in res]

    full_shapes = [gsmall[n].shape for n in SMALL]
    gs_all = _unflatten(_allreduce_small(_flatten([gsmall[n] for n in SMALL])), full_shapes)
    gs = {}
    for n, g in zip(SMALL, gs_all):
        if n in SMALL_SHARDED:
            width = w[n].shape[-1]
            g = lax.dynamic_slice_in_dim(g, chip * width, width, axis=g.ndim - 1)
        gs[n] = g.reshape(w[n].shape)
    shapes = [w[n].shape for n in SMALL]
    d_s, m_s, v_s = _adam_flat(_flatten([gs[n] for n in SMALL]), _flatten([w[n] for n in SMALL]),
                               _flatten([mom[n] for n in SMALL]), _flatten([var[n] for n in SMALL]))
    for n, d_, m_, v_ in zip(SMALL, _unflatten(d_s, shapes), _unflatten(m_s, shapes), _unflatten(v_s, shapes)):
        out[n] = [gs[n], d_, m_, v_]

    loss = lax.psum(loss, ("x", "y", "c"))
    return (loss, gx, *[out[n][0] for n in WEIGHTS], *[out[n][1] for n in WEIGHTS], *[out[n][2] for n in WEIGHTS],
            *[out[n][3] for n in WEIGHTS])
```

```python
import functools

import numpy as np
import jax
import jax.numpy as jnp
from jax import lax
from jax.experimental import pallas as pl
from jax.experimental.pallas import tpu as pltpu

F32, BF16 = jnp.float32, jnp.bfloat16
S = jax.ShapeDtypeStruct
BS = pl.BlockSpec
MESH = pl.DeviceIdType.MESH

D = 1024
GRID_W = 64
SSD_INNER, SSD_G, SSD_N, SSD_L = 2048, 8, 128, 128
SSD_GW = 512
MLP_H = 4096
N_MOD = 6
ALPHA = 2.0 ** 0.25
LN_EPS, RMS_EPS = 1e-6, 1e-5
LRU_C = 8.0
P_XBC, P_LRU, P_Z, P_LG, P_MG, P_DT, P_W = 0, 4096, 5120, 7168, 8192, 10240, 10368
P_CB = 3456
IN_COLS = 10304
LRU_CB = 256
ADAM_LR, ADAM_B1, ADAM_B2, ADAM_EPS, ADAM_WD, ADAM_STEP = 0.001, 0.9, 0.999, 1e-08, 0.01, 10
VMEM_LIMIT = 56 * 2 ** 20


def _params(**kw):
    return pltpu.CompilerParams(vmem_limit_bytes=VMEM_LIMIT, **kw)


def _dot(a, b):
    return jnp.dot(a.astype(BF16), b.astype(BF16), preferred_element_type=F32)


def _dot_nt(a, b):
    return lax.dot_general(a.astype(BF16), b.astype(BF16), (((1,), (1,)), ((), ())), preferred_element_type=F32)


def _dot_tn(a, b):
    return lax.dot_general(a.astype(BF16), b.astype(BF16), (((0,), (0,)), ((), ())), preferred_element_type=F32)


@jax.custom_vjp
def _mm(a, b):
    return _dot(a, b)


def _cast_pair(a, b):
    return a.astype(BF16), b.astype(BF16)


def _mm_f(a, b):
    r = _cast_pair(a, b)
    return _dot(*r), r


def _mm_b(r, g):
    g = g.astype(BF16)
    return _dot_nt(g, r[1]), _dot_tn(r[0], g)


_mm.defvjp(_mm_f, _mm_b)


@jax.custom_vjp
def _mm_nt(a, b):
    return _dot_nt(a, b)


def _mm_nt_f(a, b):
    r = _cast_pair(a, b)
    return _dot_nt(*r), r


def _mm_nt_b(r, g):
    g = g.astype(BF16)
    return _dot(g, r[1]), _dot_tn(g, r[0])


_mm_nt.defvjp(_mm_nt_f, _mm_nt_b)


@jax.custom_vjp
def _mm_tn(a, b):
    return _dot_tn(a, b)


def _mm_tn_f(a, b):
    r = _cast_pair(a, b)
    return _dot_tn(*r), r


def _mm_tn_b(r, g):
    g = g.astype(BF16)
    return _dot_nt(r[1], g), _dot(r[0], g)


_mm_tn.defvjp(_mm_tn_f, _mm_tn_b)

def _split3(v):
    h = v.astype(BF16)
    r = v - h.astype(F32)
    m = r.astype(BF16)
    return h, m, (r - m.astype(F32)).astype(BF16)


def _sel_dot(sel, v, dims):
    sel_first = dims[0] == "s"
    dn = {"sv": (((1,), (0,)), ((), ())), "sTv": (((0,), (0,)), ((), ())), "vs": (((1,), (0,)), ((), ())),
          "vsT": (((1,), (1,)), ((), ()))}[dims]
    out = None
    for part in _split3(v):
        a, b = (sel, part) if sel_first else (part, sel)
        term = lax.dot_general(a, b, dn, preferred_element_type=F32)
        out = term if out is None else out + term
    return out


@jax.custom_vjp
def _cum_mm(tri, v):
    return _sel_dot(tri, v, "sv")


_cum_mm.defvjp(lambda tri, v: (_sel_dot(tri, v, "sv"), tri),
               lambda tri, g: (jnp.zeros_like(tri), _sel_dot(tri, g, "sTv")))


@jax.custom_vjp
def _xp_mm(v, e):
    return _sel_dot(e, v, "vs")


_xp_mm.defvjp(lambda v, e: (_sel_dot(e, v, "vs"), e),
              lambda e, g: (_sel_dot(e, g, "vsT"), jnp.zeros_like(e)))


def _sigmoid(x):
    return 1.0 / (1.0 + jnp.exp(-x))


def _silu(x):
    return x * _sigmoid(x)


def _dsilu(x):
    s = _sigmoid(x)
    return s * (1.0 + x * (1.0 - s))


def _softplus(x):
    return jnp.maximum(x, 0.0) + jnp.log1p(jnp.exp(-jnp.abs(x)))


def _gelu(x):
    return 0.5 * x * (1.0 + jnp.tanh(0.7978845608028654 * (x + 0.044715 * x * x * x)))


def _ln(x):
    mu = jnp.mean(x, axis=-1, keepdims=True)
    xc = x - mu
    var = jnp.mean(xc * xc, axis=-1, keepdims=True)
    return xc * lax.rsqrt(var + LN_EPS)


def _modln(x, shift, scale):
    return _ln(x) * (1.0 + scale) + shift


def _resln(x, sub, gate, g, b):
    return _ln(ALPHA * x + gate * sub) * g + b


def _grms(y, z, w):
    u = y * _silu(z)
    return u * lax.rsqrt(jnp.mean(u * u, axis=-1, keepdims=True) + RMS_EPS) * w


def _colsum(v):
    return jnp.sum(v, axis=0, keepdims=True)


def _conv_taps(width, period):
    def masks(rows):
        pos = lax.broadcasted_iota(jnp.int32, (rows, width), 0) & (period - 1)
        return [jnp.logical_and(pos + (k - 2) >= 0, pos + (k - 2) < period) for k in range(4)]
    return masks


def _conv_fwd(raw, w, b, masks):
    rows = raw.shape[0]
    pre = b + raw * w[2:3, :]
    for k in (0, 1, 3):
        sh = pltpu.roll(raw, (2 - k) % rows, 0)
        pre = pre + jnp.where(masks[k], sh, 0.0) * w[k:k + 1, :]
    return pre


def _for_rows(t, rb, fn):
    n = t // rb
    unroll = 4 if n % 4 == 0 else 1

    def step(i, carry):
        for u in range(unroll):
            fn(pl.multiple_of((i * unroll + u) * rb, rb))
        return carry

    lax.fori_loop(0, n // unroll, step, 0)


def _loop_unrolled(n, unroll, body, init):
    def step(i, carry):
        for u in range(unroll):
            carry = body(i * unroll + u, carry)
        return carry

    return lax.fori_loop(0, n // unroll, step, init)


def _conv_bwd(dpre, raw, w, masks):
    rows = raw.shape[0]
    draw = dpre * w[2:3, :]
    dws = []
    for k in range(4):
        if k == 2:
            dws.append(_colsum(dpre * raw))
            continue
        sh = pltpu.roll(raw, (2 - k) % rows, 0)
        dws.append(_colsum(dpre * jnp.where(masks[k], sh, 0.0)))
        back = pltpu.roll(jnp.where(masks[k], dpre, 0.0) * w[k:k + 1, :], (k - 2) % rows, 0)
        draw = draw + back
    return draw, jnp.concatenate(dws, axis=0), _colsum(dpre)


def _mod_fwd(cc, wmod, bmod):
    def body(cc_ref, w_ref, b_ref, o_ref):
        o_ref[...] = _dot(_silu(cc_ref[...]), w_ref[...]) + b_ref[...]

    return pl.pallas_call(
        body, grid=(N_MOD,), name="mod_fwd",
        in_specs=[BS((8, D), lambda j: (0, 0)), BS((D, D), lambda j: (0, j)), BS((1, D), lambda j: (0, j))],
        out_specs=BS((8, D), lambda j: (0, j)), out_shape=S((8, N_MOD * D), F32), compiler_params=_params(),
    )(cc, wmod, bmod)


def _mod_bwd(cc, wmod, dm):
    def body(cc_ref, w_ref, dm_ref, dw_ref, db_ref, dcc_ref):
        j = pl.program_id(0)
        c = cc_ref[...]
        dmv = dm_ref[...]
        dw_ref[...] = _dot_tn(_silu(c), dmv)
        db_ref[...] = _colsum(dmv)

        @pl.when(j == 0)
        def _():
            dcc_ref[...] = jnp.zeros_like(dcc_ref)

        dcc_ref[...] += _dot_nt(dmv, w_ref[...]) * _dsilu(c)

    return pl.pallas_call(
        body, grid=(N_MOD,), name="mod_bwd",
        in_specs=[BS((8, D), lambda j: (0, 0)), BS((D, D), lambda j: (0, j)), BS((8, D), lambda j: (0, j))],
        out_specs=[BS((D, D), lambda j: (0, j)), BS((1, D), lambda j: (0, j)), BS((8, D), lambda j: (0, 0))],
        out_shape=[S((D, N_MOD * D), F32), S((1, N_MOD * D), F32), S((8, D), F32)], compiler_params=_params(),
    )(cc, wmod, dm)


def _inproj_fwd(xa, m4, win, rt, n_lat_tiles, tiles_per_b, ctx_row, name):
    n_tiles = xa.shape[0] // rt

    def mrow(i):
        return jnp.where(i < n_lat_tiles, i // tiles_per_b, ctx_row)

    def body(x_ref, sh_ref, sc_ref, w_hbm, p_ref, h_ref, w_vm, sem):
        @pl.when(pl.program_id(0) == 0)
        def _():
            cp = pltpu.make_async_copy(w_hbm, w_vm, sem)
            cp.start()
            cp.wait()

        hb = _modln(x_ref[...], sh_ref[...], sc_ref[...]).astype(BF16)
        h_ref[...] = hb
        for j in range(P_W // P_CB):
            sl = slice(j * P_CB, (j + 1) * P_CB)
            p_ref[:, sl] = jnp.dot(hb, w_vm[:, sl], preferred_element_type=F32)

    return pl.pallas_call(
        body, grid=(n_tiles,), name=name,
        in_specs=[BS((rt, D), lambda i: (i, 0)),
                  BS((None, None, 1, D), lambda i: (mrow(i), 0, 0, 0)),
                  BS((None, None, 1, D), lambda i: (mrow(i), 1, 0, 0)),
                  BS(memory_space=pl.ANY)],
        out_specs=[BS((rt, P_W), lambda i: (i, 0)), BS((rt, D), lambda i: (i, 0))],
        out_shape=[S((xa.shape[0], P_W), F32), S((xa.shape[0], D), BF16)],
        scratch_shapes=[pltpu.VMEM((D, P_W), BF16), pltpu.SemaphoreType.DMA(())], compiler_params=_params(),
    )(xa, m4, m4, win)


def _inproj_bwd(xa, m4, win, dproj, rt, tile0, n_tiles, tiles_per_b, ctx_row, latent, dxres):
    def mrow(i):
        return (i // tiles_per_b) if latent else ctx_row

    def body(x_ref, sh_ref, sc_ref, dp_ref, w_hbm, *rest):
        if latent:
            dxr_ref, gx_ref, dm_ref, w_vm, sem = rest
        else:
            dm_ref, w_vm, sem = rest
        i = pl.program_id(0)

        @pl.when(i == 0)
        def _():
            cp = pltpu.make_async_copy(w_hbm, w_vm, sem)
            cp.start()
            cp.wait()

        dh = lax.dot_general(dp_ref[...], w_vm[...], (((1,), (1,)), ((), ())), preferred_element_type=F32)
        _, vjp = jax.vjp(_modln, x_ref[...], sh_ref[...], sc_ref[...])
        dx, dsh, dsc = vjp(dh)
        if latent:
            gx_ref[...] = dx + dxr_ref[...]

        @pl.when(i % tiles_per_b == 0)
        def _():
            dm_ref[...] = jnp.zeros_like(dm_ref)

        dm_ref[0:1, :] += dsh
        dm_ref[1:2, :] += dsc

    nb = n_tiles // tiles_per_b
    in_specs = [BS((rt, D), lambda i: (tile0 + i, 0)),
                BS((None, None, 1, D), lambda i: (mrow(i), 0, 0, 0)),
                BS((None, None, 1, D), lambda i: (mrow(i), 1, 0, 0)),
                BS((rt, P_W), lambda i: (tile0 + i, 0)),
                BS(memory_space=pl.ANY)]
    args = [xa, m4, m4, dproj, win]
    dm_spec = BS((None, 2, D), lambda i: (i // tiles_per_b, 0, 0))
    if latent:
        in_specs.append(BS((rt, D), lambda i: (i, 0)))
        args.append(dxres)
        out_specs = [BS((rt, D), lambda i: (i, 0)), dm_spec]
        out_shape = [S((n_tiles * rt, D), F32), S((nb, 2, D), F32)]
    else:
        out_specs = [dm_spec]
        out_shape = [S((nb, 2, D), F32)]
    return pl.pallas_call(
        body, grid=(n_tiles,), name="inproj_bwd_lat" if latent else "inproj_bwd_ctx",
        in_specs=in_specs, out_specs=out_specs, out_shape=out_shape,
        scratch_shapes=[pltpu.VMEM((D, P_W), BF16), pltpu.SemaphoreType.DMA(())],
        compiler_params=_params(),
    )(*args)


def _matmul_tn(a, b, tm, tn, tk, name):
    k, m = a.shape
    n = b.shape[1]

    def body(a_ref, b_ref, o_ref):
        @pl.when(pl.program_id(2) == 0)
        def _():
            o_ref[...] = jnp.zeros_like(o_ref)

        o_ref[...] += lax.dot_general(a_ref[...], b_ref[...], (((0,), (0,)), ((), ())), preferred_element_type=F32)

    return pl.pallas_call(
        body, grid=(m // tm, n // tn, k // tk), name=name,
        in_specs=[BS((tk, tm), lambda i, j, kk: (kk, i)), BS((tk, tn), lambda i, j, kk: (kk, j))],
        out_specs=BS((tm, tn), lambda i, j, kk: (i, j)), out_shape=S((m, n), F32), compiler_params=_params(),
    )(a, b)


def _matmul_tn2(a1, b1, a2, b2, tm, tn, tk, name):
    k1, m = a1.shape
    n = b1.shape[1]
    n1, n2 = k1 // tk, a2.shape[0] // tk

    def body(a1_ref, b1_ref, a2_ref, b2_ref, o_ref):
        kk = pl.program_id(2)

        @pl.when(kk == 0)
        def _():
            o_ref[...] = jnp.zeros_like(o_ref)

        @pl.when(kk < n1)
        def _():
            o_ref[...] += lax.dot_general(a1_ref[...], b1_ref[...], (((0,), (0,)), ((), ())), preferred_element_type=F32)

        @pl.when(kk >= n1)
        def _():
            o_ref[...] += lax.dot_general(a2_ref[...], b2_ref[...], (((0,), (0,)), ((), ())), preferred_element_type=F32)

    first = lambda kk: jnp.minimum(kk, n1 - 1)
    second = lambda kk: jnp.maximum(kk - n1, 0)
    return pl.pallas_call(
        body, grid=(m // tm, n // tn, n1 + n2), name=name,
        in_specs=[BS((tk, tm), lambda i, j, kk: (first(kk), i)), BS((tk, tn), lambda i, j, kk: (first(kk), j)),
                  BS((tk, tm), lambda i, j, kk: (second(kk), i)), BS((tk, tn), lambda i, j, kk: (second(kk), j))],
        out_specs=BS((tm, tn), lambda i, j, kk: (i, j)), out_shape=S((m, n), F32), compiler_params=_params(),
    )(a1, b1, a2, b2)


def _ssd_consts(heads_per_tile):
    n = SSD_L
    ii = lax.broadcasted_iota(jnp.int32, (n, n), 0)
    jj = lax.broadcasted_iota(jnp.int32, (n, n), 1)
    er = lax.broadcasted_iota(jnp.int32, (128, 256), 0)
    ec = lax.broadcasted_iota(jnp.int32, (128, 256), 1) >> 6
    lane = lax.broadcasted_iota(jnp.int32, (1, 128 * heads_per_tile), 1) >> 6
    per_dir = []
    for d in (0, 1):
        mask = (jj >= ii) if d else (jj <= ii)
        per_dir.append((mask, mask.astype(BF16), (er == ec + 4 * d).astype(BF16)))
    return per_dir, [(lane == h).astype(F32) for h in range(2 * heads_per_tile)]


def _ssd_chunk(x, bm, cm, dtc, dtx, alog, hst, consts, hmasks, rev):
    n = SSD_L
    mask, tri, e = consts
    cum = _cum_mm(tri, dtc * (-jnp.exp(alog)))
    cum_x = _xp_mm(cum, e)
    tot_x = cum_x[0:1, :] if rev else cum_x[n - 1:n, :]
    xd = x * dtx
    hn = jnp.exp(tot_x) * hst + _mm_tn(bm, xd * jnp.exp(tot_x - cum_x))
    if cm is None:
        return hn
    cum_t = cum.T
    cb = _mm_nt(cm, bm)
    if len(hmasks) == 4:
        y = jnp.exp(cum_x) * _mm(cm, hst)
        for h in range(4):
            k = 4 * rev + h
            decay = jnp.exp(jnp.where(mask, cum[:, k:k + 1] - cum_t[k:k + 1, :], -1e30))
            y = y + _mm(cb * decay, xd * hmasks[h])
        return y, hn
    pairs = []
    for p in range(2):
        xdp = xd[:, 128 * p:128 * p + 128]
        yp = None
        for hh in range(2):
            k = 4 * rev + 2 * p + hh
            decay = jnp.exp(jnp.where(mask, cum[:, k:k + 1] - cum_t[k:k + 1, :], -1e30))
            term = _mm(cb * decay, xdp * hmasks[hh])
            yp = term if yp is None else yp + term
        pairs.append(yp)
    return jnp.exp(cum_x) * _mm(cm, hst) + jnp.concatenate(pairs, axis=1), hn


def _ssd_fwd(proj, dtg, cw, cb, dtb, alog, drow, h0f, h0b, nb, t, period, blk0, need_y):
    nc = t // SSD_L
    rb = period
    assert t % rb == 0
    unroll = 4 if nc % 4 == 0 else (2 if nc % 2 == 0 else 1)
    masks_of = _conv_taps(SSD_GW, period)

    def body(p_ref, dt_ref, cw_ref, cb_ref, dtb_ref, al_ref, d_ref, h0f_ref, h0b_ref, *rest):
        if need_y:
            y_ref, hsf_ref, hsb_ref, sf_ref, sb_ref, act, dts, dtxs = rest
        else:
            hsf_ref, hsb_ref, sf_ref, sb_ref, act, dts, dtxs = rest
        masks = masks_of(rb)
        per_dir, hmasks = _ssd_consts(2)

        def prologue(r0):
            rows = pl.ds(r0, rb)
            a = _silu(_conv_fwd(p_ref[rows, :], cw_ref[...], cb_ref[...], masks))
            act[rows, :] = a
            if need_y:
                y_ref[rows, :] = d_ref[...] * a[:, 0:256]
            dtv = _softplus(dt_ref[rows, :] + dtb_ref[...])
            dts[rows, :] = dtv
            for d in (0, 1):
                dtxs[rows, 256 * d:256 * d + 256] = _xp_mm(dtv, per_dir[d][2])

        _for_rows(t, rb, prologue)
        al = al_ref[...]

        def chunk(ci, carry):
            out = []
            for d, hst, hs_ref in ((0, carry[0], hsf_ref), (1, carry[1], hsb_ref)):
                c = (nc - 1 - ci) if d else ci
                r0 = pl.multiple_of(c * SSD_L, SSD_L)
                a = act[pl.ds(r0, SSD_L), :]
                hs_ref[c] = hst
                res = _ssd_chunk(a[:, 0:256], a[:, 256:384], a[:, 384:512] if need_y else None,
                                 dts[pl.ds(r0, SSD_L), :], dtxs[pl.ds(r0, SSD_L), 256 * d:256 * d + 256], al, hst,
                                 per_dir[d], hmasks, d)
                if need_y:
                    y_ref[pl.ds(r0, SSD_L), :] += res[0]
                    res = res[1]
                out.append(res)
            return tuple(out)

        sf_ref[...], sb_ref[...] = _loop_unrolled(nc, unroll, chunk, (h0f_ref[...], h0b_ref[...]))

    gspec = lambda shp: BS((None,) + shp, lambda b, g: (g,) + (0,) * len(shp))
    st_spec = BS((None, None, SSD_N, 256), lambda b, g: (b, g, 0, 0))
    hs_spec = BS((None, None, nc, SSD_N, 256), lambda b, g: (b, g, 0, 0, 0))
    in_specs = [BS((t, SSD_GW), lambda b, g: (blk0 + b, g)), BS((None, t, 128), lambda b, g: (g, blk0 + b, 0)),
                gspec((4, SSD_GW)), gspec((1, SSD_GW)), gspec((1, 128)), gspec((1, 128)), gspec((1, 256)),
                st_spec, st_spec]
    out_specs = [hs_spec, hs_spec, st_spec, st_spec]
    out_shape = [S((nb, SSD_G, nc, SSD_N, 256), F32)] * 2 + [S((nb, SSD_G, SSD_N, 256), F32)] * 2
    if need_y:
        out_specs = [BS((t, 256), lambda b, g: (b, g))] + out_specs
        out_shape = [S((nb * t, SSD_INNER), F32)] + out_shape
    return pl.pallas_call(
        body, grid=(nb, SSD_G), name="ssd_fwd_lat" if need_y else "ssd_fwd_ctx",
        in_specs=in_specs, out_specs=out_specs, out_shape=out_shape,
        scratch_shapes=[pltpu.VMEM((t, SSD_GW), F32), pltpu.VMEM((t, 128), F32), pltpu.VMEM((t, SSD_GW), F32)],
        compiler_params=_params(),
    )(proj, dtg, cw, cb, dtb, alog, drow, h0f, h0b)


def _ssd_bwd(proj, dtg, cw, cb, dtb, alog, drow, hsf, hsb, dy, dsf, dsb, dproj, nb, t, period, blk0, need_y):
    nc = t // SSD_L
    rb = period
    assert t % rb == 0
    unroll = 2 if nc % 2 == 0 else 1
    masks_of = _conv_taps(SSD_GW, period)

    def body(*refs):
        if need_y:
            (p_ref, dt_ref, cw_ref, cb_ref, dtb_ref, al_ref, d_ref, hsf_ref, hsb_ref, dy_ref, dsf_ref, dsb_ref, _,
             dp_ref, ddt_ref, dhf_ref, dhb_ref, dcw_ref, dcb_ref, ddtb_ref, dal_ref, dd_ref,
             pre, dact, dts, ddts, dtxs) = refs
        else:
            (p_ref, dt_ref, cw_ref, cb_ref, dtb_ref, al_ref, d_ref, hsf_ref, hsb_ref, dsf_ref, dsb_ref, _,
             dp_ref, ddt_ref, dhf_ref, dhb_ref, dcw_ref, dcb_ref, ddtb_ref, dal_ref, dd_ref,
             pre, dact, dts, ddts, dtxs) = refs
            dy_ref = None
        b, g = pl.program_id(0), pl.program_id(1)

        @pl.when(jnp.logical_and(b == 0, g == 0))
        def _():
            for r in (dcw_ref, dcb_ref, ddtb_ref, dal_ref, dd_ref):
                r[...] = jnp.zeros_like(r)

        masks = masks_of(rb)
        per_dir, hmasks = _ssd_consts(1)

        def prologue(r0):
            rows = pl.ds(r0, rb)
            pre[rows, :] = _conv_fwd(p_ref[rows, :], cw_ref[...], cb_ref[...], masks)
            dtv = _softplus(dt_ref[rows, :] + dtb_ref[...])
            dts[rows, :] = dtv
            for d in (0, 1):
                dtxs[rows, 256 * d:256 * d + 256] = _xp_mm(dtv, per_dir[d][2])
            dact[rows, :] = jnp.zeros((rb, SSD_GW), F32)
            ddts[rows, :] = jnp.zeros((rb, 128), F32)

        _for_rows(t, rb, prologue)
        al = al_ref[...]
        def chunk(ci, carry):
            dal_c = carry[2]
            dhs_out = []
            for d, dh, hs_ref in ((0, carry[0], hsf_ref), (1, carry[1], hsb_ref)):
                c = ci if d else (nc - 1 - ci)
                r0 = pl.multiple_of(c * SSD_L, SSD_L)
                a = _silu(pre[pl.ds(r0, SSD_L), :])
                dtc = dts[pl.ds(r0, SSD_L), :]
                dtx = dtxs[pl.ds(r0, SSD_L), 256 * d:256 * d + 256]
                if need_y:
                    fn = lambda x_, bm_, cm_, dt_, dx_, al_, hs_: _ssd_chunk(x_, bm_, cm_, dt_, dx_, al_, hs_, per_dir[d],
                                                                             hmasks, d)
                    _, vjp = jax.vjp(fn, a[:, 0:256], a[:, 256:384], a[:, 384:512], dtc, dtx, al, hs_ref[c])
                    dx, dbm, dcm, ddtc, ddtx, dal_k, dhs = vjp((dy_ref[pl.ds(r0, SSD_L), :], dh))
                    dact[pl.ds(r0, SSD_L), 384:512] += dcm
                else:
                    fn = lambda x_, bm_, dt_, dx_, al_, hs_: _ssd_chunk(x_, bm_, None, dt_, dx_, al_, hs_, per_dir[d],
                                                                        hmasks, d)
                    _, vjp = jax.vjp(fn, a[:, 0:256], a[:, 256:384], dtc, dtx, al, hs_ref[c])
                    dx, dbm, ddtc, ddtx, dal_k, dhs = vjp(dh)
                dact[pl.ds(r0, SSD_L), 0:256] += dx
                dact[pl.ds(r0, SSD_L), 256:384] += dbm
                ddts[pl.ds(r0, SSD_L), :] += ddtc + _dot_nt(ddtx, per_dir[d][2])
                dhs_out.append(dhs)
                dal_c = dal_c + dal_k
            return dhs_out[0], dhs_out[1], dal_c

        dhf_ref[...], dhb_ref[...], dal_acc = _loop_unrolled(
            nc, unroll, chunk, (dsf_ref[...], dsb_ref[...], jnp.zeros((1, 128), F32)))

        def epilogue(r0):
            rows = pl.ds(r0, rb)
            prev = pre[rows, :]
            if need_y:
                dyv = dy_ref[rows, :]
                dact[rows, 0:256] += d_ref[...] * dyv
                dd_ref[g] += _colsum(dyv * _silu(prev[:, 0:256]))
            dpre = dact[rows, :] * _dsilu(prev)
            draw, dcw, dcb = _conv_bwd(dpre, p_ref[rows, :], cw_ref[...], masks)
            dp_ref[rows, :] = draw.astype(BF16)
            dcw_ref[g] += dcw
            dcb_ref[g] += dcb
            ddraw = ddts[rows, :] * _sigmoid(dt_ref[rows, :] + dtb_ref[...])
            ddt_ref[rows, :] = ddraw
            ddtb_ref[g] += _colsum(ddraw)

        _for_rows(t, rb, epilogue)
        dal_ref[g] += dal_acc

    gspec = lambda shp: BS((None,) + shp, lambda b, g: (g,) + (0,) * len(shp))
    full = lambda shp: BS(shp, lambda b, g: (0,) * len(shp))
    st_spec = BS((None, None, SSD_N, 256), lambda b, g: (b, g, 0, 0))
    hs_spec = BS((None, None, nc, SSD_N, 256), lambda b, g: (b, g, 0, 0, 0))
    p_spec = BS((t, SSD_GW), lambda b, g: (blk0 + b, g))
    in_specs = [p_spec, BS((None, t, 128), lambda b, g: (g, blk0 + b, 0)),
                gspec((4, SSD_GW)), gspec((1, SSD_GW)), gspec((1, 128)), gspec((1, 128)), gspec((1, 256)),
                hs_spec, hs_spec]
    args = [proj, dtg, cw, cb, dtb, alog, drow, hsf, hsb]
    if need_y:
        in_specs.append(BS((t, 256), lambda b, g: (b, g)))
        args.append(dy)
    in_specs += [st_spec, st_spec, BS(memory_space=pl.ANY)]
    args += [dsf, dsb, dproj]
    out_specs = [p_spec, BS((None, t, 128), lambda b, g: (g, b, 0)), st_spec, st_spec,
                 full((SSD_G, 4, SSD_GW)), full((SSD_G, 1, SSD_GW)), full((SSD_G, 1, 128)), full((SSD_G, 1, 128)),
                 full((SSD_G, 1, 256))]
    out_shape = [S(dproj.shape, BF16), S((SSD_G, nb * t, 128), F32),
                 S((nb, SSD_G, SSD_N, 256), F32), S((nb, SSD_G, SSD_N, 256), F32),
                 S((SSD_G, 4, SSD_GW), F32), S((SSD_G, 1, SSD_GW), F32), S((SSD_G, 1, 128), F32),
                 S((SSD_G, 1, 128), F32), S((SSD_G, 1, 256), F32)]
    return pl.pallas_call(
        body, grid=(nb, SSD_G), name="ssd_bwd_lat" if need_y else "ssd_bwd_ctx",
        in_specs=in_specs, out_specs=out_specs, out_shape=out_shape,
        input_output_aliases={len(args) - 1: 0},
        scratch_shapes=[pltpu.VMEM((t, SSD_GW), F32), pltpu.VMEM((t, SSD_GW), F32), pltpu.VMEM((t, 128), F32),
                        pltpu.VMEM((t, 128), F32), pltpu.VMEM((t, SSD_GW), F32)],
        compiler_params=_params(),
    )(*args)


def _lru_gate(u, wa, ba, wi, bi, lam):
    r = _sigmoid(_mm(u, wa) + ba)
    i = _sigmoid(_mm(u, wi) + bi)
    log_a = -LRU_C * r * _softplus(-lam)
    a = jnp.exp(log_a)
    x2 = 2.0 * log_a
    em1 = jnp.where(x2 > -0.01, x2 * (1.0 + x2 * (0.5 + x2 * (1.0 / 6.0 + x2 * (1.0 / 24.0)))), jnp.exp(x2) - 1.0)
    return a, jnp.sqrt(-em1) * (i * u)


def _scan_pair(fwd, rev, nblk, width):
    row = lax.broadcasted_iota(jnp.int32, (8, width), 0)

    def block(a_ref, b_ref, h_ref, st, carry, reverse):
        av, bv = a_ref[pl.ds(st, 8), :], b_ref[pl.ds(st, 8), :]
        for s in (1, 2, 4):
            ok = (row < 8 - s) if reverse else (row >= s)
            sh = (8 - s) if reverse else s
            a_sh = jnp.where(ok, pltpu.roll(av, sh, 0), 1.0)
            b_sh = jnp.where(ok, pltpu.roll(bv, sh, 0), 0.0)
            bv = av * b_sh + bv
            av = av * a_sh
        h = bv + av * carry
        h_ref[pl.ds(st, 8), :] = h
        return h[0:1, :] if reverse else h[7:8, :]

    def step(i, carry):
        cf, cr = carry
        cf = block(fwd[0], fwd[1], fwd[2], pl.multiple_of(i * 8, 8), cf, False)
        cr = block(rev[0], rev[1], rev[2], pl.multiple_of((nblk - 1 - i) * 8, 8), cr, True)
        return cf, cr

    return lax.fori_loop(0, nblk, step, (fwd[3], rev[3]))


def _lru_specs(t, blk0):
    p_spec = BS((t, LRU_CB), lambda b, q: (blk0 + b, P_LRU // LRU_CB + q))
    w_spec = BS((2, 2, 128, 128), lambda b, q: (0, q, 0, 0))
    v_spec = BS((2, LRU_CB), lambda b, q: (0, q))
    c_spec = lambda r: BS((r, LRU_CB), lambda b, q: (0, q))
    s_spec = BS((None, 2, LRU_CB), lambda b, q: (b, 0, q))
    return p_spec, w_spec, v_spec, c_spec, s_spec


def _lru_fwd(proj, cw, cb, wa, ba, wi, bi, lam, h0, nb, t, period, blk0, need_y):
    nq = D // LRU_CB
    masks_of = _conv_taps(LRU_CB, period)

    def body(p_ref, cw_ref, cb_ref, wa_ref, ba_ref, wi_ref, bi_ref, lam_ref, h0_ref, *rest):
        if need_y:
            y_ref, hf_ref, hb_ref, fin_ref, sa0, sb0, sa1, sb1 = rest
        else:
            hf_ref, hb_ref, fin_ref, sa0, sb0, sa1, sb1 = rest
        u = _conv_fwd(p_ref[...], cw_ref[...], cb_ref[...], masks_of(t))
        for d, (sa, sb) in enumerate(((sa0, sb0), (sa1, sb1))):
            for j in range(2):
                sl = slice(128 * j, 128 * j + 128)
                a, bb = _lru_gate(u[:, sl], wa_ref[d, j], ba_ref[d:d + 1, sl], wi_ref[d, j], bi_ref[d:d + 1, sl],
                                  lam_ref[d:d + 1, sl])
                sa[:, sl] = a
                sb[:, sl] = bb
        lf, lb = _scan_pair((sa0, sb0, hf_ref, h0_ref[0:1, :]), (sa1, sb1, hb_ref, h0_ref[1:2, :]), t // 8, LRU_CB)
        fin_ref[0:1, :] = lf
        fin_ref[1:2, :] = lb
        if need_y:
            y_ref[...] = hf_ref[...] + hb_ref[...]

    p_spec, w_spec, v_spec, c_spec, s_spec = _lru_specs(t, blk0)
    o_spec = BS((t, LRU_CB), lambda b, q: (b, q))
    out_specs = [o_spec, o_spec, s_spec]
    out_shape = [S((nb * t, D), F32), S((nb * t, D), F32), S((nb, 2, D), F32)]
    if need_y:
        out_specs = [o_spec] + out_specs
        out_shape = [S((nb * t, D), F32)] + out_shape
    return pl.pallas_call(
        body, grid=(nb, nq), name="lru_fwd_lat" if need_y else "lru_fwd_ctx",
        in_specs=[p_spec, c_spec(4), c_spec(1), w_spec, v_spec, w_spec, v_spec, v_spec, s_spec],
        out_specs=out_specs, out_shape=out_shape,
        scratch_shapes=[pltpu.VMEM((t, LRU_CB), F32)] * 4, compiler_params=_params(),
    )(proj, cw, cb, wa, ba, wi, bi, lam, h0)


def _lru_bwd(proj, cw, cb, wa, ba, wi, bi, lam, h0, hf, hb, dy, dfin, dproj, nb, t, period, blk0, need_y):
    nq = D // LRU_CB
    rc = min(256, t)
    masks_of = _conv_taps(LRU_CB, period)

    def body(*refs):
        if need_y:
            (p_ref, cw_ref, cb_ref, wa_ref, ba_ref, wi_ref, bi_ref, lam_ref, h0_ref, hf_ref, hb_ref, dy_ref, dfin_ref, _,
             dp_ref, dh0_ref, dcw_ref, dcb_ref, dwa_ref, dwi_ref, dba_ref, dbi_ref, dlam_ref,
             su, sa0, sa1, sc0, sc1, sg0, sg1) = refs
        else:
            (p_ref, cw_ref, cb_ref, wa_ref, ba_ref, wi_ref, bi_ref, lam_ref, h0_ref, hf_ref, hb_ref, dfin_ref, _,
             dp_ref, dh0_ref, dcw_ref, dcb_ref, dwa_ref, dwi_ref, dba_ref, dbi_ref, dlam_ref,
             su, sa0, sa1, sc0, sc1, sg0, sg1) = refs
            dy_ref = None
        b, q = pl.program_id(0), pl.program_id(1)

        @pl.when(jnp.logical_and(b == 0, q == 0))
        def _():
            for r in (dcw_ref, dcb_ref, dwa_ref, dwi_ref, dba_ref, dbi_ref, dlam_ref):
                r[...] = jnp.zeros_like(r)

        masks = masks_of(t)
        u = _conv_fwd(p_ref[...], cw_ref[...], cb_ref[...], masks)
        su[...] = u
        for d, sa in enumerate((sa0, sa1)):
            for j in range(2):
                sl = slice(128 * j, 128 * j + 128)
                a, _unused = _lru_gate(u[:, sl], wa_ref[d, j], ba_ref[d:d + 1, sl], wi_ref[d, j], bi_ref[d:d + 1, sl],
                                       lam_ref[d:d + 1, sl])
                sa[:, sl] = a
        rowi = lax.broadcasted_iota(jnp.int32, (t, LRU_CB), 0)
        last, first = rowi == t - 1, rowi == 0
        sc0[...] = jnp.where(last, 0.0, pltpu.roll(sa0[...], t - 1, 0))
        sc1[...] = jnp.where(first, 0.0, pltpu.roll(sa1[...], 1, 0))
        g0 = jnp.where(last, dfin_ref[0:1, :], 0.0)
        g1 = jnp.where(first, dfin_ref[1:2, :], 0.0)
        if need_y:
            g0 = g0 + dy_ref[...]
            g1 = g1 + dy_ref[...]
        sg0[...] = g0
        sg1[...] = g1
        zero = jnp.zeros((1, LRU_CB), F32)
        _scan_pair((sc1, sg1, sg1, zero), (sc0, sg0, sg0, zero), t // 8, LRU_CB)
        dh0_ref[0:1, :] = sa0[0:1, :] * sg0[0:1, :]
        dh0_ref[1:2, :] = sa1[t - 1:t, :] * sg1[t - 1:t, :]
        sc0[...] = sg0[...] * jnp.where(first, h0_ref[0:1, :], pltpu.roll(hf_ref[...], 1, 0))
        sc1[...] = sg1[...] * jnp.where(last, h0_ref[1:2, :], pltpu.roll(hb_ref[...], t - 1, 0))

        def rows(ci, carry):
            r0 = pl.multiple_of(ci * rc, rc)
            for j in range(2):
                sl = slice(128 * j, 128 * j + 128)
                du = jnp.zeros((rc, 128), F32)
                for d, (sc, sg) in enumerate(((sc0, sg0), (sc1, sg1))):
                    _, vjp = jax.vjp(_lru_gate, su[pl.ds(r0, rc), sl], wa_ref[d, j], ba_ref[d:d + 1, sl], wi_ref[d, j],
                                     bi_ref[d:d + 1, sl], lam_ref[d:d + 1, sl])
                    du_d, dwa, dba, dwi, dbi, dlam = vjp((sc[pl.ds(r0, rc), sl], sg[pl.ds(r0, rc), sl]))
                    du = du + du_d
                    dwa_ref[d, 2 * q + j] += dwa
                    dwi_ref[d, 2 * q + j] += dwi
                    dba_ref[q, d:d + 1, sl] += dba
                    dbi_ref[q, d:d + 1, sl] += dbi
                    dlam_ref[q, d:d + 1, sl] += dlam
                sa0[pl.ds(r0, rc), sl] = du
            return carry

        lax.fori_loop(0, t // rc, rows, 0)
        draw, dcw, dcb = _conv_bwd(sa0[...], p_ref[...], cw_ref[...], masks)
        dp_ref[...] = draw.astype(BF16)
        dcw_ref[q] += dcw
        dcb_ref[q] += dcb

    p_spec, w_spec, v_spec, c_spec, s_spec = _lru_specs(t, blk0)
    o_spec = BS((t, LRU_CB), lambda b, q: (b, q))
    full = lambda shp: BS(shp, lambda b, q: (0,) * len(shp))
    in_specs = [p_spec, c_spec(4), c_spec(1), w_spec, v_spec, w_spec, v_spec, v_spec, s_spec, o_spec, o_spec]
    args = [proj, cw, cb, wa, ba, wi, bi, lam, h0, hf, hb]
    if need_y:
        in_specs.append(o_spec)
        args.append(dy)
    in_specs += [s_spec, BS(memory_space=pl.ANY)]
    args += [dfin, dproj]
    out_specs = [p_spec, s_spec, full((nq, 4, LRU_CB)), full((nq, 1, LRU_CB)), full((2, 8, 128, 128)),
                 full((2, 8, 128, 128)), full((nq, 2, LRU_CB)), full((nq, 2, LRU_CB)), full((nq, 2, LRU_CB))]
    out_shape = [S(dproj.shape, BF16), S((nb, 2, D), F32), S((nq, 4, LRU_CB), F32), S((nq, 1, LRU_CB), F32),
                 S((2, 8, 128, 128), F32), S((2, 8, 128, 128), F32), S((nq, 2, LRU_CB), F32), S((nq, 2, LRU_CB), F32),
                 S((nq, 2, LRU_CB), F32)]
    return pl.pallas_call(
        body, grid=(nb, nq), name="lru_bwd_lat" if need_y else "lru_bwd_ctx",
        in_specs=in_specs, out_specs=out_specs, out_shape=out_shape, input_output_aliases={len(args) - 1: 0},
        scratch_shapes=[pltpu.VMEM((t, LRU_CB), F32)] * 7, compiler_params=_params(),
    )(*args)


def _mix_core(y_ref, yl_ref, p_ref, nw_ref, bg_ref, wbs_ref, wbl_ref, wo_ref, nrm_s):
    for g in range(SSD_G):
        sl = slice(256 * g, 256 * g + 256)
        nrm_s[:, sl] = _grms(y_ref[:, sl], p_ref[:, sl], nw_ref[:, sl]).astype(BF16)
    br_s = jnp.dot(nrm_s[...], wbs_ref[...], preferred_element_type=F32)
    gl = (yl_ref[...] * _gelu(p_ref[:, 2048:3072])).astype(BF16)
    br_l = jnp.dot(gl, wbl_ref[...], preferred_element_type=F32)
    gs = _sigmoid(p_ref[:, 3072:4096] + bg_ref[:, 0:D])
    gr = _sigmoid(p_ref[:, 4096:5120] + bg_ref[:, D:2 * D])
    mix = (gs * br_s + gr * br_l).astype(BF16)
    xmix = jnp.dot(mix, wo_ref[...], preferred_element_type=F32)
    return br_s, gl, br_l, gs, gr, mix, xmix


def _mix_specs(rt, tiles_per_b):
    row = lambda w: BS((rt, w), lambda i: (i, 0))
    const = lambda shp: BS(shp, lambda i: (0,) * len(shp))
    gate = BS((None, None, 1, D), lambda i: (i // tiles_per_b, 2, 0, 0))
    return row, const, gate


def _mix_fwd(y, ylru, proj, x, m4, wbs, wbl, wo, nw, bg, l1g, l1b, rt, tiles_per_b):
    n = x.shape[0]

    def body(y_ref, yl_ref, p_ref, x_ref, g1_ref, wbs_ref, wbl_ref, wo_ref, nw_ref, bg_ref, lg_ref, lb_ref,
             x1_ref, nrm_ref, gl_ref, mix_ref, brs_ref, brl_ref, xm_ref):
        br_s, gl, br_l, _, _, mix, xmix = _mix_core(y_ref, yl_ref, p_ref, nw_ref, bg_ref, wbs_ref, wbl_ref, wo_ref, nrm_ref)
        gl_ref[...] = gl
        mix_ref[...] = mix
        brs_ref[...] = br_s
        brl_ref[...] = br_l
        xm_ref[...] = xmix
        x1_ref[...] = _resln(x_ref[...], xmix, g1_ref[...], lg_ref[...], lb_ref[...])

    row, const, gate = _mix_specs(rt, tiles_per_b)
    return pl.pallas_call(
        body, grid=(n // rt,), name="mix_fwd",
        in_specs=[row(SSD_INNER), row(D), BS((rt, 5120), lambda i: (i, 1)), row(D), gate,
                  const((SSD_INNER, D)), const((D, D)), const((D, D)), const((1, SSD_INNER)), const((1, 2 * D)),
                  const((1, D)), const((1, D))],
        out_specs=[row(D), row(SSD_INNER), row(D), row(D), row(D), row(D), row(D)],
        out_shape=[S((n, D), F32), S((n, SSD_INNER), BF16), S((n, D), BF16), S((n, D), BF16), S((n, D), F32),
                   S((n, D), F32), S((n, D), F32)],
        compiler_params=_params(),
    )(y, ylru, proj, x, m4, wbs, wbl, wo, nw, bg, l1g, l1b)


def _mix_bwd(y, ylru, proj, x, m4, wbs, wbl, wo, nw, bg, l1g, l1b, brs, brl, xmix, dx1, dproj, rt, tiles_per_b):
    n = x.shape[0]

    def body(y_ref, yl_ref, p_ref, x_ref, g1_ref, wbs_ref, wbl_ref, wo_ref, nw_ref, bg_ref, lg_ref, lb_ref,
             brs_ref, brl_ref, xm_ref, dx1_ref, _,
             dp_ref, dy_ref, dyl_ref, dxr_ref, dbrs_ref, dbrl_ref, dxm_ref,
             dg1_ref, dnw_ref, dbg_ref, dlg_ref, dlb_ref):
        i = pl.program_id(0)

        @pl.when(i == 0)
        def _():
            for r in (dnw_ref, dbg_ref, dlg_ref, dlb_ref):
                r[...] = jnp.zeros_like(r)

        @pl.when(i % tiles_per_b == 0)
        def _():
            dg1_ref[...] = jnp.zeros_like(dg1_ref)

        br_s, br_l = brs_ref[...], brl_ref[...]
        gs = _sigmoid(p_ref[:, 3072:4096] + bg_ref[:, 0:D])
        gr = _sigmoid(p_ref[:, 4096:5120] + bg_ref[:, D:2 * D])
        _, vjp = jax.vjp(_resln, x_ref[...], xm_ref[...], g1_ref[...], lg_ref[...], lb_ref[...])
        dxr, dxmix, dg1, dlg, dlb = vjp(dx1_ref[...])
        dxr_ref[...] = dxr
        dg1_ref[...] += dg1
        dlg_ref[...] += dlg
        dlb_ref[...] += dlb
        dxmb = dxmix.astype(BF16)
        dxm_ref[...] = dxmb
        dmix = lax.dot_general(dxmb, wo_ref[...], (((1,), (1,)), ((), ())), preferred_element_type=F32)
        dbrs = (dmix * gs).astype(BF16)
        dbrl = (dmix * gr).astype(BF16)
        dbrs_ref[...] = dbrs
        dbrl_ref[...] = dbrl
        dmg_s = dmix * br_s * gs * (1.0 - gs)
        dmg_r = dmix * br_l * gr * (1.0 - gr)
        dp_ref[:, 3072:4096] = dmg_s.astype(BF16)
        dp_ref[:, 4096:5120] = dmg_r.astype(BF16)
        dbg_ref[:, 0:D] += _colsum(dmg_s)
        dbg_ref[:, D:2 * D] += _colsum(dmg_r)
        dnrm = lax.dot_general(dbrs, wbs_ref[...], (((1,), (1,)), ((), ())), preferred_element_type=F32)
        for g in range(SSD_G):
            sl = slice(256 * g, 256 * g + 256)
            _, vjp = jax.vjp(_grms, y_ref[:, sl], p_ref[:, sl], nw_ref[:, sl])
            dyg, dzg, dnwg = vjp(dnrm[:, sl])
            dy_ref[:, sl] = dyg
            dp_ref[:, sl] = dzg.astype(BF16)
            dnw_ref[:, sl] += dnwg
        dgl = lax.dot_general(dbrl, wbl_ref[...], (((1,), (1,)), ((), ())), preferred_element_type=F32)
        _, vjp = jax.vjp(lambda a, c: a * _gelu(c), yl_ref[...], p_ref[:, 2048:3072])
        dyl, dlgate = vjp(dgl)
        dyl_ref[...] = dyl
        dp_ref[:, 2048:3072] = dlgate.astype(BF16)

    row, const, gate = _mix_specs(rt, tiles_per_b)
    pblk = BS((rt, 5120), lambda i: (i, 1))
    nb = n // (rt * tiles_per_b)
    out_specs = [pblk, row(SSD_INNER), row(D), row(D), row(D), row(D), row(D),
                 BS((None, 1, D), lambda i: (i // tiles_per_b, 0, 0)), const((1, SSD_INNER)), const((1, 2 * D)),
                 const((1, D)), const((1, D))]
    out_shape = [S(dproj.shape, BF16), S((n, SSD_INNER), F32), S((n, D), F32), S((n, D), F32),
                 S((n, D), BF16), S((n, D), BF16), S((n, D), BF16),
                 S((nb, 1, D), F32), S((1, SSD_INNER), F32), S((1, 2 * D), F32), S((1, D), F32), S((1, D), F32)]
    return pl.pallas_call(
        body, grid=(n // rt,), name="mix_bwd",
        in_specs=[row(SSD_INNER), row(D), pblk, row(D), gate,
                  const((SSD_INNER, D)), const((D, D)), const((D, D)), const((1, SSD_INNER)), const((1, 2 * D)),
                  const((1, D)), const((1, D)), row(D), row(D), row(D), row(D), BS(memory_space=pl.ANY)],
        out_specs=out_specs, out_shape=out_shape, input_output_aliases={16: 0},
        compiler_params=_params(),
    )(y, ylru, proj, x, m4, wbs, wbl, wo, nw, bg, l1g, l1b, brs, brl, xmix, dx1, dproj)


def _mlp_step(x1, tgt, m4, w1, b1, w2, b2, l2g, l2b, rt, tiles_per_b):
    n = x1.shape[0]

    def body(x_ref, t_ref, sh_ref, sc_ref, gt_ref, w1_hbm, b1_ref, w2_hbm, b2_ref, lg_ref, lb_ref,
             loss_ref, dx_ref, h2_ref, da1_ref, r2_ref, dmlp_ref, dm_ref, db1_ref, db2_ref, dlg_ref, dlb_ref,
             w1_vm, w2_vm, sem):
        i = pl.program_id(0)

        @pl.when(i == 0)
        def _():
            c1 = pltpu.make_async_copy(w1_hbm, w1_vm, sem.at[0])
            c2 = pltpu.make_async_copy(w2_hbm, w2_vm, sem.at[1])
            c1.start()
            c2.start()
            for r in (loss_ref, db1_ref, db2_ref, dlg_ref, dlb_ref):
                r[...] = jnp.zeros_like(r)
            c1.wait()
            c2.wait()

        @pl.when(i % tiles_per_b == 0)
        def _():
            dm_ref[...] = jnp.zeros_like(dm_ref)

        x1v = x_ref[...]
        h2, vjp_h = jax.vjp(_modln, x1v, sh_ref[...], sc_ref[...])
        h2b = h2.astype(BF16)
        h2_ref[...] = h2b
        r = jnp.maximum(jnp.dot(h2b, w1_vm[...], preferred_element_type=F32) + b1_ref[...], 0.0)
        r2b = (r * r).astype(BF16)
        r2_ref[...] = r2b
        mlp = jnp.dot(r2b, w2_vm[...], preferred_element_type=F32) + b2_ref[...]
        x2, vjp_r = jax.vjp(_resln, x1v, mlp, gt_ref[...], lg_ref[...], lb_ref[...])
        diff = x2 - t_ref[...]
        loss_ref[...] += (0.5 / D) * jnp.sum(diff * diff)
        dxa, dmlp, dgt, dlg, dlb = vjp_r(diff * (1.0 / D))
        dlg_ref[...] += dlg
        dlb_ref[...] += dlb
        dm_ref[2:3, :] += dgt
        db2_ref[...] += _colsum(dmlp)
        dmlpb = dmlp.astype(BF16)
        dmlp_ref[...] = dmlpb
        da1 = lax.dot_general(dmlpb, w2_vm[...], (((1,), (1,)), ((), ())), preferred_element_type=F32) * (2.0 * r)
        db1_ref[...] += _colsum(da1)
        da1b = da1.astype(BF16)
        da1_ref[...] = da1b
        dh2 = lax.dot_general(da1b, w1_vm[...], (((1,), (1,)), ((), ())), preferred_element_type=F32)
        dxb, dsh, dsc = vjp_h(dh2)
        dx_ref[...] = dxa + dxb
        dm_ref[0:1, :] += dsh
        dm_ref[1:2, :] += dsc

    row = lambda w: BS((rt, w), lambda i: (i, 0))
    const = lambda shp: BS(shp, lambda i: (0,) * len(shp))
    mod = lambda k: BS((None, None, 1, D), lambda i: (i // tiles_per_b, k, 0, 0))
    nb = n // (rt * tiles_per_b)
    anyspec = BS(memory_space=pl.ANY)
    return pl.pallas_call(
        body, grid=(n // rt,), name="mlp_step",
        in_specs=[row(D), row(D), mod(3), mod(4), mod(5), anyspec, const((1, MLP_H)), anyspec, const((1, D)),
                  const((1, D)), const((1, D))],
        out_specs=[const((8, 128)), row(D), row(D), row(MLP_H), row(MLP_H), row(D),
                   BS((None, 3, D), lambda i: (i // tiles_per_b, 0, 0)), const((1, MLP_H)), const((1, D)),
                   const((1, D)), const((1, D))],
        out_shape=[S((8, 128), F32), S((n, D), F32), S((n, D), BF16), S((n, MLP_H), BF16), S((n, MLP_H), BF16),
                   S((n, D), BF16), S((nb, 3, D), F32), S((1, MLP_H), F32), S((1, D), F32), S((1, D), F32),
                   S((1, D), F32)],
        scratch_shapes=[pltpu.VMEM((D, MLP_H), BF16), pltpu.VMEM((MLP_H, D), BF16), pltpu.SemaphoreType.DMA((2,))],
        compiler_params=_params(),
    )(x1, tgt, m4, m4, m4, w1, b1, w2, b2, l2g, l2b)


def _pack_win(w):
    parts = []
    for g in range(SSD_G):
        parts += [w[:, 256 * g:256 * g + 256], w[:, 2048 + 128 * g:2176 + 128 * g], w[:, 4160 + 128 * g:4288 + 128 * g]]
    parts += [w[:, 3136:4160], w[:, 5184:7232], w[:, 7232:8256], w[:, 8256:10304], w[:, 3072:3136],
              jnp.zeros((w.shape[0], P_W - P_DT - 64), w.dtype)]
    return jnp.concatenate(parts, axis=1)


def _unpack_win(p):
    xs = [p[:, 512 * g:512 * g + 256] for g in range(SSD_G)]
    bs = [p[:, 512 * g + 256:512 * g + 384] for g in range(SSD_G)]
    cs = [p[:, 512 * g + 384:512 * g + 512] for g in range(SSD_G)]
    return jnp.concatenate(xs + bs + [p[:, P_DT:P_DT + 64], p[:, P_LRU:P_Z]] + cs + [p[:, P_Z:P_DT]], axis=1)


def _pack_conv(w):
    return jnp.stack([jnp.concatenate([w[:, 256 * g:256 * g + 256], w[:, 2048 + 128 * g:2176 + 128 * g],
                                       w[:, 3072 + 128 * g:3200 + 128 * g]], axis=1) for g in range(SSD_G)])


def _unpack_conv(p):
    r = p.shape[1]
    x = jnp.transpose(p[:, :, 0:256], (1, 0, 2)).reshape(r, 2048)
    b = jnp.transpose(p[:, :, 256:384], (1, 0, 2)).reshape(r, 1024)
    c = jnp.transpose(p[:, :, 384:512], (1, 0, 2)).reshape(r, 1024)
    return jnp.concatenate([x, b, c], axis=1)


def _pack_heads(v):
    p = jnp.transpose(v.reshape(2, SSD_G, 4), (1, 0, 2)).reshape(SSD_G, 1, 8)
    return jnp.pad(p, ((0, 0), (0, 0), (0, 120)))


def _unpack_heads(p):
    return jnp.transpose(p[:, 0, 0:8].reshape(SSD_G, 2, 4), (1, 0, 2)).reshape(2, 32)


def _pack_dt(dt):
    n = dt.shape[0]
    p = jnp.transpose(dt.reshape(n, 2, SSD_G, 4), (2, 0, 1, 3)).reshape(SSD_G, n, 8)
    return jnp.pad(p, ((0, 0), (0, 0), (0, 120)))


def _unpack_dt(p):
    n = p.shape[1]
    return jnp.transpose(p[:, :, 0:8].reshape(SSD_G, n, 2, 4), (1, 2, 0, 3)).reshape(n, 64)


def _tk(rows):
    return 512 if rows % 512 == 0 else (256 if rows % 256 == 0 else 128)


def _local_step(x, c, ctx, tgt, sm, wmod, win, wbs, wbl, wo, w1, w2):
    nb, t, _ = x.shape
    tc = ctx.shape[1]
    nl, ncx = nb * t, nb * tc
    rt = 256 if tc % 256 == 0 else 128
    rtm = 128
    xl, xc = x.reshape(nl, D), ctx.reshape(ncx, D)
    tgt2 = tgt.reshape(nl, D)
    cc = jnp.zeros((8, D), F32).at[0:nb].set(c).at[nb].set(sm["c_ctx"])
    m = _mod_fwd(cc, wmod, sm["b_mod"])
    m4 = m.reshape(8, N_MOD, 1, D)
    proj, h1 = _inproj_fwd(xl, m4, win, rtm, nl // rtm, t // rtm, nb, "inproj_fwd_lat")
    proj_c, h1_c = _inproj_fwd(xc, m4, win, rtm, 0, 1, nb, "inproj_fwd_ctx")

    cw_s, cb_s = _pack_conv(sm["ssd_conv_w"]), _pack_conv(sm["ssd_conv_b"])
    dtb, alog = _pack_heads(sm["ssd_dt_bias"]), _pack_heads(sm["ssd_a_log"])
    drow = jnp.repeat(sm["ssd_d"].reshape(32), 64).reshape(SSD_G, 1, 256)
    dtg, dtg_c = _pack_dt(proj[:, P_DT:P_DT + 64]), _pack_dt(proj_c[:, P_DT:P_DT + 64])
    zst = jnp.zeros((nb, SSD_G, SSD_N, 256), F32)
    zl = jnp.zeros((nb, 2, D), F32)
    ssd_p = (cw_s, cb_s, dtb, alog, drow)
    lru_p = (sm["lru_conv_w"], sm["lru_conv_b"], sm["lru_wa"], sm["lru_ba"], sm["lru_wi"], sm["lru_bi"], sm["lru_lambda"])

    chsf, chsb, csf, csb = _ssd_fwd(proj_c, dtg_c, *ssd_p, zst, zst, nb, tc, tc, 0, False)
    y, lhsf, lhsb, _, _ = _ssd_fwd(proj, dtg, *ssd_p, csf, csb, nb, t, GRID_W, 0, True)
    chf, chb, cfin = _lru_fwd(proj_c, *lru_p, zl, nb, tc, tc, 0, False)
    ylru, lhf, lhb, _ = _lru_fwd(proj, *lru_p, cfin, nb, t, GRID_W, 0, True)
    mix_w = (wbs, wbl, wo, sm["ssd_norm_w"], sm["b_gate"], sm["ln1_g"], sm["ln1_b"])
    x1, nrm, gl, mixb, brs, brl, xmix = _mix_fwd(y, ylru, proj, xl, m4, *mix_w, rtm, t // rtm)
    (loss, dx1, h2, da1, r2, dmlp, dm2, db1, db2, dl2g, dl2b) = _mlp_step(
        x1, tgt2, m4, w1, sm["b_mlp1"], w2, sm["b_mlp2"], sm["ln2_g"], sm["ln2_b"], rt, t // rt)

    dproj = lax.empty((nl, P_W), BF16)
    dproj_c = jnp.zeros((ncx, P_W), BF16)
    (dproj, dy, dylru, dxres, dbrs, dbrl, dxm, dg1, dnw, dbg, dl1g, dl1b) = _mix_bwd(
        y, ylru, proj, xl, m4, *mix_w, brs, brl, xmix, dx1, dproj, rtm, t // rtm)
    big = {
        "w_br_ssd": _matmul_tn(nrm, dbrs, D, D, _tk(nl), "dw_br_ssd"),
        "w_br_lru": _matmul_tn(gl, dbrl, D, D, _tk(nl), "dw_br_lru"),
        "w_out": _matmul_tn(mixb, dxm, D, D, _tk(nl), "dw_out"),
        "w_mlp1": _matmul_tn(h2, da1, D, D, _tk(nl), "dw_mlp1"),
        "w_mlp2": _matmul_tn(r2, dmlp, D, D, _tk(nl), "dw_mlp2"),
    }
    (dproj, ddt_l, dh0f, dh0b, dcw_l, dcb_l, ddtb_l, dal_l, dd) = _ssd_bwd(
        proj, dtg, *ssd_p, lhsf, lhsb, dy, zst, zst, dproj, nb, t, GRID_W, 0, True)
    (dproj_c, ddt_c, _, _, dcw_c, dcb_c, ddtb_c, dal_c, _) = _ssd_bwd(
        proj_c, dtg_c, *ssd_p, chsf, chsb, None, dh0f, dh0b, dproj_c, nb, tc, tc, 0, False)
    (dproj, dlh0, gcw_l, gcb_l, gwa_l, gwi_l, gba_l, gbi_l, glam_l) = _lru_bwd(
        proj, *lru_p, cfin, lhf, lhb, dylru, zl, dproj, nb, t, GRID_W, 0, True)
    (dproj_c, _, gcw_c, gcb_c, gwa_c, gwi_c, gba_c, gbi_c, glam_c) = _lru_bwd(
        proj_c, *lru_p, zl, chf, chb, None, dlh0, dproj_c, nb, tc, tc, 0, False)
    pad_dt = lambda d: jnp.pad(_unpack_dt(d).astype(BF16), ((0, 0), (0, P_W - P_DT - 64)))
    dproj = lax.dynamic_update_slice(dproj, pad_dt(ddt_l), (0, P_DT))
    dproj_c = lax.dynamic_update_slice(dproj_c, pad_dt(ddt_c), (0, P_DT))

    gx, dm1 = _inproj_bwd(xl, m4, win, dproj, rt, 0, nl // rt, t // rt, nb, True, dxres)
    (dmc,) = _inproj_bwd(xc, m4, win, dproj_c, rt, 0, ncx // rt, ncx // rt, nb, False, None)
    dm = jnp.zeros((8, N_MOD, D), F32)
    dm = dm.at[0:nb].set(jnp.concatenate([dm1, dg1, dm2], axis=1)).at[nb, 0:2].set(dmc[0])
    dwmod, dbmod, dcc = _mod_bwd(cc, wmod, dm.reshape(8, N_MOD * D))

    big["w_mod"] = dwmod
    big["w_in"] = _matmul_tn2(h1, dproj, h1_c, dproj_c, D, 1152, min(_tk(nl), _tk(ncx)), "dw_in")
    nq = D // LRU_CB
    small = {
        "c_ctx": dcc[nb],
        "b_mod": dbmod,
        "b_gate": dbg,
        "ssd_conv_w": _unpack_conv(dcw_l + dcw_c),
        "ssd_conv_b": _unpack_conv(dcb_l + dcb_c),
        "ssd_dt_bias": _unpack_heads(ddtb_l + ddtb_c),
        "ssd_a_log": _unpack_heads(dal_l + dal_c),
        "ssd_d": jnp.sum(dd.reshape(32, 64), axis=1),
        "ssd_norm_w": dnw,
        "lru_conv_w": jnp.transpose(gcw_l + gcw_c, (1, 0, 2)).reshape(4, D),
        "lru_conv_b": (gcb_l + gcb_c).reshape(1, D),
        "lru_wa": gwa_l + gwa_c,
        "lru_ba": jnp.transpose(gba_l + gba_c, (1, 0, 2)).reshape(2, D),
        "lru_wi": gwi_l + gwi_c,
        "lru_bi": jnp.transpose(gbi_l + gbi_c, (1, 0, 2)).reshape(2, D),
        "lru_lambda": jnp.transpose(glam_l + glam_c, (1, 0, 2)).reshape(2, D),
        "ln1_g": dl1g, "ln1_b": dl1b, "b_mlp1": db1, "b_mlp2": db2, "ln2_g": dl2g, "ln2_b": dl2b,
    }
    return loss[0, 0], gx.reshape(nb, t, D), big, small


_HBM = BS(memory_space=pl.ANY)


def _place():
    return lax.axis_index("x"), lax.axis_index("y"), lax.axis_index("c")


def _other_chips(x, y):
    return [(1 - x, y), (x, 1 - y), (1 - x, 1 - y)]


def _gather_chips(arrs):
    n = len(arrs)

    def body(*refs):
        ins, outs = refs[:n], refs[n:2 * n]
        ici_send, ici_recv, d2d_send, d2d_recv, loc_sems = refs[2 * n:]
        x, y, c = _place()
        me = 2 * x + y
        chips = _other_chips(x, y)
        halves = [(pl.ds(pl.multiple_of(c * (a.shape[0] // 2), 8), a.shape[0] // 2),
                   pl.ds(pl.multiple_of((1 - c) * (a.shape[0] // 2), 8), a.shape[0] // 2)) for a in arrs]
        drains = []
        for a in range(n):
            mine = halves[a][0]
            loc = pltpu.make_async_copy(ins[a], outs[a].at[me], loc_sems.at[a])
            loc.start()
            drains.append(loc.wait)
            for k, (px, py) in enumerate(chips):
                cp = pltpu.make_async_remote_copy(src_ref=ins[a].at[mine], dst_ref=outs[a].at[me, mine],
                                                  send_sem=ici_send.at[3 * a + k], recv_sem=ici_recv.at[3 * a + k],
                                                  device_id=(px, py, c), device_id_type=MESH)
                cp.start()
                drains.append(cp.wait_send)
        for a in range(n):
            mine = halves[a][0]
            for k, (px, py) in enumerate(chips):
                landed = outs[a].at[2 * px + py, mine]
                pltpu.make_async_remote_copy(src_ref=ins[a].at[mine], dst_ref=landed, send_sem=ici_send.at[3 * a + k],
                                             recv_sem=ici_recv.at[3 * a + k], device_id=(px, py, c),
                                             device_id_type=MESH).wait_recv()
                fw = pltpu.make_async_remote_copy(src_ref=landed, dst_ref=landed, send_sem=d2d_send.at[3 * a + k],
                                                  recv_sem=d2d_recv.at[3 * a + k], device_id=(x, y, 1 - c),
                                                  device_id_type=MESH)
                fw.start()
                drains.append(fw.wait_send)
        for a in range(n):
            theirs = halves[a][1]
            for k, (px, py) in enumerate(chips):
                got = outs[a].at[2 * px + py, theirs]
                pltpu.make_async_remote_copy(src_ref=got, dst_ref=got, send_sem=d2d_send.at[3 * a + k],
                                             recv_sem=d2d_recv.at[3 * a + k], device_id=(x, y, 1 - c),
                                             device_id_type=MESH).wait_recv()
        for wait in drains:
            wait()

    return pl.pallas_call(
        body, name="gather_weights", in_specs=[_HBM] * n, out_specs=[_HBM] * n,
        out_shape=[S((4,) + a.shape, a.dtype) for a in arrs],
        scratch_shapes=[pltpu.SemaphoreType.DMA((3 * n,))] * 4 + [pltpu.SemaphoreType.DMA((n,))],
    )(*arrs)


def _scatter_chips(arrs):
    n = len(arrs)

    def body(*refs):
        ins, outs = refs[:n], refs[n:2 * n]
        send_sems, recv_sems = refs[2 * n:]
        x, y, c = _place()
        chips = _other_chips(x, y)
        sends = []
        for a in range(n):
            for k, (px, py) in enumerate(chips):
                cp = pltpu.make_async_remote_copy(src_ref=ins[a].at[2 * px + py], dst_ref=outs[a].at[k],
                                                  send_sem=send_sems.at[a, k], recv_sem=recv_sems.at[a, k],
                                                  device_id=(px, py, c), device_id_type=MESH)
                cp.start()
                sends.append(cp)
        for cp in sends:
            cp.wait_recv()
        for cp in sends:
            cp.wait_send()

    return pl.pallas_call(
        body, name="scatter_grads", in_specs=[_HBM] * n, out_specs=[_HBM] * n,
        out_shape=[S((3,) + a.shape[1:], a.dtype) for a in arrs],
        scratch_shapes=[pltpu.SemaphoreType.DMA((n, 3)), pltpu.SemaphoreType.DMA((n, 3))],
    )(*arrs)


def _swap_halves(arrs):
    n = len(arrs)

    def body(*refs):
        ins, outs = refs[:n], refs[n:2 * n]
        send_sems, recv_sems = refs[2 * n:]
        x, y, c = _place()
        sends = []
        for a in range(n):
            hr = arrs[a].shape[1] // 2
            theirs = pl.ds(pl.multiple_of((1 - c) * hr, 8), hr)
            for q in range(4):
                cp = pltpu.make_async_remote_copy(src_ref=ins[a].at[q, theirs], dst_ref=outs[a].at[q],
                                                  send_sem=send_sems.at[4 * a + q], recv_sem=recv_sems.at[4 * a + q],
                                                  device_id=(x, y, 1 - c), device_id_type=MESH)
                cp.start()
                sends.append(cp)
        for cp in sends:
            cp.wait_recv()
        for cp in sends:
            cp.wait_send()

    return pl.pallas_call(
        body, name="swap_halves", in_specs=[_HBM] * n, out_specs=[_HBM] * n,
        out_shape=[S((4, a.shape[1] // 2, a.shape[2]), a.dtype) for a in arrs],
        scratch_shapes=[pltpu.SemaphoreType.DMA((4 * n,)), pltpu.SemaphoreType.DMA((4 * n,))],
    )(*arrs)


def _allreduce_small(v):
    def body(in_ref, out_ref, r0, r1, r2, send_sems, recv_sems):
        x, y, c = _place()
        src = in_ref
        for s, (buf, peer) in enumerate(((r0, (x, y, 1 - c)), (r1, (x, 1 - y, c)), (r2, (1 - x, y, c)))):
            cp = pltpu.make_async_remote_copy(src_ref=src, dst_ref=buf, send_sem=send_sems.at[s], recv_sem=recv_sems.at[s],
                                              device_id=peer, device_id_type=MESH)
            cp.start()
            cp.wait()
            out_ref[...] = src[...] + buf[...]
            src = out_ref

    vm = BS(memory_space=pltpu.VMEM)
    return pl.pallas_call(
        body, name="allreduce_small", in_specs=[vm], out_specs=vm, out_shape=S(v.shape, F32),
        scratch_shapes=[pltpu.VMEM(v.shape, F32)] * 3 + [pltpu.SemaphoreType.DMA((3,)), pltpu.SemaphoreType.DMA((3,))],
        compiler_params=_params(),
    )(v)


def _swap_cores(arrs):
    n = len(arrs)

    def body(*refs):
        ins, outs = refs[:n], refs[n:2 * n]
        send_sems, recv_sems = refs[2 * n:]
        x, y, c = _place()
        sends = []
        for a in range(n):
            cp = pltpu.make_async_remote_copy(src_ref=ins[a], dst_ref=outs[a], send_sem=send_sems.at[a],
                                              recv_sem=recv_sems.at[a], device_id=(x, y, 1 - c), device_id_type=MESH)
            cp.start()
            sends.append(cp)
        for cp in sends:
            cp.wait_recv()
        for cp in sends:
            cp.wait_send()

    return pl.pallas_call(
        body, name="swap_cores", in_specs=[_HBM] * n, out_specs=[_HBM] * n,
        out_shape=[S(a.shape, a.dtype) for a in arrs],
        scratch_shapes=[pltpu.SemaphoreType.DMA((n,)), pltpu.SemaphoreType.DMA((n,))],
    )(*arrs)


def _row_tile(r, c=128):
    tr = 256 if c <= 1024 else (128 if c <= 2048 else 64)
    return tr if r % tr == 0 else r


def _sum_half(own, sib, core, name):
    _, r, c = own.shape
    hr = r // 2
    tr = _row_tile(hr, c)
    nbk = hr // tr

    def body(core_ref, o_ref, s_ref, p_ref, pb_ref):
        p = o_ref[...] + s_ref[...]
        p_ref[...] = p
        pb_ref[...] = p.astype(BF16)

    blk = BS((None, tr, c), lambda q, i, cr: (q, i, 0))
    return pl.pallas_call(
        body, name=name, out_shape=[S((4, hr, c), F32), S((4, hr, c), BF16)],
        grid_spec=pltpu.PrefetchScalarGridSpec(
            num_scalar_prefetch=1, grid=(4, nbk),
            in_specs=[BS((None, tr, c), lambda q, i, cr: (q, cr[0] * nbk + i, 0)), blk], out_specs=[blk, blk]),
        compiler_params=_params(),
    )(core, own, sib)


def _sum4(part, recv, chip, name):
    _, r, c = part.shape
    tr = _row_tile(r, c)

    def body(chip_ref, o_ref, r_ref, out_ref):
        acc = o_ref[...]
        for k in range(3):
            acc = acc + r_ref[k].astype(F32)
        out_ref[...] = acc

    return pl.pallas_call(
        body, name=name, out_shape=S((r, c), F32),
        grid_spec=pltpu.PrefetchScalarGridSpec(
            num_scalar_prefetch=1, grid=(r // tr,),
            in_specs=[BS((None, tr, c), lambda i, ch: (ch[0], i, 0)), BS((3, tr, c), lambda i, ch: (0, i, 0))],
            out_specs=BS((tr, c), lambda i, ch: (i, 0))),
        compiler_params=_params(),
    )(chip, part, recv)


def _adam_math(w, g, m, v):
    m = ADAM_B1 * m + (1.0 - ADAM_B1) * g
    v = ADAM_B2 * v + (1.0 - ADAM_B2) * (g * g)
    m_hat = m / (1.0 - ADAM_B1 ** ADAM_STEP)
    v_hat = v / (1.0 - ADAM_B2 ** ADAM_STEP)
    return -ADAM_LR * (m_hat / (jnp.sqrt(v_hat) + ADAM_EPS) + ADAM_WD * w), m, v


def _adam_halves(mine, other, w, m, v, core, name):
    r, c = w.shape
    tr = _row_tile(r // 2, c)
    nbk = (r // 2) // tr

    def body(core_ref, a_ref, b_ref, w_ref, m_ref, v_ref, g_ref, d_ref, nm_ref, nv_ref):
        g = jnp.where(pl.program_id(0) // nbk == core_ref[0], a_ref[...], b_ref[...])
        g_ref[...] = g
        d_ref[...], nm_ref[...], nv_ref[...] = _adam_math(w_ref[...], g, m_ref[...], v_ref[...])

    spec = BS((tr, c), lambda i, cr: (i, 0))
    half = BS((tr, c), lambda i, cr: (i % nbk, 0))
    return pl.pallas_call(
        body, name=name, out_shape=[S((r, c), F32)] * 4,
        grid_spec=pltpu.PrefetchScalarGridSpec(num_scalar_prefetch=1, grid=(r // tr,), in_specs=[half, half] + [spec] * 3,
                                               out_specs=[spec] * 4),
        compiler_params=_params(),
    )(core, mine, other, w, m, v)


def _adam_flat(g, w, m, v):
    r = w.shape[0]
    tr = _row_tile(r)

    def body(g_ref, w_ref, m_ref, v_ref, d_ref, nm_ref, nv_ref):
        d_ref[...], nm_ref[...], nv_ref[...] = _adam_math(w_ref[...], g_ref[...], m_ref[...], v_ref[...])

    spec = BS((tr, 128), lambda i: (i, 0))
    return pl.pallas_call(
        body, grid=(r // tr,), name="adam_small", in_specs=[spec] * 4, out_specs=[spec] * 3,
        out_shape=[S((r, 128), F32)] * 3, compiler_params=_params(),
    )(g, w, m, v)


def _flatten(arrs, rows_mult=256):
    flat = jnp.concatenate([a.reshape(-1) for a in arrs])
    n = flat.shape[0]
    rows = -(-n // 128)
    rows = -(-rows // rows_mult) * rows_mult
    return jnp.pad(flat, (0, rows * 128 - n)).reshape(rows, 128)


def _unflatten(flat, shapes):
    flat = flat.reshape(-1)
    out, o = [], 0
    for shp in shapes:
        n = int(np.prod(shp))
        out.append(flat[o:o + n].reshape(shp))
        o += n
    return out


BIG = ["w_mod", "w_in", "w_br_ssd", "w_br_lru", "w_out", "w_mlp1", "w_mlp2"]
COL_SHARDED = {"w_mod": N_MOD * D, "w_in": IN_COLS, "w_mlp1": MLP_H}
SMALL_SHARDED = ["ssd_conv_w", "lru_conv_w", "lru_ba", "lru_bi", "lru_lambda"]
WEIGHTS = ['c_ctx', 'w_mod', 'b_mod', 'w_in', 'b_gate', 'ssd_conv_w', 'ssd_conv_b', 'ssd_dt_bias', 'ssd_a_log', 'ssd_d',
           'ssd_norm_w', 'lru_conv_w', 'lru_conv_b', 'lru_wa', 'lru_ba', 'lru_wi', 'lru_bi', 'lru_lambda', 'w_br_ssd',
           'w_br_lru', 'w_out', 'ln1_g', 'ln1_b', 'w_mlp1', 'b_mlp1', 'w_mlp2', 'b_mlp2', 'ln2_g', 'ln2_b']
SMALL = [n for n in WEIGHTS if n not in BIG]


def _full_from_chips(g4, name):
    if name in COL_SHARDED:
        return jnp.transpose(g4, (1, 0, 2)).reshape(g4.shape[1], 4 * g4.shape[2])
    return g4.reshape(4 * g4.shape[1], g4.shape[2])


def _chips_from_full(full, name):
    if name in COL_SHARDED:
        r, c = full.shape
        return jnp.transpose(full.reshape(r, 4, c // 4), (1, 0, 2))
    return full.reshape(4, full.shape[0] // 4, full.shape[1])


def kernel(x, c, ctx, c_ctx, w_mod, b_mod, w_in, b_gate, ssd_conv_w, ssd_conv_b, ssd_dt_bias, ssd_a_log, ssd_d, ssd_norm_w, lru_conv_w, lru_conv_b, lru_wa, lru_ba, lru_wi, lru_bi, lru_lambda, w_br_ssd, w_br_lru, w_out, ln1_g, ln1_b, w_mlp1, b_mlp1, w_mlp2, b_mlp2, ln2_g, ln2_b, loss_target, m_c_ctx, m_w_mod, m_b_mod, m_w_in, m_b_gate, m_ssd_conv_w, m_ssd_conv_b, m_ssd_dt_bias, m_ssd_a_log, m_ssd_d, m_ssd_norm_w, m_lru_conv_w, m_lru_conv_b, m_lru_wa, m_lru_ba, m_lru_wi, m_lru_bi, m_lru_lambda, m_w_br_ssd, m_w_br_lru, m_w_out, m_ln1_g, m_ln1_b, m_w_mlp1, m_b_mlp1, m_w_mlp2, m_b_mlp2, m_ln2_g, m_ln2_b, v_c_ctx, v_w_mod, v_b_mod, v_w_in, v_b_gate, v_ssd_conv_w, v_ssd_conv_b, v_ssd_dt_bias, v_ssd_a_log, v_ssd_d, v_ssd_norm_w, v_lru_conv_w, v_lru_conv_b, v_lru_wa, v_lru_ba, v_lru_wi, v_lru_bi, v_lru_lambda, v_w_br_ssd, v_w_br_lru, v_w_out, v_ln1_g, v_ln1_b, v_w_mlp1, v_b_mlp1, v_w_mlp2, v_b_mlp2, v_ln2_g, v_ln2_b):
    given = dict(locals())
    w = {n: given[n] for n in WEIGHTS}
    mom = {n: given["m_" + n] for n in WEIGHTS}
    var = {n: given["v_" + n] for n in WEIGHTS}
    chip = 2 * lax.axis_index("x") + lax.axis_index("y")

    shard2d = {n: w[n].reshape(w[n].shape[-2:]) for n in BIG}
    small_pack = _flatten([w[n] for n in SMALL_SHARDED], rows_mult=16)
    gathered = _gather_chips([shard2d[n].astype(BF16) for n in BIG] + [small_pack])
    full = {n: _full_from_chips(g, n) for n, g in zip(BIG, gathered[:-1])}
    full["w_in"] = _pack_win(full["w_in"])
    per_chip = [_unflatten(gathered[-1][q], [w[n].shape for n in SMALL_SHARDED]) for q in range(4)]
    sm = {n: jnp.concatenate([per_chip[q][i] for q in range(4)], axis=-1) for i, n in enumerate(SMALL_SHARDED)}
    for n in SMALL:
        if n not in sm:
            sm[n] = w[n]
    sm = {n: (a.reshape(a.shape[1:]) if a.ndim >= 3 else a) for n, a in sm.items()}

    loss, gx, gbig, gsmall = _local_step(x, c, ctx, loss_target, sm, *[full[n] for n in BIG])
    gbig["w_in"] = _unpack_win(gbig["w_in"])

    core_id = lax.axis_index("c").astype(jnp.int32).reshape(1)
    chip_id = chip.astype(jnp.int32).reshape(1)
    slabs = [_chips_from_full(gbig[n], n) for n in BIG]
    sib = _swap_halves(slabs)
    halves = [_sum_half(s, o, core_id, "half_" + n) for n, s, o in zip(BIG, slabs, sib)]
    recv = _scatter_chips([h[1] for h in halves])
    mine = [_sum4(h[0], r, chip_id, "sum_" + n) for n, h, r in zip(BIG, halves, recv)]
    other = _swap_cores(mine)
    out = {}
    for n, pa, pb in zip(BIG, mine, other):
        shp = shard2d[n].shape
        res = _adam_halves(pa, pb, shard2d[n], mom[n].reshape(shp), var[n].reshape(shp), core_id, "adam_" + n)
        out[n] = [r.reshape(w[n].shape) for r in res]

    full_shapes = [gsmall[n].shape for n in SMALL]
    gs_all = _unflatten(_allreduce_small(_flatten([gsmall[n] for n in SMALL])), full_shapes)
    gs = {}
    for n, g in zip(SMALL, gs_all):
        if n in SMALL_SHARDED:
            width = w[n].shape[-1]
            g = lax.dynamic_slice_in_dim(g, chip * width, width, axis=g.ndim - 1)
        gs[n] = g.reshape(w[n].shape)
    shapes = [w[n].shape for n in SMALL]
    d_s, m_s, v_s = _adam_flat(_flatten([gs[n] for n in SMALL]), _flatten([w[n] for n in SMALL]),
                               _flatten([mom[n] for n in SMALL]), _flatten([var[n] for n in SMALL]))
    for n, d_, m_, v_ in zip(SMALL, _unflatten(d_s, shapes), _unflatten(m_s, shapes), _unflatten(v_s, shapes)):
        out[n] = [gs[n], d_, m_, v_]

    loss = lax.psum(loss, ("x", "y", "c"))
    return (loss, gx, *[out[n][0] for n in WEIGHTS], *[out[n][1] for n in WEIGHTS], *[out[n][2] for n in WEIGHTS],
            *[out[n][3] for n in WEIGHTS])
```

```python
import functools

import numpy as np
import jax
import jax.numpy as jnp
from jax import lax
from jax.experimental import pallas as pl
from jax.experimental.pallas import tpu as pltpu

F32, BF16 = jnp.float32, jnp.bfloat16
S = jax.ShapeDtypeStruct
BS = pl.BlockSpec
MESH = pl.DeviceIdType.MESH

D = 1024
GRID_W = 64
SSD_INNER, SSD_G, SSD_N, SSD_L = 2048, 8, 128, 128
SSD_GW = 512
MLP_H = 4096
N_MOD = 6
ALPHA = 2.0 ** 0.25
LN_EPS, RMS_EPS = 1e-6, 1e-5
LRU_C = 8.0
P_XBC, P_LRU, P_Z, P_LG, P_MG, P_DT, P_W = 0, 4096, 5120, 7168, 8192, 10240, 10368
P_CB = 3456
IN_COLS = 10304
LRU_CB = 256
ADAM_LR, ADAM_B1, ADAM_B2, ADAM_EPS, ADAM_WD, ADAM_STEP = 0.001, 0.9, 0.999, 1e-08, 0.01, 10
VMEM_LIMIT = 56 * 2 ** 20


def _params(**kw):
    return pltpu.CompilerParams(vmem_limit_bytes=VMEM_LIMIT, **kw)


def _dot(a, b):
    return jnp.dot(a.astype(BF16), b.astype(BF16), preferred_element_type=F32)


def _dot_nt(a, b):
    return lax.dot_general(a.astype(BF16), b.astype(BF16), (((1,), (1,)), ((), ())), preferred_element_type=F32)


def _dot_tn(a, b):
    return lax.dot_general(a.astype(BF16), b.astype(BF16), (((0,), (0,)), ((), ())), preferred_element_type=F32)


@jax.custom_vjp
def _mm(a, b):
    return _dot(a, b)


def _cast_pair(a, b):
    return a.astype(BF16), b.astype(BF16)


def _mm_f(a, b):
    r = _cast_pair(a, b)
    return _dot(*r), r


def _mm_b(r, g):
    g = g.astype(BF16)
    return _dot_nt(g, r[1]), _dot_tn(r[0], g)


_mm.defvjp(_mm_f, _mm_b)


@jax.custom_vjp
def _mm_nt(a, b):
    return _dot_nt(a, b)


def _mm_nt_f(a, b):
    r = _cast_pair(a, b)
    return _dot_nt(*r), r


def _mm_nt_b(r, g):
    g = g.astype(BF16)
    return _dot(g, r[1]), _dot_tn(g, r[0])


_mm_nt.defvjp(_mm_nt_f, _mm_nt_b)


@jax.custom_vjp
def _mm_tn(a, b):
    return _dot_tn(a, b)


def _mm_tn_f(a, b):
    r = _cast_pair(a, b)
    return _dot_tn(*r), r


def _mm_tn_b(r, g):
    g = g.astype(BF16)
    return _dot_nt(r[1], g), _dot(r[0], g)


_mm_tn.defvjp(_mm_tn_f, _mm_tn_b)

def _split3(v):
    h = v.astype(BF16)
    r = v - h.astype(F32)
    m = r.astype(BF16)
    return h, m, (r - m.astype(F32)).astype(BF16)


def _sel_dot(sel, v, dims):
    sel_first = dims[0] == "s"
    dn = {"sv": (((1,), (0,)), ((), ())), "sTv": (((0,), (0,)), ((), ())), "vs": (((1,), (0,)), ((), ())),
          "vsT": (((1,), (1,)), ((), ()))}[dims]
    out = None
    for part in _split3(v):
        a, b = (sel, part) if sel_first else (part, sel)
        term = lax.dot_general(a, b, dn, preferred_element_type=F32)
        out = term if out is None else out + term
    return out


@jax.custom_vjp
def _cum_mm(tri, v):
    return _sel_dot(tri, v, "sv")


_cum_mm.defvjp(lambda tri, v: (_sel_dot(tri, v, "sv"), tri),
               lambda tri, g: (jnp.zeros_like(tri), _sel_dot(tri, g, "sTv")))


@jax.custom_vjp
def _xp_mm(v, e):
    return _sel_dot(e, v, "vs")


_xp_mm.defvjp(lambda v, e: (_sel_dot(e, v, "vs"), e),
              lambda e, g: (_sel_dot(e, g, "vsT"), jnp.zeros_like(e)))


def _sigmoid(x):
    return 1.0 / (1.0 + jnp.exp(-x))


def _silu(x):
    return x * _sigmoid(x)


def _dsilu(x):
    s = _sigmoid(x)
    return s * (1.0 + x * (1.0 - s))


def _softplus(x):
    return jnp.maximum(x, 0.0) + jnp.log1p(jnp.exp(-jnp.abs(x)))


def _gelu(x):
    return 0.5 * x * (1.0 + jnp.tanh(0.7978845608028654 * (x + 0.044715 * x * x * x)))


def _ln(x):
    mu = jnp.mean(x, axis=-1, keepdims=True)
    xc = x - mu
    var = jnp.mean(xc * xc, axis=-1, keepdims=True)
    return xc * lax.rsqrt(var + LN_EPS)


def _modln(x, shift, scale):
    return _ln(x) * (1.0 + scale) + shift


def _resln(x, sub, gate, g, b):
    return _ln(ALPHA * x + gate * sub) * g + b


def _grms(y, z, w):
    u = y * _silu(z)
    return u * lax.rsqrt(jnp.mean(u * u, axis=-1, keepdims=True) + RMS_EPS) * w


def _colsum(v):
    return jnp.sum(v, axis=0, keepdims=True)


def _conv_taps(width, period):
    def masks(rows):
        pos = lax.broadcasted_iota(jnp.int32, (rows, width), 0) & (period - 1)
        return [jnp.logical_and(pos + (k - 2) >= 0, pos + (k - 2) < period) for k in range(4)]
    return masks


def _conv_fwd(raw, w, b, masks):
    rows = raw.shape[0]
    pre = b + raw * w[2:3, :]
    for k in (0, 1, 3):
        sh = pltpu.roll(raw, (2 - k) % rows, 0)
        pre = pre + jnp.where(masks[k], sh, 0.0) * w[k:k + 1, :]
    return pre


def _for_rows(t, rb, fn):
    n = t // rb
    unroll = 4 if n % 4 == 0 else 1

    def step(i, carry):
        for u in range(unroll):
            fn(pl.multiple_of((i * unroll + u) * rb, rb))
        return carry

    lax.fori_loop(0, n // unroll, step, 0)


def _loop_unrolled(n, unroll, body, init):
    def step(i, carry):
        for u in range(unroll):
            carry = body(i * unroll + u, carry)
        return carry

    return lax.fori_loop(0, n // unroll, step, init)


def _conv_bwd(dpre, raw, w, masks):
    rows = raw.shape[0]
    draw = dpre * w[2:3, :]
    dws = []
    for k in range(4):
        if k == 2:
            dws.append(_colsum(dpre * raw))
            continue
        sh = pltpu.roll(raw, (2 - k) % rows, 0)
        dws.append(_colsum(dpre * jnp.where(masks[k], sh, 0.0)))
        back = pltpu.roll(jnp.where(masks[k], dpre, 0.0) * w[k:k + 1, :], (k - 2) % rows, 0)
        draw = draw + back
    return draw, jnp.concatenate(dws, axis=0), _colsum(dpre)


def _mod_fwd(cc, wmod, bmod):
    def body(cc_ref, w_ref, b_ref, o_ref):
        o_ref[...] = _dot(_silu(cc_ref[...]), w_ref[...]) + b_ref[...]

    return pl.pallas_call(
        body, grid=(N_MOD,), name="mod_fwd",
        in_specs=[BS((8, D), lambda j: (0, 0)), BS((D, D), lambda j: (0, j)), BS((1, D), lambda j: (0, j))],
        out_specs=BS((8, D), lambda j: (0, j)), out_shape=S((8, N_MOD * D), F32), compiler_params=_params(),
    )(cc, wmod, bmod)


def _mod_bwd(cc, wmod, dm):
    def body(cc_ref, w_ref, dm_ref, dw_ref, db_ref, dcc_ref):
        j = pl.program_id(0)
        c = cc_ref[...]
        dmv = dm_ref[...]
        dw_ref[...] = _dot_tn(_silu(c), dmv)
        db_ref[...] = _colsum(dmv)

        @pl.when(j == 0)
        def _():
            dcc_ref[...] = jnp.zeros_like(dcc_ref)

        dcc_ref[...] += _dot_nt(dmv, w_ref[...]) * _dsilu(c)

    return pl.pallas_call(
        body, grid=(N_MOD,), name="mod_bwd",
        in_specs=[BS((8, D), lambda j: (0, 0)), BS((D, D), lambda j: (0, j)), BS((8, D), lambda j: (0, j))],
        out_specs=[BS((D, D), lambda j: (0, j)), BS((1, D), lambda j: (0, j)), BS((8, D), lambda j: (0, 0))],
        out_shape=[S((D, N_MOD * D), F32), S((1, N_MOD * D), F32), S((8, D), F32)], compiler_params=_params(),
    )(cc, wmod, dm)


def _inproj_fwd(xa, m4, win, rt, n_lat_tiles, tiles_per_b, ctx_row, name):
    n_tiles = xa.shape[0] // rt

    def mrow(i):
        return jnp.where(i < n_lat_tiles, i // tiles_per_b, ctx_row)

    def body(x_ref, sh_ref, sc_ref, w_hbm, p_ref, h_ref, w_vm, sem):
        @pl.when(pl.program_id(0) == 0)
        def _():
            cp = pltpu.make_async_copy(w_hbm, w_vm, sem)
            cp.start()
            cp.wait()

        hb = _modln(x_ref[...], sh_ref[...], sc_ref[...]).astype(BF16)
        h_ref[...] = hb
        for j in range(P_W // P_CB):
            sl = slice(j * P_CB, (j + 1) * P_CB)
            p_ref[:, sl] = jnp.dot(hb, w_vm[:, sl], preferred_element_type=F32)

    return pl.pallas_call(
        body, grid=(n_tiles,), name=name,
        in_specs=[BS((rt, D), lambda i: (i, 0)),
                  BS((None, None, 1, D), lambda i: (mrow(i), 0, 0, 0)),
                  BS((None, None, 1, D), lambda i: (mrow(i), 1, 0, 0)),
                  BS(memory_space=pl.ANY)],
        out_specs=[BS((rt, P_W), lambda i: (i, 0)), BS((rt, D), lambda i: (i, 0))],
        out_shape=[S((xa.shape[0], P_W), F32), S((xa.shape[0], D), BF16)],
        scratch_shapes=[pltpu.VMEM((D, P_W), BF16), pltpu.SemaphoreType.DMA(())], compiler_params=_params(),
    )(xa, m4, m4, win)


def _inproj_bwd(xa, m4, win, dproj, rt, tile0, n_tiles, tiles_per_b, ctx_row, latent, dxres):
    def mrow(i):
        return (i // tiles_per_b) if latent else ctx_row

    def body(x_ref, sh_ref, sc_ref, dp_ref, w_hbm, *rest):
        if latent:
            dxr_ref, gx_ref, dm_ref, w_vm, sem = rest
        else:
            dm_ref, w_vm, sem = rest
        i = pl.program_id(0)

        @pl.when(i == 0)
        def _():
            cp = pltpu.make_async_copy(w_hbm, w_vm, sem)
            cp.start()
            cp.wait()

        dh = lax.dot_general(dp_ref[...], w_vm[...], (((1,), (1,)), ((), ())), preferred_element_type=F32)
        _, vjp = jax.vjp(_modln, x_ref[...], sh_ref[...], sc_ref[...])
        dx, dsh, dsc = vjp(dh)
        if latent:
            gx_ref[...] = dx + dxr_ref[...]

        @pl.when(i % tiles_per_b == 0)
        def _():
            dm_ref[...] = jnp.zeros_like(dm_ref)

        dm_ref[0:1, :] += dsh
        dm_ref[1:2, :] += dsc

    nb = n_tiles // tiles_per_b
    in_specs = [BS((rt, D), lambda i: (tile0 + i, 0)),
                BS((None, None, 1, D), lambda i: (mrow(i), 0, 0, 0)),
                BS((None, None, 1, D), lambda i: (mrow(i), 1, 0, 0)),
                BS((rt, P_W), lambda i: (tile0 + i, 0)),
                BS(memory_space=pl.ANY)]
    args = [xa, m4, m4, dproj, win]
    dm_spec = BS((None, 2, D), lambda i: (i // tiles_per_b, 0, 0))
    if latent:
        in_specs.append(BS((rt, D), lambda i: (i, 0)))
        args.append(dxres)
        out_specs = [BS((rt, D), lambda i: (i, 0)), dm_spec]
        out_shape = [S((n_tiles * rt, D), F32), S((nb, 2, D), F32)]
    else:
        out_specs = [dm_spec]
        out_shape = [S((nb, 2, D), F32)]
    return pl.pallas_call(
        body, grid=(n_tiles,), name="inproj_bwd_lat" if latent else "inproj_bwd_ctx",
        in_specs=in_specs, out_specs=out_specs, out_shape=out_shape,
        scratch_shapes=[pltpu.VMEM((D, P_W), BF16), pltpu.SemaphoreType.DMA(())],
        compiler_params=_params(),
    )(*args)


def _matmul_tn(a, b, tm, tn, tk, name):
    k, m = a.shape
    n = b.shape[1]

    def body(a_ref, b_ref, o_ref):
        @pl.when(pl.program_id(2) == 0)
        def _():
            o_ref[...] = jnp.zeros_like(o_ref)

        o_ref[...] += lax.dot_general(a_ref[...], b_ref[...], (((0,), (0,)), ((), ())), preferred_element_type=F32)

    return pl.pallas_call(
        body, grid=(m // tm, n // tn, k // tk), name=name,
        in_specs=[BS((tk, tm), lambda i, j, kk: (kk, i)), BS((tk, tn), lambda i, j, kk: (kk, j))],
        out_specs=BS((tm, tn), lambda i, j, kk: (i, j)), out_shape=S((m, n), F32), compiler_params=_params(),
    )(a, b)


def _matmul_tn2(a1, b1, a2, b2, tm, tn, tk, name):
    k1, m = a1.shape
    n = b1.shape[1]
    n1, n2 = k1 // tk, a2.shape[0] // tk

    def body(a1_ref, b1_ref, a2_ref, b2_ref, o_ref):
        kk = pl.program_id(2)

        @pl.when(kk == 0)
        def _():
            o_ref[...] = jnp.zeros_like(o_ref)

        @pl.when(kk < n1)
        def _():
            o_ref[...] += lax.dot_general(a1_ref[...], b1_ref[...], (((0,), (0,)), ((), ())), preferred_element_type=F32)

        @pl.when(kk >= n1)
        def _():
            o_ref[...] += lax.dot_general(a2_ref[...], b2_ref[...], (((0,), (0,)), ((), ())), preferred_element_type=F32)

    first = lambda kk: jnp.minimum(kk, n1 - 1)
    second = lambda kk: jnp.maximum(kk - n1, 0)
    return pl.pallas_call(
        body, grid=(m // tm, n // tn, n1 + n2), name=name,
        in_specs=[BS((tk, tm), lambda i, j, kk: (first(kk), i)), BS((tk, tn), lambda i, j, kk: (first(kk), j)),
                  BS((tk, tm), lambda i, j, kk: (second(kk), i)), BS((tk, tn), lambda i, j, kk: (second(kk), j))],
        out_specs=BS((tm, tn), lambda i, j, kk: (i, j)), out_shape=S((m, n), F32), compiler_params=_params(),
    )(a1, b1, a2, b2)


def _ssd_consts(heads_per_tile):
    n = SSD_L
    ii = lax.broadcasted_iota(jnp.int32, (n, n), 0)
    jj = lax.broadcasted_iota(jnp.int32, (n, n), 1)
    er = lax.broadcasted_iota(jnp.int32, (128, 256), 0)
    ec = lax.broadcasted_iota(jnp.int32, (128, 256), 1) >> 6
    lane = lax.broadcasted_iota(jnp.int32, (1, 128 * heads_per_tile), 1) >> 6
    per_dir = []
    for d in (0, 1):
        mask = (jj >= ii) if d else (jj <= ii)
        per_dir.append((mask, mask.astype(BF16), (er == ec + 4 * d).astype(BF16)))
    return per_dir, [(lane == h).astype(F32) for h in range(2 * heads_per_tile)]


def _ssd_chunk(x, bm, cm, dtc, dtx, alog, hst, consts, hmasks, rev):
    n = SSD_L
    mask, tri, e = consts
    cum = _cum_mm(tri, dtc * (-jnp.exp(alog)))
    cum_x = _xp_mm(cum, e)
    tot_x = cum_x[0:1, :] if rev else cum_x[n - 1:n, :]
    xd = x * dtx
    hn = jnp.exp(tot_x) * hst + _mm_tn(bm, xd * jnp.exp(tot_x - cum_x))
    if cm is None:
        return hn
    cum_t = cum.T
    cb = _mm_nt(cm, bm)
    if len(hmasks) == 4:
        y = jnp.exp(cum_x) * _mm(cm, hst)
        for h in range(4):
            k = 4 * rev + h
            decay = jnp.exp(jnp.where(mask, cum[:, k:k + 1] - cum_t[k:k + 1, :], -1e30))
            y = y + _mm(cb * decay, xd * hmasks[h])
        return y, hn
    pairs = []
    for p in range(2):
        xdp = xd[:, 128 * p:128 * p + 128]
        yp = None
        for hh in range(2):
            k = 4 * rev + 2 * p + hh
            decay = jnp.exp(jnp.where(mask, cum[:, k:k + 1] - cum_t[k:k + 1, :], -1e30))
            term = _mm(cb * decay, xdp * hmasks[hh])
            yp = term if yp is None else yp + term
        pairs.append(yp)
    return jnp.exp(cum_x) * _mm(cm, hst) + jnp.concatenate(pairs, axis=1), hn


def _ssd_fwd(proj, dtg, cw, cb, dtb, alog, drow, h0f, h0b, nb, t, period, blk0, need_y, gather=()):
    nc = t // SSD_L
    rb = period
    assert t % rb == 0
    unroll = 4 if nc % 4 == 0 else (2 if nc % 2 == 0 else 1)
    masks_of = _conv_taps(SSD_GW, period)
    ng = len(gather)
    n_out = 5 if need_y else 4

    def body(p_ref, dt_ref, cw_ref, cb_ref, dtb_ref, al_ref, d_ref, h0f_ref, h0b_ref, *rest):
        g_ins, rest = rest[:ng], rest[ng:]
        outs, g_outs, (act, dts, dtxs), g_sems = rest[:n_out], rest[n_out:n_out + ng], rest[n_out + ng:n_out + ng + 3], \
            rest[n_out + ng + 3:]
        if need_y:
            y_ref, hsf_ref, hsb_ref, sf_ref, sb_ref = outs
        else:
            hsf_ref, hsb_ref, sf_ref, sb_ref = outs
        if ng:
            exchange = _GatherExchange(g_ins, g_outs, g_sems)
            step = pl.program_id(0) * SSD_G + pl.program_id(1)
            pl.when(step == 0)(exchange.begin)
        masks = masks_of(rb)
        per_dir, hmasks = _ssd_consts(2)

        def prologue(r0):
            rows = pl.ds(r0, rb)
            a = _silu(_conv_fwd(p_ref[rows, :], cw_ref[...], cb_ref[...], masks))
            act[rows, :] = a
            if need_y:
                y_ref[rows, :] = d_ref[...] * a[:, 0:256]
            dtv = _softplus(dt_ref[rows, :] + dtb_ref[...])
            dts[rows, :] = dtv
            for d in (0, 1):
                dtxs[rows, 256 * d:256 * d + 256] = _xp_mm(dtv, per_dir[d][2])

        _for_rows(t, rb, prologue)
        al = al_ref[...]

        def chunk(ci, carry):
            out = []
            for d, hst, hs_ref in ((0, carry[0], hsf_ref), (1, carry[1], hsb_ref)):
                c = (nc - 1 - ci) if d else ci
                r0 = pl.multiple_of(c * SSD_L, SSD_L)
                a = act[pl.ds(r0, SSD_L), :]
                hs_ref[c] = hst
                res = _ssd_chunk(a[:, 0:256], a[:, 256:384], a[:, 384:512] if need_y else None,
                                 dts[pl.ds(r0, SSD_L), :], dtxs[pl.ds(r0, SSD_L), 256 * d:256 * d + 256], al, hst,
                                 per_dir[d], hmasks, d)
                if need_y:
                    y_ref[pl.ds(r0, SSD_L), :] += res[0]
                    res = res[1]
                out.append(res)
            return tuple(out)

        sf_ref[...], sb_ref[...] = _loop_unrolled(nc, unroll, chunk, (h0f_ref[...], h0b_ref[...]))
        if ng:
            pl.when(step == nb * SSD_G - 1)(exchange.finish)

    gspec = lambda shp: BS((None,) + shp, lambda b, g: (g,) + (0,) * len(shp))
    st_spec = BS((None, None, SSD_N, 256), lambda b, g: (b, g, 0, 0))
    hs_spec = BS((None, None, nc, SSD_N, 256), lambda b, g: (b, g, 0, 0, 0))
    in_specs = [BS((t, SSD_GW), lambda b, g: (blk0 + b, g)), BS((None, t, 128), lambda b, g: (g, blk0 + b, 0)),
                gspec((4, SSD_GW)), gspec((1, SSD_GW)), gspec((1, 128)), gspec((1, 128)), gspec((1, 256)),
                st_spec, st_spec]
    out_specs = [hs_spec, hs_spec, st_spec, st_spec]
    out_shape = [S((nb, SSD_G, nc, SSD_N, 256), F32)] * 2 + [S((nb, SSD_G, SSD_N, 256), F32)] * 2
    if need_y:
        out_specs = [BS((t, 256), lambda b, g: (b, g))] + out_specs
        out_shape = [S((nb * t, SSD_INNER), F32)] + out_shape
    return pl.pallas_call(
        body, grid=(nb, SSD_G), name="ssd_fwd_lat" if need_y else "ssd_fwd_ctx",
        in_specs=in_specs + [_HBM] * ng, out_specs=out_specs + [_HBM] * ng,
        out_shape=out_shape + _gather_out_shapes(gather),
        scratch_shapes=[pltpu.VMEM((t, SSD_GW), F32), pltpu.VMEM((t, 128), F32), pltpu.VMEM((t, SSD_GW), F32)]
        + (_gather_sems(ng) if ng else []),
        compiler_params=_params(),
    )(proj, dtg, cw, cb, dtb, alog, drow, h0f, h0b, *gather)


def _ssd_bwd(proj, dtg, cw, cb, dtb, alog, drow, hsf, hsb, dy, dsf, dsb, dproj, nb, t, period, blk0, need_y):
    nc = t // SSD_L
    rb = period
    assert t % rb == 0
    unroll = 2 if nc % 2 == 0 else 1
    masks_of = _conv_taps(SSD_GW, period)

    def body(*refs):
        if need_y:
            (p_ref, dt_ref, cw_ref, cb_ref, dtb_ref, al_ref, d_ref, hsf_ref, hsb_ref, dy_ref, dsf_ref, dsb_ref, _,
             dp_ref, ddt_ref, dhf_ref, dhb_ref, dcw_ref, dcb_ref, ddtb_ref, dal_ref, dd_ref,
             pre, dact, dts, ddts, dtxs) = refs
        else:
            (p_ref, dt_ref, cw_ref, cb_ref, dtb_ref, al_ref, d_ref, hsf_ref, hsb_ref, dsf_ref, dsb_ref, _,
             dp_ref, ddt_ref, dhf_ref, dhb_ref, dcw_ref, dcb_ref, ddtb_ref, dal_ref, dd_ref,
             pre, dact, dts, ddts, dtxs) = refs
            dy_ref = None
        b, g = pl.program_id(0), pl.program_id(1)

        @pl.when(jnp.logical_and(b == 0, g == 0))
        def _():
            for r in (dcw_ref, dcb_ref, ddtb_ref, dal_ref, dd_ref):
                r[...] = jnp.zeros_like(r)

        masks = masks_of(rb)
        per_dir, hmasks = _ssd_consts(1)

        def prologue(r0):
            rows = pl.ds(r0, rb)
            pre[rows, :] = _conv_fwd(p_ref[rows, :], cw_ref[...], cb_ref[...], masks)
            dtv = _softplus(dt_ref[rows, :] + dtb_ref[...])
            dts[rows, :] = dtv
            for d in (0, 1):
                dtxs[rows, 256 * d:256 * d + 256] = _xp_mm(dtv, per_dir[d][2])
            dact[rows, :] = jnp.zeros((rb, SSD_GW), F32)
            ddts[rows, :] = jnp.zeros((rb, 128), F32)

        _for_rows(t, rb, prologue)
        al = al_ref[...]
        def chunk(ci, carry):
            dal_c = carry[2]
            dhs_out = []
            for d, dh, hs_ref in ((0, carry[0], hsf_ref), (1, carry[1], hsb_ref)):
                c = ci if d else (nc - 1 - ci)
                r0 = pl.multiple_of(c * SSD_L, SSD_L)
                a = _silu(pre[pl.ds(r0, SSD_L), :])
                dtc = dts[pl.ds(r0, SSD_L), :]
                dtx = dtxs[pl.ds(r0, SSD_L), 256 * d:256 * d + 256]
                if need_y:
                    fn = lambda x_, bm_, cm_, dt_, dx_, al_, hs_: _ssd_chunk(x_, bm_, cm_, dt_, dx_, al_, hs_, per_dir[d],
                                                                             hmasks, d)
                    _, vjp = jax.vjp(fn, a[:, 0:256], a[:, 256:384], a[:, 384:512], dtc, dtx, al, hs_ref[c])
                    dx, dbm, dcm, ddtc, ddtx, dal_k, dhs = vjp((dy_ref[pl.ds(r0, SSD_L), :], dh))
                    dact[pl.ds(r0, SSD_L), 384:512] += dcm
                else:
                    fn = lambda x_, bm_, dt_, dx_, al_, hs_: _ssd_chunk(x_, bm_, None, dt_, dx_, al_, hs_, per_dir[d],
                                                                        hmasks, d)
                    _, vjp = jax.vjp(fn, a[:, 0:256], a[:, 256:384], dtc, dtx, al, hs_ref[c])
                    dx, dbm, ddtc, ddtx, dal_k, dhs = vjp(dh)
                dact[pl.ds(r0, SSD_L), 0:256] += dx
                dact[pl.ds(r0, SSD_L), 256:384] += dbm
                ddts[pl.ds(r0, SSD_L), :] += ddtc + _dot_nt(ddtx, per_dir[d][2])
                dhs_out.append(dhs)
                dal_c = dal_c + dal_k
            return dhs_out[0], dhs_out[1], dal_c

        dhf_ref[...], dhb_ref[...], dal_acc = _loop_unrolled(
            nc, unroll, chunk, (dsf_ref[...], dsb_ref[...], jnp.zeros((1, 128), F32)))

        def epilogue(r0):
            rows = pl.ds(r0, rb)
            prev = pre[rows, :]
            if need_y:
                dyv = dy_ref[rows, :]
                dact[rows, 0:256] += d_ref[...] * dyv
                dd_ref[g] += _colsum(dyv * _silu(prev[:, 0:256]))
            dpre = dact[rows, :] * _dsilu(prev)
            draw, dcw, dcb = _conv_bwd(dpre, p_ref[rows, :], cw_ref[...], masks)
            dp_ref[rows, :] = draw.astype(BF16)
            dcw_ref[g] += dcw
            dcb_ref[g] += dcb
            ddraw = ddts[rows, :] * _sigmoid(dt_ref[rows, :] + dtb_ref[...])
            ddt_ref[rows, :] = ddraw
            ddtb_ref[g] += _colsum(ddraw)

        _for_rows(t, rb, epilogue)
        dal_ref[g] += dal_acc

    gspec = lambda shp: BS((None,) + shp, lambda b, g: (g,) + (0,) * len(shp))
    full = lambda shp: BS(shp, lambda b, g: (0,) * len(shp))
    st_spec = BS((None, None, SSD_N, 256), lambda b, g: (b, g, 0, 0))
    hs_spec = BS((None, None, nc, SSD_N, 256), lambda b, g: (b, g, 0, 0, 0))
    p_spec = BS((t, SSD_GW), lambda b, g: (blk0 + b, g))
    in_specs = [p_spec, BS((None, t, 128), lambda b, g: (g, blk0 + b, 0)),
                gspec((4, SSD_GW)), gspec((1, SSD_GW)), gspec((1, 128)), gspec((1, 128)), gspec((1, 256)),
                hs_spec, hs_spec]
    args = [proj, dtg, cw, cb, dtb, alog, drow, hsf, hsb]
    if need_y:
        in_specs.append(BS((t, 256), lambda b, g: (b, g)))
        args.append(dy)
    in_specs += [st_spec, st_spec, BS(memory_space=pl.ANY)]
    args += [dsf, dsb, dproj]
    out_specs = [p_spec, BS((None, t, 128), lambda b, g: (g, b, 0)), st_spec, st_spec,
                 full((SSD_G, 4, SSD_GW)), full((SSD_G, 1, SSD_GW)), full((SSD_G, 1, 128)), full((SSD_G, 1, 128)),
                 full((SSD_G, 1, 256))]
    out_shape = [S(dproj.shape, BF16), S((SSD_G, nb * t, 128), F32),
                 S((nb, SSD_G, SSD_N, 256), F32), S((nb, SSD_G, SSD_N, 256), F32),
                 S((SSD_G, 4, SSD_GW), F32), S((SSD_G, 1, SSD_GW), F32), S((SSD_G, 1, 128), F32),
                 S((SSD_G, 1, 128), F32), S((SSD_G, 1, 256), F32)]
    return pl.pallas_call(
        body, grid=(nb, SSD_G), name="ssd_bwd_lat" if need_y else "ssd_bwd_ctx",
        in_specs=in_specs, out_specs=out_specs, out_shape=out_shape,
        input_output_aliases={len(args) - 1: 0},
        scratch_shapes=[pltpu.VMEM((t, SSD_GW), F32), pltpu.VMEM((t, SSD_GW), F32), pltpu.VMEM((t, 128), F32),
                        pltpu.VMEM((t, 128), F32), pltpu.VMEM((t, SSD_GW), F32)],
        compiler_params=_params(),
    )(*args)


def _lru_gate(u, wa, ba, wi, bi, lam):
    r = _sigmoid(_mm(u, wa) + ba)
    i = _sigmoid(_mm(u, wi) + bi)
    log_a = -LRU_C * r * _softplus(-lam)
    a = jnp.exp(log_a)
    x2 = 2.0 * log_a
    em1 = jnp.where(x2 > -0.01, x2 * (1.0 + x2 * (0.5 + x2 * (1.0 / 6.0 + x2 * (1.0 / 24.0)))), a * a - 1.0)
    return a, jnp.sqrt(-em1) * (i * u)


def _scan_pair(fwd, rev, nblk, width):
    row = lax.broadcasted_iota(jnp.int32, (8, width), 0)

    def block(a_ref, b_ref, h_ref, st, carry, reverse):
        av, bv = a_ref[pl.ds(st, 8), :], b_ref[pl.ds(st, 8), :]
        for s in (1, 2, 4):
            ok = (row < 8 - s) if reverse else (row >= s)
            sh = (8 - s) if reverse else s
            a_sh = jnp.where(ok, pltpu.roll(av, sh, 0), 1.0)
            b_sh = jnp.where(ok, pltpu.roll(bv, sh, 0), 0.0)
            bv = av * b_sh + bv
            av = av * a_sh
        h = bv + av * carry
        h_ref[pl.ds(st, 8), :] = h
        return h[0:1, :] if reverse else h[7:8, :]

    def step(i, carry):
        cf, cr = carry
        cf = block(fwd[0], fwd[1], fwd[2], pl.multiple_of(i * 8, 8), cf, False)
        cr = block(rev[0], rev[1], rev[2], pl.multiple_of((nblk - 1 - i) * 8, 8), cr, True)
        return cf, cr

    return lax.fori_loop(0, nblk, step, (fwd[3], rev[3]))


def _lru_specs(t, blk0):
    p_spec = BS((t, LRU_CB), lambda b, q: (blk0 + b, P_LRU // LRU_CB + q))
    w_spec = BS((2, 2, 128, 128), lambda b, q: (0, q, 0, 0))
    v_spec = BS((2, LRU_CB), lambda b, q: (0, q))
    c_spec = lambda r: BS((r, LRU_CB), lambda b, q: (0, q))
    s_spec = BS((None, 2, LRU_CB), lambda b, q: (b, 0, q))
    return p_spec, w_spec, v_spec, c_spec, s_spec


def _lru_fwd(proj, cw, cb, wa, ba, wi, bi, lam, h0, nb, t, period, blk0, need_y):
    nq = D // LRU_CB
    masks_of = _conv_taps(LRU_CB, period)

    def body(p_ref, cw_ref, cb_ref, wa_ref, ba_ref, wi_ref, bi_ref, lam_ref, h0_ref, *rest):
        if need_y:
            y_ref, hf_ref, hb_ref, fin_ref, sa0, sb0, sa1, sb1 = rest
        else:
            hf_ref, hb_ref, fin_ref, sa0, sb0, sa1, sb1 = rest
        u = _conv_fwd(p_ref[...], cw_ref[...], cb_ref[...], masks_of(t))
        for d, (sa, sb) in enumerate(((sa0, sb0), (sa1, sb1))):
            for j in range(2):
                sl = slice(128 * j, 128 * j + 128)
                a, bb = _lru_gate(u[:, sl], wa_ref[d, j], ba_ref[d:d + 1, sl], wi_ref[d, j], bi_ref[d:d + 1, sl],
                                  lam_ref[d:d + 1, sl])
                sa[:, sl] = a
                sb[:, sl] = bb
        lf, lb = _scan_pair((sa0, sb0, hf_ref, h0_ref[0:1, :]), (sa1, sb1, hb_ref, h0_ref[1:2, :]), t // 8, LRU_CB)
        fin_ref[0:1, :] = lf
        fin_ref[1:2, :] = lb
        if need_y:
            y_ref[...] = hf_ref[...] + hb_ref[...]

    p_spec, w_spec, v_spec, c_spec, s_spec = _lru_specs(t, blk0)
    o_spec = BS((t, LRU_CB), lambda b, q: (b, q))
    out_specs = [o_spec, o_spec, s_spec]
    out_shape = [S((nb * t, D), F32), S((nb * t, D), F32), S((nb, 2, D), F32)]
    if need_y:
        out_specs = [o_spec] + out_specs
        out_shape = [S((nb * t, D), F32)] + out_shape
    return pl.pallas_call(
        body, grid=(nb, nq), name="lru_fwd_lat" if need_y else "lru_fwd_ctx",
        in_specs=[p_spec, c_spec(4), c_spec(1), w_spec, v_spec, w_spec, v_spec, v_spec, s_spec],
        out_specs=out_specs, out_shape=out_shape,
        scratch_shapes=[pltpu.VMEM((t, LRU_CB), F32)] * 4, compiler_params=_params(),
    )(proj, cw, cb, wa, ba, wi, bi, lam, h0)


def _lru_bwd(proj, cw, cb, wa, ba, wi, bi, lam, h0, hf, hb, dy, dfin, dproj, nb, t, period, blk0, need_y):
    nq = D // LRU_CB
    rc = min(256, t)
    masks_of = _conv_taps(LRU_CB, period)

    def body(*refs):
        if need_y:
            (p_ref, cw_ref, cb_ref, wa_ref, ba_ref, wi_ref, bi_ref, lam_ref, h0_ref, hf_ref, hb_ref, dy_ref, dfin_ref, _,
             dp_ref, dh0_ref, dcw_ref, dcb_ref, dwa_ref, dwi_ref, dba_ref, dbi_ref, dlam_ref,
             su, sa0, sa1, sc0, sc1, sg0, sg1) = refs
        else:
            (p_ref, cw_ref, cb_ref, wa_ref, ba_ref, wi_ref, bi_ref, lam_ref, h0_ref, hf_ref, hb_ref, dfin_ref, _,
             dp_ref, dh0_ref, dcw_ref, dcb_ref, dwa_ref, dwi_ref, dba_ref, dbi_ref, dlam_ref,
             su, sa0, sa1, sc0, sc1, sg0, sg1) = refs
            dy_ref = None
        b, q = pl.program_id(0), pl.program_id(1)

        @pl.when(jnp.logical_and(b == 0, q == 0))
        def _():
            for r in (dcw_ref, dcb_ref, dwa_ref, dwi_ref, dba_ref, dbi_ref, dlam_ref):
                r[...] = jnp.zeros_like(r)

        masks = masks_of(t)
        u = _conv_fwd(p_ref[...], cw_ref[...], cb_ref[...], masks)
        su[...] = u
        for d, sa in enumerate((sa0, sa1)):
            for j in range(2):
                sl = slice(128 * j, 128 * j + 128)
                a, _unused = _lru_gate(u[:, sl], wa_ref[d, j], ba_ref[d:d + 1, sl], wi_ref[d, j], bi_ref[d:d + 1, sl],
                                       lam_ref[d:d + 1, sl])
                sa[:, sl] = a
        rowi = lax.broadcasted_iota(jnp.int32, (t, LRU_CB), 0)
        last, first = rowi == t - 1, rowi == 0
        sc0[...] = jnp.where(last, 0.0, pltpu.roll(sa0[...], t - 1, 0))
        sc1[...] = jnp.where(first, 0.0, pltpu.roll(sa1[...], 1, 0))
        g0 = jnp.where(last, dfin_ref[0:1, :], 0.0)
        g1 = jnp.where(first, dfin_ref[1:2, :], 0.0)
        if need_y:
            g0 = g0 + dy_ref[...]
            g1 = g1 + dy_ref[...]
        sg0[...] = g0
        sg1[...] = g1
        zero = jnp.zeros((1, LRU_CB), F32)
        _scan_pair((sc1, sg1, sg1, zero), (sc0, sg0, sg0, zero), t // 8, LRU_CB)
        dh0_ref[0:1, :] = sa0[0:1, :] * sg0[0:1, :]
        dh0_ref[1:2, :] = sa1[t - 1:t, :] * sg1[t - 1:t, :]
        sc0[...] = sg0[...] * jnp.where(first, h0_ref[0:1, :], pltpu.roll(hf_ref[...], 1, 0))
        sc1[...] = sg1[...] * jnp.where(last, h0_ref[1:2, :], pltpu.roll(hb_ref[...], t - 1, 0))

        def rows(ci, carry):
            r0 = pl.multiple_of(ci * rc, rc)
            for j in range(2):
                sl = slice(128 * j, 128 * j + 128)
                du = jnp.zeros((rc, 128), F32)
                for d, (sc, sg) in enumerate(((sc0, sg0), (sc1, sg1))):
                    _, vjp = jax.vjp(_lru_gate, su[pl.ds(r0, rc), sl], wa_ref[d, j], ba_ref[d:d + 1, sl], wi_ref[d, j],
                                     bi_ref[d:d + 1, sl], lam_ref[d:d + 1, sl])
                    du_d, dwa, dba, dwi, dbi, dlam = vjp((sc[pl.ds(r0, rc), sl], sg[pl.ds(r0, rc), sl]))
                    du = du + du_d
                    dwa_ref[d, 2 * q + j] += dwa
                    dwi_ref[d, 2 * q + j] += dwi
                    dba_ref[q, d:d + 1, sl] += dba
                    dbi_ref[q, d:d + 1, sl] += dbi
                    dlam_ref[q, d:d + 1, sl] += dlam
                sa0[pl.ds(r0, rc), sl] = du
            return carry

        lax.fori_loop(0, t // rc, rows, 0)
        draw, dcw, dcb = _conv_bwd(sa0[...], p_ref[...], cw_ref[...], masks)
        dp_ref[...] = draw.astype(BF16)
        dcw_ref[q] += dcw
        dcb_ref[q] += dcb

    p_spec, w_spec, v_spec, c_spec, s_spec = _lru_specs(t, blk0)
    o_spec = BS((t, LRU_CB), lambda b, q: (b, q))
    full = lambda shp: BS(shp, lambda b, q: (0,) * len(shp))
    in_specs = [p_spec, c_spec(4), c_spec(1), w_spec, v_spec, w_spec, v_spec, v_spec, s_spec, o_spec, o_spec]
    args = [proj, cw, cb, wa, ba, wi, bi, lam, h0, hf, hb]
    if need_y:
        in_specs.append(o_spec)
        args.append(dy)
    in_specs += [s_spec, BS(memory_space=pl.ANY)]
    args += [dfin, dproj]
    out_specs = [p_spec, s_spec, full((nq, 4, LRU_CB)), full((nq, 1, LRU_CB)), full((2, 8, 128, 128)),
                 full((2, 8, 128, 128)), full((nq, 2, LRU_CB)), full((nq, 2, LRU_CB)), full((nq, 2, LRU_CB))]
    out_shape = [S(dproj.shape, BF16), S((nb, 2, D), F32), S((nq, 4, LRU_CB), F32), S((nq, 1, LRU_CB), F32),
                 S((2, 8, 128, 128), F32), S((2, 8, 128, 128), F32), S((nq, 2, LRU_CB), F32), S((nq, 2, LRU_CB), F32),
                 S((nq, 2, LRU_CB), F32)]
    return pl.pallas_call(
        body, grid=(nb, nq), name="lru_bwd_lat" if need_y else "lru_bwd_ctx",
        in_specs=in_specs, out_specs=out_specs, out_shape=out_shape, input_output_aliases={len(args) - 1: 0},
        scratch_shapes=[pltpu.VMEM((t, LRU_CB), F32)] * 7, compiler_params=_params(),
    )(*args)


def _mix_core(y_ref, yl_ref, p_ref, nw_ref, bg_ref, wbs_ref, wbl_ref, wo_ref, nrm_s):
    for g in range(SSD_G):
        sl = slice(256 * g, 256 * g + 256)
        nrm_s[:, sl] = _grms(y_ref[:, sl], p_ref[:, sl], nw_ref[:, sl]).astype(BF16)
    br_s = jnp.dot(nrm_s[...], wbs_ref[...], preferred_element_type=F32)
    gl = (yl_ref[...] * _gelu(p_ref[:, 2048:3072])).astype(BF16)
    br_l = jnp.dot(gl, wbl_ref[...], preferred_element_type=F32)
    gs = _sigmoid(p_ref[:, 3072:4096] + bg_ref[:, 0:D])
    gr = _sigmoid(p_ref[:, 4096:5120] + bg_ref[:, D:2 * D])
    mix = (gs * br_s + gr * br_l).astype(BF16)
    xmix = jnp.dot(mix, wo_ref[...], preferred_element_type=F32)
    return br_s, gl, br_l, gs, gr, mix, xmix


def _mix_specs(rt, tiles_per_b):
    row = lambda w: BS((rt, w), lambda i: (i, 0))
    const = lambda shp: BS(shp, lambda i: (0,) * len(shp))
    gate = BS((None, None, 1, D), lambda i: (i // tiles_per_b, 2, 0, 0))
    return row, const, gate


def _mix_fwd(y, ylru, proj, x, m4, wbs, wbl, wo, nw, bg, l1g, l1b, rt, tiles_per_b):
    n = x.shape[0]

    def body(y_ref, yl_ref, p_ref, x_ref, g1_ref, wbs_ref, wbl_ref, wo_ref, nw_ref, bg_ref, lg_ref, lb_ref,
             x1_ref, nrm_ref, gl_ref, mix_ref, brs_ref, brl_ref, xm_ref):
        br_s, gl, br_l, _, _, mix, xmix = _mix_core(y_ref, yl_ref, p_ref, nw_ref, bg_ref, wbs_ref, wbl_ref, wo_ref, nrm_ref)
        gl_ref[...] = gl
        mix_ref[...] = mix
        brs_ref[...] = br_s
        brl_ref[...] = br_l
        xm_ref[...] = xmix
        x1_ref[...] = _resln(x_ref[...], xmix, g1_ref[...], lg_ref[...], lb_ref[...])

    row, const, gate = _mix_specs(rt, tiles_per_b)
    return pl.pallas_call(
        body, grid=(n // rt,), name="mix_fwd",
        in_specs=[row(SSD_INNER), row(D), BS((rt, 5120), lambda i: (i, 1)), row(D), gate,
                  const((SSD_INNER, D)), const((D, D)), const((D, D)), const((1, SSD_INNER)), const((1, 2 * D)),
                  const((1, D)), const((1, D))],
        out_specs=[row(D), row(SSD_INNER), row(D), row(D), row(D), row(D), row(D)],
        out_shape=[S((n, D), F32), S((n, SSD_INNER), BF16), S((n, D), BF16), S((n, D), BF16), S((n, D), F32),
                   S((n, D), F32), S((n, D), F32)],
        compiler_params=_params(),
    )(y, ylru, proj, x, m4, wbs, wbl, wo, nw, bg, l1g, l1b)


def _mix_bwd(y, ylru, proj, x, m4, wbs, wbl, wo, nw, bg, l1g, l1b, brs, brl, xmix, dx1, dproj, rt, tiles_per_b):
    n = x.shape[0]

    def body(y_ref, yl_ref, p_ref, x_ref, g1_ref, wbs_ref, wbl_ref, wo_ref, nw_ref, bg_ref, lg_ref, lb_ref,
             brs_ref, brl_ref, xm_ref, dx1_ref, _,
             dp_ref, dy_ref, dyl_ref, dxr_ref, dbrs_ref, dbrl_ref, dxm_ref,
             dg1_ref, dnw_ref, dbg_ref, dlg_ref, dlb_ref):
        i = pl.program_id(0)

        @pl.when(i == 0)
        def _():
            for r in (dnw_ref, dbg_ref, dlg_ref, dlb_ref):
                r[...] = jnp.zeros_like(r)

        @pl.when(i % tiles_per_b == 0)
        def _():
            dg1_ref[...] = jnp.zeros_like(dg1_ref)

        br_s, br_l = brs_ref[...], brl_ref[...]
        gs = _sigmoid(p_ref[:, 3072:4096] + bg_ref[:, 0:D])
        gr = _sigmoid(p_ref[:, 4096:5120] + bg_ref[:, D:2 * D])
        _, vjp = jax.vjp(_resln, x_ref[...], xm_ref[...], g1_ref[...], lg_ref[...], lb_ref[...])
        dxr, dxmix, dg1, dlg, dlb = vjp(dx1_ref[...])
        dxr_ref[...] = dxr
        dg1_ref[...] += dg1
        dlg_ref[...] += dlg
        dlb_ref[...] += dlb
        dxmb = dxmix.astype(BF16)
        dxm_ref[...] = dxmb
        dmix = lax.dot_general(dxmb, wo_ref[...], (((1,), (1,)), ((), ())), preferred_element_type=F32)
        dbrs = (dmix * gs).astype(BF16)
        dbrl = (dmix * gr).astype(BF16)
        dbrs_ref[...] = dbrs
        dbrl_ref[...] = dbrl
        dmg_s = dmix * br_s * gs * (1.0 - gs)
        dmg_r = dmix * br_l * gr * (1.0 - gr)
        dp_ref[:, 3072:4096] = dmg_s.astype(BF16)
        dp_ref[:, 4096:5120] = dmg_r.astype(BF16)
        dbg_ref[:, 0:D] += _colsum(dmg_s)
        dbg_ref[:, D:2 * D] += _colsum(dmg_r)
        dnrm = lax.dot_general(dbrs, wbs_ref[...], (((1,), (1,)), ((), ())), preferred_element_type=F32)
        for g in range(SSD_G):
            sl = slice(256 * g, 256 * g + 256)
            _, vjp = jax.vjp(_grms, y_ref[:, sl], p_ref[:, sl], nw_ref[:, sl])
            dyg, dzg, dnwg = vjp(dnrm[:, sl])
            dy_ref[:, sl] = dyg
            dp_ref[:, sl] = dzg.astype(BF16)
            dnw_ref[:, sl] += dnwg
        dgl = lax.dot_general(dbrl, wbl_ref[...], (((1,), (1,)), ((), ())), preferred_element_type=F32)
        _, vjp = jax.vjp(lambda a, c: a * _gelu(c), yl_ref[...], p_ref[:, 2048:3072])
        dyl, dlgate = vjp(dgl)
        dyl_ref[...] = dyl
        dp_ref[:, 2048:3072] = dlgate.astype(BF16)

    row, const, gate = _mix_specs(rt, tiles_per_b)
    pblk = BS((rt, 5120), lambda i: (i, 1))
    nb = n // (rt * tiles_per_b)
    out_specs = [pblk, row(SSD_INNER), row(D), row(D), row(D), row(D), row(D),
                 BS((None, 1, D), lambda i: (i // tiles_per_b, 0, 0)), const((1, SSD_INNER)), const((1, 2 * D)),
                 const((1, D)), const((1, D))]
    out_shape = [S(dproj.shape, BF16), S((n, SSD_INNER), F32), S((n, D), F32), S((n, D), F32),
                 S((n, D), BF16), S((n, D), BF16), S((n, D), BF16),
                 S((nb, 1, D), F32), S((1, SSD_INNER), F32), S((1, 2 * D), F32), S((1, D), F32), S((1, D), F32)]
    return pl.pallas_call(
        body, grid=(n // rt,), name="mix_bwd",
        in_specs=[row(SSD_INNER), row(D), pblk, row(D), gate,
                  const((SSD_INNER, D)), const((D, D)), const((D, D)), const((1, SSD_INNER)), const((1, 2 * D)),
                  const((1, D)), const((1, D)), row(D), row(D), row(D), row(D), BS(memory_space=pl.ANY)],
        out_specs=out_specs, out_shape=out_shape, input_output_aliases={16: 0},
        compiler_params=_params(),
    )(y, ylru, proj, x, m4, wbs, wbl, wo, nw, bg, l1g, l1b, brs, brl, xmix, dx1, dproj)


def _mlp_step(x1, tgt, m4, w1, b1, w2, b2, l2g, l2b, rt, tiles_per_b):
    n = x1.shape[0]

    def body(x_ref, t_ref, sh_ref, sc_ref, gt_ref, w1_hbm, b1_ref, w2_hbm, b2_ref, lg_ref, lb_ref,
             loss_ref, dx_ref, h2_ref, da1_ref, r2_ref, dmlp_ref, dm_ref, db1_ref, db2_ref, dlg_ref, dlb_ref,
             w1_vm, w2_vm, sem):
        i = pl.program_id(0)

        @pl.when(i == 0)
        def _():
            c1 = pltpu.make_async_copy(w1_hbm, w1_vm, sem.at[0])
            c2 = pltpu.make_async_copy(w2_hbm, w2_vm, sem.at[1])
            c1.start()
            c2.start()
            for r in (loss_ref, db1_ref, db2_ref, dlg_ref, dlb_ref):
                r[...] = jnp.zeros_like(r)
            c1.wait()
            c2.wait()

        @pl.when(i % tiles_per_b == 0)
        def _():
            dm_ref[...] = jnp.zeros_like(dm_ref)

        x1v = x_ref[...]
        h2, vjp_h = jax.vjp(_modln, x1v, sh_ref[...], sc_ref[...])
        h2b = h2.astype(BF16)
        h2_ref[...] = h2b
        r = jnp.maximum(jnp.dot(h2b, w1_vm[...], preferred_element_type=F32) + b1_ref[...], 0.0)
        r2b = (r * r).astype(BF16)
        r2_ref[...] = r2b
        mlp = jnp.dot(r2b, w2_vm[...], preferred_element_type=F32) + b2_ref[...]
        x2, vjp_r = jax.vjp(_resln, x1v, mlp, gt_ref[...], lg_ref[...], lb_ref[...])
        diff = x2 - t_ref[...]
        loss_ref[...] += (0.5 / D) * jnp.sum(diff * diff)
        dxa, dmlp, dgt, dlg, dlb = vjp_r(diff * (1.0 / D))
        dlg_ref[...] += dlg
        dlb_ref[...] += dlb
        dm_ref[2:3, :] += dgt
        db2_ref[...] += _colsum(dmlp)
        dmlpb = dmlp.astype(BF16)
        dmlp_ref[...] = dmlpb
        da1 = lax.dot_general(dmlpb, w2_vm[...], (((1,), (1,)), ((), ())), preferred_element_type=F32) * (2.0 * r)
        db1_ref[...] += _colsum(da1)
        da1b = da1.astype(BF16)
        da1_ref[...] = da1b
        dh2 = lax.dot_general(da1b, w1_vm[...], (((1,), (1,)), ((), ())), preferred_element_type=F32)
        dxb, dsh, dsc = vjp_h(dh2)
        dx_ref[...] = dxa + dxb
        dm_ref[0:1, :] += dsh
        dm_ref[1:2, :] += dsc

    row = lambda w: BS((rt, w), lambda i: (i, 0))
    const = lambda shp: BS(shp, lambda i: (0,) * len(shp))
    mod = lambda k: BS((None, None, 1, D), lambda i: (i // tiles_per_b, k, 0, 0))
    nb = n // (rt * tiles_per_b)
    anyspec = BS(memory_space=pl.ANY)
    return pl.pallas_call(
        body, grid=(n // rt,), name="mlp_step",
        in_specs=[row(D), row(D), mod(3), mod(4), mod(5), anyspec, const((1, MLP_H)), anyspec, const((1, D)),
                  const((1, D)), const((1, D))],
        out_specs=[const((8, 128)), row(D), row(D), row(MLP_H), row(MLP_H), row(D),
                   BS((None, 3, D), lambda i: (i // tiles_per_b, 0, 0)), const((1, MLP_H)), const((1, D)),
                   const((1, D)), const((1, D))],
        out_shape=[S((8, 128), F32), S((n, D), F32), S((n, D), BF16), S((n, MLP_H), BF16), S((n, MLP_H), BF16),
                   S((n, D), BF16), S((nb, 3, D), F32), S((1, MLP_H), F32), S((1, D), F32), S((1, D), F32),
                   S((1, D), F32)],
        scratch_shapes=[pltpu.VMEM((D, MLP_H), BF16), pltpu.VMEM((MLP_H, D), BF16), pltpu.SemaphoreType.DMA((2,))],
        compiler_params=_params(),
    )(x1, tgt, m4, m4, m4, w1, b1, w2, b2, l2g, l2b)


def _pack_win(w):
    parts = []
    for g in range(SSD_G):
        parts += [w[:, 256 * g:256 * g + 256], w[:, 2048 + 128 * g:2176 + 128 * g], w[:, 4160 + 128 * g:4288 + 128 * g]]
    parts += [w[:, 3136:4160], w[:, 5184:7232], w[:, 7232:8256], w[:, 8256:10304], w[:, 3072:3136],
              jnp.zeros((w.shape[0], P_W - P_DT - 64), w.dtype)]
    return jnp.concatenate(parts, axis=1)


def _unpack_win(p):
    xs = [p[:, 512 * g:512 * g + 256] for g in range(SSD_G)]
    bs = [p[:, 512 * g + 256:512 * g + 384] for g in range(SSD_G)]
    cs = [p[:, 512 * g + 384:512 * g + 512] for g in range(SSD_G)]
    return jnp.concatenate(xs + bs + [p[:, P_DT:P_DT + 64], p[:, P_LRU:P_Z]] + cs + [p[:, P_Z:P_DT]], axis=1)


def _pack_conv(w):
    return jnp.stack([jnp.concatenate([w[:, 256 * g:256 * g + 256], w[:, 2048 + 128 * g:2176 + 128 * g],
                                       w[:, 3072 + 128 * g:3200 + 128 * g]], axis=1) for g in range(SSD_G)])


def _unpack_conv(p):
    r = p.shape[1]
    x = jnp.transpose(p[:, :, 0:256], (1, 0, 2)).reshape(r, 2048)
    b = jnp.transpose(p[:, :, 256:384], (1, 0, 2)).reshape(r, 1024)
    c = jnp.transpose(p[:, :, 384:512], (1, 0, 2)).reshape(r, 1024)
    return jnp.concatenate([x, b, c], axis=1)


def _pack_heads(v):
    p = jnp.transpose(v.reshape(2, SSD_G, 4), (1, 0, 2)).reshape(SSD_G, 1, 8)
    return jnp.pad(p, ((0, 0), (0, 0), (0, 120)))


def _unpack_heads(p):
    return jnp.transpose(p[:, 0, 0:8].reshape(SSD_G, 2, 4), (1, 0, 2)).reshape(2, 32)


def _pack_dt(dt):
    n = dt.shape[0]
    p = jnp.transpose(dt.reshape(n, 2, SSD_G, 4), (2, 0, 1, 3)).reshape(SSD_G, n, 8)
    return jnp.pad(p, ((0, 0), (0, 0), (0, 120)))


def _unpack_dt(p):
    n = p.shape[1]
    return jnp.transpose(p[:, :, 0:8].reshape(SSD_G, n, 2, 4), (1, 2, 0, 3)).reshape(n, 64)


def _tk(rows):
    return 512 if rows % 512 == 0 else (256 if rows % 256 == 0 else 128)


LATE = ["w_br_ssd", "w_br_lru", "w_out", "w_mlp1", "w_mlp2"]


def _local_step(x, c, ctx, tgt, sm, wmod, win, late, late_are_shards=False):
    nb, t, _ = x.shape
    tc = ctx.shape[1]
    nl, ncx = nb * t, nb * tc
    rt = 256 if tc % 256 == 0 else 128
    rtm = 128
    xl, xc = x.reshape(nl, D), ctx.reshape(ncx, D)
    tgt2 = tgt.reshape(nl, D)
    cc = jnp.zeros((8, D), F32).at[0:nb].set(c).at[nb].set(sm["c_ctx"])
    m = _mod_fwd(cc, wmod, sm["b_mod"])
    m4 = m.reshape(8, N_MOD, 1, D)
    proj, h1 = _inproj_fwd(xl, m4, win, rtm, nl // rtm, t // rtm, nb, "inproj_fwd_lat")
    proj_c, h1_c = _inproj_fwd(xc, m4, win, rtm, 0, 1, nb, "inproj_fwd_ctx")

    cw_s, cb_s = _pack_conv(sm["ssd_conv_w"]), _pack_conv(sm["ssd_conv_b"])
    dtb, alog = _pack_heads(sm["ssd_dt_bias"]), _pack_heads(sm["ssd_a_log"])
    drow = jnp.repeat(sm["ssd_d"].reshape(32), 64).reshape(SSD_G, 1, 256)
    dtg, dtg_c = _pack_dt(proj[:, P_DT:P_DT + 64]), _pack_dt(proj_c[:, P_DT:P_DT + 64])
    zst = jnp.zeros((nb, SSD_G, SSD_N, 256), F32)
    zl = jnp.zeros((nb, 2, D), F32)
    ssd_p = (cw_s, cb_s, dtb, alog, drow)
    lru_p = (sm["lru_conv_w"], sm["lru_conv_b"], sm["lru_wa"], sm["lru_ba"], sm["lru_wi"], sm["lru_bi"], sm["lru_lambda"])

    chsf, chsb, csf, csb = _ssd_fwd(proj_c, dtg_c, *ssd_p, zst, zst, nb, tc, tc, 0, False)
    y, lhsf, lhsb, _, _, *got = _ssd_fwd(proj, dtg, *ssd_p, csf, csb, nb, t, GRID_W, 0, True,
                                         tuple(late) if late_are_shards else ())
    wbs, wbl, wo, w1, w2 = [_full_from_chips(g, n) for g, n in zip(got, LATE)] if late_are_shards else late
    chf, chb, cfin = _lru_fwd(proj_c, *lru_p, zl, nb, tc, tc, 0, False)
    ylru, lhf, lhb, _ = _lru_fwd(proj, *lru_p, cfin, nb, t, GRID_W, 0, True)
    mix_w = (wbs, wbl, wo, sm["ssd_norm_w"], sm["b_gate"], sm["ln1_g"], sm["ln1_b"])
    x1, nrm, gl, mixb, brs, brl, xmix = _mix_fwd(y, ylru, proj, xl, m4, *mix_w, rtm, t // rtm)
    (loss, dx1, h2, da1, r2, dmlp, dm2, db1, db2, dl2g, dl2b) = _mlp_step(
        x1, tgt2, m4, w1, sm["b_mlp1"], w2, sm["b_mlp2"], sm["ln2_g"], sm["ln2_b"], rt, t // rt)

    dproj = lax.empty((nl, P_W), BF16)
    dproj_c = jnp.zeros((ncx, P_W), BF16)
    (dproj, dy, dylru, dxres, dbrs, dbrl, dxm, dg1, dnw, dbg, dl1g, dl1b) = _mix_bwd(
        y, ylru, proj, xl, m4, *mix_w, brs, brl, xmix, dx1, dproj, rtm, t // rtm)
    big = {
        "w_br_ssd": _matmul_tn(nrm, dbrs, D, D, _tk(nl), "dw_br_ssd"),
        "w_br_lru": _matmul_tn(gl, dbrl, D, D, _tk(nl), "dw_br_lru"),
        "w_out": _matmul_tn(mixb, dxm, D, D, _tk(nl), "dw_out"),
        "w_mlp1": _matmul_tn(h2, da1, D, D, _tk(nl), "dw_mlp1"),
        "w_mlp2": _matmul_tn(r2, dmlp, D, D, _tk(nl), "dw_mlp2"),
    }
    (dproj, ddt_l, dh0f, dh0b, dcw_l, dcb_l, ddtb_l, dal_l, dd) = _ssd_bwd(
        proj, dtg, *ssd_p, lhsf, lhsb, dy, zst, zst, dproj, nb, t, GRID_W, 0, True)
    (dproj_c, ddt_c, _, _, dcw_c, dcb_c, ddtb_c, dal_c, _) = _ssd_bwd(
        proj_c, dtg_c, *ssd_p, chsf, chsb, None, dh0f, dh0b, dproj_c, nb, tc, tc, 0, False)
    (dproj, dlh0, gcw_l, gcb_l, gwa_l, gwi_l, gba_l, gbi_l, glam_l) = _lru_bwd(
        proj, *lru_p, cfin, lhf, lhb, dylru, zl, dproj, nb, t, GRID_W, 0, True)
    (dproj_c, _, gcw_c, gcb_c, gwa_c, gwi_c, gba_c, gbi_c, glam_c) = _lru_bwd(
        proj_c, *lru_p, zl, chf, chb, None, dlh0, dproj_c, nb, tc, tc, 0, False)
    pad_dt = lambda d: jnp.pad(_unpack_dt(d).astype(BF16), ((0, 0), (0, P_W - P_DT - 64)))
    dproj = lax.dynamic_update_slice(dproj, pad_dt(ddt_l), (0, P_DT))
    dproj_c = lax.dynamic_update_slice(dproj_c, pad_dt(ddt_c), (0, P_DT))

    gx, dm1 = _inproj_bwd(xl, m4, win, dproj, rt, 0, nl // rt, t // rt, nb, True, dxres)
    (dmc,) = _inproj_bwd(xc, m4, win, dproj_c, rt, 0, ncx // rt, ncx // rt, nb, False, None)
    dm = jnp.zeros((8, N_MOD, D), F32)
    dm = dm.at[0:nb].set(jnp.concatenate([dm1, dg1, dm2], axis=1)).at[nb, 0:2].set(dmc[0])
    dwmod, dbmod, dcc = _mod_bwd(cc, wmod, dm.reshape(8, N_MOD * D))

    big["w_mod"] = dwmod
    big["w_in"] = _matmul_tn2(h1, dproj, h1_c, dproj_c, D, 1152, min(_tk(nl), _tk(ncx)), "dw_in")
    nq = D // LRU_CB
    small = {
        "c_ctx": dcc[nb],
        "b_mod": dbmod,
        "b_gate": dbg,
        "ssd_conv_w": _unpack_conv(dcw_l + dcw_c),
        "ssd_conv_b": _unpack_conv(dcb_l + dcb_c),
        "ssd_dt_bias": _unpack_heads(ddtb_l + ddtb_c),
        "ssd_a_log": _unpack_heads(dal_l + dal_c),
        "ssd_d": jnp.sum(dd.reshape(32, 64), axis=1),
        "ssd_norm_w": dnw,
        "lru_conv_w": jnp.transpose(gcw_l + gcw_c, (1, 0, 2)).reshape(4, D),
        "lru_conv_b": (gcb_l + gcb_c).reshape(1, D),
        "lru_wa": gwa_l + gwa_c,
        "lru_ba": jnp.transpose(gba_l + gba_c, (1, 0, 2)).reshape(2, D),
        "lru_wi": gwi_l + gwi_c,
        "lru_bi": jnp.transpose(gbi_l + gbi_c, (1, 0, 2)).reshape(2, D),
        "lru_lambda": jnp.transpose(glam_l + glam_c, (1, 0, 2)).reshape(2, D),
        "ln1_g": dl1g, "ln1_b": dl1b, "b_mlp1": db1, "b_mlp2": db2, "ln2_g": dl2g, "ln2_b": dl2b,
    }
    return loss[0, 0], gx.reshape(nb, t, D), big, small


_HBM = BS(memory_space=pl.ANY)


def _place():
    return lax.axis_index("x"), lax.axis_index("y"), lax.axis_index("c")


def _other_chips(x, y):
    return [(1 - x, y), (x, 1 - y), (1 - x, 1 - y)]


def _gather_chips(arrs):
    n = len(arrs)

    def body(*refs):
        ex = _GatherExchange(refs[:n], refs[n:2 * n], refs[2 * n:])
        ex.begin()
        ex.finish()

    return pl.pallas_call(
        body, name="gather_weights", in_specs=[_HBM] * n, out_specs=[_HBM] * n,
        out_shape=_gather_out_shapes(arrs), scratch_shapes=_gather_sems(n),
    )(*arrs)


def _gather_out_shapes(arrs):
    return [S((4,) + a.shape, a.dtype) for a in arrs]


def _gather_sems(n):
    return [pltpu.SemaphoreType.DMA((3 * n,))] * 4 + [pltpu.SemaphoreType.DMA((n,))]


class _GatherExchange:
    def __init__(self, ins, outs, sems):
        self.ins, self.outs = ins, outs
        self.ici_send, self.ici_recv, self.d2d_send, self.d2d_recv, self.loc_sems = sems
        self.x, self.y, self.c = _place()
        self.me = 2 * self.x + self.y
        self.chips = _other_chips(self.x, self.y)

    def _half(self, a, which):
        hr = self.ins[a].shape[0] // 2
        return pl.ds(pl.multiple_of((self.c if which == 0 else 1 - self.c) * hr, 8), hr)

    def _local(self, a):
        return pltpu.make_async_copy(self.ins[a], self.outs[a].at[self.me], self.loc_sems.at[a])

    def _ici(self, a, k, slot):
        px, py = self.chips[k]
        mine = self._half(a, 0)
        return pltpu.make_async_remote_copy(src_ref=self.ins[a].at[mine], dst_ref=self.outs[a].at[slot, mine],
                                            send_sem=self.ici_send.at[3 * a + k], recv_sem=self.ici_recv.at[3 * a + k],
                                            device_id=(px, py, self.c), device_id_type=MESH)

    def _d2d(self, a, k, which):
        px, py = self.chips[k]
        rows = self.outs[a].at[2 * px + py, self._half(a, which)]
        return pltpu.make_async_remote_copy(src_ref=rows, dst_ref=rows, send_sem=self.d2d_send.at[3 * a + k],
                                            recv_sem=self.d2d_recv.at[3 * a + k],
                                            device_id=(self.x, self.y, 1 - self.c), device_id_type=MESH)

    def begin(self):
        for a in range(len(self.ins)):
            self._local(a).start()
            for k in range(3):
                self._ici(a, k, self.me).start()

    def finish(self):
        n = len(self.ins)
        for a in range(n):
            for k, (px, py) in enumerate(self.chips):
                self._ici(a, k, 2 * px + py).wait_recv()
                self._d2d(a, k, 0).start()
        for a in range(n):
            for k in range(3):
                self._d2d(a, k, 1).wait_recv()
        for a in range(n):
            self._local(a).wait()
            for k in range(3):
                self._ici(a, k, self.me).wait_send()
                self._d2d(a, k, 0).wait_send()


def _scatter_chips(arrs):
    n = len(arrs)

    def body(*refs):
        ins, outs = refs[:n], refs[n:2 * n]
        send_sems, recv_sems = refs[2 * n:]
        x, y, c = _place()
        chips = _other_chips(x, y)
        sends = []
        for a in range(n):
            for k, (px, py) in enumerate(chips):
                cp = pltpu.make_async_remote_copy(src_ref=ins[a].at[2 * px + py], dst_ref=outs[a].at[k],
                                                  send_sem=send_sems.at[a, k], recv_sem=recv_sems.at[a, k],
                                                  device_id=(px, py, c), device_id_type=MESH)
                cp.start()
                sends.append(cp)
        for cp in sends:
            cp.wait_recv()
        for cp in sends:
            cp.wait_send()

    return pl.pallas_call(
        body, name="scatter_grads", in_specs=[_HBM] * n, out_specs=[_HBM] * n,
        out_shape=[S((3,) + a.shape[1:], a.dtype) for a in arrs],
        scratch_shapes=[pltpu.SemaphoreType.DMA((n, 3)), pltpu.SemaphoreType.DMA((n, 3))],
    )(*arrs)


def _swap_halves(arrs):
    n = len(arrs)

    def body(*refs):
        ins, outs = refs[:n], refs[n:2 * n]
        send_sems, recv_sems = refs[2 * n:]
        x, y, c = _place()
        sends = []
        for a in range(n):
            hr = arrs[a].shape[1] // 2
            theirs = pl.ds(pl.multiple_of((1 - c) * hr, 8), hr)
            for q in range(4):
                cp = pltpu.make_async_remote_copy(src_ref=ins[a].at[q, theirs], dst_ref=outs[a].at[q],
                                                  send_sem=send_sems.at[4 * a + q], recv_sem=recv_sems.at[4 * a + q],
                                                  device_id=(x, y, 1 - c), device_id_type=MESH)
                cp.start()
                sends.append(cp)
        for cp in sends:
            cp.wait_recv()
        for cp in sends:
            cp.wait_send()

    return pl.pallas_call(
        body, name="swap_halves", in_specs=[_HBM] * n, out_specs=[_HBM] * n,
        out_shape=[S((4, a.shape[1] // 2, a.shape[2]), a.dtype) for a in arrs],
        scratch_shapes=[pltpu.SemaphoreType.DMA((4 * n,)), pltpu.SemaphoreType.DMA((4 * n,))],
    )(*arrs)


def _allreduce_small(v):
    def body(in_ref, out_ref, r0, r1, r2, send_sems, recv_sems):
        x, y, c = _place()
        src = in_ref
        for s, (buf, peer) in enumerate(((r0, (x, y, 1 - c)), (r1, (x, 1 - y, c)), (r2, (1 - x, y, c)))):
            cp = pltpu.make_async_remote_copy(src_ref=src, dst_ref=buf, send_sem=send_sems.at[s], recv_sem=recv_sems.at[s],
                                              device_id=peer, device_id_type=MESH)
            cp.start()
            cp.wait()
            out_ref[...] = src[...] + buf[...]
            src = out_ref

    vm = BS(memory_space=pltpu.VMEM)
    return pl.pallas_call(
        body, name="allreduce_small", in_specs=[vm], out_specs=vm, out_shape=S(v.shape, F32),
        scratch_shapes=[pltpu.VMEM(v.shape, F32)] * 3 + [pltpu.SemaphoreType.DMA((3,)), pltpu.SemaphoreType.DMA((3,))],
        compiler_params=_params(),
    )(v)


def _swap_cores(arrs):
    n = len(arrs)

    def body(*refs):
        ins, outs = refs[:n], refs[n:2 * n]
        send_sems, recv_sems = refs[2 * n:]
        x, y, c = _place()
        sends = []
        for a in range(n):
            cp = pltpu.make_async_remote_copy(src_ref=ins[a], dst_ref=outs[a], send_sem=send_sems.at[a],
                                              recv_sem=recv_sems.at[a], device_id=(x, y, 1 - c), device_id_type=MESH)
            cp.start()
            sends.append(cp)
        for cp in sends:
            cp.wait_recv()
        for cp in sends:
            cp.wait_send()

    return pl.pallas_call(
        body, name="swap_cores", in_specs=[_HBM] * n, out_specs=[_HBM] * n,
        out_shape=[S(a.shape, a.dtype) for a in arrs],
        scratch_shapes=[pltpu.SemaphoreType.DMA((n,)), pltpu.SemaphoreType.DMA((n,))],
    )(*arrs)


def _row_tile(r, c=128):
    tr = 256 if c <= 1024 else (128 if c <= 2048 else 64)
    return tr if r % tr == 0 else r


def _sum_half(own, sib, core, name):
    _, r, c = own.shape
    hr = r // 2
    tr = _row_tile(hr, c)
    nbk = hr // tr

    def body(core_ref, o_ref, s_ref, p_ref, pb_ref):
        p = o_ref[...] + s_ref[...]
        p_ref[...] = p
        pb_ref[...] = p.astype(BF16)

    blk = BS((None, tr, c), lambda q, i, cr: (q, i, 0))
    return pl.pallas_call(
        body, name=name, out_shape=[S((4, hr, c), F32), S((4, hr, c), BF16)],
        grid_spec=pltpu.PrefetchScalarGridSpec(
            num_scalar_prefetch=1, grid=(4, nbk),
            in_specs=[BS((None, tr, c), lambda q, i, cr: (q, cr[0] * nbk + i, 0)), blk], out_specs=[blk, blk]),
        compiler_params=_params(),
    )(core, own, sib)


def _sum4(part, recv, chip, name):
    _, r, c = part.shape
    tr = _row_tile(r, c)

    def body(chip_ref, o_ref, r_ref, out_ref):
        acc = o_ref[...]
        for k in range(3):
            acc = acc + r_ref[k].astype(F32)
        out_ref[...] = acc

    return pl.pallas_call(
        body, name=name, out_shape=S((r, c), F32),
        grid_spec=pltpu.PrefetchScalarGridSpec(
            num_scalar_prefetch=1, grid=(r // tr,),
            in_specs=[BS((None, tr, c), lambda i, ch: (ch[0], i, 0)), BS((3, tr, c), lambda i, ch: (0, i, 0))],
            out_specs=BS((tr, c), lambda i, ch: (i, 0))),
        compiler_params=_params(),
    )(chip, part, recv)


def _adam_math(w, g, m, v):
    m = ADAM_B1 * m + (1.0 - ADAM_B1) * g
    v = ADAM_B2 * v + (1.0 - ADAM_B2) * (g * g)
    m_hat = m / (1.0 - ADAM_B1 ** ADAM_STEP)
    v_hat = v / (1.0 - ADAM_B2 ** ADAM_STEP)
    return -ADAM_LR * (m_hat / (jnp.sqrt(v_hat) + ADAM_EPS) + ADAM_WD * w), m, v


def _adam_halves(mine, other, w, m, v, core, name):
    r, c = w.shape
    tr = _row_tile(r // 2, c)
    nbk = (r // 2) // tr

    def body(core_ref, a_ref, b_ref, w_ref, m_ref, v_ref, g_ref, d_ref, nm_ref, nv_ref):
        g = jnp.where(pl.program_id(0) // nbk == core_ref[0], a_ref[...], b_ref[...])
        g_ref[...] = g
        d_ref[...], nm_ref[...], nv_ref[...] = _adam_math(w_ref[...], g, m_ref[...], v_ref[...])

    spec = BS((tr, c), lambda i, cr: (i, 0))
    half = BS((tr, c), lambda i, cr: (i % nbk, 0))
    return pl.pallas_call(
        body, name=name, out_shape=[S((r, c), F32)] * 4,
        grid_spec=pltpu.PrefetchScalarGridSpec(num_scalar_prefetch=1, grid=(r // tr,), in_specs=[half, half] + [spec] * 3,
                                               out_specs=[spec] * 4),
        compiler_params=_params(),
    )(core, mine, other, w, m, v)


def _adam_flat(g, w, m, v):
    r = w.shape[0]
    tr = _row_tile(r)

    def body(g_ref, w_ref, m_ref, v_ref, d_ref, nm_ref, nv_ref):
        d_ref[...], nm_ref[...], nv_ref[...] = _adam_math(w_ref[...], g_ref[...], m_ref[...], v_ref[...])

    spec = BS((tr, 128), lambda i: (i, 0))
    return pl.pallas_call(
        body, grid=(r // tr,), name="adam_small", in_specs=[spec] * 4, out_specs=[spec] * 3,
        out_shape=[S((r, 128), F32)] * 3, compiler_params=_params(),
    )(g, w, m, v)


def _flatten(arrs, rows_mult=256):
    flat = jnp.concatenate([a.reshape(-1) for a in arrs])
    n = flat.shape[0]
    rows = -(-n // 128)
    rows = -(-rows // rows_mult) * rows_mult
    return jnp.pad(flat, (0, rows * 128 - n)).reshape(rows, 128)


def _unflatten(flat, shapes):
    flat = flat.reshape(-1)
    out, o = [], 0
    for shp in shapes:
        n = int(np.prod(shp))
        out.append(flat[o:o + n].reshape(shp))
        o += n
    return out


BIG = ["w_mod", "w_in", "w_br_ssd", "w_br_lru", "w_out", "w_mlp1", "w_mlp2"]
COL_SHARDED = {"w_mod": N_MOD * D, "w_in": IN_COLS, "w_mlp1": MLP_H}
SMALL_SHARDED = ["ssd_conv_w", "lru_conv_w", "lru_ba", "lru_bi", "lru_lambda"]
WEIGHTS = ['c_ctx', 'w_mod', 'b_mod', 'w_in', 'b_gate', 'ssd_conv_w', 'ssd_conv_b', 'ssd_dt_bias', 'ssd_a_log', 'ssd_d',
           'ssd_norm_w', 'lru_conv_w', 'lru_conv_b', 'lru_wa', 'lru_ba', 'lru_wi', 'lru_bi', 'lru_lambda', 'w_br_ssd',
           'w_br_lru', 'w_out', 'ln1_g', 'ln1_b', 'w_mlp1', 'b_mlp1', 'w_mlp2', 'b_mlp2', 'ln2_g', 'ln2_b']
SMALL = [n for n in WEIGHTS if n not in BIG]


def _full_from_chips(g4, name):
    if name in COL_SHARDED:
        return jnp.transpose(g4, (1, 0, 2)).reshape(g4.shape[1], 4 * g4.shape[2])
    return g4.reshape(4 * g4.shape[1], g4.shape[2])


def _chips_from_full(full, name):
    if name in COL_SHARDED:
        r, c = full.shape
        return jnp.transpose(full.reshape(r, 4, c // 4), (1, 0, 2))
    return full.reshape(4, full.shape[0] // 4, full.shape[1])


def kernel(x, c, ctx, c_ctx, w_mod, b_mod, w_in, b_gate, ssd_conv_w, ssd_conv_b, ssd_dt_bias, ssd_a_log, ssd_d, ssd_norm_w, lru_conv_w, lru_conv_b, lru_wa, lru_ba, lru_wi, lru_bi, lru_lambda, w_br_ssd, w_br_lru, w_out, ln1_g, ln1_b, w_mlp1, b_mlp1, w_mlp2, b_mlp2, ln2_g, ln2_b, loss_target, m_c_ctx, m_w_mod, m_b_mod, m_w_in, m_b_gate, m_ssd_conv_w, m_ssd_conv_b, m_ssd_dt_bias, m_ssd_a_log, m_ssd_d, m_ssd_norm_w, m_lru_conv_w, m_lru_conv_b, m_lru_wa, m_lru_ba, m_lru_wi, m_lru_bi, m_lru_lambda, m_w_br_ssd, m_w_br_lru, m_w_out, m_ln1_g, m_ln1_b, m_w_mlp1, m_b_mlp1, m_w_mlp2, m_b_mlp2, m_ln2_g, m_ln2_b, v_c_ctx, v_w_mod, v_b_mod, v_w_in, v_b_gate, v_ssd_conv_w, v_ssd_conv_b, v_ssd_dt_bias, v_ssd_a_log, v_ssd_d, v_ssd_norm_w, v_lru_conv_w, v_lru_conv_b, v_lru_wa, v_lru_ba, v_lru_wi, v_lru_bi, v_lru_lambda, v_w_br_ssd, v_w_br_lru, v_w_out, v_ln1_g, v_ln1_b, v_w_mlp1, v_b_mlp1, v_w_mlp2, v_b_mlp2, v_ln2_g, v_ln2_b):
    given = dict(locals())
    w = {n: given[n] for n in WEIGHTS}
    mom = {n: given["m_" + n] for n in WEIGHTS}
    var = {n: given["v_" + n] for n in WEIGHTS}
    chip = 2 * lax.axis_index("x") + lax.axis_index("y")

    shard2d = {n: w[n].reshape(w[n].shape[-2:]) for n in BIG}
    small_pack = _flatten([w[n] for n in SMALL_SHARDED], rows_mult=16)
    first = ["w_mod", "w_in"]
    gathered = _gather_chips([shard2d[n].astype(BF16) for n in first] + [small_pack])
    full = {n: _full_from_chips(g, n) for n, g in zip(first, gathered[:-1])}
    full["w_in"] = _pack_win(full["w_in"])
    per_chip = [_unflatten(gathered[-1][q], [w[n].shape for n in SMALL_SHARDED]) for q in range(4)]
    sm = {n: jnp.concatenate([per_chip[q][i] for q in range(4)], axis=-1) for i, n in enumerate(SMALL_SHARDED)}
    for n in SMALL:
        if n not in sm:
            sm[n] = w[n]
    sm = {n: (a.reshape(a.shape[1:]) if a.ndim >= 3 else a) for n, a in sm.items()}

    loss, gx, gbig, gsmall = _local_step(x, c, ctx, loss_target, sm, full["w_mod"], full["w_in"],
                                         [shard2d[n].astype(BF16) for n in LATE], late_are_shards=True)
    gbig["w_in"] = _unpack_win(gbig["w_in"])

    core_id = lax.axis_index("c").astype(jnp.int32).reshape(1)
    chip_id = chip.astype(jnp.int32).reshape(1)
    slabs = [_chips_from_full(gbig[n], n) for n in BIG]
    sib = _swap_halves(slabs)
    halves = [_sum_half(s, o, core_id, "half_" + n) for n, s, o in zip(BIG, slabs, sib)]
    recv = _scatter_chips([h[1] for h in halves])
    mine = [_sum4(h[0], r, chip_id, "sum_" + n) for n, h, r in zip(BIG, halves, recv)]
    other = _swap_cores(mine)
    out = {}
    for n, pa, pb in zip(BIG, mine, other):
        shp = shard2d[n].shape
        res = _adam_halves(pa, pb, shard2d[n], mom[n].reshape(shp), var[n].reshape(shp), core_id, "adam_" + n)
        out[n] = [r.reshape(w[n].shape) for r in res]

    full_shapes = [gsmall[n].shape for n in SMALL]
    gs_all = _unflatten(_allreduce_small(_flatten([gsmall[n] for n in SMALL])), full_shapes)
    gs = {}
    for n, g in zip(SMALL, gs_all):
        if n in SMALL_SHARDED:
            width = w[n].shape[-1]
            g = lax.dynamic_slice_in_dim(g, chip * width, width, axis=g.ndim - 1)
        gs[n] = g.reshape(w[n].shape)
    shapes = [w[n].shape for n in SMALL]
    d_s, m_s, v_s = _adam_flat(_flatten([gs[n] for n in SMALL]), _flatten([w[n] for n in SMALL]),
                               _flatten([mom[n] for n in SMALL]), _flatten([var[n] for n in SMALL]))
    for n, d_, m_, v_ in zip(SMALL, _unflatten(d_s, shapes), _unflatten(m_s, shapes), _unflatten(v_s, shapes)):
        out[n] = [gs[n], d_, m_, v_]

    loss = lax.psum(loss, ("x", "y", "c"))
    return (loss, gx, *[out[n][0] for n in WEIGHTS], *[out[n][1] for n in WEIGHTS], *[out[n][2] for n in WEIGHTS],
            *[out[n][3] for n in WEIGHTS])
```

```python
import functools

import numpy as np
import jax
import jax.numpy as jnp
from jax import lax
from jax.experimental import pallas as pl
from jax.experimental.pallas import tpu as pltpu

F32, BF16 = jnp.float32, jnp.bfloat16
S = jax.ShapeDtypeStruct
BS = pl.BlockSpec
MESH = pl.DeviceIdType.MESH

D = 1024
GRID_W = 64
SSD_INNER, SSD_G, SSD_N, SSD_L = 2048, 8, 128, 128
SSD_GW = 512
MLP_H = 4096
N_MOD = 6
ALPHA = 2.0 ** 0.25
LN_EPS, RMS_EPS = 1e-6, 1e-5
LRU_C = 8.0
P_XBC, P_LRU, P_Z, P_LG, P_MG, P_DT, P_W = 0, 4096, 5120, 7168, 8192, 10240, 10368
P_CB = 3456
IN_COLS = 10304
LRU_CB = 256
ADAM_LR, ADAM_B1, ADAM_B2, ADAM_EPS, ADAM_WD, ADAM_STEP = 0.001, 0.9, 0.999, 1e-08, 0.01, 10
VMEM_LIMIT = 56 * 2 ** 20


def _params(**kw):
    return pltpu.CompilerParams(vmem_limit_bytes=VMEM_LIMIT, **kw)


def _dot(a, b):
    return jnp.dot(a.astype(BF16), b.astype(BF16), preferred_element_type=F32)


def _dot_nt(a, b):
    return lax.dot_general(a.astype(BF16), b.astype(BF16), (((1,), (1,)), ((), ())), preferred_element_type=F32)


def _dot_tn(a, b):
    return lax.dot_general(a.astype(BF16), b.astype(BF16), (((0,), (0,)), ((), ())), preferred_element_type=F32)


@jax.custom_vjp
def _mm(a, b):
    return _dot(a, b)


def _cast_pair(a, b):
    return a.astype(BF16), b.astype(BF16)


def _mm_f(a, b):
    r = _cast_pair(a, b)
    return _dot(*r), r


def _mm_b(r, g):
    g = g.astype(BF16)
    return _dot_nt(g, r[1]), _dot_tn(r[0], g)


_mm.defvjp(_mm_f, _mm_b)


@jax.custom_vjp
def _mm_nt(a, b):
    return _dot_nt(a, b)


def _mm_nt_f(a, b):
    r = _cast_pair(a, b)
    return _dot_nt(*r), r


def _mm_nt_b(r, g):
    g = g.astype(BF16)
    return _dot(g, r[1]), _dot_tn(g, r[0])


_mm_nt.defvjp(_mm_nt_f, _mm_nt_b)


@jax.custom_vjp
def _mm_tn(a, b):
    return _dot_tn(a, b)


def _mm_tn_f(a, b):
    r = _cast_pair(a, b)
    return _dot_tn(*r), r


def _mm_tn_b(r, g):
    g = g.astype(BF16)
    return _dot_nt(r[1], g), _dot(r[0], g)


_mm_tn.defvjp(_mm_tn_f, _mm_tn_b)

def _split3(v):
    h = v.astype(BF16)
    r = v - h.astype(F32)
    m = r.astype(BF16)
    return h, m, (r - m.astype(F32)).astype(BF16)


def _sel_dot(sel, v, dims):
    sel_first = dims[0] == "s"
    dn = {"sv": (((1,), (0,)), ((), ())), "sTv": (((0,), (0,)), ((), ())), "vs": (((1,), (0,)), ((), ())),
          "vsT": (((1,), (1,)), ((), ()))}[dims]
    out = None
    for part in _split3(v):
        a, b = (sel, part) if sel_first else (part, sel)
        term = lax.dot_general(a, b, dn, preferred_element_type=F32)
        out = term if out is None else out + term
    return out


@jax.custom_vjp
def _cum_mm(tri, v):
    return _sel_dot(tri, v, "sv")


_cum_mm.defvjp(lambda tri, v: (_sel_dot(tri, v, "sv"), tri),
               lambda tri, g: (jnp.zeros_like(tri), _sel_dot(tri, g, "sTv")))


@jax.custom_vjp
def _xp_mm(v, e):
    return _sel_dot(e, v, "vs")


_xp_mm.defvjp(lambda v, e: (_sel_dot(e, v, "vs"), e),
              lambda e, g: (_sel_dot(e, g, "vsT"), jnp.zeros_like(e)))


def _sigmoid(x):
    return 1.0 / (1.0 + jnp.exp(-x))


def _silu(x):
    return x * _sigmoid(x)


def _dsilu(x):
    s = _sigmoid(x)
    return s * (1.0 + x * (1.0 - s))


def _softplus(x):
    return jnp.maximum(x, 0.0) + jnp.log1p(jnp.exp(-jnp.abs(x)))


def _gelu(x):
    return 0.5 * x * (1.0 + jnp.tanh(0.7978845608028654 * (x + 0.044715 * x * x * x)))


def _ln(x):
    mu = jnp.mean(x, axis=-1, keepdims=True)
    xc = x - mu
    var = jnp.mean(xc * xc, axis=-1, keepdims=True)
    return xc * lax.rsqrt(var + LN_EPS)


def _modln(x, shift, scale):
    return _ln(x) * (1.0 + scale) + shift


def _resln(x, sub, gate, g, b):
    return _ln(ALPHA * x + gate * sub) * g + b


def _grms(y, z, w):
    u = y * _silu(z)
    return u * lax.rsqrt(jnp.mean(u * u, axis=-1, keepdims=True) + RMS_EPS) * w


def _colsum(v):
    return jnp.sum(v, axis=0, keepdims=True)


def _conv_taps(width, period):
    def masks(rows):
        pos = lax.broadcasted_iota(jnp.int32, (rows, width), 0)
        if rows != period:
            pos = pos & (period - 1)
        return [pos >= 2 - k if k < 2 else pos < period + 2 - k for k in range(4)]
    return masks


def _conv_fwd(raw, w, b, masks):
    rows = raw.shape[0]
    pre = b + raw * w[2:3, :]
    for k in (0, 1, 3):
        sh = pltpu.roll(raw, (2 - k) % rows, 0)
        pre = pre + jnp.where(masks[k], sh, 0.0) * w[k:k + 1, :]
    return pre


def _for_rows(t, rb, fn):
    n = t // rb
    unroll = 4 if n % 4 == 0 else 1

    def step(i, carry):
        for u in range(unroll):
            fn(pl.multiple_of((i * unroll + u) * rb, rb))
        return carry

    lax.fori_loop(0, n // unroll, step, 0)


def _loop_unrolled(n, unroll, body, init):
    def step(i, carry):
        for u in range(unroll):
            carry = body(i * unroll + u, carry)
        return carry

    return lax.fori_loop(0, n // unroll, step, init)


def _conv_bwd(dpre, raw, w, masks):
    rows = raw.shape[0]
    draw = dpre * w[2:3, :]
    dws = []
    for k in range(4):
        if k == 2:
            dws.append(_colsum(dpre * raw))
            continue
        sh = pltpu.roll(raw, (2 - k) % rows, 0)
        dws.append(_colsum(dpre * jnp.where(masks[k], sh, 0.0)))
        back = pltpu.roll(jnp.where(masks[k], dpre, 0.0) * w[k:k + 1, :], (k - 2) % rows, 0)
        draw = draw + back
    return draw, jnp.concatenate(dws, axis=0), _colsum(dpre)


def _mod_fwd(cc, wmod, bmod):
    def body(cc_ref, w_ref, b_ref, o_ref):
        o_ref[...] = _dot(_silu(cc_ref[...]), w_ref[...]) + b_ref[...]

    return pl.pallas_call(
        body, grid=(N_MOD,), name="mod_fwd",
        in_specs=[BS((8, D), lambda j: (0, 0)), BS((D, D), lambda j: (0, j)), BS((1, D), lambda j: (0, j))],
        out_specs=BS((8, D), lambda j: (0, j)), out_shape=S((8, N_MOD * D), F32), compiler_params=_params(),
    )(cc, wmod, bmod)


def _mod_bwd(cc, wmod, dm):
    def body(cc_ref, w_ref, dm_ref, dw_ref, db_ref, dcc_ref):
        j = pl.program_id(0)
        c = cc_ref[...]
        dmv = dm_ref[...]
        dw_ref[...] = _dot_tn(_silu(c), dmv)
        db_ref[...] = _colsum(dmv)

        @pl.when(j == 0)
        def _():
            dcc_ref[...] = jnp.zeros_like(dcc_ref)

        dcc_ref[...] += _dot_nt(dmv, w_ref[...]) * _dsilu(c)

    return pl.pallas_call(
        body, grid=(N_MOD,), name="mod_bwd",
        in_specs=[BS((8, D), lambda j: (0, 0)), BS((D, D), lambda j: (0, j)), BS((8, D), lambda j: (0, j))],
        out_specs=[BS((D, D), lambda j: (0, j)), BS((1, D), lambda j: (0, j)), BS((8, D), lambda j: (0, 0))],
        out_shape=[S((D, N_MOD * D), F32), S((1, N_MOD * D), F32), S((8, D), F32)], compiler_params=_params(),
    )(cc, wmod, dm)


def _inproj_fwd(xa, m4, win, rt, n_lat_tiles, tiles_per_b, ctx_row, name):
    n_tiles = xa.shape[0] // rt

    def mrow(i):
        return jnp.where(i < n_lat_tiles, i // tiles_per_b, ctx_row)

    def body(x_ref, sh_ref, sc_ref, w_hbm, p_ref, h_ref, w_vm, sem):
        @pl.when(pl.program_id(0) == 0)
        def _():
            cp = pltpu.make_async_copy(w_hbm, w_vm, sem)
            cp.start()
            cp.wait()

        hb = _modln(x_ref[...], sh_ref[...], sc_ref[...]).astype(BF16)
        h_ref[...] = hb
        for j in range(P_W // P_CB):
            sl = slice(j * P_CB, (j + 1) * P_CB)
            p_ref[:, sl] = jnp.dot(hb, w_vm[:, sl], preferred_element_type=F32)

    return pl.pallas_call(
        body, grid=(n_tiles,), name=name,
        in_specs=[BS((rt, D), lambda i: (i, 0)),
                  BS((None, None, 1, D), lambda i: (mrow(i), 0, 0, 0)),
                  BS((None, None, 1, D), lambda i: (mrow(i), 1, 0, 0)),
                  BS(memory_space=pl.ANY)],
        out_specs=[BS((rt, P_W), lambda i: (i, 0)), BS((rt, D), lambda i: (i, 0))],
        out_shape=[S((xa.shape[0], P_W), F32), S((xa.shape[0], D), BF16)],
        scratch_shapes=[pltpu.VMEM((D, P_W), BF16), pltpu.SemaphoreType.DMA(())], compiler_params=_params(),
    )(xa, m4, m4, win)


def _inproj_bwd(xa, m4, win, dproj, rt, tile0, n_tiles, tiles_per_b, ctx_row, latent, dxres, side=None):
    def mrow(i):
        return (i // tiles_per_b) if latent else ctx_row

    def body(*refs):
        if side is None:
            return compute(*refs)
        own, exchange = side.split(refs, len(args), len(out_shape), 2)
        side.around(exchange, pl.program_id(0), n_tiles, lambda: compute(*own))

    def compute(x_ref, sh_ref, sc_ref, dp_ref, w_hbm, *rest):
        if latent:
            dxr_ref, gx_ref, dm_ref, w_vm, sem = rest
        else:
            dm_ref, w_vm, sem = rest
        i = pl.program_id(0)

        @pl.when(i == 0)
        def _():
            cp = pltpu.make_async_copy(w_hbm, w_vm, sem)
            cp.start()
            cp.wait()

        dh = lax.dot_general(dp_ref[...], w_vm[...], (((1,), (1,)), ((), ())), preferred_element_type=F32)
        _, vjp = jax.vjp(_modln, x_ref[...], sh_ref[...], sc_ref[...])
        dx, dsh, dsc = vjp(dh)
        if latent:
            gx_ref[...] = dx + dxr_ref[...]

        @pl.when(i % tiles_per_b == 0)
        def _():
            dm_ref[...] = jnp.zeros_like(dm_ref)

        dm_ref[0:1, :] += dsh
        dm_ref[1:2, :] += dsc

    nb = n_tiles // tiles_per_b
    in_specs = [BS((rt, D), lambda i: (tile0 + i, 0)),
                BS((None, None, 1, D), lambda i: (mrow(i), 0, 0, 0)),
                BS((None, None, 1, D), lambda i: (mrow(i), 1, 0, 0)),
                BS((rt, P_W), lambda i: (tile0 + i, 0)),
                BS(memory_space=pl.ANY)]
    args = [xa, m4, m4, dproj, win]
    dm_spec = BS((None, 2, D), lambda i: (i // tiles_per_b, 0, 0))
    if latent:
        in_specs.append(BS((rt, D), lambda i: (i, 0)))
        args.append(dxres)
        out_specs = [BS((rt, D), lambda i: (i, 0)), dm_spec]
        out_shape = [S((n_tiles * rt, D), F32), S((nb, 2, D), F32)]
    else:
        out_specs = [dm_spec]
        out_shape = [S((nb, 2, D), F32)]
    extra = side.arrays if side else []
    return pl.pallas_call(
        body, grid=(n_tiles,), name="inproj_bwd_lat" if latent else "inproj_bwd_ctx",
        in_specs=in_specs + [_HBM] * len(extra), out_specs=out_specs + [_HBM] * len(extra),
        out_shape=out_shape + (side.out_shapes if side else []),
        scratch_shapes=[pltpu.VMEM((D, P_W), BF16), pltpu.SemaphoreType.DMA(())] + (side.sems if side else []),
        compiler_params=_params(),
    )(*args, *extra)


def _matmul_tn(a, b, tm, tn, tk, name):
    k, m = a.shape
    n = b.shape[1]

    def body(a_ref, b_ref, o_ref):
        @pl.when(pl.program_id(2) == 0)
        def _():
            o_ref[...] = jnp.zeros_like(o_ref)

        o_ref[...] += lax.dot_general(a_ref[...], b_ref[...], (((0,), (0,)), ((), ())), preferred_element_type=F32)

    return pl.pallas_call(
        body, grid=(m // tm, n // tn, k // tk), name=name,
        in_specs=[BS((tk, tm), lambda i, j, kk: (kk, i)), BS((tk, tn), lambda i, j, kk: (kk, j))],
        out_specs=BS((tm, tn), lambda i, j, kk: (i, j)), out_shape=S((m, n), F32), compiler_params=_params(),
    )(a, b)


def _matmul_tn2(a1, b1, a2, b2, tm, tn, tk, name):
    k1, m = a1.shape
    n = b1.shape[1]
    n1, n2 = k1 // tk, a2.shape[0] // tk

    def body(a1_ref, b1_ref, a2_ref, b2_ref, o_ref):
        kk = pl.program_id(2)

        @pl.when(kk == 0)
        def _():
            o_ref[...] = jnp.zeros_like(o_ref)

        @pl.when(kk < n1)
        def _():
            o_ref[...] += lax.dot_general(a1_ref[...], b1_ref[...], (((0,), (0,)), ((), ())), preferred_element_type=F32)

        @pl.when(kk >= n1)
        def _():
            o_ref[...] += lax.dot_general(a2_ref[...], b2_ref[...], (((0,), (0,)), ((), ())), preferred_element_type=F32)

    first = lambda kk: jnp.minimum(kk, n1 - 1)
    second = lambda kk: jnp.maximum(kk - n1, 0)
    return pl.pallas_call(
        body, grid=(m // tm, n // tn, n1 + n2), name=name,
        in_specs=[BS((tk, tm), lambda i, j, kk: (first(kk), i)), BS((tk, tn), lambda i, j, kk: (first(kk), j)),
                  BS((tk, tm), lambda i, j, kk: (second(kk), i)), BS((tk, tn), lambda i, j, kk: (second(kk), j))],
        out_specs=BS((tm, tn), lambda i, j, kk: (i, j)), out_shape=S((m, n), F32), compiler_params=_params(),
    )(a1, b1, a2, b2)


def _ssd_consts(heads_per_tile):
    n = SSD_L
    ii = lax.broadcasted_iota(jnp.int32, (n, n), 0)
    jj = lax.broadcasted_iota(jnp.int32, (n, n), 1)
    er = lax.broadcasted_iota(jnp.int32, (128, 256), 0)
    ec = lax.broadcasted_iota(jnp.int32, (128, 256), 1) >> 6
    lane = lax.broadcasted_iota(jnp.int32, (1, 128 * heads_per_tile), 1) >> 6
    per_dir = []
    for d in (0, 1):
        mask = (jj >= ii) if d else (jj <= ii)
        per_dir.append((mask, mask.astype(BF16), (er == ec + 4 * d).astype(BF16)))
    return per_dir, [(lane == h).astype(F32) for h in range(2 * heads_per_tile)]


def _ssd_chunk(x, bm, cm, dtc, dtx, alog, hst, consts, hmasks, rev):
    n = SSD_L
    mask, tri, e = consts
    cum = _cum_mm(tri, dtc * (-jnp.exp(alog)))
    cum_x = _xp_mm(cum, e)
    tot_x = cum_x[0:1, :] if rev else cum_x[n - 1:n, :]
    xd = x * dtx
    hn = jnp.exp(tot_x) * hst + _mm_tn(bm, xd * jnp.exp(tot_x - cum_x))
    if cm is None:
        return hn
    cum_t = cum.T
    cb = _mm_nt(cm, bm)
    if len(hmasks) == 4:
        y = jnp.exp(cum_x) * _mm(cm, hst)
        for h in range(4):
            k = 4 * rev + h
            decay = jnp.exp(jnp.where(mask, cum[:, k:k + 1] - cum_t[k:k + 1, :], -1e30))
            y = y + _mm(cb * decay, xd * hmasks[h])
        return y, hn
    pairs = []
    for p in range(2):
        xdp = xd[:, 128 * p:128 * p + 128]
        yp = None
        for hh in range(2):
            k = 4 * rev + 2 * p + hh
            decay = jnp.exp(jnp.where(mask, cum[:, k:k + 1] - cum_t[k:k + 1, :], -1e30))
            term = _mm(cb * decay, xdp * hmasks[hh])
            yp = term if yp is None else yp + term
        pairs.append(yp)
    return jnp.exp(cum_x) * _mm(cm, hst) + jnp.concatenate(pairs, axis=1), hn


def _ssd_fwd(proj, dtg, cw, cb, dtb, alog, drow, h0f, h0b, nb, t, period, blk0, need_y, gather=()):
    nc = t // SSD_L
    rb = period
    assert t % rb == 0
    unroll = 4 if nc % 4 == 0 else (2 if nc % 2 == 0 else 1)
    masks_of = _conv_taps(SSD_GW, period)
    ng = len(gather)
    n_out = 5 if need_y else 4

    def body(p_ref, dt_ref, cw_ref, cb_ref, dtb_ref, al_ref, d_ref, h0f_ref, h0b_ref, *rest):
        g_ins, rest = rest[:ng], rest[ng:]
        outs, g_outs, (act, dts, dtxs), g_sems = rest[:n_out], rest[n_out:n_out + ng], rest[n_out + ng:n_out + ng + 3], \
            rest[n_out + ng + 3:]
        if need_y:
            y_ref, hsf_ref, hsb_ref, sf_ref, sb_ref = outs
        else:
            hsf_ref, hsb_ref, sf_ref, sb_ref = outs
        if ng:
            exchange = _GatherExchange(g_ins, g_outs, g_sems)
            step = pl.program_id(0) * SSD_G + pl.program_id(1)
            pl.when(step == 0)(exchange.begin)
        masks = masks_of(rb)
        per_dir, hmasks = _ssd_consts(2)

        def prologue(r0):
            rows = pl.ds(r0, rb)
            a = _silu(_conv_fwd(p_ref[rows, :], cw_ref[...], cb_ref[...], masks))
            act[rows, :] = a
            if need_y:
                y_ref[rows, :] = d_ref[...] * a[:, 0:256]
            dtv = _softplus(dt_ref[rows, :] + dtb_ref[...])
            dts[rows, :] = dtv
            for d in (0, 1):
                dtxs[rows, 256 * d:256 * d + 256] = _xp_mm(dtv, per_dir[d][2])

        _for_rows(t, rb, prologue)
        al = al_ref[...]

        def chunk(ci, carry):
            out = []
            for d, hst, hs_ref in ((0, carry[0], hsf_ref), (1, carry[1], hsb_ref)):
                c = (nc - 1 - ci) if d else ci
                r0 = pl.multiple_of(c * SSD_L, SSD_L)
                a = act[pl.ds(r0, SSD_L), :]
                hs_ref[c] = hst
                res = _ssd_chunk(a[:, 0:256], a[:, 256:384], a[:, 384:512] if need_y else None,
                                 dts[pl.ds(r0, SSD_L), :], dtxs[pl.ds(r0, SSD_L), 256 * d:256 * d + 256], al, hst,
                                 per_dir[d], hmasks, d)
                if need_y:
                    y_ref[pl.ds(r0, SSD_L), :] += res[0]
                    res = res[1]
                out.append(res)
            return tuple(out)

        sf_ref[...], sb_ref[...] = _loop_unrolled(nc, unroll, chunk, (h0f_ref[...], h0b_ref[...]))
        if ng:
            pl.when(step == nb * SSD_G - 1)(exchange.finish)

    gspec = lambda shp: BS((None,) + shp, lambda b, g: (g,) + (0,) * len(shp))
    st_spec = BS((None, None, SSD_N, 256), lambda b, g: (b, g, 0, 0))
    hs_spec = BS((None, None, nc, SSD_N, 256), lambda b, g: (b, g, 0, 0, 0))
    in_specs = [BS((t, SSD_GW), lambda b, g: (blk0 + b, g)), BS((None, t, 128), lambda b, g: (g, blk0 + b, 0)),
                gspec((4, SSD_GW)), gspec((1, SSD_GW)), gspec((1, 128)), gspec((1, 128)), gspec((1, 256)),
                st_spec, st_spec]
    out_specs = [hs_spec, hs_spec, st_spec, st_spec]
    out_shape = [S((nb, SSD_G, nc, SSD_N, 256), F32)] * 2 + [S((nb, SSD_G, SSD_N, 256), F32)] * 2
    if need_y:
        out_specs = [BS((t, 256), lambda b, g: (b, g))] + out_specs
        out_shape = [S((nb * t, SSD_INNER), F32)] + out_shape
    return pl.pallas_call(
        body, grid=(nb, SSD_G), name="ssd_fwd_lat" if need_y else "ssd_fwd_ctx",
        in_specs=in_specs + [_HBM] * ng, out_specs=out_specs + [_HBM] * ng,
        out_shape=out_shape + _gather_out_shapes(gather),
        scratch_shapes=[pltpu.VMEM((t, SSD_GW), F32), pltpu.VMEM((t, 128), F32), pltpu.VMEM((t, SSD_GW), F32)]
        + (_gather_sems(ng) if ng else []),
        compiler_params=_params(),
    )(proj, dtg, cw, cb, dtb, alog, drow, h0f, h0b, *gather)


def _ssd_bwd(proj, dtg, cw, cb, dtb, alog, drow, hsf, hsb, dy, dsf, dsb, dproj, nb, t, period, blk0, need_y):
    nc = t // SSD_L
    rb = period
    assert t % rb == 0
    unroll = 2 if nc % 2 == 0 else 1
    masks_of = _conv_taps(SSD_GW, period)

    def body(*refs):
        if need_y:
            (p_ref, dt_ref, cw_ref, cb_ref, dtb_ref, al_ref, d_ref, hsf_ref, hsb_ref, dy_ref, dsf_ref, dsb_ref, _,
             dp_ref, ddt_ref, dhf_ref, dhb_ref, dcw_ref, dcb_ref, ddtb_ref, dal_ref, dd_ref,
             pre, dact, dts, ddts, dtxs) = refs
        else:
            (p_ref, dt_ref, cw_ref, cb_ref, dtb_ref, al_ref, d_ref, hsf_ref, hsb_ref, dsf_ref, dsb_ref, _,
             dp_ref, ddt_ref, dhf_ref, dhb_ref, dcw_ref, dcb_ref, ddtb_ref, dal_ref, dd_ref,
             pre, dact, dts, ddts, dtxs) = refs
            dy_ref = None
        b, g = pl.program_id(0), pl.program_id(1)

        @pl.when(jnp.logical_and(b == 0, g == 0))
        def _():
            for r in (dcw_ref, dcb_ref, ddtb_ref, dal_ref, dd_ref):
                r[...] = jnp.zeros_like(r)

        masks = masks_of(rb)
        per_dir, hmasks = _ssd_consts(1)

        def prologue(r0):
            rows = pl.ds(r0, rb)
            pre[rows, :] = _conv_fwd(p_ref[rows, :], cw_ref[...], cb_ref[...], masks)
            dtv = _softplus(dt_ref[rows, :] + dtb_ref[...])
            dts[rows, :] = dtv
            for d in (0, 1):
                dtxs[rows, 256 * d:256 * d + 256] = _xp_mm(dtv, per_dir[d][2])
            dact[rows, :] = jnp.zeros((rb, SSD_GW), F32)
            ddts[rows, :] = jnp.zeros((rb, 128), F32)

        _for_rows(t, rb, prologue)
        al = al_ref[...]
        def chunk(ci, carry):
            dal_c = carry[2]
            dhs_out = []
            for d, dh, hs_ref in ((0, carry[0], hsf_ref), (1, carry[1], hsb_ref)):
                c = ci if d else (nc - 1 - ci)
                r0 = pl.multiple_of(c * SSD_L, SSD_L)
                a = _silu(pre[pl.ds(r0, SSD_L), :])
                dtc = dts[pl.ds(r0, SSD_L), :]
                dtx = dtxs[pl.ds(r0, SSD_L), 256 * d:256 * d + 256]
                if need_y:
                    fn = lambda x_, bm_, cm_, dt_, dx_, al_, hs_: _ssd_chunk(x_, bm_, cm_, dt_, dx_, al_, hs_, per_dir[d],
                                                                             hmasks, d)
                    _, vjp = jax.vjp(fn, a[:, 0:256], a[:, 256:384], a[:, 384:512], dtc, dtx, al, hs_ref[c])
                    dx, dbm, dcm, ddtc, ddtx, dal_k, dhs = vjp((dy_ref[pl.ds(r0, SSD_L), :], dh))
                    dact[pl.ds(r0, SSD_L), 384:512] += dcm
                else:
                    fn = lambda x_, bm_, dt_, dx_, al_, hs_: _ssd_chunk(x_, bm_, None, dt_, dx_, al_, hs_, per_dir[d],
                                                                        hmasks, d)
                    _, vjp = jax.vjp(fn, a[:, 0:256], a[:, 256:384], dtc, dtx, al, hs_ref[c])
                    dx, dbm, ddtc, ddtx, dal_k, dhs = vjp(dh)
                dact[pl.ds(r0, SSD_L), 0:256] += dx
                dact[pl.ds(r0, SSD_L), 256:384] += dbm
                ddts[pl.ds(r0, SSD_L), :] += ddtc + _dot_nt(ddtx, per_dir[d][2])
                dhs_out.append(dhs)
                dal_c = dal_c + dal_k
            return dhs_out[0], dhs_out[1], dal_c

        dhf_ref[...], dhb_ref[...], dal_acc = _loop_unrolled(
            nc, unroll, chunk, (dsf_ref[...], dsb_ref[...], jnp.zeros((1, 128), F32)))

        def epilogue(r0):
            rows = pl.ds(r0, rb)
            prev = pre[rows, :]
            if need_y:
                dyv = dy_ref[rows, :]
                dact[rows, 0:256] += d_ref[...] * dyv
                dd_ref[g] += _colsum(dyv * _silu(prev[:, 0:256]))
            dpre = dact[rows, :] * _dsilu(prev)
            draw, dcw, dcb = _conv_bwd(dpre, p_ref[rows, :], cw_ref[...], masks)
            dp_ref[rows, :] = draw.astype(BF16)
            dcw_ref[g] += dcw
            dcb_ref[g] += dcb
            ddraw = ddts[rows, :] * _sigmoid(dt_ref[rows, :] + dtb_ref[...])
            ddt_ref[rows, :] = ddraw
            ddtb_ref[g] += _colsum(ddraw)

        _for_rows(t, rb, epilogue)
        dal_ref[g] += dal_acc

    gspec = lambda shp: BS((None,) + shp, lambda b, g: (g,) + (0,) * len(shp))
    full = lambda shp: BS(shp, lambda b, g: (0,) * len(shp))
    st_spec = BS((None, None, SSD_N, 256), lambda b, g: (b, g, 0, 0))
    hs_spec = BS((None, None, nc, SSD_N, 256), lambda b, g: (b, g, 0, 0, 0))
    p_spec = BS((t, SSD_GW), lambda b, g: (blk0 + b, g))
    in_specs = [p_spec, BS((None, t, 128), lambda b, g: (g, blk0 + b, 0)),
                gspec((4, SSD_GW)), gspec((1, SSD_GW)), gspec((1, 128)), gspec((1, 128)), gspec((1, 256)),
                hs_spec, hs_spec]
    args = [proj, dtg, cw, cb, dtb, alog, drow, hsf, hsb]
    if need_y:
        in_specs.append(BS((t, 256), lambda b, g: (b, g)))
        args.append(dy)
    in_specs += [st_spec, st_spec, BS(memory_space=pl.ANY)]
    args += [dsf, dsb, dproj]
    out_specs = [p_spec, BS((None, t, 128), lambda b, g: (g, b, 0)), st_spec, st_spec,
                 full((SSD_G, 4, SSD_GW)), full((SSD_G, 1, SSD_GW)), full((SSD_G, 1, 128)), full((SSD_G, 1, 128)),
                 full((SSD_G, 1, 256))]
    out_shape = [S(dproj.shape, BF16), S((SSD_G, nb * t, 128), F32),
                 S((nb, SSD_G, SSD_N, 256), F32), S((nb, SSD_G, SSD_N, 256), F32),
                 S((SSD_G, 4, SSD_GW), F32), S((SSD_G, 1, SSD_GW), F32), S((SSD_G, 1, 128), F32),
                 S((SSD_G, 1, 128), F32), S((SSD_G, 1, 256), F32)]
    return pl.pallas_call(
        body, grid=(nb, SSD_G), name="ssd_bwd_lat" if need_y else "ssd_bwd_ctx",
        in_specs=in_specs, out_specs=out_specs, out_shape=out_shape,
        input_output_aliases={len(args) - 1: 0},
        scratch_shapes=[pltpu.VMEM((t, SSD_GW), F32), pltpu.VMEM((t, SSD_GW), F32), pltpu.VMEM((t, 128), F32),
                        pltpu.VMEM((t, 128), F32), pltpu.VMEM((t, SSD_GW), F32)],
        compiler_params=_params(),
    )(*args)


def _lru_gate(u, wa, ba, wi, bi, lam):
    r = _sigmoid(_mm(u, wa) + ba)
    i = _sigmoid(_mm(u, wi) + bi)
    log_a = -LRU_C * r * _softplus(-lam)
    a = jnp.exp(log_a)
    x2 = 2.0 * log_a
    em1 = jnp.where(x2 > -0.01, x2 * (1.0 + x2 * (0.5 + x2 * (1.0 / 6.0 + x2 * (1.0 / 24.0)))), a * a - 1.0)
    return a, jnp.sqrt(-em1) * (i * u)


def _scan_pair(fwd, rev, nblk, width):
    row = lax.broadcasted_iota(jnp.int32, (8, width), 0)

    def block(a_ref, b_ref, h_ref, st, carry, reverse):
        av, bv = a_ref[pl.ds(st, 8), :], b_ref[pl.ds(st, 8), :]
        for s in (1, 2, 4):
            ok = (row < 8 - s) if reverse else (row >= s)
            sh = (8 - s) if reverse else s
            a_sh = jnp.where(ok, pltpu.roll(av, sh, 0), 1.0)
            b_sh = jnp.where(ok, pltpu.roll(bv, sh, 0), 0.0)
            bv = av * b_sh + bv
            av = av * a_sh
        h = bv + av * carry
        h_ref[pl.ds(st, 8), :] = h
        return h[0:1, :] if reverse else h[7:8, :]

    def step(i, carry):
        cf, cr = carry
        cf = block(fwd[0], fwd[1], fwd[2], pl.multiple_of(i * 8, 8), cf, False)
        cr = block(rev[0], rev[1], rev[2], pl.multiple_of((nblk - 1 - i) * 8, 8), cr, True)
        return cf, cr

    return lax.fori_loop(0, nblk, step, (fwd[3], rev[3]))


def _lru_specs(t, blk0):
    p_spec = BS((t, LRU_CB), lambda b, q: (blk0 + b, P_LRU // LRU_CB + q))
    w_spec = BS((2, 2, 128, 128), lambda b, q: (0, q, 0, 0))
    v_spec = BS((2, LRU_CB), lambda b, q: (0, q))
    c_spec = lambda r: BS((r, LRU_CB), lambda b, q: (0, q))
    s_spec = BS((None, 2, LRU_CB), lambda b, q: (b, 0, q))
    return p_spec, w_spec, v_spec, c_spec, s_spec


def _lru_fwd(proj, cw, cb, wa, ba, wi, bi, lam, h0, nb, t, period, blk0, need_y):
    nq = D // LRU_CB
    masks_of = _conv_taps(LRU_CB, period)

    def body(p_ref, cw_ref, cb_ref, wa_ref, ba_ref, wi_ref, bi_ref, lam_ref, h0_ref, *rest):
        if need_y:
            y_ref, hf_ref, hb_ref, fin_ref, sa0, sb0, sa1, sb1 = rest
        else:
            hf_ref, hb_ref, fin_ref, sa0, sb0, sa1, sb1 = rest
        u = _conv_fwd(p_ref[...], cw_ref[...], cb_ref[...], masks_of(t))
        for d, (sa, sb) in enumerate(((sa0, sb0), (sa1, sb1))):
            for j in range(2):
                sl = slice(128 * j, 128 * j + 128)
                a, bb = _lru_gate(u[:, sl], wa_ref[d, j], ba_ref[d:d + 1, sl], wi_ref[d, j], bi_ref[d:d + 1, sl],
                                  lam_ref[d:d + 1, sl])
                sa[:, sl] = a
                sb[:, sl] = bb
        lf, lb = _scan_pair((sa0, sb0, hf_ref, h0_ref[0:1, :]), (sa1, sb1, hb_ref, h0_ref[1:2, :]), t // 8, LRU_CB)
        fin_ref[0:1, :] = lf
        fin_ref[1:2, :] = lb
        if need_y:
            y_ref[...] = hf_ref[...] + hb_ref[...]

    p_spec, w_spec, v_spec, c_spec, s_spec = _lru_specs(t, blk0)
    o_spec = BS((t, LRU_CB), lambda b, q: (b, q))
    out_specs = [o_spec, o_spec, s_spec]
    out_shape = [S((nb * t, D), F32), S((nb * t, D), F32), S((nb, 2, D), F32)]
    if need_y:
        out_specs = [o_spec] + out_specs
        out_shape = [S((nb * t, D), F32)] + out_shape
    return pl.pallas_call(
        body, grid=(nb, nq), name="lru_fwd_lat" if need_y else "lru_fwd_ctx",
        in_specs=[p_spec, c_spec(4), c_spec(1), w_spec, v_spec, w_spec, v_spec, v_spec, s_spec],
        out_specs=out_specs, out_shape=out_shape,
        scratch_shapes=[pltpu.VMEM((t, LRU_CB), F32)] * 4, compiler_params=_params(),
    )(proj, cw, cb, wa, ba, wi, bi, lam, h0)


def _lru_bwd(proj, cw, cb, wa, ba, wi, bi, lam, h0, hf, hb, dy, dfin, dproj, nb, t, period, blk0, need_y, side=None):
    nq = D // LRU_CB
    rc = min(256, t)
    masks_of = _conv_taps(LRU_CB, period)

    def body(*refs):
        if side is None:
            return compute(*refs)
        own, exchange = side.split(refs, len(args), len(out_shape), 7)
        side.around(exchange, pl.program_id(0) * nq + pl.program_id(1), nb * nq, lambda: compute(*own))

    def compute(*refs):
        if need_y:
            (p_ref, cw_ref, cb_ref, wa_ref, ba_ref, wi_ref, bi_ref, lam_ref, h0_ref, hf_ref, hb_ref, dy_ref, dfin_ref, _,
             dp_ref, dh0_ref, dcw_ref, dcb_ref, dwa_ref, dwi_ref, dba_ref, dbi_ref, dlam_ref,
             su, sa0, sa1, sc0, sc1, sg0, sg1) = refs
        else:
            (p_ref, cw_ref, cb_ref, wa_ref, ba_ref, wi_ref, bi_ref, lam_ref, h0_ref, hf_ref, hb_ref, dfin_ref, _,
             dp_ref, dh0_ref, dcw_ref, dcb_ref, dwa_ref, dwi_ref, dba_ref, dbi_ref, dlam_ref,
             su, sa0, sa1, sc0, sc1, sg0, sg1) = refs
            dy_ref = None
        b, q = pl.program_id(0), pl.program_id(1)

        @pl.when(jnp.logical_and(b == 0, q == 0))
        def _():
            for r in (dcw_ref, dcb_ref, dwa_ref, dwi_ref, dba_ref, dbi_ref, dlam_ref):
                r[...] = jnp.zeros_like(r)

        masks = masks_of(t)
        u = _conv_fwd(p_ref[...], cw_ref[...], cb_ref[...], masks)
        su[...] = u
        for d, sa in enumerate((sa0, sa1)):
            for j in range(2):
                sl = slice(128 * j, 128 * j + 128)
                a, _unused = _lru_gate(u[:, sl], wa_ref[d, j], ba_ref[d:d + 1, sl], wi_ref[d, j], bi_ref[d:d + 1, sl],
                                       lam_ref[d:d + 1, sl])
                sa[:, sl] = a
        rowi = lax.broadcasted_iota(jnp.int32, (t, LRU_CB), 0)
        last, first = rowi == t - 1, rowi == 0
        sc0[...] = jnp.where(last, 0.0, pltpu.roll(sa0[...], t - 1, 0))
        sc1[...] = jnp.where(first, 0.0, pltpu.roll(sa1[...], 1, 0))
        g0 = jnp.where(last, dfin_ref[0:1, :], 0.0)
        g1 = jnp.where(first, dfin_ref[1:2, :], 0.0)
        if need_y:
            g0 = g0 + dy_ref[...]
            g1 = g1 + dy_ref[...]
        sg0[...] = g0
        sg1[...] = g1
        zero = jnp.zeros((1, LRU_CB), F32)
        _scan_pair((sc1, sg1, sg1, zero), (sc0, sg0, sg0, zero), t // 8, LRU_CB)
        dh0_ref[0:1, :] = sa0[0:1, :] * sg0[0:1, :]
        dh0_ref[1:2, :] = sa1[t - 1:t, :] * sg1[t - 1:t, :]
        sc0[...] = sg0[...] * jnp.where(first, h0_ref[0:1, :], pltpu.roll(hf_ref[...], 1, 0))
        sc1[...] = sg1[...] * jnp.where(last, h0_ref[1:2, :], pltpu.roll(hb_ref[...], t - 1, 0))

        def rows(ci, carry):
            r0 = pl.multiple_of(ci * rc, rc)
            for j in range(2):
                sl = slice(128 * j, 128 * j + 128)
                du = jnp.zeros((rc, 128), F32)
                for d, (sc, sg) in enumerate(((sc0, sg0), (sc1, sg1))):
                    _, vjp = jax.vjp(_lru_gate, su[pl.ds(r0, rc), sl], wa_ref[d, j], ba_ref[d:d + 1, sl], wi_ref[d, j],
                                     bi_ref[d:d + 1, sl], lam_ref[d:d + 1, sl])
                    du_d, dwa, dba, dwi, dbi, dlam = vjp((sc[pl.ds(r0, rc), sl], sg[pl.ds(r0, rc), sl]))
                    du = du + du_d
                    dwa_ref[d, 2 * q + j] += dwa
                    dwi_ref[d, 2 * q + j] += dwi
                    dba_ref[q, d:d + 1, sl] += dba
                    dbi_ref[q, d:d + 1, sl] += dbi
                    dlam_ref[q, d:d + 1, sl] += dlam
                sa0[pl.ds(r0, rc), sl] = du
            return carry

        lax.fori_loop(0, t // rc, rows, 0)
        draw, dcw, dcb = _conv_bwd(sa0[...], p_ref[...], cw_ref[...], masks)
        dp_ref[...] = draw.astype(BF16)
        dcw_ref[q] += dcw
        dcb_ref[q] += dcb

    p_spec, w_spec, v_spec, c_spec, s_spec = _lru_specs(t, blk0)
    o_spec = BS((t, LRU_CB), lambda b, q: (b, q))
    full = lambda shp: BS(shp, lambda b, q: (0,) * len(shp))
    in_specs = [p_spec, c_spec(4), c_spec(1), w_spec, v_spec, w_spec, v_spec, v_spec, s_spec, o_spec, o_spec]
    args = [proj, cw, cb, wa, ba, wi, bi, lam, h0, hf, hb]
    if need_y:
        in_specs.append(o_spec)
        args.append(dy)
    in_specs += [s_spec, BS(memory_space=pl.ANY)]
    args += [dfin, dproj]
    out_specs = [p_spec, s_spec, full((nq, 4, LRU_CB)), full((nq, 1, LRU_CB)), full((2, 8, 128, 128)),
                 full((2, 8, 128, 128)), full((nq, 2, LRU_CB)), full((nq, 2, LRU_CB)), full((nq, 2, LRU_CB))]
    out_shape = [S(dproj.shape, BF16), S((nb, 2, D), F32), S((nq, 4, LRU_CB), F32), S((nq, 1, LRU_CB), F32),
                 S((2, 8, 128, 128), F32), S((2, 8, 128, 128), F32), S((nq, 2, LRU_CB), F32), S((nq, 2, LRU_CB), F32),
                 S((nq, 2, LRU_CB), F32)]
    extra = side.arrays if side else []
    return pl.pallas_call(
        body, grid=(nb, nq), name="lru_bwd_lat" if need_y else "lru_bwd_ctx",
        in_specs=in_specs + [_HBM] * len(extra), out_specs=out_specs + [_HBM] * len(extra),
        out_shape=out_shape + (side.out_shapes if side else []), input_output_aliases={len(args) - 1: 0},
        scratch_shapes=[pltpu.VMEM((t, LRU_CB), F32)] * 7 + (side.sems if side else []), compiler_params=_params(),
    )(*args, *extra)


def _mix_core(y_ref, yl_ref, p_ref, nw_ref, bg_ref, wbs_ref, wbl_ref, wo_ref, nrm_s):
    for g in range(SSD_G):
        sl = slice(256 * g, 256 * g + 256)
        nrm_s[:, sl] = _grms(y_ref[:, sl], p_ref[:, sl], nw_ref[:, sl]).astype(BF16)
    br_s = jnp.dot(nrm_s[...], wbs_ref[...], preferred_element_type=F32)
    gl = (yl_ref[...] * _gelu(p_ref[:, 2048:3072])).astype(BF16)
    br_l = jnp.dot(gl, wbl_ref[...], preferred_element_type=F32)
    gs = _sigmoid(p_ref[:, 3072:4096] + bg_ref[:, 0:D])
    gr = _sigmoid(p_ref[:, 4096:5120] + bg_ref[:, D:2 * D])
    mix = (gs * br_s + gr * br_l).astype(BF16)
    xmix = jnp.dot(mix, wo_ref[...], preferred_element_type=F32)
    return br_s, gl, br_l, gs, gr, mix, xmix


def _mix_specs(rt, tiles_per_b):
    row = lambda w: BS((rt, w), lambda i: (i, 0))
    const = lambda shp: BS(shp, lambda i: (0,) * len(shp))
    gate = BS((None, None, 1, D), lambda i: (i // tiles_per_b, 2, 0, 0))
    return row, const, gate


def _mix_fwd(y, ylru, proj, x, m4, wbs, wbl, wo, nw, bg, l1g, l1b, rt, tiles_per_b):
    n = x.shape[0]

    def body(y_ref, yl_ref, p_ref, x_ref, g1_ref, wbs_ref, wbl_ref, wo_ref, nw_ref, bg_ref, lg_ref, lb_ref,
             x1_ref, nrm_ref, gl_ref, mix_ref, brs_ref, brl_ref, xm_ref):
        br_s, gl, br_l, _, _, mix, xmix = _mix_core(y_ref, yl_ref, p_ref, nw_ref, bg_ref, wbs_ref, wbl_ref, wo_ref, nrm_ref)
        gl_ref[...] = gl
        mix_ref[...] = mix
        brs_ref[...] = br_s
        brl_ref[...] = br_l
        xm_ref[...] = xmix
        x1_ref[...] = _resln(x_ref[...], xmix, g1_ref[...], lg_ref[...], lb_ref[...])

    row, const, gate = _mix_specs(rt, tiles_per_b)
    return pl.pallas_call(
        body, grid=(n // rt,), name="mix_fwd",
        in_specs=[row(SSD_INNER), row(D), BS((rt, 5120), lambda i: (i, 1)), row(D), gate,
                  const((SSD_INNER, D)), const((D, D)), const((D, D)), const((1, SSD_INNER)), const((1, 2 * D)),
                  const((1, D)), const((1, D))],
        out_specs=[row(D), row(SSD_INNER), row(D), row(D), row(D), row(D), row(D)],
        out_shape=[S((n, D), F32), S((n, SSD_INNER), BF16), S((n, D), BF16), S((n, D), BF16), S((n, D), F32),
                   S((n, D), F32), S((n, D), F32)],
        compiler_params=_params(),
    )(y, ylru, proj, x, m4, wbs, wbl, wo, nw, bg, l1g, l1b)


def _mix_bwd(y, ylru, proj, x, m4, wbs, wbl, wo, nw, bg, l1g, l1b, brs, brl, xmix, dx1, dproj, rt, tiles_per_b):
    n = x.shape[0]

    def body(y_ref, yl_ref, p_ref, x_ref, g1_ref, wbs_ref, wbl_ref, wo_ref, nw_ref, bg_ref, lg_ref, lb_ref,
             brs_ref, brl_ref, xm_ref, dx1_ref, _,
             dp_ref, dy_ref, dyl_ref, dxr_ref, dbrs_ref, dbrl_ref, dxm_ref,
             dg1_ref, dnw_ref, dbg_ref, dlg_ref, dlb_ref):
        i = pl.program_id(0)

        @pl.when(i == 0)
        def _():
            for r in (dnw_ref, dbg_ref, dlg_ref, dlb_ref):
                r[...] = jnp.zeros_like(r)

        @pl.when(i % tiles_per_b == 0)
        def _():
            dg1_ref[...] = jnp.zeros_like(dg1_ref)

        br_s, br_l = brs_ref[...], brl_ref[...]
        gs = _sigmoid(p_ref[:, 3072:4096] + bg_ref[:, 0:D])
        gr = _sigmoid(p_ref[:, 4096:5120] + bg_ref[:, D:2 * D])
        _, vjp = jax.vjp(_resln, x_ref[...], xm_ref[...], g1_ref[...], lg_ref[...], lb_ref[...])
        dxr, dxmix, dg1, dlg, dlb = vjp(dx1_ref[...])
        dxr_ref[...] = dxr
        dg1_ref[...] += dg1
        dlg_ref[...] += dlg
        dlb_ref[...] += dlb
        dxmb = dxmix.astype(BF16)
        dxm_ref[...] = dxmb
        dmix = lax.dot_general(dxmb, wo_ref[...], (((1,), (1,)), ((), ())), preferred_element_type=F32)
        dbrs = (dmix * gs).astype(BF16)
        dbrl = (dmix * gr).astype(BF16)
        dbrs_ref[...] = dbrs
        dbrl_ref[...] = dbrl
        dmg_s = dmix * br_s * gs * (1.0 - gs)
        dmg_r = dmix * br_l * gr * (1.0 - gr)
        dp_ref[:, 3072:4096] = dmg_s.astype(BF16)
        dp_ref[:, 4096:5120] = dmg_r.astype(BF16)
        dbg_ref[:, 0:D] += _colsum(dmg_s)
        dbg_ref[:, D:2 * D] += _colsum(dmg_r)
        dnrm = lax.dot_general(dbrs, wbs_ref[...], (((1,), (1,)), ((), ())), preferred_element_type=F32)
        for g in range(SSD_G):
            sl = slice(256 * g, 256 * g + 256)
            _, vjp = jax.vjp(_grms, y_ref[:, sl], p_ref[:, sl], nw_ref[:, sl])
            dyg, dzg, dnwg = vjp(dnrm[:, sl])
            dy_ref[:, sl] = dyg
            dp_ref[:, sl] = dzg.astype(BF16)
            dnw_ref[:, sl] += dnwg
        dgl = lax.dot_general(dbrl, wbl_ref[...], (((1,), (1,)), ((), ())), preferred_element_type=F32)
        _, vjp = jax.vjp(lambda a, c: a * _gelu(c), yl_ref[...], p_ref[:, 2048:3072])
        dyl, dlgate = vjp(dgl)
        dyl_ref[...] = dyl
        dp_ref[:, 2048:3072] = dlgate.astype(BF16)

    row, const, gate = _mix_specs(rt, tiles_per_b)
    pblk = BS((rt, 5120), lambda i: (i, 1))
    nb = n // (rt * tiles_per_b)
    out_specs = [pblk, row(SSD_INNER), row(D), row(D), row(D), row(D), row(D),
                 BS((None, 1, D), lambda i: (i // tiles_per_b, 0, 0)), const((1, SSD_INNER)), const((1, 2 * D)),
                 const((1, D)), const((1, D))]
    out_shape = [S(dproj.shape, BF16), S((n, SSD_INNER), F32), S((n, D), F32), S((n, D), F32),
                 S((n, D), BF16), S((n, D), BF16), S((n, D), BF16),
                 S((nb, 1, D), F32), S((1, SSD_INNER), F32), S((1, 2 * D), F32), S((1, D), F32), S((1, D), F32)]
    return pl.pallas_call(
        body, grid=(n // rt,), name="mix_bwd",
        in_specs=[row(SSD_INNER), row(D), pblk, row(D), gate,
                  const((SSD_INNER, D)), const((D, D)), const((D, D)), const((1, SSD_INNER)), const((1, 2 * D)),
                  const((1, D)), const((1, D)), row(D), row(D), row(D), row(D), BS(memory_space=pl.ANY)],
        out_specs=out_specs, out_shape=out_shape, input_output_aliases={16: 0},
        compiler_params=_params(),
    )(y, ylru, proj, x, m4, wbs, wbl, wo, nw, bg, l1g, l1b, brs, brl, xmix, dx1, dproj)


def _mlp_step(x1, tgt, m4, w1, b1, w2, b2, l2g, l2b, rt, tiles_per_b):
    n = x1.shape[0]

    def body(x_ref, t_ref, sh_ref, sc_ref, gt_ref, w1_hbm, b1_ref, w2_hbm, b2_ref, lg_ref, lb_ref,
             loss_ref, dx_ref, h2_ref, da1_ref, r2_ref, dmlp_ref, dm_ref, db1_ref, db2_ref, dlg_ref, dlb_ref,
             w1_vm, w2_vm, sem):
        i = pl.program_id(0)

        @pl.when(i == 0)
        def _():
            c1 = pltpu.make_async_copy(w1_hbm, w1_vm, sem.at[0])
            c2 = pltpu.make_async_copy(w2_hbm, w2_vm, sem.at[1])
            c1.start()
            c2.start()
            for r in (loss_ref, db1_ref, db2_ref, dlg_ref, dlb_ref):
                r[...] = jnp.zeros_like(r)
            c1.wait()
            c2.wait()

        @pl.when(i % tiles_per_b == 0)
        def _():
            dm_ref[...] = jnp.zeros_like(dm_ref)

        x1v = x_ref[...]
        h2, vjp_h = jax.vjp(_modln, x1v, sh_ref[...], sc_ref[...])
        h2b = h2.astype(BF16)
        h2_ref[...] = h2b
        r = jnp.maximum(jnp.dot(h2b, w1_vm[...], preferred_element_type=F32) + b1_ref[...], 0.0)
        r2b = (r * r).astype(BF16)
        r2_ref[...] = r2b
        mlp = jnp.dot(r2b, w2_vm[...], preferred_element_type=F32) + b2_ref[...]
        x2, vjp_r = jax.vjp(_resln, x1v, mlp, gt_ref[...], lg_ref[...], lb_ref[...])
        diff = x2 - t_ref[...]
        loss_ref[...] += (0.5 / D) * jnp.sum(diff * diff)
        dxa, dmlp, dgt, dlg, dlb = vjp_r(diff * (1.0 / D))
        dlg_ref[...] += dlg
        dlb_ref[...] += dlb
        dm_ref[2:3, :] += dgt
        db2_ref[...] += _colsum(dmlp)
        dmlpb = dmlp.astype(BF16)
        dmlp_ref[...] = dmlpb
        da1 = lax.dot_general(dmlpb, w2_vm[...], (((1,), (1,)), ((), ())), preferred_element_type=F32) * (2.0 * r)
        db1_ref[...] += _colsum(da1)
        da1b = da1.astype(BF16)
        da1_ref[...] = da1b
        dh2 = lax.dot_general(da1b, w1_vm[...], (((1,), (1,)), ((), ())), preferred_element_type=F32)
        dxb, dsh, dsc = vjp_h(dh2)
        dx_ref[...] = dxa + dxb
        dm_ref[0:1, :] += dsh
        dm_ref[1:2, :] += dsc

    row = lambda w: BS((rt, w), lambda i: (i, 0))
    const = lambda shp: BS(shp, lambda i: (0,) * len(shp))
    mod = lambda k: BS((None, None, 1, D), lambda i: (i // tiles_per_b, k, 0, 0))
    nb = n // (rt * tiles_per_b)
    anyspec = BS(memory_space=pl.ANY)
    return pl.pallas_call(
        body, grid=(n // rt,), name="mlp_step",
        in_specs=[row(D), row(D), mod(3), mod(4), mod(5), anyspec, const((1, MLP_H)), anyspec, const((1, D)),
                  const((1, D)), const((1, D))],
        out_specs=[const((8, 128)), row(D), row(D), row(MLP_H), row(MLP_H), row(D),
                   BS((None, 3, D), lambda i: (i // tiles_per_b, 0, 0)), const((1, MLP_H)), const((1, D)),
                   const((1, D)), const((1, D))],
        out_shape=[S((8, 128), F32), S((n, D), F32), S((n, D), BF16), S((n, MLP_H), BF16), S((n, MLP_H), BF16),
                   S((n, D), BF16), S((nb, 3, D), F32), S((1, MLP_H), F32), S((1, D), F32), S((1, D), F32),
                   S((1, D), F32)],
        scratch_shapes=[pltpu.VMEM((D, MLP_H), BF16), pltpu.VMEM((MLP_H, D), BF16), pltpu.SemaphoreType.DMA((2,))],
        compiler_params=_params(),
    )(x1, tgt, m4, m4, m4, w1, b1, w2, b2, l2g, l2b)


def _pack_win(w):
    parts = []
    for g in range(SSD_G):
        parts += [w[:, 256 * g:256 * g + 256], w[:, 2048 + 128 * g:2176 + 128 * g], w[:, 4160 + 128 * g:4288 + 128 * g]]
    parts += [w[:, 3136:4160], w[:, 5184:7232], w[:, 7232:8256], w[:, 8256:10304], w[:, 3072:3136],
              jnp.zeros((w.shape[0], P_W - P_DT - 64), w.dtype)]
    return jnp.concatenate(parts, axis=1)


def _unpack_win(p):
    xs = [p[:, 512 * g:512 * g + 256] for g in range(SSD_G)]
    bs = [p[:, 512 * g + 256:512 * g + 384] for g in range(SSD_G)]
    cs = [p[:, 512 * g + 384:512 * g + 512] for g in range(SSD_G)]
    return jnp.concatenate(xs + bs + [p[:, P_DT:P_DT + 64], p[:, P_LRU:P_Z]] + cs + [p[:, P_Z:P_DT]], axis=1)


def _pack_conv(w):
    return jnp.stack([jnp.concatenate([w[:, 256 * g:256 * g + 256], w[:, 2048 + 128 * g:2176 + 128 * g],
                                       w[:, 3072 + 128 * g:3200 + 128 * g]], axis=1) for g in range(SSD_G)])


def _unpack_conv(p):
    r = p.shape[1]
    x = jnp.transpose(p[:, :, 0:256], (1, 0, 2)).reshape(r, 2048)
    b = jnp.transpose(p[:, :, 256:384], (1, 0, 2)).reshape(r, 1024)
    c = jnp.transpose(p[:, :, 384:512], (1, 0, 2)).reshape(r, 1024)
    return jnp.concatenate([x, b, c], axis=1)


def _pack_heads(v):
    p = jnp.transpose(v.reshape(2, SSD_G, 4), (1, 0, 2)).reshape(SSD_G, 1, 8)
    return jnp.pad(p, ((0, 0), (0, 0), (0, 120)))


def _unpack_heads(p):
    return jnp.transpose(p[:, 0, 0:8].reshape(SSD_G, 2, 4), (1, 0, 2)).reshape(2, 32)


def _pack_dt(dt):
    n = dt.shape[0]
    p = jnp.transpose(dt.reshape(n, 2, SSD_G, 4), (2, 0, 1, 3)).reshape(SSD_G, n, 8)
    return jnp.pad(p, ((0, 0), (0, 0), (0, 120)))


def _unpack_dt(p):
    n = p.shape[1]
    return jnp.transpose(p[:, :, 0:8].reshape(SSD_G, n, 2, 4), (1, 2, 0, 3)).reshape(n, 64)


def _tk(rows):
    return 512 if rows % 512 == 0 else (256 if rows % 256 == 0 else 128)


LATE = ["w_br_ssd", "w_br_lru", "w_out", "w_mlp1", "w_mlp2"]


def _local_step(x, c, ctx, tgt, sm, wmod, win, late, late_are_shards=False, reducer=None):
    nb, t, _ = x.shape
    tc = ctx.shape[1]
    nl, ncx = nb * t, nb * tc
    rt = 256 if tc % 256 == 0 else 128
    rtm = 128
    xl, xc = x.reshape(nl, D), ctx.reshape(ncx, D)
    tgt2 = tgt.reshape(nl, D)
    cc = jnp.zeros((8, D), F32).at[0:nb].set(c).at[nb].set(sm["c_ctx"])
    m = _mod_fwd(cc, wmod, sm["b_mod"])
    m4 = m.reshape(8, N_MOD, 1, D)
    proj, h1 = _inproj_fwd(xl, m4, win, rtm, nl // rtm, t // rtm, nb, "inproj_fwd_lat")
    proj_c, h1_c = _inproj_fwd(xc, m4, win, rtm, 0, 1, nb, "inproj_fwd_ctx")

    cw_s, cb_s = _pack_conv(sm["ssd_conv_w"]), _pack_conv(sm["ssd_conv_b"])
    dtb, alog = _pack_heads(sm["ssd_dt_bias"]), _pack_heads(sm["ssd_a_log"])
    drow = jnp.repeat(sm["ssd_d"].reshape(32), 64).reshape(SSD_G, 1, 256)
    dtg, dtg_c = _pack_dt(proj[:, P_DT:P_DT + 64]), _pack_dt(proj_c[:, P_DT:P_DT + 64])
    zst = jnp.zeros((nb, SSD_G, SSD_N, 256), F32)
    zl = jnp.zeros((nb, 2, D), F32)
    ssd_p = (cw_s, cb_s, dtb, alog, drow)
    lru_p = (sm["lru_conv_w"], sm["lru_conv_b"], sm["lru_wa"], sm["lru_ba"], sm["lru_wi"], sm["lru_bi"], sm["lru_lambda"])

    chsf, chsb, csf, csb = _ssd_fwd(proj_c, dtg_c, *ssd_p, zst, zst, nb, tc, tc, 0, False)
    y, lhsf, lhsb, _, _, *got = _ssd_fwd(proj, dtg, *ssd_p, csf, csb, nb, t, GRID_W, 0, True,
                                         tuple(late) if late_are_shards else ())
    wbs, wbl, wo, w1, w2 = [_full_from_chips(g, n) for g, n in zip(got, LATE)] if late_are_shards else late
    chf, chb, cfin = _lru_fwd(proj_c, *lru_p, zl, nb, tc, tc, 0, False)
    ylru, lhf, lhb, _ = _lru_fwd(proj, *lru_p, cfin, nb, t, GRID_W, 0, True)
    mix_w = (wbs, wbl, wo, sm["ssd_norm_w"], sm["b_gate"], sm["ln1_g"], sm["ln1_b"])
    x1, nrm, gl, mixb, brs, brl, xmix = _mix_fwd(y, ylru, proj, xl, m4, *mix_w, rtm, t // rtm)
    (loss, dx1, h2, da1, r2, dmlp, dm2, db1, db2, dl2g, dl2b) = _mlp_step(
        x1, tgt2, m4, w1, sm["b_mlp1"], w2, sm["b_mlp2"], sm["ln2_g"], sm["ln2_b"], rt, t // rt)

    dproj = lax.empty((nl, P_W), BF16)
    dproj_c = jnp.zeros((ncx, P_W), BF16)
    (dproj, dy, dylru, dxres, dbrs, dbrl, dxm, dg1, dnw, dbg, dl1g, dl1b) = _mix_bwd(
        y, ylru, proj, xl, m4, *mix_w, brs, brl, xmix, dx1, dproj, rtm, t // rtm)
    big = {
        "w_br_ssd": _matmul_tn(nrm, dbrs, D, D, _tk(nl), "dw_br_ssd"),
        "w_br_lru": _matmul_tn(gl, dbrl, D, D, _tk(nl), "dw_br_lru"),
        "w_out": _matmul_tn(mixb, dxm, D, D, _tk(nl), "dw_out"),
        "w_mlp1": _matmul_tn(h2, da1, D, D, _tk(nl), "dw_mlp1"),
        "w_mlp2": _matmul_tn(r2, dmlp, D, D, _tk(nl), "dw_mlp2"),
    }
    (dproj, ddt_l, dh0f, dh0b, dcw_l, dcb_l, ddtb_l, dal_l, dd) = _ssd_bwd(
        proj, dtg, *ssd_p, lhsf, lhsb, dy, zst, zst, dproj, nb, t, GRID_W, 0, True)
    (dproj_c, ddt_c, _, _, dcw_c, dcb_c, ddtb_c, dal_c, _) = _ssd_bwd(
        proj_c, dtg_c, *ssd_p, chsf, chsb, None, dh0f, dh0b, dproj_c, nb, tc, tc, 0, False)
    side = reducer.begin(big, list(big)) if reducer else None
    (dproj, dlh0, gcw_l, gcb_l, gwa_l, gwi_l, gba_l, gbi_l, glam_l, *got) = _lru_bwd(
        proj, *lru_p, cfin, lhf, lhb, dylru, zl, dproj, nb, t, GRID_W, 0, True, side)
    if reducer:
        reducer.end(got)
    (dproj_c, _, gcw_c, gcb_c, gwa_c, gwi_c, gba_c, gbi_c, glam_c) = _lru_bwd(
        proj_c, *lru_p, zl, chf, chb, None, dlh0, dproj_c, nb, tc, tc, 0, False)
    pad_dt = lambda d: jnp.pad(_unpack_dt(d).astype(BF16), ((0, 0), (0, P_W - P_DT - 64)))
    dproj = lax.dynamic_update_slice(dproj, pad_dt(ddt_l), (0, P_DT))
    dproj_c = lax.dynamic_update_slice(dproj_c, pad_dt(ddt_c), (0, P_DT))

    big["w_in"] = _matmul_tn2(h1, dproj, h1_c, dproj_c, D, 1152, min(_tk(nl), _tk(ncx)), "dw_in")
    side = reducer.begin(big, ["w_in"]) if reducer else None
    gx, dm1, *got = _inproj_bwd(xl, m4, win, dproj, rt, 0, nl // rt, t // rt, nb, True, dxres, side)
    if reducer:
        reducer.end(got)
    (dmc,) = _inproj_bwd(xc, m4, win, dproj_c, rt, 0, ncx // rt, ncx // rt, nb, False, None)
    dm = jnp.zeros((8, N_MOD, D), F32)
    dm = dm.at[0:nb].set(jnp.concatenate([dm1, dg1, dm2], axis=1)).at[nb, 0:2].set(dmc[0])
    dwmod, dbmod, dcc = _mod_bwd(cc, wmod, dm.reshape(8, N_MOD * D))
    big["w_mod"] = dwmod
    nq = D // LRU_CB
    small = {
        "c_ctx": dcc[nb],
        "b_mod": dbmod,
        "b_gate": dbg,
        "ssd_conv_w": _unpack_conv(dcw_l + dcw_c),
        "ssd_conv_b": _unpack_conv(dcb_l + dcb_c),
        "ssd_dt_bias": _unpack_heads(ddtb_l + ddtb_c),
        "ssd_a_log": _unpack_heads(dal_l + dal_c),
        "ssd_d": jnp.sum(dd.reshape(32, 64), axis=1),
        "ssd_norm_w": dnw,
        "lru_conv_w": jnp.transpose(gcw_l + gcw_c, (1, 0, 2)).reshape(4, D),
        "lru_conv_b": (gcb_l + gcb_c).reshape(1, D),
        "lru_wa": gwa_l + gwa_c,
        "lru_ba": jnp.transpose(gba_l + gba_c, (1, 0, 2)).reshape(2, D),
        "lru_wi": gwi_l + gwi_c,
        "lru_bi": jnp.transpose(gbi_l + gbi_c, (1, 0, 2)).reshape(2, D),
        "lru_lambda": jnp.transpose(glam_l + glam_c, (1, 0, 2)).reshape(2, D),
        "ln1_g": dl1g, "ln1_b": dl1b, "b_mlp1": db1, "b_mlp2": db2, "ln2_g": dl2g, "ln2_b": dl2b,
    }
    return loss[0, 0], gx.reshape(nb, t, D), big, small


_HBM = BS(memory_space=pl.ANY)


def _place():
    return lax.axis_index("x"), lax.axis_index("y"), lax.axis_index("c")


def _other_chips(x, y):
    return [(1 - x, y), (x, 1 - y), (1 - x, 1 - y)]


def _gather_chips(arrs):
    n = len(arrs)

    def body(*refs):
        ex = _GatherExchange(refs[:n], refs[n:2 * n], refs[2 * n:])
        ex.begin()
        ex.finish()

    return pl.pallas_call(
        body, name="gather_weights", in_specs=[_HBM] * n, out_specs=[_HBM] * n,
        out_shape=_gather_out_shapes(arrs), scratch_shapes=_gather_sems(n),
    )(*arrs)


def _gather_out_shapes(arrs):
    return [S((4,) + a.shape, a.dtype) for a in arrs]


def _gather_sems(n):
    return [pltpu.SemaphoreType.DMA((3 * n,))] * 4 + [pltpu.SemaphoreType.DMA((n,))]


class _GatherExchange:
    def __init__(self, ins, outs, sems):
        self.ins, self.outs = ins, outs
        self.ici_send, self.ici_recv, self.d2d_send, self.d2d_recv, self.loc_sems = sems
        self.x, self.y, self.c = _place()
        self.me = 2 * self.x + self.y
        self.chips = _other_chips(self.x, self.y)

    def _half(self, a, which):
        hr = self.ins[a].shape[0] // 2
        return pl.ds(pl.multiple_of((self.c if which == 0 else 1 - self.c) * hr, 8), hr)

    def _local(self, a):
        return pltpu.make_async_copy(self.ins[a], self.outs[a].at[self.me], self.loc_sems.at[a])

    def _ici(self, a, k, slot):
        px, py = self.chips[k]
        mine = self._half(a, 0)
        return pltpu.make_async_remote_copy(src_ref=self.ins[a].at[mine], dst_ref=self.outs[a].at[slot, mine],
                                            send_sem=self.ici_send.at[3 * a + k], recv_sem=self.ici_recv.at[3 * a + k],
                                            device_id=(px, py, self.c), device_id_type=MESH)

    def _d2d(self, a, k, which):
        px, py = self.chips[k]
        rows = self.outs[a].at[2 * px + py, self._half(a, which)]
        return pltpu.make_async_remote_copy(src_ref=rows, dst_ref=rows, send_sem=self.d2d_send.at[3 * a + k],
                                            recv_sem=self.d2d_recv.at[3 * a + k],
                                            device_id=(self.x, self.y, 1 - self.c), device_id_type=MESH)

    def begin(self):
        for a in range(len(self.ins)):
            self._local(a).start()
            for k in range(3):
                self._ici(a, k, self.me).start()

    def finish(self):
        n = len(self.ins)
        for a in range(n):
            for k, (px, py) in enumerate(self.chips):
                self._ici(a, k, 2 * px + py).wait_recv()
                self._d2d(a, k, 0).start()
        for a in range(n):
            for k in range(3):
                self._d2d(a, k, 1).wait_recv()
        for a in range(n):
            self._local(a).wait()
            for k in range(3):
                self._ici(a, k, self.me).wait_send()
                self._d2d(a, k, 0).wait_send()


def _scatter_chips(arrs):
    side = _Side("scatter", arrs)
    n = len(arrs)

    def body(*refs):
        ex = side.make(refs[:n], refs[n:2 * n], refs[2 * n:])
        ex.begin()
        ex.finish()

    return pl.pallas_call(
        body, name="scatter_grads", in_specs=[_HBM] * n, out_specs=[_HBM] * n,
        out_shape=side.out_shapes, scratch_shapes=side.sems,
    )(*arrs)


class _ScatterExchange:
    def __init__(self, ins, outs, sems):
        self.ins, self.outs = ins, outs
        self.send_sems, self.recv_sems = sems
        x, y, self.c = _place()
        self.chips = _other_chips(x, y)

    def _copy(self, a, k):
        px, py = self.chips[k]
        return pltpu.make_async_remote_copy(src_ref=self.ins[a].at[2 * px + py], dst_ref=self.outs[a].at[k],
                                            send_sem=self.send_sems.at[3 * a + k], recv_sem=self.recv_sems.at[3 * a + k],
                                            device_id=(px, py, self.c), device_id_type=MESH)

    def begin(self):
        for a in range(len(self.ins)):
            for k in range(3):
                self._copy(a, k).start()

    def finish(self):
        for a in range(len(self.ins)):
            for k in range(3):
                self._copy(a, k).wait_recv()
        for a in range(len(self.ins)):
            for k in range(3):
                self._copy(a, k).wait_send()


class _Side:
    def __init__(self, kind, arrays):
        self.arrays = list(arrays)
        n = len(self.arrays)
        if kind == "gather":
            self.out_shapes, self.sems, self.make = _gather_out_shapes(self.arrays), _gather_sems(n), _GatherExchange
        else:
            self.out_shapes = [S((3,) + a.shape[1:], a.dtype) for a in self.arrays]
            self.sems = [pltpu.SemaphoreType.DMA((3 * n,))] * 2
            self.make = _ScatterExchange

    def split(self, refs, n_in, n_out, n_scr):
        a, b = len(self.arrays), len(self.out_shapes)
        i1 = n_in + a
        o1 = i1 + n_out
        o2 = o1 + b
        s1 = o2 + n_scr
        own = tuple(refs[:n_in]) + tuple(refs[i1:o1]) + tuple(refs[o2:s1])
        return own, self.make(refs[n_in:i1], refs[o1:o2], refs[s1:])

    def around(self, exchange, step, n_steps, compute):
        pl.when(step == 0)(exchange.begin)
        compute()
        pl.when(step == n_steps - 1)(exchange.finish)


def _swap_halves(arrs, name):
    n = len(arrs)

    def body(*refs):
        ins, outs = refs[:n], refs[n:2 * n]
        send_sems, recv_sems = refs[2 * n:]
        x, y, c = _place()
        sends = []
        for a in range(n):
            hr = arrs[a].shape[1] // 2
            theirs = pl.ds(pl.multiple_of((1 - c) * hr, 8), hr)
            for q in range(4):
                cp = pltpu.make_async_remote_copy(src_ref=ins[a].at[q, theirs], dst_ref=outs[a].at[q],
                                                  send_sem=send_sems.at[4 * a + q], recv_sem=recv_sems.at[4 * a + q],
                                                  device_id=(x, y, 1 - c), device_id_type=MESH)
                cp.start()
                sends.append(cp)
        for cp in sends:
            cp.wait_recv()
        for cp in sends:
            cp.wait_send()

    return pl.pallas_call(
        body, name=name, in_specs=[_HBM] * n, out_specs=[_HBM] * n,
        out_shape=[S((4, a.shape[1] // 2, a.shape[2]), a.dtype) for a in arrs],
        scratch_shapes=[pltpu.SemaphoreType.DMA((4 * n,)), pltpu.SemaphoreType.DMA((4 * n,))],
    )(*arrs)


def _allreduce_small(v):
    def body(in_ref, out_ref, r0, r1, r2, send_sems, recv_sems):
        x, y, c = _place()
        src = in_ref
        for s, (buf, peer) in enumerate(((r0, (x, y, 1 - c)), (r1, (x, 1 - y, c)), (r2, (1 - x, y, c)))):
            cp = pltpu.make_async_remote_copy(src_ref=src, dst_ref=buf, send_sem=send_sems.at[s], recv_sem=recv_sems.at[s],
                                              device_id=peer, device_id_type=MESH)
            cp.start()
            cp.wait()
            out_ref[...] = src[...] + buf[...]
            src = out_ref

    vm = BS(memory_space=pltpu.VMEM)
    return pl.pallas_call(
        body, name="allreduce_small", in_specs=[vm], out_specs=vm, out_shape=S(v.shape, F32),
        scratch_shapes=[pltpu.VMEM(v.shape, F32)] * 3 + [pltpu.SemaphoreType.DMA((3,)), pltpu.SemaphoreType.DMA((3,))],
        compiler_params=_params(),
    )(v)


def _swap_cores(arrs):
    n = len(arrs)

    def body(*refs):
        ins, outs = refs[:n], refs[n:2 * n]
        send_sems, recv_sems = refs[2 * n:]
        x, y, c = _place()
        sends = []
        for a in range(n):
            cp = pltpu.make_async_remote_copy(src_ref=ins[a], dst_ref=outs[a], send_sem=send_sems.at[a],
                                              recv_sem=recv_sems.at[a], device_id=(x, y, 1 - c), device_id_type=MESH)
            cp.start()
            sends.append(cp)
        for cp in sends:
            cp.wait_recv()
        for cp in sends:
            cp.wait_send()

    return pl.pallas_call(
        body, name="swap_cores", in_specs=[_HBM] * n, out_specs=[_HBM] * n,
        out_shape=[S(a.shape, a.dtype) for a in arrs],
        scratch_shapes=[pltpu.SemaphoreType.DMA((n,)), pltpu.SemaphoreType.DMA((n,))],
    )(*arrs)


def _row_tile(r, c=128):
    tr = 256 if c <= 1024 else (128 if c <= 2048 else 64)
    return tr if r % tr == 0 else r


def _sum_half(own, sib, core, name):
    _, r, c = own.shape
    hr = r // 2
    tr = _row_tile(hr, c)
    nbk = hr // tr

    def body(core_ref, o_ref, s_ref, p_ref, pb_ref):
        p = o_ref[...] + s_ref[...]
        p_ref[...] = p
        pb_ref[...] = p.astype(BF16)

    blk = BS((None, tr, c), lambda q, i, cr: (q, i, 0))
    return pl.pallas_call(
        body, name=name, out_shape=[S((4, hr, c), F32), S((4, hr, c), BF16)],
        grid_spec=pltpu.PrefetchScalarGridSpec(
            num_scalar_prefetch=1, grid=(4, nbk),
            in_specs=[BS((None, tr, c), lambda q, i, cr: (q, cr[0] * nbk + i, 0)), blk], out_specs=[blk, blk]),
        compiler_params=_params(),
    )(core, own, sib)


def _sum4(part, recv, chip, name):
    _, r, c = part.shape
    tr = _row_tile(r, c)

    def body(chip_ref, o_ref, r_ref, out_ref):
        acc = o_ref[...]
        for k in range(3):
            acc = acc + r_ref[k].astype(F32)
        out_ref[...] = acc

    return pl.pallas_call(
        body, name=name, out_shape=S((r, c), F32),
        grid_spec=pltpu.PrefetchScalarGridSpec(
            num_scalar_prefetch=1, grid=(r // tr,),
            in_specs=[BS((None, tr, c), lambda i, ch: (ch[0], i, 0)), BS((3, tr, c), lambda i, ch: (0, i, 0))],
            out_specs=BS((tr, c), lambda i, ch: (i, 0))),
        compiler_params=_params(),
    )(chip, part, recv)


def _adam_math(w, g, m, v):
    m = ADAM_B1 * m + (1.0 - ADAM_B1) * g
    v = ADAM_B2 * v + (1.0 - ADAM_B2) * (g * g)
    m_hat = m / (1.0 - ADAM_B1 ** ADAM_STEP)
    v_hat = v / (1.0 - ADAM_B2 ** ADAM_STEP)
    return -ADAM_LR * (m_hat / (jnp.sqrt(v_hat) + ADAM_EPS) + ADAM_WD * w), m, v


def _adam_halves(mine, other, w, m, v, core, name):
    r, c = w.shape
    tr = _row_tile(r // 2, c)
    nbk = (r // 2) // tr

    def body(core_ref, a_ref, b_ref, w_ref, m_ref, v_ref, g_ref, d_ref, nm_ref, nv_ref):
        g = jnp.where(pl.program_id(0) // nbk == core_ref[0], a_ref[...], b_ref[...])
        g_ref[...] = g
        d_ref[...], nm_ref[...], nv_ref[...] = _adam_math(w_ref[...], g, m_ref[...], v_ref[...])

    spec = BS((tr, c), lambda i, cr: (i, 0))
    half = BS((tr, c), lambda i, cr: (i % nbk, 0))
    return pl.pallas_call(
        body, name=name, out_shape=[S((r, c), F32)] * 4,
        grid_spec=pltpu.PrefetchScalarGridSpec(num_scalar_prefetch=1, grid=(r // tr,), in_specs=[half, half] + [spec] * 3,
                                               out_specs=[spec] * 4),
        compiler_params=_params(),
    )(core, mine, other, w, m, v)


def _adam_flat(g, w, m, v):
    r = w.shape[0]
    tr = _row_tile(r)

    def body(g_ref, w_ref, m_ref, v_ref, d_ref, nm_ref, nv_ref):
        d_ref[...], nm_ref[...], nv_ref[...] = _adam_math(w_ref[...], g_ref[...], m_ref[...], v_ref[...])

    spec = BS((tr, 128), lambda i: (i, 0))
    return pl.pallas_call(
        body, grid=(r // tr,), name="adam_small", in_specs=[spec] * 4, out_specs=[spec] * 3,
        out_shape=[S((r, 128), F32)] * 3, compiler_params=_params(),
    )(g, w, m, v)


def _flatten(arrs, rows_mult=256):
    flat = jnp.concatenate([a.reshape(-1) for a in arrs])
    n = flat.shape[0]
    rows = -(-n // 128)
    rows = -(-rows // rows_mult) * rows_mult
    return jnp.pad(flat, (0, rows * 128 - n)).reshape(rows, 128)


def _unflatten(flat, shapes):
    flat = flat.reshape(-1)
    out, o = [], 0
    for shp in shapes:
        n = int(np.prod(shp))
        out.append(flat[o:o + n].reshape(shp))
        o += n
    return out


BIG = ["w_mod", "w_in", "w_br_ssd", "w_br_lru", "w_out", "w_mlp1", "w_mlp2"]
COL_SHARDED = {"w_mod": N_MOD * D, "w_in": IN_COLS, "w_mlp1": MLP_H}
SMALL_SHARDED = ["ssd_conv_w", "lru_conv_w", "lru_ba", "lru_bi", "lru_lambda"]
WEIGHTS = ['c_ctx', 'w_mod', 'b_mod', 'w_in', 'b_gate', 'ssd_conv_w', 'ssd_conv_b', 'ssd_dt_bias', 'ssd_a_log', 'ssd_d',
           'ssd_norm_w', 'lru_conv_w', 'lru_conv_b', 'lru_wa', 'lru_ba', 'lru_wi', 'lru_bi', 'lru_lambda', 'w_br_ssd',
           'w_br_lru', 'w_out', 'ln1_g', 'ln1_b', 'w_mlp1', 'b_mlp1', 'w_mlp2', 'b_mlp2', 'ln2_g', 'ln2_b']
SMALL = [n for n in WEIGHTS if n not in BIG]


class _Reducer:
    def __init__(self, core_id, chip_id):
        self.core_id, self.chip_id = core_id, chip_id
        self.halves, self.recv, self.pending = {}, {}, []

    def _chip_sums(self, big, names):
        slabs = [_chips_from_full(_unpack_win(big[n]) if n == "w_in" else big[n], n) for n in names]
        for n, s, o in zip(names, slabs, _swap_halves(slabs, "swap_halves_" + names[0])):
            self.halves[n] = _sum_half(s, o, self.core_id, "half_" + n)
        return [self.halves[n][1] for n in names]

    def begin(self, big, names):
        self.pending = list(names)
        return _Side("scatter", self._chip_sums(big, names))

    def end(self, received):
        self.recv.update(zip(self.pending, received))

    def finish(self, big, names):
        self.recv.update(zip(names, _scatter_chips(self._chip_sums(big, names))))
        mine = [_sum4(self.halves[n][0], self.recv[n], self.chip_id, "sum_" + n) for n in BIG]
        return dict(zip(BIG, mine)), dict(zip(BIG, _swap_cores(mine)))


def _full_from_chips(g4, name):
    if name in COL_SHARDED:
        return jnp.transpose(g4, (1, 0, 2)).reshape(g4.shape[1], 4 * g4.shape[2])
    return g4.reshape(4 * g4.shape[1], g4.shape[2])


def _chips_from_full(full, name):
    if name in COL_SHARDED:
        r, c = full.shape
        return jnp.transpose(full.reshape(r, 4, c // 4), (1, 0, 2))
    return full.reshape(4, full.shape[0] // 4, full.shape[1])


def kernel(x, c, ctx, c_ctx, w_mod, b_mod, w_in, b_gate, ssd_conv_w, ssd_conv_b, ssd_dt_bias, ssd_a_log, ssd_d, ssd_norm_w, lru_conv_w, lru_conv_b, lru_wa, lru_ba, lru_wi, lru_bi, lru_lambda, w_br_ssd, w_br_lru, w_out, ln1_g, ln1_b, w_mlp1, b_mlp1, w_mlp2, b_mlp2, ln2_g, ln2_b, loss_target, m_c_ctx, m_w_mod, m_b_mod, m_w_in, m_b_gate, m_ssd_conv_w, m_ssd_conv_b, m_ssd_dt_bias, m_ssd_a_log, m_ssd_d, m_ssd_norm_w, m_lru_conv_w, m_lru_conv_b, m_lru_wa, m_lru_ba, m_lru_wi, m_lru_bi, m_lru_lambda, m_w_br_ssd, m_w_br_lru, m_w_out, m_ln1_g, m_ln1_b, m_w_mlp1, m_b_mlp1, m_w_mlp2, m_b_mlp2, m_ln2_g, m_ln2_b, v_c_ctx, v_w_mod, v_b_mod, v_w_in, v_b_gate, v_ssd_conv_w, v_ssd_conv_b, v_ssd_dt_bias, v_ssd_a_log, v_ssd_d, v_ssd_norm_w, v_lru_conv_w, v_lru_conv_b, v_lru_wa, v_lru_ba, v_lru_wi, v_lru_bi, v_lru_lambda, v_w_br_ssd, v_w_br_lru, v_w_out, v_ln1_g, v_ln1_b, v_w_mlp1, v_b_mlp1, v_w_mlp2, v_b_mlp2, v_ln2_g, v_ln2_b):
    given = dict(locals())
    w = {n: given[n] for n in WEIGHTS}
    mom = {n: given["m_" + n] for n in WEIGHTS}
    var = {n: given["v_" + n] for n in WEIGHTS}
    chip = 2 * lax.axis_index("x") + lax.axis_index("y")

    shard2d = {n: w[n].reshape(w[n].shape[-2:]) for n in BIG}
    small_pack = _flatten([w[n] for n in SMALL_SHARDED], rows_mult=16)
    first = ["w_mod", "w_in"]
    gathered = _gather_chips([shard2d[n].astype(BF16) for n in first] + [small_pack])
    full = {n: _full_from_chips(g, n) for n, g in zip(first, gathered[:-1])}
    full["w_in"] = _pack_win(full["w_in"])
    per_chip = [_unflatten(gathered[-1][q], [w[n].shape for n in SMALL_SHARDED]) for q in range(4)]
    sm = {n: jnp.concatenate([per_chip[q][i] for q in range(4)], axis=-1) for i, n in enumerate(SMALL_SHARDED)}
    for n in SMALL:
        if n not in sm:
            sm[n] = w[n]
    sm = {n: (a.reshape(a.shape[1:]) if a.ndim >= 3 else a) for n, a in sm.items()}

    core_id = lax.axis_index("c").astype(jnp.int32).reshape(1)
    reducer = _Reducer(core_id, chip.astype(jnp.int32).reshape(1))
    loss, gx, gbig, gsmall = _local_step(x, c, ctx, loss_target, sm, full["w_mod"], full["w_in"],
                                         [shard2d[n].astype(BF16) for n in LATE], late_are_shards=True, reducer=reducer)
    mine, other = reducer.finish(gbig, ["w_mod"])
    out = {}
    for n in BIG:
        shp = shard2d[n].shape
        res = _adam_halves(mine[n], other[n], shard2d[n], mom[n].reshape(shp), var[n].reshape(shp), core_id, "adam_" + n)
        out[n] = [r.reshape(w[n].shape) for r in res]

    full_shapes = [gsmall[n].shape for n in SMALL]
    gs_all = _unflatten(_allreduce_small(_flatten([gsmall[n] for n in SMALL])), full_shapes)
    gs = {}
    for n, g in zip(SMALL, gs_all):
        if n in SMALL_SHARDED:
            width = w[n].shape[-1]
            g = lax.dynamic_slice_in_dim(g, chip * width, width, axis=g.ndim - 1)
        gs[n] = g.reshape(w[n].shape)
    shapes = [w[n].shape for n in SMALL]
    d_s, m_s, v_s = _adam_flat(_flatten([gs[n] for n in SMALL]), _flatten([w[n] for n in SMALL]),
                               _flatten([mom[n] for n in SMALL]), _flatten([var[n] for n in SMALL]))
    for n, d_, m_, v_ in zip(SMALL, _unflatten(d_s, shapes), _unflatten(m_s, shapes), _unflatten(v_s, shapes)):
        out[n] = [gs[n], d_, m_, v_]

    loss = lax.psum(loss, ("x", "y", "c"))
    return (loss, gx, *[out[n][0] for n in WEIGHTS], *[out[n][1] for n in WEIGHTS], *[out[n][2] for n in WEIGHTS],
            *[out[n][3] for n in WEIGHTS])
```

```python
import functools

import numpy as np
import jax
import jax.numpy as jnp
from jax import lax
from jax.experimental import pallas as pl
from jax.experimental.pallas import tpu as pltpu

F32, BF16 = jnp.float32, jnp.bfloat16
S = jax.ShapeDtypeStruct
BS = pl.BlockSpec
MESH = pl.DeviceIdType.MESH

D = 1024
GRID_W = 64
SSD_INNER, SSD_G, SSD_N, SSD_L = 2048, 8, 128, 128
SSD_GW = 512
MLP_H = 4096
N_MOD = 6
ALPHA = 2.0 ** 0.25
LN_EPS, RMS_EPS = 1e-6, 1e-5
LRU_C = 8.0
P_XBC, P_LRU, P_Z, P_LG, P_MG, P_DT, P_W = 0, 4096, 5120, 7168, 8192, 10240, 10368
P_CB = 3456
IN_COLS = 10304
LRU_CB = 256
ADAM_LR, ADAM_B1, ADAM_B2, ADAM_EPS, ADAM_WD, ADAM_STEP = 0.001, 0.9, 0.999, 1e-08, 0.01, 10
VMEM_LIMIT = 56 * 2 ** 20


def _params(**kw):
    return pltpu.CompilerParams(vmem_limit_bytes=VMEM_LIMIT, **kw)


def _dot(a, b):
    return jnp.dot(a.astype(BF16), b.astype(BF16), preferred_element_type=F32)


def _dot_nt(a, b):
    return lax.dot_general(a.astype(BF16), b.astype(BF16), (((1,), (1,)), ((), ())), preferred_element_type=F32)


def _dot_tn(a, b):
    return lax.dot_general(a.astype(BF16), b.astype(BF16), (((0,), (0,)), ((), ())), preferred_element_type=F32)


@jax.custom_vjp
def _mm(a, b):
    return _dot(a, b)


def _cast_pair(a, b):
    return a.astype(BF16), b.astype(BF16)


def _mm_f(a, b):
    r = _cast_pair(a, b)
    return _dot(*r), r


def _mm_b(r, g):
    g = g.astype(BF16)
    return _dot_nt(g, r[1]), _dot_tn(r[0], g)


_mm.defvjp(_mm_f, _mm_b)


@jax.custom_vjp
def _mm_nt(a, b):
    return _dot_nt(a, b)


def _mm_nt_f(a, b):
    r = _cast_pair(a, b)
    return _dot_nt(*r), r


def _mm_nt_b(r, g):
    g = g.astype(BF16)
    return _dot(g, r[1]), _dot_tn(g, r[0])


_mm_nt.defvjp(_mm_nt_f, _mm_nt_b)


@jax.custom_vjp
def _mm_tn(a, b):
    return _dot_tn(a, b)


def _mm_tn_f(a, b):
    r = _cast_pair(a, b)
    return _dot_tn(*r), r


def _mm_tn_b(r, g):
    g = g.astype(BF16)
    return _dot_nt(r[1], g), _dot(r[0], g)


_mm_tn.defvjp(_mm_tn_f, _mm_tn_b)

def _split3(v):
    h = v.astype(BF16)
    r = v - h.astype(F32)
    m = r.astype(BF16)
    return h, m, (r - m.astype(F32)).astype(BF16)


def _sel_dot(sel, v, dims):
    sel_first = dims[0] == "s"
    dn = {"sv": (((1,), (0,)), ((), ())), "sTv": (((0,), (0,)), ((), ())), "vs": (((1,), (0,)), ((), ())),
          "vsT": (((1,), (1,)), ((), ()))}[dims]
    out = None
    for part in _split3(v):
        a, b = (sel, part) if sel_first else (part, sel)
        term = lax.dot_general(a, b, dn, preferred_element_type=F32)
        out = term if out is None else out + term
    return out


@jax.custom_vjp
def _cum_mm(tri, v):
    return _sel_dot(tri, v, "sv")


_cum_mm.defvjp(lambda tri, v: (_sel_dot(tri, v, "sv"), tri),
               lambda tri, g: (jnp.zeros_like(tri), _sel_dot(tri, g, "sTv")))


@jax.custom_vjp
def _xp_mm(v, e):
    return _sel_dot(e, v, "vs")


_xp_mm.defvjp(lambda v, e: (_sel_dot(e, v, "vs"), e),
              lambda e, g: (_sel_dot(e, g, "vsT"), jnp.zeros_like(e)))


def _sigmoid(x):
    return 1.0 / (1.0 + jnp.exp(-x))


def _silu(x):
    return x * _sigmoid(x)


def _dsilu(x):
    s = _sigmoid(x)
    return s * (1.0 + x * (1.0 - s))


def _softplus(x):
    return jnp.maximum(x, 0.0) + jnp.log1p(jnp.exp(-jnp.abs(x)))


def _gelu(x):
    return 0.5 * x * (1.0 + jnp.tanh(0.7978845608028654 * (x + 0.044715 * x * x * x)))


def _ln(x):
    mu = jnp.mean(x, axis=-1, keepdims=True)
    xc = x - mu
    var = jnp.mean(xc * xc, axis=-1, keepdims=True)
    return xc * lax.rsqrt(var + LN_EPS)


def _modln(x, shift, scale):
    return _ln(x) * (1.0 + scale) + shift


def _resln(x, sub, gate, g, b):
    return _ln(ALPHA * x + gate * sub) * g + b


def _grms(y, z, w):
    u = y * _silu(z)
    return u * lax.rsqrt(jnp.mean(u * u, axis=-1, keepdims=True) + RMS_EPS) * w


def _colsum(v):
    return jnp.sum(v, axis=0, keepdims=True)


def _conv_taps(width, period):
    def masks(rows):
        pos = lax.broadcasted_iota(jnp.int32, (rows, width), 0)
        if rows != period:
            pos = pos & (period - 1)
        return [pos >= 2 - k if k < 2 else pos < period + 2 - k for k in range(4)]
    return masks


def _conv_fwd(raw, w, b, masks):
    rows = raw.shape[0]
    pre = b + raw * w[2:3, :]
    for k in (0, 1, 3):
        sh = pltpu.roll(raw, (2 - k) % rows, 0)
        pre = pre + jnp.where(masks[k], sh, 0.0) * w[k:k + 1, :]
    return pre


def _for_rows(t, rb, fn):
    n = t // rb
    unroll = 4 if n % 4 == 0 else 1

    def step(i, carry):
        for u in range(unroll):
            fn(pl.multiple_of((i * unroll + u) * rb, rb))
        return carry

    lax.fori_loop(0, n // unroll, step, 0)


def _loop_unrolled(n, unroll, body, init):
    def step(i, carry):
        for u in range(unroll):
            carry = body(i * unroll + u, carry)
        return carry

    return lax.fori_loop(0, n // unroll, step, init)


def _conv_bwd(dpre, raw, w, masks):
    rows = raw.shape[0]
    draw = dpre * w[2:3, :]
    dws = []
    for k in range(4):
        if k == 2:
            dws.append(_colsum(dpre * raw))
            continue
        back = pltpu.roll(jnp.where(masks[k], dpre, 0.0), (k - 2) % rows, 0)
        dws.append(_colsum(back * raw))
        draw = draw + back * w[k:k + 1, :]
    return draw, jnp.concatenate(dws, axis=0), _colsum(dpre)


def _mod_fwd(cc, wmod, bmod):
    def body(cc_ref, w_ref, b_ref, o_ref):
        o_ref[...] = _dot(_silu(cc_ref[...]), w_ref[...]) + b_ref[...]

    return pl.pallas_call(
        body, grid=(N_MOD,), name="mod_fwd",
        in_specs=[BS((8, D), lambda j: (0, 0)), BS((D, D), lambda j: (0, j)), BS((1, D), lambda j: (0, j))],
        out_specs=BS((8, D), lambda j: (0, j)), out_shape=S((8, N_MOD * D), F32), compiler_params=_params(),
    )(cc, wmod, bmod)


def _mod_bwd(cc, wmod, dm):
    def body(cc_ref, w_ref, dm_ref, dw_ref, db_ref, dcc_ref):
        j = pl.program_id(0)
        c = cc_ref[...]
        dmv = dm_ref[...]
        dw_ref[...] = _dot_tn(_silu(c), dmv)
        db_ref[...] = _colsum(dmv)

        @pl.when(j == 0)
        def _():
            dcc_ref[...] = jnp.zeros_like(dcc_ref)

        dcc_ref[...] += _dot_nt(dmv, w_ref[...]) * _dsilu(c)

    return pl.pallas_call(
        body, grid=(N_MOD,), name="mod_bwd",
        in_specs=[BS((8, D), lambda j: (0, 0)), BS((D, D), lambda j: (0, j)), BS((8, D), lambda j: (0, j))],
        out_specs=[BS((D, D), lambda j: (0, j)), BS((1, D), lambda j: (0, j)), BS((8, D), lambda j: (0, 0))],
        out_shape=[S((D, N_MOD * D), F32), S((1, N_MOD * D), F32), S((8, D), F32)], compiler_params=_params(),
    )(cc, wmod, dm)


def _inproj_fwd(xa, m4, win, rt, n_lat_tiles, tiles_per_b, ctx_row, name):
    n_tiles = xa.shape[0] // rt

    def mrow(i):
        return jnp.where(i < n_lat_tiles, i // tiles_per_b, ctx_row)

    def body(x_ref, sh_ref, sc_ref, w_hbm, p_ref, h_ref, w_vm, sem):
        @pl.when(pl.program_id(0) == 0)
        def _():
            cp = pltpu.make_async_copy(w_hbm, w_vm, sem)
            cp.start()
            cp.wait()

        hb = _modln(x_ref[...], sh_ref[...], sc_ref[...]).astype(BF16)
        h_ref[...] = hb
        for j in range(P_W // P_CB):
            sl = slice(j * P_CB, (j + 1) * P_CB)
            p_ref[:, sl] = jnp.dot(hb, w_vm[:, sl], preferred_element_type=F32)

    return pl.pallas_call(
        body, grid=(n_tiles,), name=name,
        in_specs=[BS((rt, D), lambda i: (i, 0)),
                  BS((None, None, 1, D), lambda i: (mrow(i), 0, 0, 0)),
                  BS((None, None, 1, D), lambda i: (mrow(i), 1, 0, 0)),
                  BS(memory_space=pl.ANY)],
        out_specs=[BS((rt, P_W), lambda i: (i, 0)), BS((rt, D), lambda i: (i, 0))],
        out_shape=[S((xa.shape[0], P_W), F32), S((xa.shape[0], D), BF16)],
        scratch_shapes=[pltpu.VMEM((D, P_W), BF16), pltpu.SemaphoreType.DMA(())], compiler_params=_params(),
    )(xa, m4, m4, win)


def _inproj_bwd(xa, m4, win, dproj, rt, tile0, n_tiles, tiles_per_b, ctx_row, latent, dxres, side=None):
    def mrow(i):
        return (i // tiles_per_b) if latent else ctx_row

    def body(*refs):
        if side is None:
            return compute(*refs)
        own, exchange = side.split(refs, len(args), len(out_shape), 2)
        side.around(exchange, pl.program_id(0), n_tiles, lambda: compute(*own))

    def compute(x_ref, sh_ref, sc_ref, dp_ref, w_hbm, *rest):
        if latent:
            dxr_ref, gx_ref, dm_ref, w_vm, sem = rest
        else:
            dm_ref, w_vm, sem = rest
        i = pl.program_id(0)

        @pl.when(i == 0)
        def _():
            cp = pltpu.make_async_copy(w_hbm, w_vm, sem)
            cp.start()
            cp.wait()

        dh = lax.dot_general(dp_ref[...], w_vm[...], (((1,), (1,)), ((), ())), preferred_element_type=F32)
        _, vjp = jax.vjp(_modln, x_ref[...], sh_ref[...], sc_ref[...])
        dx, dsh, dsc = vjp(dh)
        if latent:
            gx_ref[...] = dx + dxr_ref[...]

        @pl.when(i % tiles_per_b == 0)
        def _():
            dm_ref[...] = jnp.zeros_like(dm_ref)

        dm_ref[0:1, :] += dsh
        dm_ref[1:2, :] += dsc

    nb = n_tiles // tiles_per_b
    in_specs = [BS((rt, D), lambda i: (tile0 + i, 0)),
                BS((None, None, 1, D), lambda i: (mrow(i), 0, 0, 0)),
                BS((None, None, 1, D), lambda i: (mrow(i), 1, 0, 0)),
                BS((rt, P_W), lambda i: (tile0 + i, 0)),
                BS(memory_space=pl.ANY)]
    args = [xa, m4, m4, dproj, win]
    dm_spec = BS((None, 2, D), lambda i: (i // tiles_per_b, 0, 0))
    if latent:
        in_specs.append(BS((rt, D), lambda i: (i, 0)))
        args.append(dxres)
        out_specs = [BS((rt, D), lambda i: (i, 0)), dm_spec]
        out_shape = [S((n_tiles * rt, D), F32), S((nb, 2, D), F32)]
    else:
        out_specs = [dm_spec]
        out_shape = [S((nb, 2, D), F32)]
    extra = side.arrays if side else []
    return pl.pallas_call(
        body, grid=(n_tiles,), name="inproj_bwd_lat" if latent else "inproj_bwd_ctx",
        in_specs=in_specs + [_HBM] * len(extra), out_specs=out_specs + [_HBM] * len(extra),
        out_shape=out_shape + (side.out_shapes if side else []),
        scratch_shapes=[pltpu.VMEM((D, P_W), BF16), pltpu.SemaphoreType.DMA(())] + (side.sems if side else []),
        compiler_params=_params(),
    )(*args, *extra)


def _matmul_tn(a, b, tm, tn, tk, name):
    k, m = a.shape
    n = b.shape[1]

    def body(a_ref, b_ref, o_ref):
        @pl.when(pl.program_id(2) == 0)
        def _():
            o_ref[...] = jnp.zeros_like(o_ref)

        o_ref[...] += lax.dot_general(a_ref[...], b_ref[...], (((0,), (0,)), ((), ())), preferred_element_type=F32)

    return pl.pallas_call(
        body, grid=(m // tm, n // tn, k // tk), name=name,
        in_specs=[BS((tk, tm), lambda i, j, kk: (kk, i)), BS((tk, tn), lambda i, j, kk: (kk, j))],
        out_specs=BS((tm, tn), lambda i, j, kk: (i, j)), out_shape=S((m, n), F32), compiler_params=_params(),
    )(a, b)


def _matmul_tn2(a1, b1, a2, b2, tm, tn, tk, name):
    k1, m = a1.shape
    n = b1.shape[1]
    n1, n2 = k1 // tk, a2.shape[0] // tk

    def body(a1_ref, b1_ref, a2_ref, b2_ref, o_ref):
        kk = pl.program_id(2)

        @pl.when(kk == 0)
        def _():
            o_ref[...] = jnp.zeros_like(o_ref)

        @pl.when(kk < n1)
        def _():
            o_ref[...] += lax.dot_general(a1_ref[...], b1_ref[...], (((0,), (0,)), ((), ())), preferred_element_type=F32)

        @pl.when(kk >= n1)
        def _():
            o_ref[...] += lax.dot_general(a2_ref[...], b2_ref[...], (((0,), (0,)), ((), ())), preferred_element_type=F32)

    first = lambda kk: jnp.minimum(kk, n1 - 1)
    second = lambda kk: jnp.maximum(kk - n1, 0)
    return pl.pallas_call(
        body, grid=(m // tm, n // tn, n1 + n2), name=name,
        in_specs=[BS((tk, tm), lambda i, j, kk: (first(kk), i)), BS((tk, tn), lambda i, j, kk: (first(kk), j)),
                  BS((tk, tm), lambda i, j, kk: (second(kk), i)), BS((tk, tn), lambda i, j, kk: (second(kk), j))],
        out_specs=BS((tm, tn), lambda i, j, kk: (i, j)), out_shape=S((m, n), F32), compiler_params=_params(),
    )(a1, b1, a2, b2)


def _ssd_consts(heads_per_tile):
    n = SSD_L
    ii = lax.broadcasted_iota(jnp.int32, (n, n), 0)
    jj = lax.broadcasted_iota(jnp.int32, (n, n), 1)
    er = lax.broadcasted_iota(jnp.int32, (128, 256), 0)
    ec = lax.broadcasted_iota(jnp.int32, (128, 256), 1) >> 6
    lane = lax.broadcasted_iota(jnp.int32, (1, 128 * heads_per_tile), 1) >> 6
    per_dir = []
    for d in (0, 1):
        mask = (jj >= ii) if d else (jj <= ii)
        per_dir.append((mask, mask.astype(BF16), (er == ec + 4 * d).astype(BF16)))
    return per_dir, [(lane == h).astype(F32) for h in range(2 * heads_per_tile)]


def _ssd_chunk(x, bm, cm, dtc, dtx, alog, hst, consts, hmasks, rev):
    n = SSD_L
    mask, tri, e = consts
    cum = _cum_mm(tri, dtc * (-jnp.exp(alog)))
    cum_x = _xp_mm(cum, e)
    tot_x = cum_x[0:1, :] if rev else cum_x[n - 1:n, :]
    xd = x * dtx
    hn = jnp.exp(tot_x) * hst + _mm_tn(bm, xd * jnp.exp(tot_x - cum_x))
    if cm is None:
        return hn
    cum_t = cum.T
    cb = _mm_nt(cm, bm)
    if len(hmasks) == 4:
        y = jnp.exp(cum_x) * _mm(cm, hst)
        for h in range(4):
            k = 4 * rev + h
            decay = jnp.exp(jnp.where(mask, cum[:, k:k + 1] - cum_t[k:k + 1, :], -1e30))
            y = y + _mm(cb * decay, xd * hmasks[h])
        return y, hn
    pairs = []
    for p in range(2):
        xdp = xd[:, 128 * p:128 * p + 128]
        yp = None
        for hh in range(2):
            k = 4 * rev + 2 * p + hh
            decay = jnp.exp(jnp.where(mask, cum[:, k:k + 1] - cum_t[k:k + 1, :], -1e30))
            term = _mm(cb * decay, xdp * hmasks[hh])
            yp = term if yp is None else yp + term
        pairs.append(yp)
    return jnp.exp(cum_x) * _mm(cm, hst) + jnp.concatenate(pairs, axis=1), hn


def _ssd_fwd(proj, dtg, cw, cb, dtb, alog, drow, h0f, h0b, nb, t, period, blk0, need_y, gather=()):
    nc = t // SSD_L
    rb = period
    assert t % rb == 0
    unroll = 4 if nc % 4 == 0 else (2 if nc % 2 == 0 else 1)
    masks_of = _conv_taps(SSD_GW, period)
    ng = len(gather)
    n_out = 5 if need_y else 4

    def body(p_ref, dt_ref, cw_ref, cb_ref, dtb_ref, al_ref, d_ref, h0f_ref, h0b_ref, *rest):
        g_ins, rest = rest[:ng], rest[ng:]
        outs, g_outs, (act, dts, dtxs), g_sems = rest[:n_out], rest[n_out:n_out + ng], rest[n_out + ng:n_out + ng + 3], \
            rest[n_out + ng + 3:]
        if need_y:
            y_ref, hsf_ref, hsb_ref, sf_ref, sb_ref = outs
        else:
            hsf_ref, hsb_ref, sf_ref, sb_ref = outs
        if ng:
            exchange = _GatherExchange(g_ins, g_outs, g_sems)
            step = pl.program_id(0) * SSD_G + pl.program_id(1)
            pl.when(step == 0)(exchange.begin)
        masks = masks_of(rb)
        per_dir, hmasks = _ssd_consts(2)

        def prologue(r0):
            rows = pl.ds(r0, rb)
            a = _silu(_conv_fwd(p_ref[rows, :], cw_ref[...], cb_ref[...], masks))
            act[rows, :] = a
            if need_y:
                y_ref[rows, :] = d_ref[...] * a[:, 0:256]
            dtv = _softplus(dt_ref[rows, :] + dtb_ref[...])
            dts[rows, :] = dtv
            for d in (0, 1):
                dtxs[rows, 256 * d:256 * d + 256] = _xp_mm(dtv, per_dir[d][2])

        _for_rows(t, rb, prologue)
        al = al_ref[...]

        def chunk(ci, carry):
            out = []
            for d, hst, hs_ref in ((0, carry[0], hsf_ref), (1, carry[1], hsb_ref)):
                c = (nc - 1 - ci) if d else ci
                r0 = pl.multiple_of(c * SSD_L, SSD_L)
                a = act[pl.ds(r0, SSD_L), :]
                hs_ref[c] = hst
                res = _ssd_chunk(a[:, 0:256], a[:, 256:384], a[:, 384:512] if need_y else None,
                                 dts[pl.ds(r0, SSD_L), :], dtxs[pl.ds(r0, SSD_L), 256 * d:256 * d + 256], al, hst,
                                 per_dir[d], hmasks, d)
                if need_y:
                    y_ref[pl.ds(r0, SSD_L), :] += res[0]
                    res = res[1]
                out.append(res)
            return tuple(out)

        sf_ref[...], sb_ref[...] = _loop_unrolled(nc, unroll, chunk, (h0f_ref[...], h0b_ref[...]))
        if ng:
            pl.when(step == nb * SSD_G - 1)(exchange.finish)

    gspec = lambda shp: BS((None,) + shp, lambda b, g: (g,) + (0,) * len(shp))
    st_spec = BS((None, None, SSD_N, 256), lambda b, g: (b, g, 0, 0))
    hs_spec = BS((None, None, nc, SSD_N, 256), lambda b, g: (b, g, 0, 0, 0))
    in_specs = [BS((t, SSD_GW), lambda b, g: (blk0 + b, g)), BS((None, t, 128), lambda b, g: (g, blk0 + b, 0)),
                gspec((4, SSD_GW)), gspec((1, SSD_GW)), gspec((1, 128)), gspec((1, 128)), gspec((1, 256)),
                st_spec, st_spec]
    out_specs = [hs_spec, hs_spec, st_spec, st_spec]
    out_shape = [S((nb, SSD_G, nc, SSD_N, 256), F32)] * 2 + [S((nb, SSD_G, SSD_N, 256), F32)] * 2
    if need_y:
        out_specs = [BS((t, 256), lambda b, g: (b, g))] + out_specs
        out_shape = [S((nb * t, SSD_INNER), F32)] + out_shape
    return pl.pallas_call(
        body, grid=(nb, SSD_G), name="ssd_fwd_lat" if need_y else "ssd_fwd_ctx",
        in_specs=in_specs + [_HBM] * ng, out_specs=out_specs + [_HBM] * ng,
        out_shape=out_shape + _gather_out_shapes(gather),
        scratch_shapes=[pltpu.VMEM((t, SSD_GW), F32), pltpu.VMEM((t, 128), F32), pltpu.VMEM((t, SSD_GW), F32)]
        + (_gather_sems(ng) if ng else []),
        compiler_params=_params(),
    )(proj, dtg, cw, cb, dtb, alog, drow, h0f, h0b, *gather)


def _ssd_bwd(proj, dtg, cw, cb, dtb, alog, drow, hsf, hsb, dy, dsf, dsb, dproj, nb, t, period, blk0, need_y):
    nc = t // SSD_L
    rb = period
    assert t % rb == 0
    unroll = 2 if nc % 2 == 0 else 1
    masks_of = _conv_taps(SSD_GW, period)

    def body(*refs):
        if need_y:
            (p_ref, dt_ref, cw_ref, cb_ref, dtb_ref, al_ref, d_ref, hsf_ref, hsb_ref, dy_ref, dsf_ref, dsb_ref, _,
             dp_ref, ddt_ref, dhf_ref, dhb_ref, dcw_ref, dcb_ref, ddtb_ref, dal_ref, dd_ref,
             pre, dact, dts, ddts, dtxs) = refs
        else:
            (p_ref, dt_ref, cw_ref, cb_ref, dtb_ref, al_ref, d_ref, hsf_ref, hsb_ref, dsf_ref, dsb_ref, _,
             dp_ref, ddt_ref, dhf_ref, dhb_ref, dcw_ref, dcb_ref, ddtb_ref, dal_ref, dd_ref,
             pre, dact, dts, ddts, dtxs) = refs
            dy_ref = None
        b, g = pl.program_id(0), pl.program_id(1)

        @pl.when(jnp.logical_and(b == 0, g == 0))
        def _():
            for r in (dcw_ref, dcb_ref, ddtb_ref, dal_ref, dd_ref):
                r[...] = jnp.zeros_like(r)

        masks = masks_of(rb)
        per_dir, hmasks = _ssd_consts(1)

        def prologue(r0):
            rows = pl.ds(r0, rb)
            pre[rows, :] = _conv_fwd(p_ref[rows, :], cw_ref[...], cb_ref[...], masks)
            dtv = _softplus(dt_ref[rows, :] + dtb_ref[...])
            dts[rows, :] = dtv
            for d in (0, 1):
                dtxs[rows, 256 * d:256 * d + 256] = _xp_mm(dtv, per_dir[d][2])
            dact[rows, :] = jnp.zeros((rb, SSD_GW), F32)
            ddts[rows, :] = jnp.zeros((rb, 128), F32)

        _for_rows(t, rb, prologue)
        al = al_ref[...]
        def chunk(ci, carry):
            dal_c = carry[2]
            dhs_out = []
            for d, dh, hs_ref in ((0, carry[0], hsf_ref), (1, carry[1], hsb_ref)):
                c = ci if d else (nc - 1 - ci)
                r0 = pl.multiple_of(c * SSD_L, SSD_L)
                a = _silu(pre[pl.ds(r0, SSD_L), :])
                dtc = dts[pl.ds(r0, SSD_L), :]
                dtx = dtxs[pl.ds(r0, SSD_L), 256 * d:256 * d + 256]
                if need_y:
                    fn = lambda x_, bm_, cm_, dt_, dx_, al_, hs_: _ssd_chunk(x_, bm_, cm_, dt_, dx_, al_, hs_, per_dir[d],
                                                                             hmasks, d)
                    _, vjp = jax.vjp(fn, a[:, 0:256], a[:, 256:384], a[:, 384:512], dtc, dtx, al, hs_ref[c])
                    dx, dbm, dcm, ddtc, ddtx, dal_k, dhs = vjp((dy_ref[pl.ds(r0, SSD_L), :], dh))
                    dact[pl.ds(r0, SSD_L), 384:512] += dcm
                else:
                    fn = lambda x_, bm_, dt_, dx_, al_, hs_: _ssd_chunk(x_, bm_, None, dt_, dx_, al_, hs_, per_dir[d],
                                                                        hmasks, d)
                    _, vjp = jax.vjp(fn, a[:, 0:256], a[:, 256:384], dtc, dtx, al, hs_ref[c])
                    dx, dbm, ddtc, ddtx, dal_k, dhs = vjp(dh)
                dact[pl.ds(r0, SSD_L), 0:256] += dx
                dact[pl.ds(r0, SSD_L), 256:384] += dbm
                ddts[pl.ds(r0, SSD_L), :] += ddtc + _dot_nt(ddtx, per_dir[d][2])
                dhs_out.append(dhs)
                dal_c = dal_c + dal_k
            return dhs_out[0], dhs_out[1], dal_c

        dhf_ref[...], dhb_ref[...], dal_acc = _loop_unrolled(
            nc, unroll, chunk, (dsf_ref[...], dsb_ref[...], jnp.zeros((1, 128), F32)))

        def epilogue(r0):
            rows = pl.ds(r0, rb)
            prev = pre[rows, :]
            if need_y:
                dyv = dy_ref[rows, :]
                dact[rows, 0:256] += d_ref[...] * dyv
                dd_ref[g] += _colsum(dyv * _silu(prev[:, 0:256]))
            dpre = dact[rows, :] * _dsilu(prev)
            draw, dcw, dcb = _conv_bwd(dpre, p_ref[rows, :], cw_ref[...], masks)
            dp_ref[rows, :] = draw.astype(BF16)
            dcw_ref[g] += dcw
            dcb_ref[g] += dcb
            ddraw = ddts[rows, :] * _sigmoid(dt_ref[rows, :] + dtb_ref[...])
            ddt_ref[rows, :] = ddraw
            ddtb_ref[g] += _colsum(ddraw)

        _for_rows(t, rb, epilogue)
        dal_ref[g] += dal_acc

    gspec = lambda shp: BS((None,) + shp, lambda b, g: (g,) + (0,) * len(shp))
    full = lambda shp: BS(shp, lambda b, g: (0,) * len(shp))
    st_spec = BS((None, None, SSD_N, 256), lambda b, g: (b, g, 0, 0))
    hs_spec = BS((None, None, nc, SSD_N, 256), lambda b, g: (b, g, 0, 0, 0))
    p_spec = BS((t, SSD_GW), lambda b, g: (blk0 + b, g))
    in_specs = [p_spec, BS((None, t, 128), lambda b, g: (g, blk0 + b, 0)),
                gspec((4, SSD_GW)), gspec((1, SSD_GW)), gspec((1, 128)), gspec((1, 128)), gspec((1, 256)),
                hs_spec, hs_spec]
    args = [proj, dtg, cw, cb, dtb, alog, drow, hsf, hsb]
    if need_y:
        in_specs.append(BS((t, 256), lambda b, g: (b, g)))
        args.append(dy)
    in_specs += [st_spec, st_spec, BS(memory_space=pl.ANY)]
    args += [dsf, dsb, dproj]
    out_specs = [p_spec, BS((None, t, 128), lambda b, g: (g, b, 0)), st_spec, st_spec,
                 full((SSD_G, 4, SSD_GW)), full((SSD_G, 1, SSD_GW)), full((SSD_G, 1, 128)), full((SSD_G, 1, 128)),
                 full((SSD_G, 1, 256))]
    out_shape = [S(dproj.shape, BF16), S((SSD_G, nb * t, 128), F32),
                 S((nb, SSD_G, SSD_N, 256), F32), S((nb, SSD_G, SSD_N, 256), F32),
                 S((SSD_G, 4, SSD_GW), F32), S((SSD_G, 1, SSD_GW), F32), S((SSD_G, 1, 128), F32),
                 S((SSD_G, 1, 128), F32), S((SSD_G, 1, 256), F32)]
    return pl.pallas_call(
        body, grid=(nb, SSD_G), name="ssd_bwd_lat" if need_y else "ssd_bwd_ctx",
        in_specs=in_specs, out_specs=out_specs, out_shape=out_shape,
        input_output_aliases={len(args) - 1: 0},
        scratch_shapes=[pltpu.VMEM((t, SSD_GW), F32), pltpu.VMEM((t, SSD_GW), F32), pltpu.VMEM((t, 128), F32),
                        pltpu.VMEM((t, 128), F32), pltpu.VMEM((t, SSD_GW), F32)],
        compiler_params=_params(),
    )(*args)


def _lru_gate(u, wa, ba, wi, bi, lam):
    r = _sigmoid(_mm(u, wa) + ba)
    i = _sigmoid(_mm(u, wi) + bi)
    log_a = -LRU_C * r * _softplus(-lam)
    a = jnp.exp(log_a)
    x2 = 2.0 * log_a
    em1 = jnp.where(x2 > -0.01, x2 * (1.0 + x2 * (0.5 + x2 * (1.0 / 6.0 + x2 * (1.0 / 24.0)))), a * a - 1.0)
    return a, jnp.sqrt(-em1) * (i * u)


def _scan_pair(fwd, rev, nblk, width):
    row = lax.broadcasted_iota(jnp.int32, (8, width), 0)

    def block(a_ref, b_ref, h_ref, st, carry, reverse):
        av, bv = a_ref[pl.ds(st, 8), :], b_ref[pl.ds(st, 8), :]
        for s in (1, 2, 4):
            ok = (row < 8 - s) if reverse else (row >= s)
            sh = (8 - s) if reverse else s
            a_sh = jnp.where(ok, pltpu.roll(av, sh, 0), 1.0)
            b_sh = jnp.where(ok, pltpu.roll(bv, sh, 0), 0.0)
            bv = av * b_sh + bv
            av = av * a_sh
        h = bv + av * carry
        h_ref[pl.ds(st, 8), :] = h
        return h[0:1, :] if reverse else h[7:8, :]

    def step(i, carry):
        cf, cr = carry
        cf = block(fwd[0], fwd[1], fwd[2], pl.multiple_of(i * 8, 8), cf, False)
        cr = block(rev[0], rev[1], rev[2], pl.multiple_of((nblk - 1 - i) * 8, 8), cr, True)
        return cf, cr

    return lax.fori_loop(0, nblk, step, (fwd[3], rev[3]))


def _lru_specs(t, blk0):
    p_spec = BS((t, LRU_CB), lambda b, q: (blk0 + b, P_LRU // LRU_CB + q))
    w_spec = BS((2, 2, 128, 128), lambda b, q: (0, q, 0, 0))
    v_spec = BS((2, LRU_CB), lambda b, q: (0, q))
    c_spec = lambda r: BS((r, LRU_CB), lambda b, q: (0, q))
    s_spec = BS((None, 2, LRU_CB), lambda b, q: (b, 0, q))
    return p_spec, w_spec, v_spec, c_spec, s_spec


def _lru_fwd(proj, cw, cb, wa, ba, wi, bi, lam, h0, nb, t, period, blk0, need_y):
    nq = D // LRU_CB
    masks_of = _conv_taps(LRU_CB, period)

    def body(p_ref, cw_ref, cb_ref, wa_ref, ba_ref, wi_ref, bi_ref, lam_ref, h0_ref, *rest):
        if need_y:
            y_ref, hf_ref, hb_ref, fin_ref, sa0, sb0, sa1, sb1 = rest
        else:
            hf_ref, hb_ref, fin_ref, sa0, sb0, sa1, sb1 = rest
        u = _conv_fwd(p_ref[...], cw_ref[...], cb_ref[...], masks_of(t))
        for d, (sa, sb) in enumerate(((sa0, sb0), (sa1, sb1))):
            for j in range(2):
                sl = slice(128 * j, 128 * j + 128)
                a, bb = _lru_gate(u[:, sl], wa_ref[d, j], ba_ref[d:d + 1, sl], wi_ref[d, j], bi_ref[d:d + 1, sl],
                                  lam_ref[d:d + 1, sl])
                sa[:, sl] = a
                sb[:, sl] = bb
        lf, lb = _scan_pair((sa0, sb0, hf_ref, h0_ref[0:1, :]), (sa1, sb1, hb_ref, h0_ref[1:2, :]), t // 8, LRU_CB)
        fin_ref[0:1, :] = lf
        fin_ref[1:2, :] = lb
        if need_y:
            y_ref[...] = hf_ref[...] + hb_ref[...]

    p_spec, w_spec, v_spec, c_spec, s_spec = _lru_specs(t, blk0)
    o_spec = BS((t, LRU_CB), lambda b, q: (b, q))
    out_specs = [o_spec, o_spec, s_spec]
    out_shape = [S((nb * t, D), F32), S((nb * t, D), F32), S((nb, 2, D), F32)]
    if need_y:
        out_specs = [o_spec] + out_specs
        out_shape = [S((nb * t, D), F32)] + out_shape
    return pl.pallas_call(
        body, grid=(nb, nq), name="lru_fwd_lat" if need_y else "lru_fwd_ctx",
        in_specs=[p_spec, c_spec(4), c_spec(1), w_spec, v_spec, w_spec, v_spec, v_spec, s_spec],
        out_specs=out_specs, out_shape=out_shape,
        scratch_shapes=[pltpu.VMEM((t, LRU_CB), F32)] * 4, compiler_params=_params(),
    )(proj, cw, cb, wa, ba, wi, bi, lam, h0)


def _lru_bwd(proj, cw, cb, wa, ba, wi, bi, lam, h0, hf, hb, dy, dfin, dproj, nb, t, period, blk0, need_y, side=None):
    nq = D // LRU_CB
    rc = min(256, t)
    masks_of = _conv_taps(LRU_CB, period)

    def body(*refs):
        if side is None:
            return compute(*refs)
        own, exchange = side.split(refs, len(args), len(out_shape), 7)
        side.around(exchange, pl.program_id(0) * nq + pl.program_id(1), nb * nq, lambda: compute(*own))

    def compute(*refs):
        if need_y:
            (p_ref, cw_ref, cb_ref, wa_ref, ba_ref, wi_ref, bi_ref, lam_ref, h0_ref, hf_ref, hb_ref, dy_ref, dfin_ref, _,
             dp_ref, dh0_ref, dcw_ref, dcb_ref, dwa_ref, dwi_ref, dba_ref, dbi_ref, dlam_ref,
             su, sa0, sa1, sc0, sc1, sg0, sg1) = refs
        else:
            (p_ref, cw_ref, cb_ref, wa_ref, ba_ref, wi_ref, bi_ref, lam_ref, h0_ref, hf_ref, hb_ref, dfin_ref, _,
             dp_ref, dh0_ref, dcw_ref, dcb_ref, dwa_ref, dwi_ref, dba_ref, dbi_ref, dlam_ref,
             su, sa0, sa1, sc0, sc1, sg0, sg1) = refs
            dy_ref = None
        b, q = pl.program_id(0), pl.program_id(1)

        @pl.when(jnp.logical_and(b == 0, q == 0))
        def _():
            for r in (dcw_ref, dcb_ref, dwa_ref, dwi_ref, dba_ref, dbi_ref, dlam_ref):
                r[...] = jnp.zeros_like(r)

        masks = masks_of(t)
        u = _conv_fwd(p_ref[...], cw_ref[...], cb_ref[...], masks)
        su[...] = u
        for d, sa in enumerate((sa0, sa1)):
            for j in range(2):
                sl = slice(128 * j, 128 * j + 128)
                a, _unused = _lru_gate(u[:, sl], wa_ref[d, j], ba_ref[d:d + 1, sl], wi_ref[d, j], bi_ref[d:d + 1, sl],
                                       lam_ref[d:d + 1, sl])
                sa[:, sl] = a
        rowi = lax.broadcasted_iota(jnp.int32, (t, LRU_CB), 0)
        last, first = rowi == t - 1, rowi == 0
        sc0[...] = jnp.where(last, 0.0, pltpu.roll(sa0[...], t - 1, 0))
        sc1[...] = jnp.where(first, 0.0, pltpu.roll(sa1[...], 1, 0))
        g0 = jnp.where(last, dfin_ref[0:1, :], 0.0)
        g1 = jnp.where(first, dfin_ref[1:2, :], 0.0)
        if need_y:
            g0 = g0 + dy_ref[...]
            g1 = g1 + dy_ref[...]
        sg0[...] = g0
        sg1[...] = g1
        zero = jnp.zeros((1, LRU_CB), F32)
        _scan_pair((sc1, sg1, sg1, zero), (sc0, sg0, sg0, zero), t // 8, LRU_CB)
        dh0_ref[0:1, :] = sa0[0:1, :] * sg0[0:1, :]
        dh0_ref[1:2, :] = sa1[t - 1:t, :] * sg1[t - 1:t, :]
        sc0[...] = sg0[...] * jnp.where(first, h0_ref[0:1, :], pltpu.roll(hf_ref[...], 1, 0))
        sc1[...] = sg1[...] * jnp.where(last, h0_ref[1:2, :], pltpu.roll(hb_ref[...], t - 1, 0))

        def rows(ci, carry):
            r0 = pl.multiple_of(ci * rc, rc)
            for j in range(2):
                sl = slice(128 * j, 128 * j + 128)
                du = jnp.zeros((rc, 128), F32)
                for d, (sc, sg) in enumerate(((sc0, sg0), (sc1, sg1))):
                    _, vjp = jax.vjp(_lru_gate, su[pl.ds(r0, rc), sl], wa_ref[d, j], ba_ref[d:d + 1, sl], wi_ref[d, j],
                                     bi_ref[d:d + 1, sl], lam_ref[d:d + 1, sl])
                    du_d, dwa, dba, dwi, dbi, dlam = vjp((sc[pl.ds(r0, rc), sl], sg[pl.ds(r0, rc), sl]))
                    du = du + du_d
                    dwa_ref[d, 2 * q + j] += dwa
                    dwi_ref[d, 2 * q + j] += dwi
                    dba_ref[q, d:d + 1, sl] += dba
                    dbi_ref[q, d:d + 1, sl] += dbi
                    dlam_ref[q, d:d + 1, sl] += dlam
                sa0[pl.ds(r0, rc), sl] = du
            return carry

        lax.fori_loop(0, t // rc, rows, 0)
        draw, dcw, dcb = _conv_bwd(sa0[...], p_ref[...], cw_ref[...], masks)
        dp_ref[...] = draw.astype(BF16)
        dcw_ref[q] += dcw
        dcb_ref[q] += dcb

    p_spec, w_spec, v_spec, c_spec, s_spec = _lru_specs(t, blk0)
    o_spec = BS((t, LRU_CB), lambda b, q: (b, q))
    full = lambda shp: BS(shp, lambda b, q: (0,) * len(shp))
    in_specs = [p_spec, c_spec(4), c_spec(1), w_spec, v_spec, w_spec, v_spec, v_spec, s_spec, o_spec, o_spec]
    args = [proj, cw, cb, wa, ba, wi, bi, lam, h0, hf, hb]
    if need_y:
        in_specs.append(o_spec)
        args.append(dy)
    in_specs += [s_spec, BS(memory_space=pl.ANY)]
    args += [dfin, dproj]
    out_specs = [p_spec, s_spec, full((nq, 4, LRU_CB)), full((nq, 1, LRU_CB)), full((2, 8, 128, 128)),
                 full((2, 8, 128, 128)), full((nq, 2, LRU_CB)), full((nq, 2, LRU_CB)), full((nq, 2, LRU_CB))]
    out_shape = [S(dproj.shape, BF16), S((nb, 2, D), F32), S((nq, 4, LRU_CB), F32), S((nq, 1, LRU_CB), F32),
                 S((2, 8, 128, 128), F32), S((2, 8, 128, 128), F32), S((nq, 2, LRU_CB), F32), S((nq, 2, LRU_CB), F32),
                 S((nq, 2, LRU_CB), F32)]
    extra = side.arrays if side else []
    return pl.pallas_call(
        body, grid=(nb, nq), name="lru_bwd_lat" if need_y else "lru_bwd_ctx",
        in_specs=in_specs + [_HBM] * len(extra), out_specs=out_specs + [_HBM] * len(extra),
        out_shape=out_shape + (side.out_shapes if side else []), input_output_aliases={len(args) - 1: 0},
        scratch_shapes=[pltpu.VMEM((t, LRU_CB), F32)] * 7 + (side.sems if side else []), compiler_params=_params(),
    )(*args, *extra)


def _mix_core(y_ref, yl_ref, p_ref, nw_ref, bg_ref, wbs_ref, wbl_ref, wo_ref, nrm_s):
    for g in range(SSD_G):
        sl = slice(256 * g, 256 * g + 256)
        nrm_s[:, sl] = _grms(y_ref[:, sl], p_ref[:, sl], nw_ref[:, sl]).astype(BF16)
    br_s = jnp.dot(nrm_s[...], wbs_ref[...], preferred_element_type=F32)
    gl = (yl_ref[...] * _gelu(p_ref[:, 2048:3072])).astype(BF16)
    br_l = jnp.dot(gl, wbl_ref[...], preferred_element_type=F32)
    gs = _sigmoid(p_ref[:, 3072:4096] + bg_ref[:, 0:D])
    gr = _sigmoid(p_ref[:, 4096:5120] + bg_ref[:, D:2 * D])
    mix = (gs * br_s + gr * br_l).astype(BF16)
    xmix = jnp.dot(mix, wo_ref[...], preferred_element_type=F32)
    return br_s, gl, br_l, gs, gr, mix, xmix


def _mix_specs(rt, tiles_per_b):
    row = lambda w: BS((rt, w), lambda i: (i, 0))
    const = lambda shp: BS(shp, lambda i: (0,) * len(shp))
    gate = BS((None, None, 1, D), lambda i: (i // tiles_per_b, 2, 0, 0))
    return row, const, gate


def _mix_fwd(y, ylru, proj, x, m4, wbs, wbl, wo, nw, bg, l1g, l1b, rt, tiles_per_b):
    n = x.shape[0]

    def body(y_ref, yl_ref, p_ref, x_ref, g1_ref, wbs_ref, wbl_ref, wo_ref, nw_ref, bg_ref, lg_ref, lb_ref,
             x1_ref, nrm_ref, gl_ref, mix_ref, brs_ref, brl_ref, xm_ref):
        br_s, gl, br_l, _, _, mix, xmix = _mix_core(y_ref, yl_ref, p_ref, nw_ref, bg_ref, wbs_ref, wbl_ref, wo_ref, nrm_ref)
        gl_ref[...] = gl
        mix_ref[...] = mix
        brs_ref[...] = br_s
        brl_ref[...] = br_l
        xm_ref[...] = xmix
        x1_ref[...] = _resln(x_ref[...], xmix, g1_ref[...], lg_ref[...], lb_ref[...])

    row, const, gate = _mix_specs(rt, tiles_per_b)
    return pl.pallas_call(
        body, grid=(n // rt,), name="mix_fwd",
        in_specs=[row(SSD_INNER), row(D), BS((rt, 5120), lambda i: (i, 1)), row(D), gate,
                  const((SSD_INNER, D)), const((D, D)), const((D, D)), const((1, SSD_INNER)), const((1, 2 * D)),
                  const((1, D)), const((1, D))],
        out_specs=[row(D), row(SSD_INNER), row(D), row(D), row(D), row(D), row(D)],
        out_shape=[S((n, D), F32), S((n, SSD_INNER), BF16), S((n, D), BF16), S((n, D), BF16), S((n, D), F32),
                   S((n, D), F32), S((n, D), F32)],
        compiler_params=_params(),
    )(y, ylru, proj, x, m4, wbs, wbl, wo, nw, bg, l1g, l1b)


def _mix_bwd(y, ylru, proj, x, m4, wbs, wbl, wo, nw, bg, l1g, l1b, brs, brl, xmix, dx1, dproj, rt, tiles_per_b):
    n = x.shape[0]

    def body(y_ref, yl_ref, p_ref, x_ref, g1_ref, wbs_ref, wbl_ref, wo_ref, nw_ref, bg_ref, lg_ref, lb_ref,
             brs_ref, brl_ref, xm_ref, dx1_ref, _,
             dp_ref, dy_ref, dyl_ref, dxr_ref, dbrs_ref, dbrl_ref, dxm_ref,
             dg1_ref, dnw_ref, dbg_ref, dlg_ref, dlb_ref):
        i = pl.program_id(0)

        @pl.when(i == 0)
        def _():
            for r in (dnw_ref, dbg_ref, dlg_ref, dlb_ref):
                r[...] = jnp.zeros_like(r)

        @pl.when(i % tiles_per_b == 0)
        def _():
            dg1_ref[...] = jnp.zeros_like(dg1_ref)

        br_s, br_l = brs_ref[...], brl_ref[...]
        gs = _sigmoid(p_ref[:, 3072:4096] + bg_ref[:, 0:D])
        gr = _sigmoid(p_ref[:, 4096:5120] + bg_ref[:, D:2 * D])
        _, vjp = jax.vjp(_resln, x_ref[...], xm_ref[...], g1_ref[...], lg_ref[...], lb_ref[...])
        dxr, dxmix, dg1, dlg, dlb = vjp(dx1_ref[...])
        dxr_ref[...] = dxr
        dg1_ref[...] += dg1
        dlg_ref[...] += dlg
        dlb_ref[...] += dlb
        dxmb = dxmix.astype(BF16)
        dxm_ref[...] = dxmb
        dmix = lax.dot_general(dxmb, wo_ref[...], (((1,), (1,)), ((), ())), preferred_element_type=F32)
        dbrs = (dmix * gs).astype(BF16)
        dbrl = (dmix * gr).astype(BF16)
        dbrs_ref[...] = dbrs
        dbrl_ref[...] = dbrl
        dmg_s = dmix * br_s * gs * (1.0 - gs)
        dmg_r = dmix * br_l * gr * (1.0 - gr)
        dp_ref[:, 3072:4096] = dmg_s.astype(BF16)
        dp_ref[:, 4096:5120] = dmg_r.astype(BF16)
        dbg_ref[:, 0:D] += _colsum(dmg_s)
        dbg_ref[:, D:2 * D] += _colsum(dmg_r)
        dnrm = lax.dot_general(dbrs, wbs_ref[...], (((1,), (1,)), ((), ())), preferred_element_type=F32)
        for g in range(SSD_G):
            sl = slice(256 * g, 256 * g + 256)
            _, vjp = jax.vjp(_grms, y_ref[:, sl], p_ref[:, sl], nw_ref[:, sl])
            dyg, dzg, dnwg = vjp(dnrm[:, sl])
            dy_ref[:, sl] = dyg
            dp_ref[:, sl] = dzg.astype(BF16)
            dnw_ref[:, sl] += dnwg
        dgl = lax.dot_general(dbrl, wbl_ref[...], (((1,), (1,)), ((), ())), preferred_element_type=F32)
        _, vjp = jax.vjp(lambda a, c: a * _gelu(c), yl_ref[...], p_ref[:, 2048:3072])
        dyl, dlgate = vjp(dgl)
        dyl_ref[...] = dyl
        dp_ref[:, 2048:3072] = dlgate.astype(BF16)

    row, const, gate = _mix_specs(rt, tiles_per_b)
    pblk = BS((rt, 5120), lambda i: (i, 1))
    nb = n // (rt * tiles_per_b)
    out_specs = [pblk, row(SSD_INNER), row(D), row(D), row(D), row(D), row(D),
                 BS((None, 1, D), lambda i: (i // tiles_per_b, 0, 0)), const((1, SSD_INNER)), const((1, 2 * D)),
                 const((1, D)), const((1, D))]
    out_shape = [S(dproj.shape, BF16), S((n, SSD_INNER), F32), S((n, D), F32), S((n, D), F32),
                 S((n, D), BF16), S((n, D), BF16), S((n, D), BF16),
                 S((nb, 1, D), F32), S((1, SSD_INNER), F32), S((1, 2 * D), F32), S((1, D), F32), S((1, D), F32)]
    return pl.pallas_call(
        body, grid=(n // rt,), name="mix_bwd",
        in_specs=[row(SSD_INNER), row(D), pblk, row(D), gate,
                  const((SSD_INNER, D)), const((D, D)), const((D, D)), const((1, SSD_INNER)), const((1, 2 * D)),
                  const((1, D)), const((1, D)), row(D), row(D), row(D), row(D), BS(memory_space=pl.ANY)],
        out_specs=out_specs, out_shape=out_shape, input_output_aliases={16: 0},
        compiler_params=_params(),
    )(y, ylru, proj, x, m4, wbs, wbl, wo, nw, bg, l1g, l1b, brs, brl, xmix, dx1, dproj)


def _mlp_step(x1, tgt, m4, w1, b1, w2, b2, l2g, l2b, rt, tiles_per_b):
    n = x1.shape[0]

    def body(x_ref, t_ref, sh_ref, sc_ref, gt_ref, w1_hbm, b1_ref, w2_hbm, b2_ref, lg_ref, lb_ref,
             loss_ref, dx_ref, h2_ref, da1_ref, r2_ref, dmlp_ref, dm_ref, db1_ref, db2_ref, dlg_ref, dlb_ref,
             w1_vm, w2_vm, sem):
        i = pl.program_id(0)

        @pl.when(i == 0)
        def _():
            c1 = pltpu.make_async_copy(w1_hbm, w1_vm, sem.at[0])
            c2 = pltpu.make_async_copy(w2_hbm, w2_vm, sem.at[1])
            c1.start()
            c2.start()
            for r in (loss_ref, db1_ref, db2_ref, dlg_ref, dlb_ref):
                r[...] = jnp.zeros_like(r)
            c1.wait()
            c2.wait()

        @pl.when(i % tiles_per_b == 0)
        def _():
            dm_ref[...] = jnp.zeros_like(dm_ref)

        x1v = x_ref[...]
        h2, vjp_h = jax.vjp(_modln, x1v, sh_ref[...], sc_ref[...])
        h2b = h2.astype(BF16)
        h2_ref[...] = h2b
        r = jnp.maximum(jnp.dot(h2b, w1_vm[...], preferred_element_type=F32) + b1_ref[...], 0.0)
        r2b = (r * r).astype(BF16)
        r2_ref[...] = r2b
        mlp = jnp.dot(r2b, w2_vm[...], preferred_element_type=F32) + b2_ref[...]
        x2, vjp_r = jax.vjp(_resln, x1v, mlp, gt_ref[...], lg_ref[...], lb_ref[...])
        diff = x2 - t_ref[...]
        loss_ref[...] += (0.5 / D) * jnp.sum(diff * diff)
        dxa, dmlp, dgt, dlg, dlb = vjp_r(diff * (1.0 / D))
        dlg_ref[...] += dlg
        dlb_ref[...] += dlb
        dm_ref[2:3, :] += dgt
        db2_ref[...] += _colsum(dmlp)
        dmlpb = dmlp.astype(BF16)
        dmlp_ref[...] = dmlpb
        da1 = lax.dot_general(dmlpb, w2_vm[...], (((1,), (1,)), ((), ())), preferred_element_type=F32) * (2.0 * r)
        db1_ref[...] += _colsum(da1)
        da1b = da1.astype(BF16)
        da1_ref[...] = da1b
        dh2 = lax.dot_general(da1b, w1_vm[...], (((1,), (1,)), ((), ())), preferred_element_type=F32)
        dxb, dsh, dsc = vjp_h(dh2)
        dx_ref[...] = dxa + dxb
        dm_ref[0:1, :] += dsh
        dm_ref[1:2, :] += dsc

    row = lambda w: BS((rt, w), lambda i: (i, 0))
    const = lambda shp: BS(shp, lambda i: (0,) * len(shp))
    mod = lambda k: BS((None, None, 1, D), lambda i: (i // tiles_per_b, k, 0, 0))
    nb = n // (rt * tiles_per_b)
    anyspec = BS(memory_space=pl.ANY)
    return pl.pallas_call(
        body, grid=(n // rt,), name="mlp_step",
        in_specs=[row(D), row(D), mod(3), mod(4), mod(5), anyspec, const((1, MLP_H)), anyspec, const((1, D)),
                  const((1, D)), const((1, D))],
        out_specs=[const((8, 128)), row(D), row(D), row(MLP_H), row(MLP_H), row(D),
                   BS((None, 3, D), lambda i: (i // tiles_per_b, 0, 0)), const((1, MLP_H)), const((1, D)),
                   const((1, D)), const((1, D))],
        out_shape=[S((8, 128), F32), S((n, D), F32), S((n, D), BF16), S((n, MLP_H), BF16), S((n, MLP_H), BF16),
                   S((n, D), BF16), S((nb, 3, D), F32), S((1, MLP_H), F32), S((1, D), F32), S((1, D), F32),
                   S((1, D), F32)],
        scratch_shapes=[pltpu.VMEM((D, MLP_H), BF16), pltpu.VMEM((MLP_H, D), BF16), pltpu.SemaphoreType.DMA((2,))],
        compiler_params=_params(),
    )(x1, tgt, m4, m4, m4, w1, b1, w2, b2, l2g, l2b)


def _pack_win(w):
    parts = []
    for g in range(SSD_G):
        parts += [w[:, 256 * g:256 * g + 256], w[:, 2048 + 128 * g:2176 + 128 * g], w[:, 4160 + 128 * g:4288 + 128 * g]]
    parts += [w[:, 3136:4160], w[:, 5184:7232], w[:, 7232:8256], w[:, 8256:10304], w[:, 3072:3136],
              jnp.zeros((w.shape[0], P_W - P_DT - 64), w.dtype)]
    return jnp.concatenate(parts, axis=1)


def _unpack_win(p):
    xs = [p[:, 512 * g:512 * g + 256] for g in range(SSD_G)]
    bs = [p[:, 512 * g + 256:512 * g + 384] for g in range(SSD_G)]
    cs = [p[:, 512 * g + 384:512 * g + 512] for g in range(SSD_G)]
    return jnp.concatenate(xs + bs + [p[:, P_DT:P_DT + 64], p[:, P_LRU:P_Z]] + cs + [p[:, P_Z:P_DT]], axis=1)


def _pack_conv(w):
    return jnp.stack([jnp.concatenate([w[:, 256 * g:256 * g + 256], w[:, 2048 + 128 * g:2176 + 128 * g],
                                       w[:, 3072 + 128 * g:3200 + 128 * g]], axis=1) for g in range(SSD_G)])


def _unpack_conv(p):
    r = p.shape[1]
    x = jnp.transpose(p[:, :, 0:256], (1, 0, 2)).reshape(r, 2048)
    b = jnp.transpose(p[:, :, 256:384], (1, 0, 2)).reshape(r, 1024)
    c = jnp.transpose(p[:, :, 384:512], (1, 0, 2)).reshape(r, 1024)
    return jnp.concatenate([x, b, c], axis=1)


def _pack_heads(v):
    p = jnp.transpose(v.reshape(2, SSD_G, 4), (1, 0, 2)).reshape(SSD_G, 1, 8)
    return jnp.pad(p, ((0, 0), (0, 0), (0, 120)))


def _unpack_heads(p):
    return jnp.transpose(p[:, 0, 0:8].reshape(SSD_G, 2, 4), (1, 0, 2)).reshape(2, 32)


def _pack_dt(dt):
    n = dt.shape[0]
    p = jnp.transpose(dt.reshape(n, 2, SSD_G, 4), (2, 0, 1, 3)).reshape(SSD_G, n, 8)
    return jnp.pad(p, ((0, 0), (0, 0), (0, 120)))


def _unpack_dt(p):
    n = p.shape[1]
    return jnp.transpose(p[:, :, 0:8].reshape(SSD_G, n, 2, 4), (1, 2, 0, 3)).reshape(n, 64)


def _tk(rows):
    return 512 if rows % 512 == 0 else (256 if rows % 256 == 0 else 128)


LATE = ["w_br_ssd", "w_br_lru", "w_out", "w_mlp1", "w_mlp2"]


def _local_step(x, c, ctx, tgt, sm, wmod, win, late, late_are_shards=False, reducer=None):
    nb, t, _ = x.shape
    tc = ctx.shape[1]
    nl, ncx = nb * t, nb * tc
    rt = 256 if tc % 256 == 0 else 128
    rtm = 128
    xl, xc = x.reshape(nl, D), ctx.reshape(ncx, D)
    tgt2 = tgt.reshape(nl, D)
    cc = jnp.zeros((8, D), F32).at[0:nb].set(c).at[nb].set(sm["c_ctx"])
    m = _mod_fwd(cc, wmod, sm["b_mod"])
    m4 = m.reshape(8, N_MOD, 1, D)
    proj, h1 = _inproj_fwd(xl, m4, win, rtm, nl // rtm, t // rtm, nb, "inproj_fwd_lat")
    proj_c, h1_c = _inproj_fwd(xc, m4, win, rtm, 0, 1, nb, "inproj_fwd_ctx")

    cw_s, cb_s = _pack_conv(sm["ssd_conv_w"]), _pack_conv(sm["ssd_conv_b"])
    dtb, alog = _pack_heads(sm["ssd_dt_bias"]), _pack_heads(sm["ssd_a_log"])
    drow = jnp.repeat(sm["ssd_d"].reshape(32), 64).reshape(SSD_G, 1, 256)
    dtg, dtg_c = _pack_dt(proj[:, P_DT:P_DT + 64]), _pack_dt(proj_c[:, P_DT:P_DT + 64])
    zst = jnp.zeros((nb, SSD_G, SSD_N, 256), F32)
    zl = jnp.zeros((nb, 2, D), F32)
    ssd_p = (cw_s, cb_s, dtb, alog, drow)
    lru_p = (sm["lru_conv_w"], sm["lru_conv_b"], sm["lru_wa"], sm["lru_ba"], sm["lru_wi"], sm["lru_bi"], sm["lru_lambda"])

    chsf, chsb, csf, csb = _ssd_fwd(proj_c, dtg_c, *ssd_p, zst, zst, nb, tc, tc, 0, False)
    y, lhsf, lhsb, _, _, *got = _ssd_fwd(proj, dtg, *ssd_p, csf, csb, nb, t, GRID_W, 0, True,
                                         tuple(late) if late_are_shards else ())
    wbs, wbl, wo, w1, w2 = [_full_from_chips(g, n) for g, n in zip(got, LATE)] if late_are_shards else late
    chf, chb, cfin = _lru_fwd(proj_c, *lru_p, zl, nb, tc, tc, 0, False)
    ylru, lhf, lhb, _ = _lru_fwd(proj, *lru_p, cfin, nb, t, GRID_W, 0, True)
    mix_w = (wbs, wbl, wo, sm["ssd_norm_w"], sm["b_gate"], sm["ln1_g"], sm["ln1_b"])
    x1, nrm, gl, mixb, brs, brl, xmix = _mix_fwd(y, ylru, proj, xl, m4, *mix_w, rtm, t // rtm)
    (loss, dx1, h2, da1, r2, dmlp, dm2, db1, db2, dl2g, dl2b) = _mlp_step(
        x1, tgt2, m4, w1, sm["b_mlp1"], w2, sm["b_mlp2"], sm["ln2_g"], sm["ln2_b"], rt, t // rt)

    dproj = lax.empty((nl, P_W), BF16)
    dproj_c = jnp.zeros((ncx, P_W), BF16)
    (dproj, dy, dylru, dxres, dbrs, dbrl, dxm, dg1, dnw, dbg, dl1g, dl1b) = _mix_bwd(
        y, ylru, proj, xl, m4, *mix_w, brs, brl, xmix, dx1, dproj, rtm, t // rtm)
    big = {
        "w_br_ssd": _matmul_tn(nrm, dbrs, D, D, _tk(nl), "dw_br_ssd"),
        "w_br_lru": _matmul_tn(gl, dbrl, D, D, _tk(nl), "dw_br_lru"),
        "w_out": _matmul_tn(mixb, dxm, D, D, _tk(nl), "dw_out"),
        "w_mlp1": _matmul_tn(h2, da1, D, D, _tk(nl), "dw_mlp1"),
        "w_mlp2": _matmul_tn(r2, dmlp, D, D, _tk(nl), "dw_mlp2"),
    }
    (dproj, ddt_l, dh0f, dh0b, dcw_l, dcb_l, ddtb_l, dal_l, dd) = _ssd_bwd(
        proj, dtg, *ssd_p, lhsf, lhsb, dy, zst, zst, dproj, nb, t, GRID_W, 0, True)
    (dproj_c, ddt_c, _, _, dcw_c, dcb_c, ddtb_c, dal_c, _) = _ssd_bwd(
        proj_c, dtg_c, *ssd_p, chsf, chsb, None, dh0f, dh0b, dproj_c, nb, tc, tc, 0, False)
    side = reducer.begin(big, list(big)) if reducer else None
    (dproj, dlh0, gcw_l, gcb_l, gwa_l, gwi_l, gba_l, gbi_l, glam_l, *got) = _lru_bwd(
        proj, *lru_p, cfin, lhf, lhb, dylru, zl, dproj, nb, t, GRID_W, 0, True, side)
    if reducer:
        reducer.end(got)
    (dproj_c, _, gcw_c, gcb_c, gwa_c, gwi_c, gba_c, gbi_c, glam_c) = _lru_bwd(
        proj_c, *lru_p, zl, chf, chb, None, dlh0, dproj_c, nb, tc, tc, 0, False)
    pad_dt = lambda d: jnp.pad(_unpack_dt(d).astype(BF16), ((0, 0), (0, P_W - P_DT - 64)))
    dproj = lax.dynamic_update_slice(dproj, pad_dt(ddt_l), (0, P_DT))
    dproj_c = lax.dynamic_update_slice(dproj_c, pad_dt(ddt_c), (0, P_DT))

    big["w_in"] = _matmul_tn2(h1, dproj, h1_c, dproj_c, D, 1152, min(_tk(nl), _tk(ncx)), "dw_in")
    side = reducer.begin(big, ["w_in"]) if reducer else None
    gx, dm1, *got = _inproj_bwd(xl, m4, win, dproj, rt, 0, nl // rt, t // rt, nb, True, dxres, side)
    if reducer:
        reducer.end(got)
    (dmc,) = _inproj_bwd(xc, m4, win, dproj_c, rt, 0, ncx // rt, ncx // rt, nb, False, None)
    dm = jnp.zeros((8, N_MOD, D), F32)
    dm = dm.at[0:nb].set(jnp.concatenate([dm1, dg1, dm2], axis=1)).at[nb, 0:2].set(dmc[0])
    dwmod, dbmod, dcc = _mod_bwd(cc, wmod, dm.reshape(8, N_MOD * D))
    big["w_mod"] = dwmod
    nq = D // LRU_CB
    small = {
        "c_ctx": dcc[nb],
        "b_mod": dbmod,
        "b_gate": dbg,
        "ssd_conv_w": _unpack_conv(dcw_l + dcw_c),
        "ssd_conv_b": _unpack_conv(dcb_l + dcb_c),
        "ssd_dt_bias": _unpack_heads(ddtb_l + ddtb_c),
        "ssd_a_log": _unpack_heads(dal_l + dal_c),
        "ssd_d": jnp.sum(dd.reshape(32, 64), axis=1),
        "ssd_norm_w": dnw,
        "lru_conv_w": jnp.transpose(gcw_l + gcw_c, (1, 0, 2)).reshape(4, D),
        "lru_conv_b": (gcb_l + gcb_c).reshape(1, D),
        "lru_wa": gwa_l + gwa_c,
        "lru_ba": jnp.transpose(gba_l + gba_c, (1, 0, 2)).reshape(2, D),
        "lru_wi": gwi_l + gwi_c,
        "lru_bi": jnp.transpose(gbi_l + gbi_c, (1, 0, 2)).reshape(2, D),
        "lru_lambda": jnp.transpose(glam_l + glam_c, (1, 0, 2)).reshape(2, D),
        "ln1_g": dl1g, "ln1_b": dl1b, "b_mlp1": db1, "b_mlp2": db2, "ln2_g": dl2g, "ln2_b": dl2b,
    }
    return loss[0, 0], gx.reshape(nb, t, D), big, small


_HBM = BS(memory_space=pl.ANY)


def _place():
    return lax.axis_index("x"), lax.axis_index("y"), lax.axis_index("c")


def _other_chips(x, y):
    return [(1 - x, y), (x, 1 - y), (1 - x, 1 - y)]


def _gather_chips(arrs):
    n = len(arrs)

    def body(*refs):
        ex = _GatherExchange(refs[:n], refs[n:2 * n], refs[2 * n:])
        ex.begin()
        ex.finish()

    return pl.pallas_call(
        body, name="gather_weights", in_specs=[_HBM] * n, out_specs=[_HBM] * n,
        out_shape=_gather_out_shapes(arrs), scratch_shapes=_gather_sems(n),
    )(*arrs)


def _gather_out_shapes(arrs):
    return [S((4,) + a.shape, a.dtype) for a in arrs]


def _gather_sems(n):
    return [pltpu.SemaphoreType.DMA((3 * n,))] * 4 + [pltpu.SemaphoreType.DMA((n,))]


class _GatherExchange:
    def __init__(self, ins, outs, sems):
        self.ins, self.outs = ins, outs
        self.ici_send, self.ici_recv, self.d2d_send, self.d2d_recv, self.loc_sems = sems
        self.x, self.y, self.c = _place()
        self.me = 2 * self.x + self.y
        self.chips = _other_chips(self.x, self.y)

    def _half(self, a, which):
        hr = self.ins[a].shape[0] // 2
        return pl.ds(pl.multiple_of((self.c if which == 0 else 1 - self.c) * hr, 8), hr)

    def _local(self, a):
        return pltpu.make_async_copy(self.ins[a], self.outs[a].at[self.me], self.loc_sems.at[a])

    def _ici(self, a, k, slot):
        px, py = self.chips[k]
        mine = self._half(a, 0)
        return pltpu.make_async_remote_copy(src_ref=self.ins[a].at[mine], dst_ref=self.outs[a].at[slot, mine],
                                            send_sem=self.ici_send.at[3 * a + k], recv_sem=self.ici_recv.at[3 * a + k],
                                            device_id=(px, py, self.c), device_id_type=MESH)

    def _d2d(self, a, k, which):
        px, py = self.chips[k]
        rows = self.outs[a].at[2 * px + py, self._half(a, which)]
        return pltpu.make_async_remote_copy(src_ref=rows, dst_ref=rows, send_sem=self.d2d_send.at[3 * a + k],
                                            recv_sem=self.d2d_recv.at[3 * a + k],
                                            device_id=(self.x, self.y, 1 - self.c), device_id_type=MESH)

    def begin(self):
        for a in range(len(self.ins)):
            self._local(a).start()
            for k in range(3):
                self._ici(a, k, self.me).start()

    def finish(self):
        n = len(self.ins)
        for a in range(n):
            for k, (px, py) in enumerate(self.chips):
                self._ici(a, k, 2 * px + py).wait_recv()
                self._d2d(a, k, 0).start()
        for a in range(n):
            for k in range(3):
                self._d2d(a, k, 1).wait_recv()
        for a in range(n):
            self._local(a).wait()
            for k in range(3):
                self._ici(a, k, self.me).wait_send()
                self._d2d(a, k, 0).wait_send()


def _scatter_chips(arrs):
    side = _Side("scatter", arrs)
    n = len(arrs)

    def body(*refs):
        ex = side.make(refs[:n], refs[n:2 * n], refs[2 * n:])
        ex.begin()
        ex.finish()

    return pl.pallas_call(
        body, name="scatter_grads", in_specs=[_HBM] * n, out_specs=[_HBM] * n,
        out_shape=side.out_shapes, scratch_shapes=side.sems,
    )(*arrs)


class _ScatterExchange:
    def __init__(self, ins, outs, sems):
        self.ins, self.outs = ins, outs
        self.send_sems, self.recv_sems = sems
        x, y, self.c = _place()
        self.chips = _other_chips(x, y)

    def _copy(self, a, k):
        px, py = self.chips[k]
        return pltpu.make_async_remote_copy(src_ref=self.ins[a].at[2 * px + py], dst_ref=self.outs[a].at[k],
                                            send_sem=self.send_sems.at[3 * a + k], recv_sem=self.recv_sems.at[3 * a + k],
                                            device_id=(px, py, self.c), device_id_type=MESH)

    def begin(self):
        for a in range(len(self.ins)):
            for k in range(3):
                self._copy(a, k).start()

    def finish(self):
        for a in range(len(self.ins)):
            for k in range(3):
                self._copy(a, k).wait_recv()
        for a in range(len(self.ins)):
            for k in range(3):
                self._copy(a, k).wait_send()


class _Side:
    def __init__(self, kind, arrays):
        self.arrays = list(arrays)
        n = len(self.arrays)
        if kind == "gather":
            self.out_shapes, self.sems, self.make = _gather_out_shapes(self.arrays), _gather_sems(n), _GatherExchange
        else:
            self.out_shapes = [S((3,) + a.shape[1:], a.dtype) for a in self.arrays]
            self.sems = [pltpu.SemaphoreType.DMA((3 * n,))] * 2
            self.make = _ScatterExchange

    def split(self, refs, n_in, n_out, n_scr):
        a, b = len(self.arrays), len(self.out_shapes)
        i1 = n_in + a
        o1 = i1 + n_out
        o2 = o1 + b
        s1 = o2 + n_scr
        own = tuple(refs[:n_in]) + tuple(refs[i1:o1]) + tuple(refs[o2:s1])
        return own, self.make(refs[n_in:i1], refs[o1:o2], refs[s1:])

    def around(self, exchange, step, n_steps, compute):
        pl.when(step == 0)(exchange.begin)
        compute()
        pl.when(step == n_steps - 1)(exchange.finish)


def _swap_halves(arrs, name):
    n = len(arrs)

    def body(*refs):
        ins, outs = refs[:n], refs[n:2 * n]
        send_sems, recv_sems = refs[2 * n:]
        x, y, c = _place()
        sends = []
        for a in range(n):
            hr = arrs[a].shape[1] // 2
            theirs = pl.ds(pl.multiple_of((1 - c) * hr, 8), hr)
            for q in range(4):
                cp = pltpu.make_async_remote_copy(src_ref=ins[a].at[q, theirs], dst_ref=outs[a].at[q],
                                                  send_sem=send_sems.at[4 * a + q], recv_sem=recv_sems.at[4 * a + q],
                                                  device_id=(x, y, 1 - c), device_id_type=MESH)
                cp.start()
                sends.append(cp)
        for cp in sends:
            cp.wait_recv()
        for cp in sends:
            cp.wait_send()

    return pl.pallas_call(
        body, name=name, in_specs=[_HBM] * n, out_specs=[_HBM] * n,
        out_shape=[S((4, a.shape[1] // 2, a.shape[2]), a.dtype) for a in arrs],
        scratch_shapes=[pltpu.SemaphoreType.DMA((4 * n,)), pltpu.SemaphoreType.DMA((4 * n,))],
    )(*arrs)


def _allreduce_small(vs):
    n = len(vs)

    def body(*refs):
        ins, outs, bufs = refs[:n], refs[n:2 * n], refs[2 * n:5 * n]
        send_sems, recv_sems = refs[5 * n:]
        x, y, c = _place()
        srcs = list(ins)
        for s, peer in enumerate(((x, y, 1 - c), (x, 1 - y, c), (1 - x, y, c))):
            copies = [pltpu.make_async_remote_copy(src_ref=srcs[a], dst_ref=bufs[3 * a + s], send_sem=send_sems.at[3 * a + s],
                                                   recv_sem=recv_sems.at[3 * a + s], device_id=peer, device_id_type=MESH)
                      for a in range(n)]
            for cp in copies:
                cp.start()
            for a, cp in enumerate(copies):
                cp.wait()
                outs[a][...] = srcs[a][...] + bufs[3 * a + s][...]
            srcs = list(outs)

    vm = BS(memory_space=pltpu.VMEM)
    return pl.pallas_call(
        body, name="allreduce_small", in_specs=[vm] * n, out_specs=[vm] * n, out_shape=[S(v.shape, F32) for v in vs],
        scratch_shapes=[pltpu.VMEM(v.shape, F32) for v in vs for _ in range(3)]
        + [pltpu.SemaphoreType.DMA((3 * n,)), pltpu.SemaphoreType.DMA((3 * n,))],
        compiler_params=_params(),
    )(*vs)


def _swap_cores(arrs):
    n = len(arrs)

    def body(*refs):
        ins, outs = refs[:n], refs[n:2 * n]
        send_sems, recv_sems = refs[2 * n:]
        x, y, c = _place()
        sends = []
        for a in range(n):
            cp = pltpu.make_async_remote_copy(src_ref=ins[a], dst_ref=outs[a], send_sem=send_sems.at[a],
                                              recv_sem=recv_sems.at[a], device_id=(x, y, 1 - c), device_id_type=MESH)
            cp.start()
            sends.append(cp)
        for cp in sends:
            cp.wait_recv()
        for cp in sends:
            cp.wait_send()

    return pl.pallas_call(
        body, name="swap_cores", in_specs=[_HBM] * n, out_specs=[_HBM] * n,
        out_shape=[S(a.shape, a.dtype) for a in arrs],
        scratch_shapes=[pltpu.SemaphoreType.DMA((n,)), pltpu.SemaphoreType.DMA((n,))],
    )(*arrs)


def _row_tile(r, c=128):
    tr = 256 if c <= 1024 else (128 if c <= 2048 else 64)
    return tr if r % tr == 0 else r


def _sum_half(own, sib, core, name):
    _, r, c = own.shape
    hr = r // 2
    tr = _row_tile(hr, c)
    nbk = hr // tr

    def body(core_ref, o_ref, s_ref, p_ref, pb_ref):
        p = o_ref[...] + s_ref[...]
        p_ref[...] = p
        pb_ref[...] = p.astype(BF16)

    blk = BS((None, tr, c), lambda q, i, cr: (q, i, 0))
    return pl.pallas_call(
        body, name=name, out_shape=[S((4, hr, c), F32), S((4, hr, c), BF16)],
        grid_spec=pltpu.PrefetchScalarGridSpec(
            num_scalar_prefetch=1, grid=(4, nbk),
            in_specs=[BS((None, tr, c), lambda q, i, cr: (q, cr[0] * nbk + i, 0)), blk], out_specs=[blk, blk]),
        compiler_params=_params(),
    )(core, own, sib)


def _sum4(part, recv, chip, name):
    _, r, c = part.shape
    tr = _row_tile(r, c)

    def body(chip_ref, o_ref, r_ref, out_ref):
        acc = o_ref[...]
        for k in range(3):
            acc = acc + r_ref[k].astype(F32)
        out_ref[...] = acc

    return pl.pallas_call(
        body, name=name, out_shape=S((r, c), F32),
        grid_spec=pltpu.PrefetchScalarGridSpec(
            num_scalar_prefetch=1, grid=(r // tr,),
            in_specs=[BS((None, tr, c), lambda i, ch: (ch[0], i, 0)), BS((3, tr, c), lambda i, ch: (0, i, 0))],
            out_specs=BS((tr, c), lambda i, ch: (i, 0))),
        compiler_params=_params(),
    )(chip, part, recv)


def _adam_math(w, g, m, v):
    m = ADAM_B1 * m + (1.0 - ADAM_B1) * g
    v = ADAM_B2 * v + (1.0 - ADAM_B2) * (g * g)
    m_hat = m / (1.0 - ADAM_B1 ** ADAM_STEP)
    v_hat = v / (1.0 - ADAM_B2 ** ADAM_STEP)
    return -ADAM_LR * (m_hat / (jnp.sqrt(v_hat) + ADAM_EPS) + ADAM_WD * w), m, v


def _adam_halves(mine, other, w, m, v, core, name):
    r, c = w.shape
    tr = _row_tile(r // 2, c)
    nbk = (r // 2) // tr

    def body(core_ref, a_ref, b_ref, w_ref, m_ref, v_ref, g_ref, d_ref, nm_ref, nv_ref):
        g = jnp.where(pl.program_id(0) // nbk == core_ref[0], a_ref[...], b_ref[...])
        g_ref[...] = g
        d_ref[...], nm_ref[...], nv_ref[...] = _adam_math(w_ref[...], g, m_ref[...], v_ref[...])

    spec = BS((tr, c), lambda i, cr: (i, 0))
    half = BS((tr, c), lambda i, cr: (i % nbk, 0))
    return pl.pallas_call(
        body, name=name, out_shape=[S((r, c), F32)] * 4,
        grid_spec=pltpu.PrefetchScalarGridSpec(num_scalar_prefetch=1, grid=(r // tr,), in_specs=[half, half] + [spec] * 3,
                                               out_specs=[spec] * 4),
        compiler_params=_params(),
    )(core, mine, other, w, m, v)


def _adam_flat(g, w, m, v, name):
    r = w.shape[0]
    tr = _row_tile(r)

    def body(g_ref, w_ref, m_ref, v_ref, d_ref, nm_ref, nv_ref):
        d_ref[...], nm_ref[...], nv_ref[...] = _adam_math(w_ref[...], g_ref[...], m_ref[...], v_ref[...])

    spec = BS((tr, 128), lambda i: (i, 0))
    return pl.pallas_call(
        body, grid=(r // tr,), name=name, in_specs=[spec] * 4, out_specs=[spec] * 3,
        out_shape=[S((r, 128), F32)] * 3, compiler_params=_params(),
    )(g, w, m, v)


def _flatten(arrs, rows_mult=256):
    flat = jnp.concatenate([a.reshape(-1) for a in arrs])
    n = flat.shape[0]
    rows = -(-n // 128)
    rows = -(-rows // rows_mult) * rows_mult
    return jnp.pad(flat, (0, rows * 128 - n)).reshape(rows, 128)


def _unflatten(flat, shapes):
    flat = flat.reshape(-1)
    out, o = [], 0
    for shp in shapes:
        n = int(np.prod(shp))
        out.append(flat[o:o + n].reshape(shp))
        o += n
    return out


BIG = ["w_mod", "w_in", "w_br_ssd", "w_br_lru", "w_out", "w_mlp1", "w_mlp2"]
COL_SHARDED = {"w_mod": N_MOD * D, "w_in": IN_COLS, "w_mlp1": MLP_H}
SMALL_SHARDED = ["ssd_conv_w", "lru_conv_w", "lru_ba", "lru_bi", "lru_lambda"]
WEIGHTS = ['c_ctx', 'w_mod', 'b_mod', 'w_in', 'b_gate', 'ssd_conv_w', 'ssd_conv_b', 'ssd_dt_bias', 'ssd_a_log', 'ssd_d',
           'ssd_norm_w', 'lru_conv_w', 'lru_conv_b', 'lru_wa', 'lru_ba', 'lru_wi', 'lru_bi', 'lru_lambda', 'w_br_ssd',
           'w_br_lru', 'w_out', 'ln1_g', 'ln1_b', 'w_mlp1', 'b_mlp1', 'w_mlp2', 'b_mlp2', 'ln2_g', 'ln2_b']
SMALL = [n for n in WEIGHTS if n not in BIG]
GATE_STACKS = ["lru_wa", "lru_wi"]


class _Reducer:
    def __init__(self, core_id, chip_id):
        self.core_id, self.chip_id = core_id, chip_id
        self.halves, self.recv, self.pending = {}, {}, []

    def _chip_sums(self, big, names):
        slabs = [_chips_from_full(_unpack_win(big[n]) if n == "w_in" else big[n], n) for n in names]
        for n, s, o in zip(names, slabs, _swap_halves(slabs, "swap_halves_" + names[0])):
            self.halves[n] = _sum_half(s, o, self.core_id, "half_" + n)
        return [self.halves[n][1] for n in names]

    def begin(self, big, names):
        self.pending = list(names)
        return _Side("scatter", self._chip_sums(big, names))

    def end(self, received):
        self.recv.update(zip(self.pending, received))

    def finish(self, big, names):
        self.recv.update(zip(names, _scatter_chips(self._chip_sums(big, names))))
        mine = [_sum4(self.halves[n][0], self.recv[n], self.chip_id, "sum_" + n) for n in BIG]
        return dict(zip(BIG, mine)), dict(zip(BIG, _swap_cores(mine)))


def _full_from_chips(g4, name):
    if name in COL_SHARDED:
        return jnp.transpose(g4, (1, 0, 2)).reshape(g4.shape[1], 4 * g4.shape[2])
    return g4.reshape(4 * g4.shape[1], g4.shape[2])


def _chips_from_full(full, name):
    if name in COL_SHARDED:
        r, c = full.shape
        return jnp.transpose(full.reshape(r, 4, c // 4), (1, 0, 2))
    return full.reshape(4, full.shape[0] // 4, full.shape[1])


def kernel(x, c, ctx, c_ctx, w_mod, b_mod, w_in, b_gate, ssd_conv_w, ssd_conv_b, ssd_dt_bias, ssd_a_log, ssd_d, ssd_norm_w, lru_conv_w, lru_conv_b, lru_wa, lru_ba, lru_wi, lru_bi, lru_lambda, w_br_ssd, w_br_lru, w_out, ln1_g, ln1_b, w_mlp1, b_mlp1, w_mlp2, b_mlp2, ln2_g, ln2_b, loss_target, m_c_ctx, m_w_mod, m_b_mod, m_w_in, m_b_gate, m_ssd_conv_w, m_ssd_conv_b, m_ssd_dt_bias, m_ssd_a_log, m_ssd_d, m_ssd_norm_w, m_lru_conv_w, m_lru_conv_b, m_lru_wa, m_lru_ba, m_lru_wi, m_lru_bi, m_lru_lambda, m_w_br_ssd, m_w_br_lru, m_w_out, m_ln1_g, m_ln1_b, m_w_mlp1, m_b_mlp1, m_w_mlp2, m_b_mlp2, m_ln2_g, m_ln2_b, v_c_ctx, v_w_mod, v_b_mod, v_w_in, v_b_gate, v_ssd_conv_w, v_ssd_conv_b, v_ssd_dt_bias, v_ssd_a_log, v_ssd_d, v_ssd_norm_w, v_lru_conv_w, v_lru_conv_b, v_lru_wa, v_lru_ba, v_lru_wi, v_lru_bi, v_lru_lambda, v_w_br_ssd, v_w_br_lru, v_w_out, v_ln1_g, v_ln1_b, v_w_mlp1, v_b_mlp1, v_w_mlp2, v_b_mlp2, v_ln2_g, v_ln2_b):
    given = dict(locals())
    w = {n: given[n] for n in WEIGHTS}
    mom = {n: given["m_" + n] for n in WEIGHTS}
    var = {n: given["v_" + n] for n in WEIGHTS}
    chip = 2 * lax.axis_index("x") + lax.axis_index("y")

    shard2d = {n: w[n].reshape(w[n].shape[-2:]) for n in BIG}
    small_pack = _flatten([w[n] for n in SMALL_SHARDED], rows_mult=16)
    first = ["w_mod", "w_in"]
    gathered = _gather_chips([shard2d[n].astype(BF16) for n in first] + [small_pack])
    full = {n: _full_from_chips(g, n) for n, g in zip(first, gathered[:-1])}
    full["w_in"] = _pack_win(full["w_in"])
    per_chip = [_unflatten(gathered[-1][q], [w[n].shape for n in SMALL_SHARDED]) for q in range(4)]
    sm = {n: jnp.concatenate([per_chip[q][i] for q in range(4)], axis=-1) for i, n in enumerate(SMALL_SHARDED)}
    for n in SMALL:
        if n not in sm:
            sm[n] = w[n]
    sm = {n: (a.reshape(a.shape[1:]) if a.ndim >= 3 else a) for n, a in sm.items()}

    core_id = lax.axis_index("c").astype(jnp.int32).reshape(1)
    reducer = _Reducer(core_id, chip.astype(jnp.int32).reshape(1))
    loss, gx, gbig, gsmall = _local_step(x, c, ctx, loss_target, sm, full["w_mod"], full["w_in"],
                                         [shard2d[n].astype(BF16) for n in LATE], late_are_shards=True, reducer=reducer)
    mine, other = reducer.finish(gbig, ["w_mod"])
    out = {}
    for n in BIG:
        shp = shard2d[n].shape
        res = _adam_halves(mine[n], other[n], shard2d[n], mom[n].reshape(shp), var[n].reshape(shp), core_id, "adam_" + n)
        out[n] = [r.reshape(w[n].shape) for r in res]

    tiny = [n for n in SMALL if n not in GATE_STACKS]
    as_rows = lambda a: a.reshape(-1, 128)
    summed = _allreduce_small([_flatten([gsmall[n] for n in tiny])] + [as_rows(gsmall[n]) for n in GATE_STACKS])
    gs = {}
    for n, g in zip(tiny, _unflatten(summed[0], [gsmall[n].shape for n in tiny])):
        if n in SMALL_SHARDED:
            width = w[n].shape[-1]
            g = lax.dynamic_slice_in_dim(g, chip * width, width, axis=g.ndim - 1)
        gs[n] = g.reshape(w[n].shape)
    shapes = [w[n].shape for n in tiny]
    d_s, m_s, v_s = _adam_flat(_flatten([gs[n] for n in tiny]), _flatten([w[n] for n in tiny]),
                               _flatten([mom[n] for n in tiny]), _flatten([var[n] for n in tiny]), "adam_small")
    for n, d_, m_, v_ in zip(tiny, _unflatten(d_s, shapes), _unflatten(m_s, shapes), _unflatten(v_s, shapes)):
        out[n] = [gs[n], d_, m_, v_]
    for n, g in zip(GATE_STACKS, summed[1:]):
        res = _adam_flat(g, as_rows(w[n]), as_rows(mom[n]), as_rows(var[n]), "adam_" + n)
        out[n] = [r.reshape(w[n].shape) for r in (g, *res)]

    loss = lax.psum(loss, ("x", "y", "c"))
    return (loss, gx, *[out[n][0] for n in WEIGHTS], *[out[n][1] for n in WEIGHTS], *[out[n][2] for n in WEIGHTS],
            *[out[n][3] for n in WEIGHTS])
```

```python
import functools

import numpy as np
import jax
import jax.numpy as jnp
from jax import lax
from jax.experimental import pallas as pl
from jax.experimental.pallas import tpu as pltpu

F32, BF16 = jnp.float32, jnp.bfloat16
S = jax.ShapeDtypeStruct
BS = pl.BlockSpec
MESH = pl.DeviceIdType.MESH

D = 1024
GRID_W = 64
SSD_INNER, SSD_G, SSD_N, SSD_L = 2048, 8, 128, 128
SSD_GW = 512
MLP_H = 4096
N_MOD = 6
ALPHA = 2.0 ** 0.25
LN_EPS, RMS_EPS = 1e-6, 1e-5
LRU_C = 8.0
P_XBC, P_LRU, P_Z, P_LG, P_MG, P_DT, P_W = 0, 4096, 5120, 7168, 8192, 10240, 10368
P_CB = 3456
IN_COLS = 10304
LRU_CB = 256
ADAM_LR, ADAM_B1, ADAM_B2, ADAM_EPS, ADAM_WD, ADAM_STEP = 0.001, 0.9, 0.999, 1e-08, 0.01, 10
VMEM_LIMIT = 56 * 2 ** 20


def _params(**kw):
    return pltpu.CompilerParams(vmem_limit_bytes=VMEM_LIMIT, **kw)


def _dot(a, b):
    return jnp.dot(a.astype(BF16), b.astype(BF16), preferred_element_type=F32)


def _dot_nt(a, b):
    return lax.dot_general(a.astype(BF16), b.astype(BF16), (((1,), (1,)), ((), ())), preferred_element_type=F32)


def _dot_tn(a, b):
    return lax.dot_general(a.astype(BF16), b.astype(BF16), (((0,), (0,)), ((), ())), preferred_element_type=F32)


@jax.custom_vjp
def _mm(a, b):
    return _dot(a, b)


def _cast_pair(a, b):
    return a.astype(BF16), b.astype(BF16)


def _mm_f(a, b):
    r = _cast_pair(a, b)
    return _dot(*r), r


def _mm_b(r, g):
    g = g.astype(BF16)
    return _dot_nt(g, r[1]), _dot_tn(r[0], g)


_mm.defvjp(_mm_f, _mm_b)


@jax.custom_vjp
def _mm_nt(a, b):
    return _dot_nt(a, b)


def _mm_nt_f(a, b):
    r = _cast_pair(a, b)
    return _dot_nt(*r), r


def _mm_nt_b(r, g):
    g = g.astype(BF16)
    return _dot(g, r[1]), _dot_tn(g, r[0])


_mm_nt.defvjp(_mm_nt_f, _mm_nt_b)


@jax.custom_vjp
def _mm_tn(a, b):
    return _dot_tn(a, b)


def _mm_tn_f(a, b):
    r = _cast_pair(a, b)
    return _dot_tn(*r), r


def _mm_tn_b(r, g):
    g = g.astype(BF16)
    return _dot_nt(r[1], g), _dot(r[0], g)


_mm_tn.defvjp(_mm_tn_f, _mm_tn_b)

def _split3(v):
    h = v.astype(BF16)
    r = v - h.astype(F32)
    m = r.astype(BF16)
    return h, m, (r - m.astype(F32)).astype(BF16)


def _sel_dot(sel, v, dims):
    sel_first = dims[0] == "s"
    dn = {"sv": (((1,), (0,)), ((), ())), "sTv": (((0,), (0,)), ((), ())), "vs": (((1,), (0,)), ((), ())),
          "vsT": (((1,), (1,)), ((), ()))}[dims]
    out = None
    for part in _split3(v):
        a, b = (sel, part) if sel_first else (part, sel)
        term = lax.dot_general(a, b, dn, preferred_element_type=F32)
        out = term if out is None else out + term
    return out


@jax.custom_vjp
def _cum_mm(tri, v):
    return _sel_dot(tri, v, "sv")


_cum_mm.defvjp(lambda tri, v: (_sel_dot(tri, v, "sv"), tri),
               lambda tri, g: (jnp.zeros_like(tri), _sel_dot(tri, g, "sTv")))


@jax.custom_vjp
def _xp_mm(v, e):
    return _sel_dot(e, v, "vs")


_xp_mm.defvjp(lambda v, e: (_sel_dot(e, v, "vs"), e),
              lambda e, g: (_sel_dot(e, g, "vsT"), jnp.zeros_like(e)))


def _sigmoid(x):
    return 1.0 / (1.0 + jnp.exp(-x))


def _silu(x):
    return x * _sigmoid(x)


def _dsilu(x):
    s = _sigmoid(x)
    return s * (1.0 + x * (1.0 - s))


def _softplus(x):
    return jnp.maximum(x, 0.0) + jnp.log1p(jnp.exp(-jnp.abs(x)))


def _gelu(x):
    return 0.5 * x * (1.0 + jnp.tanh(0.7978845608028654 * (x + 0.044715 * x * x * x)))


def _ln(x):
    mu = jnp.mean(x, axis=-1, keepdims=True)
    xc = x - mu
    var = jnp.mean(xc * xc, axis=-1, keepdims=True)
    return xc * lax.rsqrt(var + LN_EPS)


def _modln(x, shift, scale):
    return _ln(x) * (1.0 + scale) + shift


def _resln(x, sub, gate, g, b):
    return _ln(ALPHA * x + gate * sub) * g + b


def _grms(y, z, w):
    u = y * _silu(z)
    return u * lax.rsqrt(jnp.mean(u * u, axis=-1, keepdims=True) + RMS_EPS) * w


def _colsum(v):
    return jnp.sum(v, axis=0, keepdims=True)


def _conv_taps(width, period):
    def masks(rows):
        pos = lax.broadcasted_iota(jnp.int32, (rows, width), 0)
        if rows != period:
            pos = pos & (period - 1)
        return [pos >= 2 - k if k < 2 else pos < period + 2 - k for k in range(4)]
    return masks


def _conv_fwd(raw, w, b, masks):
    rows = raw.shape[0]
    pre = b + raw * w[2:3, :]
    for k in (0, 1, 3):
        sh = pltpu.roll(raw, (2 - k) % rows, 0)
        pre = pre + jnp.where(masks[k], sh, 0.0) * w[k:k + 1, :]
    return pre


def _for_rows(t, rb, fn):
    n = t // rb
    unroll = 4 if n % 4 == 0 else 1

    def step(i, carry):
        for u in range(unroll):
            fn(pl.multiple_of((i * unroll + u) * rb, rb))
        return carry

    lax.fori_loop(0, n // unroll, step, 0)


def _loop_unrolled(n, unroll, body, init):
    def step(i, carry):
        for u in range(unroll):
            carry = body(i * unroll + u, carry)
        return carry

    return lax.fori_loop(0, n // unroll, step, init)


def _conv_bwd(dpre, raw, w, masks):
    rows = raw.shape[0]
    draw = dpre * w[2:3, :]
    dws = []
    for k in range(4):
        if k == 2:
            dws.append(_colsum(dpre * raw))
            continue
        back = pltpu.roll(jnp.where(masks[k], dpre, 0.0), (k - 2) % rows, 0)
        dws.append(_colsum(back * raw))
        draw = draw + back * w[k:k + 1, :]
    return draw, jnp.concatenate(dws, axis=0), _colsum(dpre)


def _mod_fwd(cc, wmod, bmod):
    def body(cc_ref, w_ref, b_ref, o_ref):
        o_ref[...] = _dot(_silu(cc_ref[...]), w_ref[...]) + b_ref[...]

    return pl.pallas_call(
        body, grid=(N_MOD,), name="mod_fwd",
        in_specs=[BS((8, D), lambda j: (0, 0)), BS((D, D), lambda j: (0, j)), BS((1, D), lambda j: (0, j))],
        out_specs=BS((8, D), lambda j: (0, j)), out_shape=S((8, N_MOD * D), F32), compiler_params=_params(),
    )(cc, wmod, bmod)


def _mod_bwd(cc, wmod, dm):
    def body(cc_ref, w_ref, dm_ref, dw_ref, db_ref, dcc_ref):
        j = pl.program_id(0)
        c = cc_ref[...]
        dmv = dm_ref[...]
        dw_ref[...] = _dot_tn(_silu(c), dmv)
        db_ref[...] = _colsum(dmv)

        @pl.when(j == 0)
        def _():
            dcc_ref[...] = jnp.zeros_like(dcc_ref)

        dcc_ref[...] += _dot_nt(dmv, w_ref[...]) * _dsilu(c)

    return pl.pallas_call(
        body, grid=(N_MOD,), name="mod_bwd",
        in_specs=[BS((8, D), lambda j: (0, 0)), BS((D, D), lambda j: (0, j)), BS((8, D), lambda j: (0, j))],
        out_specs=[BS((D, D), lambda j: (0, j)), BS((1, D), lambda j: (0, j)), BS((8, D), lambda j: (0, 0))],
        out_shape=[S((D, N_MOD * D), F32), S((1, N_MOD * D), F32), S((8, D), F32)], compiler_params=_params(),
    )(cc, wmod, dm)


def _inproj_fwd(xa, m4, win, rt, n_lat_tiles, tiles_per_b, ctx_row, name):
    n_tiles = xa.shape[0] // rt

    def mrow(i):
        return jnp.where(i < n_lat_tiles, i // tiles_per_b, ctx_row)

    def body(x_ref, sh_ref, sc_ref, w_hbm, p_ref, h_ref, w_vm, sem):
        @pl.when(pl.program_id(0) == 0)
        def _():
            cp = pltpu.make_async_copy(w_hbm, w_vm, sem)
            cp.start()
            cp.wait()

        hb = _modln(x_ref[...], sh_ref[...], sc_ref[...]).astype(BF16)
        h_ref[...] = hb
        for j in range(P_W // P_CB):
            sl = slice(j * P_CB, (j + 1) * P_CB)
            p_ref[:, sl] = jnp.dot(hb, w_vm[:, sl], preferred_element_type=F32)

    return pl.pallas_call(
        body, grid=(n_tiles,), name=name,
        in_specs=[BS((rt, D), lambda i: (i, 0)),
                  BS((None, None, 1, D), lambda i: (mrow(i), 0, 0, 0)),
                  BS((None, None, 1, D), lambda i: (mrow(i), 1, 0, 0)),
                  BS(memory_space=pl.ANY)],
        out_specs=[BS((rt, P_W), lambda i: (i, 0)), BS((rt, D), lambda i: (i, 0))],
        out_shape=[S((xa.shape[0], P_W), F32), S((xa.shape[0], D), BF16)],
        scratch_shapes=[pltpu.VMEM((D, P_W), BF16), pltpu.SemaphoreType.DMA(())], compiler_params=_params(),
    )(xa, m4, m4, win)


def _inproj_bwd(xa, m4, win, dproj, rt, tile0, n_tiles, tiles_per_b, ctx_row, latent, dxres, side=None):
    def mrow(i):
        return (i // tiles_per_b) if latent else ctx_row

    def body(*refs):
        if side is None:
            return compute(*refs)
        own, exchange = side.split(refs, len(args), len(out_shape), 2)
        side.around(exchange, pl.program_id(0), n_tiles, lambda: compute(*own))

    def compute(x_ref, sh_ref, sc_ref, dp_ref, w_hbm, *rest):
        if latent:
            dxr_ref, gx_ref, dm_ref, w_vm, sem = rest
        else:
            dm_ref, w_vm, sem = rest
        i = pl.program_id(0)

        @pl.when(i == 0)
        def _():
            cp = pltpu.make_async_copy(w_hbm, w_vm, sem)
            cp.start()
            cp.wait()

        dh = lax.dot_general(dp_ref[...], w_vm[...], (((1,), (1,)), ((), ())), preferred_element_type=F32)
        _, vjp = jax.vjp(_modln, x_ref[...], sh_ref[...], sc_ref[...])
        dx, dsh, dsc = vjp(dh)
        if latent:
            gx_ref[...] = dx + dxr_ref[...]

        @pl.when(i % tiles_per_b == 0)
        def _():
            dm_ref[...] = jnp.zeros_like(dm_ref)

        dm_ref[0:1, :] += dsh
        dm_ref[1:2, :] += dsc

    nb = n_tiles // tiles_per_b
    in_specs = [BS((rt, D), lambda i: (tile0 + i, 0)),
                BS((None, None, 1, D), lambda i: (mrow(i), 0, 0, 0)),
                BS((None, None, 1, D), lambda i: (mrow(i), 1, 0, 0)),
                BS((rt, P_W), lambda i: (tile0 + i, 0)),
                BS(memory_space=pl.ANY)]
    args = [xa, m4, m4, dproj, win]
    dm_spec = BS((None, 2, D), lambda i: (i // tiles_per_b, 0, 0))
    if latent:
        in_specs.append(BS((rt, D), lambda i: (i, 0)))
        args.append(dxres)
        out_specs = [BS((rt, D), lambda i: (i, 0)), dm_spec]
        out_shape = [S((n_tiles * rt, D), F32), S((nb, 2, D), F32)]
    else:
        out_specs = [dm_spec]
        out_shape = [S((nb, 2, D), F32)]
    extra = side.arrays if side else []
    return pl.pallas_call(
        body, grid=(n_tiles,), name="inproj_bwd_lat" if latent else "inproj_bwd_ctx",
        in_specs=in_specs + [_HBM] * len(extra), out_specs=out_specs + [_HBM] * len(extra),
        out_shape=out_shape + (side.out_shapes if side else []),
        scratch_shapes=[pltpu.VMEM((D, P_W), BF16), pltpu.SemaphoreType.DMA(())] + (side.sems if side else []),
        compiler_params=_params(),
    )(*args, *extra)


def _matmul_tn(a, b, tm, tn, tk, name):
    k, m = a.shape
    n = b.shape[1]

    def body(a_ref, b_ref, o_ref):
        @pl.when(pl.program_id(2) == 0)
        def _():
            o_ref[...] = jnp.zeros_like(o_ref)

        o_ref[...] += lax.dot_general(a_ref[...], b_ref[...], (((0,), (0,)), ((), ())), preferred_element_type=F32)

    return pl.pallas_call(
        body, grid=(m // tm, n // tn, k // tk), name=name,
        in_specs=[BS((tk, tm), lambda i, j, kk: (kk, i)), BS((tk, tn), lambda i, j, kk: (kk, j))],
        out_specs=BS((tm, tn), lambda i, j, kk: (i, j)), out_shape=S((m, n), F32), compiler_params=_params(),
    )(a, b)


def _matmul_tn2(a1, b1, a2, b2, tm, tn, tk, name):
    k1, m = a1.shape
    n = b1.shape[1]
    n1, n2 = k1 // tk, a2.shape[0] // tk

    def body(a1_ref, b1_ref, a2_ref, b2_ref, o_ref):
        kk = pl.program_id(2)

        @pl.when(kk == 0)
        def _():
            o_ref[...] = jnp.zeros_like(o_ref)

        @pl.when(kk < n1)
        def _():
            o_ref[...] += lax.dot_general(a1_ref[...], b1_ref[...], (((0,), (0,)), ((), ())), preferred_element_type=F32)

        @pl.when(kk >= n1)
        def _():
            o_ref[...] += lax.dot_general(a2_ref[...], b2_ref[...], (((0,), (0,)), ((), ())), preferred_element_type=F32)

    first = lambda kk: jnp.minimum(kk, n1 - 1)
    second = lambda kk: jnp.maximum(kk - n1, 0)
    return pl.pallas_call(
        body, grid=(m // tm, n // tn, n1 + n2), name=name,
        in_specs=[BS((tk, tm), lambda i, j, kk: (first(kk), i)), BS((tk, tn), lambda i, j, kk: (first(kk), j)),
                  BS((tk, tm), lambda i, j, kk: (second(kk), i)), BS((tk, tn), lambda i, j, kk: (second(kk), j))],
        out_specs=BS((tm, tn), lambda i, j, kk: (i, j)), out_shape=S((m, n), F32), compiler_params=_params(),
    )(a1, b1, a2, b2)


def _ssd_consts(heads_per_tile):
    n = SSD_L
    ii = lax.broadcasted_iota(jnp.int32, (n, n), 0)
    jj = lax.broadcasted_iota(jnp.int32, (n, n), 1)
    er = lax.broadcasted_iota(jnp.int32, (128, 256), 0)
    ec = lax.broadcasted_iota(jnp.int32, (128, 256), 1) >> 6
    lane = lax.broadcasted_iota(jnp.int32, (1, 128 * heads_per_tile), 1) >> 6
    per_dir = []
    for d in (0, 1):
        mask = (jj >= ii) if d else (jj <= ii)
        per_dir.append((mask, mask.astype(BF16), (er == ec + 4 * d).astype(BF16)))
    return per_dir, [(lane == h).astype(F32) for h in range(2 * heads_per_tile)]


def _ssd_chunk(x, bm, cm, dtc, dtx, alog, hst, consts, hmasks, rev):
    n = SSD_L
    mask, tri, e = consts
    cum = _cum_mm(tri, dtc * (-jnp.exp(alog)))
    cum_x = _xp_mm(cum, e)
    tot_x = cum_x[0:1, :] if rev else cum_x[n - 1:n, :]
    xd = x * dtx
    hn = jnp.exp(tot_x) * hst + _mm_tn(bm, xd * jnp.exp(tot_x - cum_x))
    if cm is None:
        return hn
    cum_t = cum.T
    cb = _mm_nt(cm, bm)
    if len(hmasks) == 4:
        y = jnp.exp(cum_x) * _mm(cm, hst)
        for h in range(4):
            k = 4 * rev + h
            decay = jnp.exp(jnp.where(mask, cum[:, k:k + 1] - cum_t[k:k + 1, :], -1e30))
            y = y + _mm(cb * decay, xd * hmasks[h])
        return y, hn
    pairs = []
    for p in range(2):
        xdp = xd[:, 128 * p:128 * p + 128]
        yp = None
        for hh in range(2):
            k = 4 * rev + 2 * p + hh
            decay = jnp.exp(jnp.where(mask, cum[:, k:k + 1] - cum_t[k:k + 1, :], -1e30))
            term = _mm(cb * decay, xdp * hmasks[hh])
            yp = term if yp is None else yp + term
        pairs.append(yp)
    return jnp.exp(cum_x) * _mm(cm, hst) + jnp.concatenate(pairs, axis=1), hn


def _ssd_fwd(proj, dtg, cw, cb, dtb, alog, drow, h0f, h0b, nb, t, period, blk0, need_y, gather=()):
    nc = t // SSD_L
    rb = period
    assert t % rb == 0
    unroll = 4 if nc % 4 == 0 else (2 if nc % 2 == 0 else 1)
    masks_of = _conv_taps(SSD_GW, period)
    ng = len(gather)
    n_out = 5 if need_y else 4

    def body(p_ref, dt_ref, cw_ref, cb_ref, dtb_ref, al_ref, d_ref, h0f_ref, h0b_ref, *rest):
        g_ins, rest = rest[:ng], rest[ng:]
        outs, g_outs, (act, dts, dtxs), g_sems = rest[:n_out], rest[n_out:n_out + ng], rest[n_out + ng:n_out + ng + 3], \
            rest[n_out + ng + 3:]
        if need_y:
            y_ref, hsf_ref, hsb_ref, sf_ref, sb_ref = outs
        else:
            hsf_ref, hsb_ref, sf_ref, sb_ref = outs
        if ng:
            exchange = _GatherExchange(g_ins, g_outs, g_sems)
            step = pl.program_id(0) * SSD_G + pl.program_id(1)
            pl.when(step == 0)(exchange.begin)
        masks = masks_of(rb)
        per_dir, hmasks = _ssd_consts(2)

        def prologue(r0):
            rows = pl.ds(r0, rb)
            a = _silu(_conv_fwd(p_ref[rows, :], cw_ref[...], cb_ref[...], masks))
            act[rows, :] = a
            if need_y:
                y_ref[rows, :] = d_ref[...] * a[:, 0:256]
            dtv = _softplus(dt_ref[rows, :] + dtb_ref[...])
            dts[rows, :] = dtv
            for d in (0, 1):
                dtxs[rows, 256 * d:256 * d + 256] = _xp_mm(dtv, per_dir[d][2])

        _for_rows(t, rb, prologue)
        al = al_ref[...]

        def chunk(ci, carry):
            out = []
            for d, hst, hs_ref in ((0, carry[0], hsf_ref), (1, carry[1], hsb_ref)):
                c = (nc - 1 - ci) if d else ci
                r0 = pl.multiple_of(c * SSD_L, SSD_L)
                a = act[pl.ds(r0, SSD_L), :]
                hs_ref[c] = hst
                res = _ssd_chunk(a[:, 0:256], a[:, 256:384], a[:, 384:512] if need_y else None,
                                 dts[pl.ds(r0, SSD_L), :], dtxs[pl.ds(r0, SSD_L), 256 * d:256 * d + 256], al, hst,
                                 per_dir[d], hmasks, d)
                if need_y:
                    y_ref[pl.ds(r0, SSD_L), :] += res[0]
                    res = res[1]
                out.append(res)
            return tuple(out)

        sf_ref[...], sb_ref[...] = _loop_unrolled(nc, unroll, chunk, (h0f_ref[...], h0b_ref[...]))
        if ng:
            pl.when(step == nb * SSD_G - 1)(exchange.finish)

    gspec = lambda shp: BS((None,) + shp, lambda b, g: (g,) + (0,) * len(shp))
    st_spec = BS((None, None, SSD_N, 256), lambda b, g: (b, g, 0, 0))
    hs_spec = BS((None, None, nc, SSD_N, 256), lambda b, g: (b, g, 0, 0, 0))
    in_specs = [BS((t, SSD_GW), lambda b, g: (blk0 + b, g)), BS((None, t, 128), lambda b, g: (g, blk0 + b, 0)),
                gspec((4, SSD_GW)), gspec((1, SSD_GW)), gspec((1, 128)), gspec((1, 128)), gspec((1, 256)),
                st_spec, st_spec]
    out_specs = [hs_spec, hs_spec, st_spec, st_spec]
    out_shape = [S((nb, SSD_G, nc, SSD_N, 256), F32)] * 2 + [S((nb, SSD_G, SSD_N, 256), F32)] * 2
    if need_y:
        out_specs = [BS((t, 256), lambda b, g: (b, g))] + out_specs
        out_shape = [S((nb * t, SSD_INNER), F32)] + out_shape
    return pl.pallas_call(
        body, grid=(nb, SSD_G), name="ssd_fwd_lat" if need_y else "ssd_fwd_ctx",
        in_specs=in_specs + [_HBM] * ng, out_specs=out_specs + [_HBM] * ng,
        out_shape=out_shape + _gather_out_shapes(gather),
        scratch_shapes=[pltpu.VMEM((t, SSD_GW), F32), pltpu.VMEM((t, 128), F32), pltpu.VMEM((t, SSD_GW), F32)]
        + (_gather_sems(ng) if ng else []),
        compiler_params=_params(),
    )(proj, dtg, cw, cb, dtb, alog, drow, h0f, h0b, *gather)


def _ssd_bwd(proj, dtg, cw, cb, dtb, alog, drow, hsf, hsb, dy, dsf, dsb, dproj, nb, t, period, blk0, need_y):
    nc = t // SSD_L
    rb = period
    assert t % rb == 0
    unroll = 2 if nc % 2 == 0 else 1
    masks_of = _conv_taps(SSD_GW, period)

    def body(*refs):
        if need_y:
            (p_ref, dt_ref, cw_ref, cb_ref, dtb_ref, al_ref, d_ref, hsf_ref, hsb_ref, dy_ref, dsf_ref, dsb_ref, _,
             dp_ref, ddt_ref, dhf_ref, dhb_ref, dcw_ref, dcb_ref, ddtb_ref, dal_ref, dd_ref,
             pre, dact, dts, ddts, dtxs) = refs
        else:
            (p_ref, dt_ref, cw_ref, cb_ref, dtb_ref, al_ref, d_ref, hsf_ref, hsb_ref, dsf_ref, dsb_ref, _,
             dp_ref, ddt_ref, dhf_ref, dhb_ref, dcw_ref, dcb_ref, ddtb_ref, dal_ref, dd_ref,
             pre, dact, dts, ddts, dtxs) = refs
            dy_ref = None
        b, g = pl.program_id(0), pl.program_id(1)

        @pl.when(jnp.logical_and(b == 0, g == 0))
        def _():
            for r in (dcw_ref, dcb_ref, ddtb_ref, dal_ref, dd_ref):
                r[...] = jnp.zeros_like(r)

        masks = masks_of(rb)
        per_dir, hmasks = _ssd_consts(1)

        def prologue(r0):
            rows = pl.ds(r0, rb)
            pre[rows, :] = _conv_fwd(p_ref[rows, :], cw_ref[...], cb_ref[...], masks)
            dtv = _softplus(dt_ref[rows, :] + dtb_ref[...])
            dts[rows, :] = dtv
            for d in (0, 1):
                dtxs[rows, 256 * d:256 * d + 256] = _xp_mm(dtv, per_dir[d][2])
            dact[rows, :] = jnp.zeros((rb, SSD_GW), F32)
            ddts[rows, :] = jnp.zeros((rb, 128), F32)

        _for_rows(t, rb, prologue)
        al = al_ref[...]
        def chunk(ci, carry):
            dal_c = carry[2]
            dhs_out = []
            for d, dh, hs_ref in ((0, carry[0], hsf_ref), (1, carry[1], hsb_ref)):
                c = ci if d else (nc - 1 - ci)
                r0 = pl.multiple_of(c * SSD_L, SSD_L)
                a = _silu(pre[pl.ds(r0, SSD_L), :])
                dtc = dts[pl.ds(r0, SSD_L), :]
                dtx = dtxs[pl.ds(r0, SSD_L), 256 * d:256 * d + 256]
                if need_y:
                    fn = lambda x_, bm_, cm_, dt_, dx_, al_, hs_: _ssd_chunk(x_, bm_, cm_, dt_, dx_, al_, hs_, per_dir[d],
                                                                             hmasks, d)
                    _, vjp = jax.vjp(fn, a[:, 0:256], a[:, 256:384], a[:, 384:512], dtc, dtx, al, hs_ref[c])
                    dx, dbm, dcm, ddtc, ddtx, dal_k, dhs = vjp((dy_ref[pl.ds(r0, SSD_L), :], dh))
                    dact[pl.ds(r0, SSD_L), 384:512] += dcm
                else:
                    fn = lambda x_, bm_, dt_, dx_, al_, hs_: _ssd_chunk(x_, bm_, None, dt_, dx_, al_, hs_, per_dir[d],
                                                                        hmasks, d)
                    _, vjp = jax.vjp(fn, a[:, 0:256], a[:, 256:384], dtc, dtx, al, hs_ref[c])
                    dx, dbm, ddtc, ddtx, dal_k, dhs = vjp(dh)
                dact[pl.ds(r0, SSD_L), 0:256] += dx
                dact[pl.ds(r0, SSD_L), 256:384] += dbm
                ddts[pl.ds(r0, SSD_L), :] += ddtc + _dot_nt(ddtx, per_dir[d][2])
                dhs_out.append(dhs)
                dal_c = dal_c + dal_k
            return dhs_out[0], dhs_out[1], dal_c

        dhf_ref[...], dhb_ref[...], dal_acc = _loop_unrolled(
            nc, unroll, chunk, (dsf_ref[...], dsb_ref[...], jnp.zeros((1, 128), F32)))

        def epilogue(r0):
            rows = pl.ds(r0, rb)
            prev = pre[rows, :]
            if need_y:
                dyv = dy_ref[rows, :]
                dact[rows, 0:256] += d_ref[...] * dyv
                dd_ref[g] += _colsum(dyv * _silu(prev[:, 0:256]))
            dpre = dact[rows, :] * _dsilu(prev)
            draw, dcw, dcb = _conv_bwd(dpre, p_ref[rows, :], cw_ref[...], masks)
            dp_ref[rows, :] = draw.astype(BF16)
            dcw_ref[g] += dcw
            dcb_ref[g] += dcb
            ddraw = ddts[rows, :] * _sigmoid(dt_ref[rows, :] + dtb_ref[...])
            ddt_ref[rows, :] = ddraw
            ddtb_ref[g] += _colsum(ddraw)

        _for_rows(t, rb, epilogue)
        dal_ref[g] += dal_acc

    gspec = lambda shp: BS((None,) + shp, lambda b, g: (g,) + (0,) * len(shp))
    full = lambda shp: BS(shp, lambda b, g: (0,) * len(shp))
    st_spec = BS((None, None, SSD_N, 256), lambda b, g: (b, g, 0, 0))
    hs_spec = BS((None, None, nc, SSD_N, 256), lambda b, g: (b, g, 0, 0, 0))
    p_spec = BS((t, SSD_GW), lambda b, g: (blk0 + b, g))
    in_specs = [p_spec, BS((None, t, 128), lambda b, g: (g, blk0 + b, 0)),
                gspec((4, SSD_GW)), gspec((1, SSD_GW)), gspec((1, 128)), gspec((1, 128)), gspec((1, 256)),
                hs_spec, hs_spec]
    args = [proj, dtg, cw, cb, dtb, alog, drow, hsf, hsb]
    if need_y:
        in_specs.append(BS((t, 256), lambda b, g: (b, g)))
        args.append(dy)
    in_specs += [st_spec, st_spec, BS(memory_space=pl.ANY)]
    args += [dsf, dsb, dproj]
    out_specs = [p_spec, BS((None, t, 128), lambda b, g: (g, b, 0)), st_spec, st_spec,
                 full((SSD_G, 4, SSD_GW)), full((SSD_G, 1, SSD_GW)), full((SSD_G, 1, 128)), full((SSD_G, 1, 128)),
                 full((SSD_G, 1, 256))]
    out_shape = [S(dproj.shape, BF16), S((SSD_G, nb * t, 128), F32),
                 S((nb, SSD_G, SSD_N, 256), F32), S((nb, SSD_G, SSD_N, 256), F32),
                 S((SSD_G, 4, SSD_GW), F32), S((SSD_G, 1, SSD_GW), F32), S((SSD_G, 1, 128), F32),
                 S((SSD_G, 1, 128), F32), S((SSD_G, 1, 256), F32)]
    return pl.pallas_call(
        body, grid=(nb, SSD_G), name="ssd_bwd_lat" if need_y else "ssd_bwd_ctx",
        in_specs=in_specs, out_specs=out_specs, out_shape=out_shape,
        input_output_aliases={len(args) - 1: 0},
        scratch_shapes=[pltpu.VMEM((t, SSD_GW), F32), pltpu.VMEM((t, SSD_GW), F32), pltpu.VMEM((t, 128), F32),
                        pltpu.VMEM((t, 128), F32), pltpu.VMEM((t, SSD_GW), F32)],
        compiler_params=_params(),
    )(*args)


def _lru_gate(u, wa, ba, wi, bi, lam):
    r = _sigmoid(_mm(u, wa) + ba)
    i = _sigmoid(_mm(u, wi) + bi)
    log_a = -LRU_C * r * _softplus(-lam)
    a = jnp.exp(log_a)
    x2 = 2.0 * log_a
    em1 = jnp.where(x2 > -0.01, x2 * (1.0 + x2 * (0.5 + x2 * (1.0 / 6.0 + x2 * (1.0 / 24.0)))), a * a - 1.0)
    return a, jnp.sqrt(-em1) * (i * u)


def _scan_pair(fwd, rev, nblk, width):
    row = lax.broadcasted_iota(jnp.int32, (8, width), 0)

    def block(a_ref, b_ref, h_ref, st, carry, reverse):
        av, bv = a_ref[pl.ds(st, 8), :], b_ref[pl.ds(st, 8), :]
        for s in (1, 2, 4):
            ok = (row < 8 - s) if reverse else (row >= s)
            sh = (8 - s) if reverse else s
            a_sh = jnp.where(ok, pltpu.roll(av, sh, 0), 1.0)
            b_sh = jnp.where(ok, pltpu.roll(bv, sh, 0), 0.0)
            bv = av * b_sh + bv
            av = av * a_sh
        h = bv + av * carry
        h_ref[pl.ds(st, 8), :] = h
        return h[0:1, :] if reverse else h[7:8, :]

    def step(i, carry):
        cf, cr = carry
        cf = block(fwd[0], fwd[1], fwd[2], pl.multiple_of(i * 8, 8), cf, False)
        cr = block(rev[0], rev[1], rev[2], pl.multiple_of((nblk - 1 - i) * 8, 8), cr, True)
        return cf, cr

    return lax.fori_loop(0, nblk, step, (fwd[3], rev[3]))


def _lru_specs(t, blk0):
    p_spec = BS((t, LRU_CB), lambda b, q: (blk0 + b, P_LRU // LRU_CB + q))
    w_spec = BS((2, 2, 128, 128), lambda b, q: (0, q, 0, 0))
    v_spec = BS((2, LRU_CB), lambda b, q: (0, q))
    c_spec = lambda r: BS((r, LRU_CB), lambda b, q: (0, q))
    s_spec = BS((None, 2, LRU_CB), lambda b, q: (b, 0, q))
    return p_spec, w_spec, v_spec, c_spec, s_spec


def _lru_fwd(proj, cw, cb, wa, ba, wi, bi, lam, h0, nb, t, period, blk0, need_y):
    nq = D // LRU_CB
    masks_of = _conv_taps(LRU_CB, period)

    def body(p_ref, cw_ref, cb_ref, wa_ref, ba_ref, wi_ref, bi_ref, lam_ref, h0_ref, *rest):
        if need_y:
            y_ref, hf_ref, hb_ref, fin_ref, sa0, sb0, sa1, sb1 = rest
        else:
            hf_ref, hb_ref, fin_ref, sa0, sb0, sa1, sb1 = rest
        u = _conv_fwd(p_ref[...], cw_ref[...], cb_ref[...], masks_of(t))
        for d, (sa, sb) in enumerate(((sa0, sb0), (sa1, sb1))):
            for j in range(2):
                sl = slice(128 * j, 128 * j + 128)
                a, bb = _lru_gate(u[:, sl], wa_ref[d, j], ba_ref[d:d + 1, sl], wi_ref[d, j], bi_ref[d:d + 1, sl],
                                  lam_ref[d:d + 1, sl])
                sa[:, sl] = a
                sb[:, sl] = bb
        lf, lb = _scan_pair((sa0, sb0, hf_ref, h0_ref[0:1, :]), (sa1, sb1, hb_ref, h0_ref[1:2, :]), t // 8, LRU_CB)
        fin_ref[0:1, :] = lf
        fin_ref[1:2, :] = lb
        if need_y:
            y_ref[...] = hf_ref[...] + hb_ref[...]

    p_spec, w_spec, v_spec, c_spec, s_spec = _lru_specs(t, blk0)
    o_spec = BS((t, LRU_CB), lambda b, q: (b, q))
    out_specs = [o_spec, o_spec, s_spec]
    out_shape = [S((nb * t, D), F32), S((nb * t, D), F32), S((nb, 2, D), F32)]
    if need_y:
        out_specs = [o_spec] + out_specs
        out_shape = [S((nb * t, D), F32)] + out_shape
    return pl.pallas_call(
        body, grid=(nb, nq), name="lru_fwd_lat" if need_y else "lru_fwd_ctx",
        in_specs=[p_spec, c_spec(4), c_spec(1), w_spec, v_spec, w_spec, v_spec, v_spec, s_spec],
        out_specs=out_specs, out_shape=out_shape,
        scratch_shapes=[pltpu.VMEM((t, LRU_CB), F32)] * 4, compiler_params=_params(),
    )(proj, cw, cb, wa, ba, wi, bi, lam, h0)


def _lru_bwd(proj, cw, cb, wa, ba, wi, bi, lam, h0, hf, hb, dy, dfin, dproj, nb, t, period, blk0, need_y, side=None):
    nq = D // LRU_CB
    rc = min(256, t)
    masks_of = _conv_taps(LRU_CB, period)

    def body(*refs):
        if side is None:
            return compute(*refs)
        own, exchange = side.split(refs, len(args), len(out_shape), 7)
        side.around(exchange, pl.program_id(0) * nq + pl.program_id(1), nb * nq, lambda: compute(*own))

    def compute(*refs):
        if need_y:
            (p_ref, cw_ref, cb_ref, wa_ref, ba_ref, wi_ref, bi_ref, lam_ref, h0_ref, hf_ref, hb_ref, dy_ref, dfin_ref, _,
             dp_ref, dh0_ref, dcw_ref, dcb_ref, dwa_ref, dwi_ref, dba_ref, dbi_ref, dlam_ref,
             su, sa0, sa1, sc0, sc1, sg0, sg1) = refs
        else:
            (p_ref, cw_ref, cb_ref, wa_ref, ba_ref, wi_ref, bi_ref, lam_ref, h0_ref, hf_ref, hb_ref, dfin_ref, _,
             dp_ref, dh0_ref, dcw_ref, dcb_ref, dwa_ref, dwi_ref, dba_ref, dbi_ref, dlam_ref,
             su, sa0, sa1, sc0, sc1, sg0, sg1) = refs
            dy_ref = None
        b, q = pl.program_id(0), pl.program_id(1)

        @pl.when(jnp.logical_and(b == 0, q == 0))
        def _():
            for r in (dcw_ref, dcb_ref, dwa_ref, dwi_ref, dba_ref, dbi_ref, dlam_ref):
                r[...] = jnp.zeros_like(r)

        masks = masks_of(t)
        u = _conv_fwd(p_ref[...], cw_ref[...], cb_ref[...], masks)
        su[...] = u
        for d, sa in enumerate((sa0, sa1)):
            for j in range(2):
                sl = slice(128 * j, 128 * j + 128)
                a, _unused = _lru_gate(u[:, sl], wa_ref[d, j], ba_ref[d:d + 1, sl], wi_ref[d, j], bi_ref[d:d + 1, sl],
                                       lam_ref[d:d + 1, sl])
                sa[:, sl] = a
        rowi = lax.broadcasted_iota(jnp.int32, (t, LRU_CB), 0)
        last, first = rowi == t - 1, rowi == 0
        sc0[...] = jnp.where(last, 0.0, pltpu.roll(sa0[...], t - 1, 0))
        sc1[...] = jnp.where(first, 0.0, pltpu.roll(sa1[...], 1, 0))
        g0 = jnp.where(last, dfin_ref[0:1, :], 0.0)
        g1 = jnp.where(first, dfin_ref[1:2, :], 0.0)
        if need_y:
            g0 = g0 + dy_ref[...]
            g1 = g1 + dy_ref[...]
        sg0[...] = g0
        sg1[...] = g1
        zero = jnp.zeros((1, LRU_CB), F32)
        _scan_pair((sc1, sg1, sg1, zero), (sc0, sg0, sg0, zero), t // 8, LRU_CB)
        dh0_ref[0:1, :] = sa0[0:1, :] * sg0[0:1, :]
        dh0_ref[1:2, :] = sa1[t - 1:t, :] * sg1[t - 1:t, :]
        sc0[...] = sg0[...] * jnp.where(first, h0_ref[0:1, :], pltpu.roll(hf_ref[...], 1, 0))
        sc1[...] = sg1[...] * jnp.where(last, h0_ref[1:2, :], pltpu.roll(hb_ref[...], t - 1, 0))

        def rows(ci, carry):
            r0 = pl.multiple_of(ci * rc, rc)
            for j in range(2):
                sl = slice(128 * j, 128 * j + 128)
                du = jnp.zeros((rc, 128), F32)
                for d, (sc, sg) in enumerate(((sc0, sg0), (sc1, sg1))):
                    _, vjp = jax.vjp(_lru_gate, su[pl.ds(r0, rc), sl], wa_ref[d, j], ba_ref[d:d + 1, sl], wi_ref[d, j],
                                     bi_ref[d:d + 1, sl], lam_ref[d:d + 1, sl])
                    du_d, dwa, dba, dwi, dbi, dlam = vjp((sc[pl.ds(r0, rc), sl], sg[pl.ds(r0, rc), sl]))
                    du = du + du_d
                    dwa_ref[d, 2 * q + j] += dwa
                    dwi_ref[d, 2 * q + j] += dwi
                    dba_ref[q, d:d + 1, sl] += dba
                    dbi_ref[q, d:d + 1, sl] += dbi
                    dlam_ref[q, d:d + 1, sl] += dlam
                sa0[pl.ds(r0, rc), sl] = du
            return carry

        lax.fori_loop(0, t // rc, rows, 0)
        draw, dcw, dcb = _conv_bwd(sa0[...], p_ref[...], cw_ref[...], masks)
        dp_ref[...] = draw.astype(BF16)
        dcw_ref[q] += dcw
        dcb_ref[q] += dcb

    p_spec, w_spec, v_spec, c_spec, s_spec = _lru_specs(t, blk0)
    o_spec = BS((t, LRU_CB), lambda b, q: (b, q))
    full = lambda shp: BS(shp, lambda b, q: (0,) * len(shp))
    in_specs = [p_spec, c_spec(4), c_spec(1), w_spec, v_spec, w_spec, v_spec, v_spec, s_spec, o_spec, o_spec]
    args = [proj, cw, cb, wa, ba, wi, bi, lam, h0, hf, hb]
    if need_y:
        in_specs.append(o_spec)
        args.append(dy)
    in_specs += [s_spec, BS(memory_space=pl.ANY)]
    args += [dfin, dproj]
    out_specs = [p_spec, s_spec, full((nq, 4, LRU_CB)), full((nq, 1, LRU_CB)), full((2, 8, 128, 128)),
                 full((2, 8, 128, 128)), full((nq, 2, LRU_CB)), full((nq, 2, LRU_CB)), full((nq, 2, LRU_CB))]
    out_shape = [S(dproj.shape, BF16), S((nb, 2, D), F32), S((nq, 4, LRU_CB), F32), S((nq, 1, LRU_CB), F32),
                 S((2, 8, 128, 128), F32), S((2, 8, 128, 128), F32), S((nq, 2, LRU_CB), F32), S((nq, 2, LRU_CB), F32),
                 S((nq, 2, LRU_CB), F32)]
    extra = side.arrays if side else []
    return pl.pallas_call(
        body, grid=(nb, nq), name="lru_bwd_lat" if need_y else "lru_bwd_ctx",
        in_specs=in_specs + [_HBM] * len(extra), out_specs=out_specs + [_HBM] * len(extra),
        out_shape=out_shape + (side.out_shapes if side else []), input_output_aliases={len(args) - 1: 0},
        scratch_shapes=[pltpu.VMEM((t, LRU_CB), F32)] * 7 + (side.sems if side else []), compiler_params=_params(),
    )(*args, *extra)


def _mix_core(y_ref, yl_ref, p_ref, nw_ref, bg_ref, wbs_ref, wbl_ref, wo_ref, nrm_s):
    for g in range(SSD_G):
        sl = slice(256 * g, 256 * g + 256)
        nrm_s[:, sl] = _grms(y_ref[:, sl], p_ref[:, sl], nw_ref[:, sl]).astype(BF16)
    br_s = jnp.dot(nrm_s[...], wbs_ref[...], preferred_element_type=F32)
    gl = (yl_ref[...] * _gelu(p_ref[:, 2048:3072])).astype(BF16)
    br_l = jnp.dot(gl, wbl_ref[...], preferred_element_type=F32)
    gs = _sigmoid(p_ref[:, 3072:4096] + bg_ref[:, 0:D])
    gr = _sigmoid(p_ref[:, 4096:5120] + bg_ref[:, D:2 * D])
    mix = (gs * br_s + gr * br_l).astype(BF16)
    xmix = jnp.dot(mix, wo_ref[...], preferred_element_type=F32)
    return br_s, gl, br_l, gs, gr, mix, xmix


def _mix_specs(rt, tiles_per_b):
    row = lambda w: BS((rt, w), lambda i: (i, 0))
    const = lambda shp: BS(shp, lambda i: (0,) * len(shp))
    gate = BS((None, None, 1, D), lambda i: (i // tiles_per_b, 2, 0, 0))
    return row, const, gate


def _mix_fwd(y, ylru, proj, x, m4, wbs, wbl, wo, nw, bg, l1g, l1b, rt, tiles_per_b):
    n = x.shape[0]

    def body(y_ref, yl_ref, p_ref, x_ref, g1_ref, wbs_ref, wbl_ref, wo_ref, nw_ref, bg_ref, lg_ref, lb_ref,
             x1_ref, nrm_ref, gl_ref, mix_ref, brs_ref, brl_ref, xm_ref):
        br_s, gl, br_l, _, _, mix, xmix = _mix_core(y_ref, yl_ref, p_ref, nw_ref, bg_ref, wbs_ref, wbl_ref, wo_ref, nrm_ref)
        gl_ref[...] = gl
        mix_ref[...] = mix
        brs_ref[...] = br_s
        brl_ref[...] = br_l
        xm_ref[...] = xmix
        x1_ref[...] = _resln(x_ref[...], xmix, g1_ref[...], lg_ref[...], lb_ref[...])

    row, const, gate = _mix_specs(rt, tiles_per_b)
    return pl.pallas_call(
        body, grid=(n // rt,), name="mix_fwd",
        in_specs=[row(SSD_INNER), row(D), BS((rt, 5120), lambda i: (i, 1)), row(D), gate,
                  const((SSD_INNER, D)), const((D, D)), const((D, D)), const((1, SSD_INNER)), const((1, 2 * D)),
                  const((1, D)), const((1, D))],
        out_specs=[row(D), row(SSD_INNER), row(D), row(D), row(D), row(D), row(D)],
        out_shape=[S((n, D), F32), S((n, SSD_INNER), BF16), S((n, D), BF16), S((n, D), BF16), S((n, D), F32),
                   S((n, D), F32), S((n, D), F32)],
        compiler_params=_params(),
    )(y, ylru, proj, x, m4, wbs, wbl, wo, nw, bg, l1g, l1b)


def _mix_bwd(y, ylru, proj, x, m4, wbs, wbl, wo, nw, bg, l1g, l1b, brs, brl, xmix, dx1, dproj, rt, tiles_per_b):
    n = x.shape[0]

    def body(y_ref, yl_ref, p_ref, x_ref, g1_ref, wbs_ref, wbl_ref, wo_ref, nw_ref, bg_ref, lg_ref, lb_ref,
             brs_ref, brl_ref, xm_ref, dx1_ref, _,
             dp_ref, dy_ref, dyl_ref, dxr_ref, dbrs_ref, dbrl_ref, dxm_ref,
             dg1_ref, dnw_ref, dbg_ref, dlg_ref, dlb_ref):
        i = pl.program_id(0)

        @pl.when(i == 0)
        def _():
            for r in (dnw_ref, dbg_ref, dlg_ref, dlb_ref):
                r[...] = jnp.zeros_like(r)

        @pl.when(i % tiles_per_b == 0)
        def _():
            dg1_ref[...] = jnp.zeros_like(dg1_ref)

        br_s, br_l = brs_ref[...], brl_ref[...]
        gs = _sigmoid(p_ref[:, 3072:4096] + bg_ref[:, 0:D])
        gr = _sigmoid(p_ref[:, 4096:5120] + bg_ref[:, D:2 * D])
        _, vjp = jax.vjp(_resln, x_ref[...], xm_ref[...], g1_ref[...], lg_ref[...], lb_ref[...])
        dxr, dxmix, dg1, dlg, dlb = vjp(dx1_ref[...])
        dxr_ref[...] = dxr
        dg1_ref[...] += dg1
        dlg_ref[...] += dlg
        dlb_ref[...] += dlb
        dxmb = dxmix.astype(BF16)
        dxm_ref[...] = dxmb
        dmix = lax.dot_general(dxmb, wo_ref[...], (((1,), (1,)), ((), ())), preferred_element_type=F32)
        dbrs = (dmix * gs).astype(BF16)
        dbrl = (dmix * gr).astype(BF16)
        dbrs_ref[...] = dbrs
        dbrl_ref[...] = dbrl
        dmg_s = dmix * br_s * gs * (1.0 - gs)
        dmg_r = dmix * br_l * gr * (1.0 - gr)
        dp_ref[:, 3072:4096] = dmg_s.astype(BF16)
        dp_ref[:, 4096:5120] = dmg_r.astype(BF16)
        dbg_ref[:, 0:D] += _colsum(dmg_s)
        dbg_ref[:, D:2 * D] += _colsum(dmg_r)
        dnrm = lax.dot_general(dbrs, wbs_ref[...], (((1,), (1,)), ((), ())), preferred_element_type=F32)
        for g in range(SSD_G):
            sl = slice(256 * g, 256 * g + 256)
            _, vjp = jax.vjp(_grms, y_ref[:, sl], p_ref[:, sl], nw_ref[:, sl])
            dyg, dzg, dnwg = vjp(dnrm[:, sl])
            dy_ref[:, sl] = dyg
            dp_ref[:, sl] = dzg.astype(BF16)
            dnw_ref[:, sl] += dnwg
        dgl = lax.dot_general(dbrl, wbl_ref[...], (((1,), (1,)), ((), ())), preferred_element_type=F32)
        _, vjp = jax.vjp(lambda a, c: a * _gelu(c), yl_ref[...], p_ref[:, 2048:3072])
        dyl, dlgate = vjp(dgl)
        dyl_ref[...] = dyl
        dp_ref[:, 2048:3072] = dlgate.astype(BF16)

    row, const, gate = _mix_specs(rt, tiles_per_b)
    pblk = BS((rt, 5120), lambda i: (i, 1))
    nb = n // (rt * tiles_per_b)
    out_specs = [pblk, row(SSD_INNER), row(D), row(D), row(D), row(D), row(D),
                 BS((None, 1, D), lambda i: (i // tiles_per_b, 0, 0)), const((1, SSD_INNER)), const((1, 2 * D)),
                 const((1, D)), const((1, D))]
    out_shape = [S(dproj.shape, BF16), S((n, SSD_INNER), F32), S((n, D), F32), S((n, D), F32),
                 S((n, D), BF16), S((n, D), BF16), S((n, D), BF16),
                 S((nb, 1, D), F32), S((1, SSD_INNER), F32), S((1, 2 * D), F32), S((1, D), F32), S((1, D), F32)]
    return pl.pallas_call(
        body, grid=(n // rt,), name="mix_bwd",
        in_specs=[row(SSD_INNER), row(D), pblk, row(D), gate,
                  const((SSD_INNER, D)), const((D, D)), const((D, D)), const((1, SSD_INNER)), const((1, 2 * D)),
                  const((1, D)), const((1, D)), row(D), row(D), row(D), row(D), BS(memory_space=pl.ANY)],
        out_specs=out_specs, out_shape=out_shape, input_output_aliases={16: 0},
        compiler_params=_params(),
    )(y, ylru, proj, x, m4, wbs, wbl, wo, nw, bg, l1g, l1b, brs, brl, xmix, dx1, dproj)


def _mlp_step(x1, tgt, m4, w1, b1, w2, b2, l2g, l2b, rt, tiles_per_b):
    n = x1.shape[0]

    def body(x_ref, t_ref, sh_ref, sc_ref, gt_ref, w1_hbm, b1_ref, w2_hbm, b2_ref, lg_ref, lb_ref,
             loss_ref, dx_ref, h2_ref, da1_ref, r2_ref, dmlp_ref, dm_ref, db1_ref, db2_ref, dlg_ref, dlb_ref,
             w1_vm, w2_vm, sem):
        i = pl.program_id(0)

        @pl.when(i == 0)
        def _():
            c1 = pltpu.make_async_copy(w1_hbm, w1_vm, sem.at[0])
            c2 = pltpu.make_async_copy(w2_hbm, w2_vm, sem.at[1])
            c1.start()
            c2.start()
            for r in (loss_ref, db1_ref, db2_ref, dlg_ref, dlb_ref):
                r[...] = jnp.zeros_like(r)
            c1.wait()
            c2.wait()

        @pl.when(i % tiles_per_b == 0)
        def _():
            dm_ref[...] = jnp.zeros_like(dm_ref)

        x1v = x_ref[...]
        h2, vjp_h = jax.vjp(_modln, x1v, sh_ref[...], sc_ref[...])
        h2b = h2.astype(BF16)
        h2_ref[...] = h2b
        r = jnp.maximum(jnp.dot(h2b, w1_vm[...], preferred_element_type=F32) + b1_ref[...], 0.0)
        r2b = (r * r).astype(BF16)
        r2_ref[...] = r2b
        mlp = jnp.dot(r2b, w2_vm[...], preferred_element_type=F32) + b2_ref[...]
        x2, vjp_r = jax.vjp(_resln, x1v, mlp, gt_ref[...], lg_ref[...], lb_ref[...])
        diff = x2 - t_ref[...]
        loss_ref[...] += (0.5 / D) * jnp.sum(diff * diff)
        dxa, dmlp, dgt, dlg, dlb = vjp_r(diff * (1.0 / D))
        dlg_ref[...] += dlg
        dlb_ref[...] += dlb
        dm_ref[2:3, :] += dgt
        db2_ref[...] += _colsum(dmlp)
        dmlpb = dmlp.astype(BF16)
        dmlp_ref[...] = dmlpb
        da1 = lax.dot_general(dmlpb, w2_vm[...], (((1,), (1,)), ((), ())), preferred_element_type=F32) * (2.0 * r)
        db1_ref[...] += _colsum(da1)
        da1b = da1.astype(BF16)
        da1_ref[...] = da1b
        dh2 = lax.dot_general(da1b, w1_vm[...], (((1,), (1,)), ((), ())), preferred_element_type=F32)
        dxb, dsh, dsc = vjp_h(dh2)
        dx_ref[...] = dxa + dxb
        dm_ref[0:1, :] += dsh
        dm_ref[1:2, :] += dsc

    row = lambda w: BS((rt, w), lambda i: (i, 0))
    const = lambda shp: BS(shp, lambda i: (0,) * len(shp))
    mod = lambda k: BS((None, None, 1, D), lambda i: (i // tiles_per_b, k, 0, 0))
    nb = n // (rt * tiles_per_b)
    anyspec = BS(memory_space=pl.ANY)
    return pl.pallas_call(
        body, grid=(n // rt,), name="mlp_step",
        in_specs=[row(D), row(D), mod(3), mod(4), mod(5), anyspec, const((1, MLP_H)), anyspec, const((1, D)),
                  const((1, D)), const((1, D))],
        out_specs=[const((8, 128)), row(D), row(D), row(MLP_H), row(MLP_H), row(D),
                   BS((None, 3, D), lambda i: (i // tiles_per_b, 0, 0)), const((1, MLP_H)), const((1, D)),
                   const((1, D)), const((1, D))],
        out_shape=[S((8, 128), F32), S((n, D), F32), S((n, D), BF16), S((n, MLP_H), BF16), S((n, MLP_H), BF16),
                   S((n, D), BF16), S((nb, 3, D), F32), S((1, MLP_H), F32), S((1, D), F32), S((1, D), F32),
                   S((1, D), F32)],
        scratch_shapes=[pltpu.VMEM((D, MLP_H), BF16), pltpu.VMEM((MLP_H, D), BF16), pltpu.SemaphoreType.DMA((2,))],
        compiler_params=_params(),
    )(x1, tgt, m4, m4, m4, w1, b1, w2, b2, l2g, l2b)


def _pack_win(w):
    parts = []
    for g in range(SSD_G):
        parts += [w[:, 256 * g:256 * g + 256], w[:, 2048 + 128 * g:2176 + 128 * g], w[:, 4160 + 128 * g:4288 + 128 * g]]
    parts += [w[:, 3136:4160], w[:, 5184:7232], w[:, 7232:8256], w[:, 8256:10304], w[:, 3072:3136],
              jnp.zeros((w.shape[0], P_W - P_DT - 64), w.dtype)]
    return jnp.concatenate(parts, axis=1)


def _unpack_win(p):
    xs = [p[:, 512 * g:512 * g + 256] for g in range(SSD_G)]
    bs = [p[:, 512 * g + 256:512 * g + 384] for g in range(SSD_G)]
    cs = [p[:, 512 * g + 384:512 * g + 512] for g in range(SSD_G)]
    return jnp.concatenate(xs + bs + [p[:, P_DT:P_DT + 64], p[:, P_LRU:P_Z]] + cs + [p[:, P_Z:P_DT]], axis=1)


def _pack_conv(w):
    return jnp.stack([jnp.concatenate([w[:, 256 * g:256 * g + 256], w[:, 2048 + 128 * g:2176 + 128 * g],
                                       w[:, 3072 + 128 * g:3200 + 128 * g]], axis=1) for g in range(SSD_G)])


def _unpack_conv(p):
    r = p.shape[1]
    x = jnp.transpose(p[:, :, 0:256], (1, 0, 2)).reshape(r, 2048)
    b = jnp.transpose(p[:, :, 256:384], (1, 0, 2)).reshape(r, 1024)
    c = jnp.transpose(p[:, :, 384:512], (1, 0, 2)).reshape(r, 1024)
    return jnp.concatenate([x, b, c], axis=1)


def _pack_heads(v):
    p = jnp.transpose(v.reshape(2, SSD_G, 4), (1, 0, 2)).reshape(SSD_G, 1, 8)
    return jnp.pad(p, ((0, 0), (0, 0), (0, 120)))


def _unpack_heads(p):
    return jnp.transpose(p[:, 0, 0:8].reshape(SSD_G, 2, 4), (1, 0, 2)).reshape(2, 32)


def _pack_dt(dt):
    n = dt.shape[0]
    p = jnp.transpose(dt.reshape(n, 2, SSD_G, 4), (2, 0, 1, 3)).reshape(SSD_G, n, 8)
    return jnp.pad(p, ((0, 0), (0, 0), (0, 120)))


def _unpack_dt(p):
    n = p.shape[1]
    return jnp.transpose(p[:, :, 0:8].reshape(SSD_G, n, 2, 4), (1, 2, 0, 3)).reshape(n, 64)


def _tk(rows):
    return next(tk for tk in (1024, 512, 256, 128) if rows % tk == 0)


LATE = ["w_br_ssd", "w_br_lru", "w_out", "w_mlp1", "w_mlp2"]


def _local_step(x, c, ctx, tgt, sm, wmod, win, late, late_are_shards=False, reducer=None):
    nb, t, _ = x.shape
    tc = ctx.shape[1]
    nl, ncx = nb * t, nb * tc
    rt = 256 if tc % 256 == 0 else 128
    rtm = 128
    xl, xc = x.reshape(nl, D), ctx.reshape(ncx, D)
    tgt2 = tgt.reshape(nl, D)
    cc = jnp.zeros((8, D), F32).at[0:nb].set(c).at[nb].set(sm["c_ctx"])
    m = _mod_fwd(cc, wmod, sm["b_mod"])
    m4 = m.reshape(8, N_MOD, 1, D)
    proj, h1 = _inproj_fwd(xl, m4, win, rtm, nl // rtm, t // rtm, nb, "inproj_fwd_lat")
    proj_c, h1_c = _inproj_fwd(xc, m4, win, rtm, 0, 1, nb, "inproj_fwd_ctx")

    cw_s, cb_s = _pack_conv(sm["ssd_conv_w"]), _pack_conv(sm["ssd_conv_b"])
    dtb, alog = _pack_heads(sm["ssd_dt_bias"]), _pack_heads(sm["ssd_a_log"])
    drow = jnp.repeat(sm["ssd_d"].reshape(32), 64).reshape(SSD_G, 1, 256)
    dtg, dtg_c = _pack_dt(proj[:, P_DT:P_DT + 64]), _pack_dt(proj_c[:, P_DT:P_DT + 64])
    zst = jnp.zeros((nb, SSD_G, SSD_N, 256), F32)
    zl = jnp.zeros((nb, 2, D), F32)
    ssd_p = (cw_s, cb_s, dtb, alog, drow)
    lru_p = (sm["lru_conv_w"], sm["lru_conv_b"], sm["lru_wa"], sm["lru_ba"], sm["lru_wi"], sm["lru_bi"], sm["lru_lambda"])

    chsf, chsb, csf, csb = _ssd_fwd(proj_c, dtg_c, *ssd_p, zst, zst, nb, tc, tc, 0, False)
    y, lhsf, lhsb, _, _, *got = _ssd_fwd(proj, dtg, *ssd_p, csf, csb, nb, t, GRID_W, 0, True,
                                         tuple(late) if late_are_shards else ())
    wbs, wbl, wo, w1, w2 = [_full_from_chips(g, n) for g, n in zip(got, LATE)] if late_are_shards else late
    chf, chb, cfin = _lru_fwd(proj_c, *lru_p, zl, nb, tc, tc, 0, False)
    ylru, lhf, lhb, _ = _lru_fwd(proj, *lru_p, cfin, nb, t, GRID_W, 0, True)
    mix_w = (wbs, wbl, wo, sm["ssd_norm_w"], sm["b_gate"], sm["ln1_g"], sm["ln1_b"])
    x1, nrm, gl, mixb, brs, brl, xmix = _mix_fwd(y, ylru, proj, xl, m4, *mix_w, rtm, t // rtm)
    (loss, dx1, h2, da1, r2, dmlp, dm2, db1, db2, dl2g, dl2b) = _mlp_step(
        x1, tgt2, m4, w1, sm["b_mlp1"], w2, sm["b_mlp2"], sm["ln2_g"], sm["ln2_b"], rt, t // rt)

    dproj = lax.empty((nl, P_W), BF16)
    dproj_c = jnp.zeros((ncx, P_W), BF16)
    (dproj, dy, dylru, dxres, dbrs, dbrl, dxm, dg1, dnw, dbg, dl1g, dl1b) = _mix_bwd(
        y, ylru, proj, xl, m4, *mix_w, brs, brl, xmix, dx1, dproj, rtm, t // rtm)
    big = {
        "w_br_ssd": _matmul_tn(nrm, dbrs, D, D, _tk(nl), "dw_br_ssd"),
        "w_br_lru": _matmul_tn(gl, dbrl, D, D, _tk(nl), "dw_br_lru"),
        "w_out": _matmul_tn(mixb, dxm, D, D, _tk(nl), "dw_out"),
        "w_mlp1": _matmul_tn(h2, da1, D, D, _tk(nl), "dw_mlp1"),
        "w_mlp2": _matmul_tn(r2, dmlp, D, D, _tk(nl), "dw_mlp2"),
    }
    (dproj, ddt_l, dh0f, dh0b, dcw_l, dcb_l, ddtb_l, dal_l, dd) = _ssd_bwd(
        proj, dtg, *ssd_p, lhsf, lhsb, dy, zst, zst, dproj, nb, t, GRID_W, 0, True)
    (dproj_c, ddt_c, _, _, dcw_c, dcb_c, ddtb_c, dal_c, _) = _ssd_bwd(
        proj_c, dtg_c, *ssd_p, chsf, chsb, None, dh0f, dh0b, dproj_c, nb, tc, tc, 0, False)
    side = reducer.begin(big, list(big)) if reducer else None
    (dproj, dlh0, gcw_l, gcb_l, gwa_l, gwi_l, gba_l, gbi_l, glam_l, *got) = _lru_bwd(
        proj, *lru_p, cfin, lhf, lhb, dylru, zl, dproj, nb, t, GRID_W, 0, True, side)
    if reducer:
        reducer.end(got)
    (dproj_c, _, gcw_c, gcb_c, gwa_c, gwi_c, gba_c, gbi_c, glam_c) = _lru_bwd(
        proj_c, *lru_p, zl, chf, chb, None, dlh0, dproj_c, nb, tc, tc, 0, False)
    pad_dt = lambda d: jnp.pad(_unpack_dt(d).astype(BF16), ((0, 0), (0, P_W - P_DT - 64)))
    dproj = lax.dynamic_update_slice(dproj, pad_dt(ddt_l), (0, P_DT))
    dproj_c = lax.dynamic_update_slice(dproj_c, pad_dt(ddt_c), (0, P_DT))

    big["w_in"] = _matmul_tn2(h1, dproj, h1_c, dproj_c, D, 1152, min(_tk(nl), _tk(ncx)), "dw_in")
    side = reducer.begin(big, ["w_in"]) if reducer else None
    gx, dm1, *got = _inproj_bwd(xl, m4, win, dproj, rt, 0, nl // rt, t // rt, nb, True, dxres, side)
    if reducer:
        reducer.end(got)
    (dmc,) = _inproj_bwd(xc, m4, win, dproj_c, rt, 0, ncx // rt, ncx // rt, nb, False, None)
    dm = jnp.zeros((8, N_MOD, D), F32)
    dm = dm.at[0:nb].set(jnp.concatenate([dm1, dg1, dm2], axis=1)).at[nb, 0:2].set(dmc[0])
    dwmod, dbmod, dcc = _mod_bwd(cc, wmod, dm.reshape(8, N_MOD * D))
    big["w_mod"] = dwmod
    nq = D // LRU_CB
    small = {
        "c_ctx": dcc[nb],
        "b_mod": dbmod,
        "b_gate": dbg,
        "ssd_conv_w": _unpack_conv(dcw_l + dcw_c),
        "ssd_conv_b": _unpack_conv(dcb_l + dcb_c),
        "ssd_dt_bias": _unpack_heads(ddtb_l + ddtb_c),
        "ssd_a_log": _unpack_heads(dal_l + dal_c),
        "ssd_d": jnp.sum(dd.reshape(32, 64), axis=1),
        "ssd_norm_w": dnw,
        "lru_conv_w": jnp.transpose(gcw_l + gcw_c, (1, 0, 2)).reshape(4, D),
        "lru_conv_b": (gcb_l + gcb_c).reshape(1, D),
        "lru_wa": gwa_l + gwa_c,
        "lru_ba": jnp.transpose(gba_l + gba_c, (1, 0, 2)).reshape(2, D),
        "lru_wi": gwi_l + gwi_c,
        "lru_bi": jnp.transpose(gbi_l + gbi_c, (1, 0, 2)).reshape(2, D),
        "lru_lambda": jnp.transpose(glam_l + glam_c, (1, 0, 2)).reshape(2, D),
        "ln1_g": dl1g, "ln1_b": dl1b, "b_mlp1": db1, "b_mlp2": db2, "ln2_g": dl2g, "ln2_b": dl2b,
    }
    return loss[0, 0], gx.reshape(nb, t, D), big, small


_HBM = BS(memory_space=pl.ANY)


def _place():
    return lax.axis_index("x"), lax.axis_index("y"), lax.axis_index("c")


def _other_chips(x, y):
    return [(1 - x, y), (x, 1 - y), (1 - x, 1 - y)]


def _gather_chips(arrs):
    n = len(arrs)

    def body(*refs):
        ex = _GatherExchange(refs[:n], refs[n:2 * n], refs[2 * n:])
        ex.begin()
        ex.finish()

    return pl.pallas_call(
        body, name="gather_weights", in_specs=[_HBM] * n, out_specs=[_HBM] * n,
        out_shape=_gather_out_shapes(arrs), scratch_shapes=_gather_sems(n),
    )(*arrs)


def _gather_out_shapes(arrs):
    return [S((4,) + a.shape, a.dtype) for a in arrs]


def _gather_sems(n):
    return [pltpu.SemaphoreType.DMA((3 * n,))] * 4 + [pltpu.SemaphoreType.DMA((n,))]


class _GatherExchange:
    def __init__(self, ins, outs, sems):
        self.ins, self.outs = ins, outs
        self.ici_send, self.ici_recv, self.d2d_send, self.d2d_recv, self.loc_sems = sems
        self.x, self.y, self.c = _place()
        self.me = 2 * self.x + self.y
        self.chips = _other_chips(self.x, self.y)

    def _half(self, a, which):
        hr = self.ins[a].shape[0] // 2
        return pl.ds(pl.multiple_of((self.c if which == 0 else 1 - self.c) * hr, 8), hr)

    def _local(self, a):
        return pltpu.make_async_copy(self.ins[a], self.outs[a].at[self.me], self.loc_sems.at[a])

    def _ici(self, a, k, slot):
        px, py = self.chips[k]
        mine = self._half(a, 0)
        return pltpu.make_async_remote_copy(src_ref=self.ins[a].at[mine], dst_ref=self.outs[a].at[slot, mine],
                                            send_sem=self.ici_send.at[3 * a + k], recv_sem=self.ici_recv.at[3 * a + k],
                                            device_id=(px, py, self.c), device_id_type=MESH)

    def _d2d(self, a, k, which):
        px, py = self.chips[k]
        rows = self.outs[a].at[2 * px + py, self._half(a, which)]
        return pltpu.make_async_remote_copy(src_ref=rows, dst_ref=rows, send_sem=self.d2d_send.at[3 * a + k],
                                            recv_sem=self.d2d_recv.at[3 * a + k],
                                            device_id=(self.x, self.y, 1 - self.c), device_id_type=MESH)

    def begin(self):
        for a in range(len(self.ins)):
            self._local(a).start()
            for k in range(3):
                self._ici(a, k, self.me).start()

    def finish(self):
        n = len(self.ins)
        for a in range(n):
            for k, (px, py) in enumerate(self.chips):
                self._ici(a, k, 2 * px + py).wait_recv()
                self._d2d(a, k, 0).start()
        for a in range(n):
            for k in range(3):
                self._d2d(a, k, 1).wait_recv()
        for a in range(n):
            self._local(a).wait()
            for k in range(3):
                self._ici(a, k, self.me).wait_send()
                self._d2d(a, k, 0).wait_send()


def _scatter_chips(arrs):
    side = _Side("scatter", arrs)
    n = len(arrs)

    def body(*refs):
        ex = side.make(refs[:n], refs[n:2 * n], refs[2 * n:])
        ex.begin()
        ex.finish()

    return pl.pallas_call(
        body, name="scatter_grads", in_specs=[_HBM] * n, out_specs=[_HBM] * n,
        out_shape=side.out_shapes, scratch_shapes=side.sems,
    )(*arrs)


class _ScatterExchange:
    def __init__(self, ins, outs, sems):
        self.ins, self.outs = ins, outs
        self.send_sems, self.recv_sems = sems
        x, y, self.c = _place()
        self.chips = _other_chips(x, y)

    def _copy(self, a, k):
        px, py = self.chips[k]
        return pltpu.make_async_remote_copy(src_ref=self.ins[a].at[2 * px + py], dst_ref=self.outs[a].at[k],
                                            send_sem=self.send_sems.at[3 * a + k], recv_sem=self.recv_sems.at[3 * a + k],
                                            device_id=(px, py, self.c), device_id_type=MESH)

    def begin(self):
        for a in range(len(self.ins)):
            for k in range(3):
                self._copy(a, k).start()

    def finish(self):
        for a in range(len(self.ins)):
            for k in range(3):
                self._copy(a, k).wait_recv()
        for a in range(len(self.ins)):
            for k in range(3):
                self._copy(a, k).wait_send()


class _Side:
    def __init__(self, kind, arrays):
        self.arrays = list(arrays)
        n = len(self.arrays)
        if kind == "gather":
            self.out_shapes, self.sems, self.make = _gather_out_shapes(self.arrays), _gather_sems(n), _GatherExchange
        else:
            self.out_shapes = [S((3,) + a.shape[1:], a.dtype) for a in self.arrays]
            self.sems = [pltpu.SemaphoreType.DMA((3 * n,))] * 2
            self.make = _ScatterExchange

    def split(self, refs, n_in, n_out, n_scr):
        a, b = len(self.arrays), len(self.out_shapes)
        i1 = n_in + a
        o1 = i1 + n_out
        o2 = o1 + b
        s1 = o2 + n_scr
        own = tuple(refs[:n_in]) + tuple(refs[i1:o1]) + tuple(refs[o2:s1])
        return own, self.make(refs[n_in:i1], refs[o1:o2], refs[s1:])

    def around(self, exchange, step, n_steps, compute):
        pl.when(step == 0)(exchange.begin)
        compute()
        pl.when(step == n_steps - 1)(exchange.finish)


def _swap_halves(arrs, name):
    n = len(arrs)

    def body(*refs):
        ins, outs = refs[:n], refs[n:2 * n]
        send_sems, recv_sems = refs[2 * n:]
        x, y, c = _place()
        sends = []
        for a in range(n):
            hr = arrs[a].shape[1] // 2
            theirs = pl.ds(pl.multiple_of((1 - c) * hr, 8), hr)
            for q in range(4):
                cp = pltpu.make_async_remote_copy(src_ref=ins[a].at[q, theirs], dst_ref=outs[a].at[q],
                                                  send_sem=send_sems.at[4 * a + q], recv_sem=recv_sems.at[4 * a + q],
                                                  device_id=(x, y, 1 - c), device_id_type=MESH)
                cp.start()
                sends.append(cp)
        for cp in sends:
            cp.wait_recv()
        for cp in sends:
            cp.wait_send()

    return pl.pallas_call(
        body, name=name, in_specs=[_HBM] * n, out_specs=[_HBM] * n,
        out_shape=[S((4, a.shape[1] // 2, a.shape[2]), a.dtype) for a in arrs],
        scratch_shapes=[pltpu.SemaphoreType.DMA((4 * n,)), pltpu.SemaphoreType.DMA((4 * n,))],
    )(*arrs)


def _allreduce_small(vs, bf16_over_ici):
    n = len(vs)

    def body(*refs):
        ins, outs, bufs, sendb = refs[:n], refs[n:2 * n], refs[2 * n:5 * n], refs[5 * n:6 * n]
        send_sems, recv_sems = refs[6 * n:]
        x, y, c = _place()
        srcs = list(ins)
        for s, peer in enumerate(((x, y, 1 - c), (x, 1 - y, c), (1 - x, y, c))):
            copies = []
            for a in range(n):
                src = srcs[a]
                if s > 0 and bf16_over_ici[a]:
                    sendb[a][...] = src[...].astype(BF16)
                    src = sendb[a]
                copies.append(pltpu.make_async_remote_copy(
                    src_ref=src, dst_ref=bufs[3 * a + s], send_sem=send_sems.at[3 * a + s],
                    recv_sem=recv_sems.at[3 * a + s], device_id=peer, device_id_type=MESH))
                copies[-1].start()
            for a, cp in enumerate(copies):
                cp.wait()
                mine = sendb[a] if s > 0 and bf16_over_ici[a] else srcs[a]
                outs[a][...] = mine[...].astype(F32) + bufs[3 * a + s][...].astype(F32)
            srcs = list(outs)

    vm = BS(memory_space=pltpu.VMEM)
    wire = lambda a, s: BF16 if s > 0 and bf16_over_ici[a] else F32
    return pl.pallas_call(
        body, name="allreduce_small", in_specs=[vm] * n, out_specs=[vm] * n, out_shape=[S(v.shape, F32) for v in vs],
        scratch_shapes=[pltpu.VMEM(v.shape, wire(a, s)) for a, v in enumerate(vs) for s in range(3)]
        + [pltpu.VMEM(v.shape if bf16_over_ici[a] else (16, 128), BF16) for a, v in enumerate(vs)]
        + [pltpu.SemaphoreType.DMA((3 * n,)), pltpu.SemaphoreType.DMA((3 * n,))],
        compiler_params=_params(),
    )(*vs)


def _swap_cores(arrs):
    n = len(arrs)

    def body(*refs):
        ins, outs = refs[:n], refs[n:2 * n]
        send_sems, recv_sems = refs[2 * n:]
        x, y, c = _place()
        sends = []
        for a in range(n):
            cp = pltpu.make_async_remote_copy(src_ref=ins[a], dst_ref=outs[a], send_sem=send_sems.at[a],
                                              recv_sem=recv_sems.at[a], device_id=(x, y, 1 - c), device_id_type=MESH)
            cp.start()
            sends.append(cp)
        for cp in sends:
            cp.wait_recv()
        for cp in sends:
            cp.wait_send()

    return pl.pallas_call(
        body, name="swap_cores", in_specs=[_HBM] * n, out_specs=[_HBM] * n,
        out_shape=[S(a.shape, a.dtype) for a in arrs],
        scratch_shapes=[pltpu.SemaphoreType.DMA((n,)), pltpu.SemaphoreType.DMA((n,))],
    )(*arrs)


def _row_tile(r, c=128):
    tr = 256 if c <= 1024 else (128 if c <= 2048 else 64)
    return tr if r % tr == 0 else r


def _sum_half(own, sib, core, name):
    _, r, c = own.shape
    hr = r // 2
    tr = _row_tile(hr, c)
    nbk = hr // tr

    def body(core_ref, o_ref, s_ref, p_ref, pb_ref):
        p = o_ref[...] + s_ref[...]
        p_ref[...] = p
        pb_ref[...] = p.astype(BF16)

    blk = BS((None, tr, c), lambda q, i, cr: (q, i, 0))
    return pl.pallas_call(
        body, name=name, out_shape=[S((4, hr, c), F32), S((4, hr, c), BF16)],
        grid_spec=pltpu.PrefetchScalarGridSpec(
            num_scalar_prefetch=1, grid=(4, nbk),
            in_specs=[BS((None, tr, c), lambda q, i, cr: (q, cr[0] * nbk + i, 0)), blk], out_specs=[blk, blk]),
        compiler_params=_params(),
    )(core, own, sib)


def _sum4(part, recv, chip, name):
    _, r, c = part.shape
    tr = _row_tile(r, c)

    def body(chip_ref, o_ref, r_ref, out_ref):
        acc = o_ref[...]
        for k in range(3):
            acc = acc + r_ref[k].astype(F32)
        out_ref[...] = acc

    return pl.pallas_call(
        body, name=name, out_shape=S((r, c), F32),
        grid_spec=pltpu.PrefetchScalarGridSpec(
            num_scalar_prefetch=1, grid=(r // tr,),
            in_specs=[BS((None, tr, c), lambda i, ch: (ch[0], i, 0)), BS((3, tr, c), lambda i, ch: (0, i, 0))],
            out_specs=BS((tr, c), lambda i, ch: (i, 0))),
        compiler_params=_params(),
    )(chip, part, recv)


def _adam_math(w, g, m, v):
    m = ADAM_B1 * m + (1.0 - ADAM_B1) * g
    v = ADAM_B2 * v + (1.0 - ADAM_B2) * (g * g)
    m_hat = m / (1.0 - ADAM_B1 ** ADAM_STEP)
    v_hat = v / (1.0 - ADAM_B2 ** ADAM_STEP)
    return -ADAM_LR * (m_hat / (jnp.sqrt(v_hat) + ADAM_EPS) + ADAM_WD * w), m, v


def _adam_halves(mine, other, w, m, v, core, name):
    r, c = w.shape
    tr = _row_tile(r // 2, c)
    nbk = (r // 2) // tr

    def body(core_ref, a_ref, b_ref, w_ref, m_ref, v_ref, g_ref, d_ref, nm_ref, nv_ref):
        g = jnp.where(pl.program_id(0) // nbk == core_ref[0], a_ref[...], b_ref[...])
        g_ref[...] = g
        d_ref[...], nm_ref[...], nv_ref[...] = _adam_math(w_ref[...], g, m_ref[...], v_ref[...])

    spec = BS((tr, c), lambda i, cr: (i, 0))
    half = BS((tr, c), lambda i, cr: (i % nbk, 0))
    return pl.pallas_call(
        body, name=name, out_shape=[S((r, c), F32)] * 4,
        grid_spec=pltpu.PrefetchScalarGridSpec(num_scalar_prefetch=1, grid=(r // tr,), in_specs=[half, half] + [spec] * 3,
                                               out_specs=[spec] * 4),
        compiler_params=_params(),
    )(core, mine, other, w, m, v)


def _adam_flat(g, w, m, v, name):
    r = w.shape[0]
    tr = _row_tile(r)

    def body(g_ref, w_ref, m_ref, v_ref, d_ref, nm_ref, nv_ref):
        d_ref[...], nm_ref[...], nv_ref[...] = _adam_math(w_ref[...], g_ref[...], m_ref[...], v_ref[...])

    spec = BS((tr, 128), lambda i: (i, 0))
    return pl.pallas_call(
        body, grid=(r // tr,), name=name, in_specs=[spec] * 4, out_specs=[spec] * 3,
        out_shape=[S((r, 128), F32)] * 3, compiler_params=_params(),
    )(g, w, m, v)


def _flatten(arrs, rows_mult=256):
    flat = jnp.concatenate([a.reshape(-1) for a in arrs])
    n = flat.shape[0]
    rows = -(-n // 128)
    rows = -(-rows // rows_mult) * rows_mult
    return jnp.pad(flat, (0, rows * 128 - n)).reshape(rows, 128)


def _unflatten(flat, shapes):
    flat = flat.reshape(-1)
    out, o = [], 0
    for shp in shapes:
        n = int(np.prod(shp))
        out.append(flat[o:o + n].reshape(shp))
        o += n
    return out


BIG = ["w_mod", "w_in", "w_br_ssd", "w_br_lru", "w_out", "w_mlp1", "w_mlp2"]
COL_SHARDED = {"w_mod": N_MOD * D, "w_in": IN_COLS, "w_mlp1": MLP_H}
SMALL_SHARDED = ["ssd_conv_w", "lru_conv_w", "lru_ba", "lru_bi", "lru_lambda"]
WEIGHTS = ['c_ctx', 'w_mod', 'b_mod', 'w_in', 'b_gate', 'ssd_conv_w', 'ssd_conv_b', 'ssd_dt_bias', 'ssd_a_log', 'ssd_d',
           'ssd_norm_w', 'lru_conv_w', 'lru_conv_b', 'lru_wa', 'lru_ba', 'lru_wi', 'lru_bi', 'lru_lambda', 'w_br_ssd',
           'w_br_lru', 'w_out', 'ln1_g', 'ln1_b', 'w_mlp1', 'b_mlp1', 'w_mlp2', 'b_mlp2', 'ln2_g', 'ln2_b']
SMALL = [n for n in WEIGHTS if n not in BIG]
GATE_STACKS = ["lru_wa", "lru_wi"]


class _Reducer:
    def __init__(self, core_id, chip_id):
        self.core_id, self.chip_id = core_id, chip_id
        self.halves, self.recv, self.pending = {}, {}, []

    def _chip_sums(self, big, names):
        slabs = [_chips_from_full(_unpack_win(big[n]) if n == "w_in" else big[n], n) for n in names]
        for n, s, o in zip(names, slabs, _swap_halves(slabs, "swap_halves_" + names[0])):
            self.halves[n] = _sum_half(s, o, self.core_id, "half_" + n)
        return [self.halves[n][1] for n in names]

    def begin(self, big, names):
        self.pending = list(names)
        return _Side("scatter", self._chip_sums(big, names))

    def end(self, received):
        self.recv.update(zip(self.pending, received))

    def finish(self, big, names):
        self.recv.update(zip(names, _scatter_chips(self._chip_sums(big, names))))
        mine = [_sum4(self.halves[n][0], self.recv[n], self.chip_id, "sum_" + n) for n in BIG]
        return dict(zip(BIG, mine)), dict(zip(BIG, _swap_cores(mine)))


def _full_from_chips(g4, name):
    if name in COL_SHARDED:
        return jnp.transpose(g4, (1, 0, 2)).reshape(g4.shape[1], 4 * g4.shape[2])
    return g4.reshape(4 * g4.shape[1], g4.shape[2])


def _chips_from_full(full, name):
    if name in COL_SHARDED:
        r, c = full.shape
        return jnp.transpose(full.reshape(r, 4, c // 4), (1, 0, 2))
    return full.reshape(4, full.shape[0] // 4, full.shape[1])


def kernel(x, c, ctx, c_ctx, w_mod, b_mod, w_in, b_gate, ssd_conv_w, ssd_conv_b, ssd_dt_bias, ssd_a_log, ssd_d, ssd_norm_w, lru_conv_w, lru_conv_b, lru_wa, lru_ba, lru_wi, lru_bi, lru_lambda, w_br_ssd, w_br_lru, w_out, ln1_g, ln1_b, w_mlp1, b_mlp1, w_mlp2, b_mlp2, ln2_g, ln2_b, loss_target, m_c_ctx, m_w_mod, m_b_mod, m_w_in, m_b_gate, m_ssd_conv_w, m_ssd_conv_b, m_ssd_dt_bias, m_ssd_a_log, m_ssd_d, m_ssd_norm_w, m_lru_conv_w, m_lru_conv_b, m_lru_wa, m_lru_ba, m_lru_wi, m_lru_bi, m_lru_lambda, m_w_br_ssd, m_w_br_lru, m_w_out, m_ln1_g, m_ln1_b, m_w_mlp1, m_b_mlp1, m_w_mlp2, m_b_mlp2, m_ln2_g, m_ln2_b, v_c_ctx, v_w_mod, v_b_mod, v_w_in, v_b_gate, v_ssd_conv_w, v_ssd_conv_b, v_ssd_dt_bias, v_ssd_a_log, v_ssd_d, v_ssd_norm_w, v_lru_conv_w, v_lru_conv_b, v_lru_wa, v_lru_ba, v_lru_wi, v_lru_bi, v_lru_lambda, v_w_br_ssd, v_w_br_lru, v_w_out, v_ln1_g, v_ln1_b, v_w_mlp1, v_b_mlp1, v_w_mlp2, v_b_mlp2, v_ln2_g, v_ln2_b):
    given = dict(locals())
    w = {n: given[n] for n in WEIGHTS}
    mom = {n: given["m_" + n] for n in WEIGHTS}
    var = {n: given["v_" + n] for n in WEIGHTS}
    chip = 2 * lax.axis_index("x") + lax.axis_index("y")

    shard2d = {n: w[n].reshape(w[n].shape[-2:]) for n in BIG}
    small_pack = _flatten([w[n] for n in SMALL_SHARDED], rows_mult=16)
    first = ["w_mod", "w_in"]
    gathered = _gather_chips([shard2d[n].astype(BF16) for n in first] + [small_pack])
    full = {n: _full_from_chips(g, n) for n, g in zip(first, gathered[:-1])}
    full["w_in"] = _pack_win(full["w_in"])
    per_chip = [_unflatten(gathered[-1][q], [w[n].shape for n in SMALL_SHARDED]) for q in range(4)]
    sm = {n: jnp.concatenate([per_chip[q][i] for q in range(4)], axis=-1) for i, n in enumerate(SMALL_SHARDED)}
    for n in SMALL:
        if n not in sm:
            sm[n] = w[n]
    sm = {n: (a.reshape(a.shape[1:]) if a.ndim >= 3 else a) for n, a in sm.items()}

    core_id = lax.axis_index("c").astype(jnp.int32).reshape(1)
    reducer = _Reducer(core_id, chip.astype(jnp.int32).reshape(1))
    loss, gx, gbig, gsmall = _local_step(x, c, ctx, loss_target, sm, full["w_mod"], full["w_in"],
                                         [shard2d[n].astype(BF16) for n in LATE], late_are_shards=True, reducer=reducer)
    mine, other = reducer.finish(gbig, ["w_mod"])
    out = {}
    for n in BIG:
        shp = shard2d[n].shape
        res = _adam_halves(mine[n], other[n], shard2d[n], mom[n].reshape(shp), var[n].reshape(shp), core_id, "adam_" + n)
        out[n] = [r.reshape(w[n].shape) for r in res]

    tiny = [n for n in SMALL if n not in GATE_STACKS]
    as_rows = lambda a: a.reshape(-1, 128)
    summed = _allreduce_small([_flatten([gsmall[n] for n in tiny])] + [as_rows(gsmall[n]) for n in GATE_STACKS],
                              [False] + [True] * len(GATE_STACKS))
    gs = {}
    for n, g in zip(tiny, _unflatten(summed[0], [gsmall[n].shape for n in tiny])):
        if n in SMALL_SHARDED:
            width = w[n].shape[-1]
            g = lax.dynamic_slice_in_dim(g, chip * width, width, axis=g.ndim - 1)
        gs[n] = g.reshape(w[n].shape)
    shapes = [w[n].shape for n in tiny]
    d_s, m_s, v_s = _adam_flat(_flatten([gs[n] for n in tiny]), _flatten([w[n] for n in tiny]),
                               _flatten([mom[n] for n in tiny]), _flatten([var[n] for n in tiny]), "adam_small")
    for n, d_, m_, v_ in zip(tiny, _unflatten(d_s, shapes), _unflatten(m_s, shapes), _unflatten(v_s, shapes)):
        out[n] = [gs[n], d_, m_, v_]
    for n, g in zip(GATE_STACKS, summed[1:]):
        res = _adam_flat(g, as_rows(w[n]), as_rows(mom[n]), as_rows(var[n]), "adam_" + n)
        out[n] = [r.reshape(w[n].shape) for r in (g, *res)]

    loss = lax.psum(loss, ("x", "y", "c"))
    return (loss, gx, *[out[n][0] for n in WEIGHTS], *[out[n][1] for n in WEIGHTS], *[out[n][2] for n in WEIGHTS],
            *[out[n][3] for n in WEIGHTS])
```

```python
import functools

import numpy as np
import jax
import jax.numpy as jnp
from jax import lax
from jax.experimental import pallas as pl
from jax.experimental.pallas import tpu as pltpu

F32, BF16 = jnp.float32, jnp.bfloat16
S = jax.ShapeDtypeStruct
BS = pl.BlockSpec
MESH = pl.DeviceIdType.MESH

D = 1024
GRID_W = 64
SSD_INNER, SSD_G, SSD_N, SSD_L = 2048, 8, 128, 128
SSD_GW = 512
MLP_H = 4096
N_MOD = 6
ALPHA = 2.0 ** 0.25
LN_EPS, RMS_EPS = 1e-6, 1e-5
LRU_C = 8.0
P_XBC, P_LRU, P_Z, P_LG, P_MG, P_DT, P_W = 0, 4096, 5120, 7168, 8192, 10240, 10368
P_CB = 3456
IN_COLS = 10304
LRU_CB = 256
ADAM_LR, ADAM_B1, ADAM_B2, ADAM_EPS, ADAM_WD, ADAM_STEP = 0.001, 0.9, 0.999, 1e-08, 0.01, 10
VMEM_LIMIT = 56 * 2 ** 20


def _params(**kw):
    return pltpu.CompilerParams(vmem_limit_bytes=VMEM_LIMIT, **kw)


def _dot(a, b):
    return jnp.dot(a.astype(BF16), b.astype(BF16), preferred_element_type=F32)


def _dot_nt(a, b):
    return lax.dot_general(a.astype(BF16), b.astype(BF16), (((1,), (1,)), ((), ())), preferred_element_type=F32)


def _dot_tn(a, b):
    return lax.dot_general(a.astype(BF16), b.astype(BF16), (((0,), (0,)), ((), ())), preferred_element_type=F32)


@jax.custom_vjp
def _mm(a, b):
    return _dot(a, b)


def _cast_pair(a, b):
    return a.astype(BF16), b.astype(BF16)


def _mm_f(a, b):
    r = _cast_pair(a, b)
    return _dot(*r), r


def _mm_b(r, g):
    g = g.astype(BF16)
    return _dot_nt(g, r[1]), _dot_tn(r[0], g)


_mm.defvjp(_mm_f, _mm_b)


@jax.custom_vjp
def _mm_nt(a, b):
    return _dot_nt(a, b)


def _mm_nt_f(a, b):
    r = _cast_pair(a, b)
    return _dot_nt(*r), r


def _mm_nt_b(r, g):
    g = g.astype(BF16)
    return _dot(g, r[1]), _dot_tn(g, r[0])


_mm_nt.defvjp(_mm_nt_f, _mm_nt_b)


@jax.custom_vjp
def _mm_tn(a, b):
    return _dot_tn(a, b)


def _mm_tn_f(a, b):
    r = _cast_pair(a, b)
    return _dot_tn(*r), r


def _mm_tn_b(r, g):
    g = g.astype(BF16)
    return _dot_nt(r[1], g), _dot(r[0], g)


_mm_tn.defvjp(_mm_tn_f, _mm_tn_b)

def _split3(v):
    h = v.astype(BF16)
    r = v - h.astype(F32)
    m = r.astype(BF16)
    return h, m, (r - m.astype(F32)).astype(BF16)


def _sel_dot(sel, v, dims):
    sel_first = dims[0] == "s"
    dn = {"sv": (((1,), (0,)), ((), ())), "sTv": (((0,), (0,)), ((), ())), "vs": (((1,), (0,)), ((), ())),
          "vsT": (((1,), (1,)), ((), ()))}[dims]
    out = None
    for part in _split3(v):
        a, b = (sel, part) if sel_first else (part, sel)
        term = lax.dot_general(a, b, dn, preferred_element_type=F32)
        out = term if out is None else out + term
    return out


@jax.custom_vjp
def _cum_mm(tri, v):
    return _sel_dot(tri, v, "sv")


_cum_mm.defvjp(lambda tri, v: (_sel_dot(tri, v, "sv"), tri),
               lambda tri, g: (jnp.zeros_like(tri), _sel_dot(tri, g, "sTv")))


@jax.custom_vjp
def _xp_mm(v, e):
    return _sel_dot(e, v, "vs")


_xp_mm.defvjp(lambda v, e: (_sel_dot(e, v, "vs"), e),
              lambda e, g: (_sel_dot(e, g, "vsT"), jnp.zeros_like(e)))


def _sigmoid(x):
    return 1.0 / (1.0 + jnp.exp(-x))


def _silu(x):
    return x * _sigmoid(x)


def _dsilu(x):
    s = _sigmoid(x)
    return s * (1.0 + x * (1.0 - s))


def _softplus(x):
    return jnp.maximum(x, 0.0) + jnp.log1p(jnp.exp(-jnp.abs(x)))


def _gelu(x):
    return 0.5 * x * (1.0 + jnp.tanh(0.7978845608028654 * (x + 0.044715 * x * x * x)))


def _ln(x):
    mu = jnp.mean(x, axis=-1, keepdims=True)
    xc = x - mu
    var = jnp.mean(xc * xc, axis=-1, keepdims=True)
    return xc * lax.rsqrt(var + LN_EPS)


def _modln(x, shift, scale):
    return _ln(x) * (1.0 + scale) + shift


def _resln(x, sub, gate, g, b):
    return _ln(ALPHA * x + gate * sub) * g + b


def _grms(y, z, w):
    u = y * _silu(z)
    return u * lax.rsqrt(jnp.mean(u * u, axis=-1, keepdims=True) + RMS_EPS) * w


def _colsum(v):
    return jnp.sum(v, axis=0, keepdims=True)


def _conv_taps(width, period):
    def masks(rows):
        pos = lax.broadcasted_iota(jnp.int32, (rows, width), 0)
        if rows != period:
            pos = pos & (period - 1)
        return [pos >= 2 - k if k < 2 else pos < period + 2 - k for k in range(4)]
    return masks


def _conv_fwd(raw, w, b, masks):
    rows = raw.shape[0]
    pre = b + raw * w[2:3, :]
    for k in (0, 1, 3):
        sh = pltpu.roll(raw, (2 - k) % rows, 0)
        pre = pre + jnp.where(masks[k], sh, 0.0) * w[k:k + 1, :]
    return pre


def _for_rows(t, rb, fn):
    n = t // rb
    unroll = 4 if n % 4 == 0 else 1

    def step(i, carry):
        for u in range(unroll):
            fn(pl.multiple_of((i * unroll + u) * rb, rb))
        return carry

    lax.fori_loop(0, n // unroll, step, 0)


def _loop_unrolled(n, unroll, body, init):
    def step(i, carry):
        for u in range(unroll):
            carry = body(i * unroll + u, carry)
        return carry

    return lax.fori_loop(0, n // unroll, step, init)


def _conv_bwd(dpre, raw, w, masks):
    rows = raw.shape[0]
    draw = dpre * w[2:3, :]
    dws = []
    for k in range(4):
        if k == 2:
            dws.append(_colsum(dpre * raw))
            continue
        back = pltpu.roll(jnp.where(masks[k], dpre, 0.0), (k - 2) % rows, 0)
        dws.append(_colsum(back * raw))
        draw = draw + back * w[k:k + 1, :]
    return draw, jnp.concatenate(dws, axis=0), _colsum(dpre)


def _mod_fwd(cc, wmod, bmod):
    def body(cc_ref, w_ref, b_ref, o_ref):
        o_ref[...] = _dot(_silu(cc_ref[...]), w_ref[...]) + b_ref[...]

    return pl.pallas_call(
        body, grid=(N_MOD,), name="mod_fwd",
        in_specs=[BS((8, D), lambda j: (0, 0)), BS((D, D), lambda j: (0, j)), BS((1, D), lambda j: (0, j))],
        out_specs=BS((8, D), lambda j: (0, j)), out_shape=S((8, N_MOD * D), F32), compiler_params=_params(),
    )(cc, wmod, bmod)


def _mod_bwd(cc, wmod, dm):
    def body(cc_ref, w_ref, dm_ref, dw_ref, db_ref, dcc_ref):
        j = pl.program_id(0)
        c = cc_ref[...]
        dmv = dm_ref[...]
        dw_ref[...] = _dot_tn(_silu(c), dmv)
        db_ref[...] = _colsum(dmv)

        @pl.when(j == 0)
        def _():
            dcc_ref[...] = jnp.zeros_like(dcc_ref)

        dcc_ref[...] += _dot_nt(dmv, w_ref[...]) * _dsilu(c)

    return pl.pallas_call(
        body, grid=(N_MOD,), name="mod_bwd",
        in_specs=[BS((8, D), lambda j: (0, 0)), BS((D, D), lambda j: (0, j)), BS((8, D), lambda j: (0, j))],
        out_specs=[BS((D, D), lambda j: (0, j)), BS((1, D), lambda j: (0, j)), BS((8, D), lambda j: (0, 0))],
        out_shape=[S((D, N_MOD * D), F32), S((1, N_MOD * D), F32), S((8, D), F32)], compiler_params=_params(),
    )(cc, wmod, dm)


def _inproj_fwd(xa, m4, win, rt, n_lat_tiles, tiles_per_b, ctx_row, name):
    n_tiles = xa.shape[0] // rt

    def mrow(i):
        return jnp.where(i < n_lat_tiles, i // tiles_per_b, ctx_row)

    def body(x_ref, sh_ref, sc_ref, w_hbm, p_ref, h_ref, w_vm, sem):
        @pl.when(pl.program_id(0) == 0)
        def _():
            cp = pltpu.make_async_copy(w_hbm, w_vm, sem)
            cp.start()
            cp.wait()

        hb = _modln(x_ref[...], sh_ref[...], sc_ref[...]).astype(BF16)
        h_ref[...] = hb
        for j in range(P_W // P_CB):
            sl = slice(j * P_CB, (j + 1) * P_CB)
            p_ref[:, sl] = jnp.dot(hb, w_vm[:, sl], preferred_element_type=F32)

    return pl.pallas_call(
        body, grid=(n_tiles,), name=name,
        in_specs=[BS((rt, D), lambda i: (i, 0)),
                  BS((None, None, 1, D), lambda i: (mrow(i), 0, 0, 0)),
                  BS((None, None, 1, D), lambda i: (mrow(i), 1, 0, 0)),
                  BS(memory_space=pl.ANY)],
        out_specs=[BS((rt, P_W), lambda i: (i, 0)), BS((rt, D), lambda i: (i, 0))],
        out_shape=[S((xa.shape[0], P_W), F32), S((xa.shape[0], D), BF16)],
        scratch_shapes=[pltpu.VMEM((D, P_W), BF16), pltpu.SemaphoreType.DMA(())], compiler_params=_params(),
    )(xa, m4, m4, win)


def _inproj_bwd(xa, m4, win, dproj, rt, tile0, n_tiles, tiles_per_b, ctx_row, latent, dxres, side=None):
    def mrow(i):
        return (i // tiles_per_b) if latent else ctx_row

    def body(*refs):
        if side is None:
            return compute(*refs)
        own, exchange = side.split(refs, len(args), len(out_shape), 2)
        side.around(exchange, pl.program_id(0), n_tiles, lambda: compute(*own))

    def compute(x_ref, sh_ref, sc_ref, dp_ref, w_hbm, *rest):
        if latent:
            dxr_ref, gx_ref, dm_ref, w_vm, sem = rest
        else:
            dm_ref, w_vm, sem = rest
        i = pl.program_id(0)

        @pl.when(i == 0)
        def _():
            cp = pltpu.make_async_copy(w_hbm, w_vm, sem)
            cp.start()
            cp.wait()

        dh = lax.dot_general(dp_ref[...], w_vm[...], (((1,), (1,)), ((), ())), preferred_element_type=F32)
        _, vjp = jax.vjp(_modln, x_ref[...], sh_ref[...], sc_ref[...])
        dx, dsh, dsc = vjp(dh)
        if latent:
            gx_ref[...] = dx + dxr_ref[...]

        @pl.when(i % tiles_per_b == 0)
        def _():
            dm_ref[...] = jnp.zeros_like(dm_ref)

        dm_ref[0:1, :] += dsh
        dm_ref[1:2, :] += dsc

    nb = n_tiles // tiles_per_b
    in_specs = [BS((rt, D), lambda i: (tile0 + i, 0)),
                BS((None, None, 1, D), lambda i: (mrow(i), 0, 0, 0)),
                BS((None, None, 1, D), lambda i: (mrow(i), 1, 0, 0)),
                BS((rt, P_W), lambda i: (tile0 + i, 0)),
                BS(memory_space=pl.ANY)]
    args = [xa, m4, m4, dproj, win]
    dm_spec = BS((None, 2, D), lambda i: (i // tiles_per_b, 0, 0))
    if latent:
        in_specs.append(BS((rt, D), lambda i: (i, 0)))
        args.append(dxres)
        out_specs = [BS((rt, D), lambda i: (i, 0)), dm_spec]
        out_shape = [S((n_tiles * rt, D), F32), S((nb, 2, D), F32)]
    else:
        out_specs = [dm_spec]
        out_shape = [S((nb, 2, D), F32)]
    extra = side.arrays if side else []
    return pl.pallas_call(
        body, grid=(n_tiles,), name="inproj_bwd_lat" if latent else "inproj_bwd_ctx",
        in_specs=in_specs + [_HBM] * len(extra), out_specs=out_specs + [_HBM] * len(extra),
        out_shape=out_shape + (side.out_shapes if side else []),
        scratch_shapes=[pltpu.VMEM((D, P_W), BF16), pltpu.SemaphoreType.DMA(())] + (side.sems if side else []),
        compiler_params=_params(),
    )(*args, *extra)


def _matmul_tn(a, b, tm, tn, tk, name):
    k, m = a.shape
    n = b.shape[1]

    def body(a_ref, b_ref, o_ref):
        @pl.when(pl.program_id(2) == 0)
        def _():
            o_ref[...] = jnp.zeros_like(o_ref)

        o_ref[...] += lax.dot_general(a_ref[...], b_ref[...], (((0,), (0,)), ((), ())), preferred_element_type=F32)

    return pl.pallas_call(
        body, grid=(m // tm, n // tn, k // tk), name=name,
        in_specs=[BS((tk, tm), lambda i, j, kk: (kk, i)), BS((tk, tn), lambda i, j, kk: (kk, j))],
        out_specs=BS((tm, tn), lambda i, j, kk: (i, j)), out_shape=S((m, n), F32), compiler_params=_params(),
    )(a, b)


def _matmul_tn2(a1, b1, a2, b2, tm, tn, tk, name):
    k1, m = a1.shape
    n = b1.shape[1]
    n1, n2 = k1 // tk, a2.shape[0] // tk

    def body(a1_ref, b1_ref, a2_ref, b2_ref, o_ref):
        kk = pl.program_id(2)

        @pl.when(kk == 0)
        def _():
            o_ref[...] = jnp.zeros_like(o_ref)

        @pl.when(kk < n1)
        def _():
            o_ref[...] += lax.dot_general(a1_ref[...], b1_ref[...], (((0,), (0,)), ((), ())), preferred_element_type=F32)

        @pl.when(kk >= n1)
        def _():
            o_ref[...] += lax.dot_general(a2_ref[...], b2_ref[...], (((0,), (0,)), ((), ())), preferred_element_type=F32)

    first = lambda kk: jnp.minimum(kk, n1 - 1)
    second = lambda kk: jnp.maximum(kk - n1, 0)
    return pl.pallas_call(
        body, grid=(m // tm, n // tn, n1 + n2), name=name,
        in_specs=[BS((tk, tm), lambda i, j, kk: (first(kk), i)), BS((tk, tn), lambda i, j, kk: (first(kk), j)),
                  BS((tk, tm), lambda i, j, kk: (second(kk), i)), BS((tk, tn), lambda i, j, kk: (second(kk), j))],
        out_specs=BS((tm, tn), lambda i, j, kk: (i, j)), out_shape=S((m, n), F32), compiler_params=_params(),
    )(a1, b1, a2, b2)


def _ssd_consts(heads_per_tile):
    n = SSD_L
    ii = lax.broadcasted_iota(jnp.int32, (n, n), 0)
    jj = lax.broadcasted_iota(jnp.int32, (n, n), 1)
    er = lax.broadcasted_iota(jnp.int32, (128, 256), 0)
    ec = lax.broadcasted_iota(jnp.int32, (128, 256), 1) >> 6
    lane = lax.broadcasted_iota(jnp.int32, (1, 128 * heads_per_tile), 1) >> 6
    per_dir = []
    for d in (0, 1):
        mask = (jj >= ii) if d else (jj <= ii)
        per_dir.append((mask, mask.astype(BF16), (er == ec + 4 * d).astype(BF16)))
    return per_dir, [(lane == h).astype(F32) for h in range(2 * heads_per_tile)]


def _ssd_chunk(x, bm, cm, dtc, dtx, alog, hst, consts, hmasks, rev):
    n = SSD_L
    mask, tri, e = consts
    cum = _cum_mm(tri, dtc * (-jnp.exp(alog)))
    cum_x = _xp_mm(cum, e)
    tot_x = cum_x[0:1, :] if rev else cum_x[n - 1:n, :]
    xd = x * dtx
    hn = jnp.exp(tot_x) * hst + _mm_tn(bm, xd * jnp.exp(tot_x - cum_x))
    if cm is None:
        return hn
    cum_t = cum.T
    cb = _mm_nt(cm, bm)
    if len(hmasks) == 4:
        y = jnp.exp(cum_x) * _mm(cm, hst)
        for h in range(4):
            k = 4 * rev + h
            decay = jnp.exp(jnp.where(mask, cum[:, k:k + 1] - cum_t[k:k + 1, :], -1e30))
            y = y + _mm(cb * decay, xd * hmasks[h])
        return y, hn
    pairs = []
    for p in range(2):
        xdp = xd[:, 128 * p:128 * p + 128]
        yp = None
        for hh in range(2):
            k = 4 * rev + 2 * p + hh
            decay = jnp.exp(jnp.where(mask, cum[:, k:k + 1] - cum_t[k:k + 1, :], -1e30))
            term = _mm(cb * decay, xdp * hmasks[hh])
            yp = term if yp is None else yp + term
        pairs.append(yp)
    return jnp.exp(cum_x) * _mm(cm, hst) + jnp.concatenate(pairs, axis=1), hn


def _ssd_fwd(proj, dtg, cw, cb, dtb, alog, drow, h0f, h0b, nb, t, period, blk0, need_y, gather=()):
    nc = t // SSD_L
    rb = period
    assert t % rb == 0
    unroll = 4 if nc % 4 == 0 else (2 if nc % 2 == 0 else 1)
    masks_of = _conv_taps(SSD_GW, period)
    ng = len(gather)
    n_out = 5 if need_y else 4

    def body(p_ref, dt_ref, cw_ref, cb_ref, dtb_ref, al_ref, d_ref, h0f_ref, h0b_ref, *rest):
        g_ins, rest = rest[:ng], rest[ng:]
        outs, g_outs, (act, dts, dtxs), g_sems = rest[:n_out], rest[n_out:n_out + ng], rest[n_out + ng:n_out + ng + 3], \
            rest[n_out + ng + 3:]
        if need_y:
            y_ref, hsf_ref, hsb_ref, sf_ref, sb_ref = outs
        else:
            hsf_ref, hsb_ref, sf_ref, sb_ref = outs
        if ng:
            exchange = _GatherExchange(g_ins, g_outs, g_sems)
            step = pl.program_id(0) * SSD_G + pl.program_id(1)
            pl.when(step == 0)(exchange.begin)
        masks = masks_of(rb)
        per_dir, hmasks = _ssd_consts(2)

        def prologue(r0):
            rows = pl.ds(r0, rb)
            a = _silu(_conv_fwd(p_ref[rows, :], cw_ref[...], cb_ref[...], masks))
            act[rows, :] = a
            if need_y:
                y_ref[rows, :] = d_ref[...] * a[:, 0:256]
            dtv = _softplus(dt_ref[rows, :] + dtb_ref[...])
            dts[rows, :] = dtv
            for d in (0, 1):
                dtxs[rows, 256 * d:256 * d + 256] = _xp_mm(dtv, per_dir[d][2])

        _for_rows(t, rb, prologue)
        al = al_ref[...]

        def chunk(ci, carry):
            out = []
            for d, hst, hs_ref in ((0, carry[0], hsf_ref), (1, carry[1], hsb_ref)):
                c = (nc - 1 - ci) if d else ci
                r0 = pl.multiple_of(c * SSD_L, SSD_L)
                a = act[pl.ds(r0, SSD_L), :]
                hs_ref[c] = hst
                res = _ssd_chunk(a[:, 0:256], a[:, 256:384], a[:, 384:512] if need_y else None,
                                 dts[pl.ds(r0, SSD_L), :], dtxs[pl.ds(r0, SSD_L), 256 * d:256 * d + 256], al, hst,
                                 per_dir[d], hmasks, d)
                if need_y:
                    y_ref[pl.ds(r0, SSD_L), :] += res[0]
                    res = res[1]
                out.append(res)
            return tuple(out)

        sf_ref[...], sb_ref[...] = _loop_unrolled(nc, unroll, chunk, (h0f_ref[...], h0b_ref[...]))
        if ng:
            pl.when(step == nb * SSD_G - 1)(exchange.finish)

    gspec = lambda shp: BS((None,) + shp, lambda b, g: (g,) + (0,) * len(shp))
    st_spec = BS((None, None, SSD_N, 256), lambda b, g: (b, g, 0, 0))
    hs_spec = BS((None, None, nc, SSD_N, 256), lambda b, g: (b, g, 0, 0, 0))
    in_specs = [BS((t, SSD_GW), lambda b, g: (blk0 + b, g)), BS((None, t, 128), lambda b, g: (g, blk0 + b, 0)),
                gspec((4, SSD_GW)), gspec((1, SSD_GW)), gspec((1, 128)), gspec((1, 128)), gspec((1, 256)),
                st_spec, st_spec]
    out_specs = [hs_spec, hs_spec, st_spec, st_spec]
    out_shape = [S((nb, SSD_G, nc, SSD_N, 256), F32)] * 2 + [S((nb, SSD_G, SSD_N, 256), F32)] * 2
    if need_y:
        out_specs = [BS((t, 256), lambda b, g: (b, g))] + out_specs
        out_shape = [S((nb * t, SSD_INNER), F32)] + out_shape
    return pl.pallas_call(
        body, grid=(nb, SSD_G), name="ssd_fwd_lat" if need_y else "ssd_fwd_ctx",
        in_specs=in_specs + [_HBM] * ng, out_specs=out_specs + [_HBM] * ng,
        out_shape=out_shape + _gather_out_shapes(gather),
        scratch_shapes=[pltpu.VMEM((t, SSD_GW), F32), pltpu.VMEM((t, 128), F32), pltpu.VMEM((t, SSD_GW), F32)]
        + (_gather_sems(ng) if ng else []),
        compiler_params=_params(),
    )(proj, dtg, cw, cb, dtb, alog, drow, h0f, h0b, *gather)


def _ssd_bwd(proj, dtg, cw, cb, dtb, alog, drow, hsf, hsb, dy, dsf, dsb, dproj, nb, t, period, blk0, need_y, side=None):
    nc = t // SSD_L
    rb = period
    assert t % rb == 0
    unroll = 2 if nc % 2 == 0 else 1
    masks_of = _conv_taps(SSD_GW, period)

    def body(*refs):
        if side is None:
            return compute(*refs)
        own, exchange = side.split(refs, len(args), len(out_shape), 5)
        side.around(exchange, pl.program_id(0) * SSD_G + pl.program_id(1), nb * SSD_G, lambda: compute(*own))

    def compute(*refs):
        if need_y:
            (p_ref, dt_ref, cw_ref, cb_ref, dtb_ref, al_ref, d_ref, hsf_ref, hsb_ref, dy_ref, dsf_ref, dsb_ref, _,
             dp_ref, ddt_ref, dhf_ref, dhb_ref, dcw_ref, dcb_ref, ddtb_ref, dal_ref, dd_ref,
             pre, dact, dts, ddts, dtxs) = refs
        else:
            (p_ref, dt_ref, cw_ref, cb_ref, dtb_ref, al_ref, d_ref, hsf_ref, hsb_ref, dsf_ref, dsb_ref, _,
             dp_ref, ddt_ref, dhf_ref, dhb_ref, dcw_ref, dcb_ref, ddtb_ref, dal_ref, dd_ref,
             pre, dact, dts, ddts, dtxs) = refs
            dy_ref = None
        b, g = pl.program_id(0), pl.program_id(1)

        @pl.when(jnp.logical_and(b == 0, g == 0))
        def _():
            for r in (dcw_ref, dcb_ref, ddtb_ref, dal_ref, dd_ref):
                r[...] = jnp.zeros_like(r)

        masks = masks_of(rb)
        per_dir, hmasks = _ssd_consts(1)

        def prologue(r0):
            rows = pl.ds(r0, rb)
            pre[rows, :] = _conv_fwd(p_ref[rows, :], cw_ref[...], cb_ref[...], masks)
            dtv = _softplus(dt_ref[rows, :] + dtb_ref[...])
            dts[rows, :] = dtv
            for d in (0, 1):
                dtxs[rows, 256 * d:256 * d + 256] = _xp_mm(dtv, per_dir[d][2])
            dact[rows, :] = jnp.zeros((rb, SSD_GW), F32)
            ddts[rows, :] = jnp.zeros((rb, 128), F32)

        _for_rows(t, rb, prologue)
        al = al_ref[...]
        def chunk(ci, carry):
            dal_c = carry[2]
            dhs_out = []
            for d, dh, hs_ref in ((0, carry[0], hsf_ref), (1, carry[1], hsb_ref)):
                c = ci if d else (nc - 1 - ci)
                r0 = pl.multiple_of(c * SSD_L, SSD_L)
                a = _silu(pre[pl.ds(r0, SSD_L), :])
                dtc = dts[pl.ds(r0, SSD_L), :]
                dtx = dtxs[pl.ds(r0, SSD_L), 256 * d:256 * d + 256]
                if need_y:
                    fn = lambda x_, bm_, cm_, dt_, dx_, al_, hs_: _ssd_chunk(x_, bm_, cm_, dt_, dx_, al_, hs_, per_dir[d],
                                                                             hmasks, d)
                    _, vjp = jax.vjp(fn, a[:, 0:256], a[:, 256:384], a[:, 384:512], dtc, dtx, al, hs_ref[c])
                    dx, dbm, dcm, ddtc, ddtx, dal_k, dhs = vjp((dy_ref[pl.ds(r0, SSD_L), :], dh))
                    dact[pl.ds(r0, SSD_L), 384:512] += dcm
                else:
                    fn = lambda x_, bm_, dt_, dx_, al_, hs_: _ssd_chunk(x_, bm_, None, dt_, dx_, al_, hs_, per_dir[d],
                                                                        hmasks, d)
                    _, vjp = jax.vjp(fn, a[:, 0:256], a[:, 256:384], dtc, dtx, al, hs_ref[c])
                    dx, dbm, ddtc, ddtx, dal_k, dhs = vjp(dh)
                dact[pl.ds(r0, SSD_L), 0:256] += dx
                dact[pl.ds(r0, SSD_L), 256:384] += dbm
                ddts[pl.ds(r0, SSD_L), :] += ddtc + _dot_nt(ddtx, per_dir[d][2])
                dhs_out.append(dhs)
                dal_c = dal_c + dal_k
            return dhs_out[0], dhs_out[1], dal_c

        dhf_ref[...], dhb_ref[...], dal_acc = _loop_unrolled(
            nc, unroll, chunk, (dsf_ref[...], dsb_ref[...], jnp.zeros((1, 128), F32)))

        def epilogue(r0):
            rows = pl.ds(r0, rb)
            prev = pre[rows, :]
            if need_y:
                dyv = dy_ref[rows, :]
                dact[rows, 0:256] += d_ref[...] * dyv
                dd_ref[g] += _colsum(dyv * _silu(prev[:, 0:256]))
            dpre = dact[rows, :] * _dsilu(prev)
            draw, dcw, dcb = _conv_bwd(dpre, p_ref[rows, :], cw_ref[...], masks)
            dp_ref[rows, :] = draw.astype(BF16)
            dcw_ref[g] += dcw
            dcb_ref[g] += dcb
            ddraw = ddts[rows, :] * _sigmoid(dt_ref[rows, :] + dtb_ref[...])
            ddt_ref[rows, :] = ddraw
            ddtb_ref[g] += _colsum(ddraw)

        _for_rows(t, rb, epilogue)
        dal_ref[g] += dal_acc

    gspec = lambda shp: BS((None,) + shp, lambda b, g: (g,) + (0,) * len(shp))
    full = lambda shp: BS(shp, lambda b, g: (0,) * len(shp))
    st_spec = BS((None, None, SSD_N, 256), lambda b, g: (b, g, 0, 0))
    hs_spec = BS((None, None, nc, SSD_N, 256), lambda b, g: (b, g, 0, 0, 0))
    p_spec = BS((t, SSD_GW), lambda b, g: (blk0 + b, g))
    in_specs = [p_spec, BS((None, t, 128), lambda b, g: (g, blk0 + b, 0)),
                gspec((4, SSD_GW)), gspec((1, SSD_GW)), gspec((1, 128)), gspec((1, 128)), gspec((1, 256)),
                hs_spec, hs_spec]
    args = [proj, dtg, cw, cb, dtb, alog, drow, hsf, hsb]
    if need_y:
        in_specs.append(BS((t, 256), lambda b, g: (b, g)))
        args.append(dy)
    in_specs += [st_spec, st_spec, BS(memory_space=pl.ANY)]
    args += [dsf, dsb, dproj]
    out_specs = [p_spec, BS((None, t, 128), lambda b, g: (g, b, 0)), st_spec, st_spec,
                 full((SSD_G, 4, SSD_GW)), full((SSD_G, 1, SSD_GW)), full((SSD_G, 1, 128)), full((SSD_G, 1, 128)),
                 full((SSD_G, 1, 256))]
    out_shape = [S(dproj.shape, BF16), S((SSD_G, nb * t, 128), F32),
                 S((nb, SSD_G, SSD_N, 256), F32), S((nb, SSD_G, SSD_N, 256), F32),
                 S((SSD_G, 4, SSD_GW), F32), S((SSD_G, 1, SSD_GW), F32), S((SSD_G, 1, 128), F32),
                 S((SSD_G, 1, 128), F32), S((SSD_G, 1, 256), F32)]
    extra = side.arrays if side else []
    return pl.pallas_call(
        body, grid=(nb, SSD_G), name="ssd_bwd_lat" if need_y else "ssd_bwd_ctx",
        in_specs=in_specs + [_HBM] * len(extra), out_specs=out_specs + [_HBM] * len(extra),
        out_shape=out_shape + (side.out_shapes if side else []),
        input_output_aliases={len(args) - 1: 0},
        scratch_shapes=[pltpu.VMEM((t, SSD_GW), F32), pltpu.VMEM((t, SSD_GW), F32), pltpu.VMEM((t, 128), F32),
                        pltpu.VMEM((t, 128), F32), pltpu.VMEM((t, SSD_GW), F32)] + (side.sems if side else []),
        compiler_params=_params(),
    )(*args, *extra)


def _lru_gate(u, wa, ba, wi, bi, lam):
    r = _sigmoid(_mm(u, wa) + ba)
    i = _sigmoid(_mm(u, wi) + bi)
    log_a = -LRU_C * r * _softplus(-lam)
    a = jnp.exp(log_a)
    x2 = 2.0 * log_a
    em1 = jnp.where(x2 > -0.01, x2 * (1.0 + x2 * (0.5 + x2 * (1.0 / 6.0 + x2 * (1.0 / 24.0)))), a * a - 1.0)
    return a, jnp.sqrt(-em1) * (i * u)


def _scan_pair(fwd, rev, nblk, width):
    row = lax.broadcasted_iota(jnp.int32, (8, width), 0)

    def block(a_ref, b_ref, h_ref, st, carry, reverse):
        av, bv = a_ref[pl.ds(st, 8), :], b_ref[pl.ds(st, 8), :]
        for s in (1, 2, 4):
            ok = (row < 8 - s) if reverse else (row >= s)
            sh = (8 - s) if reverse else s
            a_sh = jnp.where(ok, pltpu.roll(av, sh, 0), 1.0)
            b_sh = jnp.where(ok, pltpu.roll(bv, sh, 0), 0.0)
            bv = av * b_sh + bv
            av = av * a_sh
        h = bv + av * carry
        h_ref[pl.ds(st, 8), :] = h
        return h[0:1, :] if reverse else h[7:8, :]

    def step(i, carry):
        cf, cr = carry
        cf = block(fwd[0], fwd[1], fwd[2], pl.multiple_of(i * 8, 8), cf, False)
        cr = block(rev[0], rev[1], rev[2], pl.multiple_of((nblk - 1 - i) * 8, 8), cr, True)
        return cf, cr

    return lax.fori_loop(0, nblk, step, (fwd[3], rev[3]))


def _lru_specs(t, blk0):
    p_spec = BS((t, LRU_CB), lambda b, q: (blk0 + b, P_LRU // LRU_CB + q))
    w_spec = BS((2, 2, 128, 128), lambda b, q: (0, q, 0, 0))
    v_spec = BS((2, LRU_CB), lambda b, q: (0, q))
    c_spec = lambda r: BS((r, LRU_CB), lambda b, q: (0, q))
    s_spec = BS((None, 2, LRU_CB), lambda b, q: (b, 0, q))
    return p_spec, w_spec, v_spec, c_spec, s_spec


def _lru_fwd(proj, cw, cb, wa, ba, wi, bi, lam, h0, nb, t, period, blk0, need_y):
    nq = D // LRU_CB
    masks_of = _conv_taps(LRU_CB, period)

    def body(p_ref, cw_ref, cb_ref, wa_ref, ba_ref, wi_ref, bi_ref, lam_ref, h0_ref, *rest):
        if need_y:
            y_ref, hf_ref, hb_ref, fin_ref, sa0, sb0, sa1, sb1 = rest
        else:
            hf_ref, hb_ref, fin_ref, sa0, sb0, sa1, sb1 = rest
        u = _conv_fwd(p_ref[...], cw_ref[...], cb_ref[...], masks_of(t))
        for d, (sa, sb) in enumerate(((sa0, sb0), (sa1, sb1))):
            for j in range(2):
                sl = slice(128 * j, 128 * j + 128)
                a, bb = _lru_gate(u[:, sl], wa_ref[d, j], ba_ref[d:d + 1, sl], wi_ref[d, j], bi_ref[d:d + 1, sl],
                                  lam_ref[d:d + 1, sl])
                sa[:, sl] = a
                sb[:, sl] = bb
        lf, lb = _scan_pair((sa0, sb0, hf_ref, h0_ref[0:1, :]), (sa1, sb1, hb_ref, h0_ref[1:2, :]), t // 8, LRU_CB)
        fin_ref[0:1, :] = lf
        fin_ref[1:2, :] = lb
        if need_y:
            y_ref[...] = hf_ref[...] + hb_ref[...]

    p_spec, w_spec, v_spec, c_spec, s_spec = _lru_specs(t, blk0)
    o_spec = BS((t, LRU_CB), lambda b, q: (b, q))
    out_specs = [o_spec, o_spec, s_spec]
    out_shape = [S((nb * t, D), F32), S((nb * t, D), F32), S((nb, 2, D), F32)]
    if need_y:
        out_specs = [o_spec] + out_specs
        out_shape = [S((nb * t, D), F32)] + out_shape
    return pl.pallas_call(
        body, grid=(nb, nq), name="lru_fwd_lat" if need_y else "lru_fwd_ctx",
        in_specs=[p_spec, c_spec(4), c_spec(1), w_spec, v_spec, w_spec, v_spec, v_spec, s_spec],
        out_specs=out_specs, out_shape=out_shape,
        scratch_shapes=[pltpu.VMEM((t, LRU_CB), F32)] * 4, compiler_params=_params(),
    )(proj, cw, cb, wa, ba, wi, bi, lam, h0)


def _lru_bwd(proj, cw, cb, wa, ba, wi, bi, lam, h0, hf, hb, dy, dfin, dproj, nb, t, period, blk0, need_y, side=None):
    nq = D // LRU_CB
    rc = min(256, t)
    masks_of = _conv_taps(LRU_CB, period)

    def body(*refs):
        if side is None:
            return compute(*refs)
        own, exchange = side.split(refs, len(args), len(out_shape), 7)
        side.around(exchange, pl.program_id(0) * nq + pl.program_id(1), nb * nq, lambda: compute(*own))

    def compute(*refs):
        if need_y:
            (p_ref, cw_ref, cb_ref, wa_ref, ba_ref, wi_ref, bi_ref, lam_ref, h0_ref, hf_ref, hb_ref, dy_ref, dfin_ref, _,
             dp_ref, dh0_ref, dcw_ref, dcb_ref, dwa_ref, dwi_ref, dba_ref, dbi_ref, dlam_ref,
             su, sa0, sa1, sc0, sc1, sg0, sg1) = refs
        else:
            (p_ref, cw_ref, cb_ref, wa_ref, ba_ref, wi_ref, bi_ref, lam_ref, h0_ref, hf_ref, hb_ref, dfin_ref, _,
             dp_ref, dh0_ref, dcw_ref, dcb_ref, dwa_ref, dwi_ref, dba_ref, dbi_ref, dlam_ref,
             su, sa0, sa1, sc0, sc1, sg0, sg1) = refs
            dy_ref = None
        b, q = pl.program_id(0), pl.program_id(1)

        @pl.when(jnp.logical_and(b == 0, q == 0))
        def _():
            for r in (dcw_ref, dcb_ref, dwa_ref, dwi_ref, dba_ref, dbi_ref, dlam_ref):
                r[...] = jnp.zeros_like(r)

        masks = masks_of(t)
        u = _conv_fwd(p_ref[...], cw_ref[...], cb_ref[...], masks)
        su[...] = u
        for d, sa in enumerate((sa0, sa1)):
            for j in range(2):
                sl = slice(128 * j, 128 * j + 128)
                a, _unused = _lru_gate(u[:, sl], wa_ref[d, j], ba_ref[d:d + 1, sl], wi_ref[d, j], bi_ref[d:d + 1, sl],
                                       lam_ref[d:d + 1, sl])
                sa[:, sl] = a
        rowi = lax.broadcasted_iota(jnp.int32, (t, LRU_CB), 0)
        last, first = rowi == t - 1, rowi == 0
        sc0[...] = jnp.where(last, 0.0, pltpu.roll(sa0[...], t - 1, 0))
        sc1[...] = jnp.where(first, 0.0, pltpu.roll(sa1[...], 1, 0))
        g0 = jnp.where(last, dfin_ref[0:1, :], 0.0)
        g1 = jnp.where(first, dfin_ref[1:2, :], 0.0)
        if need_y:
            g0 = g0 + dy_ref[...]
            g1 = g1 + dy_ref[...]
        sg0[...] = g0
        sg1[...] = g1
        zero = jnp.zeros((1, LRU_CB), F32)
        _scan_pair((sc1, sg1, sg1, zero), (sc0, sg0, sg0, zero), t // 8, LRU_CB)
        dh0_ref[0:1, :] = sa0[0:1, :] * sg0[0:1, :]
        dh0_ref[1:2, :] = sa1[t - 1:t, :] * sg1[t - 1:t, :]
        sc0[...] = sg0[...] * jnp.where(first, h0_ref[0:1, :], pltpu.roll(hf_ref[...], 1, 0))
        sc1[...] = sg1[...] * jnp.where(last, h0_ref[1:2, :], pltpu.roll(hb_ref[...], t - 1, 0))

        def rows(ci, carry):
            r0 = pl.multiple_of(ci * rc, rc)
            for j in range(2):
                sl = slice(128 * j, 128 * j + 128)
                du = jnp.zeros((rc, 128), F32)
                for d, (sc, sg) in enumerate(((sc0, sg0), (sc1, sg1))):
                    _, vjp = jax.vjp(_lru_gate, su[pl.ds(r0, rc), sl], wa_ref[d, j], ba_ref[d:d + 1, sl], wi_ref[d, j],
                                     bi_ref[d:d + 1, sl], lam_ref[d:d + 1, sl])
                    du_d, dwa, dba, dwi, dbi, dlam = vjp((sc[pl.ds(r0, rc), sl], sg[pl.ds(r0, rc), sl]))
                    du = du + du_d
                    dwa_ref[d, 2 * q + j] += dwa
                    dwi_ref[d, 2 * q + j] += dwi
                    dba_ref[q, d:d + 1, sl] += dba
                    dbi_ref[q, d:d + 1, sl] += dbi
                    dlam_ref[q, d:d + 1, sl] += dlam
                sa0[pl.ds(r0, rc), sl] = du
            return carry

        lax.fori_loop(0, t // rc, rows, 0)
        draw, dcw, dcb = _conv_bwd(sa0[...], p_ref[...], cw_ref[...], masks)
        dp_ref[...] = draw.astype(BF16)
        dcw_ref[q] += dcw
        dcb_ref[q] += dcb

    p_spec, w_spec, v_spec, c_spec, s_spec = _lru_specs(t, blk0)
    o_spec = BS((t, LRU_CB), lambda b, q: (b, q))
    full = lambda shp: BS(shp, lambda b, q: (0,) * len(shp))
    in_specs = [p_spec, c_spec(4), c_spec(1), w_spec, v_spec, w_spec, v_spec, v_spec, s_spec, o_spec, o_spec]
    args = [proj, cw, cb, wa, ba, wi, bi, lam, h0, hf, hb]
    if need_y:
        in_specs.append(o_spec)
        args.append(dy)
    in_specs += [s_spec, BS(memory_space=pl.ANY)]
    args += [dfin, dproj]
    out_specs = [p_spec, s_spec, full((nq, 4, LRU_CB)), full((nq, 1, LRU_CB)), full((2, 8, 128, 128)),
                 full((2, 8, 128, 128)), full((nq, 2, LRU_CB)), full((nq, 2, LRU_CB)), full((nq, 2, LRU_CB))]
    out_shape = [S(dproj.shape, BF16), S((nb, 2, D), F32), S((nq, 4, LRU_CB), F32), S((nq, 1, LRU_CB), F32),
                 S((2, 8, 128, 128), F32), S((2, 8, 128, 128), F32), S((nq, 2, LRU_CB), F32), S((nq, 2, LRU_CB), F32),
                 S((nq, 2, LRU_CB), F32)]
    extra = side.arrays if side else []
    return pl.pallas_call(
        body, grid=(nb, nq), name="lru_bwd_lat" if need_y else "lru_bwd_ctx",
        in_specs=in_specs + [_HBM] * len(extra), out_specs=out_specs + [_HBM] * len(extra),
        out_shape=out_shape + (side.out_shapes if side else []), input_output_aliases={len(args) - 1: 0},
        scratch_shapes=[pltpu.VMEM((t, LRU_CB), F32)] * 7 + (side.sems if side else []), compiler_params=_params(),
    )(*args, *extra)


def _mix_core(y_ref, yl_ref, p_ref, nw_ref, bg_ref, wbs_ref, wbl_ref, wo_ref, nrm_s):
    for g in range(SSD_G):
        sl = slice(256 * g, 256 * g + 256)
        nrm_s[:, sl] = _grms(y_ref[:, sl], p_ref[:, sl], nw_ref[:, sl]).astype(BF16)
    br_s = jnp.dot(nrm_s[...], wbs_ref[...], preferred_element_type=F32)
    gl = (yl_ref[...] * _gelu(p_ref[:, 2048:3072])).astype(BF16)
    br_l = jnp.dot(gl, wbl_ref[...], preferred_element_type=F32)
    gs = _sigmoid(p_ref[:, 3072:4096] + bg_ref[:, 0:D])
    gr = _sigmoid(p_ref[:, 4096:5120] + bg_ref[:, D:2 * D])
    mix = (gs * br_s + gr * br_l).astype(BF16)
    xmix = jnp.dot(mix, wo_ref[...], preferred_element_type=F32)
    return br_s, gl, br_l, gs, gr, mix, xmix


def _mix_specs(rt, tiles_per_b):
    row = lambda w: BS((rt, w), lambda i: (i, 0))
    const = lambda shp: BS(shp, lambda i: (0,) * len(shp))
    gate = BS((None, None, 1, D), lambda i: (i // tiles_per_b, 2, 0, 0))
    return row, const, gate


def _mix_fwd(y, ylru, proj, x, m4, wbs, wbl, wo, nw, bg, l1g, l1b, rt, tiles_per_b):
    n = x.shape[0]

    def body(y_ref, yl_ref, p_ref, x_ref, g1_ref, wbs_ref, wbl_ref, wo_ref, nw_ref, bg_ref, lg_ref, lb_ref,
             x1_ref, nrm_ref, gl_ref, mix_ref, brs_ref, brl_ref, xm_ref):
        br_s, gl, br_l, _, _, mix, xmix = _mix_core(y_ref, yl_ref, p_ref, nw_ref, bg_ref, wbs_ref, wbl_ref, wo_ref, nrm_ref)
        gl_ref[...] = gl
        mix_ref[...] = mix
        brs_ref[...] = br_s
        brl_ref[...] = br_l
        xm_ref[...] = xmix
        x1_ref[...] = _resln(x_ref[...], xmix, g1_ref[...], lg_ref[...], lb_ref[...])

    row, const, gate = _mix_specs(rt, tiles_per_b)
    return pl.pallas_call(
        body, grid=(n // rt,), name="mix_fwd",
        in_specs=[row(SSD_INNER), row(D), BS((rt, 5120), lambda i: (i, 1)), row(D), gate,
                  const((SSD_INNER, D)), const((D, D)), const((D, D)), const((1, SSD_INNER)), const((1, 2 * D)),
                  const((1, D)), const((1, D))],
        out_specs=[row(D), row(SSD_INNER), row(D), row(D), row(D), row(D), row(D)],
        out_shape=[S((n, D), F32), S((n, SSD_INNER), BF16), S((n, D), BF16), S((n, D), BF16), S((n, D), F32),
                   S((n, D), F32), S((n, D), F32)],
        compiler_params=_params(),
    )(y, ylru, proj, x, m4, wbs, wbl, wo, nw, bg, l1g, l1b)


def _mix_bwd(y, ylru, proj, x, m4, wbs, wbl, wo, nw, bg, l1g, l1b, brs, brl, xmix, dx1, dproj, rt, tiles_per_b):
    n = x.shape[0]

    def body(y_ref, yl_ref, p_ref, x_ref, g1_ref, wbs_ref, wbl_ref, wo_ref, nw_ref, bg_ref, lg_ref, lb_ref,
             brs_ref, brl_ref, xm_ref, dx1_ref, _,
             dp_ref, dy_ref, dyl_ref, dxr_ref, dbrs_ref, dbrl_ref, dxm_ref,
             dg1_ref, dnw_ref, dbg_ref, dlg_ref, dlb_ref):
        i = pl.program_id(0)

        @pl.when(i == 0)
        def _():
            for r in (dnw_ref, dbg_ref, dlg_ref, dlb_ref):
                r[...] = jnp.zeros_like(r)

        @pl.when(i % tiles_per_b == 0)
        def _():
            dg1_ref[...] = jnp.zeros_like(dg1_ref)

        br_s, br_l = brs_ref[...], brl_ref[...]
        gs = _sigmoid(p_ref[:, 3072:4096] + bg_ref[:, 0:D])
        gr = _sigmoid(p_ref[:, 4096:5120] + bg_ref[:, D:2 * D])
        _, vjp = jax.vjp(_resln, x_ref[...], xm_ref[...], g1_ref[...], lg_ref[...], lb_ref[...])
        dxr, dxmix, dg1, dlg, dlb = vjp(dx1_ref[...])
        dxr_ref[...] = dxr
        dg1_ref[...] += dg1
        dlg_ref[...] += dlg
        dlb_ref[...] += dlb
        dxmb = dxmix.astype(BF16)
        dxm_ref[...] = dxmb
        dmix = lax.dot_general(dxmb, wo_ref[...], (((1,), (1,)), ((), ())), preferred_element_type=F32)
        dbrs = (dmix * gs).astype(BF16)
        dbrl = (dmix * gr).astype(BF16)
        dbrs_ref[...] = dbrs
        dbrl_ref[...] = dbrl
        dmg_s = dmix * br_s * gs * (1.0 - gs)
        dmg_r = dmix * br_l * gr * (1.0 - gr)
        dp_ref[:, 3072:4096] = dmg_s.astype(BF16)
        dp_ref[:, 4096:5120] = dmg_r.astype(BF16)
        dbg_ref[:, 0:D] += _colsum(dmg_s)
        dbg_ref[:, D:2 * D] += _colsum(dmg_r)
        dnrm = lax.dot_general(dbrs, wbs_ref[...], (((1,), (1,)), ((), ())), preferred_element_type=F32)
        for g in range(SSD_G):
            sl = slice(256 * g, 256 * g + 256)
            _, vjp = jax.vjp(_grms, y_ref[:, sl], p_ref[:, sl], nw_ref[:, sl])
            dyg, dzg, dnwg = vjp(dnrm[:, sl])
            dy_ref[:, sl] = dyg
            dp_ref[:, sl] = dzg.astype(BF16)
            dnw_ref[:, sl] += dnwg
        dgl = lax.dot_general(dbrl, wbl_ref[...], (((1,), (1,)), ((), ())), preferred_element_type=F32)
        _, vjp = jax.vjp(lambda a, c: a * _gelu(c), yl_ref[...], p_ref[:, 2048:3072])
        dyl, dlgate = vjp(dgl)
        dyl_ref[...] = dyl
        dp_ref[:, 2048:3072] = dlgate.astype(BF16)

    row, const, gate = _mix_specs(rt, tiles_per_b)
    pblk = BS((rt, 5120), lambda i: (i, 1))
    nb = n // (rt * tiles_per_b)
    out_specs = [pblk, row(SSD_INNER), row(D), row(D), row(D), row(D), row(D),
                 BS((None, 1, D), lambda i: (i // tiles_per_b, 0, 0)), const((1, SSD_INNER)), const((1, 2 * D)),
                 const((1, D)), const((1, D))]
    out_shape = [S(dproj.shape, BF16), S((n, SSD_INNER), F32), S((n, D), F32), S((n, D), F32),
                 S((n, D), BF16), S((n, D), BF16), S((n, D), BF16),
                 S((nb, 1, D), F32), S((1, SSD_INNER), F32), S((1, 2 * D), F32), S((1, D), F32), S((1, D), F32)]
    return pl.pallas_call(
        body, grid=(n // rt,), name="mix_bwd",
        in_specs=[row(SSD_INNER), row(D), pblk, row(D), gate,
                  const((SSD_INNER, D)), const((D, D)), const((D, D)), const((1, SSD_INNER)), const((1, 2 * D)),
                  const((1, D)), const((1, D)), row(D), row(D), row(D), row(D), BS(memory_space=pl.ANY)],
        out_specs=out_specs, out_shape=out_shape, input_output_aliases={16: 0},
        compiler_params=_params(),
    )(y, ylru, proj, x, m4, wbs, wbl, wo, nw, bg, l1g, l1b, brs, brl, xmix, dx1, dproj)


def _mlp_step(x1, tgt, m4, w1, b1, w2, b2, l2g, l2b, rt, tiles_per_b):
    n = x1.shape[0]

    def body(x_ref, t_ref, sh_ref, sc_ref, gt_ref, w1_hbm, b1_ref, w2_hbm, b2_ref, lg_ref, lb_ref,
             loss_ref, dx_ref, h2_ref, da1_ref, r2_ref, dmlp_ref, dm_ref, db1_ref, db2_ref, dlg_ref, dlb_ref,
             w1_vm, w2_vm, sem):
        i = pl.program_id(0)

        @pl.when(i == 0)
        def _():
            c1 = pltpu.make_async_copy(w1_hbm, w1_vm, sem.at[0])
            c2 = pltpu.make_async_copy(w2_hbm, w2_vm, sem.at[1])
            c1.start()
            c2.start()
            for r in (loss_ref, db1_ref, db2_ref, dlg_ref, dlb_ref):
                r[...] = jnp.zeros_like(r)
            c1.wait()
            c2.wait()

        @pl.when(i % tiles_per_b == 0)
        def _():
            dm_ref[...] = jnp.zeros_like(dm_ref)

        x1v = x_ref[...]
        h2, vjp_h = jax.vjp(_modln, x1v, sh_ref[...], sc_ref[...])
        h2b = h2.astype(BF16)
        h2_ref[...] = h2b
        r = jnp.maximum(jnp.dot(h2b, w1_vm[...], preferred_element_type=F32) + b1_ref[...], 0.0)
        r2b = (r * r).astype(BF16)
        r2_ref[...] = r2b
        mlp = jnp.dot(r2b, w2_vm[...], preferred_element_type=F32) + b2_ref[...]
        x2, vjp_r = jax.vjp(_resln, x1v, mlp, gt_ref[...], lg_ref[...], lb_ref[...])
        diff = x2 - t_ref[...]
        loss_ref[...] += (0.5 / D) * jnp.sum(diff * diff)
        dxa, dmlp, dgt, dlg, dlb = vjp_r(diff * (1.0 / D))
        dlg_ref[...] += dlg
        dlb_ref[...] += dlb
        dm_ref[2:3, :] += dgt
        db2_ref[...] += _colsum(dmlp)
        dmlpb = dmlp.astype(BF16)
        dmlp_ref[...] = dmlpb
        da1 = lax.dot_general(dmlpb, w2_vm[...], (((1,), (1,)), ((), ())), preferred_element_type=F32) * (2.0 * r)
        db1_ref[...] += _colsum(da1)
        da1b = da1.astype(BF16)
        da1_ref[...] = da1b
        dh2 = lax.dot_general(da1b, w1_vm[...], (((1,), (1,)), ((), ())), preferred_element_type=F32)
        dxb, dsh, dsc = vjp_h(dh2)
        dx_ref[...] = dxa + dxb
        dm_ref[0:1, :] += dsh
        dm_ref[1:2, :] += dsc

    row = lambda w: BS((rt, w), lambda i: (i, 0))
    const = lambda shp: BS(shp, lambda i: (0,) * len(shp))
    mod = lambda k: BS((None, None, 1, D), lambda i: (i // tiles_per_b, k, 0, 0))
    nb = n // (rt * tiles_per_b)
    anyspec = BS(memory_space=pl.ANY)
    return pl.pallas_call(
        body, grid=(n // rt,), name="mlp_step",
        in_specs=[row(D), row(D), mod(3), mod(4), mod(5), anyspec, const((1, MLP_H)), anyspec, const((1, D)),
                  const((1, D)), const((1, D))],
        out_specs=[const((8, 128)), row(D), row(D), row(MLP_H), row(MLP_H), row(D),
                   BS((None, 3, D), lambda i: (i // tiles_per_b, 0, 0)), const((1, MLP_H)), const((1, D)),
                   const((1, D)), const((1, D))],
        out_shape=[S((8, 128), F32), S((n, D), F32), S((n, D), BF16), S((n, MLP_H), BF16), S((n, MLP_H), BF16),
                   S((n, D), BF16), S((nb, 3, D), F32), S((1, MLP_H), F32), S((1, D), F32), S((1, D), F32),
                   S((1, D), F32)],
        scratch_shapes=[pltpu.VMEM((D, MLP_H), BF16), pltpu.VMEM((MLP_H, D), BF16), pltpu.SemaphoreType.DMA((2,))],
        compiler_params=_params(),
    )(x1, tgt, m4, m4, m4, w1, b1, w2, b2, l2g, l2b)


def _pack_win(w):
    parts = []
    for g in range(SSD_G):
        parts += [w[:, 256 * g:256 * g + 256], w[:, 2048 + 128 * g:2176 + 128 * g], w[:, 4160 + 128 * g:4288 + 128 * g]]
    parts += [w[:, 3136:4160], w[:, 5184:7232], w[:, 7232:8256], w[:, 8256:10304], w[:, 3072:3136],
              jnp.zeros((w.shape[0], P_W - P_DT - 64), w.dtype)]
    return jnp.concatenate(parts, axis=1)


def _unpack_win(p):
    xs = [p[:, 512 * g:512 * g + 256] for g in range(SSD_G)]
    bs = [p[:, 512 * g + 256:512 * g + 384] for g in range(SSD_G)]
    cs = [p[:, 512 * g + 384:512 * g + 512] for g in range(SSD_G)]
    return jnp.concatenate(xs + bs + [p[:, P_DT:P_DT + 64], p[:, P_LRU:P_Z]] + cs + [p[:, P_Z:P_DT]], axis=1)


def _pack_conv(w):
    return jnp.stack([jnp.concatenate([w[:, 256 * g:256 * g + 256], w[:, 2048 + 128 * g:2176 + 128 * g],
                                       w[:, 3072 + 128 * g:3200 + 128 * g]], axis=1) for g in range(SSD_G)])


def _unpack_conv(p):
    r = p.shape[1]
    x = jnp.transpose(p[:, :, 0:256], (1, 0, 2)).reshape(r, 2048)
    b = jnp.transpose(p[:, :, 256:384], (1, 0, 2)).reshape(r, 1024)
    c = jnp.transpose(p[:, :, 384:512], (1, 0, 2)).reshape(r, 1024)
    return jnp.concatenate([x, b, c], axis=1)


def _pack_heads(v):
    p = jnp.transpose(v.reshape(2, SSD_G, 4), (1, 0, 2)).reshape(SSD_G, 1, 8)
    return jnp.pad(p, ((0, 0), (0, 0), (0, 120)))


def _unpack_heads(p):
    return jnp.transpose(p[:, 0, 0:8].reshape(SSD_G, 2, 4), (1, 0, 2)).reshape(2, 32)


def _pack_dt(dt):
    n = dt.shape[0]
    p = jnp.transpose(dt.reshape(n, 2, SSD_G, 4), (2, 0, 1, 3)).reshape(SSD_G, n, 8)
    return jnp.pad(p, ((0, 0), (0, 0), (0, 120)))


def _unpack_dt(p):
    n = p.shape[1]
    return jnp.transpose(p[:, :, 0:8].reshape(SSD_G, n, 2, 4), (1, 2, 0, 3)).reshape(n, 64)


def _tk(rows):
    return next(tk for tk in (1024, 512, 256, 128) if rows % tk == 0)


LATE = ["w_br_ssd", "w_br_lru", "w_out", "w_mlp1", "w_mlp2"]


def _local_step(x, c, ctx, tgt, sm, wmod, win, late, late_are_shards=False, reducer=None):
    nb, t, _ = x.shape
    tc = ctx.shape[1]
    nl, ncx = nb * t, nb * tc
    rt = 256 if tc % 256 == 0 else 128
    rtm = 128
    xl, xc = x.reshape(nl, D), ctx.reshape(ncx, D)
    tgt2 = tgt.reshape(nl, D)
    cc = jnp.zeros((8, D), F32).at[0:nb].set(c).at[nb].set(sm["c_ctx"])
    m = _mod_fwd(cc, wmod, sm["b_mod"])
    m4 = m.reshape(8, N_MOD, 1, D)
    proj, h1 = _inproj_fwd(xl, m4, win, rtm, nl // rtm, t // rtm, nb, "inproj_fwd_lat")
    proj_c, h1_c = _inproj_fwd(xc, m4, win, rtm, 0, 1, nb, "inproj_fwd_ctx")

    cw_s, cb_s = _pack_conv(sm["ssd_conv_w"]), _pack_conv(sm["ssd_conv_b"])
    dtb, alog = _pack_heads(sm["ssd_dt_bias"]), _pack_heads(sm["ssd_a_log"])
    drow = jnp.repeat(sm["ssd_d"].reshape(32), 64).reshape(SSD_G, 1, 256)
    dtg, dtg_c = _pack_dt(proj[:, P_DT:P_DT + 64]), _pack_dt(proj_c[:, P_DT:P_DT + 64])
    zst = jnp.zeros((nb, SSD_G, SSD_N, 256), F32)
    zl = jnp.zeros((nb, 2, D), F32)
    ssd_p = (cw_s, cb_s, dtb, alog, drow)
    lru_p = (sm["lru_conv_w"], sm["lru_conv_b"], sm["lru_wa"], sm["lru_ba"], sm["lru_wi"], sm["lru_bi"], sm["lru_lambda"])

    chsf, chsb, csf, csb = _ssd_fwd(proj_c, dtg_c, *ssd_p, zst, zst, nb, tc, tc, 0, False)
    y, lhsf, lhsb, _, _, *got = _ssd_fwd(proj, dtg, *ssd_p, csf, csb, nb, t, GRID_W, 0, True,
                                         tuple(late) if late_are_shards else ())
    wbs, wbl, wo, w1, w2 = [_full_from_chips(g, n) for g, n in zip(got, LATE)] if late_are_shards else late
    chf, chb, cfin = _lru_fwd(proj_c, *lru_p, zl, nb, tc, tc, 0, False)
    ylru, lhf, lhb, _ = _lru_fwd(proj, *lru_p, cfin, nb, t, GRID_W, 0, True)
    mix_w = (wbs, wbl, wo, sm["ssd_norm_w"], sm["b_gate"], sm["ln1_g"], sm["ln1_b"])
    x1, nrm, gl, mixb, brs, brl, xmix = _mix_fwd(y, ylru, proj, xl, m4, *mix_w, rtm, t // rtm)
    (loss, dx1, h2, da1, r2, dmlp, dm2, db1, db2, dl2g, dl2b) = _mlp_step(
        x1, tgt2, m4, w1, sm["b_mlp1"], w2, sm["b_mlp2"], sm["ln2_g"], sm["ln2_b"], rt, t // rt)

    dproj = lax.empty((nl, P_W), BF16)
    dproj_c = jnp.zeros((ncx, P_W), BF16)
    (dproj, dy, dylru, dxres, dbrs, dbrl, dxm, dg1, dnw, dbg, dl1g, dl1b) = _mix_bwd(
        y, ylru, proj, xl, m4, *mix_w, brs, brl, xmix, dx1, dproj, rtm, t // rtm)
    big = {
        "w_br_ssd": _matmul_tn(nrm, dbrs, D, D, _tk(nl), "dw_br_ssd"),
        "w_br_lru": _matmul_tn(gl, dbrl, D, D, _tk(nl), "dw_br_lru"),
        "w_out": _matmul_tn(mixb, dxm, D, D, _tk(nl), "dw_out"),
        "w_mlp1": _matmul_tn(h2, da1, D, D, _tk(nl), "dw_mlp1"),
        "w_mlp2": _matmul_tn(r2, dmlp, D, D, _tk(nl), "dw_mlp2"),
    }
    side = reducer.begin_swap(big, list(big)) if reducer else None
    (dproj, ddt_l, dh0f, dh0b, dcw_l, dcb_l, ddtb_l, dal_l, dd, *got) = _ssd_bwd(
        proj, dtg, *ssd_p, lhsf, lhsb, dy, zst, zst, dproj, nb, t, GRID_W, 0, True, side)
    (dproj_c, ddt_c, _, _, dcw_c, dcb_c, ddtb_c, dal_c, _) = _ssd_bwd(
        proj_c, dtg_c, *ssd_p, chsf, chsb, None, dh0f, dh0b, dproj_c, nb, tc, tc, 0, False)
    side = reducer.begin_scatter(got) if reducer else None
    (dproj, dlh0, gcw_l, gcb_l, gwa_l, gwi_l, gba_l, gbi_l, glam_l, *got) = _lru_bwd(
        proj, *lru_p, cfin, lhf, lhb, dylru, zl, dproj, nb, t, GRID_W, 0, True, side)
    if reducer:
        reducer.end(got)
    (dproj_c, _, gcw_c, gcb_c, gwa_c, gwi_c, gba_c, gbi_c, glam_c) = _lru_bwd(
        proj_c, *lru_p, zl, chf, chb, None, dlh0, dproj_c, nb, tc, tc, 0, False)
    pad_dt = lambda d: jnp.pad(_unpack_dt(d).astype(BF16), ((0, 0), (0, P_W - P_DT - 64)))
    dproj = lax.dynamic_update_slice(dproj, pad_dt(ddt_l), (0, P_DT))
    dproj_c = lax.dynamic_update_slice(dproj_c, pad_dt(ddt_c), (0, P_DT))

    big["w_in"] = _matmul_tn2(h1, dproj, h1_c, dproj_c, D, 1152, min(_tk(nl), _tk(ncx)), "dw_in")
    side = reducer.begin_swap(big, ["w_in"]) if reducer else None
    dmc, *got = _inproj_bwd(xc, m4, win, dproj_c, rt, 0, ncx // rt, ncx // rt, nb, False, None, side)
    side = reducer.begin_scatter(got) if reducer else None
    gx, dm1, *got = _inproj_bwd(xl, m4, win, dproj, rt, 0, nl // rt, t // rt, nb, True, dxres, side)
    if reducer:
        reducer.end(got)
    dm = jnp.zeros((8, N_MOD, D), F32)
    dm = dm.at[0:nb].set(jnp.concatenate([dm1, dg1, dm2], axis=1)).at[nb, 0:2].set(dmc[0])
    dwmod, dbmod, dcc = _mod_bwd(cc, wmod, dm.reshape(8, N_MOD * D))
    big["w_mod"] = dwmod
    nq = D // LRU_CB
    small = {
        "c_ctx": dcc[nb],
        "b_mod": dbmod,
        "b_gate": dbg,
        "ssd_conv_w": _unpack_conv(dcw_l + dcw_c),
        "ssd_conv_b": _unpack_conv(dcb_l + dcb_c),
        "ssd_dt_bias": _unpack_heads(ddtb_l + ddtb_c),
        "ssd_a_log": _unpack_heads(dal_l + dal_c),
        "ssd_d": jnp.sum(dd.reshape(32, 64), axis=1),
        "ssd_norm_w": dnw,
        "lru_conv_w": jnp.transpose(gcw_l + gcw_c, (1, 0, 2)).reshape(4, D),
        "lru_conv_b": (gcb_l + gcb_c).reshape(1, D),
        "lru_wa": gwa_l + gwa_c,
        "lru_ba": jnp.transpose(gba_l + gba_c, (1, 0, 2)).reshape(2, D),
        "lru_wi": gwi_l + gwi_c,
        "lru_bi": jnp.transpose(gbi_l + gbi_c, (1, 0, 2)).reshape(2, D),
        "lru_lambda": jnp.transpose(glam_l + glam_c, (1, 0, 2)).reshape(2, D),
        "ln1_g": dl1g, "ln1_b": dl1b, "b_mlp1": db1, "b_mlp2": db2, "ln2_g": dl2g, "ln2_b": dl2b,
    }
    return loss[0, 0], gx.reshape(nb, t, D), big, small


_HBM = BS(memory_space=pl.ANY)


def _place():
    return lax.axis_index("x"), lax.axis_index("y"), lax.axis_index("c")


def _other_chips(x, y):
    return [(1 - x, y), (x, 1 - y), (1 - x, 1 - y)]


def _gather_chips(arrs):
    n = len(arrs)

    def body(*refs):
        ex = _GatherExchange(refs[:n], refs[n:2 * n], refs[2 * n:])
        ex.begin()
        ex.finish()

    return pl.pallas_call(
        body, name="gather_weights", in_specs=[_HBM] * n, out_specs=[_HBM] * n,
        out_shape=_gather_out_shapes(arrs), scratch_shapes=_gather_sems(n),
    )(*arrs)


def _gather_out_shapes(arrs):
    return [S((4,) + a.shape, a.dtype) for a in arrs]


def _gather_sems(n):
    return [pltpu.SemaphoreType.DMA((3 * n,))] * 4 + [pltpu.SemaphoreType.DMA((n,))]


class _GatherExchange:
    def __init__(self, ins, outs, sems):
        self.ins, self.outs = ins, outs
        self.ici_send, self.ici_recv, self.d2d_send, self.d2d_recv, self.loc_sems = sems
        self.x, self.y, self.c = _place()
        self.me = 2 * self.x + self.y
        self.chips = _other_chips(self.x, self.y)

    def _half(self, a, which):
        hr = self.ins[a].shape[0] // 2
        return pl.ds(pl.multiple_of((self.c if which == 0 else 1 - self.c) * hr, 8), hr)

    def _local(self, a):
        return pltpu.make_async_copy(self.ins[a], self.outs[a].at[self.me], self.loc_sems.at[a])

    def _ici(self, a, k, slot):
        px, py = self.chips[k]
        mine = self._half(a, 0)
        return pltpu.make_async_remote_copy(src_ref=self.ins[a].at[mine], dst_ref=self.outs[a].at[slot, mine],
                                            send_sem=self.ici_send.at[3 * a + k], recv_sem=self.ici_recv.at[3 * a + k],
                                            device_id=(px, py, self.c), device_id_type=MESH)

    def _d2d(self, a, k, which):
        px, py = self.chips[k]
        rows = self.outs[a].at[2 * px + py, self._half(a, which)]
        return pltpu.make_async_remote_copy(src_ref=rows, dst_ref=rows, send_sem=self.d2d_send.at[3 * a + k],
                                            recv_sem=self.d2d_recv.at[3 * a + k],
                                            device_id=(self.x, self.y, 1 - self.c), device_id_type=MESH)

    def begin(self):
        for a in range(len(self.ins)):
            self._local(a).start()
            for k in range(3):
                self._ici(a, k, self.me).start()

    def finish(self):
        n = len(self.ins)
        for a in range(n):
            for k, (px, py) in enumerate(self.chips):
                self._ici(a, k, 2 * px + py).wait_recv()
                self._d2d(a, k, 0).start()
        for a in range(n):
            for k in range(3):
                self._d2d(a, k, 1).wait_recv()
        for a in range(n):
            self._local(a).wait()
            for k in range(3):
                self._ici(a, k, self.me).wait_send()
                self._d2d(a, k, 0).wait_send()


def _scatter_chips(arrs):
    side = _Side("scatter", arrs)
    n = len(arrs)

    def body(*refs):
        ex = side.make(refs[:n], refs[n:2 * n], refs[2 * n:])
        ex.begin()
        ex.finish()

    return pl.pallas_call(
        body, name="scatter_grads", in_specs=[_HBM] * n, out_specs=[_HBM] * n,
        out_shape=side.out_shapes, scratch_shapes=side.sems,
    )(*arrs)


class _ScatterExchange:
    def __init__(self, ins, outs, sems):
        self.ins, self.outs = ins, outs
        self.send_sems, self.recv_sems = sems
        x, y, self.c = _place()
        self.chips = _other_chips(x, y)

    def _copy(self, a, k):
        px, py = self.chips[k]
        return pltpu.make_async_remote_copy(src_ref=self.ins[a].at[2 * px + py], dst_ref=self.outs[a].at[k],
                                            send_sem=self.send_sems.at[3 * a + k], recv_sem=self.recv_sems.at[3 * a + k],
                                            device_id=(px, py, self.c), device_id_type=MESH)

    def begin(self):
        for a in range(len(self.ins)):
            for k in range(3):
                self._copy(a, k).start()

    def finish(self):
        for a in range(len(self.ins)):
            for k in range(3):
                self._copy(a, k).wait_recv()
        for a in range(len(self.ins)):
            for k in range(3):
                self._copy(a, k).wait_send()


class _Side:
    def __init__(self, kind, arrays):
        self.arrays = list(arrays)
        n = len(self.arrays)
        if kind == "gather":
            self.out_shapes, self.sems, self.make = _gather_out_shapes(self.arrays), _gather_sems(n), _GatherExchange
        elif kind == "swap":
            self.out_shapes = [S((4, a.shape[1] // 2, a.shape[2]), a.dtype) for a in self.arrays]
            self.sems = [pltpu.SemaphoreType.DMA((4 * n,))] * 2
            self.make = _SwapExchange
        else:
            self.out_shapes = [S((3,) + a.shape[1:], a.dtype) for a in self.arrays]
            self.sems = [pltpu.SemaphoreType.DMA((3 * n,))] * 2
            self.make = _ScatterExchange

    def split(self, refs, n_in, n_out, n_scr):
        a, b = len(self.arrays), len(self.out_shapes)
        i1 = n_in + a
        o1 = i1 + n_out
        o2 = o1 + b
        s1 = o2 + n_scr
        own = tuple(refs[:n_in]) + tuple(refs[i1:o1]) + tuple(refs[o2:s1])
        return own, self.make(refs[n_in:i1], refs[o1:o2], refs[s1:])

    def around(self, exchange, step, n_steps, compute):
        pl.when(step == 0)(exchange.begin)
        compute()
        pl.when(step == n_steps - 1)(exchange.finish)


def _swap_halves(arrs, name):
    side = _Side("swap", arrs)
    n = len(arrs)

    def body(*refs):
        ex = side.make(refs[:n], refs[n:2 * n], refs[2 * n:])
        ex.begin()
        ex.finish()

    return pl.pallas_call(
        body, name=name, in_specs=[_HBM] * n, out_specs=[_HBM] * n, out_shape=side.out_shapes, scratch_shapes=side.sems,
    )(*arrs)


class _SwapExchange:
    def __init__(self, ins, outs, sems):
        self.ins, self.outs = ins, outs
        self.send_sems, self.recv_sems = sems
        self.x, self.y, self.c = _place()

    def _copy(self, a, q):
        hr = self.ins[a].shape[1] // 2
        theirs = pl.ds(pl.multiple_of((1 - self.c) * hr, 8), hr)
        return pltpu.make_async_remote_copy(src_ref=self.ins[a].at[q, theirs], dst_ref=self.outs[a].at[q],
                                            send_sem=self.send_sems.at[4 * a + q], recv_sem=self.recv_sems.at[4 * a + q],
                                            device_id=(self.x, self.y, 1 - self.c), device_id_type=MESH)

    def begin(self):
        for a in range(len(self.ins)):
            for q in range(4):
                self._copy(a, q).start()

    def finish(self):
        for wait in ("wait_recv", "wait_send"):
            for a in range(len(self.ins)):
                for q in range(4):
                    getattr(self._copy(a, q), wait)()


def _allreduce_small(vs, bf16_over_ici):
    n = len(vs)

    def body(*refs):
        ins, outs, bufs, sendb = refs[:n], refs[n:2 * n], refs[2 * n:5 * n], refs[5 * n:6 * n]
        send_sems, recv_sems = refs[6 * n:]
        x, y, c = _place()
        srcs = list(ins)
        for s, peer in enumerate(((x, y, 1 - c), (x, 1 - y, c), (1 - x, y, c))):
            copies = []
            for a in range(n):
                src = srcs[a]
                if s > 0 and bf16_over_ici[a]:
                    sendb[a][...] = src[...].astype(BF16)
                    src = sendb[a]
                copies.append(pltpu.make_async_remote_copy(
                    src_ref=src, dst_ref=bufs[3 * a + s], send_sem=send_sems.at[3 * a + s],
                    recv_sem=recv_sems.at[3 * a + s], device_id=peer, device_id_type=MESH))
                copies[-1].start()
            for a, cp in enumerate(copies):
                cp.wait()
                mine = sendb[a] if s > 0 and bf16_over_ici[a] else srcs[a]
                outs[a][...] = mine[...].astype(F32) + bufs[3 * a + s][...].astype(F32)
            srcs = list(outs)

    vm = BS(memory_space=pltpu.VMEM)
    wire = lambda a, s: BF16 if s > 0 and bf16_over_ici[a] else F32
    return pl.pallas_call(
        body, name="allreduce_small", in_specs=[vm] * n, out_specs=[vm] * n, out_shape=[S(v.shape, F32) for v in vs],
        scratch_shapes=[pltpu.VMEM(v.shape, wire(a, s)) for a, v in enumerate(vs) for s in range(3)]
        + [pltpu.VMEM(v.shape if bf16_over_ici[a] else (16, 128), BF16) for a, v in enumerate(vs)]
        + [pltpu.SemaphoreType.DMA((3 * n,)), pltpu.SemaphoreType.DMA((3 * n,))],
        compiler_params=_params(),
    )(*vs)


def _swap_cores(arrs):
    n = len(arrs)

    def body(*refs):
        ins, outs = refs[:n], refs[n:2 * n]
        send_sems, recv_sems = refs[2 * n:]
        x, y, c = _place()
        sends = []
        for a in range(n):
            cp = pltpu.make_async_remote_copy(src_ref=ins[a], dst_ref=outs[a], send_sem=send_sems.at[a],
                                              recv_sem=recv_sems.at[a], device_id=(x, y, 1 - c), device_id_type=MESH)
            cp.start()
            sends.append(cp)
        for cp in sends:
            cp.wait_recv()
        for cp in sends:
            cp.wait_send()

    return pl.pallas_call(
        body, name="swap_cores", in_specs=[_HBM] * n, out_specs=[_HBM] * n,
        out_shape=[S(a.shape, a.dtype) for a in arrs],
        scratch_shapes=[pltpu.SemaphoreType.DMA((n,)), pltpu.SemaphoreType.DMA((n,))],
    )(*arrs)


def _row_tile(r, c=128):
    tr = 256 if c <= 1024 else (128 if c <= 2048 else 64)
    return tr if r % tr == 0 else r


def _sum_half(own, sib, core, name):
    _, r, c = own.shape
    hr = r // 2
    tr = _row_tile(hr, c)
    nbk = hr // tr

    def body(core_ref, o_ref, s_ref, p_ref, pb_ref):
        p = o_ref[...] + s_ref[...]
        p_ref[...] = p
        pb_ref[...] = p.astype(BF16)

    blk = BS((None, tr, c), lambda q, i, cr: (q, i, 0))
    return pl.pallas_call(
        body, name=name, out_shape=[S((4, hr, c), F32), S((4, hr, c), BF16)],
        grid_spec=pltpu.PrefetchScalarGridSpec(
            num_scalar_prefetch=1, grid=(4, nbk),
            in_specs=[BS((None, tr, c), lambda q, i, cr: (q, cr[0] * nbk + i, 0)), blk], out_specs=[blk, blk]),
        compiler_params=_params(),
    )(core, own, sib)


def _sum4(part, recv, chip, name):
    _, r, c = part.shape
    tr = _row_tile(r, c)

    def body(chip_ref, o_ref, r_ref, out_ref):
        acc = o_ref[...]
        for k in range(3):
            acc = acc + r_ref[k].astype(F32)
        out_ref[...] = acc

    return pl.pallas_call(
        body, name=name, out_shape=S((r, c), F32),
        grid_spec=pltpu.PrefetchScalarGridSpec(
            num_scalar_prefetch=1, grid=(r // tr,),
            in_specs=[BS((None, tr, c), lambda i, ch: (ch[0], i, 0)), BS((3, tr, c), lambda i, ch: (0, i, 0))],
            out_specs=BS((tr, c), lambda i, ch: (i, 0))),
        compiler_params=_params(),
    )(chip, part, recv)


def _adam_math(w, g, m, v):
    m = ADAM_B1 * m + (1.0 - ADAM_B1) * g
    v = ADAM_B2 * v + (1.0 - ADAM_B2) * (g * g)
    m_hat = m / (1.0 - ADAM_B1 ** ADAM_STEP)
    v_hat = v / (1.0 - ADAM_B2 ** ADAM_STEP)
    return -ADAM_LR * (m_hat / (jnp.sqrt(v_hat) + ADAM_EPS) + ADAM_WD * w), m, v


def _adam_halves(mine, other, w, m, v, core, name):
    r, c = w.shape
    tr = _row_tile(r // 2, c)
    nbk = (r // 2) // tr

    def body(core_ref, a_ref, b_ref, w_ref, m_ref, v_ref, g_ref, d_ref, nm_ref, nv_ref):
        g = jnp.where(pl.program_id(0) // nbk == core_ref[0], a_ref[...], b_ref[...])
        g_ref[...] = g
        d_ref[...], nm_ref[...], nv_ref[...] = _adam_math(w_ref[...], g, m_ref[...], v_ref[...])

    spec = BS((tr, c), lambda i, cr: (i, 0))
    half = BS((tr, c), lambda i, cr: (i % nbk, 0))
    return pl.pallas_call(
        body, name=name, out_shape=[S((r, c), F32)] * 4,
        grid_spec=pltpu.PrefetchScalarGridSpec(num_scalar_prefetch=1, grid=(r // tr,), in_specs=[half, half] + [spec] * 3,
                                               out_specs=[spec] * 4),
        compiler_params=_params(),
    )(core, mine, other, w, m, v)


def _adam_flat(g, w, m, v, name):
    r = w.shape[0]
    tr = _row_tile(r)

    def body(g_ref, w_ref, m_ref, v_ref, d_ref, nm_ref, nv_ref):
        d_ref[...], nm_ref[...], nv_ref[...] = _adam_math(w_ref[...], g_ref[...], m_ref[...], v_ref[...])

    spec = BS((tr, 128), lambda i: (i, 0))
    return pl.pallas_call(
        body, grid=(r // tr,), name=name, in_specs=[spec] * 4, out_specs=[spec] * 3,
        out_shape=[S((r, 128), F32)] * 3, compiler_params=_params(),
    )(g, w, m, v)


def _flatten(arrs, rows_mult=256):
    flat = jnp.concatenate([a.reshape(-1) for a in arrs])
    n = flat.shape[0]
    rows = -(-n // 128)
    rows = -(-rows // rows_mult) * rows_mult
    return jnp.pad(flat, (0, rows * 128 - n)).reshape(rows, 128)


def _unflatten(flat, shapes):
    flat = flat.reshape(-1)
    out, o = [], 0
    for shp in shapes:
        n = int(np.prod(shp))
        out.append(flat[o:o + n].reshape(shp))
        o += n
    return out


BIG = ["w_mod", "w_in", "w_br_ssd", "w_br_lru", "w_out", "w_mlp1", "w_mlp2"]
COL_SHARDED = {"w_mod": N_MOD * D, "w_in": IN_COLS, "w_mlp1": MLP_H}
SMALL_SHARDED = ["ssd_conv_w", "lru_conv_w", "lru_ba", "lru_bi", "lru_lambda"]
WEIGHTS = ['c_ctx', 'w_mod', 'b_mod', 'w_in', 'b_gate', 'ssd_conv_w', 'ssd_conv_b', 'ssd_dt_bias', 'ssd_a_log', 'ssd_d',
           'ssd_norm_w', 'lru_conv_w', 'lru_conv_b', 'lru_wa', 'lru_ba', 'lru_wi', 'lru_bi', 'lru_lambda', 'w_br_ssd',
           'w_br_lru', 'w_out', 'ln1_g', 'ln1_b', 'w_mlp1', 'b_mlp1', 'w_mlp2', 'b_mlp2', 'ln2_g', 'ln2_b']
SMALL = [n for n in WEIGHTS if n not in BIG]
GATE_STACKS = ["lru_wa", "lru_wi"]


class _Reducer:
    def __init__(self, core_id, chip_id):
        self.core_id, self.chip_id = core_id, chip_id
        self.halves, self.recv, self.pending = {}, {}, []

    def _slabs(self, big, names):
        return [_chips_from_full(_unpack_win(big[n]) if n == "w_in" else big[n], n) for n in names]

    def _chip_sums(self, names, slabs, sibling):
        for n, s, o in zip(names, slabs, sibling):
            self.halves[n] = _sum_half(s, o, self.core_id, "half_" + n)
        return [self.halves[n][1] for n in names]

    def begin_swap(self, big, names):
        self.pending, self.slabs = list(names), self._slabs(big, names)
        return _Side("swap", self.slabs)

    def begin_scatter(self, sibling):
        return _Side("scatter", self._chip_sums(self.pending, self.slabs, sibling))

    def end(self, received):
        self.recv.update(zip(self.pending, received))

    def finish(self, big, names):
        slabs = self._slabs(big, names)
        sums = self._chip_sums(names, slabs, _swap_halves(slabs, "swap_halves_" + names[0]))
        self.recv.update(zip(names, _scatter_chips(sums)))
        mine = [_sum4(self.halves[n][0], self.recv[n], self.chip_id, "sum_" + n) for n in BIG]
        return dict(zip(BIG, mine)), dict(zip(BIG, _swap_cores(mine)))


def _full_from_chips(g4, name):
    if name in COL_SHARDED:
        return jnp.transpose(g4, (1, 0, 2)).reshape(g4.shape[1], 4 * g4.shape[2])
    return g4.reshape(4 * g4.shape[1], g4.shape[2])


def _chips_from_full(full, name):
    if name in COL_SHARDED:
        r, c = full.shape
        return jnp.transpose(full.reshape(r, 4, c // 4), (1, 0, 2))
    return full.reshape(4, full.shape[0] // 4, full.shape[1])


def kernel(x, c, ctx, c_ctx, w_mod, b_mod, w_in, b_gate, ssd_conv_w, ssd_conv_b, ssd_dt_bias, ssd_a_log, ssd_d, ssd_norm_w, lru_conv_w, lru_conv_b, lru_wa, lru_ba, lru_wi, lru_bi, lru_lambda, w_br_ssd, w_br_lru, w_out, ln1_g, ln1_b, w_mlp1, b_mlp1, w_mlp2, b_mlp2, ln2_g, ln2_b, loss_target, m_c_ctx, m_w_mod, m_b_mod, m_w_in, m_b_gate, m_ssd_conv_w, m_ssd_conv_b, m_ssd_dt_bias, m_ssd_a_log, m_ssd_d, m_ssd_norm_w, m_lru_conv_w, m_lru_conv_b, m_lru_wa, m_lru_ba, m_lru_wi, m_lru_bi, m_lru_lambda, m_w_br_ssd, m_w_br_lru, m_w_out, m_ln1_g, m_ln1_b, m_w_mlp1, m_b_mlp1, m_w_mlp2, m_b_mlp2, m_ln2_g, m_ln2_b, v_c_ctx, v_w_mod, v_b_mod, v_w_in, v_b_gate, v_ssd_conv_w, v_ssd_conv_b, v_ssd_dt_bias, v_ssd_a_log, v_ssd_d, v_ssd_norm_w, v_lru_conv_w, v_lru_conv_b, v_lru_wa, v_lru_ba, v_lru_wi, v_lru_bi, v_lru_lambda, v_w_br_ssd, v_w_br_lru, v_w_out, v_ln1_g, v_ln1_b, v_w_mlp1, v_b_mlp1, v_w_mlp2, v_b_mlp2, v_ln2_g, v_ln2_b):
    given = dict(locals())
    w = {n: given[n] for n in WEIGHTS}
    mom = {n: given["m_" + n] for n in WEIGHTS}
    var = {n: given["v_" + n] for n in WEIGHTS}
    chip = 2 * lax.axis_index("x") + lax.axis_index("y")

    shard2d = {n: w[n].reshape(w[n].shape[-2:]) for n in BIG}
    small_pack = _flatten([w[n] for n in SMALL_SHARDED], rows_mult=16)
    first = ["w_mod", "w_in"]
    gathered = _gather_chips([shard2d[n].astype(BF16) for n in first] + [small_pack])
    full = {n: _full_from_chips(g, n) for n, g in zip(first, gathered[:-1])}
    full["w_in"] = _pack_win(full["w_in"])
    per_chip = [_unflatten(gathered[-1][q], [w[n].shape for n in SMALL_SHARDED]) for q in range(4)]
    sm = {n: jnp.concatenate([per_chip[q][i] for q in range(4)], axis=-1) for i, n in enumerate(SMALL_SHARDED)}
    for n in SMALL:
        if n not in sm:
            sm[n] = w[n]
    sm = {n: (a.reshape(a.shape[1:]) if a.ndim >= 3 else a) for n, a in sm.items()}

    core_id = lax.axis_index("c").astype(jnp.int32).reshape(1)
    reducer = _Reducer(core_id, chip.astype(jnp.int32).reshape(1))
    loss, gx, gbig, gsmall = _local_step(x, c, ctx, loss_target, sm, full["w_mod"], full["w_in"],
                                         [shard2d[n].astype(BF16) for n in LATE], late_are_shards=True, reducer=reducer)
    mine, other = reducer.finish(gbig, ["w_mod"])
    out = {}
    for n in BIG:
        shp = shard2d[n].shape
        res = _adam_halves(mine[n], other[n], shard2d[n], mom[n].reshape(shp), var[n].reshape(shp), core_id, "adam_" + n)
        out[n] = [r.reshape(w[n].shape) for r in res]

    tiny = [n for n in SMALL if n not in GATE_STACKS]
    as_rows = lambda a: a.reshape(-1, 128)
    summed = _allreduce_small([_flatten([gsmall[n] for n in tiny])] + [as_rows(gsmall[n]) for n in GATE_STACKS],
                              [False] + [True] * len(GATE_STACKS))
    gs = {}
    for n, g in zip(tiny, _unflatten(summed[0], [gsmall[n].shape for n in tiny])):
        if n in SMALL_SHARDED:
            width = w[n].shape[-1]
            g = lax.dynamic_slice_in_dim(g, chip * width, width, axis=g.ndim - 1)
        gs[n] = g.reshape(w[n].shape)
    shapes = [w[n].shape for n in tiny]
    d_s, m_s, v_s = _adam_flat(_flatten([gs[n] for n in tiny]), _flatten([w[n] for n in tiny]),
                               _flatten([mom[n] for n in tiny]), _flatten([var[n] for n in tiny]), "adam_small")
    for n, d_, m_, v_ in zip(tiny, _unflatten(d_s, shapes), _unflatten(m_s, shapes), _unflatten(v_s, shapes)):
        out[n] = [gs[n], d_, m_, v_]
    for n, g in zip(GATE_STACKS, summed[1:]):
        res = _adam_flat(g, as_rows(w[n]), as_rows(mom[n]), as_rows(var[n]), "adam_" + n)
        out[n] = [r.reshape(w[n].shape) for r in (g, *res)]

    loss = lax.psum(loss, ("x", "y", "c"))
    return (loss, gx, *[out[n][0] for n in WEIGHTS], *[out[n][1] for n in WEIGHTS], *[out[n][2] for n in WEIGHTS],
            *[out[n][3] for n in WEIGHTS])
```

```python
import functools

import numpy as np
import jax
import jax.numpy as jnp
from jax import lax
from jax.experimental import pallas as pl
from jax.experimental.pallas import tpu as pltpu

F32, BF16 = jnp.float32, jnp.bfloat16
S = jax.ShapeDtypeStruct
BS = pl.BlockSpec
MESH = pl.DeviceIdType.MESH

D = 1024
GRID_W = 64
SSD_INNER, SSD_G, SSD_N, SSD_L = 2048, 8, 128, 128
SSD_GW = 512
MLP_H = 4096
N_MOD = 6
ALPHA = 2.0 ** 0.25
LN_EPS, RMS_EPS = 1e-6, 1e-5
LRU_C = 8.0
P_XBC, P_LRU, P_Z, P_LG, P_MG, P_DT, P_W = 0, 4096, 5120, 7168, 8192, 10240, 10368
P_CB = 3456
IN_COLS = 10304
LRU_CB = 256
ADAM_LR, ADAM_B1, ADAM_B2, ADAM_EPS, ADAM_WD, ADAM_STEP = 0.001, 0.9, 0.999, 1e-08, 0.01, 10
VMEM_LIMIT = 56 * 2 ** 20


def _params(**kw):
    return pltpu.CompilerParams(vmem_limit_bytes=VMEM_LIMIT, **kw)


def _dot(a, b):
    return jnp.dot(a.astype(BF16), b.astype(BF16), preferred_element_type=F32)


def _dot_nt(a, b):
    return lax.dot_general(a.astype(BF16), b.astype(BF16), (((1,), (1,)), ((), ())), preferred_element_type=F32)


def _dot_tn(a, b):
    return lax.dot_general(a.astype(BF16), b.astype(BF16), (((0,), (0,)), ((), ())), preferred_element_type=F32)


@jax.custom_vjp
def _mm(a, b):
    return _dot(a, b)


def _cast_pair(a, b):
    return a.astype(BF16), b.astype(BF16)


def _mm_f(a, b):
    r = _cast_pair(a, b)
    return _dot(*r), r


def _mm_b(r, g):
    g = g.astype(BF16)
    return _dot_nt(g, r[1]), _dot_tn(r[0], g)


_mm.defvjp(_mm_f, _mm_b)


@jax.custom_vjp
def _mm_nt(a, b):
    return _dot_nt(a, b)


def _mm_nt_f(a, b):
    r = _cast_pair(a, b)
    return _dot_nt(*r), r


def _mm_nt_b(r, g):
    g = g.astype(BF16)
    return _dot(g, r[1]), _dot_tn(g, r[0])


_mm_nt.defvjp(_mm_nt_f, _mm_nt_b)


@jax.custom_vjp
def _mm_tn(a, b):
    return _dot_tn(a, b)


def _mm_tn_f(a, b):
    r = _cast_pair(a, b)
    return _dot_tn(*r), r


def _mm_tn_b(r, g):
    g = g.astype(BF16)
    return _dot_nt(r[1], g), _dot(r[0], g)


_mm_tn.defvjp(_mm_tn_f, _mm_tn_b)

def _split3(v):
    h = v.astype(BF16)
    r = v - h.astype(F32)
    m = r.astype(BF16)
    return h, m, (r - m.astype(F32)).astype(BF16)


def _sel_dot(sel, v, dims):
    sel_first = dims[0] == "s"
    dn = {"sv": (((1,), (0,)), ((), ())), "sTv": (((0,), (0,)), ((), ())), "vs": (((1,), (0,)), ((), ())),
          "vsT": (((1,), (1,)), ((), ()))}[dims]
    out = None
    for part in _split3(v):
        a, b = (sel, part) if sel_first else (part, sel)
        term = lax.dot_general(a, b, dn, preferred_element_type=F32)
        out = term if out is None else out + term
    return out


@jax.custom_vjp
def _cum_mm(tri, v):
    return _sel_dot(tri, v, "sv")


_cum_mm.defvjp(lambda tri, v: (_sel_dot(tri, v, "sv"), tri),
               lambda tri, g: (jnp.zeros_like(tri), _sel_dot(tri, g, "sTv")))


@jax.custom_vjp
def _xp_mm(v, e):
    return _sel_dot(e, v, "vs")


_xp_mm.defvjp(lambda v, e: (_sel_dot(e, v, "vs"), e),
              lambda e, g: (_sel_dot(e, g, "vsT"), jnp.zeros_like(e)))


def _sigmoid(x):
    return 0.5 * jnp.tanh(0.5 * x) + 0.5


def _silu(x):
    return x * _sigmoid(x)


def _dsilu(x):
    s = _sigmoid(x)
    return s * (1.0 + x * (1.0 - s))


def _softplus(x):
    return jnp.maximum(x, 0.0) + jnp.log1p(jnp.exp(-jnp.abs(x)))


def _gelu(x):
    return 0.5 * x * (1.0 + jnp.tanh(0.7978845608028654 * (x + 0.044715 * x * x * x)))


def _ln(x):
    mu = jnp.mean(x, axis=-1, keepdims=True)
    xc = x - mu
    var = jnp.mean(xc * xc, axis=-1, keepdims=True)
    return xc * lax.rsqrt(var + LN_EPS)


def _modln(x, shift, scale):
    return _ln(x) * (1.0 + scale) + shift


def _resln(x, sub, gate, g, b):
    return _ln(ALPHA * x + gate * sub) * g + b


def _grms(y, z, w):
    u = y * _silu(z)
    return u * lax.rsqrt(jnp.mean(u * u, axis=-1, keepdims=True) + RMS_EPS) * w


def _colsum(v):
    return jnp.sum(v, axis=0, keepdims=True)


def _conv_taps(width, period):
    def masks(rows):
        pos = lax.broadcasted_iota(jnp.int32, (rows, width), 0)
        if rows != period:
            pos = pos & (period - 1)
        return [pos >= 2 - k if k < 2 else pos < period + 2 - k for k in range(4)]
    return masks


def _conv_fwd(raw, w, b, masks):
    rows = raw.shape[0]
    pre = b + raw * w[2:3, :]
    for k in (0, 1, 3):
        sh = pltpu.roll(raw, (2 - k) % rows, 0)
        pre = pre + jnp.where(masks[k], sh, 0.0) * w[k:k + 1, :]
    return pre


def _for_rows(t, rb, fn):
    n = t // rb
    unroll = 4 if n % 4 == 0 else 1

    def step(i, carry):
        for u in range(unroll):
            fn(pl.multiple_of((i * unroll + u) * rb, rb))
        return carry

    lax.fori_loop(0, n // unroll, step, 0)


def _loop_unrolled(n, unroll, body, init):
    def step(i, carry):
        for u in range(unroll):
            carry = body(i * unroll + u, carry)
        return carry

    return lax.fori_loop(0, n // unroll, step, init)


def _conv_bwd(dpre, raw, w, masks):
    rows = raw.shape[0]
    draw = dpre * w[2:3, :]
    dws = []
    for k in range(4):
        if k == 2:
            dws.append(_colsum(dpre * raw))
            continue
        back = pltpu.roll(jnp.where(masks[k], dpre, 0.0), (k - 2) % rows, 0)
        dws.append(_colsum(back * raw))
        draw = draw + back * w[k:k + 1, :]
    return draw, jnp.concatenate(dws, axis=0), _colsum(dpre)


def _mod_fwd(cc, wmod, bmod):
    def body(cc_ref, w_ref, b_ref, o_ref):
        o_ref[...] = _dot(_silu(cc_ref[...]), w_ref[...]) + b_ref[...]

    return pl.pallas_call(
        body, grid=(N_MOD,), name="mod_fwd",
        in_specs=[BS((8, D), lambda j: (0, 0)), BS((D, D), lambda j: (0, j)), BS((1, D), lambda j: (0, j))],
        out_specs=BS((8, D), lambda j: (0, j)), out_shape=S((8, N_MOD * D), F32), compiler_params=_params(),
    )(cc, wmod, bmod)


def _mod_bwd(cc, wmod, dm):
    def body(cc_ref, w_ref, dm_ref, dw_ref, db_ref, dcc_ref):
        j = pl.program_id(0)
        c = cc_ref[...]
        dmv = dm_ref[...]
        dw_ref[...] = _dot_tn(_silu(c), dmv)
        db_ref[...] = _colsum(dmv)

        @pl.when(j == 0)
        def _():
            dcc_ref[...] = jnp.zeros_like(dcc_ref)

        dcc_ref[...] += _dot_nt(dmv, w_ref[...]) * _dsilu(c)

    return pl.pallas_call(
        body, grid=(N_MOD,), name="mod_bwd",
        in_specs=[BS((8, D), lambda j: (0, 0)), BS((D, D), lambda j: (0, j)), BS((8, D), lambda j: (0, j))],
        out_specs=[BS((D, D), lambda j: (0, j)), BS((1, D), lambda j: (0, j)), BS((8, D), lambda j: (0, 0))],
        out_shape=[S((D, N_MOD * D), F32), S((1, N_MOD * D), F32), S((8, D), F32)], compiler_params=_params(),
    )(cc, wmod, dm)


def _inproj_fwd(xa, m4, win, rt, n_lat_tiles, tiles_per_b, ctx_row, name):
    n_tiles = xa.shape[0] // rt

    def mrow(i):
        return jnp.where(i < n_lat_tiles, i // tiles_per_b, ctx_row)

    def body(x_ref, sh_ref, sc_ref, w_hbm, p_ref, h_ref, w_vm, sem):
        @pl.when(pl.program_id(0) == 0)
        def _():
            cp = pltpu.make_async_copy(w_hbm, w_vm, sem)
            cp.start()
            cp.wait()

        hb = _modln(x_ref[...], sh_ref[...], sc_ref[...]).astype(BF16)
        h_ref[...] = hb
        for j in range(P_W // P_CB):
            sl = slice(j * P_CB, (j + 1) * P_CB)
            p_ref[:, sl] = jnp.dot(hb, w_vm[:, sl], preferred_element_type=F32)

    return pl.pallas_call(
        body, grid=(n_tiles,), name=name,
        in_specs=[BS((rt, D), lambda i: (i, 0)),
                  BS((None, None, 1, D), lambda i: (mrow(i), 0, 0, 0)),
                  BS((None, None, 1, D), lambda i: (mrow(i), 1, 0, 0)),
                  BS(memory_space=pl.ANY)],
        out_specs=[BS((rt, P_W), lambda i: (i, 0)), BS((rt, D), lambda i: (i, 0))],
        out_shape=[S((xa.shape[0], P_W), F32), S((xa.shape[0], D), BF16)],
        scratch_shapes=[pltpu.VMEM((D, P_W), BF16), pltpu.SemaphoreType.DMA(())], compiler_params=_params(),
    )(xa, m4, m4, win)


def _inproj_bwd(xa, m4, win, dproj, rt, tile0, n_tiles, tiles_per_b, ctx_row, latent, dxres, side=None):
    def mrow(i):
        return (i // tiles_per_b) if latent else ctx_row

    def body(*refs):
        if side is None:
            return compute(*refs)
        own, exchange = side.split(refs, len(args), len(out_shape), 2)
        side.around(exchange, pl.program_id(0), n_tiles, lambda: compute(*own))

    def compute(x_ref, sh_ref, sc_ref, dp_ref, w_hbm, *rest):
        if latent:
            dxr_ref, gx_ref, dm_ref, w_vm, sem = rest
        else:
            dm_ref, w_vm, sem = rest
        i = pl.program_id(0)

        @pl.when(i == 0)
        def _():
            cp = pltpu.make_async_copy(w_hbm, w_vm, sem)
            cp.start()
            cp.wait()

        dh = lax.dot_general(dp_ref[...], w_vm[...], (((1,), (1,)), ((), ())), preferred_element_type=F32)
        _, vjp = jax.vjp(_modln, x_ref[...], sh_ref[...], sc_ref[...])
        dx, dsh, dsc = vjp(dh)
        if latent:
            gx_ref[...] = dx + dxr_ref[...]

        @pl.when(i % tiles_per_b == 0)
        def _():
            dm_ref[...] = jnp.zeros_like(dm_ref)

        dm_ref[0:1, :] += dsh
        dm_ref[1:2, :] += dsc

    nb = n_tiles // tiles_per_b
    in_specs = [BS((rt, D), lambda i: (tile0 + i, 0)),
                BS((None, None, 1, D), lambda i: (mrow(i), 0, 0, 0)),
                BS((None, None, 1, D), lambda i: (mrow(i), 1, 0, 0)),
                BS((rt, P_W), lambda i: (tile0 + i, 0)),
                BS(memory_space=pl.ANY)]
    args = [xa, m4, m4, dproj, win]
    dm_spec = BS((None, 2, D), lambda i: (i // tiles_per_b, 0, 0))
    if latent:
        in_specs.append(BS((rt, D), lambda i: (i, 0)))
        args.append(dxres)
        out_specs = [BS((rt, D), lambda i: (i, 0)), dm_spec]
        out_shape = [S((n_tiles * rt, D), F32), S((nb, 2, D), F32)]
    else:
        out_specs = [dm_spec]
        out_shape = [S((nb, 2, D), F32)]
    extra = side.arrays if side else []
    return pl.pallas_call(
        body, grid=(n_tiles,), name="inproj_bwd_lat" if latent else "inproj_bwd_ctx",
        in_specs=in_specs + [_HBM] * len(extra), out_specs=out_specs + [_HBM] * len(extra),
        out_shape=out_shape + (side.out_shapes if side else []),
        scratch_shapes=[pltpu.VMEM((D, P_W), BF16), pltpu.SemaphoreType.DMA(())] + (side.sems if side else []),
        compiler_params=_params(),
    )(*args, *extra)


def _matmul_tn(a, b, tm, tn, tk, name):
    k, m = a.shape
    n = b.shape[1]

    def body(a_ref, b_ref, o_ref):
        @pl.when(pl.program_id(2) == 0)
        def _():
            o_ref[...] = jnp.zeros_like(o_ref)

        o_ref[...] += lax.dot_general(a_ref[...], b_ref[...], (((0,), (0,)), ((), ())), preferred_element_type=F32)

    return pl.pallas_call(
        body, grid=(m // tm, n // tn, k // tk), name=name,
        in_specs=[BS((tk, tm), lambda i, j, kk: (kk, i)), BS((tk, tn), lambda i, j, kk: (kk, j))],
        out_specs=BS((tm, tn), lambda i, j, kk: (i, j)), out_shape=S((m, n), F32), compiler_params=_params(),
    )(a, b)


def _matmul_tn2(a1, b1, a2, b2, tm, tn, tk, name):
    k1, m = a1.shape
    n = b1.shape[1]
    n1, n2 = k1 // tk, a2.shape[0] // tk

    def body(a1_ref, b1_ref, a2_ref, b2_ref, o_ref):
        kk = pl.program_id(2)

        @pl.when(kk == 0)
        def _():
            o_ref[...] = jnp.zeros_like(o_ref)

        @pl.when(kk < n1)
        def _():
            o_ref[...] += lax.dot_general(a1_ref[...], b1_ref[...], (((0,), (0,)), ((), ())), preferred_element_type=F32)

        @pl.when(kk >= n1)
        def _():
            o_ref[...] += lax.dot_general(a2_ref[...], b2_ref[...], (((0,), (0,)), ((), ())), preferred_element_type=F32)

    first = lambda kk: jnp.minimum(kk, n1 - 1)
    second = lambda kk: jnp.maximum(kk - n1, 0)
    return pl.pallas_call(
        body, grid=(m // tm, n // tn, n1 + n2), name=name,
        in_specs=[BS((tk, tm), lambda i, j, kk: (first(kk), i)), BS((tk, tn), lambda i, j, kk: (first(kk), j)),
                  BS((tk, tm), lambda i, j, kk: (second(kk), i)), BS((tk, tn), lambda i, j, kk: (second(kk), j))],
        out_specs=BS((tm, tn), lambda i, j, kk: (i, j)), out_shape=S((m, n), F32), compiler_params=_params(),
    )(a1, b1, a2, b2)


def _ssd_consts(heads_per_tile):
    n = SSD_L
    ii = lax.broadcasted_iota(jnp.int32, (n, n), 0)
    jj = lax.broadcasted_iota(jnp.int32, (n, n), 1)
    er = lax.broadcasted_iota(jnp.int32, (128, 256), 0)
    ec = lax.broadcasted_iota(jnp.int32, (128, 256), 1) >> 6
    lane = lax.broadcasted_iota(jnp.int32, (1, 128 * heads_per_tile), 1) >> 6
    per_dir = []
    for d in (0, 1):
        mask = (jj >= ii) if d else (jj <= ii)
        per_dir.append((mask, mask.astype(BF16), (er == ec + 4 * d).astype(BF16)))
    return per_dir, [(lane == h).astype(F32) for h in range(2 * heads_per_tile)]


def _ssd_chunk(x, bm, cm, dtc, dtx, alog, hst, consts, hmasks, rev):
    n = SSD_L
    mask, tri, e = consts
    cum = _cum_mm(tri, dtc * (-jnp.exp(alog)))
    cum_x = _xp_mm(cum, e)
    tot_x = cum_x[0:1, :] if rev else cum_x[n - 1:n, :]
    xd = x * dtx
    hn = jnp.exp(tot_x) * hst + _mm_tn(bm, xd * jnp.exp(tot_x - cum_x))
    if cm is None:
        return hn
    cum_t = cum.T
    cb = _mm_nt(cm, bm)
    if len(hmasks) == 4:
        y = jnp.exp(cum_x) * _mm(cm, hst)
        for h in range(4):
            k = 4 * rev + h
            decay = jnp.exp(jnp.where(mask, cum[:, k:k + 1] - cum_t[k:k + 1, :], -1e30))
            y = y + _mm(cb * decay, xd * hmasks[h])
        return y, hn
    pairs = []
    for p in range(2):
        xdp = xd[:, 128 * p:128 * p + 128]
        yp = None
        for hh in range(2):
            k = 4 * rev + 2 * p + hh
            decay = jnp.exp(jnp.where(mask, cum[:, k:k + 1] - cum_t[k:k + 1, :], -1e30))
            term = _mm(cb * decay, xdp * hmasks[hh])
            yp = term if yp is None else yp + term
        pairs.append(yp)
    return jnp.exp(cum_x) * _mm(cm, hst) + jnp.concatenate(pairs, axis=1), hn


def _ssd_fwd(proj, dtg, cw, cb, dtb, alog, drow, h0f, h0b, nb, t, period, blk0, need_y, gather=()):
    nc = t // SSD_L
    rb = period
    assert t % rb == 0
    unroll = 4 if nc % 4 == 0 else (2 if nc % 2 == 0 else 1)
    masks_of = _conv_taps(SSD_GW, period)
    ng = len(gather)
    n_out = 5 if need_y else 4

    def body(p_ref, dt_ref, cw_ref, cb_ref, dtb_ref, al_ref, d_ref, h0f_ref, h0b_ref, *rest):
        g_ins, rest = rest[:ng], rest[ng:]
        outs, g_outs, (act, dts, dtxs), g_sems = rest[:n_out], rest[n_out:n_out + ng], rest[n_out + ng:n_out + ng + 3], \
            rest[n_out + ng + 3:]
        if need_y:
            y_ref, hsf_ref, hsb_ref, sf_ref, sb_ref = outs
        else:
            hsf_ref, hsb_ref, sf_ref, sb_ref = outs
        if ng:
            exchange = _GatherExchange(g_ins, g_outs, g_sems)
            step = pl.program_id(0) * SSD_G + pl.program_id(1)
            pl.when(step == 0)(exchange.begin)
        masks = masks_of(rb)
        per_dir, hmasks = _ssd_consts(2)

        def prologue(r0):
            rows = pl.ds(r0, rb)
            a = _silu(_conv_fwd(p_ref[rows, :], cw_ref[...], cb_ref[...], masks))
            act[rows, :] = a
            if need_y:
                y_ref[rows, :] = d_ref[...] * a[:, 0:256]
            dtv = _softplus(dt_ref[rows, :] + dtb_ref[...])
            dts[rows, :] = dtv
            for d in (0, 1):
                dtxs[rows, 256 * d:256 * d + 256] = _xp_mm(dtv, per_dir[d][2])

        _for_rows(t, rb, prologue)
        al = al_ref[...]

        def chunk(ci, carry):
            out = []
            for d, hst, hs_ref in ((0, carry[0], hsf_ref), (1, carry[1], hsb_ref)):
                c = (nc - 1 - ci) if d else ci
                r0 = pl.multiple_of(c * SSD_L, SSD_L)
                a = act[pl.ds(r0, SSD_L), :]
                hs_ref[c] = hst
                res = _ssd_chunk(a[:, 0:256], a[:, 256:384], a[:, 384:512] if need_y else None,
                                 dts[pl.ds(r0, SSD_L), :], dtxs[pl.ds(r0, SSD_L), 256 * d:256 * d + 256], al, hst,
                                 per_dir[d], hmasks, d)
                if need_y:
                    y_ref[pl.ds(r0, SSD_L), :] += res[0]
                    res = res[1]
                out.append(res)
            return tuple(out)

        sf_ref[...], sb_ref[...] = _loop_unrolled(nc, unroll, chunk, (h0f_ref[...], h0b_ref[...]))
        if ng:
            pl.when(step == nb * SSD_G - 1)(exchange.finish)

    gspec = lambda shp: BS((None,) + shp, lambda b, g: (g,) + (0,) * len(shp))
    st_spec = BS((None, None, SSD_N, 256), lambda b, g: (b, g, 0, 0))
    hs_spec = BS((None, None, nc, SSD_N, 256), lambda b, g: (b, g, 0, 0, 0))
    in_specs = [BS((t, SSD_GW), lambda b, g: (blk0 + b, g)), BS((None, t, 128), lambda b, g: (g, blk0 + b, 0)),
                gspec((4, SSD_GW)), gspec((1, SSD_GW)), gspec((1, 128)), gspec((1, 128)), gspec((1, 256)),
                st_spec, st_spec]
    out_specs = [hs_spec, hs_spec, st_spec, st_spec]
    out_shape = [S((nb, SSD_G, nc, SSD_N, 256), F32)] * 2 + [S((nb, SSD_G, SSD_N, 256), F32)] * 2
    if need_y:
        out_specs = [BS((t, 256), lambda b, g: (b, g))] + out_specs
        out_shape = [S((nb * t, SSD_INNER), F32)] + out_shape
    return pl.pallas_call(
        body, grid=(nb, SSD_G), name="ssd_fwd_lat" if need_y else "ssd_fwd_ctx",
        in_specs=in_specs + [_HBM] * ng, out_specs=out_specs + [_HBM] * ng,
        out_shape=out_shape + _gather_out_shapes(gather),
        scratch_shapes=[pltpu.VMEM((t, SSD_GW), F32), pltpu.VMEM((t, 128), F32), pltpu.VMEM((t, SSD_GW), F32)]
        + (_gather_sems(ng) if ng else []),
        compiler_params=_params(),
    )(proj, dtg, cw, cb, dtb, alog, drow, h0f, h0b, *gather)


def _ssd_bwd(proj, dtg, cw, cb, dtb, alog, drow, hsf, hsb, dy, dsf, dsb, dproj, nb, t, period, blk0, need_y, side=None):
    nc = t // SSD_L
    rb = period
    assert t % rb == 0
    unroll = 2 if nc % 2 == 0 else 1
    masks_of = _conv_taps(SSD_GW, period)

    def body(*refs):
        if side is None:
            return compute(*refs)
        own, exchange = side.split(refs, len(args), len(out_shape), 5)
        side.around(exchange, pl.program_id(0) * SSD_G + pl.program_id(1), nb * SSD_G, lambda: compute(*own))

    def compute(*refs):
        if need_y:
            (p_ref, dt_ref, cw_ref, cb_ref, dtb_ref, al_ref, d_ref, hsf_ref, hsb_ref, dy_ref, dsf_ref, dsb_ref, _,
             dp_ref, ddt_ref, dhf_ref, dhb_ref, dcw_ref, dcb_ref, ddtb_ref, dal_ref, dd_ref,
             pre, dact, dts, ddts, dtxs) = refs
        else:
            (p_ref, dt_ref, cw_ref, cb_ref, dtb_ref, al_ref, d_ref, hsf_ref, hsb_ref, dsf_ref, dsb_ref, _,
             dp_ref, ddt_ref, dhf_ref, dhb_ref, dcw_ref, dcb_ref, ddtb_ref, dal_ref, dd_ref,
             pre, dact, dts, ddts, dtxs) = refs
            dy_ref = None
        b, g = pl.program_id(0), pl.program_id(1)

        @pl.when(jnp.logical_and(b == 0, g == 0))
        def _():
            for r in (dcw_ref, dcb_ref, ddtb_ref, dal_ref, dd_ref):
                r[...] = jnp.zeros_like(r)

        masks = masks_of(rb)
        per_dir, hmasks = _ssd_consts(1)

        def prologue(r0):
            rows = pl.ds(r0, rb)
            pre[rows, :] = _conv_fwd(p_ref[rows, :], cw_ref[...], cb_ref[...], masks)
            dtv = _softplus(dt_ref[rows, :] + dtb_ref[...])
            dts[rows, :] = dtv
            for d in (0, 1):
                dtxs[rows, 256 * d:256 * d + 256] = _xp_mm(dtv, per_dir[d][2])
            dact[rows, :] = jnp.zeros((rb, SSD_GW), F32)
            ddts[rows, :] = jnp.zeros((rb, 128), F32)

        _for_rows(t, rb, prologue)
        al = al_ref[...]
        def chunk(ci, carry):
            dal_c = carry[2]
            dhs_out = []
            for d, dh, hs_ref in ((0, carry[0], hsf_ref), (1, carry[1], hsb_ref)):
                c = ci if d else (nc - 1 - ci)
                r0 = pl.multiple_of(c * SSD_L, SSD_L)
                a = _silu(pre[pl.ds(r0, SSD_L), :])
                dtc = dts[pl.ds(r0, SSD_L), :]
                dtx = dtxs[pl.ds(r0, SSD_L), 256 * d:256 * d + 256]
                if need_y:
                    fn = lambda x_, bm_, cm_, dt_, dx_, al_, hs_: _ssd_chunk(x_, bm_, cm_, dt_, dx_, al_, hs_, per_dir[d],
                                                                             hmasks, d)
                    _, vjp = jax.vjp(fn, a[:, 0:256], a[:, 256:384], a[:, 384:512], dtc, dtx, al, hs_ref[c])
                    dx, dbm, dcm, ddtc, ddtx, dal_k, dhs = vjp((dy_ref[pl.ds(r0, SSD_L), :], dh))
                    dact[pl.ds(r0, SSD_L), 384:512] += dcm
                else:
                    fn = lambda x_, bm_, dt_, dx_, al_, hs_: _ssd_chunk(x_, bm_, None, dt_, dx_, al_, hs_, per_dir[d],
                                                                        hmasks, d)
                    _, vjp = jax.vjp(fn, a[:, 0:256], a[:, 256:384], dtc, dtx, al, hs_ref[c])
                    dx, dbm, ddtc, ddtx, dal_k, dhs = vjp(dh)
                dact[pl.ds(r0, SSD_L), 0:256] += dx
                dact[pl.ds(r0, SSD_L), 256:384] += dbm
                ddts[pl.ds(r0, SSD_L), :] += ddtc + _dot_nt(ddtx, per_dir[d][2])
                dhs_out.append(dhs)
                dal_c = dal_c + dal_k
            return dhs_out[0], dhs_out[1], dal_c

        dhf_ref[...], dhb_ref[...], dal_acc = _loop_unrolled(
            nc, unroll, chunk, (dsf_ref[...], dsb_ref[...], jnp.zeros((1, 128), F32)))

        def epilogue(r0):
            rows = pl.ds(r0, rb)
            prev = pre[rows, :]
            if need_y:
                dyv = dy_ref[rows, :]
                dact[rows, 0:256] += d_ref[...] * dyv
                dd_ref[g] += _colsum(dyv * _silu(prev[:, 0:256]))
            dpre = dact[rows, :] * _dsilu(prev)
            draw, dcw, dcb = _conv_bwd(dpre, p_ref[rows, :], cw_ref[...], masks)
            dp_ref[rows, :] = draw.astype(BF16)
            dcw_ref[g] += dcw
            dcb_ref[g] += dcb
            ddraw = ddts[rows, :] * _sigmoid(dt_ref[rows, :] + dtb_ref[...])
            ddt_ref[rows, :] = ddraw
            ddtb_ref[g] += _colsum(ddraw)

        _for_rows(t, rb, epilogue)
        dal_ref[g] += dal_acc

    gspec = lambda shp: BS((None,) + shp, lambda b, g: (g,) + (0,) * len(shp))
    full = lambda shp: BS(shp, lambda b, g: (0,) * len(shp))
    st_spec = BS((None, None, SSD_N, 256), lambda b, g: (b, g, 0, 0))
    hs_spec = BS((None, None, nc, SSD_N, 256), lambda b, g: (b, g, 0, 0, 0))
    p_spec = BS((t, SSD_GW), lambda b, g: (blk0 + b, g))
    in_specs = [p_spec, BS((None, t, 128), lambda b, g: (g, blk0 + b, 0)),
                gspec((4, SSD_GW)), gspec((1, SSD_GW)), gspec((1, 128)), gspec((1, 128)), gspec((1, 256)),
                hs_spec, hs_spec]
    args = [proj, dtg, cw, cb, dtb, alog, drow, hsf, hsb]
    if need_y:
        in_specs.append(BS((t, 256), lambda b, g: (b, g)))
        args.append(dy)
    in_specs += [st_spec, st_spec, BS(memory_space=pl.ANY)]
    args += [dsf, dsb, dproj]
    out_specs = [p_spec, BS((None, t, 128), lambda b, g: (g, b, 0)), st_spec, st_spec,
                 full((SSD_G, 4, SSD_GW)), full((SSD_G, 1, SSD_GW)), full((SSD_G, 1, 128)), full((SSD_G, 1, 128)),
                 full((SSD_G, 1, 256))]
    out_shape = [S(dproj.shape, BF16), S((SSD_G, nb * t, 128), F32),
                 S((nb, SSD_G, SSD_N, 256), F32), S((nb, SSD_G, SSD_N, 256), F32),
                 S((SSD_G, 4, SSD_GW), F32), S((SSD_G, 1, SSD_GW), F32), S((SSD_G, 1, 128), F32),
                 S((SSD_G, 1, 128), F32), S((SSD_G, 1, 256), F32)]
    extra = side.arrays if side else []
    return pl.pallas_call(
        body, grid=(nb, SSD_G), name="ssd_bwd_lat" if need_y else "ssd_bwd_ctx",
        in_specs=in_specs + [_HBM] * len(extra), out_specs=out_specs + [_HBM] * len(extra),
        out_shape=out_shape + (side.out_shapes if side else []),
        input_output_aliases={len(args) - 1: 0},
        scratch_shapes=[pltpu.VMEM((t, SSD_GW), F32), pltpu.VMEM((t, SSD_GW), F32), pltpu.VMEM((t, 128), F32),
                        pltpu.VMEM((t, 128), F32), pltpu.VMEM((t, SSD_GW), F32)] + (side.sems if side else []),
        compiler_params=_params(),
    )(*args, *extra)


def _lru_gate(u, wa, ba, wi, bi, lam):
    r = _sigmoid(_mm(u, wa) + ba)
    i = _sigmoid(_mm(u, wi) + bi)
    log_a = -LRU_C * r * _softplus(-lam)
    a = jnp.exp(log_a)
    x2 = 2.0 * log_a
    em1 = jnp.where(x2 > -0.01, x2 * (1.0 + x2 * (0.5 + x2 * (1.0 / 6.0 + x2 * (1.0 / 24.0)))), a * a - 1.0)
    return a, jnp.sqrt(-em1) * (i * u)


def _scan_pair(fwd, rev, nblk, width):
    row = lax.broadcasted_iota(jnp.int32, (8, width), 0)

    def block(a_ref, b_ref, h_ref, st, carry, reverse):
        av, bv = a_ref[pl.ds(st, 8), :], b_ref[pl.ds(st, 8), :]
        for s in (1, 2, 4):
            ok = (row < 8 - s) if reverse else (row >= s)
            sh = (8 - s) if reverse else s
            a_sh = jnp.where(ok, pltpu.roll(av, sh, 0), 1.0)
            b_sh = jnp.where(ok, pltpu.roll(bv, sh, 0), 0.0)
            bv = av * b_sh + bv
            av = av * a_sh
        h = bv + av * carry
        h_ref[pl.ds(st, 8), :] = h
        return h[0:1, :] if reverse else h[7:8, :]

    def step(i, carry):
        cf, cr = carry
        cf = block(fwd[0], fwd[1], fwd[2], pl.multiple_of(i * 8, 8), cf, False)
        cr = block(rev[0], rev[1], rev[2], pl.multiple_of((nblk - 1 - i) * 8, 8), cr, True)
        return cf, cr

    return lax.fori_loop(0, nblk, step, (fwd[3], rev[3]))


def _lru_specs(t, blk0):
    p_spec = BS((t, LRU_CB), lambda b, q: (blk0 + b, P_LRU // LRU_CB + q))
    w_spec = BS((2, 2, 128, 128), lambda b, q: (0, q, 0, 0))
    v_spec = BS((2, LRU_CB), lambda b, q: (0, q))
    c_spec = lambda r: BS((r, LRU_CB), lambda b, q: (0, q))
    s_spec = BS((None, 2, LRU_CB), lambda b, q: (b, 0, q))
    return p_spec, w_spec, v_spec, c_spec, s_spec


def _lru_fwd(proj, cw, cb, wa, ba, wi, bi, lam, h0, nb, t, period, blk0, need_y):
    nq = D // LRU_CB
    masks_of = _conv_taps(LRU_CB, period)

    def body(p_ref, cw_ref, cb_ref, wa_ref, ba_ref, wi_ref, bi_ref, lam_ref, h0_ref, *rest):
        if need_y:
            y_ref, hf_ref, hb_ref, fin_ref, sa0, sb0, sa1, sb1 = rest
        else:
            hf_ref, hb_ref, fin_ref, sa0, sb0, sa1, sb1 = rest
        u = _conv_fwd(p_ref[...], cw_ref[...], cb_ref[...], masks_of(t))
        for d, (sa, sb) in enumerate(((sa0, sb0), (sa1, sb1))):
            for j in range(2):
                sl = slice(128 * j, 128 * j + 128)
                a, bb = _lru_gate(u[:, sl], wa_ref[d, j], ba_ref[d:d + 1, sl], wi_ref[d, j], bi_ref[d:d + 1, sl],
                                  lam_ref[d:d + 1, sl])
                sa[:, sl] = a
                sb[:, sl] = bb
        lf, lb = _scan_pair((sa0, sb0, hf_ref, h0_ref[0:1, :]), (sa1, sb1, hb_ref, h0_ref[1:2, :]), t // 8, LRU_CB)
        fin_ref[0:1, :] = lf
        fin_ref[1:2, :] = lb
        if need_y:
            y_ref[...] = hf_ref[...] + hb_ref[...]

    p_spec, w_spec, v_spec, c_spec, s_spec = _lru_specs(t, blk0)
    o_spec = BS((t, LRU_CB), lambda b, q: (b, q))
    out_specs = [o_spec, o_spec, s_spec]
    out_shape = [S((nb * t, D), F32), S((nb * t, D), F32), S((nb, 2, D), F32)]
    if need_y:
        out_specs = [o_spec] + out_specs
        out_shape = [S((nb * t, D), F32)] + out_shape
    return pl.pallas_call(
        body, grid=(nb, nq), name="lru_fwd_lat" if need_y else "lru_fwd_ctx",
        in_specs=[p_spec, c_spec(4), c_spec(1), w_spec, v_spec, w_spec, v_spec, v_spec, s_spec],
        out_specs=out_specs, out_shape=out_shape,
        scratch_shapes=[pltpu.VMEM((t, LRU_CB), F32)] * 4, compiler_params=_params(),
    )(proj, cw, cb, wa, ba, wi, bi, lam, h0)


def _lru_bwd(proj, cw, cb, wa, ba, wi, bi, lam, h0, hf, hb, dy, dfin, dproj, nb, t, period, blk0, need_y, side=None):
    nq = D // LRU_CB
    rc = min(256, t)
    masks_of = _conv_taps(LRU_CB, period)

    def body(*refs):
        if side is None:
            return compute(*refs)
        own, exchange = side.split(refs, len(args), len(out_shape), 7)
        side.around(exchange, pl.program_id(0) * nq + pl.program_id(1), nb * nq, lambda: compute(*own))

    def compute(*refs):
        if need_y:
            (p_ref, cw_ref, cb_ref, wa_ref, ba_ref, wi_ref, bi_ref, lam_ref, h0_ref, hf_ref, hb_ref, dy_ref, dfin_ref, _,
             dp_ref, dh0_ref, dcw_ref, dcb_ref, dwa_ref, dwi_ref, dba_ref, dbi_ref, dlam_ref,
             su, sa0, sa1, sc0, sc1, sg0, sg1) = refs
        else:
            (p_ref, cw_ref, cb_ref, wa_ref, ba_ref, wi_ref, bi_ref, lam_ref, h0_ref, hf_ref, hb_ref, dfin_ref, _,
             dp_ref, dh0_ref, dcw_ref, dcb_ref, dwa_ref, dwi_ref, dba_ref, dbi_ref, dlam_ref,
             su, sa0, sa1, sc0, sc1, sg0, sg1) = refs
            dy_ref = None
        b, q = pl.program_id(0), pl.program_id(1)

        @pl.when(jnp.logical_and(b == 0, q == 0))
        def _():
            for r in (dcw_ref, dcb_ref, dwa_ref, dwi_ref, dba_ref, dbi_ref, dlam_ref):
                r[...] = jnp.zeros_like(r)

        masks = masks_of(t)
        u = _conv_fwd(p_ref[...], cw_ref[...], cb_ref[...], masks)
        su[...] = u
        for d, sa in enumerate((sa0, sa1)):
            for j in range(2):
                sl = slice(128 * j, 128 * j + 128)
                a, _unused = _lru_gate(u[:, sl], wa_ref[d, j], ba_ref[d:d + 1, sl], wi_ref[d, j], bi_ref[d:d + 1, sl],
                                       lam_ref[d:d + 1, sl])
                sa[:, sl] = a
        rowi = lax.broadcasted_iota(jnp.int32, (t, LRU_CB), 0)
        last, first = rowi == t - 1, rowi == 0
        sc0[...] = jnp.where(last, 0.0, pltpu.roll(sa0[...], t - 1, 0))
        sc1[...] = jnp.where(first, 0.0, pltpu.roll(sa1[...], 1, 0))
        g0 = jnp.where(last, dfin_ref[0:1, :], 0.0)
        g1 = jnp.where(first, dfin_ref[1:2, :], 0.0)
        if need_y:
            g0 = g0 + dy_ref[...]
            g1 = g1 + dy_ref[...]
        sg0[...] = g0
        sg1[...] = g1
        zero = jnp.zeros((1, LRU_CB), F32)
        _scan_pair((sc1, sg1, sg1, zero), (sc0, sg0, sg0, zero), t // 8, LRU_CB)
        dh0_ref[0:1, :] = sa0[0:1, :] * sg0[0:1, :]
        dh0_ref[1:2, :] = sa1[t - 1:t, :] * sg1[t - 1:t, :]
        sc0[...] = sg0[...] * jnp.where(first, h0_ref[0:1, :], pltpu.roll(hf_ref[...], 1, 0))
        sc1[...] = sg1[...] * jnp.where(last, h0_ref[1:2, :], pltpu.roll(hb_ref[...], t - 1, 0))

        def rows(ci, carry):
            r0 = pl.multiple_of(ci * rc, rc)
            for j in range(2):
                sl = slice(128 * j, 128 * j + 128)
                du = jnp.zeros((rc, 128), F32)
                for d, (sc, sg) in enumerate(((sc0, sg0), (sc1, sg1))):
                    _, vjp = jax.vjp(_lru_gate, su[pl.ds(r0, rc), sl], wa_ref[d, j], ba_ref[d:d + 1, sl], wi_ref[d, j],
                                     bi_ref[d:d + 1, sl], lam_ref[d:d + 1, sl])
                    du_d, dwa, dba, dwi, dbi, dlam = vjp((sc[pl.ds(r0, rc), sl], sg[pl.ds(r0, rc), sl]))
                    du = du + du_d
                    dwa_ref[d, 2 * q + j] += dwa
                    dwi_ref[d, 2 * q + j] += dwi
                    dba_ref[q, d:d + 1, sl] += dba
                    dbi_ref[q, d:d + 1, sl] += dbi
                    dlam_ref[q, d:d + 1, sl] += dlam
                sa0[pl.ds(r0, rc), sl] = du
            return carry

        lax.fori_loop(0, t // rc, rows, 0)
        draw, dcw, dcb = _conv_bwd(sa0[...], p_ref[...], cw_ref[...], masks)
        dp_ref[...] = draw.astype(BF16)
        dcw_ref[q] += dcw
        dcb_ref[q] += dcb

    p_spec, w_spec, v_spec, c_spec, s_spec = _lru_specs(t, blk0)
    o_spec = BS((t, LRU_CB), lambda b, q: (b, q))
    full = lambda shp: BS(shp, lambda b, q: (0,) * len(shp))
    in_specs = [p_spec, c_spec(4), c_spec(1), w_spec, v_spec, w_spec, v_spec, v_spec, s_spec, o_spec, o_spec]
    args = [proj, cw, cb, wa, ba, wi, bi, lam, h0, hf, hb]
    if need_y:
        in_specs.append(o_spec)
        args.append(dy)
    in_specs += [s_spec, BS(memory_space=pl.ANY)]
    args += [dfin, dproj]
    out_specs = [p_spec, s_spec, full((nq, 4, LRU_CB)), full((nq, 1, LRU_CB)), full((2, 8, 128, 128)),
                 full((2, 8, 128, 128)), full((nq, 2, LRU_CB)), full((nq, 2, LRU_CB)), full((nq, 2, LRU_CB))]
    out_shape = [S(dproj.shape, BF16), S((nb, 2, D), F32), S((nq, 4, LRU_CB), F32), S((nq, 1, LRU_CB), F32),
                 S((2, 8, 128, 128), F32), S((2, 8, 128, 128), F32), S((nq, 2, LRU_CB), F32), S((nq, 2, LRU_CB), F32),
                 S((nq, 2, LRU_CB), F32)]
    extra = side.arrays if side else []
    return pl.pallas_call(
        body, grid=(nb, nq), name="lru_bwd_lat" if need_y else "lru_bwd_ctx",
        in_specs=in_specs + [_HBM] * len(extra), out_specs=out_specs + [_HBM] * len(extra),
        out_shape=out_shape + (side.out_shapes if side else []), input_output_aliases={len(args) - 1: 0},
        scratch_shapes=[pltpu.VMEM((t, LRU_CB), F32)] * 7 + (side.sems if side else []), compiler_params=_params(),
    )(*args, *extra)


def _mix_core(y_ref, yl_ref, p_ref, nw_ref, bg_ref, wbs_ref, wbl_ref, wo_ref, nrm_s):
    for g in range(SSD_G):
        sl = slice(256 * g, 256 * g + 256)
        nrm_s[:, sl] = _grms(y_ref[:, sl], p_ref[:, sl], nw_ref[:, sl]).astype(BF16)
    br_s = jnp.dot(nrm_s[...], wbs_ref[...], preferred_element_type=F32)
    gl = (yl_ref[...] * _gelu(p_ref[:, 2048:3072])).astype(BF16)
    br_l = jnp.dot(gl, wbl_ref[...], preferred_element_type=F32)
    gs = _sigmoid(p_ref[:, 3072:4096] + bg_ref[:, 0:D])
    gr = _sigmoid(p_ref[:, 4096:5120] + bg_ref[:, D:2 * D])
    mix = (gs * br_s + gr * br_l).astype(BF16)
    xmix = jnp.dot(mix, wo_ref[...], preferred_element_type=F32)
    return br_s, gl, br_l, gs, gr, mix, xmix


def _mix_specs(rt, tiles_per_b):
    row = lambda w: BS((rt, w), lambda i: (i, 0))
    const = lambda shp: BS(shp, lambda i: (0,) * len(shp))
    gate = BS((None, None, 1, D), lambda i: (i // tiles_per_b, 2, 0, 0))
    return row, const, gate


def _mix_fwd(y, ylru, proj, x, m4, wbs, wbl, wo, nw, bg, l1g, l1b, rt, tiles_per_b):
    n = x.shape[0]

    def body(y_ref, yl_ref, p_ref, x_ref, g1_ref, wbs_ref, wbl_ref, wo_ref, nw_ref, bg_ref, lg_ref, lb_ref,
             x1_ref, nrm_ref, gl_ref, mix_ref, brs_ref, brl_ref, xm_ref):
        br_s, gl, br_l, _, _, mix, xmix = _mix_core(y_ref, yl_ref, p_ref, nw_ref, bg_ref, wbs_ref, wbl_ref, wo_ref, nrm_ref)
        gl_ref[...] = gl
        mix_ref[...] = mix
        brs_ref[...] = br_s
        brl_ref[...] = br_l
        xm_ref[...] = xmix
        x1_ref[...] = _resln(x_ref[...], xmix, g1_ref[...], lg_ref[...], lb_ref[...])

    row, const, gate = _mix_specs(rt, tiles_per_b)
    return pl.pallas_call(
        body, grid=(n // rt,), name="mix_fwd",
        in_specs=[row(SSD_INNER), row(D), BS((rt, 5120), lambda i: (i, 1)), row(D), gate,
                  const((SSD_INNER, D)), const((D, D)), const((D, D)), const((1, SSD_INNER)), const((1, 2 * D)),
                  const((1, D)), const((1, D))],
        out_specs=[row(D), row(SSD_INNER), row(D), row(D), row(D), row(D), row(D)],
        out_shape=[S((n, D), F32), S((n, SSD_INNER), BF16), S((n, D), BF16), S((n, D), BF16), S((n, D), F32),
                   S((n, D), F32), S((n, D), F32)],
        compiler_params=_params(),
    )(y, ylru, proj, x, m4, wbs, wbl, wo, nw, bg, l1g, l1b)


def _mix_bwd(y, ylru, proj, x, m4, wbs, wbl, wo, nw, bg, l1g, l1b, brs, brl, xmix, dx1, dproj, rt, tiles_per_b):
    n = x.shape[0]

    def body(y_ref, yl_ref, p_ref, x_ref, g1_ref, wbs_ref, wbl_ref, wo_ref, nw_ref, bg_ref, lg_ref, lb_ref,
             brs_ref, brl_ref, xm_ref, dx1_ref, _,
             dp_ref, dy_ref, dyl_ref, dxr_ref, dbrs_ref, dbrl_ref, dxm_ref,
             dg1_ref, dnw_ref, dbg_ref, dlg_ref, dlb_ref):
        i = pl.program_id(0)

        @pl.when(i == 0)
        def _():
            for r in (dnw_ref, dbg_ref, dlg_ref, dlb_ref):
                r[...] = jnp.zeros_like(r)

        @pl.when(i % tiles_per_b == 0)
        def _():
            dg1_ref[...] = jnp.zeros_like(dg1_ref)

        br_s, br_l = brs_ref[...], brl_ref[...]
        gs = _sigmoid(p_ref[:, 3072:4096] + bg_ref[:, 0:D])
        gr = _sigmoid(p_ref[:, 4096:5120] + bg_ref[:, D:2 * D])
        _, vjp = jax.vjp(_resln, x_ref[...], xm_ref[...], g1_ref[...], lg_ref[...], lb_ref[...])
        dxr, dxmix, dg1, dlg, dlb = vjp(dx1_ref[...])
        dxr_ref[...] = dxr
        dg1_ref[...] += dg1
        dlg_ref[...] += dlg
        dlb_ref[...] += dlb
        dxmb = dxmix.astype(BF16)
        dxm_ref[...] = dxmb
        dmix = lax.dot_general(dxmb, wo_ref[...], (((1,), (1,)), ((), ())), preferred_element_type=F32)
        dbrs = (dmix * gs).astype(BF16)
        dbrl = (dmix * gr).astype(BF16)
        dbrs_ref[...] = dbrs
        dbrl_ref[...] = dbrl
        dmg_s = dmix * br_s * gs * (1.0 - gs)
        dmg_r = dmix * br_l * gr * (1.0 - gr)
        dp_ref[:, 3072:4096] = dmg_s.astype(BF16)
        dp_ref[:, 4096:5120] = dmg_r.astype(BF16)
        dbg_ref[:, 0:D] += _colsum(dmg_s)
        dbg_ref[:, D:2 * D] += _colsum(dmg_r)
        dnrm = lax.dot_general(dbrs, wbs_ref[...], (((1,), (1,)), ((), ())), preferred_element_type=F32)
        for g in range(SSD_G):
            sl = slice(256 * g, 256 * g + 256)
            _, vjp = jax.vjp(_grms, y_ref[:, sl], p_ref[:, sl], nw_ref[:, sl])
            dyg, dzg, dnwg = vjp(dnrm[:, sl])
            dy_ref[:, sl] = dyg
            dp_ref[:, sl] = dzg.astype(BF16)
            dnw_ref[:, sl] += dnwg
        dgl = lax.dot_general(dbrl, wbl_ref[...], (((1,), (1,)), ((), ())), preferred_element_type=F32)
        _, vjp = jax.vjp(lambda a, c: a * _gelu(c), yl_ref[...], p_ref[:, 2048:3072])
        dyl, dlgate = vjp(dgl)
        dyl_ref[...] = dyl
        dp_ref[:, 2048:3072] = dlgate.astype(BF16)

    row, const, gate = _mix_specs(rt, tiles_per_b)
    pblk = BS((rt, 5120), lambda i: (i, 1))
    nb = n // (rt * tiles_per_b)
    out_specs = [pblk, row(SSD_INNER), row(D), row(D), row(D), row(D), row(D),
                 BS((None, 1, D), lambda i: (i // tiles_per_b, 0, 0)), const((1, SSD_INNER)), const((1, 2 * D)),
                 const((1, D)), const((1, D))]
    out_shape = [S(dproj.shape, BF16), S((n, SSD_INNER), F32), S((n, D), F32), S((n, D), F32),
                 S((n, D), BF16), S((n, D), BF16), S((n, D), BF16),
                 S((nb, 1, D), F32), S((1, SSD_INNER), F32), S((1, 2 * D), F32), S((1, D), F32), S((1, D), F32)]
    return pl.pallas_call(
        body, grid=(n // rt,), name="mix_bwd",
        in_specs=[row(SSD_INNER), row(D), pblk, row(D), gate,
                  const((SSD_INNER, D)), const((D, D)), const((D, D)), const((1, SSD_INNER)), const((1, 2 * D)),
                  const((1, D)), const((1, D)), row(D), row(D), row(D), row(D), BS(memory_space=pl.ANY)],
        out_specs=out_specs, out_shape=out_shape, input_output_aliases={16: 0},
        compiler_params=_params(),
    )(y, ylru, proj, x, m4, wbs, wbl, wo, nw, bg, l1g, l1b, brs, brl, xmix, dx1, dproj)


def _mlp_step(x1, tgt, m4, w1, b1, w2, b2, l2g, l2b, rt, tiles_per_b):
    n = x1.shape[0]

    def body(x_ref, t_ref, sh_ref, sc_ref, gt_ref, w1_hbm, b1_ref, w2_hbm, b2_ref, lg_ref, lb_ref,
             loss_ref, dx_ref, h2_ref, da1_ref, r2_ref, dmlp_ref, dm_ref, db1_ref, db2_ref, dlg_ref, dlb_ref,
             w1_vm, w2_vm, sem):
        i = pl.program_id(0)

        @pl.when(i == 0)
        def _():
            c1 = pltpu.make_async_copy(w1_hbm, w1_vm, sem.at[0])
            c2 = pltpu.make_async_copy(w2_hbm, w2_vm, sem.at[1])
            c1.start()
            c2.start()
            for r in (loss_ref, db1_ref, db2_ref, dlg_ref, dlb_ref):
                r[...] = jnp.zeros_like(r)
            c1.wait()
            c2.wait()

        @pl.when(i % tiles_per_b == 0)
        def _():
            dm_ref[...] = jnp.zeros_like(dm_ref)

        x1v = x_ref[...]
        h2, vjp_h = jax.vjp(_modln, x1v, sh_ref[...], sc_ref[...])
        h2b = h2.astype(BF16)
        h2_ref[...] = h2b
        r = jnp.maximum(jnp.dot(h2b, w1_vm[...], preferred_element_type=F32) + b1_ref[...], 0.0)
        r2b = (r * r).astype(BF16)
        r2_ref[...] = r2b
        mlp = jnp.dot(r2b, w2_vm[...], preferred_element_type=F32) + b2_ref[...]
        x2, vjp_r = jax.vjp(_resln, x1v, mlp, gt_ref[...], lg_ref[...], lb_ref[...])
        diff = x2 - t_ref[...]
        loss_ref[...] += (0.5 / D) * jnp.sum(diff * diff)
        dxa, dmlp, dgt, dlg, dlb = vjp_r(diff * (1.0 / D))
        dlg_ref[...] += dlg
        dlb_ref[...] += dlb
        dm_ref[2:3, :] += dgt
        db2_ref[...] += _colsum(dmlp)
        dmlpb = dmlp.astype(BF16)
        dmlp_ref[...] = dmlpb
        da1 = lax.dot_general(dmlpb, w2_vm[...], (((1,), (1,)), ((), ())), preferred_element_type=F32) * (2.0 * r)
        db1_ref[...] += _colsum(da1)
        da1b = da1.astype(BF16)
        da1_ref[...] = da1b
        dh2 = lax.dot_general(da1b, w1_vm[...], (((1,), (1,)), ((), ())), preferred_element_type=F32)
        dxb, dsh, dsc = vjp_h(dh2)
        dx_ref[...] = dxa + dxb
        dm_ref[0:1, :] += dsh
        dm_ref[1:2, :] += dsc

    row = lambda w: BS((rt, w), lambda i: (i, 0))
    const = lambda shp: BS(shp, lambda i: (0,) * len(shp))
    mod = lambda k: BS((None, None, 1, D), lambda i: (i // tiles_per_b, k, 0, 0))
    nb = n // (rt * tiles_per_b)
    anyspec = BS(memory_space=pl.ANY)
    return pl.pallas_call(
        body, grid=(n // rt,), name="mlp_step",
        in_specs=[row(D), row(D), mod(3), mod(4), mod(5), anyspec, const((1, MLP_H)), anyspec, const((1, D)),
                  const((1, D)), const((1, D))],
        out_specs=[const((8, 128)), row(D), row(D), row(MLP_H), row(MLP_H), row(D),
                   BS((None, 3, D), lambda i: (i // tiles_per_b, 0, 0)), const((1, MLP_H)), const((1, D)),
                   const((1, D)), const((1, D))],
        out_shape=[S((8, 128), F32), S((n, D), F32), S((n, D), BF16), S((n, MLP_H), BF16), S((n, MLP_H), BF16),
                   S((n, D), BF16), S((nb, 3, D), F32), S((1, MLP_H), F32), S((1, D), F32), S((1, D), F32),
                   S((1, D), F32)],
        scratch_shapes=[pltpu.VMEM((D, MLP_H), BF16), pltpu.VMEM((MLP_H, D), BF16), pltpu.SemaphoreType.DMA((2,))],
        compiler_params=_params(),
    )(x1, tgt, m4, m4, m4, w1, b1, w2, b2, l2g, l2b)


def _pack_win(w):
    parts = []
    for g in range(SSD_G):
        parts += [w[:, 256 * g:256 * g + 256], w[:, 2048 + 128 * g:2176 + 128 * g], w[:, 4160 + 128 * g:4288 + 128 * g]]
    parts += [w[:, 3136:4160], w[:, 5184:7232], w[:, 7232:8256], w[:, 8256:10304], w[:, 3072:3136],
              jnp.zeros((w.shape[0], P_W - P_DT - 64), w.dtype)]
    return jnp.concatenate(parts, axis=1)


def _unpack_win(p):
    xs = [p[:, 512 * g:512 * g + 256] for g in range(SSD_G)]
    bs = [p[:, 512 * g + 256:512 * g + 384] for g in range(SSD_G)]
    cs = [p[:, 512 * g + 384:512 * g + 512] for g in range(SSD_G)]
    return jnp.concatenate(xs + bs + [p[:, P_DT:P_DT + 64], p[:, P_LRU:P_Z]] + cs + [p[:, P_Z:P_DT]], axis=1)


def _pack_conv(w):
    return jnp.stack([jnp.concatenate([w[:, 256 * g:256 * g + 256], w[:, 2048 + 128 * g:2176 + 128 * g],
                                       w[:, 3072 + 128 * g:3200 + 128 * g]], axis=1) for g in range(SSD_G)])


def _unpack_conv(p):
    r = p.shape[1]
    x = jnp.transpose(p[:, :, 0:256], (1, 0, 2)).reshape(r, 2048)
    b = jnp.transpose(p[:, :, 256:384], (1, 0, 2)).reshape(r, 1024)
    c = jnp.transpose(p[:, :, 384:512], (1, 0, 2)).reshape(r, 1024)
    return jnp.concatenate([x, b, c], axis=1)


def _pack_heads(v):
    p = jnp.transpose(v.reshape(2, SSD_G, 4), (1, 0, 2)).reshape(SSD_G, 1, 8)
    return jnp.pad(p, ((0, 0), (0, 0), (0, 120)))


def _unpack_heads(p):
    return jnp.transpose(p[:, 0, 0:8].reshape(SSD_G, 2, 4), (1, 0, 2)).reshape(2, 32)


def _pack_dt(dt):
    n = dt.shape[0]
    p = jnp.transpose(dt.reshape(n, 2, SSD_G, 4), (2, 0, 1, 3)).reshape(SSD_G, n, 8)
    return jnp.pad(p, ((0, 0), (0, 0), (0, 120)))


def _unpack_dt(p):
    n = p.shape[1]
    return jnp.transpose(p[:, :, 0:8].reshape(SSD_G, n, 2, 4), (1, 2, 0, 3)).reshape(n, 64)


def _tk(rows):
    return next(tk for tk in (1024, 512, 256, 128) if rows % tk == 0)


LATE = ["w_br_ssd", "w_br_lru", "w_out", "w_mlp1", "w_mlp2"]


def _local_step(x, c, ctx, tgt, sm, wmod, win, late, late_are_shards=False, reducer=None):
    nb, t, _ = x.shape
    tc = ctx.shape[1]
    nl, ncx = nb * t, nb * tc
    rt = 256 if tc % 256 == 0 else 128
    rtm = 128
    xl, xc = x.reshape(nl, D), ctx.reshape(ncx, D)
    tgt2 = tgt.reshape(nl, D)
    cc = jnp.zeros((8, D), F32).at[0:nb].set(c).at[nb].set(sm["c_ctx"])
    m = _mod_fwd(cc, wmod, sm["b_mod"])
    m4 = m.reshape(8, N_MOD, 1, D)
    proj, h1 = _inproj_fwd(xl, m4, win, rtm, nl // rtm, t // rtm, nb, "inproj_fwd_lat")
    proj_c, h1_c = _inproj_fwd(xc, m4, win, rtm, 0, 1, nb, "inproj_fwd_ctx")

    cw_s, cb_s = _pack_conv(sm["ssd_conv_w"]), _pack_conv(sm["ssd_conv_b"])
    dtb, alog = _pack_heads(sm["ssd_dt_bias"]), _pack_heads(sm["ssd_a_log"])
    drow = jnp.repeat(sm["ssd_d"].reshape(32), 64).reshape(SSD_G, 1, 256)
    dtg, dtg_c = _pack_dt(proj[:, P_DT:P_DT + 64]), _pack_dt(proj_c[:, P_DT:P_DT + 64])
    zst = jnp.zeros((nb, SSD_G, SSD_N, 256), F32)
    zl = jnp.zeros((nb, 2, D), F32)
    ssd_p = (cw_s, cb_s, dtb, alog, drow)
    lru_p = (sm["lru_conv_w"], sm["lru_conv_b"], sm["lru_wa"], sm["lru_ba"], sm["lru_wi"], sm["lru_bi"], sm["lru_lambda"])

    chsf, chsb, csf, csb = _ssd_fwd(proj_c, dtg_c, *ssd_p, zst, zst, nb, tc, tc, 0, False)
    y, lhsf, lhsb, _, _, *got = _ssd_fwd(proj, dtg, *ssd_p, csf, csb, nb, t, GRID_W, 0, True,
                                         tuple(late) if late_are_shards else ())
    wbs, wbl, wo, w1, w2 = [_full_from_chips(g, n) for g, n in zip(got, LATE)] if late_are_shards else late
    chf, chb, cfin = _lru_fwd(proj_c, *lru_p, zl, nb, tc, tc, 0, False)
    ylru, lhf, lhb, _ = _lru_fwd(proj, *lru_p, cfin, nb, t, GRID_W, 0, True)
    mix_w = (wbs, wbl, wo, sm["ssd_norm_w"], sm["b_gate"], sm["ln1_g"], sm["ln1_b"])
    x1, nrm, gl, mixb, brs, brl, xmix = _mix_fwd(y, ylru, proj, xl, m4, *mix_w, rtm, t // rtm)
    (loss, dx1, h2, da1, r2, dmlp, dm2, db1, db2, dl2g, dl2b) = _mlp_step(
        x1, tgt2, m4, w1, sm["b_mlp1"], w2, sm["b_mlp2"], sm["ln2_g"], sm["ln2_b"], rt, t // rt)

    dproj = lax.empty((nl, P_W), BF16)
    dproj_c = jnp.zeros((ncx, P_W), BF16)
    (dproj, dy, dylru, dxres, dbrs, dbrl, dxm, dg1, dnw, dbg, dl1g, dl1b) = _mix_bwd(
        y, ylru, proj, xl, m4, *mix_w, brs, brl, xmix, dx1, dproj, rtm, t // rtm)
    big = {
        "w_br_ssd": _matmul_tn(nrm, dbrs, D, D, _tk(nl), "dw_br_ssd"),
        "w_br_lru": _matmul_tn(gl, dbrl, D, D, _tk(nl), "dw_br_lru"),
        "w_out": _matmul_tn(mixb, dxm, D, D, _tk(nl), "dw_out"),
        "w_mlp1": _matmul_tn(h2, da1, D, D, _tk(nl), "dw_mlp1"),
        "w_mlp2": _matmul_tn(r2, dmlp, D, D, _tk(nl), "dw_mlp2"),
    }
    side = reducer.begin_swap(big, list(big)) if reducer else None
    (dproj, ddt_l, dh0f, dh0b, dcw_l, dcb_l, ddtb_l, dal_l, dd, *got) = _ssd_bwd(
        proj, dtg, *ssd_p, lhsf, lhsb, dy, zst, zst, dproj, nb, t, GRID_W, 0, True, side)
    (dproj_c, ddt_c, _, _, dcw_c, dcb_c, ddtb_c, dal_c, _) = _ssd_bwd(
        proj_c, dtg_c, *ssd_p, chsf, chsb, None, dh0f, dh0b, dproj_c, nb, tc, tc, 0, False)
    side = reducer.begin_scatter(got) if reducer else None
    (dproj, dlh0, gcw_l, gcb_l, gwa_l, gwi_l, gba_l, gbi_l, glam_l, *got) = _lru_bwd(
        proj, *lru_p, cfin, lhf, lhb, dylru, zl, dproj, nb, t, GRID_W, 0, True, side)
    if reducer:
        reducer.end(got)
    (dproj_c, _, gcw_c, gcb_c, gwa_c, gwi_c, gba_c, gbi_c, glam_c) = _lru_bwd(
        proj_c, *lru_p, zl, chf, chb, None, dlh0, dproj_c, nb, tc, tc, 0, False)
    pad_dt = lambda d: jnp.pad(_unpack_dt(d).astype(BF16), ((0, 0), (0, P_W - P_DT - 64)))
    dproj = lax.dynamic_update_slice(dproj, pad_dt(ddt_l), (0, P_DT))
    dproj_c = lax.dynamic_update_slice(dproj_c, pad_dt(ddt_c), (0, P_DT))

    big["w_in"] = _matmul_tn2(h1, dproj, h1_c, dproj_c, D, 1152, min(_tk(nl), _tk(ncx)), "dw_in")
    side = reducer.begin_swap(big, ["w_in"]) if reducer else None
    dmc, *got = _inproj_bwd(xc, m4, win, dproj_c, rt, 0, ncx // rt, ncx // rt, nb, False, None, side)
    side = reducer.begin_scatter(got) if reducer else None
    gx, dm1, *got = _inproj_bwd(xl, m4, win, dproj, rt, 0, nl // rt, t // rt, nb, True, dxres, side)
    if reducer:
        reducer.end(got)
    dm = jnp.zeros((8, N_MOD, D), F32)
    dm = dm.at[0:nb].set(jnp.concatenate([dm1, dg1, dm2], axis=1)).at[nb, 0:2].set(dmc[0])
    dwmod, dbmod, dcc = _mod_bwd(cc, wmod, dm.reshape(8, N_MOD * D))
    big["w_mod"] = dwmod
    nq = D // LRU_CB
    small = {
        "c_ctx": dcc[nb],
        "b_mod": dbmod,
        "b_gate": dbg,
        "ssd_conv_w": _unpack_conv(dcw_l + dcw_c),
        "ssd_conv_b": _unpack_conv(dcb_l + dcb_c),
        "ssd_dt_bias": _unpack_heads(ddtb_l + ddtb_c),
        "ssd_a_log": _unpack_heads(dal_l + dal_c),
        "ssd_d": jnp.sum(dd.reshape(32, 64), axis=1),
        "ssd_norm_w": dnw,
        "lru_conv_w": jnp.transpose(gcw_l + gcw_c, (1, 0, 2)).reshape(4, D),
        "lru_conv_b": (gcb_l + gcb_c).reshape(1, D),
        "lru_wa": gwa_l + gwa_c,
        "lru_ba": jnp.transpose(gba_l + gba_c, (1, 0, 2)).reshape(2, D),
        "lru_wi": gwi_l + gwi_c,
        "lru_bi": jnp.transpose(gbi_l + gbi_c, (1, 0, 2)).reshape(2, D),
        "lru_lambda": jnp.transpose(glam_l + glam_c, (1, 0, 2)).reshape(2, D),
        "ln1_g": dl1g, "ln1_b": dl1b, "b_mlp1": db1, "b_mlp2": db2, "ln2_g": dl2g, "ln2_b": dl2b,
    }
    return loss[0, 0], gx.reshape(nb, t, D), big, small


_HBM = BS(memory_space=pl.ANY)


def _place():
    return lax.axis_index("x"), lax.axis_index("y"), lax.axis_index("c")


def _other_chips(x, y):
    return [(1 - x, y), (x, 1 - y), (1 - x, 1 - y)]


def _gather_chips(arrs):
    n = len(arrs)

    def body(*refs):
        ex = _GatherExchange(refs[:n], refs[n:2 * n], refs[2 * n:])
        ex.begin()
        ex.finish()

    return pl.pallas_call(
        body, name="gather_weights", in_specs=[_HBM] * n, out_specs=[_HBM] * n,
        out_shape=_gather_out_shapes(arrs), scratch_shapes=_gather_sems(n),
    )(*arrs)


def _gather_out_shapes(arrs):
    return [S((4,) + a.shape, a.dtype) for a in arrs]


def _gather_sems(n):
    return [pltpu.SemaphoreType.DMA((3 * n,))] * 4 + [pltpu.SemaphoreType.DMA((n,))]


class _GatherExchange:
    def __init__(self, ins, outs, sems):
        self.ins, self.outs = ins, outs
        self.ici_send, self.ici_recv, self.d2d_send, self.d2d_recv, self.loc_sems = sems
        self.x, self.y, self.c = _place()
        self.me = 2 * self.x + self.y
        self.chips = _other_chips(self.x, self.y)

    def _half(self, a, which):
        hr = self.ins[a].shape[0] // 2
        return pl.ds(pl.multiple_of((self.c if which == 0 else 1 - self.c) * hr, 8), hr)

    def _local(self, a):
        return pltpu.make_async_copy(self.ins[a], self.outs[a].at[self.me], self.loc_sems.at[a])

    def _ici(self, a, k, slot):
        px, py = self.chips[k]
        mine = self._half(a, 0)
        return pltpu.make_async_remote_copy(src_ref=self.ins[a].at[mine], dst_ref=self.outs[a].at[slot, mine],
                                            send_sem=self.ici_send.at[3 * a + k], recv_sem=self.ici_recv.at[3 * a + k],
                                            device_id=(px, py, self.c), device_id_type=MESH)

    def _d2d(self, a, k, which):
        px, py = self.chips[k]
        rows = self.outs[a].at[2 * px + py, self._half(a, which)]
        return pltpu.make_async_remote_copy(src_ref=rows, dst_ref=rows, send_sem=self.d2d_send.at[3 * a + k],
                                            recv_sem=self.d2d_recv.at[3 * a + k],
                                            device_id=(self.x, self.y, 1 - self.c), device_id_type=MESH)

    def begin(self):
        for a in range(len(self.ins)):
            self._local(a).start()
            for k in range(3):
                self._ici(a, k, self.me).start()

    def finish(self):
        n = len(self.ins)
        for a in range(n):
            for k, (px, py) in enumerate(self.chips):
                self._ici(a, k, 2 * px + py).wait_recv()
                self._d2d(a, k, 0).start()
        for a in range(n):
            for k in range(3):
                self._d2d(a, k, 1).wait_recv()
        for a in range(n):
            self._local(a).wait()
            for k in range(3):
                self._ici(a, k, self.me).wait_send()
                self._d2d(a, k, 0).wait_send()


def _scatter_chips(arrs):
    side = _Side("scatter", arrs)
    n = len(arrs)

    def body(*refs):
        ex = side.make(refs[:n], refs[n:2 * n], refs[2 * n:])
        ex.begin()
        ex.finish()

    return pl.pallas_call(
        body, name="scatter_grads", in_specs=[_HBM] * n, out_specs=[_HBM] * n,
        out_shape=side.out_shapes, scratch_shapes=side.sems,
    )(*arrs)


class _ScatterExchange:
    def __init__(self, ins, outs, sems):
        self.ins, self.outs = ins, outs
        self.send_sems, self.recv_sems = sems
        x, y, self.c = _place()
        self.chips = _other_chips(x, y)

    def _copy(self, a, k):
        px, py = self.chips[k]
        return pltpu.make_async_remote_copy(src_ref=self.ins[a].at[2 * px + py], dst_ref=self.outs[a].at[k],
                                            send_sem=self.send_sems.at[3 * a + k], recv_sem=self.recv_sems.at[3 * a + k],
                                            device_id=(px, py, self.c), device_id_type=MESH)

    def begin(self):
        for a in range(len(self.ins)):
            for k in range(3):
                self._copy(a, k).start()

    def finish(self):
        for a in range(len(self.ins)):
            for k in range(3):
                self._copy(a, k).wait_recv()
        for a in range(len(self.ins)):
            for k in range(3):
                self._copy(a, k).wait_send()


class _Side:
    def __init__(self, kind, arrays):
        self.arrays = list(arrays)
        n = len(self.arrays)
        if kind == "gather":
            self.out_shapes, self.sems, self.make = _gather_out_shapes(self.arrays), _gather_sems(n), _GatherExchange
        elif kind == "swap":
            self.out_shapes = [S((4, a.shape[1] // 2, a.shape[2]), a.dtype) for a in self.arrays]
            self.sems = [pltpu.SemaphoreType.DMA((4 * n,))] * 2
            self.make = _SwapExchange
        else:
            self.out_shapes = [S((3,) + a.shape[1:], a.dtype) for a in self.arrays]
            self.sems = [pltpu.SemaphoreType.DMA((3 * n,))] * 2
            self.make = _ScatterExchange

    def split(self, refs, n_in, n_out, n_scr):
        a, b = len(self.arrays), len(self.out_shapes)
        i1 = n_in + a
        o1 = i1 + n_out
        o2 = o1 + b
        s1 = o2 + n_scr
        own = tuple(refs[:n_in]) + tuple(refs[i1:o1]) + tuple(refs[o2:s1])
        return own, self.make(refs[n_in:i1], refs[o1:o2], refs[s1:])

    def around(self, exchange, step, n_steps, compute):
        pl.when(step == 0)(exchange.begin)
        compute()
        pl.when(step == n_steps - 1)(exchange.finish)


def _swap_halves(arrs, name):
    side = _Side("swap", arrs)
    n = len(arrs)

    def body(*refs):
        ex = side.make(refs[:n], refs[n:2 * n], refs[2 * n:])
        ex.begin()
        ex.finish()

    return pl.pallas_call(
        body, name=name, in_specs=[_HBM] * n, out_specs=[_HBM] * n, out_shape=side.out_shapes, scratch_shapes=side.sems,
    )(*arrs)


class _SwapExchange:
    def __init__(self, ins, outs, sems):
        self.ins, self.outs = ins, outs
        self.send_sems, self.recv_sems = sems
        self.x, self.y, self.c = _place()

    def _copy(self, a, q):
        hr = self.ins[a].shape[1] // 2
        theirs = pl.ds(pl.multiple_of((1 - self.c) * hr, 8), hr)
        return pltpu.make_async_remote_copy(src_ref=self.ins[a].at[q, theirs], dst_ref=self.outs[a].at[q],
                                            send_sem=self.send_sems.at[4 * a + q], recv_sem=self.recv_sems.at[4 * a + q],
                                            device_id=(self.x, self.y, 1 - self.c), device_id_type=MESH)

    def begin(self):
        for a in range(len(self.ins)):
            for q in range(4):
                self._copy(a, q).start()

    def finish(self):
        for wait in ("wait_recv", "wait_send"):
            for a in range(len(self.ins)):
                for q in range(4):
                    getattr(self._copy(a, q), wait)()


def _allreduce_small(vs, bf16_over_ici):
    n = len(vs)

    def body(*refs):
        ins, outs, bufs, sendb = refs[:n], refs[n:2 * n], refs[2 * n:5 * n], refs[5 * n:6 * n]
        send_sems, recv_sems = refs[6 * n:]
        x, y, c = _place()
        srcs = list(ins)
        for s, peer in enumerate(((x, y, 1 - c), (x, 1 - y, c), (1 - x, y, c))):
            copies = []
            for a in range(n):
                src = srcs[a]
                if s > 0 and bf16_over_ici[a]:
                    sendb[a][...] = src[...].astype(BF16)
                    src = sendb[a]
                copies.append(pltpu.make_async_remote_copy(
                    src_ref=src, dst_ref=bufs[3 * a + s], send_sem=send_sems.at[3 * a + s],
                    recv_sem=recv_sems.at[3 * a + s], device_id=peer, device_id_type=MESH))
                copies[-1].start()
            for a, cp in enumerate(copies):
                cp.wait()
                mine = sendb[a] if s > 0 and bf16_over_ici[a] else srcs[a]
                outs[a][...] = mine[...].astype(F32) + bufs[3 * a + s][...].astype(F32)
            srcs = list(outs)

    vm = BS(memory_space=pltpu.VMEM)
    wire = lambda a, s: BF16 if s > 0 and bf16_over_ici[a] else F32
    return pl.pallas_call(
        body, name="allreduce_small", in_specs=[vm] * n, out_specs=[vm] * n, out_shape=[S(v.shape, F32) for v in vs],
        scratch_shapes=[pltpu.VMEM(v.shape, wire(a, s)) for a, v in enumerate(vs) for s in range(3)]
        + [pltpu.VMEM(v.shape if bf16_over_ici[a] else (16, 128), BF16) for a, v in enumerate(vs)]
        + [pltpu.SemaphoreType.DMA((3 * n,)), pltpu.SemaphoreType.DMA((3 * n,))],
        compiler_params=_params(),
    )(*vs)


def _swap_cores(arrs):
    n = len(arrs)

    def body(*refs):
        ins, outs = refs[:n], refs[n:2 * n]
        send_sems, recv_sems = refs[2 * n:]
        x, y, c = _place()
        sends = []
        for a in range(n):
            cp = pltpu.make_async_remote_copy(src_ref=ins[a], dst_ref=outs[a], send_sem=send_sems.at[a],
                                              recv_sem=recv_sems.at[a], device_id=(x, y, 1 - c), device_id_type=MESH)
            cp.start()
            sends.append(cp)
        for cp in sends:
            cp.wait_recv()
        for cp in sends:
            cp.wait_send()

    return pl.pallas_call(
        body, name="swap_cores", in_specs=[_HBM] * n, out_specs=[_HBM] * n,
        out_shape=[S(a.shape, a.dtype) for a in arrs],
        scratch_shapes=[pltpu.SemaphoreType.DMA((n,)), pltpu.SemaphoreType.DMA((n,))],
    )(*arrs)


def _row_tile(r, c=128):
    tr = 256 if c <= 1024 else (128 if c <= 2048 else 64)
    return tr if r % tr == 0 else r


def _sum_half(own, sib, core, name):
    _, r, c = own.shape
    hr = r // 2
    tr = _row_tile(hr, c)
    nbk = hr // tr

    def body(core_ref, o_ref, s_ref, p_ref, pb_ref):
        p = o_ref[...] + s_ref[...]
        p_ref[...] = p
        pb_ref[...] = p.astype(BF16)

    blk = BS((None, tr, c), lambda q, i, cr: (q, i, 0))
    return pl.pallas_call(
        body, name=name, out_shape=[S((4, hr, c), F32), S((4, hr, c), BF16)],
        grid_spec=pltpu.PrefetchScalarGridSpec(
            num_scalar_prefetch=1, grid=(4, nbk),
            in_specs=[BS((None, tr, c), lambda q, i, cr: (q, cr[0] * nbk + i, 0)), blk], out_specs=[blk, blk]),
        compiler_params=_params(),
    )(core, own, sib)


def _sum4(part, recv, chip, name):
    _, r, c = part.shape
    tr = _row_tile(r, c)

    def body(chip_ref, o_ref, r_ref, out_ref):
        acc = o_ref[...]
        for k in range(3):
            acc = acc + r_ref[k].astype(F32)
        out_ref[...] = acc

    return pl.pallas_call(
        body, name=name, out_shape=S((r, c), F32),
        grid_spec=pltpu.PrefetchScalarGridSpec(
            num_scalar_prefetch=1, grid=(r // tr,),
            in_specs=[BS((None, tr, c), lambda i, ch: (ch[0], i, 0)), BS((3, tr, c), lambda i, ch: (0, i, 0))],
            out_specs=BS((tr, c), lambda i, ch: (i, 0))),
        compiler_params=_params(),
    )(chip, part, recv)


def _adam_math(w, g, m, v):
    m = ADAM_B1 * m + (1.0 - ADAM_B1) * g
    v = ADAM_B2 * v + (1.0 - ADAM_B2) * (g * g)
    m_hat = m / (1.0 - ADAM_B1 ** ADAM_STEP)
    v_hat = v / (1.0 - ADAM_B2 ** ADAM_STEP)
    return -ADAM_LR * (m_hat / (jnp.sqrt(v_hat) + ADAM_EPS) + ADAM_WD * w), m, v


def _adam_halves(mine, other, w, m, v, core, name):
    r, c = w.shape
    tr = _row_tile(r // 2, c)
    nbk = (r // 2) // tr

    def body(core_ref, a_ref, b_ref, w_ref, m_ref, v_ref, g_ref, d_ref, nm_ref, nv_ref):
        g = jnp.where(pl.program_id(0) // nbk == core_ref[0], a_ref[...], b_ref[...])
        g_ref[...] = g
        d_ref[...], nm_ref[...], nv_ref[...] = _adam_math(w_ref[...], g, m_ref[...], v_ref[...])

    spec = BS((tr, c), lambda i, cr: (i, 0))
    half = BS((tr, c), lambda i, cr: (i % nbk, 0))
    return pl.pallas_call(
        body, name=name, out_shape=[S((r, c), F32)] * 4,
        grid_spec=pltpu.PrefetchScalarGridSpec(num_scalar_prefetch=1, grid=(r // tr,), in_specs=[half, half] + [spec] * 3,
                                               out_specs=[spec] * 4),
        compiler_params=_params(),
    )(core, mine, other, w, m, v)


def _adam_flat(g, w, m, v, name):
    r = w.shape[0]
    tr = _row_tile(r)

    def body(g_ref, w_ref, m_ref, v_ref, d_ref, nm_ref, nv_ref):
        d_ref[...], nm_ref[...], nv_ref[...] = _adam_math(w_ref[...], g_ref[...], m_ref[...], v_ref[...])

    spec = BS((tr, 128), lambda i: (i, 0))
    return pl.pallas_call(
        body, grid=(r // tr,), name=name, in_specs=[spec] * 4, out_specs=[spec] * 3,
        out_shape=[S((r, 128), F32)] * 3, compiler_params=_params(),
    )(g, w, m, v)


def _flatten(arrs, rows_mult=256):
    flat = jnp.concatenate([a.reshape(-1) for a in arrs])
    n = flat.shape[0]
    rows = -(-n // 128)
    rows = -(-rows // rows_mult) * rows_mult
    return jnp.pad(flat, (0, rows * 128 - n)).reshape(rows, 128)


def _unflatten(flat, shapes):
    flat = flat.reshape(-1)
    out, o = [], 0
    for shp in shapes:
        n = int(np.prod(shp))
        out.append(flat[o:o + n].reshape(shp))
        o += n
    return out


BIG = ["w_mod", "w_in", "w_br_ssd", "w_br_lru", "w_out", "w_mlp1", "w_mlp2"]
COL_SHARDED = {"w_mod": N_MOD * D, "w_in": IN_COLS, "w_mlp1": MLP_H}
SMALL_SHARDED = ["ssd_conv_w", "lru_conv_w", "lru_ba", "lru_bi", "lru_lambda"]
WEIGHTS = ['c_ctx', 'w_mod', 'b_mod', 'w_in', 'b_gate', 'ssd_conv_w', 'ssd_conv_b', 'ssd_dt_bias', 'ssd_a_log', 'ssd_d',
           'ssd_norm_w', 'lru_conv_w', 'lru_conv_b', 'lru_wa', 'lru_ba', 'lru_wi', 'lru_bi', 'lru_lambda', 'w_br_ssd',
           'w_br_lru', 'w_out', 'ln1_g', 'ln1_b', 'w_mlp1', 'b_mlp1', 'w_mlp2', 'b_mlp2', 'ln2_g', 'ln2_b']
SMALL = [n for n in WEIGHTS if n not in BIG]
GATE_STACKS = ["lru_wa", "lru_wi"]


class _Reducer:
    def __init__(self, core_id, chip_id):
        self.core_id, self.chip_id = core_id, chip_id
        self.halves, self.recv, self.pending = {}, {}, []

    def _slabs(self, big, names):
        return [_chips_from_full(_unpack_win(big[n]) if n == "w_in" else big[n], n) for n in names]

    def _chip_sums(self, names, slabs, sibling):
        for n, s, o in zip(names, slabs, sibling):
            self.halves[n] = _sum_half(s, o, self.core_id, "half_" + n)
        return [self.halves[n][1] for n in names]

    def begin_swap(self, big, names):
        self.pending, self.slabs = list(names), self._slabs(big, names)
        return _Side("swap", self.slabs)

    def begin_scatter(self, sibling):
        return _Side("scatter", self._chip_sums(self.pending, self.slabs, sibling))

    def end(self, received):
        self.recv.update(zip(self.pending, received))

    def finish(self, big, names):
        slabs = self._slabs(big, names)
        sums = self._chip_sums(names, slabs, _swap_halves(slabs, "swap_halves_" + names[0]))
        self.recv.update(zip(names, _scatter_chips(sums)))
        mine = [_sum4(self.halves[n][0], self.recv[n], self.chip_id, "sum_" + n) for n in BIG]
        return dict(zip(BIG, mine)), dict(zip(BIG, _swap_cores(mine)))


def _full_from_chips(g4, name):
    if name in COL_SHARDED:
        return jnp.transpose(g4, (1, 0, 2)).reshape(g4.shape[1], 4 * g4.shape[2])
    return g4.reshape(4 * g4.shape[1], g4.shape[2])


def _chips_from_full(full, name):
    if name in COL_SHARDED:
        r, c = full.shape
        return jnp.transpose(full.reshape(r, 4, c // 4), (1, 0, 2))
    return full.reshape(4, full.shape[0] // 4, full.shape[1])


def kernel(x, c, ctx, c_ctx, w_mod, b_mod, w_in, b_gate, ssd_conv_w, ssd_conv_b, ssd_dt_bias, ssd_a_log, ssd_d, ssd_norm_w, lru_conv_w, lru_conv_b, lru_wa, lru_ba, lru_wi, lru_bi, lru_lambda, w_br_ssd, w_br_lru, w_out, ln1_g, ln1_b, w_mlp1, b_mlp1, w_mlp2, b_mlp2, ln2_g, ln2_b, loss_target, m_c_ctx, m_w_mod, m_b_mod, m_w_in, m_b_gate, m_ssd_conv_w, m_ssd_conv_b, m_ssd_dt_bias, m_ssd_a_log, m_ssd_d, m_ssd_norm_w, m_lru_conv_w, m_lru_conv_b, m_lru_wa, m_lru_ba, m_lru_wi, m_lru_bi, m_lru_lambda, m_w_br_ssd, m_w_br_lru, m_w_out, m_ln1_g, m_ln1_b, m_w_mlp1, m_b_mlp1, m_w_mlp2, m_b_mlp2, m_ln2_g, m_ln2_b, v_c_ctx, v_w_mod, v_b_mod, v_w_in, v_b_gate, v_ssd_conv_w, v_ssd_conv_b, v_ssd_dt_bias, v_ssd_a_log, v_ssd_d, v_ssd_norm_w, v_lru_conv_w, v_lru_conv_b, v_lru_wa, v_lru_ba, v_lru_wi, v_lru_bi, v_lru_lambda, v_w_br_ssd, v_w_br_lru, v_w_out, v_ln1_g, v_ln1_b, v_w_mlp1, v_b_mlp1, v_w_mlp2, v_b_mlp2, v_ln2_g, v_ln2_b):
    given = dict(locals())
    w = {n: given[n] for n in WEIGHTS}
    mom = {n: given["m_" + n] for n in WEIGHTS}
    var = {n: given["v_" + n] for n in WEIGHTS}
    chip = 2 * lax.axis_index("x") + lax.axis_index("y")

    shard2d = {n: w[n].reshape(w[n].shape[-2:]) for n in BIG}
    small_pack = _flatten([w[n] for n in SMALL_SHARDED], rows_mult=16)
    first = ["w_mod", "w_in"]
    gathered = _gather_chips([shard2d[n].astype(BF16) for n in first] + [small_pack])
    full = {n: _full_from_chips(g, n) for n, g in zip(first, gathered[:-1])}
    full["w_in"] = _pack_win(full["w_in"])
    per_chip = [_unflatten(gathered[-1][q], [w[n].shape for n in SMALL_SHARDED]) for q in range(4)]
    sm = {n: jnp.concatenate([per_chip[q][i] for q in range(4)], axis=-1) for i, n in enumerate(SMALL_SHARDED)}
    for n in SMALL:
        if n not in sm:
            sm[n] = w[n]
    sm = {n: (a.reshape(a.shape[1:]) if a.ndim >= 3 else a) for n, a in sm.items()}

    core_id = lax.axis_index("c").astype(jnp.int32).reshape(1)
    reducer = _Reducer(core_id, chip.astype(jnp.int32).reshape(1))
    loss, gx, gbig, gsmall = _local_step(x, c, ctx, loss_target, sm, full["w_mod"], full["w_in"],
                                         [shard2d[n].astype(BF16) for n in LATE], late_are_shards=True, reducer=reducer)
    mine, other = reducer.finish(gbig, ["w_mod"])
    out = {}
    for n in BIG:
        shp = shard2d[n].shape
        res = _adam_halves(mine[n], other[n], shard2d[n], mom[n].reshape(shp), var[n].reshape(shp), core_id, "adam_" + n)
        out[n] = [r.reshape(w[n].shape) for r in res]

    tiny = [n for n in SMALL if n not in GATE_STACKS]
    as_rows = lambda a: a.reshape(-1, 128)
    summed = _allreduce_small([_flatten([gsmall[n] for n in tiny])] + [as_rows(gsmall[n]) for n in GATE_STACKS],
                              [False] + [True] * len(GATE_STACKS))
    gs = {}
    for n, g in zip(tiny, _unflatten(summed[0], [gsmall[n].shape for n in tiny])):
        if n in SMALL_SHARDED:
            width = w[n].shape[-1]
            g = lax.dynamic_slice_in_dim(g, chip * width, width, axis=g.ndim - 1)
        gs[n] = g.reshape(w[n].shape)
    shapes = [w[n].shape for n in tiny]
    d_s, m_s, v_s = _adam_flat(_flatten([gs[n] for n in tiny]), _flatten([w[n] for n in tiny]),
                               _flatten([mom[n] for n in tiny]), _flatten([var[n] for n in tiny]), "adam_small")
    for n, d_, m_, v_ in zip(tiny, _unflatten(d_s, shapes), _unflatten(m_s, shapes), _unflatten(v_s, shapes)):
        out[n] = [gs[n], d_, m_, v_]
    for n, g in zip(GATE_STACKS, summed[1:]):
        res = _adam_flat(g, as_rows(w[n]), as_rows(mom[n]), as_rows(var[n]), "adam_" + n)
        out[n] = [r.reshape(w[n].shape) for r in (g, *res)]

    loss = lax.psum(loss, ("x", "y", "c"))
    return (loss, gx, *[out[n][0] for n in WEIGHTS], *[out[n][1] for n in WEIGHTS], *[out[n][2] for n in WEIGHTS],
            *[out[n][3] for n in WEIGHTS])
```

```python
import functools

import numpy as np
import jax
import jax.numpy as jnp
from jax import lax
from jax.experimental import pallas as pl
from jax.experimental.pallas import tpu as pltpu

F32, BF16 = jnp.float32, jnp.bfloat16
S = jax.ShapeDtypeStruct
BS = pl.BlockSpec
MESH = pl.DeviceIdType.MESH

D = 1024
GRID_W = 64
SSD_INNER, SSD_G, SSD_N, SSD_L = 2048, 8, 128, 128
SSD_GW = 512
MLP_H = 4096
N_MOD = 6
ALPHA = 2.0 ** 0.25
LN_EPS, RMS_EPS = 1e-6, 1e-5
LRU_C = 8.0
P_XBC, P_LRU, P_Z, P_LG, P_MG, P_DT, P_W = 0, 4096, 5120, 7168, 8192, 10240, 10368
P_CB = 3456
IN_COLS = 10304
LRU_CB = 256
ADAM_LR, ADAM_B1, ADAM_B2, ADAM_EPS, ADAM_WD, ADAM_STEP = 0.001, 0.9, 0.999, 1e-08, 0.01, 10
VMEM_LIMIT = 56 * 2 ** 20


def _params(**kw):
    return pltpu.CompilerParams(vmem_limit_bytes=VMEM_LIMIT, **kw)


def _dot(a, b):
    return jnp.dot(a.astype(BF16), b.astype(BF16), preferred_element_type=F32)


def _dot_nt(a, b):
    return lax.dot_general(a.astype(BF16), b.astype(BF16), (((1,), (1,)), ((), ())), preferred_element_type=F32)


def _dot_tn(a, b):
    return lax.dot_general(a.astype(BF16), b.astype(BF16), (((0,), (0,)), ((), ())), preferred_element_type=F32)


@jax.custom_vjp
def _mm(a, b):
    return _dot(a, b)


def _cast_pair(a, b):
    return a.astype(BF16), b.astype(BF16)


def _mm_f(a, b):
    r = _cast_pair(a, b)
    return _dot(*r), r


def _mm_b(r, g):
    g = g.astype(BF16)
    return _dot_nt(g, r[1]), _dot_tn(r[0], g)


_mm.defvjp(_mm_f, _mm_b)


@jax.custom_vjp
def _mm_nt(a, b):
    return _dot_nt(a, b)


def _mm_nt_f(a, b):
    r = _cast_pair(a, b)
    return _dot_nt(*r), r


def _mm_nt_b(r, g):
    g = g.astype(BF16)
    return _dot(g, r[1]), _dot_tn(g, r[0])


_mm_nt.defvjp(_mm_nt_f, _mm_nt_b)


@jax.custom_vjp
def _mm_tn(a, b):
    return _dot_tn(a, b)


def _mm_tn_f(a, b):
    r = _cast_pair(a, b)
    return _dot_tn(*r), r


def _mm_tn_b(r, g):
    g = g.astype(BF16)
    return _dot_nt(r[1], g), _dot(r[0], g)


_mm_tn.defvjp(_mm_tn_f, _mm_tn_b)

def _split3(v):
    h = v.astype(BF16)
    r = v - h.astype(F32)
    m = r.astype(BF16)
    return h, m, (r - m.astype(F32)).astype(BF16)


def _sel_dot(sel, v, dims):
    sel_first = dims[0] == "s"
    dn = {"sv": (((1,), (0,)), ((), ())), "sTv": (((0,), (0,)), ((), ())), "vs": (((1,), (0,)), ((), ())),
          "vsT": (((1,), (1,)), ((), ()))}[dims]
    out = None
    for part in _split3(v):
        a, b = (sel, part) if sel_first else (part, sel)
        term = lax.dot_general(a, b, dn, preferred_element_type=F32)
        out = term if out is None else out + term
    return out


@jax.custom_vjp
def _cum_mm(tri, v):
    return _sel_dot(tri, v, "sv")


_cum_mm.defvjp(lambda tri, v: (_sel_dot(tri, v, "sv"), tri),
               lambda tri, g: (jnp.zeros_like(tri), _sel_dot(tri, g, "sTv")))


@jax.custom_vjp
def _xp_mm(v, e):
    return _sel_dot(e, v, "vs")


_xp_mm.defvjp(lambda v, e: (_sel_dot(e, v, "vs"), e),
              lambda e, g: (_sel_dot(e, g, "vsT"), jnp.zeros_like(e)))


def _sigmoid(x):
    return 0.5 * jnp.tanh(0.5 * x) + 0.5


def _silu(x):
    return x * _sigmoid(x)


def _dsilu(x):
    s = _sigmoid(x)
    return s * (1.0 + x * (1.0 - s))


def _softplus(x):
    return jnp.maximum(x, 0.0) + jnp.log1p(jnp.exp(-jnp.abs(x)))


def _gelu(x):
    return 0.5 * x * (1.0 + jnp.tanh(0.7978845608028654 * (x + 0.044715 * x * x * x)))


def _ln(x):
    mu = jnp.mean(x, axis=-1, keepdims=True)
    xc = x - mu
    var = jnp.mean(xc * xc, axis=-1, keepdims=True)
    return xc * lax.rsqrt(var + LN_EPS)


def _modln(x, shift, scale):
    return _ln(x) * (1.0 + scale) + shift


def _resln(x, sub, gate, g, b):
    return _ln(ALPHA * x + gate * sub) * g + b


def _grms(y, z, w):
    u = y * _silu(z)
    return u * lax.rsqrt(jnp.mean(u * u, axis=-1, keepdims=True) + RMS_EPS) * w


def _colsum(v):
    return jnp.sum(v, axis=0, keepdims=True)


def _conv_taps(width, period):
    def masks(rows):
        pos = lax.broadcasted_iota(jnp.int32, (rows, width), 0)
        if rows != period:
            pos = pos & (period - 1)
        return [pos >= 2 - k if k < 2 else pos < period + 2 - k for k in range(4)]
    return masks


def _conv_fwd(raw, w, b, masks):
    rows = raw.shape[0]
    pre = b + raw * w[2:3, :]
    for k in (0, 1, 3):
        sh = pltpu.roll(raw, (2 - k) % rows, 0)
        pre = pre + jnp.where(masks[k], sh, 0.0) * w[k:k + 1, :]
    return pre


def _for_rows(t, rb, fn):
    n = t // rb
    unroll = 4 if n % 4 == 0 else 1

    def step(i, carry):
        for u in range(unroll):
            fn(pl.multiple_of((i * unroll + u) * rb, rb))
        return carry

    lax.fori_loop(0, n // unroll, step, 0)


def _loop_unrolled(n, unroll, body, init):
    def step(i, carry):
        for u in range(unroll):
            carry = body(i * unroll + u, carry)
        return carry

    return lax.fori_loop(0, n // unroll, step, init)


def _conv_bwd(dpre, raw, w, masks):
    rows = raw.shape[0]
    draw = dpre * w[2:3, :]
    dws = []
    for k in range(4):
        if k == 2:
            dws.append(_colsum(dpre * raw))
            continue
        back = pltpu.roll(jnp.where(masks[k], dpre, 0.0), (k - 2) % rows, 0)
        dws.append(_colsum(back * raw))
        draw = draw + back * w[k:k + 1, :]
    return draw, jnp.concatenate(dws, axis=0), _colsum(dpre)


def _mod_fwd(cc, wmod, bmod):
    def body(cc_ref, w_ref, b_ref, o_ref):
        o_ref[...] = _dot(_silu(cc_ref[...]), w_ref[...]) + b_ref[...]

    return pl.pallas_call(
        body, grid=(N_MOD,), name="mod_fwd",
        in_specs=[BS((8, D), lambda j: (0, 0)), BS((D, D), lambda j: (0, j)), BS((1, D), lambda j: (0, j))],
        out_specs=BS((8, D), lambda j: (0, j)), out_shape=S((8, N_MOD * D), F32), compiler_params=_params(),
    )(cc, wmod, bmod)


def _mod_bwd(cc, wmod, dm):
    def body(cc_ref, w_ref, dm_ref, dw_ref, db_ref, dcc_ref):
        j = pl.program_id(0)
        c = cc_ref[...]
        dmv = dm_ref[...]
        dw_ref[...] = _dot_tn(_silu(c), dmv)
        db_ref[...] = _colsum(dmv)

        @pl.when(j == 0)
        def _():
            dcc_ref[...] = jnp.zeros_like(dcc_ref)

        dcc_ref[...] += _dot_nt(dmv, w_ref[...]) * _dsilu(c)

    return pl.pallas_call(
        body, grid=(N_MOD,), name="mod_bwd",
        in_specs=[BS((8, D), lambda j: (0, 0)), BS((D, D), lambda j: (0, j)), BS((8, D), lambda j: (0, j))],
        out_specs=[BS((D, D), lambda j: (0, j)), BS((1, D), lambda j: (0, j)), BS((8, D), lambda j: (0, 0))],
        out_shape=[S((D, N_MOD * D), F32), S((1, N_MOD * D), F32), S((8, D), F32)], compiler_params=_params(),
    )(cc, wmod, dm)


def _inproj_fwd(xa, m4, win, rt, n_lat_tiles, tiles_per_b, ctx_row, name):
    n_tiles = xa.shape[0] // rt

    def mrow(i):
        return jnp.where(i < n_lat_tiles, i // tiles_per_b, ctx_row)

    def body(x_ref, sh_ref, sc_ref, w_hbm, p_ref, h_ref, w_vm, sem):
        @pl.when(pl.program_id(0) == 0)
        def _():
            cp = pltpu.make_async_copy(w_hbm, w_vm, sem)
            cp.start()
            cp.wait()

        hb = _modln(x_ref[...], sh_ref[...], sc_ref[...]).astype(BF16)
        h_ref[...] = hb
        for j in range(P_W // P_CB):
            sl = slice(j * P_CB, (j + 1) * P_CB)
            p_ref[:, sl] = jnp.dot(hb, w_vm[:, sl], preferred_element_type=F32)

    return pl.pallas_call(
        body, grid=(n_tiles,), name=name,
        in_specs=[BS((rt, D), lambda i: (i, 0)),
                  BS((None, None, 1, D), lambda i: (mrow(i), 0, 0, 0)),
                  BS((None, None, 1, D), lambda i: (mrow(i), 1, 0, 0)),
                  BS(memory_space=pl.ANY)],
        out_specs=[BS((rt, P_W), lambda i: (i, 0)), BS((rt, D), lambda i: (i, 0))],
        out_shape=[S((xa.shape[0], P_W), F32), S((xa.shape[0], D), BF16)],
        scratch_shapes=[pltpu.VMEM((D, P_W), BF16), pltpu.SemaphoreType.DMA(())], compiler_params=_params(),
    )(xa, m4, m4, win)


def _inproj_bwd(xa, m4, win, dproj, rt, tile0, n_tiles, tiles_per_b, ctx_row, latent, dxres, side=None):
    def mrow(i):
        return (i // tiles_per_b) if latent else ctx_row

    def body(*refs):
        if side is None:
            return compute(*refs)
        own, exchange = side.split(refs, len(args), len(out_shape), 2)
        side.around(exchange, pl.program_id(0), n_tiles, lambda: compute(*own))

    def compute(x_ref, sh_ref, sc_ref, dp_ref, w_hbm, *rest):
        if latent:
            dxr_ref, gx_ref, dm_ref, w_vm, sem = rest
        else:
            dm_ref, w_vm, sem = rest
        i = pl.program_id(0)

        @pl.when(i == 0)
        def _():
            cp = pltpu.make_async_copy(w_hbm, w_vm, sem)
            cp.start()
            cp.wait()

        dh = lax.dot_general(dp_ref[...], w_vm[...], (((1,), (1,)), ((), ())), preferred_element_type=F32)
        _, vjp = jax.vjp(_modln, x_ref[...], sh_ref[...], sc_ref[...])
        dx, dsh, dsc = vjp(dh)
        if latent:
            gx_ref[...] = dx + dxr_ref[...]

        @pl.when(i % tiles_per_b == 0)
        def _():
            dm_ref[...] = jnp.zeros_like(dm_ref)

        dm_ref[0:1, :] += dsh
        dm_ref[1:2, :] += dsc

    nb = n_tiles // tiles_per_b
    in_specs = [BS((rt, D), lambda i: (tile0 + i, 0)),
                BS((None, None, 1, D), lambda i: (mrow(i), 0, 0, 0)),
                BS((None, None, 1, D), lambda i: (mrow(i), 1, 0, 0)),
                BS((rt, P_W), lambda i: (tile0 + i, 0)),
                BS(memory_space=pl.ANY)]
    args = [xa, m4, m4, dproj, win]
    dm_spec = BS((None, 2, D), lambda i: (i // tiles_per_b, 0, 0))
    if latent:
        in_specs.append(BS((rt, D), lambda i: (i, 0)))
        args.append(dxres)
        out_specs = [BS((rt, D), lambda i: (i, 0)), dm_spec]
        out_shape = [S((n_tiles * rt, D), F32), S((nb, 2, D), F32)]
    else:
        out_specs = [dm_spec]
        out_shape = [S((nb, 2, D), F32)]
    extra = side.arrays if side else []
    return pl.pallas_call(
        body, grid=(n_tiles,), name="inproj_bwd_lat" if latent else "inproj_bwd_ctx",
        in_specs=in_specs + [_HBM] * len(extra), out_specs=out_specs + [_HBM] * len(extra),
        out_shape=out_shape + (side.out_shapes if side else []),
        scratch_shapes=[pltpu.VMEM((D, P_W), BF16), pltpu.SemaphoreType.DMA(())] + (side.sems if side else []),
        compiler_params=_params(),
    )(*args, *extra)


def _matmul_tn(a, b, tm, tn, tk, name):
    k, m = a.shape
    n = b.shape[1]

    def body(a_ref, b_ref, o_ref):
        @pl.when(pl.program_id(2) == 0)
        def _():
            o_ref[...] = jnp.zeros_like(o_ref)

        o_ref[...] += lax.dot_general(a_ref[...], b_ref[...], (((0,), (0,)), ((), ())), preferred_element_type=F32)

    return pl.pallas_call(
        body, grid=(m // tm, n // tn, k // tk), name=name,
        in_specs=[BS((tk, tm), lambda i, j, kk: (kk, i)), BS((tk, tn), lambda i, j, kk: (kk, j))],
        out_specs=BS((tm, tn), lambda i, j, kk: (i, j)), out_shape=S((m, n), F32), compiler_params=_params(),
    )(a, b)


def _matmul_tn2(a1, b1, a2, b2, tm, tn, tk, name):
    k1, m = a1.shape
    n = b1.shape[1]
    n1, n2 = k1 // tk, a2.shape[0] // tk

    def body(a1_ref, b1_ref, a2_ref, b2_ref, o_ref):
        kk = pl.program_id(2)

        @pl.when(kk == 0)
        def _():
            o_ref[...] = jnp.zeros_like(o_ref)

        @pl.when(kk < n1)
        def _():
            o_ref[...] += lax.dot_general(a1_ref[...], b1_ref[...], (((0,), (0,)), ((), ())), preferred_element_type=F32)

        @pl.when(kk >= n1)
        def _():
            o_ref[...] += lax.dot_general(a2_ref[...], b2_ref[...], (((0,), (0,)), ((), ())), preferred_element_type=F32)

    first = lambda kk: jnp.minimum(kk, n1 - 1)
    second = lambda kk: jnp.maximum(kk - n1, 0)
    return pl.pallas_call(
        body, grid=(m // tm, n // tn, n1 + n2), name=name,
        in_specs=[BS((tk, tm), lambda i, j, kk: (first(kk), i)), BS((tk, tn), lambda i, j, kk: (first(kk), j)),
                  BS((tk, tm), lambda i, j, kk: (second(kk), i)), BS((tk, tn), lambda i, j, kk: (second(kk), j))],
        out_specs=BS((tm, tn), lambda i, j, kk: (i, j)), out_shape=S((m, n), F32), compiler_params=_params(),
    )(a1, b1, a2, b2)


def _ssd_consts(heads_per_tile):
    n = SSD_L
    ii = lax.broadcasted_iota(jnp.int32, (n, n), 0)
    jj = lax.broadcasted_iota(jnp.int32, (n, n), 1)
    er = lax.broadcasted_iota(jnp.int32, (128, 256), 0)
    ec = lax.broadcasted_iota(jnp.int32, (128, 256), 1) >> 6
    lane = lax.broadcasted_iota(jnp.int32, (1, 128 * heads_per_tile), 1) >> 6
    per_dir = []
    for d in (0, 1):
        mask = (jj >= ii) if d else (jj <= ii)
        per_dir.append((mask, mask.astype(BF16), (er == ec + 4 * d).astype(BF16)))
    return per_dir, [(lane == h).astype(F32) for h in range(2 * heads_per_tile)]


def _ssd_chunk(x, bm, cm, dtc, dtx, alog, hst, consts, hmasks, rev):
    n = SSD_L
    mask, tri, e = consts
    cum = _cum_mm(tri, dtc * (-jnp.exp(alog)))
    cum_x = _xp_mm(cum, e)
    tot_x = cum_x[0:1, :] if rev else cum_x[n - 1:n, :]
    xd = x * dtx
    hn = jnp.exp(tot_x) * hst + _mm_tn(bm, xd * jnp.exp(tot_x - cum_x))
    if cm is None:
        return hn
    cum_t = cum.T
    cb = _mm_nt(cm, bm)
    if len(hmasks) == 4:
        y = jnp.exp(cum_x) * _mm(cm, hst)
        for h in range(4):
            k = 4 * rev + h
            decay = jnp.exp(jnp.where(mask, cum[:, k:k + 1] - cum_t[k:k + 1, :], -1e30))
            y = y + _mm(cb * decay, xd * hmasks[h])
        return y, hn
    pairs = []
    for p in range(2):
        xdp = xd[:, 128 * p:128 * p + 128]
        yp = None
        for hh in range(2):
            k = 4 * rev + 2 * p + hh
            decay = jnp.exp(jnp.where(mask, cum[:, k:k + 1] - cum_t[k:k + 1, :], -1e30))
            term = _mm(cb * decay, xdp * hmasks[hh])
            yp = term if yp is None else yp + term
        pairs.append(yp)
    return jnp.exp(cum_x) * _mm(cm, hst) + jnp.concatenate(pairs, axis=1), hn


def _ssd_fwd(proj, dtg, cw, cb, dtb, alog, drow, h0f, h0b, nb, t, period, blk0, need_y, gather=()):
    nc = t // SSD_L
    rb = period
    assert t % rb == 0
    unroll = 4 if nc % 4 == 0 else (2 if nc % 2 == 0 else 1)
    masks_of = _conv_taps(SSD_GW, period)
    ng = len(gather)
    n_out = 5 if need_y else 4

    def body(p_ref, dt_ref, cw_ref, cb_ref, dtb_ref, al_ref, d_ref, h0f_ref, h0b_ref, *rest):
        g_ins, rest = rest[:ng], rest[ng:]
        outs, g_outs, (act, dts, dtxs), g_sems = rest[:n_out], rest[n_out:n_out + ng], rest[n_out + ng:n_out + ng + 3], \
            rest[n_out + ng + 3:]
        if need_y:
            y_ref, hsf_ref, hsb_ref, sf_ref, sb_ref = outs
        else:
            hsf_ref, hsb_ref, sf_ref, sb_ref = outs
        if ng:
            exchange = _GatherExchange(g_ins, g_outs, g_sems)
            step = pl.program_id(0) * SSD_G + pl.program_id(1)
            pl.when(step == 0)(exchange.begin)
        masks = masks_of(rb)
        per_dir, hmasks = _ssd_consts(2)

        def prologue(r0):
            rows = pl.ds(r0, rb)
            a = _silu(_conv_fwd(p_ref[rows, :], cw_ref[...], cb_ref[...], masks))
            act[rows, :] = a
            if need_y:
                y_ref[rows, :] = d_ref[...] * a[:, 0:256]
            dtv = _softplus(dt_ref[rows, :] + dtb_ref[...])
            dts[rows, :] = dtv
            for d in (0, 1):
                dtxs[rows, 256 * d:256 * d + 256] = _xp_mm(dtv, per_dir[d][2])

        _for_rows(t, rb, prologue)
        al = al_ref[...]

        def chunk(ci, carry):
            out = []
            for d, hst, hs_ref in ((0, carry[0], hsf_ref), (1, carry[1], hsb_ref)):
                c = (nc - 1 - ci) if d else ci
                r0 = pl.multiple_of(c * SSD_L, SSD_L)
                a = act[pl.ds(r0, SSD_L), :]
                hs_ref[c] = hst
                res = _ssd_chunk(a[:, 0:256], a[:, 256:384], a[:, 384:512] if need_y else None,
                                 dts[pl.ds(r0, SSD_L), :], dtxs[pl.ds(r0, SSD_L), 256 * d:256 * d + 256], al, hst,
                                 per_dir[d], hmasks, d)
                if need_y:
                    y_ref[pl.ds(r0, SSD_L), :] += res[0]
                    res = res[1]
                out.append(res)
            return tuple(out)

        sf_ref[...], sb_ref[...] = _loop_unrolled(nc, unroll, chunk, (h0f_ref[...], h0b_ref[...]))
        if ng:
            pl.when(step == nb * SSD_G - 1)(exchange.finish)

    gspec = lambda shp: BS((None,) + shp, lambda b, g: (g,) + (0,) * len(shp))
    st_spec = BS((None, None, SSD_N, 256), lambda b, g: (b, g, 0, 0))
    hs_spec = BS((None, None, nc, SSD_N, 256), lambda b, g: (b, g, 0, 0, 0))
    in_specs = [BS((t, SSD_GW), lambda b, g: (blk0 + b, g)), BS((None, t, 128), lambda b, g: (g, blk0 + b, 0)),
                gspec((4, SSD_GW)), gspec((1, SSD_GW)), gspec((1, 128)), gspec((1, 128)), gspec((1, 256)),
                st_spec, st_spec]
    out_specs = [hs_spec, hs_spec, st_spec, st_spec]
    out_shape = [S((nb, SSD_G, nc, SSD_N, 256), F32)] * 2 + [S((nb, SSD_G, SSD_N, 256), F32)] * 2
    if need_y:
        out_specs = [BS((t, 256), lambda b, g: (b, g))] + out_specs
        out_shape = [S((nb * t, SSD_INNER), F32)] + out_shape
    return pl.pallas_call(
        body, grid=(nb, SSD_G), name="ssd_fwd_lat" if need_y else "ssd_fwd_ctx",
        in_specs=in_specs + [_HBM] * ng, out_specs=out_specs + [_HBM] * ng,
        out_shape=out_shape + _gather_out_shapes(gather),
        scratch_shapes=[pltpu.VMEM((t, SSD_GW), F32), pltpu.VMEM((t, 128), F32), pltpu.VMEM((t, SSD_GW), F32)]
        + (_gather_sems(ng) if ng else []),
        compiler_params=_params(),
    )(proj, dtg, cw, cb, dtb, alog, drow, h0f, h0b, *gather)


def _ssd_bwd(proj, dtg, cw, cb, dtb, alog, drow, hsf, hsb, dy, dsf, dsb, dproj, nb, t, period, blk0, need_y, side=None):
    nc = t // SSD_L
    rb = period
    assert t % rb == 0
    unroll = 2 if nc % 2 == 0 else 1
    masks_of = _conv_taps(SSD_GW, period)

    def body(*refs):
        if side is None:
            return compute(*refs)
        own, exchange = side.split(refs, len(args), len(out_shape), 5)
        side.around(exchange, pl.program_id(0) * SSD_G + pl.program_id(1), nb * SSD_G, lambda: compute(*own))

    def compute(*refs):
        if need_y:
            (p_ref, dt_ref, cw_ref, cb_ref, dtb_ref, al_ref, d_ref, hsf_ref, hsb_ref, dy_ref, dsf_ref, dsb_ref, _,
             dp_ref, ddt_ref, dhf_ref, dhb_ref, dcw_ref, dcb_ref, ddtb_ref, dal_ref, dd_ref,
             pre, dact, dts, ddts, dtxs) = refs
        else:
            (p_ref, dt_ref, cw_ref, cb_ref, dtb_ref, al_ref, d_ref, hsf_ref, hsb_ref, dsf_ref, dsb_ref, _,
             dp_ref, ddt_ref, dhf_ref, dhb_ref, dcw_ref, dcb_ref, ddtb_ref, dal_ref, dd_ref,
             pre, dact, dts, ddts, dtxs) = refs
            dy_ref = None
        b, g = pl.program_id(0), pl.program_id(1)

        @pl.when(jnp.logical_and(b == 0, g == 0))
        def _():
            for r in (dcw_ref, dcb_ref, ddtb_ref, dal_ref, dd_ref):
                r[...] = jnp.zeros_like(r)

        masks = masks_of(rb)
        per_dir, hmasks = _ssd_consts(1)

        def prologue(r0):
            rows = pl.ds(r0, rb)
            pre[rows, :] = _conv_fwd(p_ref[rows, :], cw_ref[...], cb_ref[...], masks)
            dtv = _softplus(dt_ref[rows, :] + dtb_ref[...])
            dts[rows, :] = dtv
            for d in (0, 1):
                dtxs[rows, 256 * d:256 * d + 256] = _xp_mm(dtv, per_dir[d][2])
            dact[rows, :] = jnp.zeros((rb, SSD_GW), F32)
            ddts[rows, :] = jnp.zeros((rb, 128), F32)

        _for_rows(t, rb, prologue)
        al = al_ref[...]
        def chunk(ci, carry):
            dal_c = carry[2]
            dhs_out = []
            for d, dh, hs_ref in ((0, carry[0], hsf_ref), (1, carry[1], hsb_ref)):
                c = ci if d else (nc - 1 - ci)
                r0 = pl.multiple_of(c * SSD_L, SSD_L)
                a = _silu(pre[pl.ds(r0, SSD_L), :])
                dtc = dts[pl.ds(r0, SSD_L), :]
                dtx = dtxs[pl.ds(r0, SSD_L), 256 * d:256 * d + 256]
                if need_y:
                    fn = lambda x_, bm_, cm_, dt_, dx_, al_, hs_: _ssd_chunk(x_, bm_, cm_, dt_, dx_, al_, hs_, per_dir[d],
                                                                             hmasks, d)
                    _, vjp = jax.vjp(fn, a[:, 0:256], a[:, 256:384], a[:, 384:512], dtc, dtx, al, hs_ref[c])
                    dx, dbm, dcm, ddtc, ddtx, dal_k, dhs = vjp((dy_ref[pl.ds(r0, SSD_L), :], dh))
                    dact[pl.ds(r0, SSD_L), 384:512] += dcm
                else:
                    fn = lambda x_, bm_, dt_, dx_, al_, hs_: _ssd_chunk(x_, bm_, None, dt_, dx_, al_, hs_, per_dir[d],
                                                                        hmasks, d)
                    _, vjp = jax.vjp(fn, a[:, 0:256], a[:, 256:384], dtc, dtx, al, hs_ref[c])
                    dx, dbm, ddtc, ddtx, dal_k, dhs = vjp(dh)
                dact[pl.ds(r0, SSD_L), 0:256] += dx
                dact[pl.ds(r0, SSD_L), 256:384] += dbm
                ddts[pl.ds(r0, SSD_L), :] += ddtc + _dot_nt(ddtx, per_dir[d][2])
                dhs_out.append(dhs)
                dal_c = dal_c + dal_k
            return dhs_out[0], dhs_out[1], dal_c

        dhf_ref[...], dhb_ref[...], dal_acc = _loop_unrolled(
            nc, unroll, chunk, (dsf_ref[...], dsb_ref[...], jnp.zeros((1, 128), F32)))

        def epilogue(r0):
            rows = pl.ds(r0, rb)
            prev = pre[rows, :]
            if need_y:
                dyv = dy_ref[rows, :]
                dact[rows, 0:256] += d_ref[...] * dyv
                dd_ref[g] += _colsum(dyv * _silu(prev[:, 0:256]))
            dpre = dact[rows, :] * _dsilu(prev)
            draw, dcw, dcb = _conv_bwd(dpre, p_ref[rows, :], cw_ref[...], masks)
            dp_ref[rows, :] = draw.astype(BF16)
            dcw_ref[g] += dcw
            dcb_ref[g] += dcb
            ddraw = ddts[rows, :] * _sigmoid(dt_ref[rows, :] + dtb_ref[...])
            ddt_ref[rows, :] = ddraw
            ddtb_ref[g] += _colsum(ddraw)

        _for_rows(t, rb, epilogue)
        dal_ref[g] += dal_acc

    gspec = lambda shp: BS((None,) + shp, lambda b, g: (g,) + (0,) * len(shp))
    full = lambda shp: BS(shp, lambda b, g: (0,) * len(shp))
    st_spec = BS((None, None, SSD_N, 256), lambda b, g: (b, g, 0, 0))
    hs_spec = BS((None, None, nc, SSD_N, 256), lambda b, g: (b, g, 0, 0, 0))
    p_spec = BS((t, SSD_GW), lambda b, g: (blk0 + b, g))
    in_specs = [p_spec, BS((None, t, 128), lambda b, g: (g, blk0 + b, 0)),
                gspec((4, SSD_GW)), gspec((1, SSD_GW)), gspec((1, 128)), gspec((1, 128)), gspec((1, 256)),
                hs_spec, hs_spec]
    args = [proj, dtg, cw, cb, dtb, alog, drow, hsf, hsb]
    if need_y:
        in_specs.append(BS((t, 256), lambda b, g: (b, g)))
        args.append(dy)
    in_specs += [st_spec, st_spec, BS(memory_space=pl.ANY)]
    args += [dsf, dsb, dproj]
    out_specs = [p_spec, BS((None, t, 128), lambda b, g: (g, b, 0)), st_spec, st_spec,
                 full((SSD_G, 4, SSD_GW)), full((SSD_G, 1, SSD_GW)), full((SSD_G, 1, 128)), full((SSD_G, 1, 128)),
                 full((SSD_G, 1, 256))]
    out_shape = [S(dproj.shape, BF16), S((SSD_G, nb * t, 128), F32),
                 S((nb, SSD_G, SSD_N, 256), F32), S((nb, SSD_G, SSD_N, 256), F32),
                 S((SSD_G, 4, SSD_GW), F32), S((SSD_G, 1, SSD_GW), F32), S((SSD_G, 1, 128), F32),
                 S((SSD_G, 1, 128), F32), S((SSD_G, 1, 256), F32)]
    extra = side.arrays if side else []
    return pl.pallas_call(
        body, grid=(nb, SSD_G), name="ssd_bwd_lat" if need_y else "ssd_bwd_ctx",
        in_specs=in_specs + [_HBM] * len(extra), out_specs=out_specs + [_HBM] * len(extra),
        out_shape=out_shape + (side.out_shapes if side else []),
        input_output_aliases={len(args) - 1: 0},
        scratch_shapes=[pltpu.VMEM((t, SSD_GW), F32), pltpu.VMEM((t, SSD_GW), F32), pltpu.VMEM((t, 128), F32),
                        pltpu.VMEM((t, 128), F32), pltpu.VMEM((t, SSD_GW), F32)] + (side.sems if side else []),
        compiler_params=_params(),
    )(*args, *extra)


def _lru_gate(u, wa, ba, wi, bi, lam):
    r = _sigmoid(_mm(u, wa) + ba)
    i = _sigmoid(_mm(u, wi) + bi)
    log_a = -LRU_C * r * _softplus(-lam)
    a = jnp.exp(log_a)
    x2 = 2.0 * log_a
    em1 = jnp.where(x2 > -0.01, x2 * (1.0 + x2 * (0.5 + x2 * (1.0 / 6.0 + x2 * (1.0 / 24.0)))), a * a - 1.0)
    return a, jnp.sqrt(-em1) * (i * u)


def _scan_pair(fwd, rev, nblk, width):
    row = lax.broadcasted_iota(jnp.int32, (8, width), 0)

    def block(a_ref, b_ref, h_ref, st, carry, reverse):
        av, bv = a_ref[pl.ds(st, 8), :], b_ref[pl.ds(st, 8), :]
        for s in (1, 2, 4):
            ok = (row < 8 - s) if reverse else (row >= s)
            sh = (8 - s) if reverse else s
            a_sh = jnp.where(ok, pltpu.roll(av, sh, 0), 1.0)
            b_sh = jnp.where(ok, pltpu.roll(bv, sh, 0), 0.0)
            bv = av * b_sh + bv
            av = av * a_sh
        h = bv + av * carry
        h_ref[pl.ds(st, 8), :] = h
        return h[0:1, :] if reverse else h[7:8, :]

    def step(i, carry):
        cf, cr = carry
        cf = block(fwd[0], fwd[1], fwd[2], pl.multiple_of(i * 8, 8), cf, False)
        cr = block(rev[0], rev[1], rev[2], pl.multiple_of((nblk - 1 - i) * 8, 8), cr, True)
        return cf, cr

    return lax.fori_loop(0, nblk, step, (fwd[3], rev[3]))


def _lru_specs(t, blk0):
    p_spec = BS((t, LRU_CB), lambda b, q: (blk0 + b, P_LRU // LRU_CB + q))
    w_spec = BS((2, 2, 128, 128), lambda b, q: (0, q, 0, 0))
    v_spec = BS((2, LRU_CB), lambda b, q: (0, q))
    c_spec = lambda r: BS((r, LRU_CB), lambda b, q: (0, q))
    s_spec = BS((None, 2, LRU_CB), lambda b, q: (b, 0, q))
    return p_spec, w_spec, v_spec, c_spec, s_spec


def _lru_fwd(proj, cw, cb, wa, ba, wi, bi, lam, h0, nb, t, period, blk0, need_y):
    nq = D // LRU_CB
    masks_of = _conv_taps(LRU_CB, period)

    def body(p_ref, cw_ref, cb_ref, wa_ref, ba_ref, wi_ref, bi_ref, lam_ref, h0_ref, *rest):
        if need_y:
            y_ref, hf_ref, hb_ref, fin_ref, sa0, sb0, sa1, sb1 = rest
        else:
            hf_ref, hb_ref, fin_ref, sa0, sb0, sa1, sb1 = rest
        u = _conv_fwd(p_ref[...], cw_ref[...], cb_ref[...], masks_of(t))
        for d, (sa, sb) in enumerate(((sa0, sb0), (sa1, sb1))):
            for j in range(2):
                sl = slice(128 * j, 128 * j + 128)
                a, bb = _lru_gate(u[:, sl], wa_ref[d, j], ba_ref[d:d + 1, sl], wi_ref[d, j], bi_ref[d:d + 1, sl],
                                  lam_ref[d:d + 1, sl])
                sa[:, sl] = a
                sb[:, sl] = bb
        lf, lb = _scan_pair((sa0, sb0, hf_ref, h0_ref[0:1, :]), (sa1, sb1, hb_ref, h0_ref[1:2, :]), t // 8, LRU_CB)
        fin_ref[0:1, :] = lf
        fin_ref[1:2, :] = lb
        if need_y:
            y_ref[...] = hf_ref[...] + hb_ref[...]

    p_spec, w_spec, v_spec, c_spec, s_spec = _lru_specs(t, blk0)
    o_spec = BS((t, LRU_CB), lambda b, q: (b, q))
    out_specs = [o_spec, o_spec, s_spec]
    out_shape = [S((nb * t, D), F32), S((nb * t, D), F32), S((nb, 2, D), F32)]
    if need_y:
        out_specs = [o_spec] + out_specs
        out_shape = [S((nb * t, D), F32)] + out_shape
    return pl.pallas_call(
        body, grid=(nb, nq), name="lru_fwd_lat" if need_y else "lru_fwd_ctx",
        in_specs=[p_spec, c_spec(4), c_spec(1), w_spec, v_spec, w_spec, v_spec, v_spec, s_spec],
        out_specs=out_specs, out_shape=out_shape,
        scratch_shapes=[pltpu.VMEM((t, LRU_CB), F32)] * 4, compiler_params=_params(),
    )(proj, cw, cb, wa, ba, wi, bi, lam, h0)


def _lru_bwd(proj, cw, cb, wa, ba, wi, bi, lam, h0, hf, hb, dy, dfin, dproj, nb, t, period, blk0, need_y, side=None):
    nq = D // LRU_CB
    rc = min(512, t)
    masks_of = _conv_taps(LRU_CB, period)

    def body(*refs):
        if side is None:
            return compute(*refs)
        own, exchange = side.split(refs, len(args), len(out_shape), 7)
        side.around(exchange, pl.program_id(0) * nq + pl.program_id(1), nb * nq, lambda: compute(*own))

    def compute(*refs):
        if need_y:
            (p_ref, cw_ref, cb_ref, wa_ref, ba_ref, wi_ref, bi_ref, lam_ref, h0_ref, hf_ref, hb_ref, dy_ref, dfin_ref, _,
             dp_ref, dh0_ref, dcw_ref, dcb_ref, dwa_ref, dwi_ref, dba_ref, dbi_ref, dlam_ref,
             su, sa0, sa1, sc0, sc1, sg0, sg1) = refs
        else:
            (p_ref, cw_ref, cb_ref, wa_ref, ba_ref, wi_ref, bi_ref, lam_ref, h0_ref, hf_ref, hb_ref, dfin_ref, _,
             dp_ref, dh0_ref, dcw_ref, dcb_ref, dwa_ref, dwi_ref, dba_ref, dbi_ref, dlam_ref,
             su, sa0, sa1, sc0, sc1, sg0, sg1) = refs
            dy_ref = None
        b, q = pl.program_id(0), pl.program_id(1)

        @pl.when(jnp.logical_and(b == 0, q == 0))
        def _():
            for r in (dcw_ref, dcb_ref, dwa_ref, dwi_ref, dba_ref, dbi_ref, dlam_ref):
                r[...] = jnp.zeros_like(r)

        masks = masks_of(t)
        u = _conv_fwd(p_ref[...], cw_ref[...], cb_ref[...], masks)
        su[...] = u
        for d, sa in enumerate((sa0, sa1)):
            for j in range(2):
                sl = slice(128 * j, 128 * j + 128)
                a, _unused = _lru_gate(u[:, sl], wa_ref[d, j], ba_ref[d:d + 1, sl], wi_ref[d, j], bi_ref[d:d + 1, sl],
                                       lam_ref[d:d + 1, sl])
                sa[:, sl] = a
        rowi = lax.broadcasted_iota(jnp.int32, (t, LRU_CB), 0)
        last, first = rowi == t - 1, rowi == 0
        sc0[...] = jnp.where(last, 0.0, pltpu.roll(sa0[...], t - 1, 0))
        sc1[...] = jnp.where(first, 0.0, pltpu.roll(sa1[...], 1, 0))
        g0 = jnp.where(last, dfin_ref[0:1, :], 0.0)
        g1 = jnp.where(first, dfin_ref[1:2, :], 0.0)
        if need_y:
            g0 = g0 + dy_ref[...]
            g1 = g1 + dy_ref[...]
        sg0[...] = g0
        sg1[...] = g1
        zero = jnp.zeros((1, LRU_CB), F32)
        _scan_pair((sc1, sg1, sg1, zero), (sc0, sg0, sg0, zero), t // 8, LRU_CB)
        dh0_ref[0:1, :] = sa0[0:1, :] * sg0[0:1, :]
        dh0_ref[1:2, :] = sa1[t - 1:t, :] * sg1[t - 1:t, :]
        sc0[...] = sg0[...] * jnp.where(first, h0_ref[0:1, :], pltpu.roll(hf_ref[...], 1, 0))
        sc1[...] = sg1[...] * jnp.where(last, h0_ref[1:2, :], pltpu.roll(hb_ref[...], t - 1, 0))

        def rows(ci, carry):
            r0 = pl.multiple_of(ci * rc, rc)
            for j in range(2):
                sl = slice(128 * j, 128 * j + 128)
                du = jnp.zeros((rc, 128), F32)
                for d, (sc, sg) in enumerate(((sc0, sg0), (sc1, sg1))):
                    _, vjp = jax.vjp(_lru_gate, su[pl.ds(r0, rc), sl], wa_ref[d, j], ba_ref[d:d + 1, sl], wi_ref[d, j],
                                     bi_ref[d:d + 1, sl], lam_ref[d:d + 1, sl])
                    du_d, dwa, dba, dwi, dbi, dlam = vjp((sc[pl.ds(r0, rc), sl], sg[pl.ds(r0, rc), sl]))
                    du = du + du_d
                    dwa_ref[d, 2 * q + j] += dwa
                    dwi_ref[d, 2 * q + j] += dwi
                    dba_ref[q, d:d + 1, sl] += dba
                    dbi_ref[q, d:d + 1, sl] += dbi
                    dlam_ref[q, d:d + 1, sl] += dlam
                sa0[pl.ds(r0, rc), sl] = du
            return carry

        lax.fori_loop(0, t // rc, rows, 0)
        draw, dcw, dcb = _conv_bwd(sa0[...], p_ref[...], cw_ref[...], masks)
        dp_ref[...] = draw.astype(BF16)
        dcw_ref[q] += dcw
        dcb_ref[q] += dcb

    p_spec, w_spec, v_spec, c_spec, s_spec = _lru_specs(t, blk0)
    o_spec = BS((t, LRU_CB), lambda b, q: (b, q))
    full = lambda shp: BS(shp, lambda b, q: (0,) * len(shp))
    in_specs = [p_spec, c_spec(4), c_spec(1), w_spec, v_spec, w_spec, v_spec, v_spec, s_spec, o_spec, o_spec]
    args = [proj, cw, cb, wa, ba, wi, bi, lam, h0, hf, hb]
    if need_y:
        in_specs.append(o_spec)
        args.append(dy)
    in_specs += [s_spec, BS(memory_space=pl.ANY)]
    args += [dfin, dproj]
    out_specs = [p_spec, s_spec, full((nq, 4, LRU_CB)), full((nq, 1, LRU_CB)), full((2, 8, 128, 128)),
                 full((2, 8, 128, 128)), full((nq, 2, LRU_CB)), full((nq, 2, LRU_CB)), full((nq, 2, LRU_CB))]
    out_shape = [S(dproj.shape, BF16), S((nb, 2, D), F32), S((nq, 4, LRU_CB), F32), S((nq, 1, LRU_CB), F32),
                 S((2, 8, 128, 128), F32), S((2, 8, 128, 128), F32), S((nq, 2, LRU_CB), F32), S((nq, 2, LRU_CB), F32),
                 S((nq, 2, LRU_CB), F32)]
    extra = side.arrays if side else []
    return pl.pallas_call(
        body, grid=(nb, nq), name="lru_bwd_lat" if need_y else "lru_bwd_ctx",
        in_specs=in_specs + [_HBM] * len(extra), out_specs=out_specs + [_HBM] * len(extra),
        out_shape=out_shape + (side.out_shapes if side else []), input_output_aliases={len(args) - 1: 0},
        scratch_shapes=[pltpu.VMEM((t, LRU_CB), F32)] * 7 + (side.sems if side else []), compiler_params=_params(),
    )(*args, *extra)


def _mix_core(y_ref, yl_ref, p_ref, nw_ref, bg_ref, wbs_ref, wbl_ref, wo_ref, nrm_s):
    for g in range(SSD_G):
        sl = slice(256 * g, 256 * g + 256)
        nrm_s[:, sl] = _grms(y_ref[:, sl], p_ref[:, sl], nw_ref[:, sl]).astype(BF16)
    br_s = jnp.dot(nrm_s[...], wbs_ref[...], preferred_element_type=F32)
    gl = (yl_ref[...] * _gelu(p_ref[:, 2048:3072])).astype(BF16)
    br_l = jnp.dot(gl, wbl_ref[...], preferred_element_type=F32)
    gs = _sigmoid(p_ref[:, 3072:4096] + bg_ref[:, 0:D])
    gr = _sigmoid(p_ref[:, 4096:5120] + bg_ref[:, D:2 * D])
    mix = (gs * br_s + gr * br_l).astype(BF16)
    xmix = jnp.dot(mix, wo_ref[...], preferred_element_type=F32)
    return br_s, gl, br_l, gs, gr, mix, xmix


def _mix_specs(rt, tiles_per_b):
    row = lambda w: BS((rt, w), lambda i: (i, 0))
    const = lambda shp: BS(shp, lambda i: (0,) * len(shp))
    gate = BS((None, None, 1, D), lambda i: (i // tiles_per_b, 2, 0, 0))
    return row, const, gate


def _mix_fwd(y, ylru, proj, x, m4, wbs, wbl, wo, nw, bg, l1g, l1b, rt, tiles_per_b):
    n = x.shape[0]

    def body(y_ref, yl_ref, p_ref, x_ref, g1_ref, wbs_ref, wbl_ref, wo_ref, nw_ref, bg_ref, lg_ref, lb_ref,
             x1_ref, nrm_ref, gl_ref, mix_ref, brs_ref, brl_ref, xm_ref):
        br_s, gl, br_l, _, _, mix, xmix = _mix_core(y_ref, yl_ref, p_ref, nw_ref, bg_ref, wbs_ref, wbl_ref, wo_ref, nrm_ref)
        gl_ref[...] = gl
        mix_ref[...] = mix
        brs_ref[...] = br_s
        brl_ref[...] = br_l
        xm_ref[...] = xmix
        x1_ref[...] = _resln(x_ref[...], xmix, g1_ref[...], lg_ref[...], lb_ref[...])

    row, const, gate = _mix_specs(rt, tiles_per_b)
    return pl.pallas_call(
        body, grid=(n // rt,), name="mix_fwd",
        in_specs=[row(SSD_INNER), row(D), BS((rt, 5120), lambda i: (i, 1)), row(D), gate,
                  const((SSD_INNER, D)), const((D, D)), const((D, D)), const((1, SSD_INNER)), const((1, 2 * D)),
                  const((1, D)), const((1, D))],
        out_specs=[row(D), row(SSD_INNER), row(D), row(D), row(D), row(D), row(D)],
        out_shape=[S((n, D), F32), S((n, SSD_INNER), BF16), S((n, D), BF16), S((n, D), BF16), S((n, D), F32),
                   S((n, D), F32), S((n, D), F32)],
        compiler_params=_params(),
    )(y, ylru, proj, x, m4, wbs, wbl, wo, nw, bg, l1g, l1b)


def _mix_bwd(y, ylru, proj, x, m4, wbs, wbl, wo, nw, bg, l1g, l1b, brs, brl, xmix, dx1, dproj, rt, tiles_per_b):
    n = x.shape[0]

    def body(y_ref, yl_ref, p_ref, x_ref, g1_ref, wbs_ref, wbl_ref, wo_ref, nw_ref, bg_ref, lg_ref, lb_ref,
             brs_ref, brl_ref, xm_ref, dx1_ref, _,
             dp_ref, dy_ref, dyl_ref, dxr_ref, dbrs_ref, dbrl_ref, dxm_ref,
             dg1_ref, dnw_ref, dbg_ref, dlg_ref, dlb_ref):
        i = pl.program_id(0)

        @pl.when(i == 0)
        def _():
            for r in (dnw_ref, dbg_ref, dlg_ref, dlb_ref):
                r[...] = jnp.zeros_like(r)

        @pl.when(i % tiles_per_b == 0)
        def _():
            dg1_ref[...] = jnp.zeros_like(dg1_ref)

        br_s, br_l = brs_ref[...], brl_ref[...]
        gs = _sigmoid(p_ref[:, 3072:4096] + bg_ref[:, 0:D])
        gr = _sigmoid(p_ref[:, 4096:5120] + bg_ref[:, D:2 * D])
        _, vjp = jax.vjp(_resln, x_ref[...], xm_ref[...], g1_ref[...], lg_ref[...], lb_ref[...])
        dxr, dxmix, dg1, dlg, dlb = vjp(dx1_ref[...])
        dxr_ref[...] = dxr
        dg1_ref[...] += dg1
        dlg_ref[...] += dlg
        dlb_ref[...] += dlb
        dxmb = dxmix.astype(BF16)
        dxm_ref[...] = dxmb
        dmix = lax.dot_general(dxmb, wo_ref[...], (((1,), (1,)), ((), ())), preferred_element_type=F32)
        dbrs = (dmix * gs).astype(BF16)
        dbrl = (dmix * gr).astype(BF16)
        dbrs_ref[...] = dbrs
        dbrl_ref[...] = dbrl
        dmg_s = dmix * br_s * gs * (1.0 - gs)
        dmg_r = dmix * br_l * gr * (1.0 - gr)
        dp_ref[:, 3072:4096] = dmg_s.astype(BF16)
        dp_ref[:, 4096:5120] = dmg_r.astype(BF16)
        dbg_ref[:, 0:D] += _colsum(dmg_s)
        dbg_ref[:, D:2 * D] += _colsum(dmg_r)
        dnrm = lax.dot_general(dbrs, wbs_ref[...], (((1,), (1,)), ((), ())), preferred_element_type=F32)
        for g in range(SSD_G):
            sl = slice(256 * g, 256 * g + 256)
            _, vjp = jax.vjp(_grms, y_ref[:, sl], p_ref[:, sl], nw_ref[:, sl])
            dyg, dzg, dnwg = vjp(dnrm[:, sl])
            dy_ref[:, sl] = dyg
            dp_ref[:, sl] = dzg.astype(BF16)
            dnw_ref[:, sl] += dnwg
        dgl = lax.dot_general(dbrl, wbl_ref[...], (((1,), (1,)), ((), ())), preferred_element_type=F32)
        _, vjp = jax.vjp(lambda a, c: a * _gelu(c), yl_ref[...], p_ref[:, 2048:3072])
        dyl, dlgate = vjp(dgl)
        dyl_ref[...] = dyl
        dp_ref[:, 2048:3072] = dlgate.astype(BF16)

    row, const, gate = _mix_specs(rt, tiles_per_b)
    pblk = BS((rt, 5120), lambda i: (i, 1))
    nb = n // (rt * tiles_per_b)
    out_specs = [pblk, row(SSD_INNER), row(D), row(D), row(D), row(D), row(D),
                 BS((None, 1, D), lambda i: (i // tiles_per_b, 0, 0)), const((1, SSD_INNER)), const((1, 2 * D)),
                 const((1, D)), const((1, D))]
    out_shape = [S(dproj.shape, BF16), S((n, SSD_INNER), F32), S((n, D), F32), S((n, D), F32),
                 S((n, D), BF16), S((n, D), BF16), S((n, D), BF16),
                 S((nb, 1, D), F32), S((1, SSD_INNER), F32), S((1, 2 * D), F32), S((1, D), F32), S((1, D), F32)]
    return pl.pallas_call(
        body, grid=(n // rt,), name="mix_bwd",
        in_specs=[row(SSD_INNER), row(D), pblk, row(D), gate,
                  const((SSD_INNER, D)), const((D, D)), const((D, D)), const((1, SSD_INNER)), const((1, 2 * D)),
                  const((1, D)), const((1, D)), row(D), row(D), row(D), row(D), BS(memory_space=pl.ANY)],
        out_specs=out_specs, out_shape=out_shape, input_output_aliases={16: 0},
        compiler_params=_params(),
    )(y, ylru, proj, x, m4, wbs, wbl, wo, nw, bg, l1g, l1b, brs, brl, xmix, dx1, dproj)


def _mlp_step(x1, tgt, m4, w1, b1, w2, b2, l2g, l2b, rt, tiles_per_b):
    n = x1.shape[0]

    def body(x_ref, t_ref, sh_ref, sc_ref, gt_ref, w1_hbm, b1_ref, w2_hbm, b2_ref, lg_ref, lb_ref,
             loss_ref, dx_ref, h2_ref, da1_ref, r2_ref, dmlp_ref, dm_ref, db1_ref, db2_ref, dlg_ref, dlb_ref,
             w1_vm, w2_vm, sem):
        i = pl.program_id(0)

        @pl.when(i == 0)
        def _():
            c1 = pltpu.make_async_copy(w1_hbm, w1_vm, sem.at[0])
            c2 = pltpu.make_async_copy(w2_hbm, w2_vm, sem.at[1])
            c1.start()
            c2.start()
            for r in (loss_ref, db1_ref, db2_ref, dlg_ref, dlb_ref):
                r[...] = jnp.zeros_like(r)
            c1.wait()
            c2.wait()

        @pl.when(i % tiles_per_b == 0)
        def _():
            dm_ref[...] = jnp.zeros_like(dm_ref)

        x1v = x_ref[...]
        h2, vjp_h = jax.vjp(_modln, x1v, sh_ref[...], sc_ref[...])
        h2b = h2.astype(BF16)
        h2_ref[...] = h2b
        r = jnp.maximum(jnp.dot(h2b, w1_vm[...], preferred_element_type=F32) + b1_ref[...], 0.0)
        r2b = (r * r).astype(BF16)
        r2_ref[...] = r2b
        mlp = jnp.dot(r2b, w2_vm[...], preferred_element_type=F32) + b2_ref[...]
        x2, vjp_r = jax.vjp(_resln, x1v, mlp, gt_ref[...], lg_ref[...], lb_ref[...])
        diff = x2 - t_ref[...]
        loss_ref[...] += (0.5 / D) * jnp.sum(diff * diff)
        dxa, dmlp, dgt, dlg, dlb = vjp_r(diff * (1.0 / D))
        dlg_ref[...] += dlg
        dlb_ref[...] += dlb
        dm_ref[2:3, :] += dgt
        db2_ref[...] += _colsum(dmlp)
        dmlpb = dmlp.astype(BF16)
        dmlp_ref[...] = dmlpb
        da1 = lax.dot_general(dmlpb, w2_vm[...], (((1,), (1,)), ((), ())), preferred_element_type=F32) * (2.0 * r)
        db1_ref[...] += _colsum(da1)
        da1b = da1.astype(BF16)
        da1_ref[...] = da1b
        dh2 = lax.dot_general(da1b, w1_vm[...], (((1,), (1,)), ((), ())), preferred_element_type=F32)
        dxb, dsh, dsc = vjp_h(dh2)
        dx_ref[...] = dxa + dxb
        dm_ref[0:1, :] += dsh
        dm_ref[1:2, :] += dsc

    row = lambda w: BS((rt, w), lambda i: (i, 0))
    const = lambda shp: BS(shp, lambda i: (0,) * len(shp))
    mod = lambda k: BS((None, None, 1, D), lambda i: (i // tiles_per_b, k, 0, 0))
    nb = n // (rt * tiles_per_b)
    anyspec = BS(memory_space=pl.ANY)
    return pl.pallas_call(
        body, grid=(n // rt,), name="mlp_step",
        in_specs=[row(D), row(D), mod(3), mod(4), mod(5), anyspec, const((1, MLP_H)), anyspec, const((1, D)),
                  const((1, D)), const((1, D))],
        out_specs=[const((8, 128)), row(D), row(D), row(MLP_H), row(MLP_H), row(D),
                   BS((None, 3, D), lambda i: (i // tiles_per_b, 0, 0)), const((1, MLP_H)), const((1, D)),
                   const((1, D)), const((1, D))],
        out_shape=[S((8, 128), F32), S((n, D), F32), S((n, D), BF16), S((n, MLP_H), BF16), S((n, MLP_H), BF16),
                   S((n, D), BF16), S((nb, 3, D), F32), S((1, MLP_H), F32), S((1, D), F32), S((1, D), F32),
                   S((1, D), F32)],
        scratch_shapes=[pltpu.VMEM((D, MLP_H), BF16), pltpu.VMEM((MLP_H, D), BF16), pltpu.SemaphoreType.DMA((2,))],
        compiler_params=_params(),
    )(x1, tgt, m4, m4, m4, w1, b1, w2, b2, l2g, l2b)


def _pack_win(w):
    parts = []
    for g in range(SSD_G):
        parts += [w[:, 256 * g:256 * g + 256], w[:, 2048 + 128 * g:2176 + 128 * g], w[:, 4160 + 128 * g:4288 + 128 * g]]
    parts += [w[:, 3136:4160], w[:, 5184:7232], w[:, 7232:8256], w[:, 8256:10304], w[:, 3072:3136],
              jnp.zeros((w.shape[0], P_W - P_DT - 64), w.dtype)]
    return jnp.concatenate(parts, axis=1)


def _unpack_win(p):
    xs = [p[:, 512 * g:512 * g + 256] for g in range(SSD_G)]
    bs = [p[:, 512 * g + 256:512 * g + 384] for g in range(SSD_G)]
    cs = [p[:, 512 * g + 384:512 * g + 512] for g in range(SSD_G)]
    return jnp.concatenate(xs + bs + [p[:, P_DT:P_DT + 64], p[:, P_LRU:P_Z]] + cs + [p[:, P_Z:P_DT]], axis=1)


def _pack_conv(w):
    return jnp.stack([jnp.concatenate([w[:, 256 * g:256 * g + 256], w[:, 2048 + 128 * g:2176 + 128 * g],
                                       w[:, 3072 + 128 * g:3200 + 128 * g]], axis=1) for g in range(SSD_G)])


def _unpack_conv(p):
    r = p.shape[1]
    x = jnp.transpose(p[:, :, 0:256], (1, 0, 2)).reshape(r, 2048)
    b = jnp.transpose(p[:, :, 256:384], (1, 0, 2)).reshape(r, 1024)
    c = jnp.transpose(p[:, :, 384:512], (1, 0, 2)).reshape(r, 1024)
    return jnp.concatenate([x, b, c], axis=1)


def _pack_heads(v):
    p = jnp.transpose(v.reshape(2, SSD_G, 4), (1, 0, 2)).reshape(SSD_G, 1, 8)
    return jnp.pad(p, ((0, 0), (0, 0), (0, 120)))


def _unpack_heads(p):
    return jnp.transpose(p[:, 0, 0:8].reshape(SSD_G, 2, 4), (1, 0, 2)).reshape(2, 32)


def _pack_dt(dt):
    n = dt.shape[0]
    p = jnp.transpose(dt.reshape(n, 2, SSD_G, 4), (2, 0, 1, 3)).reshape(SSD_G, n, 8)
    return jnp.pad(p, ((0, 0), (0, 0), (0, 120)))


def _unpack_dt(p):
    n = p.shape[1]
    return jnp.transpose(p[:, :, 0:8].reshape(SSD_G, n, 2, 4), (1, 2, 0, 3)).reshape(n, 64)


def _tk(rows):
    return next(tk for tk in (1024, 512, 256, 128) if rows % tk == 0)


LATE = ["w_br_ssd", "w_br_lru", "w_out", "w_mlp1", "w_mlp2"]


def _local_step(x, c, ctx, tgt, sm, wmod, win, late, late_are_shards=False, reducer=None):
    nb, t, _ = x.shape
    tc = ctx.shape[1]
    nl, ncx = nb * t, nb * tc
    rt = 256 if tc % 256 == 0 else 128
    rtm = 128
    xl, xc = x.reshape(nl, D), ctx.reshape(ncx, D)
    tgt2 = tgt.reshape(nl, D)
    cc = jnp.zeros((8, D), F32).at[0:nb].set(c).at[nb].set(sm["c_ctx"])
    m = _mod_fwd(cc, wmod, sm["b_mod"])
    m4 = m.reshape(8, N_MOD, 1, D)
    proj, h1 = _inproj_fwd(xl, m4, win, rtm, nl // rtm, t // rtm, nb, "inproj_fwd_lat")
    proj_c, h1_c = _inproj_fwd(xc, m4, win, rtm, 0, 1, nb, "inproj_fwd_ctx")

    cw_s, cb_s = _pack_conv(sm["ssd_conv_w"]), _pack_conv(sm["ssd_conv_b"])
    dtb, alog = _pack_heads(sm["ssd_dt_bias"]), _pack_heads(sm["ssd_a_log"])
    drow = jnp.repeat(sm["ssd_d"].reshape(32), 64).reshape(SSD_G, 1, 256)
    dtg, dtg_c = _pack_dt(proj[:, P_DT:P_DT + 64]), _pack_dt(proj_c[:, P_DT:P_DT + 64])
    zst = jnp.zeros((nb, SSD_G, SSD_N, 256), F32)
    zl = jnp.zeros((nb, 2, D), F32)
    ssd_p = (cw_s, cb_s, dtb, alog, drow)
    lru_p = (sm["lru_conv_w"], sm["lru_conv_b"], sm["lru_wa"], sm["lru_ba"], sm["lru_wi"], sm["lru_bi"], sm["lru_lambda"])

    chsf, chsb, csf, csb = _ssd_fwd(proj_c, dtg_c, *ssd_p, zst, zst, nb, tc, tc, 0, False)
    y, lhsf, lhsb, _, _, *got = _ssd_fwd(proj, dtg, *ssd_p, csf, csb, nb, t, GRID_W, 0, True,
                                         tuple(late) if late_are_shards else ())
    wbs, wbl, wo, w1, w2 = [_full_from_chips(g, n) for g, n in zip(got, LATE)] if late_are_shards else late
    chf, chb, cfin = _lru_fwd(proj_c, *lru_p, zl, nb, tc, tc, 0, False)
    ylru, lhf, lhb, _ = _lru_fwd(proj, *lru_p, cfin, nb, t, GRID_W, 0, True)
    mix_w = (wbs, wbl, wo, sm["ssd_norm_w"], sm["b_gate"], sm["ln1_g"], sm["ln1_b"])
    x1, nrm, gl, mixb, brs, brl, xmix = _mix_fwd(y, ylru, proj, xl, m4, *mix_w, rtm, t // rtm)
    (loss, dx1, h2, da1, r2, dmlp, dm2, db1, db2, dl2g, dl2b) = _mlp_step(
        x1, tgt2, m4, w1, sm["b_mlp1"], w2, sm["b_mlp2"], sm["ln2_g"], sm["ln2_b"], rt, t // rt)

    dproj = lax.empty((nl, P_W), BF16)
    dproj_c = jnp.zeros((ncx, P_W), BF16)
    (dproj, dy, dylru, dxres, dbrs, dbrl, dxm, dg1, dnw, dbg, dl1g, dl1b) = _mix_bwd(
        y, ylru, proj, xl, m4, *mix_w, brs, brl, xmix, dx1, dproj, rtm, t // rtm)
    big = {
        "w_br_ssd": _matmul_tn(nrm, dbrs, D, D, _tk(nl), "dw_br_ssd"),
        "w_br_lru": _matmul_tn(gl, dbrl, D, D, _tk(nl), "dw_br_lru"),
        "w_out": _matmul_tn(mixb, dxm, D, D, _tk(nl), "dw_out"),
        "w_mlp1": _matmul_tn(h2, da1, D, D, _tk(nl), "dw_mlp1"),
        "w_mlp2": _matmul_tn(r2, dmlp, D, D, _tk(nl), "dw_mlp2"),
    }
    side = reducer.begin_swap(big, list(big)) if reducer else None
    (dproj, ddt_l, dh0f, dh0b, dcw_l, dcb_l, ddtb_l, dal_l, dd, *got) = _ssd_bwd(
        proj, dtg, *ssd_p, lhsf, lhsb, dy, zst, zst, dproj, nb, t, GRID_W, 0, True, side)
    (dproj_c, ddt_c, _, _, dcw_c, dcb_c, ddtb_c, dal_c, _) = _ssd_bwd(
        proj_c, dtg_c, *ssd_p, chsf, chsb, None, dh0f, dh0b, dproj_c, nb, tc, tc, 0, False)
    side = reducer.begin_scatter(got) if reducer else None
    (dproj, dlh0, gcw_l, gcb_l, gwa_l, gwi_l, gba_l, gbi_l, glam_l, *got) = _lru_bwd(
        proj, *lru_p, cfin, lhf, lhb, dylru, zl, dproj, nb, t, GRID_W, 0, True, side)
    if reducer:
        reducer.end(got)
    (dproj_c, _, gcw_c, gcb_c, gwa_c, gwi_c, gba_c, gbi_c, glam_c) = _lru_bwd(
        proj_c, *lru_p, zl, chf, chb, None, dlh0, dproj_c, nb, tc, tc, 0, False)
    pad_dt = lambda d: jnp.pad(_unpack_dt(d).astype(BF16), ((0, 0), (0, P_W - P_DT - 64)))
    dproj = lax.dynamic_update_slice(dproj, pad_dt(ddt_l), (0, P_DT))
    dproj_c = lax.dynamic_update_slice(dproj_c, pad_dt(ddt_c), (0, P_DT))

    big["w_in"] = _matmul_tn2(h1, dproj, h1_c, dproj_c, D, 1152, min(_tk(nl), _tk(ncx)), "dw_in")
    side = reducer.begin_swap(big, ["w_in"]) if reducer else None
    dmc, *got = _inproj_bwd(xc, m4, win, dproj_c, rt, 0, ncx // rt, ncx // rt, nb, False, None, side)
    side = reducer.begin_scatter(got) if reducer else None
    gx, dm1, *got = _inproj_bwd(xl, m4, win, dproj, rt, 0, nl // rt, t // rt, nb, True, dxres, side)
    if reducer:
        reducer.end(got)
    dm = jnp.zeros((8, N_MOD, D), F32)
    dm = dm.at[0:nb].set(jnp.concatenate([dm1, dg1, dm2], axis=1)).at[nb, 0:2].set(dmc[0])
    dwmod, dbmod, dcc = _mod_bwd(cc, wmod, dm.reshape(8, N_MOD * D))
    big["w_mod"] = dwmod
    nq = D // LRU_CB
    small = {
        "c_ctx": dcc[nb],
        "b_mod": dbmod,
        "b_gate": dbg,
        "ssd_conv_w": _unpack_conv(dcw_l + dcw_c),
        "ssd_conv_b": _unpack_conv(dcb_l + dcb_c),
        "ssd_dt_bias": _unpack_heads(ddtb_l + ddtb_c),
        "ssd_a_log": _unpack_heads(dal_l + dal_c),
        "ssd_d": jnp.sum(dd.reshape(32, 64), axis=1),
        "ssd_norm_w": dnw,
        "lru_conv_w": jnp.transpose(gcw_l + gcw_c, (1, 0, 2)).reshape(4, D),
        "lru_conv_b": (gcb_l + gcb_c).reshape(1, D),
        "lru_wa": gwa_l + gwa_c,
        "lru_ba": jnp.transpose(gba_l + gba_c, (1, 0, 2)).reshape(2, D),
        "lru_wi": gwi_l + gwi_c,
        "lru_bi": jnp.transpose(gbi_l + gbi_c, (1, 0, 2)).reshape(2, D),
        "lru_lambda": jnp.transpose(glam_l + glam_c, (1, 0, 2)).reshape(2, D),
        "ln1_g": dl1g, "ln1_b": dl1b, "b_mlp1": db1, "b_mlp2": db2, "ln2_g": dl2g, "ln2_b": dl2b,
    }
    return loss[0, 0], gx.reshape(nb, t, D), big, small


_HBM = BS(memory_space=pl.ANY)


def _place():
    return lax.axis_index("x"), lax.axis_index("y"), lax.axis_index("c")


def _other_chips(x, y):
    return [(1 - x, y), (x, 1 - y), (1 - x, 1 - y)]


def _gather_chips(arrs):
    n = len(arrs)

    def body(*refs):
        ex = _GatherExchange(refs[:n], refs[n:2 * n], refs[2 * n:])
        ex.begin()
        ex.finish()

    return pl.pallas_call(
        body, name="gather_weights", in_specs=[_HBM] * n, out_specs=[_HBM] * n,
        out_shape=_gather_out_shapes(arrs), scratch_shapes=_gather_sems(n),
    )(*arrs)


def _gather_out_shapes(arrs):
    return [S((4,) + a.shape, a.dtype) for a in arrs]


def _gather_sems(n):
    return [pltpu.SemaphoreType.DMA((3 * n,))] * 4 + [pltpu.SemaphoreType.DMA((n,))]


class _GatherExchange:
    def __init__(self, ins, outs, sems):
        self.ins, self.outs = ins, outs
        self.ici_send, self.ici_recv, self.d2d_send, self.d2d_recv, self.loc_sems = sems
        self.x, self.y, self.c = _place()
        self.me = 2 * self.x + self.y
        self.chips = _other_chips(self.x, self.y)

    def _half(self, a, which):
        hr = self.ins[a].shape[0] // 2
        return pl.ds(pl.multiple_of((self.c if which == 0 else 1 - self.c) * hr, 8), hr)

    def _local(self, a):
        return pltpu.make_async_copy(self.ins[a], self.outs[a].at[self.me], self.loc_sems.at[a])

    def _ici(self, a, k, slot):
        px, py = self.chips[k]
        mine = self._half(a, 0)
        return pltpu.make_async_remote_copy(src_ref=self.ins[a].at[mine], dst_ref=self.outs[a].at[slot, mine],
                                            send_sem=self.ici_send.at[3 * a + k], recv_sem=self.ici_recv.at[3 * a + k],
                                            device_id=(px, py, self.c), device_id_type=MESH)

    def _d2d(self, a, k, which):
        px, py = self.chips[k]
        rows = self.outs[a].at[2 * px + py, self._half(a, which)]
        return pltpu.make_async_remote_copy(src_ref=rows, dst_ref=rows, send_sem=self.d2d_send.at[3 * a + k],
                                            recv_sem=self.d2d_recv.at[3 * a + k],
                                            device_id=(self.x, self.y, 1 - self.c), device_id_type=MESH)

    def begin(self):
        for a in range(len(self.ins)):
            self._local(a).start()
            for k in range(3):
                self._ici(a, k, self.me).start()

    def finish(self):
        n = len(self.ins)
        for a in range(n):
            for k, (px, py) in enumerate(self.chips):
                self._ici(a, k, 2 * px + py).wait_recv()
                self._d2d(a, k, 0).start()
        for a in range(n):
            for k in range(3):
                self._d2d(a, k, 1).wait_recv()
        for a in range(n):
            self._local(a).wait()
            for k in range(3):
                self._ici(a, k, self.me).wait_send()
                self._d2d(a, k, 0).wait_send()


def _scatter_chips(arrs):
    side = _Side("scatter", arrs)
    n = len(arrs)

    def body(*refs):
        ex = side.make(refs[:n], refs[n:2 * n], refs[2 * n:])
        ex.begin()
        ex.finish()

    return pl.pallas_call(
        body, name="scatter_grads", in_specs=[_HBM] * n, out_specs=[_HBM] * n,
        out_shape=side.out_shapes, scratch_shapes=side.sems,
    )(*arrs)


class _ScatterExchange:
    def __init__(self, ins, outs, sems):
        self.ins, self.outs = ins, outs
        self.send_sems, self.recv_sems = sems
        x, y, self.c = _place()
        self.chips = _other_chips(x, y)

    def _copy(self, a, k):
        px, py = self.chips[k]
        return pltpu.make_async_remote_copy(src_ref=self.ins[a].at[2 * px + py], dst_ref=self.outs[a].at[k],
                                            send_sem=self.send_sems.at[3 * a + k], recv_sem=self.recv_sems.at[3 * a + k],
                                            device_id=(px, py, self.c), device_id_type=MESH)

    def begin(self):
        for a in range(len(self.ins)):
            for k in range(3):
                self._copy(a, k).start()

    def finish(self):
        for a in range(len(self.ins)):
            for k in range(3):
                self._copy(a, k).wait_recv()
        for a in range(len(self.ins)):
            for k in range(3):
                self._copy(a, k).wait_send()


class _Side:
    def __init__(self, kind, arrays):
        self.arrays = list(arrays)
        n = len(self.arrays)
        if kind == "gather":
            self.out_shapes, self.sems, self.make = _gather_out_shapes(self.arrays), _gather_sems(n), _GatherExchange
        elif kind == "swap":
            self.out_shapes = [S((4, a.shape[1] // 2, a.shape[2]), a.dtype) for a in self.arrays]
            self.sems = [pltpu.SemaphoreType.DMA((4 * n,))] * 2
            self.make = _SwapExchange
        else:
            self.out_shapes = [S((3,) + a.shape[1:], a.dtype) for a in self.arrays]
            self.sems = [pltpu.SemaphoreType.DMA((3 * n,))] * 2
            self.make = _ScatterExchange

    def split(self, refs, n_in, n_out, n_scr):
        a, b = len(self.arrays), len(self.out_shapes)
        i1 = n_in + a
        o1 = i1 + n_out
        o2 = o1 + b
        s1 = o2 + n_scr
        own = tuple(refs[:n_in]) + tuple(refs[i1:o1]) + tuple(refs[o2:s1])
        return own, self.make(refs[n_in:i1], refs[o1:o2], refs[s1:])

    def around(self, exchange, step, n_steps, compute):
        pl.when(step == 0)(exchange.begin)
        compute()
        pl.when(step == n_steps - 1)(exchange.finish)


def _swap_halves(arrs, name):
    side = _Side("swap", arrs)
    n = len(arrs)

    def body(*refs):
        ex = side.make(refs[:n], refs[n:2 * n], refs[2 * n:])
        ex.begin()
        ex.finish()

    return pl.pallas_call(
        body, name=name, in_specs=[_HBM] * n, out_specs=[_HBM] * n, out_shape=side.out_shapes, scratch_shapes=side.sems,
    )(*arrs)


class _SwapExchange:
    def __init__(self, ins, outs, sems):
        self.ins, self.outs = ins, outs
        self.send_sems, self.recv_sems = sems
        self.x, self.y, self.c = _place()

    def _copy(self, a, q):
        hr = self.ins[a].shape[1] // 2
        theirs = pl.ds(pl.multiple_of((1 - self.c) * hr, 8), hr)
        return pltpu.make_async_remote_copy(src_ref=self.ins[a].at[q, theirs], dst_ref=self.outs[a].at[q],
                                            send_sem=self.send_sems.at[4 * a + q], recv_sem=self.recv_sems.at[4 * a + q],
                                            device_id=(self.x, self.y, 1 - self.c), device_id_type=MESH)

    def begin(self):
        for a in range(len(self.ins)):
            for q in range(4):
                self._copy(a, q).start()

    def finish(self):
        for wait in ("wait_recv", "wait_send"):
            for a in range(len(self.ins)):
                for q in range(4):
                    getattr(self._copy(a, q), wait)()


def _allreduce_small(vs, bf16_over_ici):
    n = len(vs)

    def body(*refs):
        ins, outs, bufs, sendb = refs[:n], refs[n:2 * n], refs[2 * n:5 * n], refs[5 * n:6 * n]
        send_sems, recv_sems = refs[6 * n:]
        x, y, c = _place()
        srcs = list(ins)
        for s, peer in enumerate(((x, y, 1 - c), (x, 1 - y, c), (1 - x, y, c))):
            copies = []
            for a in range(n):
                src = srcs[a]
                if s > 0 and bf16_over_ici[a]:
                    sendb[a][...] = src[...].astype(BF16)
                    src = sendb[a]
                copies.append(pltpu.make_async_remote_copy(
                    src_ref=src, dst_ref=bufs[3 * a + s], send_sem=send_sems.at[3 * a + s],
                    recv_sem=recv_sems.at[3 * a + s], device_id=peer, device_id_type=MESH))
                copies[-1].start()
            for a, cp in enumerate(copies):
                cp.wait()
                mine = sendb[a] if s > 0 and bf16_over_ici[a] else srcs[a]
                outs[a][...] = mine[...].astype(F32) + bufs[3 * a + s][...].astype(F32)
            srcs = list(outs)

    vm = BS(memory_space=pltpu.VMEM)
    wire = lambda a, s: BF16 if s > 0 and bf16_over_ici[a] else F32
    return pl.pallas_call(
        body, name="allreduce_small", in_specs=[vm] * n, out_specs=[vm] * n, out_shape=[S(v.shape, F32) for v in vs],
        scratch_shapes=[pltpu.VMEM(v.shape, wire(a, s)) for a, v in enumerate(vs) for s in range(3)]
        + [pltpu.VMEM(v.shape if bf16_over_ici[a] else (16, 128), BF16) for a, v in enumerate(vs)]
        + [pltpu.SemaphoreType.DMA((3 * n,)), pltpu.SemaphoreType.DMA((3 * n,))],
        compiler_params=_params(),
    )(*vs)


def _swap_cores(arrs):
    n = len(arrs)

    def body(*refs):
        ins, outs = refs[:n], refs[n:2 * n]
        send_sems, recv_sems = refs[2 * n:]
        x, y, c = _place()
        sends = []
        for a in range(n):
            cp = pltpu.make_async_remote_copy(src_ref=ins[a], dst_ref=outs[a], send_sem=send_sems.at[a],
                                              recv_sem=recv_sems.at[a], device_id=(x, y, 1 - c), device_id_type=MESH)
            cp.start()
            sends.append(cp)
        for cp in sends:
            cp.wait_recv()
        for cp in sends:
            cp.wait_send()

    return pl.pallas_call(
        body, name="swap_cores", in_specs=[_HBM] * n, out_specs=[_HBM] * n,
        out_shape=[S(a.shape, a.dtype) for a in arrs],
        scratch_shapes=[pltpu.SemaphoreType.DMA((n,)), pltpu.SemaphoreType.DMA((n,))],
    )(*arrs)


def _row_tile(r, c=128):
    tr = 256 if c <= 1024 else (128 if c <= 2048 else 64)
    return tr if r % tr == 0 else r


def _sum_half(own, sib, core, name):
    _, r, c = own.shape
    hr = r // 2
    tr = _row_tile(hr, c)
    nbk = hr // tr

    def body(core_ref, o_ref, s_ref, p_ref, pb_ref):
        p = o_ref[...] + s_ref[...]
        p_ref[...] = p
        pb_ref[...] = p.astype(BF16)

    blk = BS((None, tr, c), lambda q, i, cr: (q, i, 0))
    return pl.pallas_call(
        body, name=name, out_shape=[S((4, hr, c), F32), S((4, hr, c), BF16)],
        grid_spec=pltpu.PrefetchScalarGridSpec(
            num_scalar_prefetch=1, grid=(4, nbk),
            in_specs=[BS((None, tr, c), lambda q, i, cr: (q, cr[0] * nbk + i, 0)), blk], out_specs=[blk, blk]),
        compiler_params=_params(),
    )(core, own, sib)


def _sum4(part, recv, chip, name):
    _, r, c = part.shape
    tr = _row_tile(r, c)

    def body(chip_ref, o_ref, r_ref, out_ref):
        acc = o_ref[...]
        for k in range(3):
            acc = acc + r_ref[k].astype(F32)
        out_ref[...] = acc

    return pl.pallas_call(
        body, name=name, out_shape=S((r, c), F32),
        grid_spec=pltpu.PrefetchScalarGridSpec(
            num_scalar_prefetch=1, grid=(r // tr,),
            in_specs=[BS((None, tr, c), lambda i, ch: (ch[0], i, 0)), BS((3, tr, c), lambda i, ch: (0, i, 0))],
            out_specs=BS((tr, c), lambda i, ch: (i, 0))),
        compiler_params=_params(),
    )(chip, part, recv)


def _adam_math(w, g, m, v):
    m = ADAM_B1 * m + (1.0 - ADAM_B1) * g
    v = ADAM_B2 * v + (1.0 - ADAM_B2) * (g * g)
    m_hat = m / (1.0 - ADAM_B1 ** ADAM_STEP)
    v_hat = v / (1.0 - ADAM_B2 ** ADAM_STEP)
    return -ADAM_LR * (m_hat / (jnp.sqrt(v_hat) + ADAM_EPS) + ADAM_WD * w), m, v


def _adam_halves(mine, other, w, m, v, core, name):
    r, c = w.shape
    tr = _row_tile(r // 2, c)
    nbk = (r // 2) // tr

    def body(core_ref, a_ref, b_ref, w_ref, m_ref, v_ref, g_ref, d_ref, nm_ref, nv_ref):
        g = jnp.where(pl.program_id(0) // nbk == core_ref[0], a_ref[...], b_ref[...])
        g_ref[...] = g
        d_ref[...], nm_ref[...], nv_ref[...] = _adam_math(w_ref[...], g, m_ref[...], v_ref[...])

    spec = BS((tr, c), lambda i, cr: (i, 0))
    half = BS((tr, c), lambda i, cr: (i % nbk, 0))
    return pl.pallas_call(
        body, name=name, out_shape=[S((r, c), F32)] * 4,
        grid_spec=pltpu.PrefetchScalarGridSpec(num_scalar_prefetch=1, grid=(r // tr,), in_specs=[half, half] + [spec] * 3,
                                               out_specs=[spec] * 4),
        compiler_params=_params(),
    )(core, mine, other, w, m, v)


def _adam_flat(g, w, m, v, name):
    r = w.shape[0]
    tr = _row_tile(r)

    def body(g_ref, w_ref, m_ref, v_ref, d_ref, nm_ref, nv_ref):
        d_ref[...], nm_ref[...], nv_ref[...] = _adam_math(w_ref[...], g_ref[...], m_ref[...], v_ref[...])

    spec = BS((tr, 128), lambda i: (i, 0))
    return pl.pallas_call(
        body, grid=(r // tr,), name=name, in_specs=[spec] * 4, out_specs=[spec] * 3,
        out_shape=[S((r, 128), F32)] * 3, compiler_params=_params(),
    )(g, w, m, v)


def _flatten(arrs, rows_mult=256):
    flat = jnp.concatenate([a.reshape(-1) for a in arrs])
    n = flat.shape[0]
    rows = -(-n // 128)
    rows = -(-rows // rows_mult) * rows_mult
    return jnp.pad(flat, (0, rows * 128 - n)).reshape(rows, 128)


def _unflatten(flat, shapes):
    flat = flat.reshape(-1)
    out, o = [], 0
    for shp in shapes:
        n = int(np.prod(shp))
        out.append(flat[o:o + n].reshape(shp))
        o += n
    return out


BIG = ["w_mod", "w_in", "w_br_ssd", "w_br_lru", "w_out", "w_mlp1", "w_mlp2"]
COL_SHARDED = {"w_mod": N_MOD * D, "w_in": IN_COLS, "w_mlp1": MLP_H}
SMALL_SHARDED = ["ssd_conv_w", "lru_conv_w", "lru_ba", "lru_bi", "lru_lambda"]
WEIGHTS = ['c_ctx', 'w_mod', 'b_mod', 'w_in', 'b_gate', 'ssd_conv_w', 'ssd_conv_b', 'ssd_dt_bias', 'ssd_a_log', 'ssd_d',
           'ssd_norm_w', 'lru_conv_w', 'lru_conv_b', 'lru_wa', 'lru_ba', 'lru_wi', 'lru_bi', 'lru_lambda', 'w_br_ssd',
           'w_br_lru', 'w_out', 'ln1_g', 'ln1_b', 'w_mlp1', 'b_mlp1', 'w_mlp2', 'b_mlp2', 'ln2_g', 'ln2_b']
SMALL = [n for n in WEIGHTS if n not in BIG]
GATE_STACKS = ["lru_wa", "lru_wi"]


class _Reducer:
    def __init__(self, core_id, chip_id):
        self.core_id, self.chip_id = core_id, chip_id
        self.halves, self.recv, self.pending = {}, {}, []

    def _slabs(self, big, names):
        return [_chips_from_full(_unpack_win(big[n]) if n == "w_in" else big[n], n) for n in names]

    def _chip_sums(self, names, slabs, sibling):
        for n, s, o in zip(names, slabs, sibling):
            self.halves[n] = _sum_half(s, o, self.core_id, "half_" + n)
        return [self.halves[n][1] for n in names]

    def begin_swap(self, big, names):
        self.pending, self.slabs = list(names), self._slabs(big, names)
        return _Side("swap", self.slabs)

    def begin_scatter(self, sibling):
        return _Side("scatter", self._chip_sums(self.pending, self.slabs, sibling))

    def end(self, received):
        self.recv.update(zip(self.pending, received))

    def finish(self, big, names):
        slabs = self._slabs(big, names)
        sums = self._chip_sums(names, slabs, _swap_halves(slabs, "swap_halves_" + names[0]))
        self.recv.update(zip(names, _scatter_chips(sums)))
        mine = [_sum4(self.halves[n][0], self.recv[n], self.chip_id, "sum_" + n) for n in BIG]
        return dict(zip(BIG, mine)), dict(zip(BIG, _swap_cores(mine)))


def _full_from_chips(g4, name):
    if name in COL_SHARDED:
        return jnp.transpose(g4, (1, 0, 2)).reshape(g4.shape[1], 4 * g4.shape[2])
    return g4.reshape(4 * g4.shape[1], g4.shape[2])


def _chips_from_full(full, name):
    if name in COL_SHARDED:
        r, c = full.shape
        return jnp.transpose(full.reshape(r, 4, c // 4), (1, 0, 2))
    return full.reshape(4, full.shape[0] // 4, full.shape[1])


def kernel(x, c, ctx, c_ctx, w_mod, b_mod, w_in, b_gate, ssd_conv_w, ssd_conv_b, ssd_dt_bias, ssd_a_log, ssd_d, ssd_norm_w, lru_conv_w, lru_conv_b, lru_wa, lru_ba, lru_wi, lru_bi, lru_lambda, w_br_ssd, w_br_lru, w_out, ln1_g, ln1_b, w_mlp1, b_mlp1, w_mlp2, b_mlp2, ln2_g, ln2_b, loss_target, m_c_ctx, m_w_mod, m_b_mod, m_w_in, m_b_gate, m_ssd_conv_w, m_ssd_conv_b, m_ssd_dt_bias, m_ssd_a_log, m_ssd_d, m_ssd_norm_w, m_lru_conv_w, m_lru_conv_b, m_lru_wa, m_lru_ba, m_lru_wi, m_lru_bi, m_lru_lambda, m_w_br_ssd, m_w_br_lru, m_w_out, m_ln1_g, m_ln1_b, m_w_mlp1, m_b_mlp1, m_w_mlp2, m_b_mlp2, m_ln2_g, m_ln2_b, v_c_ctx, v_w_mod, v_b_mod, v_w_in, v_b_gate, v_ssd_conv_w, v_ssd_conv_b, v_ssd_dt_bias, v_ssd_a_log, v_ssd_d, v_ssd_norm_w, v_lru_conv_w, v_lru_conv_b, v_lru_wa, v_lru_ba, v_lru_wi, v_lru_bi, v_lru_lambda, v_w_br_ssd, v_w_br_lru, v_w_out, v_ln1_g, v_ln1_b, v_w_mlp1, v_b_mlp1, v_w_mlp2, v_b_mlp2, v_ln2_g, v_ln2_b):
    given = dict(locals())
    w = {n: given[n] for n in WEIGHTS}
    mom = {n: given["m_" + n] for n in WEIGHTS}
    var = {n: given["v_" + n] for n in WEIGHTS}
    chip = 2 * lax.axis_index("x") + lax.axis_index("y")

    shard2d = {n: w[n].reshape(w[n].shape[-2:]) for n in BIG}
    small_pack = _flatten([w[n] for n in SMALL_SHARDED], rows_mult=16)
    first = ["w_mod", "w_in"]
    gathered = _gather_chips([shard2d[n].astype(BF16) for n in first] + [small_pack])
    full = {n: _full_from_chips(g, n) for n, g in zip(first, gathered[:-1])}
    full["w_in"] = _pack_win(full["w_in"])
    per_chip = [_unflatten(gathered[-1][q], [w[n].shape for n in SMALL_SHARDED]) for q in range(4)]
    sm = {n: jnp.concatenate([per_chip[q][i] for q in range(4)], axis=-1) for i, n in enumerate(SMALL_SHARDED)}
    for n in SMALL:
        if n not in sm:
            sm[n] = w[n]
    sm = {n: (a.reshape(a.shape[1:]) if a.ndim >= 3 else a) for n, a in sm.items()}

    core_id = lax.axis_index("c").astype(jnp.int32).reshape(1)
    reducer = _Reducer(core_id, chip.astype(jnp.int32).reshape(1))
    loss, gx, gbig, gsmall = _local_step(x, c, ctx, loss_target, sm, full["w_mod"], full["w_in"],
                                         [shard2d[n].astype(BF16) for n in LATE], late_are_shards=True, reducer=reducer)
    mine, other = reducer.finish(gbig, ["w_mod"])
    out = {}
    for n in BIG:
        shp = shard2d[n].shape
        res = _adam_halves(mine[n], other[n], shard2d[n], mom[n].reshape(shp), var[n].reshape(shp), core_id, "adam_" + n)
        out[n] = [r.reshape(w[n].shape) for r in res]

    tiny = [n for n in SMALL if n not in GATE_STACKS]
    as_rows = lambda a: a.reshape(-1, 128)
    summed = _allreduce_small([_flatten([gsmall[n] for n in tiny])] + [as_rows(gsmall[n]) for n in GATE_STACKS],
                              [False] + [True] * len(GATE_STACKS))
    gs = {}
    for n, g in zip(tiny, _unflatten(summed[0], [gsmall[n].shape for n in tiny])):
        if n in SMALL_SHARDED:
            width = w[n].shape[-1]
            g = lax.dynamic_slice_in_dim(g, chip * width, width, axis=g.ndim - 1)
        gs[n] = g.reshape(w[n].shape)
    shapes = [w[n].shape for n in tiny]
    d_s, m_s, v_s = _adam_flat(_flatten([gs[n] for n in tiny]), _flatten([w[n] for n in tiny]),
                               _flatten([mom[n] for n in tiny]), _flatten([var[n] for n in tiny]), "adam_small")
    for n, d_, m_, v_ in zip(tiny, _unflatten(d_s, shapes), _unflatten(m_s, shapes), _unflatten(v_s, shapes)):
        out[n] = [gs[n], d_, m_, v_]
    for n, g in zip(GATE_STACKS, summed[1:]):
        res = _adam_flat(g, as_rows(w[n]), as_rows(mom[n]), as_rows(var[n]), "adam_" + n)
        out[n] = [r.reshape(w[n].shape) for r in (g, *res)]

    loss = lax.psum(loss, ("x", "y", "c"))
    return (loss, gx, *[out[n][0] for n in WEIGHTS], *[out[n][1] for n in WEIGHTS], *[out[n][2] for n in WEIGHTS],
            *[out[n][3] for n in WEIGHTS])
```

```python
import functools

import numpy as np
import jax
import jax.numpy as jnp
from jax import lax
from jax.experimental import pallas as pl
from jax.experimental.pallas import tpu as pltpu

F32, BF16 = jnp.float32, jnp.bfloat16
S = jax.ShapeDtypeStruct
BS = pl.BlockSpec
MESH = pl.DeviceIdType.MESH

D = 1024
GRID_W = 64
SSD_INNER, SSD_G, SSD_N, SSD_L = 2048, 8, 128, 128
SSD_GW = 512
MLP_H = 4096
N_MOD = 6
ALPHA = 2.0 ** 0.25
LN_EPS, RMS_EPS = 1e-6, 1e-5
LRU_C = 8.0
P_XBC, P_LRU, P_Z, P_LG, P_MG, P_DT, P_W = 0, 4096, 5120, 7168, 8192, 10240, 10368
P_CB = 3456
IN_COLS = 10304
LRU_CB = 256
ADAM_LR, ADAM_B1, ADAM_B2, ADAM_EPS, ADAM_WD, ADAM_STEP = 0.001, 0.9, 0.999, 1e-08, 0.01, 10
VMEM_LIMIT = 56 * 2 ** 20


def _params(**kw):
    return pltpu.CompilerParams(vmem_limit_bytes=VMEM_LIMIT, **kw)


def _dot(a, b):
    return jnp.dot(a.astype(BF16), b.astype(BF16), preferred_element_type=F32)


def _dot_nt(a, b):
    return lax.dot_general(a.astype(BF16), b.astype(BF16), (((1,), (1,)), ((), ())), preferred_element_type=F32)


def _dot_tn(a, b):
    return lax.dot_general(a.astype(BF16), b.astype(BF16), (((0,), (0,)), ((), ())), preferred_element_type=F32)


@jax.custom_vjp
def _mm(a, b):
    return _dot(a, b)


def _cast_pair(a, b):
    return a.astype(BF16), b.astype(BF16)


def _mm_f(a, b):
    r = _cast_pair(a, b)
    return _dot(*r), r


def _mm_b(r, g):
    g = g.astype(BF16)
    return _dot_nt(g, r[1]), _dot_tn(r[0], g)


_mm.defvjp(_mm_f, _mm_b)


@jax.custom_vjp
def _mm_nt(a, b):
    return _dot_nt(a, b)


def _mm_nt_f(a, b):
    r = _cast_pair(a, b)
    return _dot_nt(*r), r


def _mm_nt_b(r, g):
    g = g.astype(BF16)
    return _dot(g, r[1]), _dot_tn(g, r[0])


_mm_nt.defvjp(_mm_nt_f, _mm_nt_b)


@jax.custom_vjp
def _mm_tn(a, b):
    return _dot_tn(a, b)


def _mm_tn_f(a, b):
    r = _cast_pair(a, b)
    return _dot_tn(*r), r


def _mm_tn_b(r, g):
    g = g.astype(BF16)
    return _dot_nt(r[1], g), _dot(r[0], g)


_mm_tn.defvjp(_mm_tn_f, _mm_tn_b)

def _split3(v):
    h = v.astype(BF16)
    r = v - h.astype(F32)
    m = r.astype(BF16)
    return h, m, (r - m.astype(F32)).astype(BF16)


def _sel_dot(sel, v, dims):
    sel_first = dims[0] == "s"
    dn = {"sv": (((1,), (0,)), ((), ())), "sTv": (((0,), (0,)), ((), ())), "vs": (((1,), (0,)), ((), ())),
          "vsT": (((1,), (1,)), ((), ()))}[dims]
    out = None
    for part in _split3(v):
        a, b = (sel, part) if sel_first else (part, sel)
        term = lax.dot_general(a, b, dn, preferred_element_type=F32)
        out = term if out is None else out + term
    return out


@jax.custom_vjp
def _cum_mm(tri, v):
    return _sel_dot(tri, v, "sv")


_cum_mm.defvjp(lambda tri, v: (_sel_dot(tri, v, "sv"), tri),
               lambda tri, g: (jnp.zeros_like(tri), _sel_dot(tri, g, "sTv")))


@jax.custom_vjp
def _xp_mm(v, e):
    return _sel_dot(e, v, "vs")


_xp_mm.defvjp(lambda v, e: (_sel_dot(e, v, "vs"), e),
              lambda e, g: (_sel_dot(e, g, "vsT"), jnp.zeros_like(e)))


def _sigmoid(x):
    return 0.5 * jnp.tanh(0.5 * x) + 0.5


def _silu(x):
    return x * _sigmoid(x)


def _dsilu(x):
    s = _sigmoid(x)
    return s * (1.0 + x * (1.0 - s))


def _softplus(x):
    return jnp.maximum(x, 0.0) + jnp.log1p(jnp.exp(-jnp.abs(x)))


def _gelu(x):
    return 0.5 * x * (1.0 + jnp.tanh(0.7978845608028654 * (x + 0.044715 * x * x * x)))


def _ln(x):
    mu = jnp.mean(x, axis=-1, keepdims=True)
    xc = x - mu
    var = jnp.mean(xc * xc, axis=-1, keepdims=True)
    return xc * lax.rsqrt(var + LN_EPS)


def _modln(x, shift, scale):
    return _ln(x) * (1.0 + scale) + shift


def _resln(x, sub, gate, g, b):
    return _ln(ALPHA * x + gate * sub) * g + b


def _grms(y, z, w):
    u = y * _silu(z)
    return u * lax.rsqrt(jnp.mean(u * u, axis=-1, keepdims=True) + RMS_EPS) * w


def _colsum(v):
    return jnp.sum(v, axis=0, keepdims=True)


def _conv_taps(width, period):
    def masks(rows):
        pos = lax.broadcasted_iota(jnp.int32, (rows, width), 0)
        if rows != period:
            pos = pos & (period - 1)
        return [pos >= 2 - k if k < 2 else pos < period + 2 - k for k in range(4)]
    return masks


def _conv_fwd(raw, w, b, masks):
    rows = raw.shape[0]
    pre = b + raw * w[2:3, :]
    for k in (0, 1, 3):
        sh = pltpu.roll(raw, (2 - k) % rows, 0)
        pre = pre + jnp.where(masks[k], sh, 0.0) * w[k:k + 1, :]
    return pre


def _for_rows(t, rb, fn):
    n = t // rb
    unroll = 4 if n % 4 == 0 else 1

    def step(i, carry):
        for u in range(unroll):
            fn(pl.multiple_of((i * unroll + u) * rb, rb))
        return carry

    lax.fori_loop(0, n // unroll, step, 0)


def _loop_unrolled(n, unroll, body, init):
    def step(i, carry):
        for u in range(unroll):
            carry = body(i * unroll + u, carry)
        return carry

    return lax.fori_loop(0, n // unroll, step, init)


def _conv_bwd(dpre, raw, w, masks):
    rows = raw.shape[0]
    draw = dpre * w[2:3, :]
    dws = []
    for k in range(4):
        if k == 2:
            dws.append(_colsum(dpre * raw))
            continue
        back = pltpu.roll(jnp.where(masks[k], dpre, 0.0), (k - 2) % rows, 0)
        dws.append(_colsum(back * raw))
        draw = draw + back * w[k:k + 1, :]
    return draw, jnp.concatenate(dws, axis=0), _colsum(dpre)


def _mod_fwd(cc, wmod, bmod):
    def body(cc_ref, w_ref, b_ref, o_ref):
        o_ref[...] = _dot(_silu(cc_ref[...]), w_ref[...]) + b_ref[...]

    return pl.pallas_call(
        body, grid=(N_MOD,), name="mod_fwd",
        in_specs=[BS((8, D), lambda j: (0, 0)), BS((D, D), lambda j: (0, j)), BS((1, D), lambda j: (0, j))],
        out_specs=BS((8, D), lambda j: (0, j)), out_shape=S((8, N_MOD * D), F32), compiler_params=_params(),
    )(cc, wmod, bmod)


def _mod_bwd(cc, wmod, dm):
    def body(cc_ref, w_ref, dm_ref, dw_ref, db_ref, dcc_ref):
        j = pl.program_id(0)
        c = cc_ref[...]
        dmv = dm_ref[...]
        dw_ref[...] = _dot_tn(_silu(c), dmv)
        db_ref[...] = _colsum(dmv)

        @pl.when(j == 0)
        def _():
            dcc_ref[...] = jnp.zeros_like(dcc_ref)

        dcc_ref[...] += _dot_nt(dmv, w_ref[...]) * _dsilu(c)

    return pl.pallas_call(
        body, grid=(N_MOD,), name="mod_bwd",
        in_specs=[BS((8, D), lambda j: (0, 0)), BS((D, D), lambda j: (0, j)), BS((8, D), lambda j: (0, j))],
        out_specs=[BS((D, D), lambda j: (0, j)), BS((1, D), lambda j: (0, j)), BS((8, D), lambda j: (0, 0))],
        out_shape=[S((D, N_MOD * D), F32), S((1, N_MOD * D), F32), S((8, D), F32)], compiler_params=_params(),
    )(cc, wmod, dm)


def _inproj_fwd(xa, m4, win, rt, n_lat_tiles, tiles_per_b, ctx_row, name):
    n_tiles = xa.shape[0] // rt

    def mrow(i):
        return jnp.where(i < n_lat_tiles, i // tiles_per_b, ctx_row)

    def body(x_ref, sh_ref, sc_ref, w_hbm, p_ref, h_ref, w_vm, sem):
        @pl.when(pl.program_id(0) == 0)
        def _():
            cp = pltpu.make_async_copy(w_hbm, w_vm, sem)
            cp.start()
            cp.wait()

        hb = _modln(x_ref[...], sh_ref[...], sc_ref[...]).astype(BF16)
        h_ref[...] = hb
        for j in range(P_W // P_CB):
            sl = slice(j * P_CB, (j + 1) * P_CB)
            p_ref[:, sl] = jnp.dot(hb, w_vm[:, sl], preferred_element_type=F32)

    return pl.pallas_call(
        body, grid=(n_tiles,), name=name,
        in_specs=[BS((rt, D), lambda i: (i, 0)),
                  BS((None, None, 1, D), lambda i: (mrow(i), 0, 0, 0)),
                  BS((None, None, 1, D), lambda i: (mrow(i), 1, 0, 0)),
                  BS(memory_space=pl.ANY)],
        out_specs=[BS((rt, P_W), lambda i: (i, 0)), BS((rt, D), lambda i: (i, 0))],
        out_shape=[S((xa.shape[0], P_W), F32), S((xa.shape[0], D), BF16)],
        scratch_shapes=[pltpu.VMEM((D, P_W), BF16), pltpu.SemaphoreType.DMA(())], compiler_params=_params(),
    )(xa, m4, m4, win)


def _inproj_bwd(xa, m4, win, dproj, rt, tile0, n_tiles, tiles_per_b, ctx_row, latent, dxres, side=None):
    def mrow(i):
        return (i // tiles_per_b) if latent else ctx_row

    def body(*refs):
        if side is None:
            return compute(*refs)
        own, exchange = side.split(refs, len(args), len(out_shape), 2)
        side.around(exchange, pl.program_id(0), n_tiles, lambda: compute(*own))

    def compute(x_ref, sh_ref, sc_ref, dp_ref, w_hbm, *rest):
        if latent:
            dxr_ref, gx_ref, dm_ref, w_vm, sem = rest
        else:
            dm_ref, w_vm, sem = rest
        i = pl.program_id(0)

        @pl.when(i == 0)
        def _():
            cp = pltpu.make_async_copy(w_hbm, w_vm, sem)
            cp.start()
            cp.wait()

        dh = lax.dot_general(dp_ref[...], w_vm[...], (((1,), (1,)), ((), ())), preferred_element_type=F32)
        _, vjp = jax.vjp(_modln, x_ref[...], sh_ref[...], sc_ref[...])
        dx, dsh, dsc = vjp(dh)
        if latent:
            gx_ref[...] = dx + dxr_ref[...]

        @pl.when(i % tiles_per_b == 0)
        def _():
            dm_ref[...] = jnp.zeros_like(dm_ref)

        dm_ref[0:1, :] += dsh
        dm_ref[1:2, :] += dsc

    nb = n_tiles // tiles_per_b
    in_specs = [BS((rt, D), lambda i: (tile0 + i, 0)),
                BS((None, None, 1, D), lambda i: (mrow(i), 0, 0, 0)),
                BS((None, None, 1, D), lambda i: (mrow(i), 1, 0, 0)),
                BS((rt, P_W), lambda i: (tile0 + i, 0)),
                BS(memory_space=pl.ANY)]
    args = [xa, m4, m4, dproj, win]
    dm_spec = BS((None, 2, D), lambda i: (i // tiles_per_b, 0, 0))
    if latent:
        in_specs.append(BS((rt, D), lambda i: (i, 0)))
        args.append(dxres)
        out_specs = [BS((rt, D), lambda i: (i, 0)), dm_spec]
        out_shape = [S((n_tiles * rt, D), F32), S((nb, 2, D), F32)]
    else:
        out_specs = [dm_spec]
        out_shape = [S((nb, 2, D), F32)]
    extra = side.arrays if side else []
    return pl.pallas_call(
        body, grid=(n_tiles,), name="inproj_bwd_lat" if latent else "inproj_bwd_ctx",
        in_specs=in_specs + [_HBM] * len(extra), out_specs=out_specs + [_HBM] * len(extra),
        out_shape=out_shape + (side.out_shapes if side else []),
        scratch_shapes=[pltpu.VMEM((D, P_W), BF16), pltpu.SemaphoreType.DMA(())] + (side.sems if side else []),
        compiler_params=_params(),
    )(*args, *extra)


def _matmul_tn(a, b, tm, tn, tk, name):
    k, m = a.shape
    n = b.shape[1]

    def body(a_ref, b_ref, o_ref):
        @pl.when(pl.program_id(2) == 0)
        def _():
            o_ref[...] = jnp.zeros_like(o_ref)

        o_ref[...] += lax.dot_general(a_ref[...], b_ref[...], (((0,), (0,)), ((), ())), preferred_element_type=F32)

    return pl.pallas_call(
        body, grid=(m // tm, n // tn, k // tk), name=name,
        in_specs=[BS((tk, tm), lambda i, j, kk: (kk, i)), BS((tk, tn), lambda i, j, kk: (kk, j))],
        out_specs=BS((tm, tn), lambda i, j, kk: (i, j)), out_shape=S((m, n), F32), compiler_params=_params(),
    )(a, b)


def _matmul_tn2(a1, b1, a2, b2, tm, tn, tk, name):
    k1, m = a1.shape
    n = b1.shape[1]
    n1, n2 = k1 // tk, a2.shape[0] // tk

    def body(a1_ref, b1_ref, a2_ref, b2_ref, o_ref):
        kk = pl.program_id(2)

        @pl.when(kk == 0)
        def _():
            o_ref[...] = jnp.zeros_like(o_ref)

        @pl.when(kk < n1)
        def _():
            o_ref[...] += lax.dot_general(a1_ref[...], b1_ref[...], (((0,), (0,)), ((), ())), preferred_element_type=F32)

        @pl.when(kk >= n1)
        def _():
            o_ref[...] += lax.dot_general(a2_ref[...], b2_ref[...], (((0,), (0,)), ((), ())), preferred_element_type=F32)

    first = lambda kk: jnp.minimum(kk, n1 - 1)
    second = lambda kk: jnp.maximum(kk - n1, 0)
    return pl.pallas_call(
        body, grid=(m // tm, n // tn, n1 + n2), name=name,
        in_specs=[BS((tk, tm), lambda i, j, kk: (first(kk), i)), BS((tk, tn), lambda i, j, kk: (first(kk), j)),
                  BS((tk, tm), lambda i, j, kk: (second(kk), i)), BS((tk, tn), lambda i, j, kk: (second(kk), j))],
        out_specs=BS((tm, tn), lambda i, j, kk: (i, j)), out_shape=S((m, n), F32), compiler_params=_params(),
    )(a1, b1, a2, b2)


def _ssd_consts(heads_per_tile):
    n = SSD_L
    ii = lax.broadcasted_iota(jnp.int32, (n, n), 0)
    jj = lax.broadcasted_iota(jnp.int32, (n, n), 1)
    er = lax.broadcasted_iota(jnp.int32, (128, 256), 0)
    ec = lax.broadcasted_iota(jnp.int32, (128, 256), 1) >> 6
    lane = lax.broadcasted_iota(jnp.int32, (1, 128 * heads_per_tile), 1) >> 6
    per_dir = []
    for d in (0, 1):
        mask = (jj >= ii) if d else (jj <= ii)
        per_dir.append((mask, mask.astype(BF16), (er == ec + 4 * d).astype(BF16)))
    return per_dir, [(lane == h).astype(F32) for h in range(2 * heads_per_tile)]


def _ssd_chunk(x, bm, cm, dtc, dtx, alog, hst, consts, hmasks, rev):
    n = SSD_L
    mask, tri, e = consts
    cum = _cum_mm(tri, dtc * (-jnp.exp(alog)))
    cum_x = _xp_mm(cum, e)
    tot_x = cum_x[0:1, :] if rev else cum_x[n - 1:n, :]
    xd = x * dtx
    hn = jnp.exp(tot_x) * hst + _mm_tn(bm, xd * jnp.exp(tot_x - cum_x))
    if cm is None:
        return hn
    cum_t = cum.T
    cb = _mm_nt(cm, bm)
    if len(hmasks) == 4:
        y = jnp.exp(cum_x) * _mm(cm, hst)
        for h in range(4):
            k = 4 * rev + h
            decay = jnp.exp(jnp.where(mask, cum[:, k:k + 1] - cum_t[k:k + 1, :], -1e30))
            y = y + _mm(cb * decay, xd * hmasks[h])
        return y, hn
    pairs = []
    for p in range(2):
        xdp = xd[:, 128 * p:128 * p + 128]
        yp = None
        for hh in range(2):
            k = 4 * rev + 2 * p + hh
            decay = jnp.exp(jnp.where(mask, cum[:, k:k + 1] - cum_t[k:k + 1, :], -1e30))
            term = _mm(cb * decay, xdp * hmasks[hh])
            yp = term if yp is None else yp + term
        pairs.append(yp)
    return jnp.exp(cum_x) * _mm(cm, hst) + jnp.concatenate(pairs, axis=1), hn


def _ssd_fwd(proj, dtg, cw, cb, dtb, alog, drow, h0f, h0b, nb, t, period, blk0, need_y, gather=()):
    nc = t // SSD_L
    rb = period
    assert t % rb == 0
    unroll = 4 if nc % 4 == 0 else (2 if nc % 2 == 0 else 1)
    masks_of = _conv_taps(SSD_GW, period)
    ng = len(gather)
    n_out = 5 if need_y else 4

    def body(p_ref, dt_ref, cw_ref, cb_ref, dtb_ref, al_ref, d_ref, h0f_ref, h0b_ref, *rest):
        g_ins, rest = rest[:ng], rest[ng:]
        outs, g_outs, (act, dts, dtxs), g_sems = rest[:n_out], rest[n_out:n_out + ng], rest[n_out + ng:n_out + ng + 3], \
            rest[n_out + ng + 3:]
        if need_y:
            y_ref, hsf_ref, hsb_ref, sf_ref, sb_ref = outs
        else:
            hsf_ref, hsb_ref, sf_ref, sb_ref = outs
        if ng:
            exchange = _GatherExchange(g_ins, g_outs, g_sems)
            step = pl.program_id(0) * SSD_G + pl.program_id(1)
            pl.when(step == 0)(exchange.begin)
        masks = masks_of(rb)
        per_dir, hmasks = _ssd_consts(2)

        def prologue(r0):
            rows = pl.ds(r0, rb)
            a = _silu(_conv_fwd(p_ref[rows, :], cw_ref[...], cb_ref[...], masks))
            act[rows, :] = a
            if need_y:
                y_ref[rows, :] = d_ref[...] * a[:, 0:256]
            dtv = _softplus(dt_ref[rows, :] + dtb_ref[...])
            dts[rows, :] = dtv
            for d in (0, 1):
                dtxs[rows, 256 * d:256 * d + 256] = _xp_mm(dtv, per_dir[d][2])

        _for_rows(t, rb, prologue)
        al = al_ref[...]

        def chunk(ci, carry):
            out = []
            for d, hst, hs_ref in ((0, carry[0], hsf_ref), (1, carry[1], hsb_ref)):
                c = (nc - 1 - ci) if d else ci
                r0 = pl.multiple_of(c * SSD_L, SSD_L)
                a = act[pl.ds(r0, SSD_L), :]
                hs_ref[c] = hst
                res = _ssd_chunk(a[:, 0:256], a[:, 256:384], a[:, 384:512] if need_y else None,
                                 dts[pl.ds(r0, SSD_L), :], dtxs[pl.ds(r0, SSD_L), 256 * d:256 * d + 256], al, hst,
                                 per_dir[d], hmasks, d)
                if need_y:
                    y_ref[pl.ds(r0, SSD_L), :] += res[0]
                    res = res[1]
                out.append(res)
            return tuple(out)

        sf_ref[...], sb_ref[...] = _loop_unrolled(nc, unroll, chunk, (h0f_ref[...], h0b_ref[...]))
        if ng:
            pl.when(step == nb * SSD_G - 1)(exchange.finish)

    gspec = lambda shp: BS((None,) + shp, lambda b, g: (g,) + (0,) * len(shp))
    st_spec = BS((None, None, SSD_N, 256), lambda b, g: (b, g, 0, 0))
    hs_spec = BS((None, None, nc, SSD_N, 256), lambda b, g: (b, g, 0, 0, 0))
    in_specs = [BS((t, SSD_GW), lambda b, g: (blk0 + b, g)), BS((None, t, 128), lambda b, g: (g, blk0 + b, 0)),
                gspec((4, SSD_GW)), gspec((1, SSD_GW)), gspec((1, 128)), gspec((1, 128)), gspec((1, 256)),
                st_spec, st_spec]
    out_specs = [hs_spec, hs_spec, st_spec, st_spec]
    out_shape = [S((nb, SSD_G, nc, SSD_N, 256), F32)] * 2 + [S((nb, SSD_G, SSD_N, 256), F32)] * 2
    if need_y:
        out_specs = [BS((t, 256), lambda b, g: (b, g))] + out_specs
        out_shape = [S((nb * t, SSD_INNER), F32)] + out_shape
    return pl.pallas_call(
        body, grid=(nb, SSD_G), name="ssd_fwd_lat" if need_y else "ssd_fwd_ctx",
        in_specs=in_specs + [_HBM] * ng, out_specs=out_specs + [_HBM] * ng,
        out_shape=out_shape + _gather_out_shapes(gather),
        scratch_shapes=[pltpu.VMEM((t, SSD_GW), F32), pltpu.VMEM((t, 128), F32), pltpu.VMEM((t, SSD_GW), F32)]
        + (_gather_sems(ng) if ng else []),
        compiler_params=_params(),
    )(proj, dtg, cw, cb, dtb, alog, drow, h0f, h0b, *gather)


def _ssd_bwd(proj, dtg, cw, cb, dtb, alog, drow, hsf, hsb, dy, dsf, dsb, dproj, nb, t, period, blk0, need_y, side=None):
    nc = t // SSD_L
    rb = period
    assert t % rb == 0
    unroll = 2 if nc % 2 == 0 else 1
    masks_of = _conv_taps(SSD_GW, period)

    def body(*refs):
        if side is None:
            return compute(*refs)
        own, exchange = side.split(refs, len(args), len(out_shape), 5)
        side.around(exchange, pl.program_id(0) * SSD_G + pl.program_id(1), nb * SSD_G, lambda: compute(*own))

    def compute(*refs):
        if need_y:
            (p_ref, dt_ref, cw_ref, cb_ref, dtb_ref, al_ref, d_ref, hsf_ref, hsb_ref, dy_ref, dsf_ref, dsb_ref, _,
             dp_ref, ddt_ref, dhf_ref, dhb_ref, dcw_ref, dcb_ref, ddtb_ref, dal_ref, dd_ref,
             pre, dact, dts, ddts, dtxs) = refs
        else:
            (p_ref, dt_ref, cw_ref, cb_ref, dtb_ref, al_ref, d_ref, hsf_ref, hsb_ref, dsf_ref, dsb_ref, _,
             dp_ref, ddt_ref, dhf_ref, dhb_ref, dcw_ref, dcb_ref, ddtb_ref, dal_ref, dd_ref,
             pre, dact, dts, ddts, dtxs) = refs
            dy_ref = None
        b, g = pl.program_id(0), pl.program_id(1)

        @pl.when(jnp.logical_and(b == 0, g == 0))
        def _():
            for r in (dcw_ref, dcb_ref, ddtb_ref, dal_ref, dd_ref):
                r[...] = jnp.zeros_like(r)

        masks = masks_of(rb)
        per_dir, hmasks = _ssd_consts(1)

        def prologue(r0):
            rows = pl.ds(r0, rb)
            pre[rows, :] = _conv_fwd(p_ref[rows, :], cw_ref[...], cb_ref[...], masks)
            dtv = _softplus(dt_ref[rows, :] + dtb_ref[...])
            dts[rows, :] = dtv
            for d in (0, 1):
                dtxs[rows, 256 * d:256 * d + 256] = _xp_mm(dtv, per_dir[d][2])
            dact[rows, :] = jnp.zeros((rb, SSD_GW), F32)
            ddts[rows, :] = jnp.zeros((rb, 128), F32)

        _for_rows(t, rb, prologue)
        al = al_ref[...]
        def chunk(ci, carry):
            dal_c = carry[2]
            dhs_out = []
            for d, dh, hs_ref in ((0, carry[0], hsf_ref), (1, carry[1], hsb_ref)):
                c = ci if d else (nc - 1 - ci)
                r0 = pl.multiple_of(c * SSD_L, SSD_L)
                a = _silu(pre[pl.ds(r0, SSD_L), :])
                dtc = dts[pl.ds(r0, SSD_L), :]
                dtx = dtxs[pl.ds(r0, SSD_L), 256 * d:256 * d + 256]
                if need_y:
                    fn = lambda x_, bm_, cm_, dt_, dx_, al_, hs_: _ssd_chunk(x_, bm_, cm_, dt_, dx_, al_, hs_, per_dir[d],
                                                                             hmasks, d)
                    _, vjp = jax.vjp(fn, a[:, 0:256], a[:, 256:384], a[:, 384:512], dtc, dtx, al, hs_ref[c])
                    dx, dbm, dcm, ddtc, ddtx, dal_k, dhs = vjp((dy_ref[pl.ds(r0, SSD_L), :], dh))
                    dact[pl.ds(r0, SSD_L), 384:512] += dcm
                else:
                    fn = lambda x_, bm_, dt_, dx_, al_, hs_: _ssd_chunk(x_, bm_, None, dt_, dx_, al_, hs_, per_dir[d],
                                                                        hmasks, d)
                    _, vjp = jax.vjp(fn, a[:, 0:256], a[:, 256:384], dtc, dtx, al, hs_ref[c])
                    dx, dbm, ddtc, ddtx, dal_k, dhs = vjp(dh)
                dact[pl.ds(r0, SSD_L), 0:256] += dx
                dact[pl.ds(r0, SSD_L), 256:384] += dbm
                ddts[pl.ds(r0, SSD_L), :] += ddtc + _dot_nt(ddtx, per_dir[d][2])
                dhs_out.append(dhs)
                dal_c = dal_c + dal_k
            return dhs_out[0], dhs_out[1], dal_c

        dhf_ref[...], dhb_ref[...], dal_acc = _loop_unrolled(
            nc, unroll, chunk, (dsf_ref[...], dsb_ref[...], jnp.zeros((1, 128), F32)))

        def epilogue(r0):
            rows = pl.ds(r0, rb)
            prev = pre[rows, :]
            if need_y:
                dyv = dy_ref[rows, :]
                dact[rows, 0:256] += d_ref[...] * dyv
                dd_ref[g] += _colsum(dyv * _silu(prev[:, 0:256]))
            dpre = dact[rows, :] * _dsilu(prev)
            draw, dcw, dcb = _conv_bwd(dpre, p_ref[rows, :], cw_ref[...], masks)
            dp_ref[rows, :] = draw.astype(BF16)
            dcw_ref[g] += dcw
            dcb_ref[g] += dcb
            ddraw = ddts[rows, :] * _sigmoid(dt_ref[rows, :] + dtb_ref[...])
            ddt_ref[rows, :] = ddraw
            ddtb_ref[g] += _colsum(ddraw)

        _for_rows(t, rb, epilogue)
        dal_ref[g] += dal_acc

    gspec = lambda shp: BS((None,) + shp, lambda b, g: (g,) + (0,) * len(shp))
    full = lambda shp: BS(shp, lambda b, g: (0,) * len(shp))
    st_spec = BS((None, None, SSD_N, 256), lambda b, g: (b, g, 0, 0))
    hs_spec = BS((None, None, nc, SSD_N, 256), lambda b, g: (b, g, 0, 0, 0))
    p_spec = BS((t, SSD_GW), lambda b, g: (blk0 + b, g))
    in_specs = [p_spec, BS((None, t, 128), lambda b, g: (g, blk0 + b, 0)),
                gspec((4, SSD_GW)), gspec((1, SSD_GW)), gspec((1, 128)), gspec((1, 128)), gspec((1, 256)),
                hs_spec, hs_spec]
    args = [proj, dtg, cw, cb, dtb, alog, drow, hsf, hsb]
    if need_y:
        in_specs.append(BS((t, 256), lambda b, g: (b, g)))
        args.append(dy)
    in_specs += [st_spec, st_spec, BS(memory_space=pl.ANY)]
    args += [dsf, dsb, dproj]
    out_specs = [p_spec, BS((None, t, 128), lambda b, g: (g, b, 0)), st_spec, st_spec,
                 full((SSD_G, 4, SSD_GW)), full((SSD_G, 1, SSD_GW)), full((SSD_G, 1, 128)), full((SSD_G, 1, 128)),
                 full((SSD_G, 1, 256))]
    out_shape = [S(dproj.shape, BF16), S((SSD_G, nb * t, 128), F32),
                 S((nb, SSD_G, SSD_N, 256), F32), S((nb, SSD_G, SSD_N, 256), F32),
                 S((SSD_G, 4, SSD_GW), F32), S((SSD_G, 1, SSD_GW), F32), S((SSD_G, 1, 128), F32),
                 S((SSD_G, 1, 128), F32), S((SSD_G, 1, 256), F32)]
    extra = side.arrays if side else []
    return pl.pallas_call(
        body, grid=(nb, SSD_G), name="ssd_bwd_lat" if need_y else "ssd_bwd_ctx",
        in_specs=in_specs + [_HBM] * len(extra), out_specs=out_specs + [_HBM] * len(extra),
        out_shape=out_shape + (side.out_shapes if side else []),
        input_output_aliases={len(args) - 1: 0},
        scratch_shapes=[pltpu.VMEM((t, SSD_GW), F32), pltpu.VMEM((t, SSD_GW), F32), pltpu.VMEM((t, 128), F32),
                        pltpu.VMEM((t, 128), F32), pltpu.VMEM((t, SSD_GW), F32)] + (side.sems if side else []),
        compiler_params=_params(),
    )(*args, *extra)


def _lru_gate(u, wa, ba, wi, bi, lam):
    r = _sigmoid(_mm(u, wa) + ba)
    i = _sigmoid(_mm(u, wi) + bi)
    log_a = -LRU_C * r * _softplus(-lam)
    a = jnp.exp(log_a)
    x2 = 2.0 * log_a
    em1 = jnp.where(x2 > -0.01, x2 * (1.0 + x2 * (0.5 + x2 * (1.0 / 6.0 + x2 * (1.0 / 24.0)))), a * a - 1.0)
    return a, jnp.sqrt(-em1) * (i * u)


def _scan_pair(fwd, rev, nblk, width):
    row = lax.broadcasted_iota(jnp.int32, (8, width), 0)

    def block(a_ref, b_ref, h_ref, st, carry, reverse):
        av, bv = a_ref[pl.ds(st, 8), :], b_ref[pl.ds(st, 8), :]
        for s in (1, 2, 4):
            ok = (row < 8 - s) if reverse else (row >= s)
            sh = (8 - s) if reverse else s
            a_sh = jnp.where(ok, pltpu.roll(av, sh, 0), 1.0)
            b_sh = jnp.where(ok, pltpu.roll(bv, sh, 0), 0.0)
            bv = av * b_sh + bv
            av = av * a_sh
        h = bv + av * carry
        h_ref[pl.ds(st, 8), :] = h
        return h[0:1, :] if reverse else h[7:8, :]

    def step(i, carry):
        cf, cr = carry
        cf = block(fwd[0], fwd[1], fwd[2], pl.multiple_of(i * 8, 8), cf, False)
        cr = block(rev[0], rev[1], rev[2], pl.multiple_of((nblk - 1 - i) * 8, 8), cr, True)
        return cf, cr

    return lax.fori_loop(0, nblk, step, (fwd[3], rev[3]))


def _lru_specs(t, blk0):
    p_spec = BS((t, LRU_CB), lambda b, q: (blk0 + b, P_LRU // LRU_CB + q))
    w_spec = BS((2, 2, 128, 128), lambda b, q: (0, q, 0, 0))
    v_spec = BS((2, LRU_CB), lambda b, q: (0, q))
    c_spec = lambda r: BS((r, LRU_CB), lambda b, q: (0, q))
    s_spec = BS((None, 2, LRU_CB), lambda b, q: (b, 0, q))
    return p_spec, w_spec, v_spec, c_spec, s_spec


def _lru_fwd(proj, cw, cb, wa, ba, wi, bi, lam, h0, nb, t, period, blk0, need_y):
    nq = D // LRU_CB
    masks_of = _conv_taps(LRU_CB, period)

    def body(p_ref, cw_ref, cb_ref, wa_ref, ba_ref, wi_ref, bi_ref, lam_ref, h0_ref, *rest):
        if need_y:
            y_ref, hf_ref, hb_ref, fin_ref, sa0, sb0, sa1, sb1 = rest
        else:
            hf_ref, hb_ref, fin_ref, sa0, sb0, sa1, sb1 = rest
        u = _conv_fwd(p_ref[...], cw_ref[...], cb_ref[...], masks_of(t))
        for d, (sa, sb) in enumerate(((sa0, sb0), (sa1, sb1))):
            for j in range(2):
                sl = slice(128 * j, 128 * j + 128)
                a, bb = _lru_gate(u[:, sl], wa_ref[d, j], ba_ref[d:d + 1, sl], wi_ref[d, j], bi_ref[d:d + 1, sl],
                                  lam_ref[d:d + 1, sl])
                sa[:, sl] = a
                sb[:, sl] = bb
        lf, lb = _scan_pair((sa0, sb0, hf_ref, h0_ref[0:1, :]), (sa1, sb1, hb_ref, h0_ref[1:2, :]), t // 8, LRU_CB)
        fin_ref[0:1, :] = lf
        fin_ref[1:2, :] = lb
        if need_y:
            y_ref[...] = hf_ref[...] + hb_ref[...]

    p_spec, w_spec, v_spec, c_spec, s_spec = _lru_specs(t, blk0)
    o_spec = BS((t, LRU_CB), lambda b, q: (b, q))
    out_specs = [o_spec, o_spec, s_spec]
    out_shape = [S((nb * t, D), F32), S((nb * t, D), F32), S((nb, 2, D), F32)]
    if need_y:
        out_specs = [o_spec] + out_specs
        out_shape = [S((nb * t, D), F32)] + out_shape
    return pl.pallas_call(
        body, grid=(nb, nq), name="lru_fwd_lat" if need_y else "lru_fwd_ctx",
        in_specs=[p_spec, c_spec(4), c_spec(1), w_spec, v_spec, w_spec, v_spec, v_spec, s_spec],
        out_specs=out_specs, out_shape=out_shape,
        scratch_shapes=[pltpu.VMEM((t, LRU_CB), F32)] * 4, compiler_params=_params(),
    )(proj, cw, cb, wa, ba, wi, bi, lam, h0)


def _lru_bwd(proj, cw, cb, wa, ba, wi, bi, lam, h0, hf, hb, dy, dfin, dproj, nb, t, period, blk0, need_y, side=None):
    nq = D // LRU_CB
    rc = min(1024, t)
    masks_of = _conv_taps(LRU_CB, period)

    def body(*refs):
        if side is None:
            return compute(*refs)
        own, exchange = side.split(refs, len(args), len(out_shape), 7)
        side.around(exchange, pl.program_id(0) * nq + pl.program_id(1), nb * nq, lambda: compute(*own))

    def compute(*refs):
        if need_y:
            (p_ref, cw_ref, cb_ref, wa_ref, ba_ref, wi_ref, bi_ref, lam_ref, h0_ref, hf_ref, hb_ref, dy_ref, dfin_ref, _,
             dp_ref, dh0_ref, dcw_ref, dcb_ref, dwa_ref, dwi_ref, dba_ref, dbi_ref, dlam_ref,
             su, sa0, sa1, sc0, sc1, sg0, sg1) = refs
        else:
            (p_ref, cw_ref, cb_ref, wa_ref, ba_ref, wi_ref, bi_ref, lam_ref, h0_ref, hf_ref, hb_ref, dfin_ref, _,
             dp_ref, dh0_ref, dcw_ref, dcb_ref, dwa_ref, dwi_ref, dba_ref, dbi_ref, dlam_ref,
             su, sa0, sa1, sc0, sc1, sg0, sg1) = refs
            dy_ref = None
        b, q = pl.program_id(0), pl.program_id(1)

        @pl.when(jnp.logical_and(b == 0, q == 0))
        def _():
            for r in (dcw_ref, dcb_ref, dwa_ref, dwi_ref, dba_ref, dbi_ref, dlam_ref):
                r[...] = jnp.zeros_like(r)

        masks = masks_of(t)
        u = _conv_fwd(p_ref[...], cw_ref[...], cb_ref[...], masks)
        su[...] = u
        for d, sa in enumerate((sa0, sa1)):
            for j in range(2):
                sl = slice(128 * j, 128 * j + 128)
                a, _unused = _lru_gate(u[:, sl], wa_ref[d, j], ba_ref[d:d + 1, sl], wi_ref[d, j], bi_ref[d:d + 1, sl],
                                       lam_ref[d:d + 1, sl])
                sa[:, sl] = a
        rowi = lax.broadcasted_iota(jnp.int32, (t, LRU_CB), 0)
        last, first = rowi == t - 1, rowi == 0
        sc0[...] = jnp.where(last, 0.0, pltpu.roll(sa0[...], t - 1, 0))
        sc1[...] = jnp.where(first, 0.0, pltpu.roll(sa1[...], 1, 0))
        g0 = jnp.where(last, dfin_ref[0:1, :], 0.0)
        g1 = jnp.where(first, dfin_ref[1:2, :], 0.0)
        if need_y:
            g0 = g0 + dy_ref[...]
            g1 = g1 + dy_ref[...]
        sg0[...] = g0
        sg1[...] = g1
        zero = jnp.zeros((1, LRU_CB), F32)
        _scan_pair((sc1, sg1, sg1, zero), (sc0, sg0, sg0, zero), t // 8, LRU_CB)
        dh0_ref[0:1, :] = sa0[0:1, :] * sg0[0:1, :]
        dh0_ref[1:2, :] = sa1[t - 1:t, :] * sg1[t - 1:t, :]
        sc0[...] = sg0[...] * jnp.where(first, h0_ref[0:1, :], pltpu.roll(hf_ref[...], 1, 0))
        sc1[...] = sg1[...] * jnp.where(last, h0_ref[1:2, :], pltpu.roll(hb_ref[...], t - 1, 0))

        def rows(ci, carry):
            r0 = pl.multiple_of(ci * rc, rc)
            for j in range(2):
                sl = slice(128 * j, 128 * j + 128)
                du = jnp.zeros((rc, 128), F32)
                for d, (sc, sg) in enumerate(((sc0, sg0), (sc1, sg1))):
                    _, vjp = jax.vjp(_lru_gate, su[pl.ds(r0, rc), sl], wa_ref[d, j], ba_ref[d:d + 1, sl], wi_ref[d, j],
                                     bi_ref[d:d + 1, sl], lam_ref[d:d + 1, sl])
                    du_d, dwa, dba, dwi, dbi, dlam = vjp((sc[pl.ds(r0, rc), sl], sg[pl.ds(r0, rc), sl]))
                    du = du + du_d
                    dwa_ref[d, 2 * q + j] += dwa
                    dwi_ref[d, 2 * q + j] += dwi
                    dba_ref[q, d:d + 1, sl] += dba
                    dbi_ref[q, d:d + 1, sl] += dbi
                    dlam_ref[q, d:d + 1, sl] += dlam
                sa0[pl.ds(r0, rc), sl] = du
            return carry

        lax.fori_loop(0, t // rc, rows, 0)
        draw, dcw, dcb = _conv_bwd(sa0[...], p_ref[...], cw_ref[...], masks)
        dp_ref[...] = draw.astype(BF16)
        dcw_ref[q] += dcw
        dcb_ref[q] += dcb

    p_spec, w_spec, v_spec, c_spec, s_spec = _lru_specs(t, blk0)
    o_spec = BS((t, LRU_CB), lambda b, q: (b, q))
    full = lambda shp: BS(shp, lambda b, q: (0,) * len(shp))
    in_specs = [p_spec, c_spec(4), c_spec(1), w_spec, v_spec, w_spec, v_spec, v_spec, s_spec, o_spec, o_spec]
    args = [proj, cw, cb, wa, ba, wi, bi, lam, h0, hf, hb]
    if need_y:
        in_specs.append(o_spec)
        args.append(dy)
    in_specs += [s_spec, BS(memory_space=pl.ANY)]
    args += [dfin, dproj]
    out_specs = [p_spec, s_spec, full((nq, 4, LRU_CB)), full((nq, 1, LRU_CB)), full((2, 8, 128, 128)),
                 full((2, 8, 128, 128)), full((nq, 2, LRU_CB)), full((nq, 2, LRU_CB)), full((nq, 2, LRU_CB))]
    out_shape = [S(dproj.shape, BF16), S((nb, 2, D), F32), S((nq, 4, LRU_CB), F32), S((nq, 1, LRU_CB), F32),
                 S((2, 8, 128, 128), F32), S((2, 8, 128, 128), F32), S((nq, 2, LRU_CB), F32), S((nq, 2, LRU_CB), F32),
                 S((nq, 2, LRU_CB), F32)]
    extra = side.arrays if side else []
    return pl.pallas_call(
        body, grid=(nb, nq), name="lru_bwd_lat" if need_y else "lru_bwd_ctx",
        in_specs=in_specs + [_HBM] * len(extra), out_specs=out_specs + [_HBM] * len(extra),
        out_shape=out_shape + (side.out_shapes if side else []), input_output_aliases={len(args) - 1: 0},
        scratch_shapes=[pltpu.VMEM((t, LRU_CB), F32)] * 7 + (side.sems if side else []), compiler_params=_params(),
    )(*args, *extra)


def _mix_core(y_ref, yl_ref, p_ref, nw_ref, bg_ref, wbs_ref, wbl_ref, wo_ref, nrm_s):
    for g in range(SSD_G):
        sl = slice(256 * g, 256 * g + 256)
        nrm_s[:, sl] = _grms(y_ref[:, sl], p_ref[:, sl], nw_ref[:, sl]).astype(BF16)
    br_s = jnp.dot(nrm_s[...], wbs_ref[...], preferred_element_type=F32)
    gl = (yl_ref[...] * _gelu(p_ref[:, 2048:3072])).astype(BF16)
    br_l = jnp.dot(gl, wbl_ref[...], preferred_element_type=F32)
    gs = _sigmoid(p_ref[:, 3072:4096] + bg_ref[:, 0:D])
    gr = _sigmoid(p_ref[:, 4096:5120] + bg_ref[:, D:2 * D])
    mix = (gs * br_s + gr * br_l).astype(BF16)
    xmix = jnp.dot(mix, wo_ref[...], preferred_element_type=F32)
    return br_s, gl, br_l, gs, gr, mix, xmix


def _mix_specs(rt, tiles_per_b):
    row = lambda w: BS((rt, w), lambda i: (i, 0))
    const = lambda shp: BS(shp, lambda i: (0,) * len(shp))
    gate = BS((None, None, 1, D), lambda i: (i // tiles_per_b, 2, 0, 0))
    return row, const, gate


def _mix_fwd(y, ylru, proj, x, m4, wbs, wbl, wo, nw, bg, l1g, l1b, rt, tiles_per_b):
    n = x.shape[0]

    def body(y_ref, yl_ref, p_ref, x_ref, g1_ref, wbs_ref, wbl_ref, wo_ref, nw_ref, bg_ref, lg_ref, lb_ref,
             x1_ref, nrm_ref, gl_ref, mix_ref, brs_ref, brl_ref, xm_ref):
        br_s, gl, br_l, _, _, mix, xmix = _mix_core(y_ref, yl_ref, p_ref, nw_ref, bg_ref, wbs_ref, wbl_ref, wo_ref, nrm_ref)
        gl_ref[...] = gl
        mix_ref[...] = mix
        brs_ref[...] = br_s
        brl_ref[...] = br_l
        xm_ref[...] = xmix
        x1_ref[...] = _resln(x_ref[...], xmix, g1_ref[...], lg_ref[...], lb_ref[...])

    row, const, gate = _mix_specs(rt, tiles_per_b)
    return pl.pallas_call(
        body, grid=(n // rt,), name="mix_fwd",
        in_specs=[row(SSD_INNER), row(D), BS((rt, 5120), lambda i: (i, 1)), row(D), gate,
                  const((SSD_INNER, D)), const((D, D)), const((D, D)), const((1, SSD_INNER)), const((1, 2 * D)),
                  const((1, D)), const((1, D))],
        out_specs=[row(D), row(SSD_INNER), row(D), row(D), row(D), row(D), row(D)],
        out_shape=[S((n, D), F32), S((n, SSD_INNER), BF16), S((n, D), BF16), S((n, D), BF16), S((n, D), F32),
                   S((n, D), F32), S((n, D), F32)],
        compiler_params=_params(),
    )(y, ylru, proj, x, m4, wbs, wbl, wo, nw, bg, l1g, l1b)


def _mix_bwd(y, ylru, proj, x, m4, wbs, wbl, wo, nw, bg, l1g, l1b, brs, brl, xmix, dx1, dproj, rt, tiles_per_b):
    n = x.shape[0]

    def body(y_ref, yl_ref, p_ref, x_ref, g1_ref, wbs_ref, wbl_ref, wo_ref, nw_ref, bg_ref, lg_ref, lb_ref,
             brs_ref, brl_ref, xm_ref, dx1_ref, _,
             dp_ref, dy_ref, dyl_ref, dxr_ref, dbrs_ref, dbrl_ref, dxm_ref,
             dg1_ref, dnw_ref, dbg_ref, dlg_ref, dlb_ref):
        i = pl.program_id(0)

        @pl.when(i == 0)
        def _():
            for r in (dnw_ref, dbg_ref, dlg_ref, dlb_ref):
                r[...] = jnp.zeros_like(r)

        @pl.when(i % tiles_per_b == 0)
        def _():
            dg1_ref[...] = jnp.zeros_like(dg1_ref)

        br_s, br_l = brs_ref[...], brl_ref[...]
        gs = _sigmoid(p_ref[:, 3072:4096] + bg_ref[:, 0:D])
        gr = _sigmoid(p_ref[:, 4096:5120] + bg_ref[:, D:2 * D])
        _, vjp = jax.vjp(_resln, x_ref[...], xm_ref[...], g1_ref[...], lg_ref[...], lb_ref[...])
        dxr, dxmix, dg1, dlg, dlb = vjp(dx1_ref[...])
        dxr_ref[...] = dxr
        dg1_ref[...] += dg1
        dlg_ref[...] += dlg
        dlb_ref[...] += dlb
        dxmb = dxmix.astype(BF16)
        dxm_ref[...] = dxmb
        dmix = lax.dot_general(dxmb, wo_ref[...], (((1,), (1,)), ((), ())), preferred_element_type=F32)
        dbrs = (dmix * gs).astype(BF16)
        dbrl = (dmix * gr).astype(BF16)
        dbrs_ref[...] = dbrs
        dbrl_ref[...] = dbrl
        dmg_s = dmix * br_s * gs * (1.0 - gs)
        dmg_r = dmix * br_l * gr * (1.0 - gr)
        dp_ref[:, 3072:4096] = dmg_s.astype(BF16)
        dp_ref[:, 4096:5120] = dmg_r.astype(BF16)
        dbg_ref[:, 0:D] += _colsum(dmg_s)
        dbg_ref[:, D:2 * D] += _colsum(dmg_r)
        for g in range(SSD_G):
            sl = slice(256 * g, 256 * g + 256)
            dnrm = lax.dot_general(dbrs, wbs_ref[sl, :], (((1,), (1,)), ((), ())), preferred_element_type=F32)
            _, vjp = jax.vjp(_grms, y_ref[:, sl], p_ref[:, sl], nw_ref[:, sl])
            dyg, dzg, dnwg = vjp(dnrm)
            dy_ref[:, sl] = dyg
            dp_ref[:, sl] = dzg.astype(BF16)
            dnw_ref[:, sl] += dnwg
        dgl = lax.dot_general(dbrl, wbl_ref[...], (((1,), (1,)), ((), ())), preferred_element_type=F32)
        _, vjp = jax.vjp(lambda a, c: a * _gelu(c), yl_ref[...], p_ref[:, 2048:3072])
        dyl, dlgate = vjp(dgl)
        dyl_ref[...] = dyl
        dp_ref[:, 2048:3072] = dlgate.astype(BF16)

    row, const, gate = _mix_specs(rt, tiles_per_b)
    pblk = BS((rt, 5120), lambda i: (i, 1))
    nb = n // (rt * tiles_per_b)
    out_specs = [pblk, row(SSD_INNER), row(D), row(D), row(D), row(D), row(D),
                 BS((None, 1, D), lambda i: (i // tiles_per_b, 0, 0)), const((1, SSD_INNER)), const((1, 2 * D)),
                 const((1, D)), const((1, D))]
    out_shape = [S(dproj.shape, BF16), S((n, SSD_INNER), F32), S((n, D), F32), S((n, D), F32),
                 S((n, D), BF16), S((n, D), BF16), S((n, D), BF16),
                 S((nb, 1, D), F32), S((1, SSD_INNER), F32), S((1, 2 * D), F32), S((1, D), F32), S((1, D), F32)]
    return pl.pallas_call(
        body, grid=(n // rt,), name="mix_bwd",
        in_specs=[row(SSD_INNER), row(D), pblk, row(D), gate,
                  const((SSD_INNER, D)), const((D, D)), const((D, D)), const((1, SSD_INNER)), const((1, 2 * D)),
                  const((1, D)), const((1, D)), row(D), row(D), row(D), row(D), BS(memory_space=pl.ANY)],
        out_specs=out_specs, out_shape=out_shape, input_output_aliases={16: 0},
        compiler_params=_params(),
    )(y, ylru, proj, x, m4, wbs, wbl, wo, nw, bg, l1g, l1b, brs, brl, xmix, dx1, dproj)


def _mlp_step(x1, tgt, m4, w1, b1, w2, b2, l2g, l2b, rt, tiles_per_b):
    n = x1.shape[0]

    def body(x_ref, t_ref, sh_ref, sc_ref, gt_ref, w1_hbm, b1_ref, w2_hbm, b2_ref, lg_ref, lb_ref,
             loss_ref, dx_ref, h2_ref, da1_ref, r2_ref, dmlp_ref, dm_ref, db1_ref, db2_ref, dlg_ref, dlb_ref,
             w1_vm, w2_vm, sem):
        i = pl.program_id(0)

        @pl.when(i == 0)
        def _():
            c1 = pltpu.make_async_copy(w1_hbm, w1_vm, sem.at[0])
            c2 = pltpu.make_async_copy(w2_hbm, w2_vm, sem.at[1])
            c1.start()
            c2.start()
            for r in (loss_ref, db1_ref, db2_ref, dlg_ref, dlb_ref):
                r[...] = jnp.zeros_like(r)
            c1.wait()
            c2.wait()

        @pl.when(i % tiles_per_b == 0)
        def _():
            dm_ref[...] = jnp.zeros_like(dm_ref)

        x1v = x_ref[...]
        h2, vjp_h = jax.vjp(_modln, x1v, sh_ref[...], sc_ref[...])
        h2b = h2.astype(BF16)
        h2_ref[...] = h2b
        r = jnp.maximum(jnp.dot(h2b, w1_vm[...], preferred_element_type=F32) + b1_ref[...], 0.0)
        r2b = (r * r).astype(BF16)
        r2_ref[...] = r2b
        mlp = jnp.dot(r2b, w2_vm[...], preferred_element_type=F32) + b2_ref[...]
        x2, vjp_r = jax.vjp(_resln, x1v, mlp, gt_ref[...], lg_ref[...], lb_ref[...])
        diff = x2 - t_ref[...]
        loss_ref[...] += (0.5 / D) * jnp.sum(diff * diff)
        dxa, dmlp, dgt, dlg, dlb = vjp_r(diff * (1.0 / D))
        dlg_ref[...] += dlg
        dlb_ref[...] += dlb
        dm_ref[2:3, :] += dgt
        db2_ref[...] += _colsum(dmlp)
        dmlpb = dmlp.astype(BF16)
        dmlp_ref[...] = dmlpb
        da1 = lax.dot_general(dmlpb, w2_vm[...], (((1,), (1,)), ((), ())), preferred_element_type=F32) * (2.0 * r)
        db1_ref[...] += _colsum(da1)
        da1b = da1.astype(BF16)
        da1_ref[...] = da1b
        dh2 = lax.dot_general(da1b, w1_vm[...], (((1,), (1,)), ((), ())), preferred_element_type=F32)
        dxb, dsh, dsc = vjp_h(dh2)
        dx_ref[...] = dxa + dxb
        dm_ref[0:1, :] += dsh
        dm_ref[1:2, :] += dsc

    row = lambda w: BS((rt, w), lambda i: (i, 0))
    const = lambda shp: BS(shp, lambda i: (0,) * len(shp))
    mod = lambda k: BS((None, None, 1, D), lambda i: (i // tiles_per_b, k, 0, 0))
    nb = n // (rt * tiles_per_b)
    anyspec = BS(memory_space=pl.ANY)
    return pl.pallas_call(
        body, grid=(n // rt,), name="mlp_step",
        in_specs=[row(D), row(D), mod(3), mod(4), mod(5), anyspec, const((1, MLP_H)), anyspec, const((1, D)),
                  const((1, D)), const((1, D))],
        out_specs=[const((8, 128)), row(D), row(D), row(MLP_H), row(MLP_H), row(D),
                   BS((None, 3, D), lambda i: (i // tiles_per_b, 0, 0)), const((1, MLP_H)), const((1, D)),
                   const((1, D)), const((1, D))],
        out_shape=[S((8, 128), F32), S((n, D), F32), S((n, D), BF16), S((n, MLP_H), BF16), S((n, MLP_H), BF16),
                   S((n, D), BF16), S((nb, 3, D), F32), S((1, MLP_H), F32), S((1, D), F32), S((1, D), F32),
                   S((1, D), F32)],
        scratch_shapes=[pltpu.VMEM((D, MLP_H), BF16), pltpu.VMEM((MLP_H, D), BF16), pltpu.SemaphoreType.DMA((2,))],
        compiler_params=_params(),
    )(x1, tgt, m4, m4, m4, w1, b1, w2, b2, l2g, l2b)


def _pack_win(w):
    parts = []
    for g in range(SSD_G):
        parts += [w[:, 256 * g:256 * g + 256], w[:, 2048 + 128 * g:2176 + 128 * g], w[:, 4160 + 128 * g:4288 + 128 * g]]
    parts += [w[:, 3136:4160], w[:, 5184:7232], w[:, 7232:8256], w[:, 8256:10304], w[:, 3072:3136],
              jnp.zeros((w.shape[0], P_W - P_DT - 64), w.dtype)]
    return jnp.concatenate(parts, axis=1)


def _unpack_win(p):
    xs = [p[:, 512 * g:512 * g + 256] for g in range(SSD_G)]
    bs = [p[:, 512 * g + 256:512 * g + 384] for g in range(SSD_G)]
    cs = [p[:, 512 * g + 384:512 * g + 512] for g in range(SSD_G)]
    return jnp.concatenate(xs + bs + [p[:, P_DT:P_DT + 64], p[:, P_LRU:P_Z]] + cs + [p[:, P_Z:P_DT]], axis=1)


def _pack_conv(w):
    return jnp.stack([jnp.concatenate([w[:, 256 * g:256 * g + 256], w[:, 2048 + 128 * g:2176 + 128 * g],
                                       w[:, 3072 + 128 * g:3200 + 128 * g]], axis=1) for g in range(SSD_G)])


def _unpack_conv(p):
    r = p.shape[1]
    x = jnp.transpose(p[:, :, 0:256], (1, 0, 2)).reshape(r, 2048)
    b = jnp.transpose(p[:, :, 256:384], (1, 0, 2)).reshape(r, 1024)
    c = jnp.transpose(p[:, :, 384:512], (1, 0, 2)).reshape(r, 1024)
    return jnp.concatenate([x, b, c], axis=1)


def _pack_heads(v):
    p = jnp.transpose(v.reshape(2, SSD_G, 4), (1, 0, 2)).reshape(SSD_G, 1, 8)
    return jnp.pad(p, ((0, 0), (0, 0), (0, 120)))


def _unpack_heads(p):
    return jnp.transpose(p[:, 0, 0:8].reshape(SSD_G, 2, 4), (1, 0, 2)).reshape(2, 32)


def _pack_dt(dt):
    n = dt.shape[0]
    p = jnp.transpose(dt.reshape(n, 2, SSD_G, 4), (2, 0, 1, 3)).reshape(SSD_G, n, 8)
    return jnp.pad(p, ((0, 0), (0, 0), (0, 120)))


def _unpack_dt(p):
    n = p.shape[1]
    return jnp.transpose(p[:, :, 0:8].reshape(SSD_G, n, 2, 4), (1, 2, 0, 3)).reshape(n, 64)


def _tk(rows):
    return next(tk for tk in (1024, 512, 256, 128) if rows % tk == 0)


LATE = ["w_br_ssd", "w_br_lru", "w_out", "w_mlp1", "w_mlp2"]


def _local_step(x, c, ctx, tgt, sm, wmod, win, late, late_are_shards=False, reducer=None):
    nb, t, _ = x.shape
    tc = ctx.shape[1]
    nl, ncx = nb * t, nb * tc
    rt = 256 if tc % 256 == 0 else 128
    rtm = 128
    xl, xc = x.reshape(nl, D), ctx.reshape(ncx, D)
    tgt2 = tgt.reshape(nl, D)
    cc = jnp.zeros((8, D), F32).at[0:nb].set(c).at[nb].set(sm["c_ctx"])
    m = _mod_fwd(cc, wmod, sm["b_mod"])
    m4 = m.reshape(8, N_MOD, 1, D)
    proj, h1 = _inproj_fwd(xl, m4, win, rtm, nl // rtm, t // rtm, nb, "inproj_fwd_lat")
    proj_c, h1_c = _inproj_fwd(xc, m4, win, rtm, 0, 1, nb, "inproj_fwd_ctx")

    cw_s, cb_s = _pack_conv(sm["ssd_conv_w"]), _pack_conv(sm["ssd_conv_b"])
    dtb, alog = _pack_heads(sm["ssd_dt_bias"]), _pack_heads(sm["ssd_a_log"])
    drow = jnp.repeat(sm["ssd_d"].reshape(32), 64).reshape(SSD_G, 1, 256)
    dtg, dtg_c = _pack_dt(proj[:, P_DT:P_DT + 64]), _pack_dt(proj_c[:, P_DT:P_DT + 64])
    zst = jnp.zeros((nb, SSD_G, SSD_N, 256), F32)
    zl = jnp.zeros((nb, 2, D), F32)
    ssd_p = (cw_s, cb_s, dtb, alog, drow)
    lru_p = (sm["lru_conv_w"], sm["lru_conv_b"], sm["lru_wa"], sm["lru_ba"], sm["lru_wi"], sm["lru_bi"], sm["lru_lambda"])

    chsf, chsb, csf, csb = _ssd_fwd(proj_c, dtg_c, *ssd_p, zst, zst, nb, tc, tc, 0, False)
    y, lhsf, lhsb, _, _, *got = _ssd_fwd(proj, dtg, *ssd_p, csf, csb, nb, t, GRID_W, 0, True,
                                         tuple(late) if late_are_shards else ())
    wbs, wbl, wo, w1, w2 = [_full_from_chips(g, n) for g, n in zip(got, LATE)] if late_are_shards else late
    chf, chb, cfin = _lru_fwd(proj_c, *lru_p, zl, nb, tc, tc, 0, False)
    ylru, lhf, lhb, _ = _lru_fwd(proj, *lru_p, cfin, nb, t, GRID_W, 0, True)
    mix_w = (wbs, wbl, wo, sm["ssd_norm_w"], sm["b_gate"], sm["ln1_g"], sm["ln1_b"])
    x1, nrm, gl, mixb, brs, brl, xmix = _mix_fwd(y, ylru, proj, xl, m4, *mix_w, rtm, t // rtm)
    (loss, dx1, h2, da1, r2, dmlp, dm2, db1, db2, dl2g, dl2b) = _mlp_step(
        x1, tgt2, m4, w1, sm["b_mlp1"], w2, sm["b_mlp2"], sm["ln2_g"], sm["ln2_b"], rt, t // rt)

    dproj = lax.empty((nl, P_W), BF16)
    dproj_c = jnp.zeros((ncx, P_W), BF16)
    (dproj, dy, dylru, dxres, dbrs, dbrl, dxm, dg1, dnw, dbg, dl1g, dl1b) = _mix_bwd(
        y, ylru, proj, xl, m4, *mix_w, brs, brl, xmix, dx1, dproj, rtm, t // rtm)
    big = {
        "w_br_ssd": _matmul_tn(nrm, dbrs, D, D, _tk(nl), "dw_br_ssd"),
        "w_br_lru": _matmul_tn(gl, dbrl, D, D, _tk(nl), "dw_br_lru"),
        "w_out": _matmul_tn(mixb, dxm, D, D, _tk(nl), "dw_out"),
        "w_mlp1": _matmul_tn(h2, da1, D, D, _tk(nl), "dw_mlp1"),
        "w_mlp2": _matmul_tn(r2, dmlp, D, D, _tk(nl), "dw_mlp2"),
    }
    side = reducer.begin_swap(big, list(big)) if reducer else None
    (dproj, ddt_l, dh0f, dh0b, dcw_l, dcb_l, ddtb_l, dal_l, dd, *got) = _ssd_bwd(
        proj, dtg, *ssd_p, lhsf, lhsb, dy, zst, zst, dproj, nb, t, GRID_W, 0, True, side)
    (dproj_c, ddt_c, _, _, dcw_c, dcb_c, ddtb_c, dal_c, _) = _ssd_bwd(
        proj_c, dtg_c, *ssd_p, chsf, chsb, None, dh0f, dh0b, dproj_c, nb, tc, tc, 0, False)
    side = reducer.begin_scatter(got) if reducer else None
    (dproj, dlh0, gcw_l, gcb_l, gwa_l, gwi_l, gba_l, gbi_l, glam_l, *got) = _lru_bwd(
        proj, *lru_p, cfin, lhf, lhb, dylru, zl, dproj, nb, t, GRID_W, 0, True, side)
    if reducer:
        reducer.end(got)
    (dproj_c, _, gcw_c, gcb_c, gwa_c, gwi_c, gba_c, gbi_c, glam_c) = _lru_bwd(
        proj_c, *lru_p, zl, chf, chb, None, dlh0, dproj_c, nb, tc, tc, 0, False)
    pad_dt = lambda d: jnp.pad(_unpack_dt(d).astype(BF16), ((0, 0), (0, P_W - P_DT - 64)))
    dproj = lax.dynamic_update_slice(dproj, pad_dt(ddt_l), (0, P_DT))
    dproj_c = lax.dynamic_update_slice(dproj_c, pad_dt(ddt_c), (0, P_DT))

    big["w_in"] = _matmul_tn2(h1, dproj, h1_c, dproj_c, D, 1152, min(_tk(nl), _tk(ncx)), "dw_in")
    side = reducer.begin_swap(big, ["w_in"]) if reducer else None
    dmc, *got = _inproj_bwd(xc, m4, win, dproj_c, rt, 0, ncx // rt, ncx // rt, nb, False, None, side)
    side = reducer.begin_scatter(got) if reducer else None
    gx, dm1, *got = _inproj_bwd(xl, m4, win, dproj, rt, 0, nl // rt, t // rt, nb, True, dxres, side)
    if reducer:
        reducer.end(got)
    dm = jnp.zeros((8, N_MOD, D), F32)
    dm = dm.at[0:nb].set(jnp.concatenate([dm1, dg1, dm2], axis=1)).at[nb, 0:2].set(dmc[0])
    dwmod, dbmod, dcc = _mod_bwd(cc, wmod, dm.reshape(8, N_MOD * D))
    big["w_mod"] = dwmod
    nq = D // LRU_CB
    small = {
        "c_ctx": dcc[nb],
        "b_mod": dbmod,
        "b_gate": dbg,
        "ssd_conv_w": _unpack_conv(dcw_l + dcw_c),
        "ssd_conv_b": _unpack_conv(dcb_l + dcb_c),
        "ssd_dt_bias": _unpack_heads(ddtb_l + ddtb_c),
        "ssd_a_log": _unpack_heads(dal_l + dal_c),
        "ssd_d": jnp.sum(dd.reshape(32, 64), axis=1),
        "ssd_norm_w": dnw,
        "lru_conv_w": jnp.transpose(gcw_l + gcw_c, (1, 0, 2)).reshape(4, D),
        "lru_conv_b": (gcb_l + gcb_c).reshape(1, D),
        "lru_wa": gwa_l + gwa_c,
        "lru_ba": jnp.transpose(gba_l + gba_c, (1, 0, 2)).reshape(2, D),
        "lru_wi": gwi_l + gwi_c,
        "lru_bi": jnp.transpose(gbi_l + gbi_c, (1, 0, 2)).reshape(2, D),
        "lru_lambda": jnp.transpose(glam_l + glam_c, (1, 0, 2)).reshape(2, D),
        "ln1_g": dl1g, "ln1_b": dl1b, "b_mlp1": db1, "b_mlp2": db2, "ln2_g": dl2g, "ln2_b": dl2b,
    }
    return loss[0, 0], gx.reshape(nb, t, D), big, small


_HBM = BS(memory_space=pl.ANY)


def _place():
    return lax.axis_index("x"), lax.axis_index("y"), lax.axis_index("c")


def _other_chips(x, y):
    return [(1 - x, y), (x, 1 - y), (1 - x, 1 - y)]


def _gather_chips(arrs):
    n = len(arrs)

    def body(*refs):
        ex = _GatherExchange(refs[:n], refs[n:2 * n], refs[2 * n:])
        ex.begin()
        ex.finish()

    return pl.pallas_call(
        body, name="gather_weights", in_specs=[_HBM] * n, out_specs=[_HBM] * n,
        out_shape=_gather_out_shapes(arrs), scratch_shapes=_gather_sems(n),
    )(*arrs)


def _gather_out_shapes(arrs):
    return [S((4,) + a.shape, a.dtype) for a in arrs]


def _gather_sems(n):
    return [pltpu.SemaphoreType.DMA((3 * n,))] * 4 + [pltpu.SemaphoreType.DMA((n,))]


class _GatherExchange:
    def __init__(self, ins, outs, sems):
        self.ins, self.outs = ins, outs
        self.ici_send, self.ici_recv, self.d2d_send, self.d2d_recv, self.loc_sems = sems
        self.x, self.y, self.c = _place()
        self.me = 2 * self.x + self.y
        self.chips = _other_chips(self.x, self.y)

    def _half(self, a, which):
        hr = self.ins[a].shape[0] // 2
        return pl.ds(pl.multiple_of((self.c if which == 0 else 1 - self.c) * hr, 8), hr)

    def _local(self, a):
        return pltpu.make_async_copy(self.ins[a], self.outs[a].at[self.me], self.loc_sems.at[a])

    def _ici(self, a, k, slot):
        px, py = self.chips[k]
        mine = self._half(a, 0)
        return pltpu.make_async_remote_copy(src_ref=self.ins[a].at[mine], dst_ref=self.outs[a].at[slot, mine],
                                            send_sem=self.ici_send.at[3 * a + k], recv_sem=self.ici_recv.at[3 * a + k],
                                            device_id=(px, py, self.c), device_id_type=MESH)

    def _d2d(self, a, k, which):
        px, py = self.chips[k]
        rows = self.outs[a].at[2 * px + py, self._half(a, which)]
        return pltpu.make_async_remote_copy(src_ref=rows, dst_ref=rows, send_sem=self.d2d_send.at[3 * a + k],
                                            recv_sem=self.d2d_recv.at[3 * a + k],
                                            device_id=(self.x, self.y, 1 - self.c), device_id_type=MESH)

    def begin(self):
        for a in range(len(self.ins)):
            self._local(a).start()
            for k in range(3):
                self._ici(a, k, self.me).start()

    def finish(self):
        n = len(self.ins)
        for a in range(n):
            for k, (px, py) in enumerate(self.chips):
                self._ici(a, k, 2 * px + py).wait_recv()
                self._d2d(a, k, 0).start()
        for a in range(n):
            for k in range(3):
                self._d2d(a, k, 1).wait_recv()
        for a in range(n):
            self._local(a).wait()
            for k in range(3):
                self._ici(a, k, self.me).wait_send()
                self._d2d(a, k, 0).wait_send()


def _scatter_chips(arrs):
    side = _Side("scatter", arrs)
    n = len(arrs)

    def body(*refs):
        ex = side.make(refs[:n], refs[n:2 * n], refs[2 * n:])
        ex.begin()
        ex.finish()

    return pl.pallas_call(
        body, name="scatter_grads", in_specs=[_HBM] * n, out_specs=[_HBM] * n,
        out_shape=side.out_shapes, scratch_shapes=side.sems,
    )(*arrs)


class _ScatterExchange:
    def __init__(self, ins, outs, sems):
        self.ins, self.outs = ins, outs
        self.send_sems, self.recv_sems = sems
        x, y, self.c = _place()
        self.chips = _other_chips(x, y)

    def _copy(self, a, k):
        px, py = self.chips[k]
        return pltpu.make_async_remote_copy(src_ref=self.ins[a].at[2 * px + py], dst_ref=self.outs[a].at[k],
                                            send_sem=self.send_sems.at[3 * a + k], recv_sem=self.recv_sems.at[3 * a + k],
                                            device_id=(px, py, self.c), device_id_type=MESH)

    def begin(self):
        for a in range(len(self.ins)):
            for k in range(3):
                self._copy(a, k).start()

    def finish(self):
        for a in range(len(self.ins)):
            for k in range(3):
                self._copy(a, k).wait_recv()
        for a in range(len(self.ins)):
            for k in range(3):
                self._copy(a, k).wait_send()


class _Side:
    def __init__(self, kind, arrays):
        self.arrays = list(arrays)
        n = len(self.arrays)
        if kind == "gather":
            self.out_shapes, self.sems, self.make = _gather_out_shapes(self.arrays), _gather_sems(n), _GatherExchange
        elif kind == "swap":
            self.out_shapes = [S((4, a.shape[1] // 2, a.shape[2]), a.dtype) for a in self.arrays]
            self.sems = [pltpu.SemaphoreType.DMA((4 * n,))] * 2
            self.make = _SwapExchange
        else:
            self.out_shapes = [S((3,) + a.shape[1:], a.dtype) for a in self.arrays]
            self.sems = [pltpu.SemaphoreType.DMA((3 * n,))] * 2
            self.make = _ScatterExchange

    def split(self, refs, n_in, n_out, n_scr):
        a, b = len(self.arrays), len(self.out_shapes)
        i1 = n_in + a
        o1 = i1 + n_out
        o2 = o1 + b
        s1 = o2 + n_scr
        own = tuple(refs[:n_in]) + tuple(refs[i1:o1]) + tuple(refs[o2:s1])
        return own, self.make(refs[n_in:i1], refs[o1:o2], refs[s1:])

    def around(self, exchange, step, n_steps, compute):
        pl.when(step == 0)(exchange.begin)
        compute()
        pl.when(step == n_steps - 1)(exchange.finish)


def _swap_halves(arrs, name):
    side = _Side("swap", arrs)
    n = len(arrs)

    def body(*refs):
        ex = side.make(refs[:n], refs[n:2 * n], refs[2 * n:])
        ex.begin()
        ex.finish()

    return pl.pallas_call(
        body, name=name, in_specs=[_HBM] * n, out_specs=[_HBM] * n, out_shape=side.out_shapes, scratch_shapes=side.sems,
    )(*arrs)


class _SwapExchange:
    def __init__(self, ins, outs, sems):
        self.ins, self.outs = ins, outs
        self.send_sems, self.recv_sems = sems
        self.x, self.y, self.c = _place()

    def _copy(self, a, q):
        hr = self.ins[a].shape[1] // 2
        theirs = pl.ds(pl.multiple_of((1 - self.c) * hr, 8), hr)
        return pltpu.make_async_remote_copy(src_ref=self.ins[a].at[q, theirs], dst_ref=self.outs[a].at[q],
                                            send_sem=self.send_sems.at[4 * a + q], recv_sem=self.recv_sems.at[4 * a + q],
                                            device_id=(self.x, self.y, 1 - self.c), device_id_type=MESH)

    def begin(self):
        for a in range(len(self.ins)):
            for q in range(4):
                self._copy(a, q).start()

    def finish(self):
        for wait in ("wait_recv", "wait_send"):
            for a in range(len(self.ins)):
                for q in range(4):
                    getattr(self._copy(a, q), wait)()


def _allreduce_small(vs, bf16_over_ici):
    n = len(vs)

    def body(*refs):
        ins, outs, bufs, sendb = refs[:n], refs[n:2 * n], refs[2 * n:5 * n], refs[5 * n:6 * n]
        send_sems, recv_sems = refs[6 * n:]
        x, y, c = _place()
        srcs = list(ins)
        for s, peer in enumerate(((x, y, 1 - c), (x, 1 - y, c), (1 - x, y, c))):
            copies = []
            for a in range(n):
                src = srcs[a]
                if s > 0 and bf16_over_ici[a]:
                    sendb[a][...] = src[...].astype(BF16)
                    src = sendb[a]
                copies.append(pltpu.make_async_remote_copy(
                    src_ref=src, dst_ref=bufs[3 * a + s], send_sem=send_sems.at[3 * a + s],
                    recv_sem=recv_sems.at[3 * a + s], device_id=peer, device_id_type=MESH))
                copies[-1].start()
            for a, cp in enumerate(copies):
                cp.wait()
                mine = sendb[a] if s > 0 and bf16_over_ici[a] else srcs[a]
                outs[a][...] = mine[...].astype(F32) + bufs[3 * a + s][...].astype(F32)
            srcs = list(outs)

    vm = BS(memory_space=pltpu.VMEM)
    wire = lambda a, s: BF16 if s > 0 and bf16_over_ici[a] else F32
    return pl.pallas_call(
        body, name="allreduce_small", in_specs=[vm] * n, out_specs=[vm] * n, out_shape=[S(v.shape, F32) for v in vs],
        scratch_shapes=[pltpu.VMEM(v.shape, wire(a, s)) for a, v in enumerate(vs) for s in range(3)]
        + [pltpu.VMEM(v.shape if bf16_over_ici[a] else (16, 128), BF16) for a, v in enumerate(vs)]
        + [pltpu.SemaphoreType.DMA((3 * n,)), pltpu.SemaphoreType.DMA((3 * n,))],
        compiler_params=_params(),
    )(*vs)


def _swap_cores(arrs):
    n = len(arrs)

    def body(*refs):
        ins, outs = refs[:n], refs[n:2 * n]
        send_sems, recv_sems = refs[2 * n:]
        x, y, c = _place()
        sends = []
        for a in range(n):
            cp = pltpu.make_async_remote_copy(src_ref=ins[a], dst_ref=outs[a], send_sem=send_sems.at[a],
                                              recv_sem=recv_sems.at[a], device_id=(x, y, 1 - c), device_id_type=MESH)
            cp.start()
            sends.append(cp)
        for cp in sends:
            cp.wait_recv()
        for cp in sends:
            cp.wait_send()

    return pl.pallas_call(
        body, name="swap_cores", in_specs=[_HBM] * n, out_specs=[_HBM] * n,
        out_shape=[S(a.shape, a.dtype) for a in arrs],
        scratch_shapes=[pltpu.SemaphoreType.DMA((n,)), pltpu.SemaphoreType.DMA((n,))],
    )(*arrs)


def _row_tile(r, c=128):
    tr = 256 if c <= 1024 else (128 if c <= 2048 else 64)
    return tr if r % tr == 0 else r


def _sum_half(own, sib, core, name):
    _, r, c = own.shape
    hr = r // 2
    tr = _row_tile(hr, c)
    nbk = hr // tr

    def body(core_ref, o_ref, s_ref, p_ref, pb_ref):
        p = o_ref[...] + s_ref[...]
        p_ref[...] = p
        pb_ref[...] = p.astype(BF16)

    blk = BS((None, tr, c), lambda q, i, cr: (q, i, 0))
    return pl.pallas_call(
        body, name=name, out_shape=[S((4, hr, c), F32), S((4, hr, c), BF16)],
        grid_spec=pltpu.PrefetchScalarGridSpec(
            num_scalar_prefetch=1, grid=(4, nbk),
            in_specs=[BS((None, tr, c), lambda q, i, cr: (q, cr[0] * nbk + i, 0)), blk], out_specs=[blk, blk]),
        compiler_params=_params(),
    )(core, own, sib)


def _sum4(part, recv, chip, name):
    _, r, c = part.shape
    tr = _row_tile(r, c)

    def body(chip_ref, o_ref, r_ref, out_ref):
        acc = o_ref[...]
        for k in range(3):
            acc = acc + r_ref[k].astype(F32)
        out_ref[...] = acc

    return pl.pallas_call(
        body, name=name, out_shape=S((r, c), F32),
        grid_spec=pltpu.PrefetchScalarGridSpec(
            num_scalar_prefetch=1, grid=(r // tr,),
            in_specs=[BS((None, tr, c), lambda i, ch: (ch[0], i, 0)), BS((3, tr, c), lambda i, ch: (0, i, 0))],
            out_specs=BS((tr, c), lambda i, ch: (i, 0))),
        compiler_params=_params(),
    )(chip, part, recv)


def _adam_math(w, g, m, v):
    m = ADAM_B1 * m + (1.0 - ADAM_B1) * g
    v = ADAM_B2 * v + (1.0 - ADAM_B2) * (g * g)
    m_hat = m / (1.0 - ADAM_B1 ** ADAM_STEP)
    v_hat = v / (1.0 - ADAM_B2 ** ADAM_STEP)
    return -ADAM_LR * (m_hat / (jnp.sqrt(v_hat) + ADAM_EPS) + ADAM_WD * w), m, v


def _adam_halves(mine, other, w, m, v, core, name):
    r, c = w.shape
    tr = _row_tile(r // 2, c)
    nbk = (r // 2) // tr

    def body(core_ref, a_ref, b_ref, w_ref, m_ref, v_ref, g_ref, d_ref, nm_ref, nv_ref):
        g = jnp.where(pl.program_id(0) // nbk == core_ref[0], a_ref[...], b_ref[...])
        g_ref[...] = g
        d_ref[...], nm_ref[...], nv_ref[...] = _adam_math(w_ref[...], g, m_ref[...], v_ref[...])

    spec = BS((tr, c), lambda i, cr: (i, 0))
    half = BS((tr, c), lambda i, cr: (i % nbk, 0))
    return pl.pallas_call(
        body, name=name, out_shape=[S((r, c), F32)] * 4,
        grid_spec=pltpu.PrefetchScalarGridSpec(num_scalar_prefetch=1, grid=(r // tr,), in_specs=[half, half] + [spec] * 3,
                                               out_specs=[spec] * 4),
        compiler_params=_params(),
    )(core, mine, other, w, m, v)


def _adam_flat(g, w, m, v, name):
    r = w.shape[0]
    tr = _row_tile(r)

    def body(g_ref, w_ref, m_ref, v_ref, d_ref, nm_ref, nv_ref):
        d_ref[...], nm_ref[...], nv_ref[...] = _adam_math(w_ref[...], g_ref[...], m_ref[...], v_ref[...])

    spec = BS((tr, 128), lambda i: (i, 0))
    return pl.pallas_call(
        body, grid=(r // tr,), name=name, in_specs=[spec] * 4, out_specs=[spec] * 3,
        out_shape=[S((r, 128), F32)] * 3, compiler_params=_params(),
    )(g, w, m, v)


def _flatten(arrs, rows_mult=256):
    flat = jnp.concatenate([a.reshape(-1) for a in arrs])
    n = flat.shape[0]
    rows = -(-n // 128)
    rows = -(-rows // rows_mult) * rows_mult
    return jnp.pad(flat, (0, rows * 128 - n)).reshape(rows, 128)


def _unflatten(flat, shapes):
    flat = flat.reshape(-1)
    out, o = [], 0
    for shp in shapes:
        n = int(np.prod(shp))
        out.append(flat[o:o + n].reshape(shp))
        o += n
    return out


BIG = ["w_mod", "w_in", "w_br_ssd", "w_br_lru", "w_out", "w_mlp1", "w_mlp2"]
COL_SHARDED = {"w_mod": N_MOD * D, "w_in": IN_COLS, "w_mlp1": MLP_H}
SMALL_SHARDED = ["ssd_conv_w", "lru_conv_w", "lru_ba", "lru_bi", "lru_lambda"]
WEIGHTS = ['c_ctx', 'w_mod', 'b_mod', 'w_in', 'b_gate', 'ssd_conv_w', 'ssd_conv_b', 'ssd_dt_bias', 'ssd_a_log', 'ssd_d',
           'ssd_norm_w', 'lru_conv_w', 'lru_conv_b', 'lru_wa', 'lru_ba', 'lru_wi', 'lru_bi', 'lru_lambda', 'w_br_ssd',
           'w_br_lru', 'w_out', 'ln1_g', 'ln1_b', 'w_mlp1', 'b_mlp1', 'w_mlp2', 'b_mlp2', 'ln2_g', 'ln2_b']
SMALL = [n for n in WEIGHTS if n not in BIG]
GATE_STACKS = ["lru_wa", "lru_wi"]


class _Reducer:
    def __init__(self, core_id, chip_id):
        self.core_id, self.chip_id = core_id, chip_id
        self.halves, self.recv, self.pending = {}, {}, []

    def _slabs(self, big, names):
        return [_chips_from_full(_unpack_win(big[n]) if n == "w_in" else big[n], n) for n in names]

    def _chip_sums(self, names, slabs, sibling):
        for n, s, o in zip(names, slabs, sibling):
            self.halves[n] = _sum_half(s, o, self.core_id, "half_" + n)
        return [self.halves[n][1] for n in names]

    def begin_swap(self, big, names):
        self.pending, self.slabs = list(names), self._slabs(big, names)
        return _Side("swap", self.slabs)

    def begin_scatter(self, sibling):
        return _Side("scatter", self._chip_sums(self.pending, self.slabs, sibling))

    def end(self, received):
        self.recv.update(zip(self.pending, received))

    def finish(self, big, names):
        slabs = self._slabs(big, names)
        sums = self._chip_sums(names, slabs, _swap_halves(slabs, "swap_halves_" + names[0]))
        self.recv.update(zip(names, _scatter_chips(sums)))
        mine = [_sum4(self.halves[n][0], self.recv[n], self.chip_id, "sum_" + n) for n in BIG]
        return dict(zip(BIG, mine)), dict(zip(BIG, _swap_cores(mine)))


def _full_from_chips(g4, name):
    if name in COL_SHARDED:
        return jnp.transpose(g4, (1, 0, 2)).reshape(g4.shape[1], 4 * g4.shape[2])
    return g4.reshape(4 * g4.shape[1], g4.shape[2])


def _chips_from_full(full, name):
    if name in COL_SHARDED:
        r, c = full.shape
        return jnp.transpose(full.reshape(r, 4, c // 4), (1, 0, 2))
    return full.reshape(4, full.shape[0] // 4, full.shape[1])


def kernel(x, c, ctx, c_ctx, w_mod, b_mod, w_in, b_gate, ssd_conv_w, ssd_conv_b, ssd_dt_bias, ssd_a_log, ssd_d, ssd_norm_w, lru_conv_w, lru_conv_b, lru_wa, lru_ba, lru_wi, lru_bi, lru_lambda, w_br_ssd, w_br_lru, w_out, ln1_g, ln1_b, w_mlp1, b_mlp1, w_mlp2, b_mlp2, ln2_g, ln2_b, loss_target, m_c_ctx, m_w_mod, m_b_mod, m_w_in, m_b_gate, m_ssd_conv_w, m_ssd_conv_b, m_ssd_dt_bias, m_ssd_a_log, m_ssd_d, m_ssd_norm_w, m_lru_conv_w, m_lru_conv_b, m_lru_wa, m_lru_ba, m_lru_wi, m_lru_bi, m_lru_lambda, m_w_br_ssd, m_w_br_lru, m_w_out, m_ln1_g, m_ln1_b, m_w_mlp1, m_b_mlp1, m_w_mlp2, m_b_mlp2, m_ln2_g, m_ln2_b, v_c_ctx, v_w_mod, v_b_mod, v_w_in, v_b_gate, v_ssd_conv_w, v_ssd_conv_b, v_ssd_dt_bias, v_ssd_a_log, v_ssd_d, v_ssd_norm_w, v_lru_conv_w, v_lru_conv_b, v_lru_wa, v_lru_ba, v_lru_wi, v_lru_bi, v_lru_lambda, v_w_br_ssd, v_w_br_lru, v_w_out, v_ln1_g, v_ln1_b, v_w_mlp1, v_b_mlp1, v_w_mlp2, v_b_mlp2, v_ln2_g, v_ln2_b):
    given = dict(locals())
    w = {n: given[n] for n in WEIGHTS}
    mom = {n: given["m_" + n] for n in WEIGHTS}
    var = {n: given["v_" + n] for n in WEIGHTS}
    chip = 2 * lax.axis_index("x") + lax.axis_index("y")

    shard2d = {n: w[n].reshape(w[n].shape[-2:]) for n in BIG}
    small_pack = _flatten([w[n] for n in SMALL_SHARDED], rows_mult=16)
    first = ["w_mod", "w_in"]
    gathered = _gather_chips([shard2d[n].astype(BF16) for n in first] + [small_pack])
    full = {n: _full_from_chips(g, n) for n, g in zip(first, gathered[:-1])}
    full["w_in"] = _pack_win(full["w_in"])
    per_chip = [_unflatten(gathered[-1][q], [w[n].shape for n in SMALL_SHARDED]) for q in range(4)]
    sm = {n: jnp.concatenate([per_chip[q][i] for q in range(4)], axis=-1) for i, n in enumerate(SMALL_SHARDED)}
    for n in SMALL:
        if n not in sm:
            sm[n] = w[n]
    sm = {n: (a.reshape(a.shape[1:]) if a.ndim >= 3 else a) for n, a in sm.items()}

    core_id = lax.axis_index("c").astype(jnp.int32).reshape(1)
    reducer = _Reducer(core_id, chip.astype(jnp.int32).reshape(1))
    loss, gx, gbig, gsmall = _local_step(x, c, ctx, loss_target, sm, full["w_mod"], full["w_in"],
                                         [shard2d[n].astype(BF16) for n in LATE], late_are_shards=True, reducer=reducer)
    mine, other = reducer.finish(gbig, ["w_mod"])
    out = {}
    for n in BIG:
        shp = shard2d[n].shape
        res = _adam_halves(mine[n], other[n], shard2d[n], mom[n].reshape(shp), var[n].reshape(shp), core_id, "adam_" + n)
        out[n] = [r.reshape(w[n].shape) for r in res]

    tiny = [n for n in SMALL if n not in GATE_STACKS]
    as_rows = lambda a: a.reshape(-1, 128)
    summed = _allreduce_small([_flatten([gsmall[n] for n in tiny])] + [as_rows(gsmall[n]) for n in GATE_STACKS],
                              [False] + [True] * len(GATE_STACKS))
    gs = {}
    for n, g in zip(tiny, _unflatten(summed[0], [gsmall[n].shape for n in tiny])):
        if n in SMALL_SHARDED:
            width = w[n].shape[-1]
            g = lax.dynamic_slice_in_dim(g, chip * width, width, axis=g.ndim - 1)
        gs[n] = g.reshape(w[n].shape)
    shapes = [w[n].shape for n in tiny]
    d_s, m_s, v_s = _adam_flat(_flatten([gs[n] for n in tiny]), _flatten([w[n] for n in tiny]),
                               _flatten([mom[n] for n in tiny]), _flatten([var[n] for n in tiny]), "adam_small")
    for n, d_, m_, v_ in zip(tiny, _unflatten(d_s, shapes), _unflatten(m_s, shapes), _unflatten(v_s, shapes)):
        out[n] = [gs[n], d_, m_, v_]
    for n, g in zip(GATE_STACKS, summed[1:]):
        res = _adam_flat(g, as_rows(w[n]), as_rows(mom[n]), as_rows(var[n]), "adam_" + n)
        out[n] = [r.reshape(w[n].shape) for r in (g, *res)]

    loss = lax.psum(loss, ("x", "y", "c"))
    return (loss, gx, *[out[n][0] for n in WEIGHTS], *[out[n][1] for n in WEIGHTS], *[out[n][2] for n in WEIGHTS],
            *[out[n][3] for n in WEIGHTS])
```
